```python
import math
import jax, jax.numpy as jnp
from jax import lax
import numpy as np

D_MODEL = 1024
BATCH = 8
SEQ = 2048
DEPTH = 1

N_META = 16
HEAD_DIM = 64
N_Q_HEADS = 8
N_KV_HEADS = 2
GQA_GROUP = N_Q_HEADS // N_KV_HEADS
ATTN_W = N_Q_HEADS * HEAD_DIM
KV_W = N_KV_HEADS * HEAD_DIM
WINDOW = 128
BLOCK = 128
CONV_CH = D_MODEL - ATTN_W
CONV_K = 31
IN_W = ATTN_W + 2 * KV_W + 2 * CONV_CH
MIX_W = ATTN_W + CONV_CH
NUM_BUCKETS = 32
MAX_EXACT = NUM_BUCKETS // 2
REL_MAX_DIST = 128
N_EXPERTS = 256
TOP_K = 8
N_GROUPS = 8
TOPK_GROUPS = 4
EXPERT_FF = 256
SHARED_FF = 256
ROUTED_SCALE = 2.5
ALPHA = (2.0 * DEPTH) ** 0.25
BETA = (8.0 * DEPTH) ** -0.25
LN_EPS = 1e-5
NEG = -1e30

kernel_name = "hymba_conformer_swa_sink_t5bias_dsv3moe_deepnorm"


def layer_norm(x, g, b):
    xf = x.astype(jnp.float32)
    mu = jnp.mean(xf, axis=-1, keepdims=True)
    var = jnp.mean(jnp.square(xf - mu), axis=-1, keepdims=True)
    y = (xf - mu) * lax.rsqrt(var + LN_EPS)
    return (y * g.astype(jnp.float32) + b.astype(jnp.float32)).astype(x.dtype)


def t5_causal_bucket(dist):
    is_small = dist < MAX_EXACT
    nf = jnp.maximum(dist, 1).astype(jnp.float32)
    large = MAX_EXACT + (jnp.log(nf / MAX_EXACT) / math.log(REL_MAX_DIST / MAX_EXACT)
                         * (NUM_BUCKETS - MAX_EXACT)).astype(jnp.int32)
    large = jnp.minimum(large, NUM_BUCKETS - 1)
    return jnp.where(is_small, dist, large)


def window_attention(q, k, v, sinks, rel_bias, pad_front):
    B, Lp = q.shape[0], q.shape[1]
    nb = Lp // BLOCK
    qb = q.reshape(B, nb, BLOCK, N_KV_HEADS, GQA_GROUP, HEAD_DIM)
    kb = k.reshape(B, nb, BLOCK, N_KV_HEADS, HEAD_DIM)
    vb = v.reshape(B, nb, BLOCK, N_KV_HEADS, HEAD_DIM)
    shift = ((0, 0), (1, 0), (0, 0), (0, 0), (0, 0))
    kw = jnp.concatenate([jnp.pad(kb, shift)[:, :-1], kb], axis=2)
    vw = jnp.concatenate([jnp.pad(vb, shift)[:, :-1], vb], axis=2)

    logits = jnp.einsum('bnqgrd,bnkgd->bngrqk', qb, kw).astype(jnp.float32)
    logits = logits * (HEAD_DIM ** -0.5)

    qi = jnp.arange(BLOCK, dtype=jnp.int32)[:, None]
    kj = jnp.arange(2 * BLOCK, dtype=jnp.int32)[None, :]
    dist = BLOCK + qi - kj
    bucket = t5_causal_bucket(jnp.clip(dist, 0, WINDOW - 1))
    bias = rel_bias.astype(jnp.float32)[bucket]
    bias = jnp.transpose(bias, (2, 0, 1)).reshape(N_KV_HEADS, GQA_GROUP, BLOCK, 2 * BLOCK)
    kpos = (jnp.arange(nb, dtype=jnp.int32)[:, None, None] - 1) * BLOCK + kj[None]
    valid = (dist >= 0)[None] & (dist < WINDOW)[None] & (kpos >= pad_front)

    logits = jnp.where(valid[None, :, None, None], logits + bias, NEG)
    sink = jnp.broadcast_to(sinks.astype(jnp.float32).reshape(1, 1, N_KV_HEADS, GQA_GROUP, 1, 1),
                            logits.shape[:-1] + (1,))
    probs = jax.nn.softmax(jnp.concatenate([logits, sink], axis=-1), axis=-1)[..., :-1]
    out = jnp.einsum('bngrqk,bnkgd->bnqgrd', probs.astype(v.dtype), vw)
    return out.reshape(B, Lp, ATTN_W)


def hybrid_mixer(h, w_in, conv_w, conv_b, conv_ln_g, conv_ln_b, sinks, w_out, rel_bias):
    B, L, _ = h.shape
    proj = h @ w_in
    q = proj[..., :ATTN_W]
    k = proj[..., ATTN_W:ATTN_W + KV_W]
    v = proj[..., ATTN_W + KV_W:ATTN_W + 2 * KV_W]
    c = proj[..., ATTN_W + 2 * KV_W:]

    pad_front = (-N_META) % BLOCK
    pad_back = (-(L + pad_front)) % BLOCK
    padw = ((0, 0), (pad_front, pad_back), (0, 0))
    qp = jnp.pad(q, padw).reshape(B, -1, N_Q_HEADS, HEAD_DIM)
    kp = jnp.pad(k, padw).reshape(B, -1, N_KV_HEADS, HEAD_DIM)
    vp = jnp.pad(v, padw).reshape(B, -1, N_KV_HEADS, HEAD_DIM)
    attn = window_attention(qp, kp, vp, sinks, rel_bias, pad_front)[:, pad_front:pad_front + L]

    a, gate = c[..., :CONV_CH], c[..., CONV_CH:]
    u = a * jax.nn.sigmoid(gate)
    u = lax.conv_general_dilated(u, conv_w[:, None, :].astype(u.dtype), window_strides=(1,),
                                 padding=[(CONV_K - 1, 0)],
                                 dimension_numbers=('NWC', 'WIO', 'NWC'),
                                 feature_group_count=CONV_CH) + conv_b
    u = jax.nn.silu(layer_norm(u, conv_ln_g, conv_ln_b))

    return jnp.concatenate([attn, u], axis=-1) @ w_out


def moe_ffn(h2, w_router, router_bias, w_gate, w_up, w_down, ws_gate, ws_up, ws_down):
    T = h2.shape[0]
    scores = jax.nn.sigmoid((h2 @ w_router).astype(jnp.float32))
    choice = scores + router_bias.astype(jnp.float32)
    grp = choice.reshape(T, N_GROUPS, N_EXPERTS // N_GROUPS)
    grp_score = lax.top_k(grp, 2)[0].sum(-1)
    _, gidx = lax.top_k(grp_score, TOPK_GROUPS)
    gmask = jnp.any(gidx[..., None] == jnp.arange(N_GROUPS)[None, None], axis=1)
    emask = jnp.repeat(gmask, N_EXPERTS // N_GROUPS, axis=1)
    _, idx = lax.top_k(jnp.where(emask, choice, NEG), TOP_K)
    wts = jnp.take_along_axis(scores, idx, axis=1)
    wts = wts / jnp.sum(wts, axis=-1, keepdims=True) * ROUTED_SCALE

    flat_e = idx.reshape(-1)
    order = jnp.argsort(flat_e)
    tok = order // TOP_K
    xs = h2[tok]
    gs = jnp.bincount(flat_e, length=N_EXPERTS).astype(jnp.int32)
    hid = jax.nn.silu(lax.ragged_dot(xs, w_gate, gs)) * lax.ragged_dot(xs, w_up, gs)
    ys = lax.ragged_dot(hid, w_down, gs) * wts.reshape(-1)[order][:, None].astype(h2.dtype)
    routed = jax.ops.segment_sum(ys, tok, num_segments=T)
    shared = (jax.nn.silu(h2 @ ws_gate) * (h2 @ ws_up)) @ ws_down
    return routed + shared


def setup_inputs(seed: int = 0) -> dict:
    key = jax.random.key(seed)
    ks = jax.random.split(key, 24)
    f32 = jnp.float32
    D = D_MODEL

    def nrm(k, shape, scale):
        return jax.random.normal(k, shape, f32) * scale

    col_scale = jnp.concatenate([jnp.ones((ATTN_W + KV_W,), f32), jnp.full((KV_W,), BETA, f32),
                                 jnp.ones((2 * CONV_CH,), f32)])
    return {
        "x": nrm(ks[0], (BATCH, SEQ, D), 1.0),
        "meta_tokens": nrm(ks[1], (N_META, D), 1.0),
        "ln_in_g": 1.0 + nrm(ks[2], (D,), 0.02),
        "ln_in_b": nrm(ks[3], (D,), 0.02),
        "rel_bias": nrm(ks[4], (NUM_BUCKETS, N_Q_HEADS), 0.5),
        "w_in": nrm(ks[5], (DEPTH, D, IN_W), D ** -0.5) * col_scale,
        "conv_w": nrm(ks[6], (DEPTH, CONV_K, CONV_CH), CONV_K ** -0.5),
        "conv_b": nrm(ks[7], (DEPTH, CONV_CH), 0.02),
        "conv_ln_g": 1.0 + nrm(ks[8], (DEPTH, CONV_CH), 0.02),
        "conv_ln_b": nrm(ks[9], (DEPTH, CONV_CH), 0.02),
        "sinks": nrm(ks[10], (DEPTH, N_Q_HEADS), 1.0),
        "w_out": nrm(ks[11], (DEPTH, MIX_W, D), MIX_W ** -0.5 * BETA),
        "ln1_g": 1.0 + nrm(ks[12], (DEPTH, D), 0.02),
        "ln1_b": nrm(ks[13], (DEPTH, D), 0.02),
        "w_router": nrm(ks[14], (DEPTH, D, N_EXPERTS), D ** -0.5),
        "router_bias": nrm(ks[15], (DEPTH, N_EXPERTS), 0.01),
        "w_gate": nrm(ks[16], (DEPTH, N_EXPERTS, D, EXPERT_FF), D ** -0.5),
        "w_up": nrm(ks[17], (DEPTH, N_EXPERTS, D, EXPERT_FF), D ** -0.5),
        "w_down": nrm(ks[18], (DEPTH, N_EXPERTS, EXPERT_FF, D), EXPERT_FF ** -0.5 * BETA),
        "ws_gate": nrm(ks[19], (DEPTH, D, SHARED_FF), D ** -0.5),
        "ws_up": nrm(ks[20], (DEPTH, D, SHARED_FF), D ** -0.5),
        "ws_down": nrm(ks[21], (DEPTH, SHARED_FF, D), SHARED_FF ** -0.5 * BETA),
        "ln2_g": 1.0 + nrm(ks[22], (DEPTH, D), 0.02),
        "ln2_b": nrm(ks[23], (DEPTH, D), 0.02),
    }


def reference(x, meta_tokens, ln_in_g, ln_in_b, rel_bias, w_in, conv_w, conv_b, conv_ln_g,
              conv_ln_b, sinks, w_out, ln1_g, ln1_b, w_router, router_bias, w_gate, w_up,
              w_down, ws_gate, ws_up, ws_down, ln2_g, ln2_b):
    B = x.shape[0]
    meta = jnp.broadcast_to(meta_tokens[None].astype(x.dtype), (B, N_META, D_MODEL))
    h = jnp.concatenate([meta, x], axis=1)
    h = layer_norm(h, ln_in_g, ln_in_b)
    L = h.shape[1]
    for l in range(DEPTH):
        mix = hybrid_mixer(h, w_in[l], conv_w[l], conv_b[l], conv_ln_g[l], conv_ln_b[l],
                           sinks[l], w_out[l], rel_bias)
        h = layer_norm(ALPHA * h + mix, ln1_g[l], ln1_b[l])
        ff = moe_ffn(h.reshape(B * L, D_MODEL), w_router[l], router_bias[l], w_gate[l], w_up[l],
                     w_down[l], ws_gate[l], ws_up[l], ws_down[l]).reshape(B, L, D_MODEL)
        h = layer_norm(ALPHA * h + ff, ln2_g[l], ln2_b[l])
    return h[:, N_META:]
```

```python
import functools
import math

import numpy as np
import jax
import jax.numpy as jnp
from jax import lax
from jax.experimental import pallas as pl
from jax.experimental.pallas import tpu as pltpu

F32 = jnp.float32
BF16 = jnp.bfloat16
I32 = jnp.int32
SDS = jax.ShapeDtypeStruct

D_MODEL = 1024
N_META = 16
HEAD_DIM = 64
N_Q_HEADS = 8
N_KV_HEADS = 2
GQA_GROUP = N_Q_HEADS // N_KV_HEADS
ATTN_W = N_Q_HEADS * HEAD_DIM
KV_W = N_KV_HEADS * HEAD_DIM
WINDOW = 128
BLOCK = 128
CONV_CH = D_MODEL - ATTN_W
CONV_K = 31
IN_W = ATTN_W + 2 * KV_W + 2 * CONV_CH
NUM_BUCKETS = 32
MAX_EXACT = NUM_BUCKETS // 2
REL_MAX_DIST = 128
N_EXPERTS = 256
TOP_K = 8
N_GROUPS = 8
GROUP_SIZE = N_EXPERTS // N_GROUPS
TOPK_GROUPS = 4
EXPERT_FF = 256
SHARED_FF = 256
ROUTED_SCALE = 2.5
DEPTH = 1
ALPHA = (2.0 * DEPTH) ** 0.25
LN_EPS = 1e-5
NEG = -1e30
PAD_FRONT = (-N_META) % BLOCK

VMEM_LIMIT = 48 * 1024 * 1024

TQ_PROJ = 512
T_CONV = 256
CONV_HALO = 32
R_CONV = 32
TQ_MIX = 256
TN_ROUTE = 256
TN_DISP = 256
TM_EXP = 256
TN_COMB = 128


def _cparams(*sem):
    return pltpu.CompilerParams(dimension_semantics=sem, vmem_limit_bytes=VMEM_LIMIT)


def _layer_norm(x, g, b):
    mu = jnp.mean(x, axis=-1, keepdims=True)
    xc = x - mu
    var = jnp.mean(xc * xc, axis=-1, keepdims=True)
    return xc * lax.rsqrt(var + LN_EPS) * g + b


def _sigmoid(x):
    return 1.0 / (1.0 + jnp.exp(-x))


def _proj_kernel(x_ref, g_ref, b_ref, w_ref, q_ref, k_ref, v_ref, u_ref):
    h = _layer_norm(x_ref[...], g_ref[...], b_ref[...])
    p = jnp.dot(h.astype(BF16), w_ref[...], preferred_element_type=F32)
    q_ref[...] = (p[:, :ATTN_W] * (HEAD_DIM ** -0.5)).astype(BF16)
    k_ref[...] = p[:, ATTN_W:ATTN_W + KV_W].astype(BF16)
    v_ref[...] = p[:, ATTN_W + KV_W:ATTN_W + 2 * KV_W].astype(BF16)
    a = p[:, ATTN_W + 2 * KV_W:ATTN_W + 2 * KV_W + CONV_CH]
    gate = p[:, ATTN_W + 2 * KV_W + CONV_CH:]
    u_ref[...] = a * _sigmoid(gate)


def _proj_call(x2d, gin, bin_, w_in_b, tq):
    t = x2d.shape[0]
    row = lambda i: (i, 0)
    fix = lambda i: (0, 0)
    return pl.pallas_call(
        _proj_kernel,
        grid=(t // tq,),
        in_specs=[pl.BlockSpec((tq, D_MODEL), row), pl.BlockSpec((1, D_MODEL), fix),
                  pl.BlockSpec((1, D_MODEL), fix), pl.BlockSpec((D_MODEL, IN_W), fix)],
        out_specs=[pl.BlockSpec((tq, ATTN_W), row), pl.BlockSpec((tq, KV_W), row),
                   pl.BlockSpec((tq, KV_W), row), pl.BlockSpec((tq, CONV_CH), row)],
        out_shape=[SDS((t, ATTN_W), BF16), SDS((t, KV_W), BF16), SDS((t, KV_W), BF16), SDS((t, CONV_CH), F32)],
        compiler_params=_cparams("parallel"),
        name="ln_in_proj",
    )(x2d, gin, bin_, w_in_b)


def _attn_kernel(sinks_ref, q_ref, kc_ref, kp_ref, vc_ref, vp_ref, km_ref, vm_ref, bias_ref, o_ref):
    first = pl.program_id(1) == 0
    kp = jnp.where(first, km_ref[...], kp_ref[...])
    vp = jnp.where(first, vm_ref[...], vp_ref[...])
    k = jnp.concatenate([kp, kc_ref[...]], axis=0)
    v = jnp.concatenate([vp, vc_ref[...]], axis=0)
    col = lax.broadcasted_iota(I32, (BLOCK, 2 * BLOCK), 1)
    pad_bias = jnp.where(jnp.logical_and(first, col < PAD_FRONT), NEG, 0.0).astype(F32)
    q = q_ref[...]
    outs = []
    for h in range(N_Q_HEADS):
        g = h // GQA_GROUP
        qh = q[:, h * HEAD_DIM:(h + 1) * HEAD_DIM]
        kg = k[:, g * HEAD_DIM:(g + 1) * HEAD_DIM]
        vg = v[:, g * HEAD_DIM:(g + 1) * HEAD_DIM]
        s = lax.dot_general(qh, kg, (((1,), (1,)), ((), ())), preferred_element_type=F32)
        s = s + bias_ref[h] + pad_bias
        sink = sinks_ref[h]
        m = jnp.maximum(jnp.max(s, axis=-1, keepdims=True), sink)
        p = jnp.exp(s - m)
        den = jnp.sum(p, axis=-1, keepdims=True) + jnp.exp(sink - m)
        o = jnp.dot(p.astype(BF16), vg, preferred_element_type=F32)
        outs.append(o / den)
    o_ref[...] = jnp.concatenate(outs, axis=1).astype(BF16)


def _attn_call(q, k, v, k_meta, v_meta, bias, sinks, nbatch, nblk):
    t = q.shape[0]
    cur = lambda b, j: (b * nblk + j, 0)
    prev = lambda b, j: (jnp.maximum(b * nblk + j - 1, 0), 0)
    fix2 = lambda b, j: (0, 0)
    return pl.pallas_call(
        _attn_kernel,
        grid=(nbatch, nblk),
        in_specs=[pl.BlockSpec(memory_space=pltpu.SMEM),
                  pl.BlockSpec((BLOCK, ATTN_W), cur),
                  pl.BlockSpec((BLOCK, KV_W), cur), pl.BlockSpec((BLOCK, KV_W), prev),
                  pl.BlockSpec((BLOCK, KV_W), cur), pl.BlockSpec((BLOCK, KV_W), prev),
                  pl.BlockSpec((BLOCK, KV_W), fix2), pl.BlockSpec((BLOCK, KV_W), fix2),
                  pl.BlockSpec((N_Q_HEADS, BLOCK, 2 * BLOCK), lambda b, j: (0, 0, 0))],
        out_specs=pl.BlockSpec((BLOCK, ATTN_W), cur),
        out_shape=SDS((t, ATTN_W), BF16),
        compiler_params=_cparams("parallel", "parallel"),
        name="swa_attn",
    )(sinks, q, k, k, v, v, k_meta, v_meta, bias)


def _rel_bias_table(rel_bias):
    qi = np.arange(BLOCK, dtype=np.int32)[:, None]
    kj = np.arange(2 * BLOCK, dtype=np.int32)[None, :]
    dist = BLOCK + qi - kj
    dc = np.clip(dist, 0, WINDOW - 1)
    nf = np.maximum(dc, 1).astype(np.float32)
    large = MAX_EXACT + (np.log(nf / np.float32(MAX_EXACT)) / np.float32(math.log(REL_MAX_DIST / MAX_EXACT))
                         * np.float32(NUM_BUCKETS - MAX_EXACT)).astype(np.int32)
    large = np.minimum(large, NUM_BUCKETS - 1)
    bucket = np.where(dc < MAX_EXACT, dc, large)
    in_window = (dist >= 0) & (dist < WINDOW)
    bias = jnp.transpose(rel_bias.astype(F32)[bucket], (2, 0, 1))
    return jnp.where(in_window[None], bias, NEG)


def _conv_kernel(uc_ref, up_ref, um_ref, w_ref, cb_ref, g_ref, b_ref, o_ref, s_ref):
    first = pl.program_id(1) == 0
    s_ref[0:CONV_HALO, :] = jnp.where(first, um_ref[...], up_ref[...])
    s_ref[CONV_HALO:CONV_HALO + T_CONV, :] = uc_ref[...]
    off = CONV_HALO - (CONV_K - 1)
    for c in range(0, T_CONV, R_CONV):
        acc = jnp.zeros((R_CONV, CONV_CH), F32) + cb_ref[...]
        for kk in range(CONV_K):
            acc = acc + s_ref[c + off + kk:c + off + kk + R_CONV, :] * w_ref[kk:kk + 1, :]
        y = _layer_norm(acc, g_ref[...], b_ref[...])
        o_ref[c:c + R_CONV, :] = (y * _sigmoid(y)).astype(BF16)


def _conv_call(u, u_meta_halo, conv_w, conv_b, g, b, nbatch, seq):
    t = u.shape[0]
    nj = seq // T_CONV
    per = T_CONV // CONV_HALO
    cur = lambda bb, j: (bb * nj + j, 0)
    prev = lambda bb, j: (jnp.maximum((bb * nj + j) * per - 1, 0), 0)
    fix = lambda bb, j: (0, 0)
    return pl.pallas_call(
        _conv_kernel,
        grid=(nbatch, nj),
        in_specs=[pl.BlockSpec((T_CONV, CONV_CH), cur), pl.BlockSpec((CONV_HALO, CONV_CH), prev),
                  pl.BlockSpec((CONV_HALO, CONV_CH), fix), pl.BlockSpec((CONV_K, CONV_CH), fix),
                  pl.BlockSpec((1, CONV_CH), fix), pl.BlockSpec((1, CONV_CH), fix), pl.BlockSpec((1, CONV_CH), fix)],
        out_specs=pl.BlockSpec((T_CONV, CONV_CH), cur),
        out_shape=SDS((t, CONV_CH), BF16),
        scratch_shapes=[pltpu.VMEM((CONV_HALO + T_CONV, CONV_CH), F32)],
        compiler_params=_cparams("parallel", "parallel"),
        name="conv_ln",
    )(u, u, u_meta_halo, conv_w, conv_b, g, b)


def _mix_kernel(x_ref, at_ref, cv_ref, gin_ref, bin_ref, woa_ref, woc_ref, g1_ref, b1_ref,
                wrh_ref, wrl_ref, h1_ref, h1r_ref, lg_ref):
    h = _layer_norm(x_ref[...], gin_ref[...], bin_ref[...])
    mix = (jnp.dot(at_ref[...], woa_ref[...], preferred_element_type=F32)
           + jnp.dot(cv_ref[...], woc_ref[...], preferred_element_type=F32))
    h1 = _layer_norm(ALPHA * h + mix, g1_ref[...], b1_ref[...])
    h1_ref[...] = h1
    h1r_ref[...] = h1.reshape(h1r_ref.shape)
    hh = h1.astype(BF16)
    hl = (h1 - hh.astype(F32)).astype(BF16)
    nt = (((1,), (1,)), ((), ()))
    lg = lax.dot_general(wrh_ref[...], hh, nt, preferred_element_type=F32)
    lg = lg + lax.dot_general(wrh_ref[...], hl, nt, preferred_element_type=F32)
    lg = lg + lax.dot_general(wrl_ref[...], hh, nt, preferred_element_type=F32)
    lg_ref[...] = lg


def _mix_call(x2d, attn, conv, gin, bin_, woa, woc, g1, b1, wrh, wrl):
    t = x2d.shape[0]
    tq = TQ_MIX
    row = lambda i: (i, 0)
    fix = lambda i: (0, 0)
    return pl.pallas_call(
        _mix_kernel,
        grid=(t // tq,),
        in_specs=[pl.BlockSpec((tq, D_MODEL), row), pl.BlockSpec((tq, ATTN_W), row), pl.BlockSpec((tq, CONV_CH), row),
                  pl.BlockSpec((1, D_MODEL), fix), pl.BlockSpec((1, D_MODEL), fix),
                  pl.BlockSpec((ATTN_W, D_MODEL), fix), pl.BlockSpec((CONV_CH, D_MODEL), fix),
                  pl.BlockSpec((1, D_MODEL), fix), pl.BlockSpec((1, D_MODEL), fix),
                  pl.BlockSpec((N_EXPERTS, D_MODEL), fix), pl.BlockSpec((N_EXPERTS, D_MODEL), fix)],
        out_specs=[pl.BlockSpec((tq, D_MODEL), row), pl.BlockSpec((tq, 1, D_MODEL), lambda i: (i, 0, 0)),
                   pl.BlockSpec((N_EXPERTS, tq), lambda i: (0, i))],
        out_shape=[SDS((t, D_MODEL), F32), SDS((t, 1, D_MODEL), F32), SDS((N_EXPERTS, t), F32)],
        compiler_params=_cparams("parallel"),
        name="mix_ln1",
    )(x2d, attn, conv, gin, bin_, woa, woc, g1, b1, wrh, wrl)


def _first_argmax(x, rows, nrows):
    m = jnp.max(x, axis=0, keepdims=True)
    idx = jnp.min(jnp.where(x == m, rows, nrows), axis=0, keepdims=True)
    return m, idx


def _route_kernel(lg_ref, rb_ref, idx_ref, wts_ref, rank_ref, cnt_ref, carry_ref):
    tn = lg_ref.shape[1]

    @pl.when(pl.program_id(0) == 0)
    def _():
        carry_ref[...] = jnp.zeros_like(carry_ref)

    scores = _sigmoid(lg_ref[...])
    choice = scores + rb_ref[...]
    rows = lax.broadcasted_iota(I32, (N_EXPERTS, tn), 0)
    rows_g = lax.broadcasted_iota(I32, (GROUP_SIZE, tn), 0)
    rows_8 = lax.broadcasted_iota(I32, (N_GROUPS, tn), 0)

    gs = []
    for g in range(N_GROUPS):
        xg = choice[g * GROUP_SIZE:(g + 1) * GROUP_SIZE, :]
        m1, i1 = _first_argmax(xg, rows_g, GROUP_SIZE)
        m2 = jnp.max(jnp.where(rows_g == i1, -jnp.inf, xg), axis=0, keepdims=True)
        gs.append(m1 + m2)
    gsc = jnp.concatenate(gs, axis=0)
    gsel = jnp.zeros((N_GROUPS, tn), F32)
    for _ in range(TOPK_GROUPS):
        _, gi = _first_argmax(gsc, rows_8, N_GROUPS)
        hit = rows_8 == gi
        gsel = jnp.where(hit, 1.0, gsel)
        gsc = jnp.where(hit, -jnp.inf, gsc)
    emask = jnp.concatenate(
        [jnp.broadcast_to(gsel[g:g + 1, :], (GROUP_SIZE, tn)) for g in range(N_GROUPS)], axis=0)
    masked = jnp.where(emask > 0.5, choice, NEG)

    sel_all = jnp.zeros((N_EXPERTS, tn), F32)
    hits, idxs, ws = [], [], []
    for _ in range(TOP_K):
        _, ii = _first_argmax(masked, rows, N_EXPERTS)
        hit = rows == ii
        hits.append(hit)
        idxs.append(ii)
        ws.append(jnp.sum(jnp.where(hit, scores, 0.0), axis=0, keepdims=True))
        sel_all = jnp.where(hit, 1.0, sel_all)
        masked = jnp.where(hit, -jnp.inf, masked)
    wsum = ws[0]
    for w in ws[1:]:
        wsum = wsum + w
    idx_ref[...] = jnp.concatenate(idxs, axis=0)
    wts_ref[...] = jnp.concatenate([w / wsum * ROUTED_SCALE for w in ws], axis=0)

    r_i = lax.broadcasted_iota(I32, (tn, tn), 0)
    c_i = lax.broadcasted_iota(I32, (tn, tn), 1)
    upper = jnp.where(r_i < c_i, 1.0, 0.0).astype(BF16)
    sel_b = sel_all.astype(BF16)
    carry = carry_ref[...]
    before = jnp.dot(sel_b, upper, preferred_element_type=F32)
    before = before + jnp.concatenate([carry] * (tn // 128), axis=1)
    rank_ref[...] = jnp.concatenate(
        [jnp.sum(jnp.where(h, before, 0.0), axis=0, keepdims=True) for h in hits], axis=0).astype(I32)
    carry = carry + jnp.dot(sel_b, jnp.ones((tn, 128), BF16), preferred_element_type=F32)
    carry_ref[...] = carry
    cnt_ref[...] = carry.astype(I32)


def _route_call(lg, rbias):
    t = lg.shape[1]
    tn = TN_ROUTE
    col = lambda i: (0, i)
    return pl.pallas_call(
        _route_kernel,
        grid=(t // tn,),
        in_specs=[pl.BlockSpec((N_EXPERTS, tn), col), pl.BlockSpec((N_EXPERTS, 1), lambda i: (0, 0))],
        out_specs=[pl.BlockSpec((TOP_K, tn), col), pl.BlockSpec((TOP_K, tn), col), pl.BlockSpec((TOP_K, tn), col),
                   pl.BlockSpec((N_EXPERTS, 128), lambda i: (0, 0))],
        out_shape=[SDS((TOP_K, t), I32), SDS((TOP_K, t), F32), SDS((TOP_K, t), I32), SDS((N_EXPERTS, 128), I32)],
        scratch_shapes=[pltpu.VMEM((N_EXPERTS, 128), F32)],
        compiler_params=_cparams("arbitrary"),
        name="route",
    )(lg, rbias)


def _dispatch_kernel(idx_ref, rank_ref, offs_ref, h_ref, xs_ref, dest_ref, sem):
    tn = idx_ref.shape[1]
    base = pl.program_id(0) * tn

    def row_copy(src_row, dst_row):
        return pltpu.make_async_copy(h_ref.at[src_row], xs_ref.at[dst_row], sem)

    def issue(t, c):
        for kk in range(TOP_K):
            d = offs_ref[idx_ref[kk, t]] + rank_ref[kk, t]
            dest_ref[kk, t] = d
            row_copy(base + t, d).start()
        return c

    lax.fori_loop(0, tn, issue, 0)

    def drain(t, c):
        for kk in range(TOP_K):
            row_copy(0, 0).wait()
        return c

    lax.fori_loop(0, tn, drain, 0)


def _dispatch_call(idx, rank, offs, h1rows, n_rows):
    t = idx.shape[1]
    tn = TN_DISP
    col = lambda i: (0, i)
    return pl.pallas_call(
        _dispatch_kernel,
        grid=(t // tn,),
        in_specs=[pl.BlockSpec((TOP_K, tn), col, memory_space=pltpu.SMEM),
                  pl.BlockSpec((TOP_K, tn), col, memory_space=pltpu.SMEM),
                  pl.BlockSpec(memory_space=pltpu.SMEM),
                  pl.BlockSpec(memory_space=pl.ANY)],
        out_specs=[pl.BlockSpec(memory_space=pl.ANY),
                   pl.BlockSpec((TOP_K, tn), col, memory_space=pltpu.SMEM)],
        out_shape=[SDS((n_rows, 1, D_MODEL), F32), SDS((TOP_K, t), I32)],
        scratch_shapes=[pltpu.SemaphoreType.DMA(())],
        compiler_params=_cparams("arbitrary"),
        name="dispatch",
    )(idx, rank, offs, h1rows)


def _expert_kernel(te_ref, tr_ref, x_ref, wg_ref, wu_ref, wd_ref, y_ref):
    tm = x_ref.shape[0]
    x = x_ref[...].reshape(tm, D_MODEL)
    g = jnp.dot(x, wg_ref[0], preferred_element_type=F32)
    u = jnp.dot(x, wu_ref[0], preferred_element_type=F32)
    live = lax.broadcasted_iota(I32, (tm, EXPERT_FF), 0) < tr_ref[pl.program_id(0)]
    hid = jnp.where(live, g * _sigmoid(g) * u, 0.0)
    y = jnp.dot(hid, wd_ref[0], preferred_element_type=F32)
    y_ref[...] = y.reshape(tm, 1, D_MODEL)


def _expert_call(tile_expert, tile_rows, xs, w_gate, w_up, w_down):
    n_rows = xs.shape[0]
    tm = TM_EXP
    return pl.pallas_call(
        _expert_kernel,
        grid_spec=pltpu.PrefetchScalarGridSpec(
            num_scalar_prefetch=2,
            grid=(n_rows // tm,),
            in_specs=[pl.BlockSpec((tm, 1, D_MODEL), lambda i, te, tr: (i, 0, 0)),
                      pl.BlockSpec((1, D_MODEL, EXPERT_FF), lambda i, te, tr: (te[i], 0, 0)),
                      pl.BlockSpec((1, D_MODEL, EXPERT_FF), lambda i, te, tr: (te[i], 0, 0)),
                      pl.BlockSpec((1, EXPERT_FF, D_MODEL), lambda i, te, tr: (te[i], 0, 0))],
            out_specs=pl.BlockSpec((tm, 1, D_MODEL), lambda i, te, tr: (i, 0, 0)),
        ),
        out_shape=SDS((n_rows, 1, D_MODEL), F32),
        compiler_params=_cparams("arbitrary"),
        name="experts",
    )(tile_expert, tile_rows, xs, w_gate, w_up, w_down)


def _combine_kernel(dest_ref, wts_ref, h1_ref, ys_ref, wsg_ref, wsu_ref, wsd_ref, g2_ref, b2_ref,
                    o_ref, ybuf_ref, sem):
    tn = h1_ref.shape[0]

    def row_copy(src_row, slot):
        return pltpu.make_async_copy(ys_ref.at[src_row], ybuf_ref.at[slot], sem)

    def issue(t, c):
        for kk in range(TOP_K):
            row_copy(dest_ref[kk, t], kk * tn + t).start()
        return c

    lax.fori_loop(0, tn, issue, 0)

    h1 = h1_ref[...]
    hb = h1.astype(BF16)
    sg = jnp.dot(hb, wsg_ref[...], preferred_element_type=F32)
    su = jnp.dot(hb, wsu_ref[...], preferred_element_type=F32)
    ff = jnp.dot((sg * _sigmoid(sg) * su).astype(BF16), wsd_ref[...], preferred_element_type=F32)

    def drain(t, c):
        for kk in range(TOP_K):
            row_copy(0, 0).wait()
        return c

    lax.fori_loop(0, tn, drain, 0)

    eye = lax.broadcasted_iota(I32, (tn, tn), 0) == lax.broadcasted_iota(I32, (tn, tn), 1)
    w = wts_ref[...]
    for kk in range(TOP_K):
        wk = jnp.where(eye, w[kk:kk + 1, :], 0.0)
        wh = wk.astype(BF16).astype(F32)
        yk = ybuf_ref[kk * tn:(kk + 1) * tn].reshape(tn, D_MODEL)
        ff = ff + jnp.dot(wh, yk, preferred_element_type=F32)
        ff = ff + jnp.dot(wk - wh, yk, preferred_element_type=F32)
    o_ref[...] = _layer_norm(ALPHA * h1 + ff, g2_ref[...], b2_ref[...])


def _combine_call(dest, wts, h1, ys, wsg, wsu, wsd, g2, b2):
    t = h1.shape[0]
    tn = TN_COMB
    col = lambda i: (0, i)
    row = lambda i: (i, 0)
    fix = lambda i: (0, 0)
    return pl.pallas_call(
        _combine_kernel,
        grid=(t // tn,),
        in_specs=[pl.BlockSpec((TOP_K, tn), col, memory_space=pltpu.SMEM),
                  pl.BlockSpec((TOP_K, tn), col),
                  pl.BlockSpec((tn, D_MODEL), row),
                  pl.BlockSpec(memory_space=pl.ANY),
                  pl.BlockSpec((D_MODEL, SHARED_FF), fix), pl.BlockSpec((D_MODEL, SHARED_FF), fix),
                  pl.BlockSpec((SHARED_FF, D_MODEL), fix),
                  pl.BlockSpec((1, D_MODEL), fix), pl.BlockSpec((1, D_MODEL), fix)],
        out_specs=pl.BlockSpec((tn, D_MODEL), row),
        out_shape=SDS((t, D_MODEL), F32),
        scratch_shapes=[pltpu.VMEM((TOP_K * tn, 1, D_MODEL), F32), pltpu.SemaphoreType.DMA(())],
        compiler_params=_cparams("arbitrary"),
        name="combine_ln2",
    )(dest, wts, h1, ys, wsg, wsu, wsd, g2, b2)


def kernel(x, meta_tokens, ln_in_g, ln_in_b, rel_bias, w_in, conv_w, conv_b, conv_ln_g, conv_ln_b, sinks,
           w_out, ln1_g, ln1_b, w_router, router_bias, w_gate, w_up, w_down, ws_gate, ws_up, ws_down,
           ln2_g, ln2_b):
    nbatch, seq, d = x.shape
    assert d == D_MODEL and seq % TQ_PROJ == 0 and w_in.shape[0] == DEPTH
    t = nbatch * seq
    x2d = x.reshape(t, D_MODEL)
    vec = lambda a: a.reshape(1, -1).astype(F32)
    gin, bin_ = vec(ln_in_g), vec(ln_in_b)
    w_in_b = w_in[0].astype(BF16)

    q, k, v, u = _proj_call(x2d, gin, bin_, w_in_b, TQ_PROJ)
    meta_blk = jnp.concatenate([jnp.zeros((PAD_FRONT, D_MODEL), F32), meta_tokens.astype(F32)], axis=0)
    _, k_meta, v_meta, u_meta = _proj_call(meta_blk, gin, bin_, w_in_b, BLOCK)

    attn = _attn_call(q, k, v, k_meta, v_meta, _rel_bias_table(rel_bias), sinks[0].astype(F32),
                      nbatch, seq // BLOCK)

    u_halo = jnp.concatenate([jnp.zeros((CONV_HALO - N_META, CONV_CH), F32), u_meta[PAD_FRONT:]], axis=0)
    conv = _conv_call(u, u_halo, conv_w[0].astype(F32), vec(conv_b[0]), vec(conv_ln_g[0]), vec(conv_ln_b[0]),
                      nbatch, seq)

    w_out_b = w_out[0].astype(BF16)
    wr_t = w_router[0].astype(F32).T
    wr_hi = wr_t.astype(BF16)
    wr_lo = (wr_t - wr_hi.astype(F32)).astype(BF16)
    h1, h1rows, logits = _mix_call(x2d, attn, conv, gin, bin_, w_out_b[:ATTN_W], w_out_b[ATTN_W:],
                                   vec(ln1_g[0]), vec(ln1_b[0]), wr_hi, wr_lo)

    idx, wts, rank, cnt = _route_call(logits, router_bias[0].astype(F32).reshape(N_EXPERTS, 1))

    tm = TM_EXP
    n_tiles = (t * TOP_K) // tm + N_EXPERTS
    counts = cnt[:, 0]
    tiles_e = (counts + tm - 1) // tm
    tile_end = jnp.cumsum(tiles_e)
    tile_start = tile_end - tiles_e
    offs = (tile_start * tm).astype(I32)
    tile_id = jnp.arange(n_tiles, dtype=I32)
    tile_expert = jnp.minimum(jnp.sum(tile_id[:, None] >= tile_end[None, :], axis=1), N_EXPERTS - 1).astype(I32)
    tile_rows = jnp.clip(counts[tile_expert] - (tile_id - tile_start[tile_expert]) * tm, 0, tm)
    tile_rows = jnp.where(tile_id < tile_end[-1], tile_rows, 0).astype(I32)

    xs, dest = _dispatch_call(idx, rank, offs, h1rows, n_tiles * tm)
    ys = _expert_call(tile_expert, tile_rows, xs, w_gate[0], w_up[0], w_down[0])
    out = _combine_call(dest, wts, h1, ys, ws_gate[0].astype(BF16), ws_up[0].astype(BF16),
                        ws_down[0].astype(BF16), vec(ln2_g[0]), vec(ln2_b[0]))
    return out.reshape(nbatch, seq, D_MODEL)
```

```python
import functools
import math

import numpy as np
import jax
import jax.numpy as jnp
from jax import lax
from jax.experimental import pallas as pl
from jax.experimental.pallas import tpu as pltpu

F32 = jnp.float32
BF16 = jnp.bfloat16
I32 = jnp.int32
SDS = jax.ShapeDtypeStruct

D_MODEL = 1024
N_META = 16
HEAD_DIM = 64
N_Q_HEADS = 8
N_KV_HEADS = 2
GQA_GROUP = N_Q_HEADS // N_KV_HEADS
ATTN_W = N_Q_HEADS * HEAD_DIM
KV_W = N_KV_HEADS * HEAD_DIM
WINDOW = 128
BLOCK = 128
CONV_CH = D_MODEL - ATTN_W
CONV_K = 31
IN_W = ATTN_W + 2 * KV_W + 2 * CONV_CH
NUM_BUCKETS = 32
MAX_EXACT = NUM_BUCKETS // 2
REL_MAX_DIST = 128
N_EXPERTS = 256
TOP_K = 8
N_GROUPS = 8
GROUP_SIZE = N_EXPERTS // N_GROUPS
TOPK_GROUPS = 4
EXPERT_FF = 256
SHARED_FF = 256
ROUTED_SCALE = 2.5
DEPTH = 1
ALPHA = (2.0 * DEPTH) ** 0.25
LN_EPS = 1e-5
NEG = -1e30
PAD_FRONT = (-N_META) % BLOCK

VMEM_LIMIT = 48 * 1024 * 1024

TQ_PROJ = 512
T_CONV = 256
CONV_HALO = 32
R_CONV = 32
TQ_MIX = 256
TN_ROUTE = 256
TN_DISP = 256
TM_EXP = 256
TN_COMB = 128


def _cparams(*sem):
    return pltpu.CompilerParams(dimension_semantics=sem, vmem_limit_bytes=VMEM_LIMIT)


def _layer_norm(x, g, b):
    mu = jnp.mean(x, axis=-1, keepdims=True)
    xc = x - mu
    var = jnp.mean(xc * xc, axis=-1, keepdims=True)
    return xc * lax.rsqrt(var + LN_EPS) * g + b


def _sigmoid(x):
    return 1.0 / (1.0 + jnp.exp(-x))


def _proj_kernel(x_ref, g_ref, b_ref, w_ref, q_ref, k_ref, v_ref, u_ref):
    h = _layer_norm(x_ref[...], g_ref[...], b_ref[...])
    p = jnp.dot(h.astype(BF16), w_ref[...], preferred_element_type=F32)
    q_ref[...] = (p[:, :ATTN_W] * (HEAD_DIM ** -0.5)).astype(BF16)
    k_ref[...] = p[:, ATTN_W:ATTN_W + KV_W].astype(BF16)
    v_ref[...] = p[:, ATTN_W + KV_W:ATTN_W + 2 * KV_W].astype(BF16)
    a = p[:, ATTN_W + 2 * KV_W:ATTN_W + 2 * KV_W + CONV_CH]
    gate = p[:, ATTN_W + 2 * KV_W + CONV_CH:]
    u_ref[...] = a * _sigmoid(gate)


def _proj_call(x2d, gin, bin_, w_in_b, tq):
    t = x2d.shape[0]
    row = lambda i: (i, 0)
    fix = lambda i: (0, 0)
    return pl.pallas_call(
        _proj_kernel,
        grid=(t // tq,),
        in_specs=[pl.BlockSpec((tq, D_MODEL), row), pl.BlockSpec((1, D_MODEL), fix),
                  pl.BlockSpec((1, D_MODEL), fix), pl.BlockSpec((D_MODEL, IN_W), fix)],
        out_specs=[pl.BlockSpec((tq, ATTN_W), row), pl.BlockSpec((tq, KV_W), row),
                   pl.BlockSpec((tq, KV_W), row), pl.BlockSpec((tq, CONV_CH), row)],
        out_shape=[SDS((t, ATTN_W), BF16), SDS((t, KV_W), BF16), SDS((t, KV_W), BF16), SDS((t, CONV_CH), F32)],
        compiler_params=_cparams("arbitrary"),
        name="ln_in_proj",
    )(x2d, gin, bin_, w_in_b)


def _attn_kernel(sinks_ref, q_ref, kc_ref, kp_ref, vc_ref, vp_ref, km_ref, vm_ref, bias_ref, o_ref):
    first = pl.program_id(1) == 0
    kp = jnp.where(first, km_ref[...], kp_ref[...])
    vp = jnp.where(first, vm_ref[...], vp_ref[...])
    k = jnp.concatenate([kp, kc_ref[...]], axis=0)
    v = jnp.concatenate([vp, vc_ref[...]], axis=0)
    col = lax.broadcasted_iota(I32, (BLOCK, 2 * BLOCK), 1)
    pad_bias = jnp.where(jnp.logical_and(first, col < PAD_FRONT), NEG, 0.0).astype(F32)
    q = q_ref[...]
    outs = []
    for h in range(N_Q_HEADS):
        g = h // GQA_GROUP
        qh = q[:, h * HEAD_DIM:(h + 1) * HEAD_DIM]
        kg = k[:, g * HEAD_DIM:(g + 1) * HEAD_DIM]
        vg = v[:, g * HEAD_DIM:(g + 1) * HEAD_DIM]
        s = lax.dot_general(qh, kg, (((1,), (1,)), ((), ())), preferred_element_type=F32)
        s = s + bias_ref[h] + pad_bias
        sink = sinks_ref[h]
        m = jnp.maximum(jnp.max(s, axis=-1, keepdims=True), sink)
        p = jnp.exp(s - m)
        den = jnp.sum(p, axis=-1, keepdims=True) + jnp.exp(sink - m)
        o = jnp.dot(p.astype(BF16), vg, preferred_element_type=F32)
        outs.append(o / den)
    o_ref[...] = jnp.concatenate(outs, axis=1).astype(BF16)


def _attn_call(q, k, v, k_meta, v_meta, bias, sinks, nbatch, nblk):
    t = q.shape[0]
    cur = lambda b, j: (b * nblk + j, 0)
    prev = lambda b, j: (jnp.maximum(b * nblk + j - 1, 0), 0)
    fix2 = lambda b, j: (0, 0)
    return pl.pallas_call(
        _attn_kernel,
        grid=(nbatch, nblk),
        in_specs=[pl.BlockSpec(memory_space=pltpu.SMEM),
                  pl.BlockSpec((BLOCK, ATTN_W), cur),
                  pl.BlockSpec((BLOCK, KV_W), cur), pl.BlockSpec((BLOCK, KV_W), prev),
                  pl.BlockSpec((BLOCK, KV_W), cur), pl.BlockSpec((BLOCK, KV_W), prev),
                  pl.BlockSpec((BLOCK, KV_W), fix2), pl.BlockSpec((BLOCK, KV_W), fix2),
                  pl.BlockSpec((N_Q_HEADS, BLOCK, 2 * BLOCK), lambda b, j: (0, 0, 0))],
        out_specs=pl.BlockSpec((BLOCK, ATTN_W), cur),
        out_shape=SDS((t, ATTN_W), BF16),
        compiler_params=_cparams("arbitrary", "arbitrary"),
        name="swa_attn",
    )(sinks, q, k, k, v, v, k_meta, v_meta, bias)


def _rel_bias_table(rel_bias):
    qi = np.arange(BLOCK, dtype=np.int32)[:, None]
    kj = np.arange(2 * BLOCK, dtype=np.int32)[None, :]
    dist = BLOCK + qi - kj
    dc = np.clip(dist, 0, WINDOW - 1)
    nf = np.maximum(dc, 1).astype(np.float32)
    large = MAX_EXACT + (np.log(nf / np.float32(MAX_EXACT)) / np.float32(math.log(REL_MAX_DIST / MAX_EXACT))
                         * np.float32(NUM_BUCKETS - MAX_EXACT)).astype(np.int32)
    large = np.minimum(large, NUM_BUCKETS - 1)
    bucket = np.where(dc < MAX_EXACT, dc, large)
    in_window = (dist >= 0) & (dist < WINDOW)
    onehot = (bucket.reshape(-1, 1) == np.arange(NUM_BUCKETS)[None, :]).astype(np.float32)
    bias = jnp.dot(jnp.asarray(onehot), rel_bias.astype(F32), precision=lax.Precision.HIGHEST)
    bias = jnp.transpose(bias.reshape(BLOCK, 2 * BLOCK, N_Q_HEADS), (2, 0, 1))
    return jnp.where(in_window[None], bias, NEG)


def _conv_kernel(uc_ref, up_ref, um_ref, w_ref, cb_ref, g_ref, b_ref, o_ref, s_ref):
    first = pl.program_id(1) == 0
    s_ref[0:CONV_HALO, :] = jnp.where(first, um_ref[...], up_ref[...])
    s_ref[CONV_HALO:CONV_HALO + T_CONV, :] = uc_ref[...]
    off = CONV_HALO - (CONV_K - 1)
    for c in range(0, T_CONV, R_CONV):
        acc = jnp.zeros((R_CONV, CONV_CH), F32) + cb_ref[...]
        for kk in range(CONV_K):
            acc = acc + s_ref[c + off + kk:c + off + kk + R_CONV, :] * w_ref[kk:kk + 1, :]
        y = _layer_norm(acc, g_ref[...], b_ref[...])
        o_ref[c:c + R_CONV, :] = (y * _sigmoid(y)).astype(BF16)


def _conv_call(u, u_meta_halo, conv_w, conv_b, g, b, nbatch, seq):
    t = u.shape[0]
    nj = seq // T_CONV
    per = T_CONV // CONV_HALO
    cur = lambda bb, j: (bb * nj + j, 0)
    prev = lambda bb, j: (jnp.maximum((bb * nj + j) * per - 1, 0), 0)
    fix = lambda bb, j: (0, 0)
    return pl.pallas_call(
        _conv_kernel,
        grid=(nbatch, nj),
        in_specs=[pl.BlockSpec((T_CONV, CONV_CH), cur), pl.BlockSpec((CONV_HALO, CONV_CH), prev),
                  pl.BlockSpec((CONV_HALO, CONV_CH), fix), pl.BlockSpec((CONV_K, CONV_CH), fix),
                  pl.BlockSpec((1, CONV_CH), fix), pl.BlockSpec((1, CONV_CH), fix), pl.BlockSpec((1, CONV_CH), fix)],
        out_specs=pl.BlockSpec((T_CONV, CONV_CH), cur),
        out_shape=SDS((t, CONV_CH), BF16),
        scratch_shapes=[pltpu.VMEM((CONV_HALO + T_CONV, CONV_CH), F32)],
        compiler_params=_cparams("arbitrary", "arbitrary"),
        name="conv_ln",
    )(u, u, u_meta_halo, conv_w, conv_b, g, b)


def _mix_kernel(x_ref, at_ref, cv_ref, gin_ref, bin_ref, woa_ref, woc_ref, g1_ref, b1_ref,
                wrh_ref, wrl_ref, h1_ref, h1r_ref, lg_ref):
    h = _layer_norm(x_ref[...], gin_ref[...], bin_ref[...])
    mix = (jnp.dot(at_ref[...], woa_ref[...], preferred_element_type=F32)
           + jnp.dot(cv_ref[...], woc_ref[...], preferred_element_type=F32))
    h1 = _layer_norm(ALPHA * h + mix, g1_ref[...], b1_ref[...])
    h1_ref[...] = h1
    h1r_ref[...] = h1.reshape(h1r_ref.shape)
    hh = h1.astype(BF16)
    hl = (h1 - hh.astype(F32)).astype(BF16)
    nt = (((1,), (1,)), ((), ()))
    lg = lax.dot_general(wrh_ref[...], hh, nt, preferred_element_type=F32)
    lg = lg + lax.dot_general(wrh_ref[...], hl, nt, preferred_element_type=F32)
    lg = lg + lax.dot_general(wrl_ref[...], hh, nt, preferred_element_type=F32)
    lg_ref[...] = lg


def _mix_call(x2d, attn, conv, gin, bin_, woa, woc, g1, b1, wrh, wrl):
    t = x2d.shape[0]
    tq = TQ_MIX
    row = lambda i: (i, 0)
    fix = lambda i: (0, 0)
    return pl.pallas_call(
        _mix_kernel,
        grid=(t // tq,),
        in_specs=[pl.BlockSpec((tq, D_MODEL), row), pl.BlockSpec((tq, ATTN_W), row), pl.BlockSpec((tq, CONV_CH), row),
                  pl.BlockSpec((1, D_MODEL), fix), pl.BlockSpec((1, D_MODEL), fix),
                  pl.BlockSpec((ATTN_W, D_MODEL), fix), pl.BlockSpec((CONV_CH, D_MODEL), fix),
                  pl.BlockSpec((1, D_MODEL), fix), pl.BlockSpec((1, D_MODEL), fix),
                  pl.BlockSpec((N_EXPERTS, D_MODEL), fix), pl.BlockSpec((N_EXPERTS, D_MODEL), fix)],
        out_specs=[pl.BlockSpec((tq, D_MODEL), row), pl.BlockSpec((tq, 1, D_MODEL), lambda i: (i, 0, 0)),
                   pl.BlockSpec((N_EXPERTS, tq), lambda i: (0, i))],
        out_shape=[SDS((t, D_MODEL), F32), SDS((t, 1, D_MODEL), F32), SDS((N_EXPERTS, t), F32)],
        compiler_params=_cparams("arbitrary"),
        name="mix_ln1",
    )(x2d, attn, conv, gin, bin_, woa, woc, g1, b1, wrh, wrl)


def _first_argmax(x, rows, nrows):
    m = jnp.max(x, axis=0, keepdims=True)
    idx = jnp.min(jnp.where(x == m, rows, nrows), axis=0, keepdims=True)
    return m, idx


def _route_kernel(lg_ref, rb_ref, idx_ref, wts_ref, rank_ref, cnt_ref, carry_ref):
    tn = lg_ref.shape[1]

    @pl.when(pl.program_id(0) == 0)
    def _():
        carry_ref[...] = jnp.zeros_like(carry_ref)

    scores = _sigmoid(lg_ref[...])
    choice = scores + rb_ref[...]
    rows = lax.broadcasted_iota(I32, (N_EXPERTS, tn), 0)
    rows_g = lax.broadcasted_iota(I32, (GROUP_SIZE, tn), 0)
    rows_8 = lax.broadcasted_iota(I32, (N_GROUPS, tn), 0)

    gs = []
    for g in range(N_GROUPS):
        xg = choice[g * GROUP_SIZE:(g + 1) * GROUP_SIZE, :]
        m1, i1 = _first_argmax(xg, rows_g, GROUP_SIZE)
        m2 = jnp.max(jnp.where(rows_g == i1, -jnp.inf, xg), axis=0, keepdims=True)
        gs.append(m1 + m2)
    gsc = jnp.concatenate(gs, axis=0)
    gsel = jnp.zeros((N_GROUPS, tn), F32)
    for _ in range(TOPK_GROUPS):
        _, gi = _first_argmax(gsc, rows_8, N_GROUPS)
        hit = rows_8 == gi
        gsel = jnp.where(hit, 1.0, gsel)
        gsc = jnp.where(hit, -jnp.inf, gsc)
    emask = jnp.concatenate(
        [jnp.broadcast_to(gsel[g:g + 1, :], (GROUP_SIZE, tn)) for g in range(N_GROUPS)], axis=0)
    masked = jnp.where(emask > 0.5, choice, NEG)

    sel_all = jnp.zeros((N_EXPERTS, tn), F32)
    hits, idxs, ws = [], [], []
    for _ in range(TOP_K):
        _, ii = _first_argmax(masked, rows, N_EXPERTS)
        hit = rows == ii
        hits.append(hit)
        idxs.append(ii)
        ws.append(jnp.sum(jnp.where(hit, scores, 0.0), axis=0, keepdims=True))
        sel_all = jnp.where(hit, 1.0, sel_all)
        masked = jnp.where(hit, -jnp.inf, masked)
    wsum = ws[0]
    for w in ws[1:]:
        wsum = wsum + w
    idx_ref[...] = jnp.concatenate(idxs, axis=0)
    wts_ref[...] = jnp.concatenate([w / wsum * ROUTED_SCALE for w in ws], axis=0)

    r_i = lax.broadcasted_iota(I32, (tn, tn), 0)
    c_i = lax.broadcasted_iota(I32, (tn, tn), 1)
    upper = jnp.where(r_i < c_i, 1.0, 0.0).astype(BF16)
    sel_b = sel_all.astype(BF16)
    carry = carry_ref[...]
    before = jnp.dot(sel_b, upper, preferred_element_type=F32)
    before = before + jnp.concatenate([carry] * (tn // 128), axis=1)
    rank_ref[...] = jnp.concatenate(
        [jnp.sum(jnp.where(h, before, 0.0), axis=0, keepdims=True) for h in hits], axis=0).astype(I32)
    carry = carry + jnp.dot(sel_b, jnp.ones((tn, 128), BF16), preferred_element_type=F32)
    carry_ref[...] = carry
    cnt_ref[...] = carry.astype(I32)


def _route_call(lg, rbias):
    t = lg.shape[1]
    tn = TN_ROUTE
    col = lambda i: (0, i)
    return pl.pallas_call(
        _route_kernel,
        grid=(t // tn,),
        in_specs=[pl.BlockSpec((N_EXPERTS, tn), col), pl.BlockSpec((N_EXPERTS, 1), lambda i: (0, 0))],
        out_specs=[pl.BlockSpec((TOP_K, tn), col), pl.BlockSpec((TOP_K, tn), col), pl.BlockSpec((TOP_K, tn), col),
                   pl.BlockSpec((N_EXPERTS, 128), lambda i: (0, 0))],
        out_shape=[SDS((TOP_K, t), I32), SDS((TOP_K, t), F32), SDS((TOP_K, t), I32), SDS((N_EXPERTS, 128), I32)],
        scratch_shapes=[pltpu.VMEM((N_EXPERTS, 128), F32)],
        compiler_params=_cparams("arbitrary"),
        name="route",
    )(lg, rbias)


def _dispatch_kernel(idx_ref, rank_ref, offs_ref, h_ref, xs_ref, dest_ref, sem):
    tn = idx_ref.shape[1]

    def row_copy(src_row, dst_row):
        return pltpu.make_async_copy(h_ref.at[src_row], xs_ref.at[dst_row], sem)

    def issue(t, c):
        for kk in range(TOP_K):
            d = offs_ref[idx_ref[kk, t]] + rank_ref[kk, t]
            dest_ref[kk, t] = d
            row_copy(t, d).start()
        return c

    lax.fori_loop(0, tn, issue, 0)

    def drain(t, c):
        for kk in range(TOP_K):
            row_copy(0, 0).wait()
        return c

    lax.fori_loop(0, tn, drain, 0)


def _dispatch_call(idx, rank, offs, h1rows, n_rows):
    t = idx.shape[1]
    tn = TN_DISP
    col = lambda i: (0, i)
    return pl.pallas_call(
        _dispatch_kernel,
        grid=(t // tn,),
        in_specs=[pl.BlockSpec((TOP_K, tn), col, memory_space=pltpu.SMEM),
                  pl.BlockSpec((TOP_K, tn), col, memory_space=pltpu.SMEM),
                  pl.BlockSpec(memory_space=pltpu.SMEM),
                  pl.BlockSpec((tn, 1, D_MODEL), lambda i: (i, 0, 0))],
        out_specs=[pl.BlockSpec(memory_space=pl.ANY),
                   pl.BlockSpec((TOP_K, tn), col, memory_space=pltpu.SMEM)],
        out_shape=[SDS((n_rows, 1, D_MODEL), F32), SDS((TOP_K, t), I32)],
        scratch_shapes=[pltpu.SemaphoreType.DMA(())],
        compiler_params=_cparams("arbitrary"),
        name="dispatch",
    )(idx, rank, offs, h1rows)


def _expert_kernel(te_ref, tr_ref, x_ref, wg_ref, wu_ref, wd_ref, y_ref):
    tm = x_ref.shape[0]
    x = x_ref[...].reshape(tm, D_MODEL)
    g = jnp.dot(x, wg_ref[0], preferred_element_type=F32)
    u = jnp.dot(x, wu_ref[0], preferred_element_type=F32)
    live = lax.broadcasted_iota(I32, (tm, EXPERT_FF), 0) < tr_ref[pl.program_id(0)]
    hid = jnp.where(live, g * _sigmoid(g) * u, 0.0)
    y = jnp.dot(hid, wd_ref[0], preferred_element_type=F32)
    y_ref[...] = y.reshape(tm, 1, D_MODEL)


def _expert_call(tile_expert, tile_rows, xs, w_gate, w_up, w_down):
    n_rows = xs.shape[0]
    tm = TM_EXP
    return pl.pallas_call(
        _expert_kernel,
        grid_spec=pltpu.PrefetchScalarGridSpec(
            num_scalar_prefetch=2,
            grid=(n_rows // tm,),
            in_specs=[pl.BlockSpec((tm, 1, D_MODEL), lambda i, te, tr: (i, 0, 0)),
                      pl.BlockSpec((1, D_MODEL, EXPERT_FF), lambda i, te, tr: (te[i], 0, 0)),
                      pl.BlockSpec((1, D_MODEL, EXPERT_FF), lambda i, te, tr: (te[i], 0, 0)),
                      pl.BlockSpec((1, EXPERT_FF, D_MODEL), lambda i, te, tr: (te[i], 0, 0))],
            out_specs=pl.BlockSpec((tm, 1, D_MODEL), lambda i, te, tr: (i, 0, 0)),
        ),
        out_shape=SDS((n_rows, 1, D_MODEL), F32),
        compiler_params=_cparams("arbitrary"),
        name="experts",
    )(tile_expert, tile_rows, xs, w_gate, w_up, w_down)


def _combine_kernel(dest_ref, wts_ref, h1_ref, ys_ref, wsg_ref, wsu_ref, wsd_ref, g2_ref, b2_ref,
                    o_ref, ybuf_ref, sem):
    tn = h1_ref.shape[0]

    def row_copy(src_row, slot):
        return pltpu.make_async_copy(ys_ref.at[src_row], ybuf_ref.at[slot], sem)

    def issue(t, c):
        for kk in range(TOP_K):
            row_copy(dest_ref[kk, t], kk * tn + t).start()
        return c

    lax.fori_loop(0, tn, issue, 0)

    h1 = h1_ref[...]
    hb = h1.astype(BF16)
    sg = jnp.dot(hb, wsg_ref[...], preferred_element_type=F32)
    su = jnp.dot(hb, wsu_ref[...], preferred_element_type=F32)
    ff = jnp.dot((sg * _sigmoid(sg) * su).astype(BF16), wsd_ref[...], preferred_element_type=F32)

    def drain(t, c):
        for kk in range(TOP_K):
            row_copy(0, 0).wait()
        return c

    lax.fori_loop(0, tn, drain, 0)

    eye = lax.broadcasted_iota(I32, (tn, tn), 0) == lax.broadcasted_iota(I32, (tn, tn), 1)
    w = wts_ref[...]
    for kk in range(TOP_K):
        wk = jnp.where(eye, w[kk:kk + 1, :], 0.0)
        wh = wk.astype(BF16).astype(F32)
        yk = ybuf_ref[kk * tn:(kk + 1) * tn].reshape(tn, D_MODEL)
        ff = ff + jnp.dot(wh, yk, preferred_element_type=F32)
        ff = ff + jnp.dot(wk - wh, yk, preferred_element_type=F32)
    o_ref[...] = _layer_norm(ALPHA * h1 + ff, g2_ref[...], b2_ref[...])


def _combine_call(dest, wts, h1, ys, wsg, wsu, wsd, g2, b2):
    t = h1.shape[0]
    tn = TN_COMB
    col = lambda i: (0, i)
    row = lambda i: (i, 0)
    fix = lambda i: (0, 0)
    return pl.pallas_call(
        _combine_kernel,
        grid=(t // tn,),
        in_specs=[pl.BlockSpec((TOP_K, tn), col, memory_space=pltpu.SMEM),
                  pl.BlockSpec((TOP_K, tn), col),
                  pl.BlockSpec((tn, D_MODEL), row),
                  pl.BlockSpec(memory_space=pl.ANY),
                  pl.BlockSpec((D_MODEL, SHARED_FF), fix), pl.BlockSpec((D_MODEL, SHARED_FF), fix),
                  pl.BlockSpec((SHARED_FF, D_MODEL), fix),
                  pl.BlockSpec((1, D_MODEL), fix), pl.BlockSpec((1, D_MODEL), fix)],
        out_specs=pl.BlockSpec((tn, D_MODEL), row),
        out_shape=SDS((t, D_MODEL), F32),
        scratch_shapes=[pltpu.VMEM((TOP_K * tn, 1, D_MODEL), F32), pltpu.SemaphoreType.DMA(())],
        compiler_params=_cparams("arbitrary"),
        name="combine_ln2",
    )(dest, wts, h1, ys, wsg, wsu, wsd, g2, b2)


def kernel(x, meta_tokens, ln_in_g, ln_in_b, rel_bias, w_in, conv_w, conv_b, conv_ln_g, conv_ln_b, sinks,
           w_out, ln1_g, ln1_b, w_router, router_bias, w_gate, w_up, w_down, ws_gate, ws_up, ws_down,
           ln2_g, ln2_b):
    nbatch, seq, d = x.shape
    assert d == D_MODEL and seq % TQ_PROJ == 0 and w_in.shape[0] == DEPTH
    t = nbatch * seq
    x2d = x.reshape(t, D_MODEL)
    vec = lambda a: a.reshape(1, -1).astype(F32)
    gin, bin_ = vec(ln_in_g), vec(ln_in_b)
    w_in_b = w_in[0].astype(BF16)

    q, k, v, u = _proj_call(x2d, gin, bin_, w_in_b, TQ_PROJ)
    meta_blk = jnp.concatenate([jnp.zeros((PAD_FRONT, D_MODEL), F32), meta_tokens.astype(F32)], axis=0)
    _, k_meta, v_meta, u_meta = _proj_call(meta_blk, gin, bin_, w_in_b, BLOCK)

    attn = _attn_call(q, k, v, k_meta, v_meta, _rel_bias_table(rel_bias), sinks[0].astype(F32),
                      nbatch, seq // BLOCK)

    u_halo = jnp.concatenate([jnp.zeros((CONV_HALO - N_META, CONV_CH), F32), u_meta[PAD_FRONT:]], axis=0)
    conv = _conv_call(u, u_halo, conv_w[0].astype(F32), vec(conv_b[0]), vec(conv_ln_g[0]), vec(conv_ln_b[0]),
                      nbatch, seq)

    w_out_b = w_out[0].astype(BF16)
    wr_t = w_router[0].astype(F32).T
    wr_hi = wr_t.astype(BF16)
    wr_lo = (wr_t - wr_hi.astype(F32)).astype(BF16)
    h1, h1rows, logits = _mix_call(x2d, attn, conv, gin, bin_, w_out_b[:ATTN_W], w_out_b[ATTN_W:],
                                   vec(ln1_g[0]), vec(ln1_b[0]), wr_hi, wr_lo)

    idx, wts, rank, cnt = _route_call(logits, router_bias[0].astype(F32).reshape(N_EXPERTS, 1))

    tm = TM_EXP
    n_tiles = (t * TOP_K) // tm + N_EXPERTS
    counts = cnt[:, 0]
    tiles_e = (counts + tm - 1) // tm
    tile_end = jnp.cumsum(tiles_e)
    tile_start = tile_end - tiles_e
    offs = (tile_start * tm).astype(I32)
    tile_id = jnp.arange(n_tiles, dtype=I32)
    tile_expert = jnp.minimum(jnp.sum(tile_id[:, None] >= tile_end[None, :], axis=1), N_EXPERTS - 1).astype(I32)
    lo = jnp.maximum(tile_id[:, None] * tm, offs[None, :])
    hi = jnp.minimum((tile_id[:, None] + 1) * tm, (offs + counts)[None, :])
    tile_rows = jnp.sum(jnp.clip(hi - lo, 0, tm), axis=1).astype(I32)

    xs, dest = _dispatch_call(idx, rank, offs, h1rows, n_tiles * tm)
    ys = _expert_call(tile_expert, tile_rows, xs, w_gate[0], w_up[0], w_down[0])
    out = _combine_call(dest, wts, h1, ys, ws_gate[0].astype(BF16), ws_up[0].astype(BF16),
                        ws_down[0].astype(BF16), vec(ln2_g[0]), vec(ln2_b[0]))
    return out.reshape(nbatch, seq, D_MODEL)
```

```python
import functools
import math

import numpy as np
import jax
import jax.numpy as jnp
from jax import lax
from jax.experimental import pallas as pl
from jax.experimental.pallas import tpu as pltpu

F32 = jnp.float32
BF16 = jnp.bfloat16
I32 = jnp.int32
U32 = jnp.uint32
SDS = jax.ShapeDtypeStruct

D_MODEL = 1024
HALF = D_MODEL // 2
LANES = 128
ROW_CHUNKS = HALF // LANES
N_META = 16
HEAD_DIM = 64
N_Q_HEADS = 8
N_KV_HEADS = 2
GQA_GROUP = N_Q_HEADS // N_KV_HEADS
ATTN_W = N_Q_HEADS * HEAD_DIM
KV_W = N_KV_HEADS * HEAD_DIM
WINDOW = 128
BLOCK = 128
CONV_CH = D_MODEL - ATTN_W
CONV_K = 31
IN_W = ATTN_W + 2 * KV_W + 2 * CONV_CH
NUM_BUCKETS = 32
MAX_EXACT = NUM_BUCKETS // 2
REL_MAX_DIST = 128
N_EXPERTS = 256
TOP_K = 8
N_GROUPS = 8
GROUP_SIZE = N_EXPERTS // N_GROUPS
TOPK_GROUPS = 4
EXPERT_FF = 256
SHARED_FF = 256
ROUTED_SCALE = 2.5
DEPTH = 1
ALPHA = (2.0 * DEPTH) ** 0.25
LN_EPS = 1e-5
NEG = -1e30
PAD_FRONT = (-N_META) % BLOCK

VMEM_LIMIT = 48 * 1024 * 1024

TQ_PROJ = 512
T_CONV = 256
CONV_HALO = 32
R_CONV = 32
TQ_MIX = 256
TN_ROUTE = 256
TN_DISP = 256
TM_EXP = 256
TN_COMB = 128


def _cparams(*sem):
    return pltpu.CompilerParams(dimension_semantics=sem, vmem_limit_bytes=VMEM_LIMIT)


def _layer_norm(x, g, b):
    mu = jnp.mean(x, axis=-1, keepdims=True)
    xc = x - mu
    var = jnp.mean(xc * xc, axis=-1, keepdims=True)
    return xc * lax.rsqrt(var + LN_EPS) * g + b


def _sigmoid(x):
    return 1.0 / (1.0 + jnp.exp(-x))


def _pack_rows(lo_half, hi_half):
    lo = lax.bitcast_convert_type(lo_half.astype(BF16).astype(F32), U32)
    hi = lax.bitcast_convert_type(hi_half.astype(BF16).astype(F32), U32)
    return lax.shift_right_logical(lo, jnp.uint32(16)) | hi


def _unpack_rows(p):
    lo = lax.bitcast_convert_type(lax.shift_left(p, jnp.uint32(16)), F32)
    hi = lax.bitcast_convert_type(p & jnp.uint32(0xFFFF0000), F32)
    return lo, hi


def _store_packed(ref, start, n, packed):
    for j in range(ROW_CHUNKS):
        ref[pl.ds(start + j, n, stride=ROW_CHUNKS), :] = packed[:, j * LANES:(j + 1) * LANES]


def _load_packed_bf16(ref, start, n):
    halves = [_unpack_rows(ref[pl.ds(start + j, n, stride=ROW_CHUNKS), :]) for j in range(ROW_CHUNKS)]
    return jnp.concatenate([h[0] for h in halves] + [h[1] for h in halves], axis=1).astype(BF16)


def _packed_row(ref, token):
    return ref.at[pl.ds(pl.multiple_of(token * ROW_CHUNKS, ROW_CHUNKS), ROW_CHUNKS), :]


def _proj_kernel(x_ref, g_ref, b_ref, w_ref, q_ref, k_ref, v_ref, u_ref):
    h = _layer_norm(x_ref[...], g_ref[...], b_ref[...])
    p = jnp.dot(h.astype(BF16), w_ref[...], preferred_element_type=F32)
    q_ref[...] = (p[:, :ATTN_W] * (HEAD_DIM ** -0.5)).astype(BF16)
    k_ref[...] = p[:, ATTN_W:ATTN_W + KV_W].astype(BF16)
    v_ref[...] = p[:, ATTN_W + KV_W:ATTN_W + 2 * KV_W].astype(BF16)
    a = p[:, ATTN_W + 2 * KV_W:ATTN_W + 2 * KV_W + CONV_CH]
    gate = p[:, ATTN_W + 2 * KV_W + CONV_CH:]
    u_ref[...] = a * _sigmoid(gate)


def _proj_call(x2d, gin, bin_, w_in_b, tq):
    t = x2d.shape[0]
    row = lambda i: (i, 0)
    fix = lambda i: (0, 0)
    return pl.pallas_call(
        _proj_kernel,
        grid=(t // tq,),
        in_specs=[pl.BlockSpec((tq, D_MODEL), row), pl.BlockSpec((1, D_MODEL), fix),
                  pl.BlockSpec((1, D_MODEL), fix), pl.BlockSpec((D_MODEL, IN_W), fix)],
        out_specs=[pl.BlockSpec((tq, ATTN_W), row), pl.BlockSpec((tq, KV_W), row),
                   pl.BlockSpec((tq, KV_W), row), pl.BlockSpec((tq, CONV_CH), row)],
        out_shape=[SDS((t, ATTN_W), BF16), SDS((t, KV_W), BF16), SDS((t, KV_W), BF16), SDS((t, CONV_CH), F32)],
        compiler_params=_cparams("arbitrary"),
        name="ln_in_proj",
    )(x2d, gin, bin_, w_in_b)


def _attn_kernel(sinks_ref, q_ref, kc_ref, kp_ref, vc_ref, vp_ref, km_ref, vm_ref, bias_ref, o_ref):
    first = pl.program_id(1) == 0
    kp = jnp.where(first, km_ref[...], kp_ref[...])
    vp = jnp.where(first, vm_ref[...], vp_ref[...])
    k = jnp.concatenate([kp, kc_ref[...]], axis=0)
    v = jnp.concatenate([vp, vc_ref[...]], axis=0)
    col = lax.broadcasted_iota(I32, (BLOCK, 2 * BLOCK), 1)
    pad_bias = jnp.where(jnp.logical_and(first, col < PAD_FRONT), NEG, 0.0).astype(F32)
    q = q_ref[...]
    outs = []
    for h in range(N_Q_HEADS):
        g = h // GQA_GROUP
        qh = q[:, h * HEAD_DIM:(h + 1) * HEAD_DIM]
        kg = k[:, g * HEAD_DIM:(g + 1) * HEAD_DIM]
        vg = v[:, g * HEAD_DIM:(g + 1) * HEAD_DIM]
        s = lax.dot_general(qh, kg, (((1,), (1,)), ((), ())), preferred_element_type=F32)
        s = s + bias_ref[h] + pad_bias
        sink = sinks_ref[h]
        m = jnp.maximum(jnp.max(s, axis=-1, keepdims=True), sink)
        p = jnp.exp(s - m)
        den = jnp.sum(p, axis=-1, keepdims=True) + jnp.exp(sink - m)
        o = jnp.dot(p.astype(BF16), vg, preferred_element_type=F32)
        outs.append(o / den)
    o_ref[...] = jnp.concatenate(outs, axis=1).astype(BF16)


def _attn_call(q, k, v, k_meta, v_meta, bias, sinks, nbatch, nblk):
    t = q.shape[0]
    cur = lambda b, j: (b * nblk + j, 0)
    prev = lambda b, j: (jnp.maximum(b * nblk + j - 1, 0), 0)
    fix2 = lambda b, j: (0, 0)
    return pl.pallas_call(
        _attn_kernel,
        grid=(nbatch, nblk),
        in_specs=[pl.BlockSpec(memory_space=pltpu.SMEM),
                  pl.BlockSpec((BLOCK, ATTN_W), cur),
                  pl.BlockSpec((BLOCK, KV_W), cur), pl.BlockSpec((BLOCK, KV_W), prev),
                  pl.BlockSpec((BLOCK, KV_W), cur), pl.BlockSpec((BLOCK, KV_W), prev),
                  pl.BlockSpec((BLOCK, KV_W), fix2), pl.BlockSpec((BLOCK, KV_W), fix2),
                  pl.BlockSpec((N_Q_HEADS, BLOCK, 2 * BLOCK), lambda b, j: (0, 0, 0))],
        out_specs=pl.BlockSpec((BLOCK, ATTN_W), cur),
        out_shape=SDS((t, ATTN_W), BF16),
        compiler_params=_cparams("arbitrary", "arbitrary"),
        name="swa_attn",
    )(sinks, q, k, k, v, v, k_meta, v_meta, bias)


def _rel_bias_table(rel_bias):
    qi = np.arange(BLOCK, dtype=np.int32)[:, None]
    kj = np.arange(2 * BLOCK, dtype=np.int32)[None, :]
    dist = BLOCK + qi - kj
    dc = np.clip(dist, 0, WINDOW - 1)
    nf = np.maximum(dc, 1).astype(np.float32)
    large = MAX_EXACT + (np.log(nf / np.float32(MAX_EXACT)) / np.float32(math.log(REL_MAX_DIST / MAX_EXACT))
                         * np.float32(NUM_BUCKETS - MAX_EXACT)).astype(np.int32)
    large = np.minimum(large, NUM_BUCKETS - 1)
    bucket = np.where(dc < MAX_EXACT, dc, large)
    in_window = (dist >= 0) & (dist < WINDOW)
    onehot = (bucket.reshape(-1, 1) == np.arange(NUM_BUCKETS)[None, :]).astype(np.float32)
    bias = jnp.dot(jnp.asarray(onehot), rel_bias.astype(F32), precision=lax.Precision.HIGHEST)
    bias = jnp.transpose(bias.reshape(BLOCK, 2 * BLOCK, N_Q_HEADS), (2, 0, 1))
    return jnp.where(in_window[None], bias, NEG)


def _conv_kernel(uc_ref, up_ref, um_ref, w_ref, cb_ref, g_ref, b_ref, o_ref, s_ref):
    first = pl.program_id(1) == 0
    s_ref[0:CONV_HALO, :] = jnp.where(first, um_ref[...], up_ref[...])
    s_ref[CONV_HALO:CONV_HALO + T_CONV, :] = uc_ref[...]
    off = CONV_HALO - (CONV_K - 1)
    for c in range(0, T_CONV, R_CONV):
        acc = jnp.zeros((R_CONV, CONV_CH), F32) + cb_ref[...]
        for kk in range(CONV_K):
            acc = acc + s_ref[c + off + kk:c + off + kk + R_CONV, :] * w_ref[kk:kk + 1, :]
        y = _layer_norm(acc, g_ref[...], b_ref[...])
        o_ref[c:c + R_CONV, :] = (y * _sigmoid(y)).astype(BF16)


def _conv_call(u, u_meta_halo, conv_w, conv_b, g, b, nbatch, seq):
    t = u.shape[0]
    nj = seq // T_CONV
    per = T_CONV // CONV_HALO
    cur = lambda bb, j: (bb * nj + j, 0)
    prev = lambda bb, j: (jnp.maximum((bb * nj + j) * per - 1, 0), 0)
    fix = lambda bb, j: (0, 0)
    return pl.pallas_call(
        _conv_kernel,
        grid=(nbatch, nj),
        in_specs=[pl.BlockSpec((T_CONV, CONV_CH), cur), pl.BlockSpec((CONV_HALO, CONV_CH), prev),
                  pl.BlockSpec((CONV_HALO, CONV_CH), fix), pl.BlockSpec((CONV_K, CONV_CH), fix),
                  pl.BlockSpec((1, CONV_CH), fix), pl.BlockSpec((1, CONV_CH), fix), pl.BlockSpec((1, CONV_CH), fix)],
        out_specs=pl.BlockSpec((T_CONV, CONV_CH), cur),
        out_shape=SDS((t, CONV_CH), BF16),
        scratch_shapes=[pltpu.VMEM((CONV_HALO + T_CONV, CONV_CH), F32)],
        compiler_params=_cparams("arbitrary", "arbitrary"),
        name="conv_ln",
    )(u, u, u_meta_halo, conv_w, conv_b, g, b)


def _mix_kernel(x_ref, at_ref, cv_ref, gin_ref, bin_ref, woa_ref, woc_ref, g1_ref, b1_ref,
                wrh_ref, wrl_ref, h1_ref, h1r_ref, lg_ref):
    h = _layer_norm(x_ref[...], gin_ref[...], bin_ref[...])
    mix = (jnp.dot(at_ref[...], woa_ref[...], preferred_element_type=F32)
           + jnp.dot(cv_ref[...], woc_ref[...], preferred_element_type=F32))
    h1 = _layer_norm(ALPHA * h + mix, g1_ref[...], b1_ref[...])
    h1_ref[...] = h1
    _store_packed(h1r_ref, 0, h1.shape[0], _pack_rows(h1[:, :HALF], h1[:, HALF:]))
    hh = h1.astype(BF16)
    hl = (h1 - hh.astype(F32)).astype(BF16)
    nt = (((1,), (1,)), ((), ()))
    lg = lax.dot_general(wrh_ref[...], hh, nt, preferred_element_type=F32)
    lg = lg + lax.dot_general(wrh_ref[...], hl, nt, preferred_element_type=F32)
    lg = lg + lax.dot_general(wrl_ref[...], hh, nt, preferred_element_type=F32)
    lg_ref[...] = lg


def _mix_call(x2d, attn, conv, gin, bin_, woa, woc, g1, b1, wrh, wrl):
    t = x2d.shape[0]
    tq = TQ_MIX
    row = lambda i: (i, 0)
    fix = lambda i: (0, 0)
    return pl.pallas_call(
        _mix_kernel,
        grid=(t // tq,),
        in_specs=[pl.BlockSpec((tq, D_MODEL), row), pl.BlockSpec((tq, ATTN_W), row), pl.BlockSpec((tq, CONV_CH), row),
                  pl.BlockSpec((1, D_MODEL), fix), pl.BlockSpec((1, D_MODEL), fix),
                  pl.BlockSpec((ATTN_W, D_MODEL), fix), pl.BlockSpec((CONV_CH, D_MODEL), fix),
                  pl.BlockSpec((1, D_MODEL), fix), pl.BlockSpec((1, D_MODEL), fix),
                  pl.BlockSpec((N_EXPERTS, D_MODEL), fix), pl.BlockSpec((N_EXPERTS, D_MODEL), fix)],
        out_specs=[pl.BlockSpec((tq, D_MODEL), row), pl.BlockSpec((tq * ROW_CHUNKS, LANES), row),
                   pl.BlockSpec((N_EXPERTS, tq), lambda i: (0, i))],
        out_shape=[SDS((t, D_MODEL), F32), SDS((t * ROW_CHUNKS, LANES), U32), SDS((N_EXPERTS, t), F32)],
        compiler_params=_cparams("arbitrary"),
        name="mix_ln1",
    )(x2d, attn, conv, gin, bin_, woa, woc, g1, b1, wrh, wrl)


def _first_argmax(x, rows, nrows):
    m = jnp.max(x, axis=0, keepdims=True)
    idx = jnp.min(jnp.where(x == m, rows, nrows), axis=0, keepdims=True)
    return m, idx


def _route_kernel(lg_ref, rb_ref, idx_ref, wts_ref, rank_ref, cnt_ref, carry_ref):
    tn = lg_ref.shape[1]

    @pl.when(pl.program_id(0) == 0)
    def _():
        carry_ref[...] = jnp.zeros_like(carry_ref)

    scores = _sigmoid(lg_ref[...])
    choice = scores + rb_ref[...]
    rows = lax.broadcasted_iota(I32, (N_EXPERTS, tn), 0)
    rows_g = lax.broadcasted_iota(I32, (GROUP_SIZE, tn), 0)
    rows_8 = lax.broadcasted_iota(I32, (N_GROUPS, tn), 0)

    gs = []
    for g in range(N_GROUPS):
        xg = choice[g * GROUP_SIZE:(g + 1) * GROUP_SIZE, :]
        m1, i1 = _first_argmax(xg, rows_g, GROUP_SIZE)
        m2 = jnp.max(jnp.where(rows_g == i1, -jnp.inf, xg), axis=0, keepdims=True)
        gs.append(m1 + m2)
    gsc = jnp.concatenate(gs, axis=0)
    gsel = jnp.zeros((N_GROUPS, tn), F32)
    for _ in range(TOPK_GROUPS):
        _, gi = _first_argmax(gsc, rows_8, N_GROUPS)
        hit = rows_8 == gi
        gsel = jnp.where(hit, 1.0, gsel)
        gsc = jnp.where(hit, -jnp.inf, gsc)
    emask = jnp.concatenate(
        [jnp.broadcast_to(gsel[g:g + 1, :], (GROUP_SIZE, tn)) for g in range(N_GROUPS)], axis=0)
    masked = jnp.where(emask > 0.5, choice, NEG)

    sel_all = jnp.zeros((N_EXPERTS, tn), F32)
    hits, idxs, ws = [], [], []
    for _ in range(TOP_K):
        _, ii = _first_argmax(masked, rows, N_EXPERTS)
        hit = rows == ii
        hits.append(hit)
        idxs.append(ii)
        ws.append(jnp.sum(jnp.where(hit, scores, 0.0), axis=0, keepdims=True))
        sel_all = jnp.where(hit, 1.0, sel_all)
        masked = jnp.where(hit, -jnp.inf, masked)
    wsum = ws[0]
    for w in ws[1:]:
        wsum = wsum + w
    idx_ref[...] = jnp.concatenate(idxs, axis=0)
    wts_ref[...] = jnp.concatenate([w / wsum * ROUTED_SCALE for w in ws], axis=0)

    r_i = lax.broadcasted_iota(I32, (tn, tn), 0)
    c_i = lax.broadcasted_iota(I32, (tn, tn), 1)
    upper = jnp.where(r_i < c_i, 1.0, 0.0).astype(BF16)
    sel_b = sel_all.astype(BF16)
    carry = carry_ref[...]
    before = jnp.dot(sel_b, upper, preferred_element_type=F32)
    before = before + jnp.concatenate([carry] * (tn // 128), axis=1)
    rank_ref[...] = jnp.concatenate(
        [jnp.sum(jnp.where(h, before, 0.0), axis=0, keepdims=True) for h in hits], axis=0).astype(I32)
    carry = carry + jnp.dot(sel_b, jnp.ones((tn, 128), BF16), preferred_element_type=F32)
    carry_ref[...] = carry
    cnt_ref[...] = carry.astype(I32)


def _route_call(lg, rbias):
    t = lg.shape[1]
    tn = TN_ROUTE
    col = lambda i: (0, i)
    return pl.pallas_call(
        _route_kernel,
        grid=(t // tn,),
        in_specs=[pl.BlockSpec((N_EXPERTS, tn), col), pl.BlockSpec((N_EXPERTS, 1), lambda i: (0, 0))],
        out_specs=[pl.BlockSpec((TOP_K, tn), col), pl.BlockSpec((TOP_K, tn), col), pl.BlockSpec((TOP_K, tn), col),
                   pl.BlockSpec((N_EXPERTS, 128), lambda i: (0, 0))],
        out_shape=[SDS((TOP_K, t), I32), SDS((TOP_K, t), F32), SDS((TOP_K, t), I32), SDS((N_EXPERTS, 128), I32)],
        scratch_shapes=[pltpu.VMEM((N_EXPERTS, 128), F32)],
        compiler_params=_cparams("arbitrary"),
        name="route",
    )(lg, rbias)


def _dest_kernel(idx_ref, rank_ref, offs_ref, dest_ref):
    tn = idx_ref.shape[1]
    rows = lax.broadcasted_iota(I32, (N_EXPERTS, tn), 0)
    offs = offs_ref[...]
    out = []
    for kk in range(TOP_K):
        hit = rows == idx_ref[kk:kk + 1, :]
        out.append(jnp.sum(jnp.where(hit, offs, 0.0), axis=0, keepdims=True))
    dest_ref[...] = jnp.concatenate(out, axis=0).astype(I32) + rank_ref[...]


def _dest_call(idx, rank, offs_col):
    t = idx.shape[1]
    tn = TN_ROUTE
    col = lambda i: (0, i)
    return pl.pallas_call(
        _dest_kernel,
        grid=(t // tn,),
        in_specs=[pl.BlockSpec((TOP_K, tn), col), pl.BlockSpec((TOP_K, tn), col),
                  pl.BlockSpec((N_EXPERTS, 1), lambda i: (0, 0))],
        out_specs=pl.BlockSpec((TOP_K, tn), col),
        out_shape=SDS((TOP_K, t), I32),
        compiler_params=_cparams("arbitrary"),
        name="dest",
    )(idx, rank, offs_col)


DISP_UNROLL = 2


def _dispatch_kernel(dest_ref, h_ref, xs_ref, sem):
    tn = dest_ref.shape[1]

    def issue(j, c):
        slots = [[dest_ref[kk, j * DISP_UNROLL + r] for kk in range(TOP_K)] for r in range(DISP_UNROLL)]
        for r in range(DISP_UNROLL):
            src = _packed_row(h_ref, j * DISP_UNROLL + r)
            for kk in range(TOP_K):
                pltpu.make_async_copy(src, _packed_row(xs_ref, slots[r][kk]), sem).start()
        return c

    lax.fori_loop(0, tn // DISP_UNROLL, issue, 0)
    n = TOP_K * tn * ROW_CHUNKS
    pltpu.make_async_copy(xs_ref.at[pl.ds(0, n), :], xs_ref.at[pl.ds(0, n), :], sem).wait()


def _dispatch_call(dest, h1rows, n_rows):
    t = dest.shape[1]
    tn = TN_DISP
    return pl.pallas_call(
        _dispatch_kernel,
        grid=(t // tn,),
        in_specs=[pl.BlockSpec((TOP_K, tn), lambda i: (0, i), memory_space=pltpu.SMEM),
                  pl.BlockSpec((tn * ROW_CHUNKS, LANES), lambda i: (i, 0))],
        out_specs=pl.BlockSpec(memory_space=pl.ANY),
        out_shape=SDS((n_rows * ROW_CHUNKS, LANES), U32),
        scratch_shapes=[pltpu.SemaphoreType.DMA(())],
        compiler_params=_cparams("arbitrary"),
        name="dispatch",
    )(dest, h1rows)


def _expert_kernel(te_ref, tr_ref, nt_ref, x_ref, wg_ref, wu_ref, wd_ref, y_ref):
    i = pl.program_id(0)

    @pl.when(i < nt_ref[0])
    def _():
        tm = x_ref.shape[0] // ROW_CHUNKS
        x = _load_packed_bf16(x_ref, 0, tm)
        g = jnp.dot(x, wg_ref[0].astype(BF16), preferred_element_type=F32)
        u = jnp.dot(x, wu_ref[0].astype(BF16), preferred_element_type=F32)
        live = lax.broadcasted_iota(I32, (tm, EXPERT_FF), 0) < tr_ref[i]
        hid = jnp.where(live, g * _sigmoid(g) * u, 0.0).astype(BF16)
        y = jnp.dot(hid, wd_ref[0].astype(BF16), preferred_element_type=F32)
        _store_packed(y_ref, 0, tm, _pack_rows(y[:, :HALF], y[:, HALF:]))


def _expert_call(tile_expert, tile_rows, n_valid, xs, w_gate, w_up, w_down):
    n_rows = xs.shape[0] // ROW_CHUNKS
    tm = TM_EXP
    rows_map = lambda i, te, tr, nt: (jnp.minimum(i, nt[0] - 1), 0)
    w_map = lambda i, te, tr, nt: (te[i], 0, 0)
    return pl.pallas_call(
        _expert_kernel,
        grid_spec=pltpu.PrefetchScalarGridSpec(
            num_scalar_prefetch=3,
            grid=(n_rows // tm,),
            in_specs=[pl.BlockSpec((tm * ROW_CHUNKS, LANES), rows_map),
                      pl.BlockSpec((1, D_MODEL, EXPERT_FF), w_map),
                      pl.BlockSpec((1, D_MODEL, EXPERT_FF), w_map),
                      pl.BlockSpec((1, EXPERT_FF, D_MODEL), w_map)],
            out_specs=pl.BlockSpec((tm * ROW_CHUNKS, LANES), rows_map),
        ),
        out_shape=SDS((n_rows * ROW_CHUNKS, LANES), U32),
        compiler_params=_cparams("arbitrary"),
        name="experts",
    )(tile_expert, tile_rows, n_valid, xs, w_gate, w_up, w_down)


def _combine_kernel(dest_ref, dest_next_ref, wts_ref, h1_ref, ys_ref, wsg_ref, wsu_ref, wsd_ref, g2_ref, b2_ref,
                    o_ref, ybuf_ref, sem):
    i = pl.program_id(0)
    tn = h1_ref.shape[0]
    rows = TOP_K * tn
    slot = i % 2

    def gather(d_ref, s):
        base = s * rows

        def issue(t, c):
            srcs = [d_ref[kk, t] for kk in range(TOP_K)]
            for kk in range(TOP_K):
                pltpu.make_async_copy(_packed_row(ys_ref, srcs[kk]), _packed_row(ybuf_ref, base + kk * tn + t),
                                      sem.at[s]).start()
            return c

        lax.fori_loop(0, tn, issue, 0)

    @pl.when(i == 0)
    def _():
        gather(dest_ref, 0)

    @pl.when(i + 1 < pl.num_programs(0))
    def _():
        gather(dest_next_ref, 1 - slot)

    h1 = h1_ref[...]
    hb = h1.astype(BF16)
    sg = jnp.dot(hb, wsg_ref[...], preferred_element_type=F32)
    su = jnp.dot(hb, wsu_ref[...], preferred_element_type=F32)
    ff = jnp.dot((sg * _sigmoid(sg) * su).astype(BF16), wsd_ref[...], preferred_element_type=F32)

    base = slot * rows
    pltpu.make_async_copy(
        ys_ref.at[pl.ds(0, rows * ROW_CHUNKS), :],
        ybuf_ref.at[pl.ds(pl.multiple_of(base * ROW_CHUNKS, rows * ROW_CHUNKS), rows * ROW_CHUNKS), :],
        sem.at[slot]).wait()

    eye = lax.broadcasted_iota(I32, (tn, tn), 0) == lax.broadcasted_iota(I32, (tn, tn), 1)
    w = wts_ref[...]
    for kk in range(TOP_K):
        wk = jnp.where(eye, w[kk:kk + 1, :], 0.0)
        wh = wk.astype(BF16)
        wl = (wk - wh.astype(F32)).astype(BF16)
        yk = _load_packed_bf16(ybuf_ref, pl.multiple_of((base + kk * tn) * ROW_CHUNKS, tn * ROW_CHUNKS), tn)
        ff = ff + jnp.dot(wh, yk, preferred_element_type=F32)
        ff = ff + jnp.dot(wl, yk, preferred_element_type=F32)
    o_ref[...] = _layer_norm(ALPHA * h1 + ff, g2_ref[...], b2_ref[...])


def _combine_call(dest, wts, h1, ys, wsg, wsu, wsd, g2, b2):
    t = h1.shape[0]
    tn = TN_COMB
    nsteps = t // tn
    col = lambda i: (0, i)
    col_next = lambda i: (0, jnp.minimum(i + 1, nsteps - 1))
    row = lambda i: (i, 0)
    fix = lambda i: (0, 0)
    return pl.pallas_call(
        _combine_kernel,
        grid=(nsteps,),
        in_specs=[pl.BlockSpec((TOP_K, tn), col, memory_space=pltpu.SMEM),
                  pl.BlockSpec((TOP_K, tn), col_next, memory_space=pltpu.SMEM),
                  pl.BlockSpec((TOP_K, tn), col),
                  pl.BlockSpec((tn, D_MODEL), row),
                  pl.BlockSpec(memory_space=pl.ANY),
                  pl.BlockSpec((D_MODEL, SHARED_FF), fix), pl.BlockSpec((D_MODEL, SHARED_FF), fix),
                  pl.BlockSpec((SHARED_FF, D_MODEL), fix),
                  pl.BlockSpec((1, D_MODEL), fix), pl.BlockSpec((1, D_MODEL), fix)],
        out_specs=pl.BlockSpec((tn, D_MODEL), row),
        out_shape=SDS((t, D_MODEL), F32),
        scratch_shapes=[pltpu.VMEM((2 * TOP_K * tn * ROW_CHUNKS, LANES), U32), pltpu.SemaphoreType.DMA((2,))],
        compiler_params=_cparams("arbitrary"),
        name="combine_ln2",
    )(dest, dest, wts, h1, ys, wsg, wsu, wsd, g2, b2)


def kernel(x, meta_tokens, ln_in_g, ln_in_b, rel_bias, w_in, conv_w, conv_b, conv_ln_g, conv_ln_b, sinks,
           w_out, ln1_g, ln1_b, w_router, router_bias, w_gate, w_up, w_down, ws_gate, ws_up, ws_down,
           ln2_g, ln2_b):
    nbatch, seq, d = x.shape
    assert d == D_MODEL and seq % TQ_PROJ == 0 and w_in.shape[0] == DEPTH
    t = nbatch * seq
    x2d = x.reshape(t, D_MODEL)
    vec = lambda a: a.reshape(1, -1).astype(F32)
    gin, bin_ = vec(ln_in_g), vec(ln_in_b)
    w_in_b = w_in[0].astype(BF16)

    q, k, v, u = _proj_call(x2d, gin, bin_, w_in_b, TQ_PROJ)
    meta_blk = jnp.concatenate([jnp.zeros((PAD_FRONT, D_MODEL), F32), meta_tokens.astype(F32)], axis=0)
    _, k_meta, v_meta, u_meta = _proj_call(meta_blk, gin, bin_, w_in_b, BLOCK)

    attn = _attn_call(q, k, v, k_meta, v_meta, _rel_bias_table(rel_bias), sinks[0].astype(F32),
                      nbatch, seq // BLOCK)

    u_halo = jnp.concatenate([jnp.zeros((CONV_HALO - N_META, CONV_CH), F32), u_meta[PAD_FRONT:]], axis=0)
    conv = _conv_call(u, u_halo, conv_w[0].astype(F32), vec(conv_b[0]), vec(conv_ln_g[0]), vec(conv_ln_b[0]),
                      nbatch, seq)

    w_out_b = w_out[0].astype(BF16)
    wr_t = w_router[0].astype(F32).T
    wr_hi = wr_t.astype(BF16)
    wr_lo = (wr_t - wr_hi.astype(F32)).astype(BF16)
    h1, h1rows, logits = _mix_call(x2d, attn, conv, gin, bin_, w_out_b[:ATTN_W], w_out_b[ATTN_W:],
                                   vec(ln1_g[0]), vec(ln1_b[0]), wr_hi, wr_lo)

    idx, wts, rank, cnt = _route_call(logits, router_bias[0].astype(F32).reshape(N_EXPERTS, 1))

    tm = TM_EXP
    n_tiles = (t * TOP_K) // tm + N_EXPERTS
    counts = cnt[:, 0]
    tiles_e = (counts + tm - 1) // tm
    tile_end = jnp.cumsum(tiles_e)
    offs = ((tile_end - tiles_e) * tm).astype(I32)
    tile_id = jnp.arange(n_tiles, dtype=I32)
    tile_expert = jnp.minimum(jnp.sum(tile_id[:, None] >= tile_end[None, :], axis=1), N_EXPERTS - 1).astype(I32)
    lo = jnp.maximum(tile_id[:, None] * tm, offs[None, :])
    hi = jnp.minimum((tile_id[:, None] + 1) * tm, (offs + counts)[None, :])
    tile_rows = jnp.sum(jnp.clip(hi - lo, 0, tm), axis=1).astype(I32)

    n_valid = tile_end[-1:].astype(I32)

    dest = _dest_call(idx, rank, offs.astype(F32).reshape(N_EXPERTS, 1))
    xs = _dispatch_call(dest, h1rows, n_tiles * tm)
    ys = _expert_call(tile_expert, tile_rows, n_valid, xs, w_gate[0], w_up[0], w_down[0])
    out = _combine_call(dest, wts, h1, ys, ws_gate[0].astype(BF16), ws_up[0].astype(BF16),
                        ws_down[0].astype(BF16), vec(ln2_g[0]), vec(ln2_b[0]))
    return out.reshape(nbatch, seq, D_MODEL)
```

```python
import functools
import math

import numpy as np
import jax
import jax.numpy as jnp
from jax import lax
from jax.experimental import pallas as pl
from jax.experimental.pallas import tpu as pltpu

F32 = jnp.float32
BF16 = jnp.bfloat16
I32 = jnp.int32
U32 = jnp.uint32
SDS = jax.ShapeDtypeStruct

D_MODEL = 1024
HALF = D_MODEL // 2
LANES = 128
SUBLANES = 8
ROW_CHUNKS = HALF // LANES
N_META = 16
HEAD_DIM = 64
N_Q_HEADS = 8
N_KV_HEADS = 2
GQA_GROUP = N_Q_HEADS // N_KV_HEADS
ATTN_W = N_Q_HEADS * HEAD_DIM
KV_W = N_KV_HEADS * HEAD_DIM
WINDOW = 128
BLOCK = 128
CONV_CH = D_MODEL - ATTN_W
CONV_K = 31
IN_W = ATTN_W + 2 * KV_W + 2 * CONV_CH
NUM_BUCKETS = 32
MAX_EXACT = NUM_BUCKETS // 2
REL_MAX_DIST = 128
N_EXPERTS = 256
TOP_K = 8
N_GROUPS = 8
GROUP_SIZE = N_EXPERTS // N_GROUPS
TOPK_GROUPS = 4
EXPERT_FF = 256
SHARED_FF = 256
ROUTED_SCALE = 2.5
DEPTH = 1
ALPHA = (2.0 * DEPTH) ** 0.25
LN_EPS = 1e-5
NEG = -1e30
PAD_FRONT = (-N_META) % BLOCK

VMEM_LIMIT = 48 * 1024 * 1024

TQ_PROJ = 512
T_CONV = 256
CONV_HALO = 32
R_CONV = 64
TQ_MIX = 256
TN_ROUTE = 256
TN_DISP = 256
TM_EXP = 256
TN_COMB = 128


def _cparams(*sem):
    return pltpu.CompilerParams(dimension_semantics=sem, vmem_limit_bytes=VMEM_LIMIT)


def _layer_norm(x, g, b):
    mu = jnp.mean(x, axis=-1, keepdims=True)
    xc = x - mu
    var = jnp.mean(xc * xc, axis=-1, keepdims=True)
    return xc * lax.rsqrt(var + LN_EPS) * g + b


def _sigmoid(x):
    return 1.0 / (1.0 + jnp.exp(-x))


def _pack_rows(lo_half, hi_half):
    lo = lax.bitcast_convert_type(lo_half.astype(BF16).astype(F32), U32)
    hi = lax.bitcast_convert_type(hi_half.astype(BF16).astype(F32), U32)
    return lax.shift_right_logical(lo, jnp.uint32(16)) | hi


def _unpack_rows(p):
    lo = lax.bitcast_convert_type(lax.shift_left(p, jnp.uint32(16)), F32)
    hi = lax.bitcast_convert_type(p & jnp.uint32(0xFFFF0000), F32)
    return lo, hi


def _store_packed(ref, start, n, packed):
    for j in range(ROW_CHUNKS):
        ref[pl.ds(start + j, n, stride=ROW_CHUNKS), :] = packed[:, j * LANES:(j + 1) * LANES]


def _load_packed_bf16(ref, start, n):
    halves = [_unpack_rows(ref[pl.ds(start + j, n, stride=ROW_CHUNKS), :]) for j in range(ROW_CHUNKS)]
    return jnp.concatenate([h[0] for h in halves] + [h[1] for h in halves], axis=1).astype(BF16)


def _packed_row(ref, token):
    return ref.at[pl.ds(pl.multiple_of(token * ROW_CHUNKS, ROW_CHUNKS), ROW_CHUNKS), :]


def _proj_kernel(x_ref, g_ref, b_ref, w_ref, q_ref, k_ref, v_ref, u_ref):
    h = _layer_norm(x_ref[...], g_ref[...], b_ref[...])
    p = jnp.dot(h.astype(BF16), w_ref[...], preferred_element_type=F32)
    q_ref[...] = (p[:, :ATTN_W] * (HEAD_DIM ** -0.5)).astype(BF16)
    k_ref[...] = p[:, ATTN_W:ATTN_W + KV_W].astype(BF16)
    v_ref[...] = p[:, ATTN_W + KV_W:ATTN_W + 2 * KV_W].astype(BF16)
    a = p[:, ATTN_W + 2 * KV_W:ATTN_W + 2 * KV_W + CONV_CH]
    gate = p[:, ATTN_W + 2 * KV_W + CONV_CH:]
    u_ref[...] = a * _sigmoid(gate)


def _proj_call(x2d, gin, bin_, w_in_b, tq):
    t = x2d.shape[0]
    row = lambda i: (i, 0)
    fix = lambda i: (0, 0)
    return pl.pallas_call(
        _proj_kernel,
        grid=(t // tq,),
        in_specs=[pl.BlockSpec((tq, D_MODEL), row), pl.BlockSpec((1, D_MODEL), fix),
                  pl.BlockSpec((1, D_MODEL), fix), pl.BlockSpec((D_MODEL, IN_W), fix)],
        out_specs=[pl.BlockSpec((tq, ATTN_W), row), pl.BlockSpec((tq, KV_W), row),
                   pl.BlockSpec((tq, KV_W), row), pl.BlockSpec((tq, CONV_CH), row)],
        out_shape=[SDS((t, ATTN_W), BF16), SDS((t, KV_W), BF16), SDS((t, KV_W), BF16), SDS((t, CONV_CH), F32)],
        compiler_params=_cparams("arbitrary"),
        name="ln_in_proj",
    )(x2d, gin, bin_, w_in_b)


def _attn_kernel(sinks_ref, q_ref, kc_ref, kp_ref, vc_ref, vp_ref, km_ref, vm_ref, bias_ref, o_ref):
    first = pl.program_id(1) == 0
    kp = jnp.where(first, km_ref[...], kp_ref[...])
    vp = jnp.where(first, vm_ref[...], vp_ref[...])
    k = jnp.concatenate([kp, kc_ref[...]], axis=0)
    v = jnp.concatenate([vp, vc_ref[...]], axis=0)
    col = lax.broadcasted_iota(I32, (BLOCK, 2 * BLOCK), 1)
    pad_bias = jnp.where(jnp.logical_and(first, col < PAD_FRONT), NEG, 0.0).astype(F32)
    q = q_ref[...]
    outs = []
    for h in range(N_Q_HEADS):
        g = h // GQA_GROUP
        qh = q[:, h * HEAD_DIM:(h + 1) * HEAD_DIM]
        kg = k[:, g * HEAD_DIM:(g + 1) * HEAD_DIM]
        vg = v[:, g * HEAD_DIM:(g + 1) * HEAD_DIM]
        s = lax.dot_general(qh, kg, (((1,), (1,)), ((), ())), preferred_element_type=F32)
        s = s + bias_ref[h] + pad_bias
        sink = sinks_ref[h]
        m = jnp.maximum(jnp.max(s, axis=-1, keepdims=True), sink)
        p = jnp.exp(s - m)
        den = jnp.sum(p, axis=-1, keepdims=True) + jnp.exp(sink - m)
        o = jnp.dot(p.astype(BF16), vg, preferred_element_type=F32)
        outs.append(o / den)
    o_ref[...] = jnp.concatenate(outs, axis=1).astype(BF16)


def _attn_call(q, k, v, k_meta, v_meta, bias, sinks, nbatch, nblk):
    t = q.shape[0]
    cur = lambda b, j: (b * nblk + j, 0)
    prev = lambda b, j: (jnp.maximum(b * nblk + j - 1, 0), 0)
    fix2 = lambda b, j: (0, 0)
    return pl.pallas_call(
        _attn_kernel,
        grid=(nbatch, nblk),
        in_specs=[pl.BlockSpec(memory_space=pltpu.SMEM),
                  pl.BlockSpec((BLOCK, ATTN_W), cur),
                  pl.BlockSpec((BLOCK, KV_W), cur), pl.BlockSpec((BLOCK, KV_W), prev),
                  pl.BlockSpec((BLOCK, KV_W), cur), pl.BlockSpec((BLOCK, KV_W), prev),
                  pl.BlockSpec((BLOCK, KV_W), fix2), pl.BlockSpec((BLOCK, KV_W), fix2),
                  pl.BlockSpec((N_Q_HEADS, BLOCK, 2 * BLOCK), lambda b, j: (0, 0, 0))],
        out_specs=pl.BlockSpec((BLOCK, ATTN_W), cur),
        out_shape=SDS((t, ATTN_W), BF16),
        compiler_params=_cparams("arbitrary", "arbitrary"),
        name="swa_attn",
    )(sinks, q, k, k, v, v, k_meta, v_meta, bias)


def _rel_bias_table(rel_bias):
    qi = np.arange(BLOCK, dtype=np.int32)[:, None]
    kj = np.arange(2 * BLOCK, dtype=np.int32)[None, :]
    dist = BLOCK + qi - kj
    dc = np.clip(dist, 0, WINDOW - 1)
    nf = np.maximum(dc, 1).astype(np.float32)
    large = MAX_EXACT + (np.log(nf / np.float32(MAX_EXACT)) / np.float32(math.log(REL_MAX_DIST / MAX_EXACT))
                         * np.float32(NUM_BUCKETS - MAX_EXACT)).astype(np.int32)
    large = np.minimum(large, NUM_BUCKETS - 1)
    bucket = np.where(dc < MAX_EXACT, dc, large)
    in_window = (dist >= 0) & (dist < WINDOW)
    onehot = (bucket.reshape(-1, 1) == np.arange(NUM_BUCKETS)[None, :]).astype(np.float32)
    bias = jnp.dot(jnp.asarray(onehot), rel_bias.astype(F32), precision=lax.Precision.HIGHEST)
    bias = jnp.transpose(bias.reshape(BLOCK, 2 * BLOCK, N_Q_HEADS), (2, 0, 1))
    return jnp.where(in_window[None], bias, NEG)


def _conv_kernel(uc_ref, up_ref, um_ref, w_ref, cb_ref, g_ref, b_ref, o_ref, s_ref, sh_ref):
    first = pl.program_id(1) == 0
    s_ref[0:CONV_HALO, :] = jnp.where(first, um_ref[...], up_ref[...])
    s_ref[CONV_HALO:CONV_HALO + T_CONV, :] = uc_ref[...]
    off = CONV_HALO - (CONV_K - 1)
    span = sh_ref.shape[1]
    for c in range(0, T_CONV, R_CONV):
        for p in range(1, SUBLANES):
            sh_ref[p] = s_ref[c + p:c + p + span, :]
        acc = jnp.zeros((R_CONV, CONV_CH), F32) + cb_ref[...]
        for kk in range(CONV_K):
            p, a = (off + kk) % SUBLANES, (off + kk) // SUBLANES * SUBLANES
            if p == 0:
                win = s_ref[c + a:c + a + R_CONV, :]
            else:
                win = sh_ref[p, a:a + R_CONV, :]
            acc = acc + win * w_ref[kk:kk + 1, :]
        y = _layer_norm(acc, g_ref[...], b_ref[...])
        o_ref[c:c + R_CONV, :] = (y * _sigmoid(y)).astype(BF16)


def _conv_call(u, u_meta_halo, conv_w, conv_b, g, b, nbatch, seq):
    t = u.shape[0]
    nj = seq // T_CONV
    per = T_CONV // CONV_HALO
    cur = lambda bb, j: (bb * nj + j, 0)
    prev = lambda bb, j: (jnp.maximum((bb * nj + j) * per - 1, 0), 0)
    fix = lambda bb, j: (0, 0)
    return pl.pallas_call(
        _conv_kernel,
        grid=(nbatch, nj),
        in_specs=[pl.BlockSpec((T_CONV, CONV_CH), cur), pl.BlockSpec((CONV_HALO, CONV_CH), prev),
                  pl.BlockSpec((CONV_HALO, CONV_CH), fix), pl.BlockSpec((CONV_K, CONV_CH), fix),
                  pl.BlockSpec((1, CONV_CH), fix), pl.BlockSpec((1, CONV_CH), fix), pl.BlockSpec((1, CONV_CH), fix)],
        out_specs=pl.BlockSpec((T_CONV, CONV_CH), cur),
        out_shape=SDS((t, CONV_CH), BF16),
        scratch_shapes=[pltpu.VMEM((CONV_HALO + T_CONV, CONV_CH), F32),
                        pltpu.VMEM((SUBLANES, R_CONV + CONV_HALO - SUBLANES, CONV_CH), F32)],
        compiler_params=_cparams("arbitrary", "arbitrary"),
        name="conv_ln",
    )(u, u, u_meta_halo, conv_w, conv_b, g, b)


def _mix_kernel(x_ref, at_ref, cv_ref, gin_ref, bin_ref, woa_ref, woc_ref, g1_ref, b1_ref,
                wrh_ref, wrl_ref, h1_ref, h1r_ref, lg_ref):
    h = _layer_norm(x_ref[...], gin_ref[...], bin_ref[...])
    mix = (jnp.dot(at_ref[...], woa_ref[...], preferred_element_type=F32)
           + jnp.dot(cv_ref[...], woc_ref[...], preferred_element_type=F32))
    h1 = _layer_norm(ALPHA * h + mix, g1_ref[...], b1_ref[...])
    h1_ref[...] = h1
    _store_packed(h1r_ref, 0, h1.shape[0], _pack_rows(h1[:, :HALF], h1[:, HALF:]))
    hh = h1.astype(BF16)
    hl = (h1 - hh.astype(F32)).astype(BF16)
    nt = (((1,), (1,)), ((), ()))
    lg = lax.dot_general(wrh_ref[...], hh, nt, preferred_element_type=F32)
    lg = lg + lax.dot_general(wrh_ref[...], hl, nt, preferred_element_type=F32)
    lg = lg + lax.dot_general(wrl_ref[...], hh, nt, preferred_element_type=F32)
    lg_ref[...] = lg


def _mix_call(x2d, attn, conv, gin, bin_, woa, woc, g1, b1, wrh, wrl):
    t = x2d.shape[0]
    tq = TQ_MIX
    row = lambda i: (i, 0)
    fix = lambda i: (0, 0)
    return pl.pallas_call(
        _mix_kernel,
        grid=(t // tq,),
        in_specs=[pl.BlockSpec((tq, D_MODEL), row), pl.BlockSpec((tq, ATTN_W), row), pl.BlockSpec((tq, CONV_CH), row),
                  pl.BlockSpec((1, D_MODEL), fix), pl.BlockSpec((1, D_MODEL), fix),
                  pl.BlockSpec((ATTN_W, D_MODEL), fix), pl.BlockSpec((CONV_CH, D_MODEL), fix),
                  pl.BlockSpec((1, D_MODEL), fix), pl.BlockSpec((1, D_MODEL), fix),
                  pl.BlockSpec((N_EXPERTS, D_MODEL), fix), pl.BlockSpec((N_EXPERTS, D_MODEL), fix)],
        out_specs=[pl.BlockSpec((tq, D_MODEL), row), pl.BlockSpec((tq * ROW_CHUNKS, LANES), row),
                   pl.BlockSpec((N_EXPERTS, tq), lambda i: (0, i))],
        out_shape=[SDS((t, D_MODEL), F32), SDS((t * ROW_CHUNKS, LANES), U32), SDS((N_EXPERTS, t), F32)],
        compiler_params=_cparams("arbitrary"),
        name="mix_ln1",
    )(x2d, attn, conv, gin, bin_, woa, woc, g1, b1, wrh, wrl)


def _first_argmax(x, rows, nrows):
    m = jnp.max(x, axis=0, keepdims=True)
    idx = jnp.min(jnp.where(x == m, rows, nrows), axis=0, keepdims=True)
    return m, idx


def _route_kernel(lg_ref, rb_ref, idx_ref, wts_ref, rank_ref, cnt_ref, carry_ref):
    tn = lg_ref.shape[1]

    @pl.when(pl.program_id(0) == 0)
    def _():
        carry_ref[...] = jnp.zeros_like(carry_ref)

    scores = _sigmoid(lg_ref[...])
    choice = scores + rb_ref[...]
    rows = lax.broadcasted_iota(I32, (N_EXPERTS, tn), 0)
    rows_g = lax.broadcasted_iota(I32, (GROUP_SIZE, tn), 0)
    rows_8 = lax.broadcasted_iota(I32, (N_GROUPS, tn), 0)

    gs = []
    for g in range(N_GROUPS):
        xg = choice[g * GROUP_SIZE:(g + 1) * GROUP_SIZE, :]
        m1, i1 = _first_argmax(xg, rows_g, GROUP_SIZE)
        m2 = jnp.max(jnp.where(rows_g == i1, -jnp.inf, xg), axis=0, keepdims=True)
        gs.append(m1 + m2)
    gsc = jnp.concatenate(gs, axis=0)
    gsel = jnp.zeros((N_GROUPS, tn), F32)
    for _ in range(TOPK_GROUPS):
        _, gi = _first_argmax(gsc, rows_8, N_GROUPS)
        hit = rows_8 == gi
        gsel = jnp.where(hit, 1.0, gsel)
        gsc = jnp.where(hit, -jnp.inf, gsc)
    emask = jnp.concatenate(
        [jnp.broadcast_to(gsel[g:g + 1, :], (GROUP_SIZE, tn)) for g in range(N_GROUPS)], axis=0)
    masked = jnp.where(emask > 0.5, choice, NEG)

    sel_all = jnp.zeros((N_EXPERTS, tn), F32)
    hits, idxs, ws = [], [], []
    for _ in range(TOP_K):
        _, ii = _first_argmax(masked, rows, N_EXPERTS)
        hit = rows == ii
        hits.append(hit)
        idxs.append(ii)
        ws.append(jnp.sum(jnp.where(hit, scores, 0.0), axis=0, keepdims=True))
        sel_all = jnp.where(hit, 1.0, sel_all)
        masked = jnp.where(hit, -jnp.inf, masked)
    wsum = ws[0]
    for w in ws[1:]:
        wsum = wsum + w
    idx_ref[...] = jnp.concatenate(idxs, axis=0)
    wts_ref[...] = jnp.concatenate([w / wsum * ROUTED_SCALE for w in ws], axis=0)

    r_i = lax.broadcasted_iota(I32, (tn, tn), 0)
    c_i = lax.broadcasted_iota(I32, (tn, tn), 1)
    upper = jnp.where(r_i < c_i, 1.0, 0.0).astype(BF16)
    sel_b = sel_all.astype(BF16)
    carry = carry_ref[...]
    before = jnp.dot(sel_b, upper, preferred_element_type=F32)
    before = before + jnp.concatenate([carry] * (tn // 128), axis=1)
    rank_ref[...] = jnp.concatenate(
        [jnp.sum(jnp.where(h, before, 0.0), axis=0, keepdims=True) for h in hits], axis=0).astype(I32)
    carry = carry + jnp.dot(sel_b, jnp.ones((tn, 128), BF16), preferred_element_type=F32)
    carry_ref[...] = carry
    cnt_ref[...] = carry.astype(I32)


def _route_call(lg, rbias):
    t = lg.shape[1]
    tn = TN_ROUTE
    col = lambda i: (0, i)
    return pl.pallas_call(
        _route_kernel,
        grid=(t // tn,),
        in_specs=[pl.BlockSpec((N_EXPERTS, tn), col), pl.BlockSpec((N_EXPERTS, 1), lambda i: (0, 0))],
        out_specs=[pl.BlockSpec((TOP_K, tn), col), pl.BlockSpec((TOP_K, tn), col), pl.BlockSpec((TOP_K, tn), col),
                   pl.BlockSpec((N_EXPERTS, 128), lambda i: (0, 0))],
        out_shape=[SDS((TOP_K, t), I32), SDS((TOP_K, t), F32), SDS((TOP_K, t), I32), SDS((N_EXPERTS, 128), I32)],
        scratch_shapes=[pltpu.VMEM((N_EXPERTS, 128), F32)],
        compiler_params=_cparams("arbitrary"),
        name="route",
    )(lg, rbias)


def _dest_kernel(idx_ref, rank_ref, offs_ref, dest_ref):
    tn = idx_ref.shape[1]
    rows = lax.broadcasted_iota(I32, (N_EXPERTS, tn), 0)
    offs = offs_ref[...]
    out = []
    for kk in range(TOP_K):
        hit = rows == idx_ref[kk:kk + 1, :]
        out.append(jnp.sum(jnp.where(hit, offs, 0.0), axis=0, keepdims=True))
    dest_ref[...] = jnp.concatenate(out, axis=0).astype(I32) + rank_ref[...]


def _dest_call(idx, rank, offs_col):
    t = idx.shape[1]
    tn = TN_ROUTE
    col = lambda i: (0, i)
    return pl.pallas_call(
        _dest_kernel,
        grid=(t // tn,),
        in_specs=[pl.BlockSpec((TOP_K, tn), col), pl.BlockSpec((TOP_K, tn), col),
                  pl.BlockSpec((N_EXPERTS, 1), lambda i: (0, 0))],
        out_specs=pl.BlockSpec((TOP_K, tn), col),
        out_shape=SDS((TOP_K, t), I32),
        compiler_params=_cparams("arbitrary"),
        name="dest",
    )(idx, rank, offs_col)


DISP_UNROLL = 2


def _dispatch_kernel(dest_ref, h_ref, xs_ref, sem):
    tn = dest_ref.shape[1]

    def issue(j, c):
        slots = [[dest_ref[kk, j * DISP_UNROLL + r] for kk in range(TOP_K)] for r in range(DISP_UNROLL)]
        for r in range(DISP_UNROLL):
            src = _packed_row(h_ref, j * DISP_UNROLL + r)
            for kk in range(TOP_K):
                pltpu.make_async_copy(src, _packed_row(xs_ref, slots[r][kk]), sem).start(priority=kk % 2)
        return c

    lax.fori_loop(0, tn // DISP_UNROLL, issue, 0)
    n = TOP_K * tn * ROW_CHUNKS
    pltpu.make_async_copy(xs_ref.at[pl.ds(0, n), :], xs_ref.at[pl.ds(0, n), :], sem).wait()


def _dispatch_call(dest, h1rows, n_rows):
    t = dest.shape[1]
    tn = TN_DISP
    return pl.pallas_call(
        _dispatch_kernel,
        grid=(t // tn,),
        in_specs=[pl.BlockSpec((TOP_K, tn), lambda i: (0, i), memory_space=pltpu.SMEM),
                  pl.BlockSpec((tn * ROW_CHUNKS, LANES), lambda i: (i, 0))],
        out_specs=pl.BlockSpec(memory_space=pl.ANY),
        out_shape=SDS((n_rows * ROW_CHUNKS, LANES), U32),
        scratch_shapes=[pltpu.SemaphoreType.DMA(())],
        compiler_params=_cparams("arbitrary"),
        name="dispatch",
    )(dest, h1rows)


def _expert_kernel(te_ref, tr_ref, nt_ref, first_ref, slot_ref, nxt_ref,
                   x_ref, wg_hbm, wu_hbm, wd_hbm, y_ref,
                   wg_buf, wu_buf, wd_buf, wg_b, wu_b, wd_b, sem):
    i = pl.program_id(0)

    def fetch(e, s):
        return (pltpu.make_async_copy(wg_hbm.at[e], wg_buf.at[s], sem.at[s]),
                pltpu.make_async_copy(wu_hbm.at[e], wu_buf.at[s], sem.at[s]),
                pltpu.make_async_copy(wd_hbm.at[e], wd_buf.at[s], sem.at[s]))

    @pl.when(i == 0)
    def _():
        for c in fetch(te_ref[0], 0):
            c.start()

    live_tile = i < nt_ref[0]

    @pl.when(jnp.logical_and(live_tile, first_ref[i] == 1))
    def _():
        s = slot_ref[i]
        nxt = nxt_ref[i]

        @pl.when(nxt >= 0)
        def _():
            for c in fetch(nxt, 1 - s):
                c.start()

        for c in fetch(te_ref[i], s):
            c.wait()
        wg_b[...] = wg_buf[s].astype(BF16)
        wu_b[...] = wu_buf[s].astype(BF16)
        wd_b[...] = wd_buf[s].astype(BF16)

    @pl.when(live_tile)
    def _():
        tm = x_ref.shape[0] // ROW_CHUNKS
        x = _load_packed_bf16(x_ref, 0, tm)
        g = jnp.dot(x, wg_b[...], preferred_element_type=F32)
        u = jnp.dot(x, wu_b[...], preferred_element_type=F32)
        live = lax.broadcasted_iota(I32, (tm, EXPERT_FF), 0) < tr_ref[i]
        hid = jnp.where(live, g * _sigmoid(g) * u, 0.0).astype(BF16)
        y = jnp.dot(hid, wd_b[...], preferred_element_type=F32)
        _store_packed(y_ref, 0, tm, _pack_rows(y[:, :HALF], y[:, HALF:]))


def _expert_call(tile_meta, xs, w_gate, w_up, w_down):
    n_rows = xs.shape[0] // ROW_CHUNKS
    tm = TM_EXP
    rows_map = lambda i, te, tr, nt, *_: (jnp.minimum(i, nt[0] - 1), 0)
    hbm = pl.BlockSpec(memory_space=pl.ANY)
    return pl.pallas_call(
        _expert_kernel,
        grid_spec=pltpu.PrefetchScalarGridSpec(
            num_scalar_prefetch=len(tile_meta),
            grid=(n_rows // tm,),
            in_specs=[pl.BlockSpec((tm * ROW_CHUNKS, LANES), rows_map), hbm, hbm, hbm],
            out_specs=pl.BlockSpec((tm * ROW_CHUNKS, LANES), rows_map),
            scratch_shapes=[pltpu.VMEM((2, D_MODEL, EXPERT_FF), F32), pltpu.VMEM((2, D_MODEL, EXPERT_FF), F32),
                            pltpu.VMEM((2, EXPERT_FF, D_MODEL), F32),
                            pltpu.VMEM((D_MODEL, EXPERT_FF), BF16), pltpu.VMEM((D_MODEL, EXPERT_FF), BF16),
                            pltpu.VMEM((EXPERT_FF, D_MODEL), BF16),
                            pltpu.SemaphoreType.DMA((2,))],
        ),
        out_shape=SDS((n_rows * ROW_CHUNKS, LANES), U32),
        compiler_params=_cparams("arbitrary"),
        name="experts",
    )(*tile_meta, xs, w_gate, w_up, w_down)


def _combine_kernel(dest_ref, dest_next_ref, wts_ref, h1_ref, ys_ref, wsg_ref, wsu_ref, wsd_ref, g2_ref, b2_ref,
                    o_ref, ybuf_ref, sem):
    i = pl.program_id(0)
    tn = h1_ref.shape[0]
    rows = TOP_K * tn
    slot = i % 2

    def gather(d_ref, s):
        base = s * rows

        def issue(t, c):
            srcs = [d_ref[kk, t] for kk in range(TOP_K)]
            for kk in range(TOP_K):
                pltpu.make_async_copy(_packed_row(ys_ref, srcs[kk]), _packed_row(ybuf_ref, base + kk * tn + t),
                                      sem.at[s]).start(priority=kk % 2)
            return c

        lax.fori_loop(0, tn, issue, 0)

    @pl.when(i == 0)
    def _():
        gather(dest_ref, 0)

    @pl.when(i + 1 < pl.num_programs(0))
    def _():
        gather(dest_next_ref, 1 - slot)

    h1 = h1_ref[...]
    hb = h1.astype(BF16)
    sg = jnp.dot(hb, wsg_ref[...], preferred_element_type=F32)
    su = jnp.dot(hb, wsu_ref[...], preferred_element_type=F32)
    ff = jnp.dot((sg * _sigmoid(sg) * su).astype(BF16), wsd_ref[...], preferred_element_type=F32)

    base = slot * rows
    pltpu.make_async_copy(
        ys_ref.at[pl.ds(0, rows * ROW_CHUNKS), :],
        ybuf_ref.at[pl.ds(pl.multiple_of(base * ROW_CHUNKS, rows * ROW_CHUNKS), rows * ROW_CHUNKS), :],
        sem.at[slot]).wait()

    eye = lax.broadcasted_iota(I32, (tn, tn), 0) == lax.broadcasted_iota(I32, (tn, tn), 1)
    w = wts_ref[...]
    for kk in range(TOP_K):
        wk = jnp.where(eye, w[kk:kk + 1, :], 0.0)
        wh = wk.astype(BF16)
        wl = (wk - wh.astype(F32)).astype(BF16)
        yk = _load_packed_bf16(ybuf_ref, pl.multiple_of((base + kk * tn) * ROW_CHUNKS, tn * ROW_CHUNKS), tn)
        ff = ff + jnp.dot(wh, yk, preferred_element_type=F32)
        ff = ff + jnp.dot(wl, yk, preferred_element_type=F32)
    o_ref[...] = _layer_norm(ALPHA * h1 + ff, g2_ref[...], b2_ref[...])


def _combine_call(dest, wts, h1, ys, wsg, wsu, wsd, g2, b2):
    t = h1.shape[0]
    tn = TN_COMB
    nsteps = t // tn
    col = lambda i: (0, i)
    col_next = lambda i: (0, jnp.minimum(i + 1, nsteps - 1))
    row = lambda i: (i, 0)
    fix = lambda i: (0, 0)
    return pl.pallas_call(
        _combine_kernel,
        grid=(nsteps,),
        in_specs=[pl.BlockSpec((TOP_K, tn), col, memory_space=pltpu.SMEM),
                  pl.BlockSpec((TOP_K, tn), col_next, memory_space=pltpu.SMEM),
                  pl.BlockSpec((TOP_K, tn), col),
                  pl.BlockSpec((tn, D_MODEL), row),
                  pl.BlockSpec(memory_space=pl.ANY),
                  pl.BlockSpec((D_MODEL, SHARED_FF), fix), pl.BlockSpec((D_MODEL, SHARED_FF), fix),
                  pl.BlockSpec((SHARED_FF, D_MODEL), fix),
                  pl.BlockSpec((1, D_MODEL), fix), pl.BlockSpec((1, D_MODEL), fix)],
        out_specs=pl.BlockSpec((tn, D_MODEL), row),
        out_shape=SDS((t, D_MODEL), F32),
        scratch_shapes=[pltpu.VMEM((2 * TOP_K * tn * ROW_CHUNKS, LANES), U32), pltpu.SemaphoreType.DMA((2,))],
        compiler_params=_cparams("arbitrary"),
        name="combine_ln2",
    )(dest, dest, wts, h1, ys, wsg, wsu, wsd, g2, b2)


def kernel(x, meta_tokens, ln_in_g, ln_in_b, rel_bias, w_in, conv_w, conv_b, conv_ln_g, conv_ln_b, sinks,
           w_out, ln1_g, ln1_b, w_router, router_bias, w_gate, w_up, w_down, ws_gate, ws_up, ws_down,
           ln2_g, ln2_b):
    nbatch, seq, d = x.shape
    assert d == D_MODEL and seq % TQ_PROJ == 0 and w_in.shape[0] == DEPTH
    t = nbatch * seq
    x2d = x.reshape(t, D_MODEL)
    vec = lambda a: a.reshape(1, -1).astype(F32)
    gin, bin_ = vec(ln_in_g), vec(ln_in_b)
    w_in_b = w_in[0].astype(BF16)

    q, k, v, u = _proj_call(x2d, gin, bin_, w_in_b, TQ_PROJ)
    meta_blk = jnp.concatenate([jnp.zeros((PAD_FRONT, D_MODEL), F32), meta_tokens.astype(F32)], axis=0)
    _, k_meta, v_meta, u_meta = _proj_call(meta_blk, gin, bin_, w_in_b, BLOCK)

    attn = _attn_call(q, k, v, k_meta, v_meta, _rel_bias_table(rel_bias), sinks[0].astype(F32),
                      nbatch, seq // BLOCK)

    u_halo = jnp.concatenate([jnp.zeros((CONV_HALO - N_META, CONV_CH), F32), u_meta[PAD_FRONT:]], axis=0)
    conv = _conv_call(u, u_halo, conv_w[0].astype(F32), vec(conv_b[0]), vec(conv_ln_g[0]), vec(conv_ln_b[0]),
                      nbatch, seq)

    w_out_b = w_out[0].astype(BF16)
    wr_t = w_router[0].astype(F32).T
    wr_hi = wr_t.astype(BF16)
    wr_lo = (wr_t - wr_hi.astype(F32)).astype(BF16)
    h1, h1rows, logits = _mix_call(x2d, attn, conv, gin, bin_, w_out_b[:ATTN_W], w_out_b[ATTN_W:],
                                   vec(ln1_g[0]), vec(ln1_b[0]), wr_hi, wr_lo)

    idx, wts, rank, cnt = _route_call(logits, router_bias[0].astype(F32).reshape(N_EXPERTS, 1))

    tm = TM_EXP
    n_tiles = (t * TOP_K) // tm + N_EXPERTS
    counts = cnt[:, 0]
    tiles_e = (counts + tm - 1) // tm
    tile_end = jnp.cumsum(tiles_e)
    offs = ((tile_end - tiles_e) * tm).astype(I32)
    tile_id = jnp.arange(n_tiles, dtype=I32)
    tile_expert = jnp.minimum(jnp.sum(tile_id[:, None] >= tile_end[None, :], axis=1), N_EXPERTS - 1).astype(I32)
    lo = jnp.maximum(tile_id[:, None] * tm, offs[None, :])
    hi = jnp.minimum((tile_id[:, None] + 1) * tm, (offs + counts)[None, :])
    tile_rows = jnp.sum(jnp.clip(hi - lo, 0, tm), axis=1).astype(I32)

    n_valid = tile_end[-1:].astype(I32)
    experts = jnp.arange(N_EXPERTS, dtype=I32)
    nonempty = tiles_e > 0
    slot_e = (jnp.cumsum(nonempty.astype(I32)) - 1) % 2
    later = jnp.where(nonempty, experts, N_EXPERTS)
    next_e = jnp.concatenate([jnp.flip(lax.cummin(jnp.flip(later)))[1:], jnp.full((1,), N_EXPERTS, I32)])
    next_e = jnp.where(next_e >= N_EXPERTS, -1, next_e)
    onehot = (tile_expert[:, None] == experts[None, :]).astype(I32)
    pick = lambda per_expert: jnp.sum(onehot * per_expert[None, :].astype(I32), axis=1).astype(I32)
    tile_first = (tile_id == pick(tile_end - tiles_e)).astype(I32)
    tile_meta = (tile_expert, tile_rows, n_valid, tile_first, pick(slot_e), pick(next_e))

    dest = _dest_call(idx, rank, offs.astype(F32).reshape(N_EXPERTS, 1))
    xs = _dispatch_call(dest, h1rows, n_tiles * tm)
    ys = _expert_call(tile_meta, xs, w_gate[0], w_up[0], w_down[0])
    out = _combine_call(dest, wts, h1, ys, ws_gate[0].astype(BF16), ws_up[0].astype(BF16),
                        ws_down[0].astype(BF16), vec(ln2_g[0]), vec(ln2_b[0]))
    return out.reshape(nbatch, seq, D_MODEL)
```

```python
import functools
import math

import numpy as np
import jax
import jax.numpy as jnp
from jax import lax
from jax.experimental import pallas as pl
from jax.experimental.pallas import tpu as pltpu

F32 = jnp.float32
BF16 = jnp.bfloat16
I32 = jnp.int32
U32 = jnp.uint32
SDS = jax.ShapeDtypeStruct

D_MODEL = 1024
HALF = D_MODEL // 2
LANES = 128
SUBLANES = 8
ROW_CHUNKS = HALF // LANES
N_META = 16
HEAD_DIM = 64
N_Q_HEADS = 8
N_KV_HEADS = 2
GQA_GROUP = N_Q_HEADS // N_KV_HEADS
ATTN_W = N_Q_HEADS * HEAD_DIM
KV_W = N_KV_HEADS * HEAD_DIM
WINDOW = 128
BLOCK = 128
CONV_CH = D_MODEL - ATTN_W
CONV_K = 31
IN_W = ATTN_W + 2 * KV_W + 2 * CONV_CH
NUM_BUCKETS = 32
MAX_EXACT = NUM_BUCKETS // 2
REL_MAX_DIST = 128
N_EXPERTS = 256
TOP_K = 8
N_GROUPS = 8
GROUP_SIZE = N_EXPERTS // N_GROUPS
TOPK_GROUPS = 4
EXPERT_FF = 256
SHARED_FF = 256
ROUTED_SCALE = 2.5
DEPTH = 1
ALPHA = (2.0 * DEPTH) ** 0.25
LN_EPS = 1e-5
NEG = -1e30
PAD_FRONT = (-N_META) % BLOCK

VMEM_LIMIT = 48 * 1024 * 1024

TQ_PROJ = 512
T_CONV = 256
CONV_HALO = 32
R_CONV = 64
TQ_MIX = 256
TN_ROUTE = 256
TN_DISP = 256
TM_EXP = 256
TN_COMB = 128


def _cparams(*sem):
    return pltpu.CompilerParams(dimension_semantics=sem, vmem_limit_bytes=VMEM_LIMIT)


def _layer_norm(x, g, b):
    mu = jnp.mean(x, axis=-1, keepdims=True)
    xc = x - mu
    var = jnp.mean(xc * xc, axis=-1, keepdims=True)
    return xc * lax.rsqrt(var + LN_EPS) * g + b


def _sigmoid(x):
    return 1.0 / (1.0 + jnp.exp(-x))


def _pack_rows(lo_half, hi_half):
    lo = lax.bitcast_convert_type(lo_half.astype(BF16).astype(F32), U32)
    hi = lax.bitcast_convert_type(hi_half.astype(BF16).astype(F32), U32)
    return lax.shift_right_logical(lo, jnp.uint32(16)) | hi


def _unpack_rows(p):
    lo = lax.bitcast_convert_type(lax.shift_left(p, jnp.uint32(16)), F32)
    hi = lax.bitcast_convert_type(p & jnp.uint32(0xFFFF0000), F32)
    return lo, hi


def _chunk_index(start, j, n, lead):
    rows = pl.ds(start + j, n, stride=ROW_CHUNKS)
    return (rows, slice(None)) if lead is None else (lead, rows, slice(None))


def _store_packed(ref, start, n, packed, lead=None):
    for j in range(ROW_CHUNKS):
        ref[_chunk_index(start, j, n, lead)] = packed[:, j * LANES:(j + 1) * LANES]


def _load_packed_bf16(ref, start, n, lead=None):
    halves = [_unpack_rows(ref[_chunk_index(start, j, n, lead)]) for j in range(ROW_CHUNKS)]
    return jnp.concatenate([h[0] for h in halves] + [h[1] for h in halves], axis=1).astype(BF16)


def _packed_row(ref, token):
    return ref.at[pl.ds(pl.multiple_of(token * ROW_CHUNKS, ROW_CHUNKS), ROW_CHUNKS), :]


def _proj_kernel(x_ref, g_ref, b_ref, w_ref, q_ref, k_ref, v_ref, u_ref):
    h = _layer_norm(x_ref[...], g_ref[...], b_ref[...])
    p = jnp.dot(h.astype(BF16), w_ref[...], preferred_element_type=F32)
    q_ref[...] = (p[:, :ATTN_W] * (HEAD_DIM ** -0.5)).astype(BF16)
    k_ref[...] = p[:, ATTN_W:ATTN_W + KV_W].astype(BF16)
    v_ref[...] = p[:, ATTN_W + KV_W:ATTN_W + 2 * KV_W].astype(BF16)
    a = p[:, ATTN_W + 2 * KV_W:ATTN_W + 2 * KV_W + CONV_CH]
    gate = p[:, ATTN_W + 2 * KV_W + CONV_CH:]
    u_ref[...] = a * _sigmoid(gate)


def _proj_call(x2d, gin, bin_, w_in_b, tq):
    t = x2d.shape[0]
    row = lambda i: (i, 0)
    fix = lambda i: (0, 0)
    return pl.pallas_call(
        _proj_kernel,
        grid=(t // tq,),
        in_specs=[pl.BlockSpec((tq, D_MODEL), row), pl.BlockSpec((1, D_MODEL), fix),
                  pl.BlockSpec((1, D_MODEL), fix), pl.BlockSpec((D_MODEL, IN_W), fix)],
        out_specs=[pl.BlockSpec((tq, ATTN_W), row), pl.BlockSpec((tq, KV_W), row),
                   pl.BlockSpec((tq, KV_W), row), pl.BlockSpec((tq, CONV_CH), row)],
        out_shape=[SDS((t, ATTN_W), BF16), SDS((t, KV_W), BF16), SDS((t, KV_W), BF16), SDS((t, CONV_CH), F32)],
        compiler_params=_cparams("arbitrary"),
        name="ln_in_proj",
    )(x2d, gin, bin_, w_in_b)


def _attn_kernel(sinks_ref, q_ref, kc_ref, kp_ref, vc_ref, vp_ref, km_ref, vm_ref, bias_ref, o_ref):
    first = pl.program_id(1) == 0
    kp = jnp.where(first, km_ref[...], kp_ref[...])
    vp = jnp.where(first, vm_ref[...], vp_ref[...])
    k = jnp.concatenate([kp, kc_ref[...]], axis=0)
    v = jnp.concatenate([vp, vc_ref[...]], axis=0)
    col = lax.broadcasted_iota(I32, (BLOCK, 2 * BLOCK), 1)
    pad_bias = jnp.where(jnp.logical_and(first, col < PAD_FRONT), NEG, 0.0).astype(F32)
    q = q_ref[...]
    outs = []
    for h in range(N_Q_HEADS):
        g = h // GQA_GROUP
        qh = q[:, h * HEAD_DIM:(h + 1) * HEAD_DIM]
        kg = k[:, g * HEAD_DIM:(g + 1) * HEAD_DIM]
        vg = v[:, g * HEAD_DIM:(g + 1) * HEAD_DIM]
        s = lax.dot_general(qh, kg, (((1,), (1,)), ((), ())), preferred_element_type=F32)
        s = s + bias_ref[h] + pad_bias
        sink = sinks_ref[h]
        m = jnp.maximum(jnp.max(s, axis=-1, keepdims=True), sink)
        p = jnp.exp(s - m)
        den = jnp.sum(p, axis=-1, keepdims=True) + jnp.exp(sink - m)
        o = jnp.dot(p.astype(BF16), vg, preferred_element_type=F32)
        outs.append(o / den)
    o_ref[...] = jnp.concatenate(outs, axis=1).astype(BF16)


def _attn_call(q, k, v, k_meta, v_meta, bias, sinks, nbatch, nblk):
    t = q.shape[0]
    cur = lambda b, j: (b * nblk + j, 0)
    prev = lambda b, j: (jnp.maximum(b * nblk + j - 1, 0), 0)
    fix2 = lambda b, j: (0, 0)
    return pl.pallas_call(
        _attn_kernel,
        grid=(nbatch, nblk),
        in_specs=[pl.BlockSpec(memory_space=pltpu.SMEM),
                  pl.BlockSpec((BLOCK, ATTN_W), cur),
                  pl.BlockSpec((BLOCK, KV_W), cur), pl.BlockSpec((BLOCK, KV_W), prev),
                  pl.BlockSpec((BLOCK, KV_W), cur), pl.BlockSpec((BLOCK, KV_W), prev),
                  pl.BlockSpec((BLOCK, KV_W), fix2), pl.BlockSpec((BLOCK, KV_W), fix2),
                  pl.BlockSpec((N_Q_HEADS, BLOCK, 2 * BLOCK), lambda b, j: (0, 0, 0))],
        out_specs=pl.BlockSpec((BLOCK, ATTN_W), cur),
        out_shape=SDS((t, ATTN_W), BF16),
        compiler_params=_cparams("arbitrary", "arbitrary"),
        name="swa_attn",
    )(sinks, q, k, k, v, v, k_meta, v_meta, bias)


def _rel_bias_table(rel_bias):
    qi = np.arange(BLOCK, dtype=np.int32)[:, None]
    kj = np.arange(2 * BLOCK, dtype=np.int32)[None, :]
    dist = BLOCK + qi - kj
    dc = np.clip(dist, 0, WINDOW - 1)
    nf = np.maximum(dc, 1).astype(np.float32)
    large = MAX_EXACT + (np.log(nf / np.float32(MAX_EXACT)) / np.float32(math.log(REL_MAX_DIST / MAX_EXACT))
                         * np.float32(NUM_BUCKETS - MAX_EXACT)).astype(np.int32)
    large = np.minimum(large, NUM_BUCKETS - 1)
    bucket = np.where(dc < MAX_EXACT, dc, large)
    in_window = (dist >= 0) & (dist < WINDOW)
    onehot = (bucket.reshape(-1, 1) == np.arange(NUM_BUCKETS)[None, :]).astype(np.float32)
    bias = jnp.dot(jnp.asarray(onehot), rel_bias.astype(F32), precision=lax.Precision.HIGHEST)
    bias = jnp.transpose(bias.reshape(BLOCK, 2 * BLOCK, N_Q_HEADS), (2, 0, 1))
    return jnp.where(in_window[None], bias, NEG)


def _conv_kernel(uc_ref, up_ref, um_ref, w_ref, cb_ref, g_ref, b_ref, o_ref, s_ref, sh_ref):
    first = pl.program_id(1) == 0
    s_ref[0:CONV_HALO, :] = jnp.where(first, um_ref[...], up_ref[...])
    s_ref[CONV_HALO:CONV_HALO + T_CONV, :] = uc_ref[...]
    off = CONV_HALO - (CONV_K - 1)
    span = sh_ref.shape[1]
    for c in range(0, T_CONV, R_CONV):
        for p in range(1, SUBLANES):
            sh_ref[p] = s_ref[c + p:c + p + span, :]
        acc = jnp.zeros((R_CONV, CONV_CH), F32) + cb_ref[...]
        for kk in range(CONV_K):
            p, a = (off + kk) % SUBLANES, (off + kk) // SUBLANES * SUBLANES
            if p == 0:
                win = s_ref[c + a:c + a + R_CONV, :]
            else:
                win = sh_ref[p, a:a + R_CONV, :]
            acc = acc + win * w_ref[kk:kk + 1, :]
        y = _layer_norm(acc, g_ref[...], b_ref[...])
        o_ref[c:c + R_CONV, :] = (y * _sigmoid(y)).astype(BF16)


def _conv_call(u, u_meta_halo, conv_w, conv_b, g, b, nbatch, seq):
    t = u.shape[0]
    nj = seq // T_CONV
    per = T_CONV // CONV_HALO
    cur = lambda bb, j: (bb * nj + j, 0)
    prev = lambda bb, j: (jnp.maximum((bb * nj + j) * per - 1, 0), 0)
    fix = lambda bb, j: (0, 0)
    return pl.pallas_call(
        _conv_kernel,
        grid=(nbatch, nj),
        in_specs=[pl.BlockSpec((T_CONV, CONV_CH), cur), pl.BlockSpec((CONV_HALO, CONV_CH), prev),
                  pl.BlockSpec((CONV_HALO, CONV_CH), fix), pl.BlockSpec((CONV_K, CONV_CH), fix),
                  pl.BlockSpec((1, CONV_CH), fix), pl.BlockSpec((1, CONV_CH), fix), pl.BlockSpec((1, CONV_CH), fix)],
        out_specs=pl.BlockSpec((T_CONV, CONV_CH), cur),
        out_shape=SDS((t, CONV_CH), BF16),
        scratch_shapes=[pltpu.VMEM((CONV_HALO + T_CONV, CONV_CH), F32),
                        pltpu.VMEM((SUBLANES, R_CONV + CONV_HALO - SUBLANES, CONV_CH), F32)],
        compiler_params=_cparams("arbitrary", "arbitrary"),
        name="conv_ln",
    )(u, u, u_meta_halo, conv_w, conv_b, g, b)


def _mix_kernel(x_ref, at_ref, cv_ref, gin_ref, bin_ref, woa_ref, woc_ref, g1_ref, b1_ref,
                wrh_ref, wrl_ref, h1_ref, h1r_ref, lg_ref):
    h = _layer_norm(x_ref[...], gin_ref[...], bin_ref[...])
    mix = (jnp.dot(at_ref[...], woa_ref[...], preferred_element_type=F32)
           + jnp.dot(cv_ref[...], woc_ref[...], preferred_element_type=F32))
    h1 = _layer_norm(ALPHA * h + mix, g1_ref[...], b1_ref[...])
    h1_ref[...] = h1
    _store_packed(h1r_ref, 0, h1.shape[0], _pack_rows(h1[:, :HALF], h1[:, HALF:]))
    hh = h1.astype(BF16)
    hl = (h1 - hh.astype(F32)).astype(BF16)
    nt = (((1,), (1,)), ((), ()))
    lg = lax.dot_general(wrh_ref[...], hh, nt, preferred_element_type=F32)
    lg = lg + lax.dot_general(wrh_ref[...], hl, nt, preferred_element_type=F32)
    lg = lg + lax.dot_general(wrl_ref[...], hh, nt, preferred_element_type=F32)
    lg_ref[...] = lg


def _mix_call(x2d, attn, conv, gin, bin_, woa, woc, g1, b1, wrh, wrl):
    t = x2d.shape[0]
    tq = TQ_MIX
    row = lambda i: (i, 0)
    fix = lambda i: (0, 0)
    return pl.pallas_call(
        _mix_kernel,
        grid=(t // tq,),
        in_specs=[pl.BlockSpec((tq, D_MODEL), row), pl.BlockSpec((tq, ATTN_W), row), pl.BlockSpec((tq, CONV_CH), row),
                  pl.BlockSpec((1, D_MODEL), fix), pl.BlockSpec((1, D_MODEL), fix),
                  pl.BlockSpec((ATTN_W, D_MODEL), fix), pl.BlockSpec((CONV_CH, D_MODEL), fix),
                  pl.BlockSpec((1, D_MODEL), fix), pl.BlockSpec((1, D_MODEL), fix),
                  pl.BlockSpec((N_EXPERTS, D_MODEL), fix), pl.BlockSpec((N_EXPERTS, D_MODEL), fix)],
        out_specs=[pl.BlockSpec((tq, D_MODEL), row), pl.BlockSpec((tq * ROW_CHUNKS, LANES), row),
                   pl.BlockSpec((N_EXPERTS, tq), lambda i: (0, i))],
        out_shape=[SDS((t, D_MODEL), F32), SDS((t * ROW_CHUNKS, LANES), U32), SDS((N_EXPERTS, t), F32)],
        compiler_params=_cparams("arbitrary"),
        name="mix_ln1",
    )(x2d, attn, conv, gin, bin_, woa, woc, g1, b1, wrh, wrl)


def _first_argmax(x, rows, nrows):
    m = jnp.max(x, axis=0, keepdims=True)
    idx = jnp.min(jnp.where(x == m, rows, nrows), axis=0, keepdims=True)
    return m, idx


def _route_kernel(lg_ref, rb_ref, idx_ref, wts_ref, rank_ref, cnt_ref, carry_ref):
    tn = lg_ref.shape[1]

    @pl.when(pl.program_id(0) == 0)
    def _():
        carry_ref[...] = jnp.zeros_like(carry_ref)

    scores = _sigmoid(lg_ref[...])
    choice = scores + rb_ref[...]
    rows = lax.broadcasted_iota(I32, (N_EXPERTS, tn), 0)
    rows_g = lax.broadcasted_iota(I32, (GROUP_SIZE, tn), 0)
    rows_8 = lax.broadcasted_iota(I32, (N_GROUPS, tn), 0)

    gs = []
    for g in range(N_GROUPS):
        xg = choice[g * GROUP_SIZE:(g + 1) * GROUP_SIZE, :]
        m1, i1 = _first_argmax(xg, rows_g, GROUP_SIZE)
        m2 = jnp.max(jnp.where(rows_g == i1, -jnp.inf, xg), axis=0, keepdims=True)
        gs.append(m1 + m2)
    gsc = jnp.concatenate(gs, axis=0)
    gsel = jnp.zeros((N_GROUPS, tn), F32)
    for _ in range(TOPK_GROUPS):
        _, gi = _first_argmax(gsc, rows_8, N_GROUPS)
        hit = rows_8 == gi
        gsel = jnp.where(hit, 1.0, gsel)
        gsc = jnp.where(hit, -jnp.inf, gsc)
    emask = jnp.concatenate(
        [jnp.broadcast_to(gsel[g:g + 1, :], (GROUP_SIZE, tn)) for g in range(N_GROUPS)], axis=0)
    masked = jnp.where(emask > 0.5, choice, NEG)

    sel_all = jnp.zeros((N_EXPERTS, tn), F32)
    hits, idxs, ws = [], [], []
    for _ in range(TOP_K):
        _, ii = _first_argmax(masked, rows, N_EXPERTS)
        hit = rows == ii
        hits.append(hit)
        idxs.append(ii)
        ws.append(jnp.sum(jnp.where(hit, scores, 0.0), axis=0, keepdims=True))
        sel_all = jnp.where(hit, 1.0, sel_all)
        masked = jnp.where(hit, -jnp.inf, masked)
    wsum = ws[0]
    for w in ws[1:]:
        wsum = wsum + w
    idx_ref[...] = jnp.concatenate(idxs, axis=0)
    wts_ref[...] = jnp.concatenate([w / wsum * ROUTED_SCALE for w in ws], axis=0)

    r_i = lax.broadcasted_iota(I32, (tn, tn), 0)
    c_i = lax.broadcasted_iota(I32, (tn, tn), 1)
    upper = jnp.where(r_i < c_i, 1.0, 0.0).astype(BF16)
    sel_b = sel_all.astype(BF16)
    carry = carry_ref[...]
    before = jnp.dot(sel_b, upper, preferred_element_type=F32)
    before = before + jnp.concatenate([carry] * (tn // 128), axis=1)
    rank_ref[...] = jnp.concatenate(
        [jnp.sum(jnp.where(h, before, 0.0), axis=0, keepdims=True) for h in hits], axis=0).astype(I32)
    carry = carry + jnp.dot(sel_b, jnp.ones((tn, 128), BF16), preferred_element_type=F32)
    carry_ref[...] = carry
    cnt_ref[...] = carry.astype(I32)


def _route_call(lg, rbias):
    t = lg.shape[1]
    tn = TN_ROUTE
    col = lambda i: (0, i)
    return pl.pallas_call(
        _route_kernel,
        grid=(t // tn,),
        in_specs=[pl.BlockSpec((N_EXPERTS, tn), col), pl.BlockSpec((N_EXPERTS, 1), lambda i: (0, 0))],
        out_specs=[pl.BlockSpec((TOP_K, tn), col), pl.BlockSpec((TOP_K, tn), col), pl.BlockSpec((TOP_K, tn), col),
                   pl.BlockSpec((N_EXPERTS, 128), lambda i: (0, 0))],
        out_shape=[SDS((TOP_K, t), I32), SDS((TOP_K, t), F32), SDS((TOP_K, t), I32), SDS((N_EXPERTS, 128), I32)],
        scratch_shapes=[pltpu.VMEM((N_EXPERTS, 128), F32)],
        compiler_params=_cparams("arbitrary"),
        name="route",
    )(lg, rbias)


def _dest_kernel(idx_ref, rank_ref, offs_ref, dest_ref):
    tn = idx_ref.shape[1]
    rows = lax.broadcasted_iota(I32, (N_EXPERTS, tn), 0)
    offs = offs_ref[...]
    out = []
    for kk in range(TOP_K):
        hit = rows == idx_ref[kk:kk + 1, :]
        out.append(jnp.sum(jnp.where(hit, offs, 0.0), axis=0, keepdims=True))
    dest_ref[...] = jnp.concatenate(out, axis=0).astype(I32) + rank_ref[...]


def _dest_call(idx, rank, offs_col):
    t = idx.shape[1]
    tn = TN_ROUTE
    col = lambda i: (0, i)
    return pl.pallas_call(
        _dest_kernel,
        grid=(t // tn,),
        in_specs=[pl.BlockSpec((TOP_K, tn), col), pl.BlockSpec((TOP_K, tn), col),
                  pl.BlockSpec((N_EXPERTS, 1), lambda i: (0, 0))],
        out_specs=pl.BlockSpec((TOP_K, tn), col),
        out_shape=SDS((TOP_K, t), I32),
        compiler_params=_cparams("arbitrary"),
        name="dest",
    )(idx, rank, offs_col)


DISP_UNROLL = 2


def _dispatch_kernel(dest_ref, h_ref, xs_ref, sem):
    tn = dest_ref.shape[1]

    def issue(j, c):
        slots = [[dest_ref[kk, j * DISP_UNROLL + r] for kk in range(TOP_K)] for r in range(DISP_UNROLL)]
        for r in range(DISP_UNROLL):
            src = _packed_row(h_ref, j * DISP_UNROLL + r)
            for kk in range(TOP_K):
                pltpu.make_async_copy(src, _packed_row(xs_ref, slots[r][kk]), sem).start(priority=kk % 2)
        return c

    lax.fori_loop(0, tn // DISP_UNROLL, issue, 0)
    n = TOP_K * tn * ROW_CHUNKS
    pltpu.make_async_copy(xs_ref.at[pl.ds(0, n), :], xs_ref.at[pl.ds(0, n), :], sem).wait()


def _dispatch_call(dest, h1rows, n_rows):
    t = dest.shape[1]
    tn = TN_DISP
    return pl.pallas_call(
        _dispatch_kernel,
        grid=(t // tn,),
        in_specs=[pl.BlockSpec((TOP_K, tn), lambda i: (0, i), memory_space=pltpu.SMEM),
                  pl.BlockSpec((tn * ROW_CHUNKS, LANES), lambda i: (i, 0))],
        out_specs=pl.BlockSpec(memory_space=pl.ANY),
        out_shape=SDS((n_rows * ROW_CHUNKS, LANES), U32),
        scratch_shapes=[pltpu.SemaphoreType.DMA(())],
        compiler_params=_cparams("arbitrary"),
        name="dispatch",
    )(dest, h1rows)


def _expert_kernel(ts_ref, te_ref, tr_ref, nv_ref, wg_ref, wu_ref, wd_ref, xs_hbm, ys_hbm,
                   xbuf, ybuf, wg_b, wu_b, wd_b, xsem, ysem):
    e = pl.program_id(0)
    rows = xbuf.shape[1]
    tm = rows // ROW_CHUNKS
    g0, g1, nv = ts_ref[e], te_ref[e], nv_ref[0]

    def x_copy(g, s):
        return pltpu.make_async_copy(xs_hbm.at[pl.ds(pl.multiple_of(g * rows, rows), rows), :], xbuf.at[s], xsem.at[s])

    def y_copy(g, s):
        return pltpu.make_async_copy(ybuf.at[s], ys_hbm.at[pl.ds(pl.multiple_of(g * rows, rows), rows), :], ysem.at[s])

    @pl.when(e == 0)
    def _():
        x_copy(0, 0).start()

    @pl.when(g1 > g0)
    def _():
        wg_b[...] = wg_ref[0].astype(BF16)
        wu_b[...] = wu_ref[0].astype(BF16)
        wd_b[...] = wd_ref[0].astype(BF16)

        def tile(g, c):
            s = g % 2
            x_copy(g, s).wait()

            @pl.when(g + 1 < nv)
            def _():
                x_copy(g + 1, 1 - s).start()

            x = _load_packed_bf16(xbuf, 0, tm, lead=s)
            gate = jnp.dot(x, wg_b[...], preferred_element_type=F32)
            up = jnp.dot(x, wu_b[...], preferred_element_type=F32)
            live = lax.broadcasted_iota(I32, (tm, EXPERT_FF), 0) < tr_ref[g]
            hid = jnp.where(live, gate * _sigmoid(gate) * up, 0.0).astype(BF16)
            y = jnp.dot(hid, wd_b[...], preferred_element_type=F32)

            @pl.when(g >= 2)
            def _():
                y_copy(g - 2, s).wait()

            _store_packed(ybuf, 0, tm, _pack_rows(y[:, :HALF], y[:, HALF:]), lead=s)
            y_copy(g, s).start()
            return c

        lax.fori_loop(g0, g1, tile, 0)

    @pl.when(e == pl.num_programs(0) - 1)
    def _():
        @pl.when(nv >= 2)
        def _():
            y_copy(nv - 2, nv % 2).wait()

        y_copy(nv - 1, (nv - 1) % 2).wait()


def _expert_call(tile_start, tile_end, tile_rows, n_valid, xs, w_gate, w_up, w_down, n_rows):
    tm = TM_EXP
    w_map = lambda e, *_: (e, 0, 0)
    hbm = pl.BlockSpec(memory_space=pl.ANY)
    return pl.pallas_call(
        _expert_kernel,
        grid_spec=pltpu.PrefetchScalarGridSpec(
            num_scalar_prefetch=4,
            grid=(N_EXPERTS,),
            in_specs=[pl.BlockSpec((1, D_MODEL, EXPERT_FF), w_map), pl.BlockSpec((1, D_MODEL, EXPERT_FF), w_map),
                      pl.BlockSpec((1, EXPERT_FF, D_MODEL), w_map), hbm],
            out_specs=hbm,
            scratch_shapes=[pltpu.VMEM((2, tm * ROW_CHUNKS, LANES), U32), pltpu.VMEM((2, tm * ROW_CHUNKS, LANES), U32),
                            pltpu.VMEM((D_MODEL, EXPERT_FF), BF16), pltpu.VMEM((D_MODEL, EXPERT_FF), BF16),
                            pltpu.VMEM((EXPERT_FF, D_MODEL), BF16),
                            pltpu.SemaphoreType.DMA((2,)), pltpu.SemaphoreType.DMA((2,))],
        ),
        out_shape=SDS((n_rows * ROW_CHUNKS, LANES), U32),
        compiler_params=_cparams("arbitrary"),
        name="experts",
    )(tile_start, tile_end, tile_rows, n_valid, w_gate, w_up, w_down, xs)


def _combine_kernel(dest_ref, dest_next_ref, wts_ref, h1_ref, ys_ref, wsg_ref, wsu_ref, wsd_ref, g2_ref, b2_ref,
                    o_ref, ybuf_ref, sem):
    i = pl.program_id(0)
    tn = h1_ref.shape[0]
    rows = TOP_K * tn
    slot = i % 2

    def gather(d_ref, s):
        base = s * rows

        def issue(t, c):
            srcs = [d_ref[kk, t] for kk in range(TOP_K)]
            for kk in range(TOP_K):
                pltpu.make_async_copy(_packed_row(ys_ref, srcs[kk]), _packed_row(ybuf_ref, base + kk * tn + t),
                                      sem.at[s]).start(priority=kk % 2)
            return c

        lax.fori_loop(0, tn, issue, 0)

    @pl.when(i == 0)
    def _():
        gather(dest_ref, 0)

    @pl.when(i + 1 < pl.num_programs(0))
    def _():
        gather(dest_next_ref, 1 - slot)

    h1 = h1_ref[...]
    hb = h1.astype(BF16)
    sg = jnp.dot(hb, wsg_ref[...], preferred_element_type=F32)
    su = jnp.dot(hb, wsu_ref[...], preferred_element_type=F32)
    ff = jnp.dot((sg * _sigmoid(sg) * su).astype(BF16), wsd_ref[...], preferred_element_type=F32)

    base = slot * rows
    pltpu.make_async_copy(
        ys_ref.at[pl.ds(0, rows * ROW_CHUNKS), :],
        ybuf_ref.at[pl.ds(pl.multiple_of(base * ROW_CHUNKS, rows * ROW_CHUNKS), rows * ROW_CHUNKS), :],
        sem.at[slot]).wait()

    eye = lax.broadcasted_iota(I32, (tn, tn), 0) == lax.broadcasted_iota(I32, (tn, tn), 1)
    w = wts_ref[...]
    for kk in range(TOP_K):
        wk = jnp.where(eye, w[kk:kk + 1, :], 0.0)
        wh = wk.astype(BF16)
        wl = (wk - wh.astype(F32)).astype(BF16)
        yk = _load_packed_bf16(ybuf_ref, pl.multiple_of((base + kk * tn) * ROW_CHUNKS, tn * ROW_CHUNKS), tn)
        ff = ff + jnp.dot(wh, yk, preferred_element_type=F32)
        ff = ff + jnp.dot(wl, yk, preferred_element_type=F32)
    o_ref[...] = _layer_norm(ALPHA * h1 + ff, g2_ref[...], b2_ref[...])


def _combine_call(dest, wts, h1, ys, wsg, wsu, wsd, g2, b2):
    t = h1.shape[0]
    tn = TN_COMB
    nsteps = t // tn
    col = lambda i: (0, i)
    col_next = lambda i: (0, jnp.minimum(i + 1, nsteps - 1))
    row = lambda i: (i, 0)
    fix = lambda i: (0, 0)
    return pl.pallas_call(
        _combine_kernel,
        grid=(nsteps,),
        in_specs=[pl.BlockSpec((TOP_K, tn), col, memory_space=pltpu.SMEM),
                  pl.BlockSpec((TOP_K, tn), col_next, memory_space=pltpu.SMEM),
                  pl.BlockSpec((TOP_K, tn), col),
                  pl.BlockSpec((tn, D_MODEL), row),
                  pl.BlockSpec(memory_space=pl.ANY),
                  pl.BlockSpec((D_MODEL, SHARED_FF), fix), pl.BlockSpec((D_MODEL, SHARED_FF), fix),
                  pl.BlockSpec((SHARED_FF, D_MODEL), fix),
                  pl.BlockSpec((1, D_MODEL), fix), pl.BlockSpec((1, D_MODEL), fix)],
        out_specs=pl.BlockSpec((tn, D_MODEL), row),
        out_shape=SDS((t, D_MODEL), F32),
        scratch_shapes=[pltpu.VMEM((2 * TOP_K * tn * ROW_CHUNKS, LANES), U32), pltpu.SemaphoreType.DMA((2,))],
        compiler_params=_cparams("arbitrary"),
        name="combine_ln2",
    )(dest, dest, wts, h1, ys, wsg, wsu, wsd, g2, b2)


def kernel(x, meta_tokens, ln_in_g, ln_in_b, rel_bias, w_in, conv_w, conv_b, conv_ln_g, conv_ln_b, sinks,
           w_out, ln1_g, ln1_b, w_router, router_bias, w_gate, w_up, w_down, ws_gate, ws_up, ws_down,
           ln2_g, ln2_b):
    nbatch, seq, d = x.shape
    assert d == D_MODEL and seq % TQ_PROJ == 0 and w_in.shape[0] == DEPTH
    t = nbatch * seq
    x2d = x.reshape(t, D_MODEL)
    vec = lambda a: a.reshape(1, -1).astype(F32)
    gin, bin_ = vec(ln_in_g), vec(ln_in_b)
    w_in_b = w_in[0].astype(BF16)

    q, k, v, u = _proj_call(x2d, gin, bin_, w_in_b, TQ_PROJ)
    meta_blk = jnp.concatenate([jnp.zeros((PAD_FRONT, D_MODEL), F32), meta_tokens.astype(F32)], axis=0)
    _, k_meta, v_meta, u_meta = _proj_call(meta_blk, gin, bin_, w_in_b, BLOCK)

    attn = _attn_call(q, k, v, k_meta, v_meta, _rel_bias_table(rel_bias), sinks[0].astype(F32),
                      nbatch, seq // BLOCK)

    u_halo = jnp.concatenate([jnp.zeros((CONV_HALO - N_META, CONV_CH), F32), u_meta[PAD_FRONT:]], axis=0)
    conv = _conv_call(u, u_halo, conv_w[0].astype(F32), vec(conv_b[0]), vec(conv_ln_g[0]), vec(conv_ln_b[0]),
                      nbatch, seq)

    w_out_b = w_out[0].astype(BF16)
    wr_t = w_router[0].astype(F32).T
    wr_hi = wr_t.astype(BF16)
    wr_lo = (wr_t - wr_hi.astype(F32)).astype(BF16)
    h1, h1rows, logits = _mix_call(x2d, attn, conv, gin, bin_, w_out_b[:ATTN_W], w_out_b[ATTN_W:],
                                   vec(ln1_g[0]), vec(ln1_b[0]), wr_hi, wr_lo)

    idx, wts, rank, cnt = _route_call(logits, router_bias[0].astype(F32).reshape(N_EXPERTS, 1))

    tm = TM_EXP
    n_tiles = (t * TOP_K) // tm + N_EXPERTS
    counts = cnt[:, 0]
    tiles_e = (counts + tm - 1) // tm
    tile_end = jnp.cumsum(tiles_e).astype(I32)
    tile_start = (tile_end - tiles_e).astype(I32)
    offs = tile_start * tm
    tile_id = jnp.arange(n_tiles, dtype=I32)
    lo = jnp.maximum(tile_id[:, None] * tm, offs[None, :])
    hi = jnp.minimum((tile_id[:, None] + 1) * tm, (offs + counts)[None, :])
    tile_rows = jnp.sum(jnp.clip(hi - lo, 0, tm), axis=1).astype(I32)
    n_valid = tile_end[-1:]

    dest = _dest_call(idx, rank, offs.astype(F32).reshape(N_EXPERTS, 1))
    xs = _dispatch_call(dest, h1rows, n_tiles * tm)
    ys = _expert_call(tile_start, tile_end, tile_rows, n_valid, xs, w_gate[0], w_up[0], w_down[0], n_tiles * tm)
    out = _combine_call(dest, wts, h1, ys, ws_gate[0].astype(BF16), ws_up[0].astype(BF16),
                        ws_down[0].astype(BF16), vec(ln2_g[0]), vec(ln2_b[0]))
    return out.reshape(nbatch, seq, D_MODEL)
```

```python
import functools
import math

import numpy as np
import jax
import jax.numpy as jnp
from jax import lax
from jax.experimental import pallas as pl
from jax.experimental.pallas import tpu as pltpu

F32 = jnp.float32
BF16 = jnp.bfloat16
I32 = jnp.int32
U32 = jnp.uint32
SDS = jax.ShapeDtypeStruct

D_MODEL = 1024
HALF = D_MODEL // 2
LANES = 128
SUBLANES = 8
ROW_CHUNKS = HALF // LANES
N_META = 16
HEAD_DIM = 64
N_Q_HEADS = 8
N_KV_HEADS = 2
GQA_GROUP = N_Q_HEADS // N_KV_HEADS
ATTN_W = N_Q_HEADS * HEAD_DIM
KV_W = N_KV_HEADS * HEAD_DIM
WINDOW = 128
BLOCK = 128
CONV_CH = D_MODEL - ATTN_W
CONV_K = 31
IN_W = ATTN_W + 2 * KV_W + 2 * CONV_CH
NUM_BUCKETS = 32
MAX_EXACT = NUM_BUCKETS // 2
REL_MAX_DIST = 128
N_EXPERTS = 256
TOP_K = 8
N_GROUPS = 8
GROUP_SIZE = N_EXPERTS // N_GROUPS
TOPK_GROUPS = 4
EXPERT_FF = 256
SHARED_FF = 256
ROUTED_SCALE = 2.5
DEPTH = 1
ALPHA = (2.0 * DEPTH) ** 0.25
LN_EPS = 1e-5
NEG = -1e30
PAD_FRONT = (-N_META) % BLOCK

VMEM_LIMIT = 48 * 1024 * 1024

TQ_PROJ = 512
T_CONV = 256
CONV_HALO = 32
R_CONV = 64
TQ_MIX = 256
TN_ROUTE = 256
TN_DISP = 256
TM_EXP = 256
X_SLOTS = 4
X_AHEAD = X_SLOTS - 1
Y_SLOTS = 4
TN_COMB = 128


def _cparams(*sem):
    return pltpu.CompilerParams(dimension_semantics=sem, vmem_limit_bytes=VMEM_LIMIT)


def _layer_norm(x, g, b):
    mu = jnp.mean(x, axis=-1, keepdims=True)
    xc = x - mu
    var = jnp.mean(xc * xc, axis=-1, keepdims=True)
    return xc * lax.rsqrt(var + LN_EPS) * g + b


def _sigmoid(x):
    return 1.0 / (1.0 + jnp.exp(-x))


def _pack_rows(lo_half, hi_half):
    lo = lax.bitcast_convert_type(lo_half.astype(BF16).astype(F32), U32)
    hi = lax.bitcast_convert_type(hi_half.astype(BF16).astype(F32), U32)
    return lax.shift_right_logical(lo, jnp.uint32(16)) | hi


def _unpack_rows(p):
    lo = lax.bitcast_convert_type(lax.shift_left(p, jnp.uint32(16)), F32)
    hi = lax.bitcast_convert_type(p & jnp.uint32(0xFFFF0000), F32)
    return lo, hi


def _chunk_index(start, j, n, lead):
    rows = pl.ds(start + j, n, stride=ROW_CHUNKS)
    return (rows, slice(None)) if lead is None else (lead, rows, slice(None))


def _store_packed(ref, start, n, packed, lead=None):
    for j in range(ROW_CHUNKS):
        ref[_chunk_index(start, j, n, lead)] = packed[:, j * LANES:(j + 1) * LANES]


def _load_packed_bf16(ref, start, n, lead=None):
    halves = [_unpack_rows(ref[_chunk_index(start, j, n, lead)]) for j in range(ROW_CHUNKS)]
    return jnp.concatenate([h[0] for h in halves] + [h[1] for h in halves], axis=1).astype(BF16)


def _packed_row(ref, token):
    return ref.at[pl.ds(pl.multiple_of(token * ROW_CHUNKS, ROW_CHUNKS), ROW_CHUNKS), :]


def _proj_kernel(x_ref, g_ref, b_ref, w_ref, q_ref, k_ref, v_ref, u_ref):
    h = _layer_norm(x_ref[...], g_ref[...], b_ref[...])
    p = jnp.dot(h.astype(BF16), w_ref[...], preferred_element_type=F32)
    q_ref[...] = (p[:, :ATTN_W] * (HEAD_DIM ** -0.5)).astype(BF16)
    k_ref[...] = p[:, ATTN_W:ATTN_W + KV_W].astype(BF16)
    v_ref[...] = p[:, ATTN_W + KV_W:ATTN_W + 2 * KV_W].astype(BF16)
    a = p[:, ATTN_W + 2 * KV_W:ATTN_W + 2 * KV_W + CONV_CH]
    gate = p[:, ATTN_W + 2 * KV_W + CONV_CH:]
    u_ref[...] = a * _sigmoid(gate)


def _proj_call(x2d, gin, bin_, w_in_b, tq):
    t = x2d.shape[0]
    row = lambda i: (i, 0)
    fix = lambda i: (0, 0)
    return pl.pallas_call(
        _proj_kernel,
        grid=(t // tq,),
        in_specs=[pl.BlockSpec((tq, D_MODEL), row), pl.BlockSpec((1, D_MODEL), fix),
                  pl.BlockSpec((1, D_MODEL), fix), pl.BlockSpec((D_MODEL, IN_W), fix)],
        out_specs=[pl.BlockSpec((tq, ATTN_W), row), pl.BlockSpec((tq, KV_W), row),
                   pl.BlockSpec((tq, KV_W), row), pl.BlockSpec((tq, CONV_CH), row)],
        out_shape=[SDS((t, ATTN_W), BF16), SDS((t, KV_W), BF16), SDS((t, KV_W), BF16), SDS((t, CONV_CH), F32)],
        compiler_params=_cparams("arbitrary"),
        name="ln_in_proj",
    )(x2d, gin, bin_, w_in_b)


def _attn_kernel(sinks_ref, q_ref, kc_ref, kp_ref, vc_ref, vp_ref, km_ref, vm_ref, bias_ref, o_ref):
    first = pl.program_id(1) == 0
    kp = jnp.where(first, km_ref[...], kp_ref[...])
    vp = jnp.where(first, vm_ref[...], vp_ref[...])
    k = jnp.concatenate([kp, kc_ref[...]], axis=0)
    v = jnp.concatenate([vp, vc_ref[...]], axis=0)
    col = lax.broadcasted_iota(I32, (BLOCK, 2 * BLOCK), 1)
    pad_bias = jnp.where(jnp.logical_and(first, col < PAD_FRONT), NEG, 0.0).astype(F32)
    q = q_ref[...]
    outs = []
    for h in range(N_Q_HEADS):
        g = h // GQA_GROUP
        qh = q[:, h * HEAD_DIM:(h + 1) * HEAD_DIM]
        kg = k[:, g * HEAD_DIM:(g + 1) * HEAD_DIM]
        vg = v[:, g * HEAD_DIM:(g + 1) * HEAD_DIM]
        s = lax.dot_general(qh, kg, (((1,), (1,)), ((), ())), preferred_element_type=F32)
        s = s + bias_ref[h] + pad_bias
        sink = sinks_ref[h]
        m = jnp.maximum(jnp.max(s, axis=-1, keepdims=True), sink)
        p = jnp.exp(s - m)
        den = jnp.sum(p, axis=-1, keepdims=True) + jnp.exp(sink - m)
        o = jnp.dot(p.astype(BF16), vg, preferred_element_type=F32)
        outs.append(o / den)
    o_ref[...] = jnp.concatenate(outs, axis=1).astype(BF16)


def _attn_call(q, k, v, k_meta, v_meta, bias, sinks, nbatch, nblk):
    t = q.shape[0]
    cur = lambda b, j: (b * nblk + j, 0)
    prev = lambda b, j: (jnp.maximum(b * nblk + j - 1, 0), 0)
    fix2 = lambda b, j: (0, 0)
    return pl.pallas_call(
        _attn_kernel,
        grid=(nbatch, nblk),
        in_specs=[pl.BlockSpec(memory_space=pltpu.SMEM),
                  pl.BlockSpec((BLOCK, ATTN_W), cur),
                  pl.BlockSpec((BLOCK, KV_W), cur), pl.BlockSpec((BLOCK, KV_W), prev),
                  pl.BlockSpec((BLOCK, KV_W), cur), pl.BlockSpec((BLOCK, KV_W), prev),
                  pl.BlockSpec((BLOCK, KV_W), fix2), pl.BlockSpec((BLOCK, KV_W), fix2),
                  pl.BlockSpec((N_Q_HEADS, BLOCK, 2 * BLOCK), lambda b, j: (0, 0, 0))],
        out_specs=pl.BlockSpec((BLOCK, ATTN_W), cur),
        out_shape=SDS((t, ATTN_W), BF16),
        compiler_params=_cparams("arbitrary", "arbitrary"),
        name="swa_attn",
    )(sinks, q, k, k, v, v, k_meta, v_meta, bias)


def _rel_bias_table(rel_bias):
    qi = np.arange(BLOCK, dtype=np.int32)[:, None]
    kj = np.arange(2 * BLOCK, dtype=np.int32)[None, :]
    dist = BLOCK + qi - kj
    dc = np.clip(dist, 0, WINDOW - 1)
    nf = np.maximum(dc, 1).astype(np.float32)
    large = MAX_EXACT + (np.log(nf / np.float32(MAX_EXACT)) / np.float32(math.log(REL_MAX_DIST / MAX_EXACT))
                         * np.float32(NUM_BUCKETS - MAX_EXACT)).astype(np.int32)
    large = np.minimum(large, NUM_BUCKETS - 1)
    bucket = np.where(dc < MAX_EXACT, dc, large)
    in_window = (dist >= 0) & (dist < WINDOW)
    onehot = (bucket.reshape(-1, 1) == np.arange(NUM_BUCKETS)[None, :]).astype(np.float32)
    bias = jnp.dot(jnp.asarray(onehot), rel_bias.astype(F32), precision=lax.Precision.HIGHEST)
    bias = jnp.transpose(bias.reshape(BLOCK, 2 * BLOCK, N_Q_HEADS), (2, 0, 1))
    return jnp.where(in_window[None], bias, NEG)


def _conv_kernel(uc_ref, up_ref, um_ref, w_ref, cb_ref, g_ref, b_ref, o_ref, s_ref, sh_ref):
    first = pl.program_id(1) == 0
    s_ref[0:CONV_HALO, :] = jnp.where(first, um_ref[...], up_ref[...])
    s_ref[CONV_HALO:CONV_HALO + T_CONV, :] = uc_ref[...]
    off = CONV_HALO - (CONV_K - 1)
    span = sh_ref.shape[1]
    for c in range(0, T_CONV, R_CONV):
        for p in range(1, SUBLANES):
            sh_ref[p] = s_ref[c + p:c + p + span, :]
        acc = jnp.zeros((R_CONV, CONV_CH), F32) + cb_ref[...]
        for kk in range(CONV_K):
            p, a = (off + kk) % SUBLANES, (off + kk) // SUBLANES * SUBLANES
            if p == 0:
                win = s_ref[c + a:c + a + R_CONV, :]
            else:
                win = sh_ref[p, a:a + R_CONV, :]
            acc = acc + win * w_ref[kk:kk + 1, :]
        y = _layer_norm(acc, g_ref[...], b_ref[...])
        o_ref[c:c + R_CONV, :] = (y * _sigmoid(y)).astype(BF16)


def _conv_call(u, u_meta_halo, conv_w, conv_b, g, b, nbatch, seq):
    t = u.shape[0]
    nj = seq // T_CONV
    per = T_CONV // CONV_HALO
    cur = lambda bb, j: (bb * nj + j, 0)
    prev = lambda bb, j: (jnp.maximum((bb * nj + j) * per - 1, 0), 0)
    fix = lambda bb, j: (0, 0)
    return pl.pallas_call(
        _conv_kernel,
        grid=(nbatch, nj),
        in_specs=[pl.BlockSpec((T_CONV, CONV_CH), cur), pl.BlockSpec((CONV_HALO, CONV_CH), prev),
                  pl.BlockSpec((CONV_HALO, CONV_CH), fix), pl.BlockSpec((CONV_K, CONV_CH), fix),
                  pl.BlockSpec((1, CONV_CH), fix), pl.BlockSpec((1, CONV_CH), fix), pl.BlockSpec((1, CONV_CH), fix)],
        out_specs=pl.BlockSpec((T_CONV, CONV_CH), cur),
        out_shape=SDS((t, CONV_CH), BF16),
        scratch_shapes=[pltpu.VMEM((CONV_HALO + T_CONV, CONV_CH), F32),
                        pltpu.VMEM((SUBLANES, R_CONV + CONV_HALO - SUBLANES, CONV_CH), F32)],
        compiler_params=_cparams("arbitrary", "arbitrary"),
        name="conv_ln",
    )(u, u, u_meta_halo, conv_w, conv_b, g, b)


def _mix_kernel(x_ref, at_ref, cv_ref, gin_ref, bin_ref, woa_ref, woc_ref, g1_ref, b1_ref,
                wrh_ref, wrl_ref, h1_ref, h1r_ref, lg_ref):
    h = _layer_norm(x_ref[...], gin_ref[...], bin_ref[...])
    mix = (jnp.dot(at_ref[...], woa_ref[...], preferred_element_type=F32)
           + jnp.dot(cv_ref[...], woc_ref[...], preferred_element_type=F32))
    h1 = _layer_norm(ALPHA * h + mix, g1_ref[...], b1_ref[...])
    h1_ref[...] = h1
    _store_packed(h1r_ref, 0, h1.shape[0], _pack_rows(h1[:, :HALF], h1[:, HALF:]))
    hh = h1.astype(BF16)
    hl = (h1 - hh.astype(F32)).astype(BF16)
    nt = (((1,), (1,)), ((), ()))
    lg = lax.dot_general(wrh_ref[...], hh, nt, preferred_element_type=F32)
    lg = lg + lax.dot_general(wrh_ref[...], hl, nt, preferred_element_type=F32)
    lg = lg + lax.dot_general(wrl_ref[...], hh, nt, preferred_element_type=F32)
    lg_ref[...] = lg


def _mix_call(x2d, attn, conv, gin, bin_, woa, woc, g1, b1, wrh, wrl):
    t = x2d.shape[0]
    tq = TQ_MIX
    row = lambda i: (i, 0)
    fix = lambda i: (0, 0)
    return pl.pallas_call(
        _mix_kernel,
        grid=(t // tq,),
        in_specs=[pl.BlockSpec((tq, D_MODEL), row), pl.BlockSpec((tq, ATTN_W), row), pl.BlockSpec((tq, CONV_CH), row),
                  pl.BlockSpec((1, D_MODEL), fix), pl.BlockSpec((1, D_MODEL), fix),
                  pl.BlockSpec((ATTN_W, D_MODEL), fix), pl.BlockSpec((CONV_CH, D_MODEL), fix),
                  pl.BlockSpec((1, D_MODEL), fix), pl.BlockSpec((1, D_MODEL), fix),
                  pl.BlockSpec((N_EXPERTS, D_MODEL), fix), pl.BlockSpec((N_EXPERTS, D_MODEL), fix)],
        out_specs=[pl.BlockSpec((tq, D_MODEL), row), pl.BlockSpec((tq * ROW_CHUNKS, LANES), row),
                   pl.BlockSpec((N_EXPERTS, tq), lambda i: (0, i))],
        out_shape=[SDS((t, D_MODEL), F32), SDS((t * ROW_CHUNKS, LANES), U32), SDS((N_EXPERTS, t), F32)],
        compiler_params=_cparams("arbitrary"),
        name="mix_ln1",
    )(x2d, attn, conv, gin, bin_, woa, woc, g1, b1, wrh, wrl)


def _first_argmax(x, rows, nrows):
    m = jnp.max(x, axis=0, keepdims=True)
    idx = jnp.min(jnp.where(x == m, rows, nrows), axis=0, keepdims=True)
    return m, idx


def _route_kernel(lg_ref, rb_ref, idx_ref, wts_ref, rank_ref, cnt_ref, carry_ref):
    tn = lg_ref.shape[1]

    @pl.when(pl.program_id(0) == 0)
    def _():
        carry_ref[...] = jnp.zeros_like(carry_ref)

    scores = _sigmoid(lg_ref[...])
    choice = scores + rb_ref[...]
    rows = lax.broadcasted_iota(I32, (N_EXPERTS, tn), 0)
    rows_g = lax.broadcasted_iota(I32, (GROUP_SIZE, tn), 0)
    rows_8 = lax.broadcasted_iota(I32, (N_GROUPS, tn), 0)

    gs = []
    for g in range(N_GROUPS):
        xg = choice[g * GROUP_SIZE:(g + 1) * GROUP_SIZE, :]
        m1, i1 = _first_argmax(xg, rows_g, GROUP_SIZE)
        m2 = jnp.max(jnp.where(rows_g == i1, -jnp.inf, xg), axis=0, keepdims=True)
        gs.append(m1 + m2)
    gsc = jnp.concatenate(gs, axis=0)
    gsel = jnp.zeros((N_GROUPS, tn), F32)
    for _ in range(TOPK_GROUPS):
        _, gi = _first_argmax(gsc, rows_8, N_GROUPS)
        hit = rows_8 == gi
        gsel = jnp.where(hit, 1.0, gsel)
        gsc = jnp.where(hit, -jnp.inf, gsc)
    emask = jnp.concatenate(
        [jnp.broadcast_to(gsel[g:g + 1, :], (GROUP_SIZE, tn)) for g in range(N_GROUPS)], axis=0)
    masked = jnp.where(emask > 0.5, choice, NEG)

    sel_all = jnp.zeros((N_EXPERTS, tn), F32)
    hits, idxs, ws = [], [], []
    for _ in range(TOP_K):
        _, ii = _first_argmax(masked, rows, N_EXPERTS)
        hit = rows == ii
        hits.append(hit)
        idxs.append(ii)
        ws.append(jnp.sum(jnp.where(hit, scores, 0.0), axis=0, keepdims=True))
        sel_all = jnp.where(hit, 1.0, sel_all)
        masked = jnp.where(hit, -jnp.inf, masked)
    wsum = ws[0]
    for w in ws[1:]:
        wsum = wsum + w
    idx_ref[...] = jnp.concatenate(idxs, axis=0)
    wts_ref[...] = jnp.concatenate([w / wsum * ROUTED_SCALE for w in ws], axis=0)

    r_i = lax.broadcasted_iota(I32, (tn, tn), 0)
    c_i = lax.broadcasted_iota(I32, (tn, tn), 1)
    upper = jnp.where(r_i < c_i, 1.0, 0.0).astype(BF16)
    sel_b = sel_all.astype(BF16)
    carry = carry_ref[...]
    before = jnp.dot(sel_b, upper, preferred_element_type=F32)
    before = before + jnp.concatenate([carry] * (tn // 128), axis=1)
    rank_ref[...] = jnp.concatenate(
        [jnp.sum(jnp.where(h, before, 0.0), axis=0, keepdims=True) for h in hits], axis=0).astype(I32)
    carry = carry + jnp.dot(sel_b, jnp.ones((tn, 128), BF16), preferred_element_type=F32)
    carry_ref[...] = carry
    cnt_ref[...] = carry.astype(I32)


def _route_call(lg, rbias):
    t = lg.shape[1]
    tn = TN_ROUTE
    col = lambda i: (0, i)
    return pl.pallas_call(
        _route_kernel,
        grid=(t // tn,),
        in_specs=[pl.BlockSpec((N_EXPERTS, tn), col), pl.BlockSpec((N_EXPERTS, 1), lambda i: (0, 0))],
        out_specs=[pl.BlockSpec((TOP_K, tn), col), pl.BlockSpec((TOP_K, tn), col), pl.BlockSpec((TOP_K, tn), col),
                   pl.BlockSpec((N_EXPERTS, 128), lambda i: (0, 0))],
        out_shape=[SDS((TOP_K, t), I32), SDS((TOP_K, t), F32), SDS((TOP_K, t), I32), SDS((N_EXPERTS, 128), I32)],
        scratch_shapes=[pltpu.VMEM((N_EXPERTS, 128), F32)],
        compiler_params=_cparams("arbitrary"),
        name="route",
    )(lg, rbias)


def _dest_kernel(idx_ref, rank_ref, offs_ref, dest_ref):
    tn = idx_ref.shape[1]
    rows = lax.broadcasted_iota(I32, (N_EXPERTS, tn), 0)
    offs = offs_ref[...]
    out = []
    for kk in range(TOP_K):
        hit = rows == idx_ref[kk:kk + 1, :]
        out.append(jnp.sum(jnp.where(hit, offs, 0.0), axis=0, keepdims=True))
    dest_ref[...] = jnp.concatenate(out, axis=0).astype(I32) + rank_ref[...]


def _dest_call(idx, rank, offs_col):
    t = idx.shape[1]
    tn = TN_ROUTE
    col = lambda i: (0, i)
    return pl.pallas_call(
        _dest_kernel,
        grid=(t // tn,),
        in_specs=[pl.BlockSpec((TOP_K, tn), col), pl.BlockSpec((TOP_K, tn), col),
                  pl.BlockSpec((N_EXPERTS, 1), lambda i: (0, 0))],
        out_specs=pl.BlockSpec((TOP_K, tn), col),
        out_shape=SDS((TOP_K, t), I32),
        compiler_params=_cparams("arbitrary"),
        name="dest",
    )(idx, rank, offs_col)


DISP_UNROLL = 2


def _dispatch_kernel(dest_ref, h_ref, xs_ref, sem):
    tn = dest_ref.shape[1]

    def issue(j, c):
        slots = [[dest_ref[kk, j * DISP_UNROLL + r] for kk in range(TOP_K)] for r in range(DISP_UNROLL)]
        for r in range(DISP_UNROLL):
            src = _packed_row(h_ref, j * DISP_UNROLL + r)
            for kk in range(TOP_K):
                pltpu.make_async_copy(src, _packed_row(xs_ref, slots[r][kk]), sem).start(priority=kk % 2)
        return c

    lax.fori_loop(0, tn // DISP_UNROLL, issue, 0)
    n = TOP_K * tn * ROW_CHUNKS
    pltpu.make_async_copy(xs_ref.at[pl.ds(0, n), :], xs_ref.at[pl.ds(0, n), :], sem).wait()


def _dispatch_call(dest, h1rows, n_rows):
    t = dest.shape[1]
    tn = TN_DISP
    return pl.pallas_call(
        _dispatch_kernel,
        grid=(t // tn,),
        in_specs=[pl.BlockSpec((TOP_K, tn), lambda i: (0, i), memory_space=pltpu.SMEM),
                  pl.BlockSpec((tn * ROW_CHUNKS, LANES), lambda i: (i, 0))],
        out_specs=pl.BlockSpec(memory_space=pl.ANY),
        out_shape=SDS((n_rows * ROW_CHUNKS, LANES), U32),
        scratch_shapes=[pltpu.SemaphoreType.DMA(())],
        compiler_params=_cparams("arbitrary"),
        name="dispatch",
    )(dest, h1rows)


def _expert_kernel(ts_ref, te_ref, tr_ref, nv_ref, wg_ref, wu_ref, wd_ref, xs_hbm, ys_hbm,
                   xbuf, ybuf, wg_b, wu_b, wd_b, xsem, ysem):
    e = pl.program_id(0)
    rows = xbuf.shape[1]
    tm = rows // ROW_CHUNKS
    g0, g1, nv = ts_ref[e], te_ref[e], nv_ref[0]

    def x_copy(g, s):
        return pltpu.make_async_copy(xs_hbm.at[pl.ds(pl.multiple_of(g * rows, rows), rows), :], xbuf.at[s], xsem.at[s])

    def y_copy(g, s):
        return pltpu.make_async_copy(ybuf.at[s], ys_hbm.at[pl.ds(pl.multiple_of(g * rows, rows), rows), :], ysem.at[s])

    @pl.when(e == 0)
    def _():
        for g in range(X_AHEAD):
            @pl.when(g < nv)
            def _():
                x_copy(g, g).start(priority=1)

    @pl.when(g1 > g0)
    def _():
        wg_b[...] = wg_ref[0].astype(BF16)
        wu_b[...] = wu_ref[0].astype(BF16)
        wd_b[...] = wd_ref[0].astype(BF16)

        def tile(g, c):
            s = g % Y_SLOTS
            sx = g % X_SLOTS
            x_copy(g, sx).wait()

            @pl.when(g + X_AHEAD < nv)
            def _():
                x_copy(g + X_AHEAD, (g + X_AHEAD) % X_SLOTS).start(priority=1)

            x = _load_packed_bf16(xbuf, 0, tm, lead=sx)
            gate = jnp.dot(x, wg_b[...], preferred_element_type=F32)
            up = jnp.dot(x, wu_b[...], preferred_element_type=F32)
            live = lax.broadcasted_iota(I32, (tm, EXPERT_FF), 0) < tr_ref[g]
            hid = jnp.where(live, gate * _sigmoid(gate) * up, 0.0).astype(BF16)
            y = jnp.dot(hid, wd_b[...], preferred_element_type=F32)

            @pl.when(g >= Y_SLOTS)
            def _():
                y_copy(g - Y_SLOTS, s).wait()

            _store_packed(ybuf, 0, tm, _pack_rows(y[:, :HALF], y[:, HALF:]), lead=s)
            y_copy(g, s).start(priority=1)
            return c

        lax.fori_loop(g0, g1, tile, 0)

    @pl.when(e == pl.num_programs(0) - 1)
    def _():
        for back in range(1, Y_SLOTS + 1):
            @pl.when(nv >= back)
            def _():
                y_copy(nv - back, (nv - back) % Y_SLOTS).wait()


def _expert_call(tile_start, tile_end, tile_rows, n_valid, xs, w_gate, w_up, w_down, n_rows):
    tm = TM_EXP
    w_map = lambda e, *_: (e, 0, 0)
    hbm = pl.BlockSpec(memory_space=pl.ANY)
    return pl.pallas_call(
        _expert_kernel,
        grid_spec=pltpu.PrefetchScalarGridSpec(
            num_scalar_prefetch=4,
            grid=(N_EXPERTS,),
            in_specs=[pl.BlockSpec((1, D_MODEL, EXPERT_FF), w_map), pl.BlockSpec((1, D_MODEL, EXPERT_FF), w_map),
                      pl.BlockSpec((1, EXPERT_FF, D_MODEL), w_map), hbm],
            out_specs=hbm,
            scratch_shapes=[pltpu.VMEM((X_SLOTS, tm * ROW_CHUNKS, LANES), U32),
                            pltpu.VMEM((Y_SLOTS, tm * ROW_CHUNKS, LANES), U32),
                            pltpu.VMEM((D_MODEL, EXPERT_FF), BF16), pltpu.VMEM((D_MODEL, EXPERT_FF), BF16),
                            pltpu.VMEM((EXPERT_FF, D_MODEL), BF16),
                            pltpu.SemaphoreType.DMA((X_SLOTS,)), pltpu.SemaphoreType.DMA((Y_SLOTS,))],
        ),
        out_shape=SDS((n_rows * ROW_CHUNKS, LANES), U32),
        compiler_params=_cparams("arbitrary"),
        name="experts",
    )(tile_start, tile_end, tile_rows, n_valid, w_gate, w_up, w_down, xs)


def _combine_kernel(dest_ref, dest_next_ref, wts_ref, h1_ref, ys_ref, wsg_ref, wsu_ref, wsd_ref, g2_ref, b2_ref,
                    o_ref, ybuf_ref, sem):
    i = pl.program_id(0)
    tn = h1_ref.shape[0]
    rows = TOP_K * tn
    slot = i % 2

    def gather(d_ref, s):
        base = s * rows

        def issue(t, c):
            srcs = [d_ref[kk, t] for kk in range(TOP_K)]
            for kk in range(TOP_K):
                pltpu.make_async_copy(_packed_row(ys_ref, srcs[kk]), _packed_row(ybuf_ref, base + kk * tn + t),
                                      sem.at[s]).start(priority=kk % 2)
            return c

        lax.fori_loop(0, tn, issue, 0)

    @pl.when(i == 0)
    def _():
        gather(dest_ref, 0)

    @pl.when(i + 1 < pl.num_programs(0))
    def _():
        gather(dest_next_ref, 1 - slot)

    h1 = h1_ref[...]
    hb = h1.astype(BF16)
    sg = jnp.dot(hb, wsg_ref[...], preferred_element_type=F32)
    su = jnp.dot(hb, wsu_ref[...], preferred_element_type=F32)
    ff = jnp.dot((sg * _sigmoid(sg) * su).astype(BF16), wsd_ref[...], preferred_element_type=F32)

    base = slot * rows
    pltpu.make_async_copy(
        ys_ref.at[pl.ds(0, rows * ROW_CHUNKS), :],
        ybuf_ref.at[pl.ds(pl.multiple_of(base * ROW_CHUNKS, rows * ROW_CHUNKS), rows * ROW_CHUNKS), :],
        sem.at[slot]).wait()

    eye = lax.broadcasted_iota(I32, (tn, tn), 0) == lax.broadcasted_iota(I32, (tn, tn), 1)
    w = wts_ref[...]
    for kk in range(TOP_K):
        wk = jnp.where(eye, w[kk:kk + 1, :], 0.0)
        wh = wk.astype(BF16)
        wl = (wk - wh.astype(F32)).astype(BF16)
        yk = _load_packed_bf16(ybuf_ref, pl.multiple_of((base + kk * tn) * ROW_CHUNKS, tn * ROW_CHUNKS), tn)
        ff = ff + jnp.dot(wh, yk, preferred_element_type=F32)
        ff = ff + jnp.dot(wl, yk, preferred_element_type=F32)
    o_ref[...] = _layer_norm(ALPHA * h1 + ff, g2_ref[...], b2_ref[...])


def _combine_call(dest, wts, h1, ys, wsg, wsu, wsd, g2, b2):
    t = h1.shape[0]
    tn = TN_COMB
    nsteps = t // tn
    col = lambda i: (0, i)
    col_next = lambda i: (0, jnp.minimum(i + 1, nsteps - 1))
    row = lambda i: (i, 0)
    fix = lambda i: (0, 0)
    return pl.pallas_call(
        _combine_kernel,
        grid=(nsteps,),
        in_specs=[pl.BlockSpec((TOP_K, tn), col, memory_space=pltpu.SMEM),
                  pl.BlockSpec((TOP_K, tn), col_next, memory_space=pltpu.SMEM),
                  pl.BlockSpec((TOP_K, tn), col),
                  pl.BlockSpec((tn, D_MODEL), row),
                  pl.BlockSpec(memory_space=pl.ANY),
                  pl.BlockSpec((D_MODEL, SHARED_FF), fix), pl.BlockSpec((D_MODEL, SHARED_FF), fix),
                  pl.BlockSpec((SHARED_FF, D_MODEL), fix),
                  pl.BlockSpec((1, D_MODEL), fix), pl.BlockSpec((1, D_MODEL), fix)],
        out_specs=pl.BlockSpec((tn, D_MODEL), row),
        out_shape=SDS((t, D_MODEL), F32),
        scratch_shapes=[pltpu.VMEM((2 * TOP_K * tn * ROW_CHUNKS, LANES), U32), pltpu.SemaphoreType.DMA((2,))],
        compiler_params=_cparams("arbitrary"),
        name="combine_ln2",
    )(dest, dest, wts, h1, ys, wsg, wsu, wsd, g2, b2)


def kernel(x, meta_tokens, ln_in_g, ln_in_b, rel_bias, w_in, conv_w, conv_b, conv_ln_g, conv_ln_b, sinks,
           w_out, ln1_g, ln1_b, w_router, router_bias, w_gate, w_up, w_down, ws_gate, ws_up, ws_down,
           ln2_g, ln2_b):
    nbatch, seq, d = x.shape
    assert d == D_MODEL and seq % TQ_PROJ == 0 and w_in.shape[0] == DEPTH
    t = nbatch * seq
    x2d = x.reshape(t, D_MODEL)
    vec = lambda a: a.reshape(1, -1).astype(F32)
    gin, bin_ = vec(ln_in_g), vec(ln_in_b)
    w_in_b = w_in[0].astype(BF16)

    q, k, v, u = _proj_call(x2d, gin, bin_, w_in_b, TQ_PROJ)
    meta_blk = jnp.concatenate([jnp.zeros((PAD_FRONT, D_MODEL), F32), meta_tokens.astype(F32)], axis=0)
    _, k_meta, v_meta, u_meta = _proj_call(meta_blk, gin, bin_, w_in_b, BLOCK)

    attn = _attn_call(q, k, v, k_meta, v_meta, _rel_bias_table(rel_bias), sinks[0].astype(F32),
                      nbatch, seq // BLOCK)

    u_halo = jnp.concatenate([jnp.zeros((CONV_HALO - N_META, CONV_CH), F32), u_meta[PAD_FRONT:]], axis=0)
    conv = _conv_call(u, u_halo, conv_w[0].astype(F32), vec(conv_b[0]), vec(conv_ln_g[0]), vec(conv_ln_b[0]),
                      nbatch, seq)

    w_out_b = w_out[0].astype(BF16)
    wr_t = w_router[0].astype(F32).T
    wr_hi = wr_t.astype(BF16)
    wr_lo = (wr_t - wr_hi.astype(F32)).astype(BF16)
    h1, h1rows, logits = _mix_call(x2d, attn, conv, gin, bin_, w_out_b[:ATTN_W], w_out_b[ATTN_W:],
                                   vec(ln1_g[0]), vec(ln1_b[0]), wr_hi, wr_lo)

    idx, wts, rank, cnt = _route_call(logits, router_bias[0].astype(F32).reshape(N_EXPERTS, 1))

    tm = TM_EXP
    n_tiles = (t * TOP_K) // tm + N_EXPERTS
    counts = cnt[:, 0]
    tiles_e = (counts + tm - 1) // tm
    tile_end = jnp.cumsum(tiles_e).astype(I32)
    tile_start = (tile_end - tiles_e).astype(I32)
    offs = tile_start * tm
    tile_id = jnp.arange(n_tiles, dtype=I32)
    lo = jnp.maximum(tile_id[:, None] * tm, offs[None, :])
    hi = jnp.minimum((tile_id[:, None] + 1) * tm, (offs + counts)[None, :])
    tile_rows = jnp.sum(jnp.clip(hi - lo, 0, tm), axis=1).astype(I32)
    n_valid = tile_end[-1:]

    dest = _dest_call(idx, rank, offs.astype(F32).reshape(N_EXPERTS, 1))
    xs = _dispatch_call(dest, h1rows, n_tiles * tm)
    ys = _expert_call(tile_start, tile_end, tile_rows, n_valid, xs, w_gate[0], w_up[0], w_down[0], n_tiles * tm)
    out = _combine_call(dest, wts, h1, ys, ws_gate[0].astype(BF16), ws_up[0].astype(BF16),
                        ws_down[0].astype(BF16), vec(ln2_g[0]), vec(ln2_b[0]))
    return out.reshape(nbatch, seq, D_MODEL)
```

```python
import functools
import math

import numpy as np
import jax
import jax.numpy as jnp
from jax import lax
from jax.experimental import pallas as pl
from jax.experimental.pallas import tpu as pltpu

F32 = jnp.float32
BF16 = jnp.bfloat16
I32 = jnp.int32
U32 = jnp.uint32
SDS = jax.ShapeDtypeStruct

D_MODEL = 1024
HALF = D_MODEL // 2
LANES = 128
SUBLANES = 8
ROW_CHUNKS = HALF // LANES
N_META = 16
HEAD_DIM = 64
N_Q_HEADS = 8
N_KV_HEADS = 2
GQA_GROUP = N_Q_HEADS // N_KV_HEADS
ATTN_W = N_Q_HEADS * HEAD_DIM
KV_W = N_KV_HEADS * HEAD_DIM
WINDOW = 128
BLOCK = 128
CONV_CH = D_MODEL - ATTN_W
CONV_K = 31
IN_W = ATTN_W + 2 * KV_W + 2 * CONV_CH
NUM_BUCKETS = 32
MAX_EXACT = NUM_BUCKETS // 2
REL_MAX_DIST = 128
N_EXPERTS = 256
TOP_K = 8
N_GROUPS = 8
GROUP_SIZE = N_EXPERTS // N_GROUPS
TOPK_GROUPS = 4
EXPERT_FF = 256
SHARED_FF = 256
ROUTED_SCALE = 2.5
DEPTH = 1
ALPHA = (2.0 * DEPTH) ** 0.25
LN_EPS = 1e-5
NEG = -1e30
PAD_FRONT = (-N_META) % BLOCK

VMEM_LIMIT = 48 * 1024 * 1024

TQ_PROJ = 512
T_CONV = 256
CONV_HALO = 32
R_CONV = 64
TQ_MIX = 256
TN_ROUTE = 256
TN_DISP = 256
TM_EXP = 256
X_SLOTS = 4
X_AHEAD = X_SLOTS - 1
Y_SLOTS = 4
TN_COMB = 128


def _cparams(*sem):
    return pltpu.CompilerParams(dimension_semantics=sem, vmem_limit_bytes=VMEM_LIMIT)


def _layer_norm(x, g, b):
    mu = jnp.mean(x, axis=-1, keepdims=True)
    xc = x - mu
    var = jnp.mean(xc * xc, axis=-1, keepdims=True)
    return xc * lax.rsqrt(var + LN_EPS) * g + b


def _sigmoid(x):
    return 1.0 / (1.0 + jnp.exp(-x))


def _pack_rows(lo_half, hi_half):
    lo = lax.bitcast_convert_type(lo_half.astype(BF16).astype(F32), U32)
    hi = lax.bitcast_convert_type(hi_half.astype(BF16).astype(F32), U32)
    return lax.shift_right_logical(lo, jnp.uint32(16)) | hi


def _unpack_rows(p):
    lo = lax.bitcast_convert_type(lax.shift_left(p, jnp.uint32(16)), F32)
    hi = lax.bitcast_convert_type(p & jnp.uint32(0xFFFF0000), F32)
    return lo, hi


def _chunk_index(start, j, n, lead):
    rows = pl.ds(start + j, n, stride=ROW_CHUNKS)
    return (rows, slice(None)) if lead is None else (lead, rows, slice(None))


def _store_packed(ref, start, n, packed, lead=None):
    for j in range(ROW_CHUNKS):
        ref[_chunk_index(start, j, n, lead)] = packed[:, j * LANES:(j + 1) * LANES]


def _load_packed_bf16(ref, start, n, lead=None):
    halves = [_unpack_rows(ref[_chunk_index(start, j, n, lead)]) for j in range(ROW_CHUNKS)]
    return jnp.concatenate([h[0] for h in halves] + [h[1] for h in halves], axis=1).astype(BF16)


def _packed_row(ref, token):
    return ref.at[pl.ds(pl.multiple_of(token * ROW_CHUNKS, ROW_CHUNKS), ROW_CHUNKS), :]


def _proj_kernel(x_ref, g_ref, b_ref, w_ref, q_ref, k_ref, v_ref, u_ref):
    h = _layer_norm(x_ref[...], g_ref[...], b_ref[...])
    p = jnp.dot(h.astype(BF16), w_ref[...], preferred_element_type=F32)
    q_ref[...] = (p[:, :ATTN_W] * (HEAD_DIM ** -0.5)).astype(BF16)
    k_ref[...] = p[:, ATTN_W:ATTN_W + KV_W].astype(BF16)
    v_ref[...] = p[:, ATTN_W + KV_W:ATTN_W + 2 * KV_W].astype(BF16)
    a = p[:, ATTN_W + 2 * KV_W:ATTN_W + 2 * KV_W + CONV_CH]
    gate = p[:, ATTN_W + 2 * KV_W + CONV_CH:]
    u_ref[...] = a * _sigmoid(gate)


def _proj_call(x2d, gin, bin_, w_in_b, tq):
    t = x2d.shape[0]
    row = lambda i: (i, 0)
    fix = lambda i: (0, 0)
    return pl.pallas_call(
        _proj_kernel,
        grid=(t // tq,),
        in_specs=[pl.BlockSpec((tq, D_MODEL), row), pl.BlockSpec((1, D_MODEL), fix),
                  pl.BlockSpec((1, D_MODEL), fix), pl.BlockSpec((D_MODEL, IN_W), fix)],
        out_specs=[pl.BlockSpec((tq, ATTN_W), row), pl.BlockSpec((tq, KV_W), row),
                   pl.BlockSpec((tq, KV_W), row), pl.BlockSpec((tq, CONV_CH), row)],
        out_shape=[SDS((t, ATTN_W), BF16), SDS((t, KV_W), BF16), SDS((t, KV_W), BF16), SDS((t, CONV_CH), F32)],
        compiler_params=_cparams("arbitrary"),
        name="ln_in_proj",
    )(x2d, gin, bin_, w_in_b)


def _attn_kernel(sinks_ref, q_ref, kc_ref, kp_ref, vc_ref, vp_ref, km_ref, vm_ref, bias_ref, o_ref):
    first = pl.program_id(1) == 0
    kp = jnp.where(first, km_ref[...], kp_ref[...])
    vp = jnp.where(first, vm_ref[...], vp_ref[...])
    k = jnp.concatenate([kp, kc_ref[...]], axis=0)
    v = jnp.concatenate([vp, vc_ref[...]], axis=0)
    col = lax.broadcasted_iota(I32, (BLOCK, 2 * BLOCK), 1)
    pad_bias = jnp.where(jnp.logical_and(first, col < PAD_FRONT), NEG, 0.0).astype(F32)
    q = q_ref[...]
    outs = []
    for h in range(N_Q_HEADS):
        g = h // GQA_GROUP
        qh = q[:, h * HEAD_DIM:(h + 1) * HEAD_DIM]
        kg = k[:, g * HEAD_DIM:(g + 1) * HEAD_DIM]
        vg = v[:, g * HEAD_DIM:(g + 1) * HEAD_DIM]
        s = lax.dot_general(qh, kg, (((1,), (1,)), ((), ())), preferred_element_type=F32)
        s = s + bias_ref[h] + pad_bias
        sink = sinks_ref[h]
        m = jnp.maximum(jnp.max(s, axis=-1, keepdims=True), sink)
        p = jnp.exp(s - m)
        den = jnp.sum(p, axis=-1, keepdims=True) + jnp.exp(sink - m)
        o = jnp.dot(p.astype(BF16), vg, preferred_element_type=F32)
        outs.append(o / den)
    o_ref[...] = jnp.concatenate(outs, axis=1).astype(BF16)


def _attn_call(q, k, v, k_meta, v_meta, bias, sinks, nbatch, nblk):
    t = q.shape[0]
    cur = lambda b, j: (b * nblk + j, 0)
    prev = lambda b, j: (jnp.maximum(b * nblk + j - 1, 0), 0)
    fix2 = lambda b, j: (0, 0)
    return pl.pallas_call(
        _attn_kernel,
        grid=(nbatch, nblk),
        in_specs=[pl.BlockSpec(memory_space=pltpu.SMEM),
                  pl.BlockSpec((BLOCK, ATTN_W), cur),
                  pl.BlockSpec((BLOCK, KV_W), cur), pl.BlockSpec((BLOCK, KV_W), prev),
                  pl.BlockSpec((BLOCK, KV_W), cur), pl.BlockSpec((BLOCK, KV_W), prev),
                  pl.BlockSpec((BLOCK, KV_W), fix2), pl.BlockSpec((BLOCK, KV_W), fix2),
                  pl.BlockSpec((N_Q_HEADS, BLOCK, 2 * BLOCK), lambda b, j: (0, 0, 0))],
        out_specs=pl.BlockSpec((BLOCK, ATTN_W), cur),
        out_shape=SDS((t, ATTN_W), BF16),
        compiler_params=_cparams("arbitrary", "arbitrary"),
        name="swa_attn",
    )(sinks, q, k, k, v, v, k_meta, v_meta, bias)


def _rel_bias_table(rel_bias):
    qi = np.arange(BLOCK, dtype=np.int32)[:, None]
    kj = np.arange(2 * BLOCK, dtype=np.int32)[None, :]
    dist = BLOCK + qi - kj
    dc = np.clip(dist, 0, WINDOW - 1)
    nf = np.maximum(dc, 1).astype(np.float32)
    large = MAX_EXACT + (np.log(nf / np.float32(MAX_EXACT)) / np.float32(math.log(REL_MAX_DIST / MAX_EXACT))
                         * np.float32(NUM_BUCKETS - MAX_EXACT)).astype(np.int32)
    large = np.minimum(large, NUM_BUCKETS - 1)
    bucket = np.where(dc < MAX_EXACT, dc, large)
    in_window = (dist >= 0) & (dist < WINDOW)
    onehot = (bucket.reshape(-1, 1) == np.arange(NUM_BUCKETS)[None, :]).astype(np.float32)
    bias = jnp.dot(jnp.asarray(onehot), rel_bias.astype(F32), precision=lax.Precision.HIGHEST)
    bias = jnp.transpose(bias.reshape(BLOCK, 2 * BLOCK, N_Q_HEADS), (2, 0, 1))
    return jnp.where(in_window[None], bias, NEG)


def _conv_kernel(uc_ref, up_ref, um_ref, w_ref, cb_ref, g_ref, b_ref, o_ref, s_ref, sh_ref):
    first = pl.program_id(1) == 0
    s_ref[0:CONV_HALO, :] = jnp.where(first, um_ref[...], up_ref[...])
    s_ref[CONV_HALO:CONV_HALO + T_CONV, :] = uc_ref[...]
    off = CONV_HALO - (CONV_K - 1)
    span = sh_ref.shape[1]
    for c in range(0, T_CONV, R_CONV):
        for p in range(1, SUBLANES):
            sh_ref[p] = s_ref[c + p:c + p + span, :]
        acc = jnp.zeros((R_CONV, CONV_CH), F32) + cb_ref[...]
        for kk in range(CONV_K):
            p, a = (off + kk) % SUBLANES, (off + kk) // SUBLANES * SUBLANES
            if p == 0:
                win = s_ref[c + a:c + a + R_CONV, :]
            else:
                win = sh_ref[p, a:a + R_CONV, :]
            acc = acc + win * w_ref[kk:kk + 1, :]
        y = _layer_norm(acc, g_ref[...], b_ref[...])
        o_ref[c:c + R_CONV, :] = (y * _sigmoid(y)).astype(BF16)


def _conv_call(u, u_meta_halo, conv_w, conv_b, g, b, nbatch, seq):
    t = u.shape[0]
    nj = seq // T_CONV
    per = T_CONV // CONV_HALO
    cur = lambda bb, j: (bb * nj + j, 0)
    prev = lambda bb, j: (jnp.maximum((bb * nj + j) * per - 1, 0), 0)
    fix = lambda bb, j: (0, 0)
    return pl.pallas_call(
        _conv_kernel,
        grid=(nbatch, nj),
        in_specs=[pl.BlockSpec((T_CONV, CONV_CH), cur), pl.BlockSpec((CONV_HALO, CONV_CH), prev),
                  pl.BlockSpec((CONV_HALO, CONV_CH), fix), pl.BlockSpec((CONV_K, CONV_CH), fix),
                  pl.BlockSpec((1, CONV_CH), fix), pl.BlockSpec((1, CONV_CH), fix), pl.BlockSpec((1, CONV_CH), fix)],
        out_specs=pl.BlockSpec((T_CONV, CONV_CH), cur),
        out_shape=SDS((t, CONV_CH), BF16),
        scratch_shapes=[pltpu.VMEM((CONV_HALO + T_CONV, CONV_CH), F32),
                        pltpu.VMEM((SUBLANES, R_CONV + CONV_HALO - SUBLANES, CONV_CH), F32)],
        compiler_params=_cparams("arbitrary", "arbitrary"),
        name="conv_ln",
    )(u, u, u_meta_halo, conv_w, conv_b, g, b)


def _mix_kernel(x_ref, at_ref, cv_ref, gin_ref, bin_ref, woa_ref, woc_ref, g1_ref, b1_ref,
                wrh_ref, wrl_ref, h1_ref, h1r_ref, lg_ref):
    h = _layer_norm(x_ref[...], gin_ref[...], bin_ref[...])
    mix = (jnp.dot(at_ref[...], woa_ref[...], preferred_element_type=F32)
           + jnp.dot(cv_ref[...], woc_ref[...], preferred_element_type=F32))
    h1 = _layer_norm(ALPHA * h + mix, g1_ref[...], b1_ref[...])
    h1_ref[...] = h1
    _store_packed(h1r_ref, 0, h1.shape[0], _pack_rows(h1[:, :HALF], h1[:, HALF:]))
    hh = h1.astype(BF16)
    hl = (h1 - hh.astype(F32)).astype(BF16)
    nt = (((1,), (1,)), ((), ()))
    lg = lax.dot_general(wrh_ref[...], hh, nt, preferred_element_type=F32)
    lg = lg + lax.dot_general(wrh_ref[...], hl, nt, preferred_element_type=F32)
    lg = lg + lax.dot_general(wrl_ref[...], hh, nt, preferred_element_type=F32)
    lg_ref[...] = lg


def _mix_call(x2d, attn, conv, gin, bin_, woa, woc, g1, b1, wrh, wrl):
    t = x2d.shape[0]
    tq = TQ_MIX
    row = lambda i: (i, 0)
    fix = lambda i: (0, 0)
    return pl.pallas_call(
        _mix_kernel,
        grid=(t // tq,),
        in_specs=[pl.BlockSpec((tq, D_MODEL), row), pl.BlockSpec((tq, ATTN_W), row), pl.BlockSpec((tq, CONV_CH), row),
                  pl.BlockSpec((1, D_MODEL), fix), pl.BlockSpec((1, D_MODEL), fix),
                  pl.BlockSpec((ATTN_W, D_MODEL), fix), pl.BlockSpec((CONV_CH, D_MODEL), fix),
                  pl.BlockSpec((1, D_MODEL), fix), pl.BlockSpec((1, D_MODEL), fix),
                  pl.BlockSpec((N_EXPERTS, D_MODEL), fix), pl.BlockSpec((N_EXPERTS, D_MODEL), fix)],
        out_specs=[pl.BlockSpec((tq, D_MODEL), row), pl.BlockSpec((tq * ROW_CHUNKS, LANES), row),
                   pl.BlockSpec((N_EXPERTS, tq), lambda i: (0, i))],
        out_shape=[SDS((t, D_MODEL), F32), SDS((t * ROW_CHUNKS, LANES), U32), SDS((N_EXPERTS, t), F32)],
        compiler_params=_cparams("arbitrary"),
        name="mix_ln1",
    )(x2d, attn, conv, gin, bin_, woa, woc, g1, b1, wrh, wrl)


def _first_argmax(x, rows, nrows):
    m = jnp.max(x, axis=0, keepdims=True)
    idx = jnp.min(jnp.where(x == m, rows, nrows), axis=0, keepdims=True)
    return m, idx


def _route_kernel(lg_ref, rb_ref, idx_ref, wts_ref, rank_ref, cnt_ref, carry_ref):
    tn = lg_ref.shape[1]

    @pl.when(pl.program_id(0) == 0)
    def _():
        carry_ref[...] = jnp.zeros_like(carry_ref)

    scores = _sigmoid(lg_ref[...])
    choice = scores + rb_ref[...]
    rows = lax.broadcasted_iota(I32, (N_EXPERTS, tn), 0)
    rows_g = lax.broadcasted_iota(I32, (GROUP_SIZE, tn), 0)
    rows_8 = lax.broadcasted_iota(I32, (N_GROUPS, tn), 0)

    gs = []
    for g in range(N_GROUPS):
        xg = choice[g * GROUP_SIZE:(g + 1) * GROUP_SIZE, :]
        m1, i1 = _first_argmax(xg, rows_g, GROUP_SIZE)
        m2 = jnp.max(jnp.where(rows_g == i1, -jnp.inf, xg), axis=0, keepdims=True)
        gs.append(m1 + m2)
    gsc = jnp.concatenate(gs, axis=0)
    gsel = jnp.zeros((N_GROUPS, tn), F32)
    for _ in range(TOPK_GROUPS):
        _, gi = _first_argmax(gsc, rows_8, N_GROUPS)
        hit = rows_8 == gi
        gsel = jnp.where(hit, 1.0, gsel)
        gsc = jnp.where(hit, -jnp.inf, gsc)
    emask = jnp.concatenate(
        [jnp.broadcast_to(gsel[g:g + 1, :], (GROUP_SIZE, tn)) for g in range(N_GROUPS)], axis=0)
    masked = jnp.where(emask > 0.5, choice, NEG)

    sel_all = jnp.zeros((N_EXPERTS, tn), F32)
    hits, idxs, ws = [], [], []
    for _ in range(TOP_K):
        _, ii = _first_argmax(masked, rows, N_EXPERTS)
        hit = rows == ii
        hits.append(hit)
        idxs.append(ii)
        ws.append(jnp.sum(jnp.where(hit, scores, 0.0), axis=0, keepdims=True))
        sel_all = jnp.where(hit, 1.0, sel_all)
        masked = jnp.where(hit, -jnp.inf, masked)
    wsum = ws[0]
    for w in ws[1:]:
        wsum = wsum + w
    idx_ref[...] = jnp.concatenate(idxs, axis=0)
    wts_ref[...] = jnp.concatenate([w / wsum * ROUTED_SCALE for w in ws], axis=0)

    r_i = lax.broadcasted_iota(I32, (tn, tn), 0)
    c_i = lax.broadcasted_iota(I32, (tn, tn), 1)
    upper = jnp.where(r_i < c_i, 1.0, 0.0).astype(BF16)
    sel_b = sel_all.astype(BF16)
    carry = carry_ref[...]
    before = jnp.dot(sel_b, upper, preferred_element_type=F32)
    before = before + jnp.concatenate([carry] * (tn // 128), axis=1)
    rank_ref[...] = jnp.concatenate(
        [jnp.sum(jnp.where(h, before, 0.0), axis=0, keepdims=True) for h in hits], axis=0).astype(I32)
    carry = carry + jnp.dot(sel_b, jnp.ones((tn, 128), BF16), preferred_element_type=F32)
    carry_ref[...] = carry
    cnt_ref[...] = carry.astype(I32)


def _route_call(lg, rbias):
    t = lg.shape[1]
    tn = TN_ROUTE
    col = lambda i: (0, i)
    return pl.pallas_call(
        _route_kernel,
        grid=(t // tn,),
        in_specs=[pl.BlockSpec((N_EXPERTS, tn), col), pl.BlockSpec((N_EXPERTS, 1), lambda i: (0, 0))],
        out_specs=[pl.BlockSpec((TOP_K, tn), col), pl.BlockSpec((TOP_K, tn), col), pl.BlockSpec((TOP_K, tn), col),
                   pl.BlockSpec((N_EXPERTS, 128), lambda i: (0, 0))],
        out_shape=[SDS((TOP_K, t), I32), SDS((TOP_K, t), F32), SDS((TOP_K, t), I32), SDS((N_EXPERTS, 128), I32)],
        scratch_shapes=[pltpu.VMEM((N_EXPERTS, 128), F32)],
        compiler_params=_cparams("arbitrary"),
        name="route",
    )(lg, rbias)


def _dest_kernel(idx_ref, rank_ref, offs_ref, dest_ref):
    tn = idx_ref.shape[1]
    rows = lax.broadcasted_iota(I32, (N_EXPERTS, tn), 0)
    offs = offs_ref[...]
    out = []
    for kk in range(TOP_K):
        hit = rows == idx_ref[kk:kk + 1, :]
        out.append(jnp.sum(jnp.where(hit, offs, 0.0), axis=0, keepdims=True))
    dest_ref[...] = jnp.concatenate(out, axis=0).astype(I32) + rank_ref[...]


def _dest_call(idx, rank, offs_col):
    t = idx.shape[1]
    tn = TN_ROUTE
    col = lambda i: (0, i)
    return pl.pallas_call(
        _dest_kernel,
        grid=(t // tn,),
        in_specs=[pl.BlockSpec((TOP_K, tn), col), pl.BlockSpec((TOP_K, tn), col),
                  pl.BlockSpec((N_EXPERTS, 1), lambda i: (0, 0))],
        out_specs=pl.BlockSpec((TOP_K, tn), col),
        out_shape=SDS((TOP_K, t), I32),
        compiler_params=_cparams("arbitrary"),
        name="dest",
    )(idx, rank, offs_col)


DISP_UNROLL = 2


def _dispatch_kernel(dest_ref, h_ref, xs_ref, sem):
    tn = dest_ref.shape[1]

    def issue(j, c):
        slots = [[dest_ref[kk, j * DISP_UNROLL + r] for kk in range(TOP_K)] for r in range(DISP_UNROLL)]
        for r in range(DISP_UNROLL):
            src = _packed_row(h_ref, j * DISP_UNROLL + r)
            for kk in range(TOP_K):
                pltpu.make_async_copy(src, _packed_row(xs_ref, slots[r][kk]), sem).start(priority=kk % 2)
        return c

    lax.fori_loop(0, tn // DISP_UNROLL, issue, 0)
    n = TOP_K * tn * ROW_CHUNKS
    pltpu.make_async_copy(xs_ref.at[pl.ds(0, n), :], xs_ref.at[pl.ds(0, n), :], sem).wait()


def _dispatch_call(dest, h1rows, n_rows):
    t = dest.shape[1]
    tn = TN_DISP
    return pl.pallas_call(
        _dispatch_kernel,
        grid=(t // tn,),
        in_specs=[pl.BlockSpec((TOP_K, tn), lambda i: (0, i), memory_space=pltpu.SMEM),
                  pl.BlockSpec((tn * ROW_CHUNKS, LANES), lambda i: (i, 0))],
        out_specs=pl.BlockSpec(memory_space=pl.ANY),
        out_shape=SDS((n_rows * ROW_CHUNKS, LANES), U32),
        scratch_shapes=[pltpu.SemaphoreType.DMA(())],
        compiler_params=_cparams("arbitrary"),
        name="dispatch",
    )(dest, h1rows)


def _expert_kernel(ts_ref, te_ref, tr_ref, nv_ref, wg_ref, wu_ref, wd_ref, xs_hbm, ys_hbm,
                   xbuf, ybuf, wg_b, wu_b, wd_b, xsem, ysem):
    e = pl.program_id(0)
    rows = xbuf.shape[1]
    tm = rows // ROW_CHUNKS
    g0, g1, nv = ts_ref[e], te_ref[e], nv_ref[0]

    def x_copy(g, s):
        return pltpu.make_async_copy(xs_hbm.at[pl.ds(pl.multiple_of(g * rows, rows), rows), :], xbuf.at[s], xsem.at[s])

    def y_copy(g, s):
        return pltpu.make_async_copy(ybuf.at[s], ys_hbm.at[pl.ds(pl.multiple_of(g * rows, rows), rows), :], ysem.at[s])

    @pl.when(e == 0)
    def _():
        for g in range(X_AHEAD):
            @pl.when(g < nv)
            def _():
                x_copy(g, g).start(priority=1)

    @pl.when(g1 > g0)
    def _():
        wg_b[...] = wg_ref[0].astype(BF16)
        wu_b[...] = wu_ref[0].astype(BF16)
        wd_b[...] = wd_ref[0].astype(BF16)

        def tile(g, c):
            s = g % Y_SLOTS
            sx = g % X_SLOTS
            x_copy(g, sx).wait()

            @pl.when(g + X_AHEAD < nv)
            def _():
                x_copy(g + X_AHEAD, (g + X_AHEAD) % X_SLOTS).start(priority=1)

            x = _load_packed_bf16(xbuf, 0, tm, lead=sx)
            gate = jnp.dot(x, wg_b[...], preferred_element_type=F32)
            up = jnp.dot(x, wu_b[...], preferred_element_type=F32)
            live = lax.broadcasted_iota(I32, (tm, EXPERT_FF), 0) < tr_ref[g]
            hid = jnp.where(live, gate * _sigmoid(gate) * up, 0.0).astype(BF16)
            y = jnp.dot(hid, wd_b[...], preferred_element_type=F32)

            @pl.when(g >= Y_SLOTS)
            def _():
                y_copy(g - Y_SLOTS, s).wait()

            _store_packed(ybuf, 0, tm, _pack_rows(y[:, :HALF], y[:, HALF:]), lead=s)
            y_copy(g, s).start(priority=1)
            return c

        lax.fori_loop(g0, g1, tile, 0)

    @pl.when(e == pl.num_programs(0) - 1)
    def _():
        for back in range(1, Y_SLOTS + 1):
            @pl.when(nv >= back)
            def _():
                y_copy(nv - back, (nv - back) % Y_SLOTS).wait()


def _expert_call(tile_start, tile_end, tile_rows, n_valid, xs, w_gate, w_up, w_down, n_rows):
    tm = TM_EXP
    w_map = lambda e, *_: (e, 0, 0)
    hbm = pl.BlockSpec(memory_space=pl.ANY)
    return pl.pallas_call(
        _expert_kernel,
        grid_spec=pltpu.PrefetchScalarGridSpec(
            num_scalar_prefetch=4,
            grid=(N_EXPERTS,),
            in_specs=[pl.BlockSpec((1, D_MODEL, EXPERT_FF), w_map), pl.BlockSpec((1, D_MODEL, EXPERT_FF), w_map),
                      pl.BlockSpec((1, EXPERT_FF, D_MODEL), w_map), hbm],
            out_specs=hbm,
            scratch_shapes=[pltpu.VMEM((X_SLOTS, tm * ROW_CHUNKS, LANES), U32),
                            pltpu.VMEM((Y_SLOTS, tm * ROW_CHUNKS, LANES), U32),
                            pltpu.VMEM((D_MODEL, EXPERT_FF), BF16), pltpu.VMEM((D_MODEL, EXPERT_FF), BF16),
                            pltpu.VMEM((EXPERT_FF, D_MODEL), BF16),
                            pltpu.SemaphoreType.DMA((X_SLOTS,)), pltpu.SemaphoreType.DMA((Y_SLOTS,))],
        ),
        out_shape=SDS((n_rows * ROW_CHUNKS, LANES), U32),
        compiler_params=_cparams("arbitrary"),
        name="experts",
    )(tile_start, tile_end, tile_rows, n_valid, w_gate, w_up, w_down, xs)


def _combine_kernel(dest_ref, dest_next_ref, wts_ref, h1_ref, ys_ref, wsg_ref, wsu_ref, wsd_ref, g2_ref, b2_ref,
                    o_ref, ybuf_ref, routed_ref, sem):
    i = pl.program_id(0)
    tn = h1_ref.shape[0]
    rows = TOP_K * tn
    slot = i % 2
    has_next = i + 1 < pl.num_programs(0)

    def issue_group(d_ref, s, j):
        srcs = [[d_ref[kk, j * SUBLANES + r] for kk in range(TOP_K)] for r in range(SUBLANES)]
        for r in range(SUBLANES):
            for kk in range(TOP_K):
                pltpu.make_async_copy(_packed_row(ys_ref, srcs[r][kk]),
                                      _packed_row(ybuf_ref, s * rows + kk * tn + j * SUBLANES + r),
                                      sem.at[s]).start(priority=kk % 2)

    @pl.when(i == 0)
    def _():
        def first(j, c):
            issue_group(dest_ref, 0, j)
            return c

        lax.fori_loop(0, tn // SUBLANES, first, 0)

    base = slot * rows
    pltpu.make_async_copy(
        ys_ref.at[pl.ds(0, rows * ROW_CHUNKS), :],
        ybuf_ref.at[pl.ds(pl.multiple_of(base * ROW_CHUNKS, rows * ROW_CHUNKS), rows * ROW_CHUNKS), :],
        sem.at[slot]).wait()

    def reduce_group(j):
        t0 = pl.multiple_of(j * SUBLANES, SUBLANES)
        acc = [jnp.zeros((SUBLANES, LANES), F32) for _ in range(2 * ROW_CHUNKS)]
        for kk in range(TOP_K):
            wk = wts_ref[pl.ds(t0, SUBLANES), kk:kk + 1]
            for cc in range(ROW_CHUNKS):
                start = (base + kk * tn + t0) * ROW_CHUNKS + cc
                lo, hi = _unpack_rows(ybuf_ref[pl.ds(start, SUBLANES, stride=ROW_CHUNKS), :])
                acc[cc] = acc[cc] + wk * lo
                acc[ROW_CHUNKS + cc] = acc[ROW_CHUNKS + cc] + wk * hi
        routed_ref[pl.ds(t0, SUBLANES), :] = jnp.concatenate(acc, axis=1)

    @pl.when(has_next)
    def _():
        def group(j, c):
            reduce_group(j)
            issue_group(dest_next_ref, 1 - slot, j)
            return c

        lax.fori_loop(0, tn // SUBLANES, group, 0)

    @pl.when(jnp.logical_not(has_next))
    def _():
        def group(j, c):
            reduce_group(j)
            return c

        lax.fori_loop(0, tn // SUBLANES, group, 0)

    h1 = h1_ref[...]
    hb = h1.astype(BF16)
    sg = jnp.dot(hb, wsg_ref[...], preferred_element_type=F32)
    su = jnp.dot(hb, wsu_ref[...], preferred_element_type=F32)
    ff = jnp.dot((sg * _sigmoid(sg) * su).astype(BF16), wsd_ref[...], preferred_element_type=F32)
    o_ref[...] = _layer_norm(ALPHA * h1 + ff + routed_ref[...], g2_ref[...], b2_ref[...])


def _combine_call(dest, wts, h1, ys, wsg, wsu, wsd, g2, b2):
    t = h1.shape[0]
    tn = TN_COMB
    nsteps = t // tn
    col = lambda i: (0, i)
    col_next = lambda i: (0, jnp.minimum(i + 1, nsteps - 1))
    row = lambda i: (i, 0)
    fix = lambda i: (0, 0)
    return pl.pallas_call(
        _combine_kernel,
        grid=(nsteps,),
        in_specs=[pl.BlockSpec((TOP_K, tn), col, memory_space=pltpu.SMEM),
                  pl.BlockSpec((TOP_K, tn), col_next, memory_space=pltpu.SMEM),
                  pl.BlockSpec((tn, TOP_K), row),
                  pl.BlockSpec((tn, D_MODEL), row),
                  pl.BlockSpec(memory_space=pl.ANY),
                  pl.BlockSpec((D_MODEL, SHARED_FF), fix), pl.BlockSpec((D_MODEL, SHARED_FF), fix),
                  pl.BlockSpec((SHARED_FF, D_MODEL), fix),
                  pl.BlockSpec((1, D_MODEL), fix), pl.BlockSpec((1, D_MODEL), fix)],
        out_specs=pl.BlockSpec((tn, D_MODEL), row),
        out_shape=SDS((t, D_MODEL), F32),
        scratch_shapes=[pltpu.VMEM((2 * TOP_K * tn * ROW_CHUNKS, LANES), U32), pltpu.VMEM((tn, D_MODEL), F32),
                        pltpu.SemaphoreType.DMA((2,))],
        compiler_params=_cparams("arbitrary"),
        name="combine_ln2",
    )(dest, dest, wts.T, h1, ys, wsg, wsu, wsd, g2, b2)


def kernel(x, meta_tokens, ln_in_g, ln_in_b, rel_bias, w_in, conv_w, conv_b, conv_ln_g, conv_ln_b, sinks,
           w_out, ln1_g, ln1_b, w_router, router_bias, w_gate, w_up, w_down, ws_gate, ws_up, ws_down,
           ln2_g, ln2_b):
    nbatch, seq, d = x.shape
    assert d == D_MODEL and seq % TQ_PROJ == 0 and w_in.shape[0] == DEPTH
    t = nbatch * seq
    x2d = x.reshape(t, D_MODEL)
    vec = lambda a: a.reshape(1, -1).astype(F32)
    gin, bin_ = vec(ln_in_g), vec(ln_in_b)
    w_in_b = w_in[0].astype(BF16)

    q, k, v, u = _proj_call(x2d, gin, bin_, w_in_b, TQ_PROJ)
    meta_blk = jnp.concatenate([jnp.zeros((PAD_FRONT, D_MODEL), F32), meta_tokens.astype(F32)], axis=0)
    _, k_meta, v_meta, u_meta = _proj_call(meta_blk, gin, bin_, w_in_b, BLOCK)

    attn = _attn_call(q, k, v, k_meta, v_meta, _rel_bias_table(rel_bias), sinks[0].astype(F32),
                      nbatch, seq // BLOCK)

    u_halo = jnp.concatenate([jnp.zeros((CONV_HALO - N_META, CONV_CH), F32), u_meta[PAD_FRONT:]], axis=0)
    conv = _conv_call(u, u_halo, conv_w[0].astype(F32), vec(conv_b[0]), vec(conv_ln_g[0]), vec(conv_ln_b[0]),
                      nbatch, seq)

    w_out_b = w_out[0].astype(BF16)
    wr_t = w_router[0].astype(F32).T
    wr_hi = wr_t.astype(BF16)
    wr_lo = (wr_t - wr_hi.astype(F32)).astype(BF16)
    h1, h1rows, logits = _mix_call(x2d, attn, conv, gin, bin_, w_out_b[:ATTN_W], w_out_b[ATTN_W:],
                                   vec(ln1_g[0]), vec(ln1_b[0]), wr_hi, wr_lo)

    idx, wts, rank, cnt = _route_call(logits, router_bias[0].astype(F32).reshape(N_EXPERTS, 1))

    tm = TM_EXP
    n_tiles = (t * TOP_K) // tm + N_EXPERTS
    counts = cnt[:, 0]
    tiles_e = (counts + tm - 1) // tm
    tile_end = jnp.cumsum(tiles_e).astype(I32)
    tile_start = (tile_end - tiles_e).astype(I32)
    offs = tile_start * tm
    tile_id = jnp.arange(n_tiles, dtype=I32)
    lo = jnp.maximum(tile_id[:, None] * tm, offs[None, :])
    hi = jnp.minimum((tile_id[:, None] + 1) * tm, (offs + counts)[None, :])
    tile_rows = jnp.sum(jnp.clip(hi - lo, 0, tm), axis=1).astype(I32)
    n_valid = tile_end[-1:]

    dest = _dest_call(idx, rank, offs.astype(F32).reshape(N_EXPERTS, 1))
    xs = _dispatch_call(dest, h1rows, n_tiles * tm)
    ys = _expert_call(tile_start, tile_end, tile_rows, n_valid, xs, w_gate[0], w_up[0], w_down[0], n_tiles * tm)
    out = _combine_call(dest, wts, h1, ys, ws_gate[0].astype(BF16), ws_up[0].astype(BF16),
                        ws_down[0].astype(BF16), vec(ln2_g[0]), vec(ln2_b[0]))
    return out.reshape(nbatch, seq, D_MODEL)
```

```python
import functools
import math

import numpy as np
import jax
import jax.numpy as jnp
from jax import lax
from jax.experimental import pallas as pl
from jax.experimental.pallas import tpu as pltpu

F32 = jnp.float32
BF16 = jnp.bfloat16
I32 = jnp.int32
U32 = jnp.uint32
SDS = jax.ShapeDtypeStruct

D_MODEL = 1024
HALF = D_MODEL // 2
LANES = 128
SUBLANES = 8
ROW_CHUNKS = HALF // LANES
N_META = 16
HEAD_DIM = 64
N_Q_HEADS = 8
N_KV_HEADS = 2
GQA_GROUP = N_Q_HEADS // N_KV_HEADS
ATTN_W = N_Q_HEADS * HEAD_DIM
KV_W = N_KV_HEADS * HEAD_DIM
WINDOW = 128
BLOCK = 128
CONV_CH = D_MODEL - ATTN_W
CONV_K = 31
IN_W = ATTN_W + 2 * KV_W + 2 * CONV_CH
NUM_BUCKETS = 32
MAX_EXACT = NUM_BUCKETS // 2
REL_MAX_DIST = 128
N_EXPERTS = 256
TOP_K = 8
N_GROUPS = 8
GROUP_SIZE = N_EXPERTS // N_GROUPS
TOPK_GROUPS = 4
EXPERT_FF = 256
SHARED_FF = 256
ROUTED_SCALE = 2.5
DEPTH = 1
ALPHA = (2.0 * DEPTH) ** 0.25
LN_EPS = 1e-5
NEG = -1e30
PAD_FRONT = (-N_META) % BLOCK

VMEM_LIMIT = 48 * 1024 * 1024

TQ_PROJ = 512
T_CONV = 256
CONV_HALO = 32
R_CONV = 64
TQ_MIX = 256
TN_ROUTE = 256
TN_DISP = 256
TM_EXP = 256
X_SLOTS = 8
X_AHEAD = 4
Y_SLOTS = 4
TN_COMB = 128


def _cparams(*sem):
    return pltpu.CompilerParams(dimension_semantics=sem, vmem_limit_bytes=VMEM_LIMIT)


def _layer_norm(x, g, b):
    mu = jnp.mean(x, axis=-1, keepdims=True)
    xc = x - mu
    var = jnp.mean(xc * xc, axis=-1, keepdims=True)
    return xc * lax.rsqrt(var + LN_EPS) * g + b


def _sigmoid(x):
    return 1.0 / (1.0 + jnp.exp(-x))


def _pack_rows(lo_half, hi_half):
    lo = lax.bitcast_convert_type(lo_half.astype(BF16).astype(F32), U32)
    hi = lax.bitcast_convert_type(hi_half.astype(BF16).astype(F32), U32)
    return lax.shift_right_logical(lo, jnp.uint32(16)) | hi


def _unpack_rows(p):
    lo = lax.bitcast_convert_type(lax.shift_left(p, jnp.uint32(16)), F32)
    hi = lax.bitcast_convert_type(p & jnp.uint32(0xFFFF0000), F32)
    return lo, hi


def _chunk_index(start, j, n, lead):
    rows = pl.ds(start + j, n, stride=ROW_CHUNKS)
    return (rows, slice(None)) if lead is None else (lead, rows, slice(None))


def _store_packed(ref, start, n, packed, lead=None):
    for j in range(ROW_CHUNKS):
        ref[_chunk_index(start, j, n, lead)] = packed[:, j * LANES:(j + 1) * LANES]


def _load_packed_bf16(ref, start, n, lead=None):
    halves = [_unpack_rows(ref[_chunk_index(start, j, n, lead)]) for j in range(ROW_CHUNKS)]
    return jnp.concatenate([h[0] for h in halves] + [h[1] for h in halves], axis=1).astype(BF16)


def _packed_row(ref, token):
    return ref.at[pl.ds(pl.multiple_of(token * ROW_CHUNKS, ROW_CHUNKS), ROW_CHUNKS), :]


def _proj_kernel(x_ref, g_ref, b_ref, w_ref, q_ref, k_ref, v_ref, u_ref):
    h = _layer_norm(x_ref[...], g_ref[...], b_ref[...])
    p = jnp.dot(h.astype(BF16), w_ref[...], preferred_element_type=F32)
    q_ref[...] = (p[:, :ATTN_W] * (HEAD_DIM ** -0.5)).astype(BF16)
    k_ref[...] = p[:, ATTN_W:ATTN_W + KV_W].astype(BF16)
    v_ref[...] = p[:, ATTN_W + KV_W:ATTN_W + 2 * KV_W].astype(BF16)
    a = p[:, ATTN_W + 2 * KV_W:ATTN_W + 2 * KV_W + CONV_CH]
    gate = p[:, ATTN_W + 2 * KV_W + CONV_CH:]
    u_ref[...] = a * _sigmoid(gate)


def _proj_call(x2d, gin, bin_, w_in_b, tq):
    t = x2d.shape[0]
    row = lambda i: (i, 0)
    fix = lambda i: (0, 0)
    return pl.pallas_call(
        _proj_kernel,
        grid=(t // tq,),
        in_specs=[pl.BlockSpec((tq, D_MODEL), row), pl.BlockSpec((1, D_MODEL), fix),
                  pl.BlockSpec((1, D_MODEL), fix), pl.BlockSpec((D_MODEL, IN_W), fix)],
        out_specs=[pl.BlockSpec((tq, ATTN_W), row), pl.BlockSpec((tq, KV_W), row),
                   pl.BlockSpec((tq, KV_W), row), pl.BlockSpec((tq, CONV_CH), row)],
        out_shape=[SDS((t, ATTN_W), BF16), SDS((t, KV_W), BF16), SDS((t, KV_W), BF16), SDS((t, CONV_CH), F32)],
        compiler_params=_cparams("arbitrary"),
        name="ln_in_proj",
    )(x2d, gin, bin_, w_in_b)


def _attn_kernel(sinks_ref, q_ref, kc_ref, kp_ref, vc_ref, vp_ref, km_ref, vm_ref, bias_ref, o_ref):
    first = pl.program_id(1) == 0
    kp = jnp.where(first, km_ref[...], kp_ref[...])
    vp = jnp.where(first, vm_ref[...], vp_ref[...])
    k = jnp.concatenate([kp, kc_ref[...]], axis=0)
    v = jnp.concatenate([vp, vc_ref[...]], axis=0)
    col = lax.broadcasted_iota(I32, (BLOCK, 2 * BLOCK), 1)
    pad_bias = jnp.where(jnp.logical_and(first, col < PAD_FRONT), NEG, 0.0).astype(F32)
    q = q_ref[...]
    outs = []
    for h in range(N_Q_HEADS):
        g = h // GQA_GROUP
        qh = q[:, h * HEAD_DIM:(h + 1) * HEAD_DIM]
        kg = k[:, g * HEAD_DIM:(g + 1) * HEAD_DIM]
        vg = v[:, g * HEAD_DIM:(g + 1) * HEAD_DIM]
        s = lax.dot_general(qh, kg, (((1,), (1,)), ((), ())), preferred_element_type=F32)
        s = s + bias_ref[h] + pad_bias
        sink = sinks_ref[h]
        m = jnp.maximum(jnp.max(s, axis=-1, keepdims=True), sink)
        p = jnp.exp(s - m)
        den = jnp.sum(p, axis=-1, keepdims=True) + jnp.exp(sink - m)
        o = jnp.dot(p.astype(BF16), vg, preferred_element_type=F32)
        outs.append(o / den)
    o_ref[...] = jnp.concatenate(outs, axis=1).astype(BF16)


def _attn_call(q, k, v, k_meta, v_meta, bias, sinks, nbatch, nblk):
    t = q.shape[0]
    cur = lambda b, j: (b * nblk + j, 0)
    prev = lambda b, j: (jnp.maximum(b * nblk + j - 1, 0), 0)
    fix2 = lambda b, j: (0, 0)
    return pl.pallas_call(
        _attn_kernel,
        grid=(nbatch, nblk),
        in_specs=[pl.BlockSpec(memory_space=pltpu.SMEM),
                  pl.BlockSpec((BLOCK, ATTN_W), cur),
                  pl.BlockSpec((BLOCK, KV_W), cur), pl.BlockSpec((BLOCK, KV_W), prev),
                  pl.BlockSpec((BLOCK, KV_W), cur), pl.BlockSpec((BLOCK, KV_W), prev),
                  pl.BlockSpec((BLOCK, KV_W), fix2), pl.BlockSpec((BLOCK, KV_W), fix2),
                  pl.BlockSpec((N_Q_HEADS, BLOCK, 2 * BLOCK), lambda b, j: (0, 0, 0))],
        out_specs=pl.BlockSpec((BLOCK, ATTN_W), cur),
        out_shape=SDS((t, ATTN_W), BF16),
        compiler_params=_cparams("arbitrary", "arbitrary"),
        name="swa_attn",
    )(sinks, q, k, k, v, v, k_meta, v_meta, bias)


def _rel_bias_table(rel_bias):
    qi = np.arange(BLOCK, dtype=np.int32)[:, None]
    kj = np.arange(2 * BLOCK, dtype=np.int32)[None, :]
    dist = BLOCK + qi - kj
    dc = np.clip(dist, 0, WINDOW - 1)
    nf = np.maximum(dc, 1).astype(np.float32)
    large = MAX_EXACT + (np.log(nf / np.float32(MAX_EXACT)) / np.float32(math.log(REL_MAX_DIST / MAX_EXACT))
                         * np.float32(NUM_BUCKETS - MAX_EXACT)).astype(np.int32)
    large = np.minimum(large, NUM_BUCKETS - 1)
    bucket = np.where(dc < MAX_EXACT, dc, large)
    in_window = (dist >= 0) & (dist < WINDOW)
    onehot = (bucket.reshape(-1, 1) == np.arange(NUM_BUCKETS)[None, :]).astype(np.float32)
    bias = jnp.dot(jnp.asarray(onehot), rel_bias.astype(F32), precision=lax.Precision.HIGHEST)
    bias = jnp.transpose(bias.reshape(BLOCK, 2 * BLOCK, N_Q_HEADS), (2, 0, 1))
    return jnp.where(in_window[None], bias, NEG)


def _conv_kernel(uc_ref, up_ref, um_ref, w_ref, cb_ref, g_ref, b_ref, o_ref, s_ref, sh_ref):
    first = pl.program_id(1) == 0
    s_ref[0:CONV_HALO, :] = jnp.where(first, um_ref[...], up_ref[...])
    s_ref[CONV_HALO:CONV_HALO + T_CONV, :] = uc_ref[...]
    off = CONV_HALO - (CONV_K - 1)
    span = sh_ref.shape[1]
    for c in range(0, T_CONV, R_CONV):
        for p in range(1, SUBLANES):
            sh_ref[p] = s_ref[c + p:c + p + span, :]
        acc = jnp.zeros((R_CONV, CONV_CH), F32) + cb_ref[...]
        for kk in range(CONV_K):
            p, a = (off + kk) % SUBLANES, (off + kk) // SUBLANES * SUBLANES
            if p == 0:
                win = s_ref[c + a:c + a + R_CONV, :]
            else:
                win = sh_ref[p, a:a + R_CONV, :]
            acc = acc + win * w_ref[kk:kk + 1, :]
        y = _layer_norm(acc, g_ref[...], b_ref[...])
        o_ref[c:c + R_CONV, :] = (y * _sigmoid(y)).astype(BF16)


def _conv_call(u, u_meta_halo, conv_w, conv_b, g, b, nbatch, seq):
    t = u.shape[0]
    nj = seq // T_CONV
    per = T_CONV // CONV_HALO
    cur = lambda bb, j: (bb * nj + j, 0)
    prev = lambda bb, j: (jnp.maximum((bb * nj + j) * per - 1, 0), 0)
    fix = lambda bb, j: (0, 0)
    return pl.pallas_call(
        _conv_kernel,
        grid=(nbatch, nj),
        in_specs=[pl.BlockSpec((T_CONV, CONV_CH), cur), pl.BlockSpec((CONV_HALO, CONV_CH), prev),
                  pl.BlockSpec((CONV_HALO, CONV_CH), fix), pl.BlockSpec((CONV_K, CONV_CH), fix),
                  pl.BlockSpec((1, CONV_CH), fix), pl.BlockSpec((1, CONV_CH), fix), pl.BlockSpec((1, CONV_CH), fix)],
        out_specs=pl.BlockSpec((T_CONV, CONV_CH), cur),
        out_shape=SDS((t, CONV_CH), BF16),
        scratch_shapes=[pltpu.VMEM((CONV_HALO + T_CONV, CONV_CH), F32),
                        pltpu.VMEM((SUBLANES, R_CONV + CONV_HALO - SUBLANES, CONV_CH), F32)],
        compiler_params=_cparams("arbitrary", "arbitrary"),
        name="conv_ln",
    )(u, u, u_meta_halo, conv_w, conv_b, g, b)


def _mix_kernel(x_ref, at_ref, cv_ref, gin_ref, bin_ref, woa_ref, woc_ref, g1_ref, b1_ref,
                wrh_ref, wrl_ref, h1_ref, h1r_ref, lg_ref):
    h = _layer_norm(x_ref[...], gin_ref[...], bin_ref[...])
    mix = (jnp.dot(at_ref[...], woa_ref[...], preferred_element_type=F32)
           + jnp.dot(cv_ref[...], woc_ref[...], preferred_element_type=F32))
    h1 = _layer_norm(ALPHA * h + mix, g1_ref[...], b1_ref[...])
    h1_ref[...] = h1
    _store_packed(h1r_ref, 0, h1.shape[0], _pack_rows(h1[:, :HALF], h1[:, HALF:]))
    hh = h1.astype(BF16)
    hl = (h1 - hh.astype(F32)).astype(BF16)
    nt = (((1,), (1,)), ((), ()))
    lg = lax.dot_general(wrh_ref[...], hh, nt, preferred_element_type=F32)
    lg = lg + lax.dot_general(wrh_ref[...], hl, nt, preferred_element_type=F32)
    lg = lg + lax.dot_general(wrl_ref[...], hh, nt, preferred_element_type=F32)
    lg_ref[...] = lg


def _mix_call(x2d, attn, conv, gin, bin_, woa, woc, g1, b1, wrh, wrl):
    t = x2d.shape[0]
    tq = TQ_MIX
    row = lambda i: (i, 0)
    fix = lambda i: (0, 0)
    return pl.pallas_call(
        _mix_kernel,
        grid=(t // tq,),
        in_specs=[pl.BlockSpec((tq, D_MODEL), row), pl.BlockSpec((tq, ATTN_W), row), pl.BlockSpec((tq, CONV_CH), row),
                  pl.BlockSpec((1, D_MODEL), fix), pl.BlockSpec((1, D_MODEL), fix),
                  pl.BlockSpec((ATTN_W, D_MODEL), fix), pl.BlockSpec((CONV_CH, D_MODEL), fix),
                  pl.BlockSpec((1, D_MODEL), fix), pl.BlockSpec((1, D_MODEL), fix),
                  pl.BlockSpec((N_EXPERTS, D_MODEL), fix), pl.BlockSpec((N_EXPERTS, D_MODEL), fix)],
        out_specs=[pl.BlockSpec((tq, D_MODEL), row), pl.BlockSpec((tq * ROW_CHUNKS, LANES), row),
                   pl.BlockSpec((N_EXPERTS, tq), lambda i: (0, i))],
        out_shape=[SDS((t, D_MODEL), F32), SDS((t * ROW_CHUNKS, LANES), U32), SDS((N_EXPERTS, t), F32)],
        compiler_params=_cparams("arbitrary"),
        name="mix_ln1",
    )(x2d, attn, conv, gin, bin_, woa, woc, g1, b1, wrh, wrl)


def _first_argmax(x, rows, nrows):
    m = jnp.max(x, axis=0, keepdims=True)
    idx = jnp.min(jnp.where(x == m, rows, nrows), axis=0, keepdims=True)
    return m, idx


def _route_kernel(lg_ref, rb_ref, idx_ref, wts_ref, rank_ref, cnt_ref, carry_ref):
    tn = lg_ref.shape[1]

    @pl.when(pl.program_id(0) == 0)
    def _():
        carry_ref[...] = jnp.zeros_like(carry_ref)

    scores = _sigmoid(lg_ref[...])
    choice = scores + rb_ref[...]
    rows = lax.broadcasted_iota(I32, (N_EXPERTS, tn), 0)
    rows_g = lax.broadcasted_iota(I32, (GROUP_SIZE, tn), 0)
    rows_8 = lax.broadcasted_iota(I32, (N_GROUPS, tn), 0)

    gs = []
    for g in range(N_GROUPS):
        xg = choice[g * GROUP_SIZE:(g + 1) * GROUP_SIZE, :]
        m1, i1 = _first_argmax(xg, rows_g, GROUP_SIZE)
        m2 = jnp.max(jnp.where(rows_g == i1, -jnp.inf, xg), axis=0, keepdims=True)
        gs.append(m1 + m2)
    gsc = jnp.concatenate(gs, axis=0)
    gsel = jnp.zeros((N_GROUPS, tn), F32)
    for _ in range(TOPK_GROUPS):
        _, gi = _first_argmax(gsc, rows_8, N_GROUPS)
        hit = rows_8 == gi
        gsel = jnp.where(hit, 1.0, gsel)
        gsc = jnp.where(hit, -jnp.inf, gsc)
    emask = jnp.concatenate(
        [jnp.broadcast_to(gsel[g:g + 1, :], (GROUP_SIZE, tn)) for g in range(N_GROUPS)], axis=0)
    masked = jnp.where(emask > 0.5, choice, NEG)

    sel_all = jnp.zeros((N_EXPERTS, tn), F32)
    hits, idxs, ws = [], [], []
    for _ in range(TOP_K):
        _, ii = _first_argmax(masked, rows, N_EXPERTS)
        hit = rows == ii
        hits.append(hit)
        idxs.append(ii)
        ws.append(jnp.sum(jnp.where(hit, scores, 0.0), axis=0, keepdims=True))
        sel_all = jnp.where(hit, 1.0, sel_all)
        masked = jnp.where(hit, -jnp.inf, masked)
    wsum = ws[0]
    for w in ws[1:]:
        wsum = wsum + w
    idx_ref[...] = jnp.concatenate(idxs, axis=0)
    wts_ref[...] = jnp.concatenate([w / wsum * ROUTED_SCALE for w in ws], axis=0)

    r_i = lax.broadcasted_iota(I32, (tn, tn), 0)
    c_i = lax.broadcasted_iota(I32, (tn, tn), 1)
    upper = jnp.where(r_i < c_i, 1.0, 0.0).astype(BF16)
    sel_b = sel_all.astype(BF16)
    carry = carry_ref[...]
    before = jnp.dot(sel_b, upper, preferred_element_type=F32)
    before = before + jnp.concatenate([carry] * (tn // 128), axis=1)
    rank_ref[...] = jnp.concatenate(
        [jnp.sum(jnp.where(h, before, 0.0), axis=0, keepdims=True) for h in hits], axis=0).astype(I32)
    carry = carry + jnp.dot(sel_b, jnp.ones((tn, 128), BF16), preferred_element_type=F32)
    carry_ref[...] = carry
    cnt_ref[...] = carry.astype(I32)


def _route_call(lg, rbias):
    t = lg.shape[1]
    tn = TN_ROUTE
    col = lambda i: (0, i)
    return pl.pallas_call(
        _route_kernel,
        grid=(t // tn,),
        in_specs=[pl.BlockSpec((N_EXPERTS, tn), col), pl.BlockSpec((N_EXPERTS, 1), lambda i: (0, 0))],
        out_specs=[pl.BlockSpec((TOP_K, tn), col), pl.BlockSpec((TOP_K, tn), col), pl.BlockSpec((TOP_K, tn), col),
                   pl.BlockSpec((N_EXPERTS, 128), lambda i: (0, 0))],
        out_shape=[SDS((TOP_K, t), I32), SDS((TOP_K, t), F32), SDS((TOP_K, t), I32), SDS((N_EXPERTS, 128), I32)],
        scratch_shapes=[pltpu.VMEM((N_EXPERTS, 128), F32)],
        compiler_params=_cparams("arbitrary"),
        name="route",
    )(lg, rbias)


def _dest_kernel(idx_ref, rank_ref, offs_ref, dest_ref):
    tn = idx_ref.shape[1]
    rows = lax.broadcasted_iota(I32, (N_EXPERTS, tn), 0)
    offs = offs_ref[...]
    out = []
    for kk in range(TOP_K):
        hit = rows == idx_ref[kk:kk + 1, :]
        out.append(jnp.sum(jnp.where(hit, offs, 0.0), axis=0, keepdims=True))
    dest_ref[...] = jnp.concatenate(out, axis=0).astype(I32) + rank_ref[...]


def _dest_call(idx, rank, offs_col):
    t = idx.shape[1]
    tn = TN_ROUTE
    col = lambda i: (0, i)
    return pl.pallas_call(
        _dest_kernel,
        grid=(t // tn,),
        in_specs=[pl.BlockSpec((TOP_K, tn), col), pl.BlockSpec((TOP_K, tn), col),
                  pl.BlockSpec((N_EXPERTS, 1), lambda i: (0, 0))],
        out_specs=pl.BlockSpec((TOP_K, tn), col),
        out_shape=SDS((TOP_K, t), I32),
        compiler_params=_cparams("arbitrary"),
        name="dest",
    )(idx, rank, offs_col)


DISP_UNROLL = 2


def _dispatch_kernel(dest_ref, h_ref, xs_ref, sem):
    tn = dest_ref.shape[1]

    def issue(j, c):
        slots = [[dest_ref[kk, j * DISP_UNROLL + r] for kk in range(TOP_K)] for r in range(DISP_UNROLL)]
        for r in range(DISP_UNROLL):
            src = _packed_row(h_ref, j * DISP_UNROLL + r)
            for kk in range(TOP_K):
                pltpu.make_async_copy(src, _packed_row(xs_ref, slots[r][kk]), sem).start(priority=kk % 2)
        return c

    lax.fori_loop(0, tn // DISP_UNROLL, issue, 0)
    n = TOP_K * tn * ROW_CHUNKS
    pltpu.make_async_copy(xs_ref.at[pl.ds(0, n), :], xs_ref.at[pl.ds(0, n), :], sem).wait()


def _dispatch_call(dest, h1rows, n_rows):
    t = dest.shape[1]
    tn = TN_DISP
    return pl.pallas_call(
        _dispatch_kernel,
        grid=(t // tn,),
        in_specs=[pl.BlockSpec((TOP_K, tn), lambda i: (0, i), memory_space=pltpu.SMEM),
                  pl.BlockSpec((tn * ROW_CHUNKS, LANES), lambda i: (i, 0))],
        out_specs=pl.BlockSpec(memory_space=pl.ANY),
        out_shape=SDS((n_rows * ROW_CHUNKS, LANES), U32),
        scratch_shapes=[pltpu.SemaphoreType.DMA(())],
        compiler_params=_cparams("arbitrary"),
        name="dispatch",
    )(dest, h1rows)


def _expert_kernel(ts_ref, te_ref, tr_ref, nv_ref, wg_ref, wu_ref, wd_ref, xs_hbm, ys_hbm,
                   xbuf, ybuf, wg_b, wu_b, wd_b, xsem, ysem):
    e = pl.program_id(0)
    rows = xbuf.shape[1]
    tm = rows // ROW_CHUNKS
    g0, g1, nv = ts_ref[e], te_ref[e], nv_ref[0]

    def x_copy(g):
        s = g % X_SLOTS
        return pltpu.make_async_copy(xs_hbm.at[pl.ds(pl.multiple_of(g * rows, rows), rows), :], xbuf.at[s], xsem.at[s])

    def y_copy(g):
        s = g % Y_SLOTS
        return pltpu.make_async_copy(ybuf.at[s], ys_hbm.at[pl.ds(pl.multiple_of(g * rows, rows), rows), :], ysem.at[s])

    def compute_tile(g):
        x = _load_packed_bf16(xbuf, 0, tm, lead=g % X_SLOTS)
        gate = jnp.dot(x, wg_b[...], preferred_element_type=F32)
        up = jnp.dot(x, wu_b[...], preferred_element_type=F32)
        live = lax.broadcasted_iota(I32, (tm, EXPERT_FF), 0) < tr_ref[g]
        hid = jnp.where(live, gate * _sigmoid(gate) * up, 0.0).astype(BF16)
        y = jnp.dot(hid, wd_b[...], preferred_element_type=F32)
        return _pack_rows(y[:, :HALF], y[:, HALF:])

    def run_tiles(g, n):
        for r in range(n):
            x_copy(g + r).wait()

            @pl.when(g + r + X_AHEAD < nv)
            def _():
                x_copy(g + r + X_AHEAD).start(priority=1)

            @pl.when(g + r >= Y_SLOTS)
            def _():
                y_copy(g + r - Y_SLOTS).wait()

        packed = [compute_tile(g + r) for r in range(n)]
        for r in range(n):
            _store_packed(ybuf, 0, tm, packed[r], lead=(g + r) % Y_SLOTS)
        for r in range(n):
            y_copy(g + r).start(priority=1)

    @pl.when(e == 0)
    def _():
        for g in range(X_AHEAD):
            @pl.when(g < nv)
            def _():
                x_copy(g).start(priority=1)

    @pl.when(g1 > g0)
    def _():
        wg_b[...] = wg_ref[0].astype(BF16)
        wu_b[...] = wu_ref[0].astype(BF16)
        wd_b[...] = wd_ref[0].astype(BF16)
        n_tiles = g1 - g0

        def pair(p, c):
            run_tiles(g0 + 2 * p, 2)
            return c

        lax.fori_loop(0, n_tiles // 2, pair, 0)

        @pl.when(n_tiles % 2 == 1)
        def _():
            run_tiles(g1 - 1, 1)

    @pl.when(e == pl.num_programs(0) - 1)
    def _():
        for back in range(1, Y_SLOTS + 1):
            @pl.when(nv >= back)
            def _():
                y_copy(nv - back).wait()


def _expert_call(tile_start, tile_end, tile_rows, n_valid, xs, w_gate, w_up, w_down, n_rows):
    tm = TM_EXP
    w_map = lambda e, *_: (e, 0, 0)
    hbm = pl.BlockSpec(memory_space=pl.ANY)
    return pl.pallas_call(
        _expert_kernel,
        grid_spec=pltpu.PrefetchScalarGridSpec(
            num_scalar_prefetch=4,
            grid=(N_EXPERTS,),
            in_specs=[pl.BlockSpec((1, D_MODEL, EXPERT_FF), w_map), pl.BlockSpec((1, D_MODEL, EXPERT_FF), w_map),
                      pl.BlockSpec((1, EXPERT_FF, D_MODEL), w_map), hbm],
            out_specs=hbm,
            scratch_shapes=[pltpu.VMEM((X_SLOTS, tm * ROW_CHUNKS, LANES), U32),
                            pltpu.VMEM((Y_SLOTS, tm * ROW_CHUNKS, LANES), U32),
                            pltpu.VMEM((D_MODEL, EXPERT_FF), BF16), pltpu.VMEM((D_MODEL, EXPERT_FF), BF16),
                            pltpu.VMEM((EXPERT_FF, D_MODEL), BF16),
                            pltpu.SemaphoreType.DMA((X_SLOTS,)), pltpu.SemaphoreType.DMA((Y_SLOTS,))],
        ),
        out_shape=SDS((n_rows * ROW_CHUNKS, LANES), U32),
        compiler_params=_cparams("arbitrary"),
        name="experts",
    )(tile_start, tile_end, tile_rows, n_valid, w_gate, w_up, w_down, xs)


def _combine_kernel(dest_ref, dest_next_ref, wts_ref, h1_ref, ys_ref, wsg_ref, wsu_ref, wsd_ref, g2_ref, b2_ref,
                    o_ref, ybuf_ref, routed_ref, sem):
    i = pl.program_id(0)
    tn = h1_ref.shape[0]
    rows = TOP_K * tn
    slot = i % 2
    has_next = i + 1 < pl.num_programs(0)

    def issue_group(d_ref, s, j):
        srcs = [[d_ref[kk, j * SUBLANES + r] for kk in range(TOP_K)] for r in range(SUBLANES)]
        for r in range(SUBLANES):
            for kk in range(TOP_K):
                pltpu.make_async_copy(_packed_row(ys_ref, srcs[r][kk]),
                                      _packed_row(ybuf_ref, s * rows + kk * tn + j * SUBLANES + r),
                                      sem.at[s]).start(priority=kk % 2)

    @pl.when(i == 0)
    def _():
        def first(j, c):
            issue_group(dest_ref, 0, j)
            return c

        lax.fori_loop(0, tn // SUBLANES, first, 0)

    base = slot * rows
    pltpu.make_async_copy(
        ys_ref.at[pl.ds(0, rows * ROW_CHUNKS), :],
        ybuf_ref.at[pl.ds(pl.multiple_of(base * ROW_CHUNKS, rows * ROW_CHUNKS), rows * ROW_CHUNKS), :],
        sem.at[slot]).wait()

    def reduce_group(j):
        t0 = pl.multiple_of(j * SUBLANES, SUBLANES)
        acc = [jnp.zeros((SUBLANES, LANES), F32) for _ in range(2 * ROW_CHUNKS)]
        for kk in range(TOP_K):
            wk = wts_ref[pl.ds(t0, SUBLANES), kk:kk + 1]
            for cc in range(ROW_CHUNKS):
                start = (base + kk * tn + t0) * ROW_CHUNKS + cc
                lo, hi = _unpack_rows(ybuf_ref[pl.ds(start, SUBLANES, stride=ROW_CHUNKS), :])
                acc[cc] = acc[cc] + wk * lo
                acc[ROW_CHUNKS + cc] = acc[ROW_CHUNKS + cc] + wk * hi
        routed_ref[pl.ds(t0, SUBLANES), :] = jnp.concatenate(acc, axis=1)

    @pl.when(has_next)
    def _():
        def group(j, c):
            reduce_group(j)
            issue_group(dest_next_ref, 1 - slot, j)
            return c

        lax.fori_loop(0, tn // SUBLANES, group, 0)

    @pl.when(jnp.logical_not(has_next))
    def _():
        def group(j, c):
            reduce_group(j)
            return c

        lax.fori_loop(0, tn // SUBLANES, group, 0)

    h1 = h1_ref[...]
    hb = h1.astype(BF16)
    sg = jnp.dot(hb, wsg_ref[...], preferred_element_type=F32)
    su = jnp.dot(hb, wsu_ref[...], preferred_element_type=F32)
    ff = jnp.dot((sg * _sigmoid(sg) * su).astype(BF16), wsd_ref[...], preferred_element_type=F32)
    o_ref[...] = _layer_norm(ALPHA * h1 + ff + routed_ref[...], g2_ref[...], b2_ref[...])


def _combine_call(dest, wts, h1, ys, wsg, wsu, wsd, g2, b2):
    t = h1.shape[0]
    tn = TN_COMB
    nsteps = t // tn
    col = lambda i: (0, i)
    col_next = lambda i: (0, jnp.minimum(i + 1, nsteps - 1))
    row = lambda i: (i, 0)
    fix = lambda i: (0, 0)
    return pl.pallas_call(
        _combine_kernel,
        grid=(nsteps,),
        in_specs=[pl.BlockSpec((TOP_K, tn), col, memory_space=pltpu.SMEM),
                  pl.BlockSpec((TOP_K, tn), col_next, memory_space=pltpu.SMEM),
                  pl.BlockSpec((tn, TOP_K), row),
                  pl.BlockSpec((tn, D_MODEL), row),
                  pl.BlockSpec(memory_space=pl.ANY),
                  pl.BlockSpec((D_MODEL, SHARED_FF), fix), pl.BlockSpec((D_MODEL, SHARED_FF), fix),
                  pl.BlockSpec((SHARED_FF, D_MODEL), fix),
                  pl.BlockSpec((1, D_MODEL), fix), pl.BlockSpec((1, D_MODEL), fix)],
        out_specs=pl.BlockSpec((tn, D_MODEL), row),
        out_shape=SDS((t, D_MODEL), F32),
        scratch_shapes=[pltpu.VMEM((2 * TOP_K * tn * ROW_CHUNKS, LANES), U32), pltpu.VMEM((tn, D_MODEL), F32),
                        pltpu.SemaphoreType.DMA((2,))],
        compiler_params=_cparams("arbitrary"),
        name="combine_ln2",
    )(dest, dest, wts.T, h1, ys, wsg, wsu, wsd, g2, b2)


def kernel(x, meta_tokens, ln_in_g, ln_in_b, rel_bias, w_in, conv_w, conv_b, conv_ln_g, conv_ln_b, sinks,
           w_out, ln1_g, ln1_b, w_router, router_bias, w_gate, w_up, w_down, ws_gate, ws_up, ws_down,
           ln2_g, ln2_b):
    nbatch, seq, d = x.shape
    assert d == D_MODEL and seq % TQ_PROJ == 0 and w_in.shape[0] == DEPTH
    t = nbatch * seq
    x2d = x.reshape(t, D_MODEL)
    vec = lambda a: a.reshape(1, -1).astype(F32)
    gin, bin_ = vec(ln_in_g), vec(ln_in_b)
    w_in_b = w_in[0].astype(BF16)

    q, k, v, u = _proj_call(x2d, gin, bin_, w_in_b, TQ_PROJ)
    meta_blk = jnp.concatenate([jnp.zeros((PAD_FRONT, D_MODEL), F32), meta_tokens.astype(F32)], axis=0)
    _, k_meta, v_meta, u_meta = _proj_call(meta_blk, gin, bin_, w_in_b, BLOCK)

    attn = _attn_call(q, k, v, k_meta, v_meta, _rel_bias_table(rel_bias), sinks[0].astype(F32),
                      nbatch, seq // BLOCK)

    u_halo = jnp.concatenate([jnp.zeros((CONV_HALO - N_META, CONV_CH), F32), u_meta[PAD_FRONT:]], axis=0)
    conv = _conv_call(u, u_halo, conv_w[0].astype(F32), vec(conv_b[0]), vec(conv_ln_g[0]), vec(conv_ln_b[0]),
                      nbatch, seq)

    w_out_b = w_out[0].astype(BF16)
    wr_t = w_router[0].astype(F32).T
    wr_hi = wr_t.astype(BF16)
    wr_lo = (wr_t - wr_hi.astype(F32)).astype(BF16)
    h1, h1rows, logits = _mix_call(x2d, attn, conv, gin, bin_, w_out_b[:ATTN_W], w_out_b[ATTN_W:],
                                   vec(ln1_g[0]), vec(ln1_b[0]), wr_hi, wr_lo)

    idx, wts, rank, cnt = _route_call(logits, router_bias[0].astype(F32).reshape(N_EXPERTS, 1))

    tm = TM_EXP
    n_tiles = (t * TOP_K) // tm + N_EXPERTS
    counts = cnt[:, 0]
    tiles_e = (counts + tm - 1) // tm
    tile_end = jnp.cumsum(tiles_e).astype(I32)
    tile_start = (tile_end - tiles_e).astype(I32)
    offs = tile_start * tm
    tile_id = jnp.arange(n_tiles, dtype=I32)
    lo = jnp.maximum(tile_id[:, None] * tm, offs[None, :])
    hi = jnp.minimum((tile_id[:, None] + 1) * tm, (offs + counts)[None, :])
    tile_rows = jnp.sum(jnp.clip(hi - lo, 0, tm), axis=1).astype(I32)
    n_valid = tile_end[-1:]

    dest = _dest_call(idx, rank, offs.astype(F32).reshape(N_EXPERTS, 1))
    xs = _dispatch_call(dest, h1rows, n_tiles * tm)
    ys = _expert_call(tile_start, tile_end, tile_rows, n_valid, xs, w_gate[0], w_up[0], w_down[0], n_tiles * tm)
    out = _combine_call(dest, wts, h1, ys, ws_gate[0].astype(BF16), ws_up[0].astype(BF16),
                        ws_down[0].astype(BF16), vec(ln2_g[0]), vec(ln2_b[0]))
    return out.reshape(nbatch, seq, D_MODEL)
```

```python
import functools
import math

import numpy as np
import jax
import jax.numpy as jnp
from jax import lax
from jax.experimental import pallas as pl
from jax.experimental.pallas import tpu as pltpu

F32 = jnp.float32
BF16 = jnp.bfloat16
I32 = jnp.int32
U32 = jnp.uint32
SDS = jax.ShapeDtypeStruct

D_MODEL = 1024
HALF = D_MODEL // 2
LANES = 128
SUBLANES = 8
ROW_CHUNKS = HALF // LANES
N_META = 16
HEAD_DIM = 64
N_Q_HEADS = 8
N_KV_HEADS = 2
GQA_GROUP = N_Q_HEADS // N_KV_HEADS
ATTN_W = N_Q_HEADS * HEAD_DIM
KV_W = N_KV_HEADS * HEAD_DIM
WINDOW = 128
BLOCK = 128
CONV_CH = D_MODEL - ATTN_W
CONV_K = 31
IN_W = ATTN_W + 2 * KV_W + 2 * CONV_CH
NUM_BUCKETS = 32
MAX_EXACT = NUM_BUCKETS // 2
REL_MAX_DIST = 128
N_EXPERTS = 256
TOP_K = 8
N_GROUPS = 8
GROUP_SIZE = N_EXPERTS // N_GROUPS
TOPK_GROUPS = 4
EXPERT_FF = 256
SHARED_FF = 256
ROUTED_SCALE = 2.5
DEPTH = 1
ALPHA = (2.0 * DEPTH) ** 0.25
LN_EPS = 1e-5
NEG = -1e30
PAD_FRONT = (-N_META) % BLOCK

VMEM_LIMIT = 48 * 1024 * 1024

TQ_PROJ = 512
ATTN_QBLOCKS = 2
T_CONV = 256
CONV_HALO = 32
R_CONV = 64
TQ_MIX = 512
MIX_CHAINS = 2
TN_ROUTE = 256
TN_DISP = 256
TM_EXP = 256
X_SLOTS = 8
X_AHEAD = 4
Y_SLOTS = 4
TN_COMB = 256


def _cparams(*sem):
    return pltpu.CompilerParams(dimension_semantics=sem, vmem_limit_bytes=VMEM_LIMIT)


def _layer_norm(x, g, b):
    mu = jnp.mean(x, axis=-1, keepdims=True)
    xc = x - mu
    var = jnp.mean(xc * xc, axis=-1, keepdims=True)
    return xc * lax.rsqrt(var + LN_EPS) * g + b


def _sigmoid(x):
    return 1.0 / (1.0 + jnp.exp(-x))


def _pack_rows(lo_half, hi_half):
    lo = lax.bitcast_convert_type(lo_half.astype(BF16).astype(F32), U32)
    hi = lax.bitcast_convert_type(hi_half.astype(BF16).astype(F32), U32)
    return lax.shift_right_logical(lo, jnp.uint32(16)) | hi


def _unpack_rows(p):
    lo = lax.bitcast_convert_type(lax.shift_left(p, jnp.uint32(16)), F32)
    hi = lax.bitcast_convert_type(p & jnp.uint32(0xFFFF0000), F32)
    return lo, hi


def _chunk_index(start, j, n, lead):
    rows = pl.ds(start + j, n, stride=ROW_CHUNKS)
    return (rows, slice(None)) if lead is None else (lead, rows, slice(None))


def _store_packed(ref, start, n, packed, lead=None):
    for j in range(ROW_CHUNKS):
        ref[_chunk_index(start, j, n, lead)] = packed[:, j * LANES:(j + 1) * LANES]


def _load_packed_bf16(ref, start, n, lead=None):
    halves = [_unpack_rows(ref[_chunk_index(start, j, n, lead)]) for j in range(ROW_CHUNKS)]
    return jnp.concatenate([h[0] for h in halves] + [h[1] for h in halves], axis=1).astype(BF16)


def _packed_row(ref, token):
    return ref.at[pl.ds(pl.multiple_of(token * ROW_CHUNKS, ROW_CHUNKS), ROW_CHUNKS), :]


def _proj_kernel(x_ref, g_ref, b_ref, w_ref, q_ref, k_ref, v_ref, u_ref):
    h = _layer_norm(x_ref[...], g_ref[...], b_ref[...])
    p = jnp.dot(h.astype(BF16), w_ref[...], preferred_element_type=F32)
    q_ref[...] = (p[:, :ATTN_W] * (HEAD_DIM ** -0.5)).astype(BF16)
    k_ref[...] = p[:, ATTN_W:ATTN_W + KV_W].astype(BF16)
    v_ref[...] = p[:, ATTN_W + KV_W:ATTN_W + 2 * KV_W].astype(BF16)
    a = p[:, ATTN_W + 2 * KV_W:ATTN_W + 2 * KV_W + CONV_CH]
    gate = p[:, ATTN_W + 2 * KV_W + CONV_CH:]
    u_ref[...] = a * _sigmoid(gate)


def _proj_call(x2d, gin, bin_, w_in_b, tq):
    t = x2d.shape[0]
    row = lambda i: (i, 0)
    fix = lambda i: (0, 0)
    return pl.pallas_call(
        _proj_kernel,
        grid=(t // tq,),
        in_specs=[pl.BlockSpec((tq, D_MODEL), row), pl.BlockSpec((1, D_MODEL), fix),
                  pl.BlockSpec((1, D_MODEL), fix), pl.BlockSpec((D_MODEL, IN_W), fix)],
        out_specs=[pl.BlockSpec((tq, ATTN_W), row), pl.BlockSpec((tq, KV_W), row),
                   pl.BlockSpec((tq, KV_W), row), pl.BlockSpec((tq, CONV_CH), row)],
        out_shape=[SDS((t, ATTN_W), BF16), SDS((t, KV_W), BF16), SDS((t, KV_W), BF16), SDS((t, CONV_CH), F32)],
        compiler_params=_cparams("arbitrary"),
        name="ln_in_proj",
    )(x2d, gin, bin_, w_in_b)


def _attn_kernel(sinks_ref, q_ref, kc_ref, kp_ref, vc_ref, vp_ref, km_ref, vm_ref, bias_ref, o_ref):
    first = pl.program_id(1) == 0
    kp = jnp.where(first, km_ref[...], kp_ref[...])
    vp = jnp.where(first, vm_ref[...], vp_ref[...])
    k = jnp.concatenate([kp, kc_ref[...]], axis=0)
    v = jnp.concatenate([vp, vc_ref[...]], axis=0)
    col = lax.broadcasted_iota(I32, (BLOCK, 2 * BLOCK), 1)
    pad_bias = jnp.where(jnp.logical_and(first, col < PAD_FRONT), NEG, 0.0).astype(F32)
    for a in range(ATTN_QBLOCKS):
        q = q_ref[a * BLOCK:(a + 1) * BLOCK, :]
        kw = k[a * BLOCK:(a + 2) * BLOCK, :]
        vw = v[a * BLOCK:(a + 2) * BLOCK, :]
        outs = []
        for h in range(N_Q_HEADS):
            g = h // GQA_GROUP
            qh = q[:, h * HEAD_DIM:(h + 1) * HEAD_DIM]
            kg = kw[:, g * HEAD_DIM:(g + 1) * HEAD_DIM]
            vg = vw[:, g * HEAD_DIM:(g + 1) * HEAD_DIM]
            s = lax.dot_general(qh, kg, (((1,), (1,)), ((), ())), preferred_element_type=F32)
            s = s + bias_ref[h]
            if a == 0:
                s = s + pad_bias
            sink = sinks_ref[h]
            m = jnp.maximum(jnp.max(s, axis=-1, keepdims=True), sink)
            p = jnp.exp(s - m)
            den = jnp.sum(p, axis=-1, keepdims=True) + jnp.exp(sink - m)
            o = jnp.dot(p.astype(BF16), vg, preferred_element_type=F32)
            outs.append(o / den)
        o_ref[a * BLOCK:(a + 1) * BLOCK, :] = jnp.concatenate(outs, axis=1).astype(BF16)


def _attn_call(q, k, v, k_meta, v_meta, bias, sinks, nbatch, nblk):
    t = q.shape[0]
    nq = ATTN_QBLOCKS
    assert nblk % nq == 0
    nstep = nblk // nq
    cur = lambda b, j: (b * nstep + j, 0)
    prev = lambda b, j: (jnp.maximum((b * nstep + j) * nq - 1, 0), 0)
    fix2 = lambda b, j: (0, 0)
    return pl.pallas_call(
        _attn_kernel,
        grid=(nbatch, nstep),
        in_specs=[pl.BlockSpec(memory_space=pltpu.SMEM),
                  pl.BlockSpec((nq * BLOCK, ATTN_W), cur),
                  pl.BlockSpec((nq * BLOCK, KV_W), cur), pl.BlockSpec((BLOCK, KV_W), prev),
                  pl.BlockSpec((nq * BLOCK, KV_W), cur), pl.BlockSpec((BLOCK, KV_W), prev),
                  pl.BlockSpec((BLOCK, KV_W), fix2), pl.BlockSpec((BLOCK, KV_W), fix2),
                  pl.BlockSpec((N_Q_HEADS, BLOCK, 2 * BLOCK), lambda b, j: (0, 0, 0))],
        out_specs=pl.BlockSpec((nq * BLOCK, ATTN_W), cur),
        out_shape=SDS((t, ATTN_W), BF16),
        compiler_params=_cparams("arbitrary", "arbitrary"),
        name="swa_attn",
    )(sinks, q, k, k, v, v, k_meta, v_meta, bias)


def _rel_bias_table(rel_bias):
    qi = np.arange(BLOCK, dtype=np.int32)[:, None]
    kj = np.arange(2 * BLOCK, dtype=np.int32)[None, :]
    dist = BLOCK + qi - kj
    dc = np.clip(dist, 0, WINDOW - 1)
    nf = np.maximum(dc, 1).astype(np.float32)
    large = MAX_EXACT + (np.log(nf / np.float32(MAX_EXACT)) / np.float32(math.log(REL_MAX_DIST / MAX_EXACT))
                         * np.float32(NUM_BUCKETS - MAX_EXACT)).astype(np.int32)
    large = np.minimum(large, NUM_BUCKETS - 1)
    bucket = np.where(dc < MAX_EXACT, dc, large)
    in_window = (dist >= 0) & (dist < WINDOW)
    onehot = (bucket.reshape(-1, 1) == np.arange(NUM_BUCKETS)[None, :]).astype(np.float32)
    bias = jnp.dot(jnp.asarray(onehot), rel_bias.astype(F32), precision=lax.Precision.HIGHEST)
    bias = jnp.transpose(bias.reshape(BLOCK, 2 * BLOCK, N_Q_HEADS), (2, 0, 1))
    return jnp.where(in_window[None], bias, NEG)


def _conv_kernel(uc_ref, up_ref, um_ref, w_ref, cb_ref, g_ref, b_ref, o_ref, s_ref, sh_ref):
    first = pl.program_id(1) == 0
    s_ref[0:CONV_HALO, :] = jnp.where(first, um_ref[...], up_ref[...])
    s_ref[CONV_HALO:CONV_HALO + T_CONV, :] = uc_ref[...]
    off = CONV_HALO - (CONV_K - 1)
    span = sh_ref.shape[1]
    for c in range(0, T_CONV, R_CONV):
        for p in range(1, SUBLANES):
            sh_ref[p] = s_ref[c + p:c + p + span, :]
        acc = jnp.zeros((R_CONV, CONV_CH), F32) + cb_ref[...]
        for kk in range(CONV_K):
            p, a = (off + kk) % SUBLANES, (off + kk) // SUBLANES * SUBLANES
            if p == 0:
                win = s_ref[c + a:c + a + R_CONV, :]
            else:
                win = sh_ref[p, a:a + R_CONV, :]
            acc = acc + win * w_ref[kk:kk + 1, :]
        y = _layer_norm(acc, g_ref[...], b_ref[...])
        o_ref[c:c + R_CONV, :] = (y * _sigmoid(y)).astype(BF16)


def _conv_call(u, u_meta_halo, conv_w, conv_b, g, b, nbatch, seq):
    t = u.shape[0]
    nj = seq // T_CONV
    per = T_CONV // CONV_HALO
    cur = lambda bb, j: (bb * nj + j, 0)
    prev = lambda bb, j: (jnp.maximum((bb * nj + j) * per - 1, 0), 0)
    fix = lambda bb, j: (0, 0)
    return pl.pallas_call(
        _conv_kernel,
        grid=(nbatch, nj),
        in_specs=[pl.BlockSpec((T_CONV, CONV_CH), cur), pl.BlockSpec((CONV_HALO, CONV_CH), prev),
                  pl.BlockSpec((CONV_HALO, CONV_CH), fix), pl.BlockSpec((CONV_K, CONV_CH), fix),
                  pl.BlockSpec((1, CONV_CH), fix), pl.BlockSpec((1, CONV_CH), fix), pl.BlockSpec((1, CONV_CH), fix)],
        out_specs=pl.BlockSpec((T_CONV, CONV_CH), cur),
        out_shape=SDS((t, CONV_CH), BF16),
        scratch_shapes=[pltpu.VMEM((CONV_HALO + T_CONV, CONV_CH), F32),
                        pltpu.VMEM((SUBLANES, R_CONV + CONV_HALO - SUBLANES, CONV_CH), F32)],
        compiler_params=_cparams("arbitrary", "arbitrary"),
        name="conv_ln",
    )(u, u, u_meta_halo, conv_w, conv_b, g, b)


def _mix_kernel(x_ref, at_ref, cv_ref, gin_ref, bin_ref, woa_ref, woc_ref, g1_ref, b1_ref,
                wrh_ref, wrl_ref, h1_ref, h1r_ref, lg_ref):
    rows = x_ref.shape[0] // MIX_CHAINS
    nt = (((1,), (1,)), ((), ()))
    for c in range(MIX_CHAINS):
        r = slice(c * rows, (c + 1) * rows)
        h = _layer_norm(x_ref[r, :], gin_ref[...], bin_ref[...])
        mix = (jnp.dot(at_ref[r, :], woa_ref[...], preferred_element_type=F32)
               + jnp.dot(cv_ref[r, :], woc_ref[...], preferred_element_type=F32))
        h1 = _layer_norm(ALPHA * h + mix, g1_ref[...], b1_ref[...])
        h1_ref[r, :] = h1
        _store_packed(h1r_ref, c * rows * ROW_CHUNKS, rows, _pack_rows(h1[:, :HALF], h1[:, HALF:]))
        hh = h1.astype(BF16)
        hl = (h1 - hh.astype(F32)).astype(BF16)
        lg = lax.dot_general(wrh_ref[...], hh, nt, preferred_element_type=F32)
        lg = lg + lax.dot_general(wrh_ref[...], hl, nt, preferred_element_type=F32)
        lg = lg + lax.dot_general(wrl_ref[...], hh, nt, preferred_element_type=F32)
        lg_ref[:, r] = lg


def _mix_call(x2d, attn, conv, gin, bin_, woa, woc, g1, b1, wrh, wrl):
    t = x2d.shape[0]
    tq = TQ_MIX
    row = lambda i: (i, 0)
    fix = lambda i: (0, 0)
    return pl.pallas_call(
        _mix_kernel,
        grid=(t // tq,),
        in_specs=[pl.BlockSpec((tq, D_MODEL), row), pl.BlockSpec((tq, ATTN_W), row), pl.BlockSpec((tq, CONV_CH), row),
                  pl.BlockSpec((1, D_MODEL), fix), pl.BlockSpec((1, D_MODEL), fix),
                  pl.BlockSpec((ATTN_W, D_MODEL), fix), pl.BlockSpec((CONV_CH, D_MODEL), fix),
                  pl.BlockSpec((1, D_MODEL), fix), pl.BlockSpec((1, D_MODEL), fix),
                  pl.BlockSpec((N_EXPERTS, D_MODEL), fix), pl.BlockSpec((N_EXPERTS, D_MODEL), fix)],
        out_specs=[pl.BlockSpec((tq, D_MODEL), row), pl.BlockSpec((tq * ROW_CHUNKS, LANES), row),
                   pl.BlockSpec((N_EXPERTS, tq), lambda i: (0, i))],
        out_shape=[SDS((t, D_MODEL), F32), SDS((t * ROW_CHUNKS, LANES), U32), SDS((N_EXPERTS, t), F32)],
        compiler_params=_cparams("arbitrary"),
        name="mix_ln1",
    )(x2d, attn, conv, gin, bin_, woa, woc, g1, b1, wrh, wrl)


def _first_argmax(x, rows, nrows):
    m = jnp.max(x, axis=0, keepdims=True)
    idx = jnp.min(jnp.where(x == m, rows, nrows), axis=0, keepdims=True)
    return m, idx


def _route_kernel(lg_ref, rb_ref, idx_ref, wts_ref, rank_ref, cnt_ref, carry_ref):
    tn = lg_ref.shape[1]

    @pl.when(pl.program_id(0) == 0)
    def _():
        carry_ref[...] = jnp.zeros_like(carry_ref)

    scores = _sigmoid(lg_ref[...])
    choice = scores + rb_ref[...]
    rows = lax.broadcasted_iota(I32, (N_EXPERTS, tn), 0)
    rows_g = lax.broadcasted_iota(I32, (GROUP_SIZE, tn), 0)
    rows_8 = lax.broadcasted_iota(I32, (N_GROUPS, tn), 0)

    gs = []
    for g in range(N_GROUPS):
        xg = choice[g * GROUP_SIZE:(g + 1) * GROUP_SIZE, :]
        m1, i1 = _first_argmax(xg, rows_g, GROUP_SIZE)
        m2 = jnp.max(jnp.where(rows_g == i1, -jnp.inf, xg), axis=0, keepdims=True)
        gs.append(m1 + m2)
    gsc = jnp.concatenate(gs, axis=0)
    gsel = jnp.zeros((N_GROUPS, tn), F32)
    for _ in range(TOPK_GROUPS):
        _, gi = _first_argmax(gsc, rows_8, N_GROUPS)
        hit = rows_8 == gi
        gsel = jnp.where(hit, 1.0, gsel)
        gsc = jnp.where(hit, -jnp.inf, gsc)
    emask = jnp.concatenate(
        [jnp.broadcast_to(gsel[g:g + 1, :], (GROUP_SIZE, tn)) for g in range(N_GROUPS)], axis=0)
    masked = jnp.where(emask > 0.5, choice, NEG)

    sel_all = jnp.zeros((N_EXPERTS, tn), F32)
    hits, idxs, ws = [], [], []
    for _ in range(TOP_K):
        _, ii = _first_argmax(masked, rows, N_EXPERTS)
        hit = rows == ii
        hits.append(hit)
        idxs.append(ii)
        ws.append(jnp.sum(jnp.where(hit, scores, 0.0), axis=0, keepdims=True))
        sel_all = jnp.where(hit, 1.0, sel_all)
        masked = jnp.where(hit, -jnp.inf, masked)
    wsum = ws[0]
    for w in ws[1:]:
        wsum = wsum + w
    idx_ref[...] = jnp.concatenate(idxs, axis=0)
    wts_ref[...] = jnp.concatenate([w / wsum * ROUTED_SCALE for w in ws], axis=0)

    r_i = lax.broadcasted_iota(I32, (tn, tn), 0)
    c_i = lax.broadcasted_iota(I32, (tn, tn), 1)
    upper = jnp.where(r_i < c_i, 1.0, 0.0).astype(BF16)
    sel_b = sel_all.astype(BF16)
    carry = carry_ref[...]
    before = jnp.dot(sel_b, upper, preferred_element_type=F32)
    before = before + jnp.concatenate([carry] * (tn // 128), axis=1)
    rank_ref[...] = jnp.concatenate(
        [jnp.sum(jnp.where(h, before, 0.0), axis=0, keepdims=True) for h in hits], axis=0).astype(I32)
    carry = carry + jnp.dot(sel_b, jnp.ones((tn, 128), BF16), preferred_element_type=F32)
    carry_ref[...] = carry
    cnt_ref[...] = carry.astype(I32)


def _route_call(lg, rbias):
    t = lg.shape[1]
    tn = TN_ROUTE
    col = lambda i: (0, i)
    return pl.pallas_call(
        _route_kernel,
        grid=(t // tn,),
        in_specs=[pl.BlockSpec((N_EXPERTS, tn), col), pl.BlockSpec((N_EXPERTS, 1), lambda i: (0, 0))],
        out_specs=[pl.BlockSpec((TOP_K, tn), col), pl.BlockSpec((TOP_K, tn), col), pl.BlockSpec((TOP_K, tn), col),
                   pl.BlockSpec((N_EXPERTS, 128), lambda i: (0, 0))],
        out_shape=[SDS((TOP_K, t), I32), SDS((TOP_K, t), F32), SDS((TOP_K, t), I32), SDS((N_EXPERTS, 128), I32)],
        scratch_shapes=[pltpu.VMEM((N_EXPERTS, 128), F32)],
        compiler_params=_cparams("arbitrary"),
        name="route",
    )(lg, rbias)


def _dest_kernel(idx_ref, rank_ref, offs_ref, dest_ref):
    tn = idx_ref.shape[1]
    rows = lax.broadcasted_iota(I32, (N_EXPERTS, tn), 0)
    offs = offs_ref[...]
    out = []
    for kk in range(TOP_K):
        hit = rows == idx_ref[kk:kk + 1, :]
        out.append(jnp.sum(jnp.where(hit, offs, 0.0), axis=0, keepdims=True))
    dest_ref[...] = jnp.concatenate(out, axis=0).astype(I32) + rank_ref[...]


def _dest_call(idx, rank, offs_col):
    t = idx.shape[1]
    tn = TN_ROUTE
    col = lambda i: (0, i)
    return pl.pallas_call(
        _dest_kernel,
        grid=(t // tn,),
        in_specs=[pl.BlockSpec((TOP_K, tn), col), pl.BlockSpec((TOP_K, tn), col),
                  pl.BlockSpec((N_EXPERTS, 1), lambda i: (0, 0))],
        out_specs=pl.BlockSpec((TOP_K, tn), col),
        out_shape=SDS((TOP_K, t), I32),
        compiler_params=_cparams("arbitrary"),
        name="dest",
    )(idx, rank, offs_col)


DISP_UNROLL = 2


def _dispatch_kernel(dest_ref, h_ref, xs_ref, sem):
    tn = dest_ref.shape[1]

    def issue(j, c):
        slots = [[dest_ref[kk, j * DISP_UNROLL + r] for kk in range(TOP_K)] for r in range(DISP_UNROLL)]
        for r in range(DISP_UNROLL):
            src = _packed_row(h_ref, j * DISP_UNROLL + r)
            for kk in range(TOP_K):
                pltpu.make_async_copy(src, _packed_row(xs_ref, slots[r][kk]), sem).start(priority=kk % 2)
        return c

    lax.fori_loop(0, tn // DISP_UNROLL, issue, 0)
    n = TOP_K * tn * ROW_CHUNKS
    pltpu.make_async_copy(xs_ref.at[pl.ds(0, n), :], xs_ref.at[pl.ds(0, n), :], sem).wait()


def _dispatch_call(dest, h1rows, n_rows):
    t = dest.shape[1]
    tn = TN_DISP
    return pl.pallas_call(
        _dispatch_kernel,
        grid=(t // tn,),
        in_specs=[pl.BlockSpec((TOP_K, tn), lambda i: (0, i), memory_space=pltpu.SMEM),
                  pl.BlockSpec((tn * ROW_CHUNKS, LANES), lambda i: (i, 0))],
        out_specs=pl.BlockSpec(memory_space=pl.ANY),
        out_shape=SDS((n_rows * ROW_CHUNKS, LANES), U32),
        scratch_shapes=[pltpu.SemaphoreType.DMA(())],
        compiler_params=_cparams("arbitrary"),
        name="dispatch",
    )(dest, h1rows)


def _expert_kernel(ts_ref, te_ref, tr_ref, nv_ref, wg_ref, wu_ref, wd_ref, xs_hbm, ys_hbm,
                   xbuf, ybuf, wg_b, wu_b, wd_b, xsem, ysem):
    e = pl.program_id(0)
    rows = xbuf.shape[1]
    tm = rows // ROW_CHUNKS
    g0, g1, nv = ts_ref[e], te_ref[e], nv_ref[0]

    def x_copy(g):
        s = g % X_SLOTS
        return pltpu.make_async_copy(xs_hbm.at[pl.ds(pl.multiple_of(g * rows, rows), rows), :], xbuf.at[s], xsem.at[s])

    def y_copy(g):
        s = g % Y_SLOTS
        return pltpu.make_async_copy(ybuf.at[s], ys_hbm.at[pl.ds(pl.multiple_of(g * rows, rows), rows), :], ysem.at[s])

    def compute_tile(g):
        x = _load_packed_bf16(xbuf, 0, tm, lead=g % X_SLOTS)
        gate = jnp.dot(x, wg_b[...], preferred_element_type=F32)
        up = jnp.dot(x, wu_b[...], preferred_element_type=F32)
        live = lax.broadcasted_iota(I32, (tm, EXPERT_FF), 0) < tr_ref[g]
        hid = jnp.where(live, gate * _sigmoid(gate) * up, 0.0).astype(BF16)
        y = jnp.dot(hid, wd_b[...], preferred_element_type=F32)
        return _pack_rows(y[:, :HALF], y[:, HALF:])

    def run_tiles(g, n):
        for r in range(n):
            x_copy(g + r).wait()

            @pl.when(g + r + X_AHEAD < nv)
            def _():
                x_copy(g + r + X_AHEAD).start(priority=1)

            @pl.when(g + r >= Y_SLOTS)
            def _():
                y_copy(g + r - Y_SLOTS).wait()

        packed = [compute_tile(g + r) for r in range(n)]
        for r in range(n):
            _store_packed(ybuf, 0, tm, packed[r], lead=(g + r) % Y_SLOTS)
        for r in range(n):
            y_copy(g + r).start(priority=1)

    @pl.when(e == 0)
    def _():
        for g in range(X_AHEAD):
            @pl.when(g < nv)
            def _():
                x_copy(g).start(priority=1)

    @pl.when(g1 > g0)
    def _():
        wg_b[...] = wg_ref[0].astype(BF16)
        wu_b[...] = wu_ref[0].astype(BF16)
        wd_b[...] = wd_ref[0].astype(BF16)
        n_tiles = g1 - g0

        def pair(p, c):
            run_tiles(g0 + 2 * p, 2)
            return c

        lax.fori_loop(0, n_tiles // 2, pair, 0)

        @pl.when(n_tiles % 2 == 1)
        def _():
            run_tiles(g1 - 1, 1)

    @pl.when(e == pl.num_programs(0) - 1)
    def _():
        for back in range(1, Y_SLOTS + 1):
            @pl.when(nv >= back)
            def _():
                y_copy(nv - back).wait()


def _expert_call(tile_start, tile_end, tile_rows, n_valid, xs, w_gate, w_up, w_down, n_rows):
    tm = TM_EXP
    w_map = lambda e, *_: (e, 0, 0)
    hbm = pl.BlockSpec(memory_space=pl.ANY)
    return pl.pallas_call(
        _expert_kernel,
        grid_spec=pltpu.PrefetchScalarGridSpec(
            num_scalar_prefetch=4,
            grid=(N_EXPERTS,),
            in_specs=[pl.BlockSpec((1, D_MODEL, EXPERT_FF), w_map), pl.BlockSpec((1, D_MODEL, EXPERT_FF), w_map),
                      pl.BlockSpec((1, EXPERT_FF, D_MODEL), w_map), hbm],
            out_specs=hbm,
            scratch_shapes=[pltpu.VMEM((X_SLOTS, tm * ROW_CHUNKS, LANES), U32),
                            pltpu.VMEM((Y_SLOTS, tm * ROW_CHUNKS, LANES), U32),
                            pltpu.VMEM((D_MODEL, EXPERT_FF), BF16), pltpu.VMEM((D_MODEL, EXPERT_FF), BF16),
                            pltpu.VMEM((EXPERT_FF, D_MODEL), BF16),
                            pltpu.SemaphoreType.DMA((X_SLOTS,)), pltpu.SemaphoreType.DMA((Y_SLOTS,))],
        ),
        out_shape=SDS((n_rows * ROW_CHUNKS, LANES), U32),
        compiler_params=_cparams("arbitrary"),
        name="experts",
    )(tile_start, tile_end, tile_rows, n_valid, w_gate, w_up, w_down, xs)


def _combine_kernel(dest_ref, dest_next_ref, wts_ref, h1_ref, ys_ref, wsg_ref, wsu_ref, wsd_ref, g2_ref, b2_ref,
                    o_ref, ybuf_ref, routed_ref, sem):
    i = pl.program_id(0)
    tn = h1_ref.shape[0]
    rows = TOP_K * tn
    slot = i % 2
    has_next = i + 1 < pl.num_programs(0)

    def issue_group(d_ref, s, j):
        srcs = [[d_ref[kk, j * SUBLANES + r] for kk in range(TOP_K)] for r in range(SUBLANES)]
        for r in range(SUBLANES):
            for kk in range(TOP_K):
                pltpu.make_async_copy(_packed_row(ys_ref, srcs[r][kk]),
                                      _packed_row(ybuf_ref, s * rows + kk * tn + j * SUBLANES + r),
                                      sem.at[s]).start(priority=kk % 2)

    @pl.when(i == 0)
    def _():
        def first(j, c):
            issue_group(dest_ref, 0, j)
            return c

        lax.fori_loop(0, tn // SUBLANES, first, 0)

    base = slot * rows
    pltpu.make_async_copy(
        ys_ref.at[pl.ds(0, rows * ROW_CHUNKS), :],
        ybuf_ref.at[pl.ds(pl.multiple_of(base * ROW_CHUNKS, rows * ROW_CHUNKS), rows * ROW_CHUNKS), :],
        sem.at[slot]).wait()

    def reduce_group(j):
        t0 = pl.multiple_of(j * SUBLANES, SUBLANES)
        acc = [jnp.zeros((SUBLANES, LANES), F32) for _ in range(2 * ROW_CHUNKS)]
        for kk in range(TOP_K):
            wk = wts_ref[pl.ds(t0, SUBLANES), kk:kk + 1]
            for cc in range(ROW_CHUNKS):
                start = (base + kk * tn + t0) * ROW_CHUNKS + cc
                lo, hi = _unpack_rows(ybuf_ref[pl.ds(start, SUBLANES, stride=ROW_CHUNKS), :])
                acc[cc] = acc[cc] + wk * lo
                acc[ROW_CHUNKS + cc] = acc[ROW_CHUNKS + cc] + wk * hi
        routed_ref[pl.ds(t0, SUBLANES), :] = jnp.concatenate(acc, axis=1)

    @pl.when(has_next)
    def _():
        def group(j, c):
            reduce_group(j)
            issue_group(dest_next_ref, 1 - slot, j)
            return c

        lax.fori_loop(0, tn // SUBLANES, group, 0)

    @pl.when(jnp.logical_not(has_next))
    def _():
        def group(j, c):
            reduce_group(j)
            return c

        lax.fori_loop(0, tn // SUBLANES, group, 0)

    h1 = h1_ref[...]
    hb = h1.astype(BF16)
    sg = jnp.dot(hb, wsg_ref[...], preferred_element_type=F32)
    su = jnp.dot(hb, wsu_ref[...], preferred_element_type=F32)
    ff = jnp.dot((sg * _sigmoid(sg) * su).astype(BF16), wsd_ref[...], preferred_element_type=F32)
    o_ref[...] = _layer_norm(ALPHA * h1 + ff + routed_ref[...], g2_ref[...], b2_ref[...])


def _combine_call(dest, wts, h1, ys, wsg, wsu, wsd, g2, b2):
    t = h1.shape[0]
    tn = TN_COMB
    nsteps = t // tn
    col = lambda i: (0, i)
    col_next = lambda i: (0, jnp.minimum(i + 1, nsteps - 1))
    row = lambda i: (i, 0)
    fix = lambda i: (0, 0)
    return pl.pallas_call(
        _combine_kernel,
        grid=(nsteps,),
        in_specs=[pl.BlockSpec((TOP_K, tn), col, memory_space=pltpu.SMEM),
                  pl.BlockSpec((TOP_K, tn), col_next, memory_space=pltpu.SMEM),
                  pl.BlockSpec((tn, TOP_K), row),
                  pl.BlockSpec((tn, D_MODEL), row),
                  pl.BlockSpec(memory_space=pl.ANY),
                  pl.BlockSpec((D_MODEL, SHARED_FF), fix), pl.BlockSpec((D_MODEL, SHARED_FF), fix),
                  pl.BlockSpec((SHARED_FF, D_MODEL), fix),
                  pl.BlockSpec((1, D_MODEL), fix), pl.BlockSpec((1, D_MODEL), fix)],
        out_specs=pl.BlockSpec((tn, D_MODEL), row),
        out_shape=SDS((t, D_MODEL), F32),
        scratch_shapes=[pltpu.VMEM((2 * TOP_K * tn * ROW_CHUNKS, LANES), U32), pltpu.VMEM((tn, D_MODEL), F32),
                        pltpu.SemaphoreType.DMA((2,))],
        compiler_params=_cparams("arbitrary"),
        name="combine_ln2",
    )(dest, dest, wts.T, h1, ys, wsg, wsu, wsd, g2, b2)


def kernel(x, meta_tokens, ln_in_g, ln_in_b, rel_bias, w_in, conv_w, conv_b, conv_ln_g, conv_ln_b, sinks,
           w_out, ln1_g, ln1_b, w_router, router_bias, w_gate, w_up, w_down, ws_gate, ws_up, ws_down,
           ln2_g, ln2_b):
    nbatch, seq, d = x.shape
    assert d == D_MODEL and seq % TQ_PROJ == 0 and w_in.shape[0] == DEPTH
    t = nbatch * seq
    x2d = x.reshape(t, D_MODEL)
    vec = lambda a: a.reshape(1, -1).astype(F32)
    gin, bin_ = vec(ln_in_g), vec(ln_in_b)
    w_in_b = w_in[0].astype(BF16)

    q, k, v, u = _proj_call(x2d, gin, bin_, w_in_b, TQ_PROJ)
    meta_blk = jnp.concatenate([jnp.zeros((PAD_FRONT, D_MODEL), F32), meta_tokens.astype(F32)], axis=0)
    _, k_meta, v_meta, u_meta = _proj_call(meta_blk, gin, bin_, w_in_b, BLOCK)

    attn = _attn_call(q, k, v, k_meta, v_meta, _rel_bias_table(rel_bias), sinks[0].astype(F32),
                      nbatch, seq // BLOCK)

    u_halo = jnp.concatenate([jnp.zeros((CONV_HALO - N_META, CONV_CH), F32), u_meta[PAD_FRONT:]], axis=0)
    conv = _conv_call(u, u_halo, conv_w[0].astype(F32), vec(conv_b[0]), vec(conv_ln_g[0]), vec(conv_ln_b[0]),
                      nbatch, seq)

    w_out_b = w_out[0].astype(BF16)
    wr_t = w_router[0].astype(F32).T
    wr_hi = wr_t.astype(BF16)
    wr_lo = (wr_t - wr_hi.astype(F32)).astype(BF16)
    h1, h1rows, logits = _mix_call(x2d, attn, conv, gin, bin_, w_out_b[:ATTN_W], w_out_b[ATTN_W:],
                                   vec(ln1_g[0]), vec(ln1_b[0]), wr_hi, wr_lo)

    idx, wts, rank, cnt = _route_call(logits, router_bias[0].astype(F32).reshape(N_EXPERTS, 1))

    tm = TM_EXP
    n_tiles = (t * TOP_K) // tm + N_EXPERTS
    counts = cnt[:, 0]
    tiles_e = (counts + tm - 1) // tm
    tile_end = jnp.cumsum(tiles_e).astype(I32)
    tile_start = (tile_end - tiles_e).astype(I32)
    offs = tile_start * tm
    tile_id = jnp.arange(n_tiles, dtype=I32)
    lo = jnp.maximum(tile_id[:, None] * tm, offs[None, :])
    hi = jnp.minimum((tile_id[:, None] + 1) * tm, (offs + counts)[None, :])
    tile_rows = jnp.sum(jnp.clip(hi - lo, 0, tm), axis=1).astype(I32)
    n_valid = tile_end[-1:]

    dest = _dest_call(idx, rank, offs.astype(F32).reshape(N_EXPERTS, 1))
    xs = _dispatch_call(dest, h1rows, n_tiles * tm)
    ys = _expert_call(tile_start, tile_end, tile_rows, n_valid, xs, w_gate[0], w_up[0], w_down[0], n_tiles * tm)
    out = _combine_call(dest, wts, h1, ys, ws_gate[0].astype(BF16), ws_up[0].astype(BF16),
                        ws_down[0].astype(BF16), vec(ln2_g[0]), vec(ln2_b[0]))
    return out.reshape(nbatch, seq, D_MODEL)
```

```python
import functools
import math

import numpy as np
import jax
import jax.numpy as jnp
from jax import lax
from jax.experimental import pallas as pl
from jax.experimental.pallas import tpu as pltpu
from jax.experimental.pallas import tpu_sc as plsc

F32 = jnp.float32
BF16 = jnp.bfloat16
I32 = jnp.int32
U32 = jnp.uint32
SDS = jax.ShapeDtypeStruct

D_MODEL = 1024
HALF = D_MODEL // 2
LANES = 128
SUBLANES = 8
ROW_CHUNKS = HALF // LANES
N_META = 16
HEAD_DIM = 64
N_Q_HEADS = 8
N_KV_HEADS = 2
GQA_GROUP = N_Q_HEADS // N_KV_HEADS
ATTN_W = N_Q_HEADS * HEAD_DIM
KV_W = N_KV_HEADS * HEAD_DIM
WINDOW = 128
BLOCK = 128
CONV_CH = D_MODEL - ATTN_W
CONV_K = 31
IN_W = ATTN_W + 2 * KV_W + 2 * CONV_CH
NUM_BUCKETS = 32
MAX_EXACT = NUM_BUCKETS // 2
REL_MAX_DIST = 128
N_EXPERTS = 256
TOP_K = 8
N_GROUPS = 8
GROUP_SIZE = N_EXPERTS // N_GROUPS
TOPK_GROUPS = 4
EXPERT_FF = 256
SHARED_FF = 256
ROUTED_SCALE = 2.5
DEPTH = 1
ALPHA = (2.0 * DEPTH) ** 0.25
LN_EPS = 1e-5
NEG = -1e30
PAD_FRONT = (-N_META) % BLOCK

VMEM_LIMIT = 48 * 1024 * 1024

TQ_PROJ = 512
ATTN_QBLOCKS = 2
T_CONV = 256
CONV_HALO = 32
R_CONV = 64
TQ_MIX = 512
MIX_CHAINS = 2
TN_ROUTE = 256
TN_DISP = 256
TM_EXP = 256
X_SLOTS = 8
X_AHEAD = 4
Y_SLOTS = 4
TN_COMB = 256


def _cparams(*sem):
    return pltpu.CompilerParams(dimension_semantics=sem, vmem_limit_bytes=VMEM_LIMIT)


def _layer_norm(x, g, b):
    mu = jnp.mean(x, axis=-1, keepdims=True)
    xc = x - mu
    var = jnp.mean(xc * xc, axis=-1, keepdims=True)
    return xc * lax.rsqrt(var + LN_EPS) * g + b


def _sigmoid(x):
    return 1.0 / (1.0 + jnp.exp(-x))


def _pack_rows(lo_half, hi_half):
    lo = lax.bitcast_convert_type(lo_half.astype(BF16).astype(F32), U32)
    hi = lax.bitcast_convert_type(hi_half.astype(BF16).astype(F32), U32)
    return lax.shift_right_logical(lo, jnp.uint32(16)) | hi


def _unpack_rows(p):
    lo = lax.bitcast_convert_type(lax.shift_left(p, jnp.uint32(16)), F32)
    hi = lax.bitcast_convert_type(p & jnp.uint32(0xFFFF0000), F32)
    return lo, hi


def _chunk_index(start, j, n, lead):
    rows = pl.ds(start + j, n, stride=ROW_CHUNKS)
    return (rows, slice(None)) if lead is None else (lead, rows, slice(None))


def _store_packed(ref, start, n, packed, lead=None):
    for j in range(ROW_CHUNKS):
        ref[_chunk_index(start, j, n, lead)] = packed[:, j * LANES:(j + 1) * LANES]


def _load_packed_bf16(ref, start, n, lead=None):
    halves = [_unpack_rows(ref[_chunk_index(start, j, n, lead)]) for j in range(ROW_CHUNKS)]
    return jnp.concatenate([h[0] for h in halves] + [h[1] for h in halves], axis=1).astype(BF16)


def _packed_row(ref, token):
    return ref.at[pl.ds(pl.multiple_of(token * ROW_CHUNKS, ROW_CHUNKS), ROW_CHUNKS), :]


def _proj_kernel(x_ref, g_ref, b_ref, w_ref, q_ref, k_ref, v_ref, u_ref):
    h = _layer_norm(x_ref[...], g_ref[...], b_ref[...])
    p = jnp.dot(h.astype(BF16), w_ref[...], preferred_element_type=F32)
    q_ref[...] = (p[:, :ATTN_W] * (HEAD_DIM ** -0.5)).astype(BF16)
    k_ref[...] = p[:, ATTN_W:ATTN_W + KV_W].astype(BF16)
    v_ref[...] = p[:, ATTN_W + KV_W:ATTN_W + 2 * KV_W].astype(BF16)
    a = p[:, ATTN_W + 2 * KV_W:ATTN_W + 2 * KV_W + CONV_CH]
    gate = p[:, ATTN_W + 2 * KV_W + CONV_CH:]
    u_ref[...] = a * _sigmoid(gate)


def _proj_call(x2d, gin, bin_, w_in_b, tq):
    t = x2d.shape[0]
    row = lambda i: (i, 0)
    fix = lambda i: (0, 0)
    return pl.pallas_call(
        _proj_kernel,
        grid=(t // tq,),
        in_specs=[pl.BlockSpec((tq, D_MODEL), row), pl.BlockSpec((1, D_MODEL), fix),
                  pl.BlockSpec((1, D_MODEL), fix), pl.BlockSpec((D_MODEL, IN_W), fix)],
        out_specs=[pl.BlockSpec((tq, ATTN_W), row), pl.BlockSpec((tq, KV_W), row),
                   pl.BlockSpec((tq, KV_W), row), pl.BlockSpec((tq, CONV_CH), row)],
        out_shape=[SDS((t, ATTN_W), BF16), SDS((t, KV_W), BF16), SDS((t, KV_W), BF16), SDS((t, CONV_CH), F32)],
        compiler_params=_cparams("arbitrary"),
        name="ln_in_proj",
    )(x2d, gin, bin_, w_in_b)


def _attn_kernel(sinks_ref, q_ref, kc_ref, kp_ref, vc_ref, vp_ref, km_ref, vm_ref, bias_ref, o_ref):
    first = pl.program_id(1) == 0
    kp = jnp.where(first, km_ref[...], kp_ref[...])
    vp = jnp.where(first, vm_ref[...], vp_ref[...])
    k = jnp.concatenate([kp, kc_ref[...]], axis=0)
    v = jnp.concatenate([vp, vc_ref[...]], axis=0)
    col = lax.broadcasted_iota(I32, (BLOCK, 2 * BLOCK), 1)
    pad_bias = jnp.where(jnp.logical_and(first, col < PAD_FRONT), NEG, 0.0).astype(F32)
    for a in range(ATTN_QBLOCKS):
        q = q_ref[a * BLOCK:(a + 1) * BLOCK, :]
        kw = k[a * BLOCK:(a + 2) * BLOCK, :]
        vw = v[a * BLOCK:(a + 2) * BLOCK, :]
        outs = []
        for h in range(N_Q_HEADS):
            g = h // GQA_GROUP
            qh = q[:, h * HEAD_DIM:(h + 1) * HEAD_DIM]
            kg = kw[:, g * HEAD_DIM:(g + 1) * HEAD_DIM]
            vg = vw[:, g * HEAD_DIM:(g + 1) * HEAD_DIM]
            s = lax.dot_general(qh, kg, (((1,), (1,)), ((), ())), preferred_element_type=F32)
            s = s + bias_ref[h]
            if a == 0:
                s = s + pad_bias
            sink = sinks_ref[h]
            m = jnp.maximum(jnp.max(s, axis=-1, keepdims=True), sink)
            p = jnp.exp(s - m)
            den = jnp.sum(p, axis=-1, keepdims=True) + jnp.exp(sink - m)
            o = jnp.dot(p.astype(BF16), vg, preferred_element_type=F32)
            outs.append(o / den)
        o_ref[a * BLOCK:(a + 1) * BLOCK, :] = jnp.concatenate(outs, axis=1).astype(BF16)


def _attn_call(q, k, v, k_meta, v_meta, bias, sinks, nbatch, nblk):
    t = q.shape[0]
    nq = ATTN_QBLOCKS
    assert nblk % nq == 0
    nstep = nblk // nq
    cur = lambda b, j: (b * nstep + j, 0)
    prev = lambda b, j: (jnp.maximum((b * nstep + j) * nq - 1, 0), 0)
    fix2 = lambda b, j: (0, 0)
    return pl.pallas_call(
        _attn_kernel,
        grid=(nbatch, nstep),
        in_specs=[pl.BlockSpec(memory_space=pltpu.SMEM),
                  pl.BlockSpec((nq * BLOCK, ATTN_W), cur),
                  pl.BlockSpec((nq * BLOCK, KV_W), cur), pl.BlockSpec((BLOCK, KV_W), prev),
                  pl.BlockSpec((nq * BLOCK, KV_W), cur), pl.BlockSpec((BLOCK, KV_W), prev),
                  pl.BlockSpec((BLOCK, KV_W), fix2), pl.BlockSpec((BLOCK, KV_W), fix2),
                  pl.BlockSpec((N_Q_HEADS, BLOCK, 2 * BLOCK), lambda b, j: (0, 0, 0))],
        out_specs=pl.BlockSpec((nq * BLOCK, ATTN_W), cur),
        out_shape=SDS((t, ATTN_W), BF16),
        compiler_params=_cparams("arbitrary", "arbitrary"),
        name="swa_attn",
    )(sinks, q, k, k, v, v, k_meta, v_meta, bias)


def _rel_bias_table(rel_bias):
    qi = np.arange(BLOCK, dtype=np.int32)[:, None]
    kj = np.arange(2 * BLOCK, dtype=np.int32)[None, :]
    dist = BLOCK + qi - kj
    dc = np.clip(dist, 0, WINDOW - 1)
    nf = np.maximum(dc, 1).astype(np.float32)
    large = MAX_EXACT + (np.log(nf / np.float32(MAX_EXACT)) / np.float32(math.log(REL_MAX_DIST / MAX_EXACT))
                         * np.float32(NUM_BUCKETS - MAX_EXACT)).astype(np.int32)
    large = np.minimum(large, NUM_BUCKETS - 1)
    bucket = np.where(dc < MAX_EXACT, dc, large)
    in_window = (dist >= 0) & (dist < WINDOW)
    onehot = (bucket.reshape(-1, 1) == np.arange(NUM_BUCKETS)[None, :]).astype(np.float32)
    bias = jnp.dot(jnp.asarray(onehot), rel_bias.astype(F32), precision=lax.Precision.HIGHEST)
    bias = jnp.transpose(bias.reshape(BLOCK, 2 * BLOCK, N_Q_HEADS), (2, 0, 1))
    return jnp.where(in_window[None], bias, NEG)


def _conv_kernel(uc_ref, up_ref, um_ref, w_ref, cb_ref, g_ref, b_ref, o_ref, s_ref, sh_ref):
    first = pl.program_id(1) == 0
    s_ref[0:CONV_HALO, :] = jnp.where(first, um_ref[...], up_ref[...])
    s_ref[CONV_HALO:CONV_HALO + T_CONV, :] = uc_ref[...]
    off = CONV_HALO - (CONV_K - 1)
    span = sh_ref.shape[1]
    for c in range(0, T_CONV, R_CONV):
        for p in range(1, SUBLANES):
            sh_ref[p] = s_ref[c + p:c + p + span, :]
        acc = jnp.zeros((R_CONV, CONV_CH), F32) + cb_ref[...]
        for kk in range(CONV_K):
            p, a = (off + kk) % SUBLANES, (off + kk) // SUBLANES * SUBLANES
            if p == 0:
                win = s_ref[c + a:c + a + R_CONV, :]
            else:
                win = sh_ref[p, a:a + R_CONV, :]
            acc = acc + win * w_ref[kk:kk + 1, :]
        y = _layer_norm(acc, g_ref[...], b_ref[...])
        o_ref[c:c + R_CONV, :] = (y * _sigmoid(y)).astype(BF16)


def _conv_call(u, u_meta_halo, conv_w, conv_b, g, b, nbatch, seq):
    t = u.shape[0]
    nj = seq // T_CONV
    per = T_CONV // CONV_HALO
    cur = lambda bb, j: (bb * nj + j, 0)
    prev = lambda bb, j: (jnp.maximum((bb * nj + j) * per - 1, 0), 0)
    fix = lambda bb, j: (0, 0)
    return pl.pallas_call(
        _conv_kernel,
        grid=(nbatch, nj),
        in_specs=[pl.BlockSpec((T_CONV, CONV_CH), cur), pl.BlockSpec((CONV_HALO, CONV_CH), prev),
                  pl.BlockSpec((CONV_HALO, CONV_CH), fix), pl.BlockSpec((CONV_K, CONV_CH), fix),
                  pl.BlockSpec((1, CONV_CH), fix), pl.BlockSpec((1, CONV_CH), fix), pl.BlockSpec((1, CONV_CH), fix)],
        out_specs=pl.BlockSpec((T_CONV, CONV_CH), cur),
        out_shape=SDS((t, CONV_CH), BF16),
        scratch_shapes=[pltpu.VMEM((CONV_HALO + T_CONV, CONV_CH), F32),
                        pltpu.VMEM((SUBLANES, R_CONV + CONV_HALO - SUBLANES, CONV_CH), F32)],
        compiler_params=_cparams("arbitrary", "arbitrary"),
        name="conv_ln",
    )(u, u, u_meta_halo, conv_w, conv_b, g, b)


def _mix_kernel(x_ref, at_ref, cv_ref, gin_ref, bin_ref, woa_ref, woc_ref, g1_ref, b1_ref,
                wrh_ref, wrl_ref, h1_ref, h1r_ref, lg_ref):
    rows = x_ref.shape[0] // MIX_CHAINS
    nt = (((1,), (1,)), ((), ()))
    for c in range(MIX_CHAINS):
        r = slice(c * rows, (c + 1) * rows)
        h = _layer_norm(x_ref[r, :], gin_ref[...], bin_ref[...])
        mix = (jnp.dot(at_ref[r, :], woa_ref[...], preferred_element_type=F32)
               + jnp.dot(cv_ref[r, :], woc_ref[...], preferred_element_type=F32))
        h1 = _layer_norm(ALPHA * h + mix, g1_ref[...], b1_ref[...])
        h1_ref[r, :] = h1
        _store_packed(h1r_ref, c * rows * ROW_CHUNKS, rows, _pack_rows(h1[:, :HALF], h1[:, HALF:]))
        hh = h1.astype(BF16)
        hl = (h1 - hh.astype(F32)).astype(BF16)
        lg = lax.dot_general(wrh_ref[...], hh, nt, preferred_element_type=F32)
        lg = lg + lax.dot_general(wrh_ref[...], hl, nt, preferred_element_type=F32)
        lg = lg + lax.dot_general(wrl_ref[...], hh, nt, preferred_element_type=F32)
        lg_ref[:, r] = lg


def _mix_call(x2d, attn, conv, gin, bin_, woa, woc, g1, b1, wrh, wrl):
    t = x2d.shape[0]
    tq = TQ_MIX
    row = lambda i: (i, 0)
    fix = lambda i: (0, 0)
    return pl.pallas_call(
        _mix_kernel,
        grid=(t // tq,),
        in_specs=[pl.BlockSpec((tq, D_MODEL), row), pl.BlockSpec((tq, ATTN_W), row), pl.BlockSpec((tq, CONV_CH), row),
                  pl.BlockSpec((1, D_MODEL), fix), pl.BlockSpec((1, D_MODEL), fix),
                  pl.BlockSpec((ATTN_W, D_MODEL), fix), pl.BlockSpec((CONV_CH, D_MODEL), fix),
                  pl.BlockSpec((1, D_MODEL), fix), pl.BlockSpec((1, D_MODEL), fix),
                  pl.BlockSpec((N_EXPERTS, D_MODEL), fix), pl.BlockSpec((N_EXPERTS, D_MODEL), fix)],
        out_specs=[pl.BlockSpec((tq, D_MODEL), row), pl.BlockSpec((tq * ROW_CHUNKS, LANES), row),
                   pl.BlockSpec((N_EXPERTS, tq), lambda i: (0, i))],
        out_shape=[SDS((t, D_MODEL), F32), SDS((t * ROW_CHUNKS, LANES), U32), SDS((N_EXPERTS, t), F32)],
        compiler_params=_cparams("arbitrary"),
        name="mix_ln1",
    )(x2d, attn, conv, gin, bin_, woa, woc, g1, b1, wrh, wrl)


def _first_argmax(x, rows, nrows):
    m = jnp.max(x, axis=0, keepdims=True)
    idx = jnp.min(jnp.where(x == m, rows, nrows), axis=0, keepdims=True)
    return m, idx


def _route_kernel(lg_ref, rb_ref, idx_ref, wts_ref, rank_ref, cnt_ref, carry_ref):
    tn = lg_ref.shape[1]

    @pl.when(pl.program_id(0) == 0)
    def _():
        carry_ref[...] = jnp.zeros_like(carry_ref)

    scores = _sigmoid(lg_ref[...])
    choice = scores + rb_ref[...]
    rows = lax.broadcasted_iota(I32, (N_EXPERTS, tn), 0)
    rows_g = lax.broadcasted_iota(I32, (GROUP_SIZE, tn), 0)
    rows_8 = lax.broadcasted_iota(I32, (N_GROUPS, tn), 0)

    gs = []
    for g in range(N_GROUPS):
        xg = choice[g * GROUP_SIZE:(g + 1) * GROUP_SIZE, :]
        m1, i1 = _first_argmax(xg, rows_g, GROUP_SIZE)
        m2 = jnp.max(jnp.where(rows_g == i1, -jnp.inf, xg), axis=0, keepdims=True)
        gs.append(m1 + m2)
    gsc = jnp.concatenate(gs, axis=0)
    gsel = jnp.zeros((N_GROUPS, tn), F32)
    for _ in range(TOPK_GROUPS):
        _, gi = _first_argmax(gsc, rows_8, N_GROUPS)
        hit = rows_8 == gi
        gsel = jnp.where(hit, 1.0, gsel)
        gsc = jnp.where(hit, -jnp.inf, gsc)
    emask = jnp.concatenate(
        [jnp.broadcast_to(gsel[g:g + 1, :], (GROUP_SIZE, tn)) for g in range(N_GROUPS)], axis=0)
    masked = jnp.where(emask > 0.5, choice, NEG)

    sel_all = jnp.zeros((N_EXPERTS, tn), F32)
    hits, idxs, ws = [], [], []
    for _ in range(TOP_K):
        _, ii = _first_argmax(masked, rows, N_EXPERTS)
        hit = rows == ii
        hits.append(hit)
        idxs.append(ii)
        ws.append(jnp.sum(jnp.where(hit, scores, 0.0), axis=0, keepdims=True))
        sel_all = jnp.where(hit, 1.0, sel_all)
        masked = jnp.where(hit, -jnp.inf, masked)
    wsum = ws[0]
    for w in ws[1:]:
        wsum = wsum + w
    idx_ref[...] = jnp.concatenate(idxs, axis=0)
    wts_ref[...] = jnp.concatenate([w / wsum * ROUTED_SCALE for w in ws], axis=0)

    r_i = lax.broadcasted_iota(I32, (tn, tn), 0)
    c_i = lax.broadcasted_iota(I32, (tn, tn), 1)
    upper = jnp.where(r_i < c_i, 1.0, 0.0).astype(BF16)
    sel_b = sel_all.astype(BF16)
    carry = carry_ref[...]
    before = jnp.dot(sel_b, upper, preferred_element_type=F32)
    before = before + jnp.concatenate([carry] * (tn // 128), axis=1)
    rank_ref[...] = jnp.concatenate(
        [jnp.sum(jnp.where(h, before, 0.0), axis=0, keepdims=True) for h in hits], axis=0).astype(I32)
    carry = carry + jnp.dot(sel_b, jnp.ones((tn, 128), BF16), preferred_element_type=F32)
    carry_ref[...] = carry
    cnt_ref[...] = carry.astype(I32)


def _route_call(lg, rbias):
    t = lg.shape[1]
    tn = TN_ROUTE
    col = lambda i: (0, i)
    return pl.pallas_call(
        _route_kernel,
        grid=(t // tn,),
        in_specs=[pl.BlockSpec((N_EXPERTS, tn), col), pl.BlockSpec((N_EXPERTS, 1), lambda i: (0, 0))],
        out_specs=[pl.BlockSpec((TOP_K, tn), col), pl.BlockSpec((TOP_K, tn), col), pl.BlockSpec((TOP_K, tn), col),
                   pl.BlockSpec((N_EXPERTS, 128), lambda i: (0, 0))],
        out_shape=[SDS((TOP_K, t), I32), SDS((TOP_K, t), F32), SDS((TOP_K, t), I32), SDS((N_EXPERTS, 128), I32)],
        scratch_shapes=[pltpu.VMEM((N_EXPERTS, 128), F32)],
        compiler_params=_cparams("arbitrary"),
        name="route",
    )(lg, rbias)


def _dest_kernel(idx_ref, rank_ref, offs_ref, dest_ref):
    tn = idx_ref.shape[1]
    rows = lax.broadcasted_iota(I32, (N_EXPERTS, tn), 0)
    offs = offs_ref[...]
    out = []
    for kk in range(TOP_K):
        hit = rows == idx_ref[kk:kk + 1, :]
        out.append(jnp.sum(jnp.where(hit, offs, 0.0), axis=0, keepdims=True))
    dest_ref[...] = jnp.concatenate(out, axis=0).astype(I32) + rank_ref[...]


def _dest_call(idx, rank, offs_col):
    t = idx.shape[1]
    tn = TN_ROUTE
    col = lambda i: (0, i)
    return pl.pallas_call(
        _dest_kernel,
        grid=(t // tn,),
        in_specs=[pl.BlockSpec((TOP_K, tn), col), pl.BlockSpec((TOP_K, tn), col),
                  pl.BlockSpec((N_EXPERTS, 1), lambda i: (0, 0))],
        out_specs=pl.BlockSpec((TOP_K, tn), col),
        out_shape=SDS((TOP_K, t), I32),
        compiler_params=_cparams("arbitrary"),
        name="dest",
    )(idx, rank, offs_col)


DISP_UNROLL = 2


def _dispatch_kernel(dest_ref, h_ref, xs_ref, sem):
    tn = dest_ref.shape[1]

    def issue(j, c):
        slots = [[dest_ref[kk, j * DISP_UNROLL + r] for kk in range(TOP_K)] for r in range(DISP_UNROLL)]
        for r in range(DISP_UNROLL):
            src = _packed_row(h_ref, j * DISP_UNROLL + r)
            for kk in range(TOP_K):
                pltpu.make_async_copy(src, _packed_row(xs_ref, slots[r][kk]), sem).start(priority=kk % 2)
        return c

    lax.fori_loop(0, tn // DISP_UNROLL, issue, 0)
    n = TOP_K * tn * ROW_CHUNKS
    pltpu.make_async_copy(xs_ref.at[pl.ds(0, n), :], xs_ref.at[pl.ds(0, n), :], sem).wait()


def _dispatch_call(dest, h1rows, n_rows):
    t = dest.shape[1]
    tn = TN_DISP
    return pl.pallas_call(
        _dispatch_kernel,
        grid=(t // tn,),
        in_specs=[pl.BlockSpec((TOP_K, tn), lambda i: (0, i), memory_space=pltpu.SMEM),
                  pl.BlockSpec((tn * ROW_CHUNKS, LANES), lambda i: (i, 0))],
        out_specs=pl.BlockSpec(memory_space=pl.ANY),
        out_shape=SDS((n_rows * ROW_CHUNKS, LANES), U32),
        scratch_shapes=[pltpu.SemaphoreType.DMA(())],
        compiler_params=_cparams("arbitrary"),
        name="dispatch",
    )(dest, h1rows)


SC_CORES = 2
SC_SUBCORES = 16
SC_CHUNK = 128


def _sc_dispatch_call(dest, h1rows3, n_rows):
    t = dest.shape[1]
    workers = SC_CORES * SC_SUBCORES
    per_worker = t // workers
    assert per_worker % SC_CHUNK == 0
    mesh = plsc.VectorSubcoreMesh(core_axis_name="c", subcore_axis_name="s")

    @functools.partial(
        pl.kernel, mesh=mesh, out_type=SDS((n_rows, ROW_CHUNKS, LANES), U32),
        scratch_types=[pltpu.VMEM((TOP_K, SC_CHUNK), I32), pltpu.VMEM((SC_CHUNK, ROW_CHUNKS, LANES), U32),
                       pltpu.SemaphoreType.DMA],
        name="sc_dispatch")
    def body(h_hbm, dest_hbm, xs_hbm, idx_v, rows_v, sem):
        wid = lax.axis_index("s") * SC_CORES + lax.axis_index("c")

        @pl.loop(0, per_worker // SC_CHUNK)
        def _(i):
            t0 = wid * per_worker + i * SC_CHUNK
            pltpu.sync_copy(dest_hbm.at[:, pl.ds(t0, SC_CHUNK)], idx_v)
            pltpu.sync_copy(h_hbm.at[pl.ds(t0, SC_CHUNK)], rows_v)
            copies = [pltpu.async_copy(rows_v, xs_hbm.at[idx_v.at[kk]], sem) for kk in range(TOP_K)]
            for c in copies:
                c.wait()

    return body(h1rows3, dest)


def _expert_kernel(ts_ref, te_ref, tr_ref, nv_ref, wg_ref, wu_ref, wd_ref, xs_hbm, ys_hbm,
                   xbuf, ybuf, wg_b, wu_b, wd_b, xsem, ysem):
    e = pl.program_id(0)
    rows = xbuf.shape[1]
    tm = rows // ROW_CHUNKS
    g0, g1, nv = ts_ref[e], te_ref[e], nv_ref[0]

    def x_copy(g):
        s = g % X_SLOTS
        return pltpu.make_async_copy(xs_hbm.at[pl.ds(pl.multiple_of(g * rows, rows), rows), :], xbuf.at[s], xsem.at[s])

    def y_copy(g):
        s = g % Y_SLOTS
        return pltpu.make_async_copy(ybuf.at[s], ys_hbm.at[pl.ds(pl.multiple_of(g * rows, rows), rows), :], ysem.at[s])

    def compute_tile(g):
        x = _load_packed_bf16(xbuf, 0, tm, lead=g % X_SLOTS)
        gate = jnp.dot(x, wg_b[...], preferred_element_type=F32)
        up = jnp.dot(x, wu_b[...], preferred_element_type=F32)
        live = lax.broadcasted_iota(I32, (tm, EXPERT_FF), 0) < tr_ref[g]
        hid = jnp.where(live, gate * _sigmoid(gate) * up, 0.0).astype(BF16)
        y = jnp.dot(hid, wd_b[...], preferred_element_type=F32)
        return _pack_rows(y[:, :HALF], y[:, HALF:])

    def run_tiles(g, n):
        for r in range(n):
            x_copy(g + r).wait()

            @pl.when(g + r + X_AHEAD < nv)
            def _():
                x_copy(g + r + X_AHEAD).start(priority=1)

            @pl.when(g + r >= Y_SLOTS)
            def _():
                y_copy(g + r - Y_SLOTS).wait()

        packed = [compute_tile(g + r) for r in range(n)]
        for r in range(n):
            _store_packed(ybuf, 0, tm, packed[r], lead=(g + r) % Y_SLOTS)
        for r in range(n):
            y_copy(g + r).start(priority=1)

    @pl.when(e == 0)
    def _():
        for g in range(X_AHEAD):
            @pl.when(g < nv)
            def _():
                x_copy(g).start(priority=1)

    @pl.when(g1 > g0)
    def _():
        wg_b[...] = wg_ref[0].astype(BF16)
        wu_b[...] = wu_ref[0].astype(BF16)
        wd_b[...] = wd_ref[0].astype(BF16)
        n_tiles = g1 - g0

        def pair(p, c):
            run_tiles(g0 + 2 * p, 2)
            return c

        lax.fori_loop(0, n_tiles // 2, pair, 0)

        @pl.when(n_tiles % 2 == 1)
        def _():
            run_tiles(g1 - 1, 1)

    @pl.when(e == pl.num_programs(0) - 1)
    def _():
        for back in range(1, Y_SLOTS + 1):
            @pl.when(nv >= back)
            def _():
                y_copy(nv - back).wait()


def _expert_call(tile_start, tile_end, tile_rows, n_valid, xs, w_gate, w_up, w_down, n_rows):
    tm = TM_EXP
    w_map = lambda e, *_: (e, 0, 0)
    hbm = pl.BlockSpec(memory_space=pl.ANY)
    return pl.pallas_call(
        _expert_kernel,
        grid_spec=pltpu.PrefetchScalarGridSpec(
            num_scalar_prefetch=4,
            grid=(N_EXPERTS,),
            in_specs=[pl.BlockSpec((1, D_MODEL, EXPERT_FF), w_map), pl.BlockSpec((1, D_MODEL, EXPERT_FF), w_map),
                      pl.BlockSpec((1, EXPERT_FF, D_MODEL), w_map), hbm],
            out_specs=hbm,
            scratch_shapes=[pltpu.VMEM((X_SLOTS, tm * ROW_CHUNKS, LANES), U32),
                            pltpu.VMEM((Y_SLOTS, tm * ROW_CHUNKS, LANES), U32),
                            pltpu.VMEM((D_MODEL, EXPERT_FF), BF16), pltpu.VMEM((D_MODEL, EXPERT_FF), BF16),
                            pltpu.VMEM((EXPERT_FF, D_MODEL), BF16),
                            pltpu.SemaphoreType.DMA((X_SLOTS,)), pltpu.SemaphoreType.DMA((Y_SLOTS,))],
        ),
        out_shape=SDS((n_rows * ROW_CHUNKS, LANES), U32),
        compiler_params=_cparams("arbitrary"),
        name="experts",
    )(tile_start, tile_end, tile_rows, n_valid, w_gate, w_up, w_down, xs)


def _combine_kernel(dest_ref, dest_next_ref, wts_ref, h1_ref, ys_ref, wsg_ref, wsu_ref, wsd_ref, g2_ref, b2_ref,
                    o_ref, ybuf_ref, routed_ref, sem):
    i = pl.program_id(0)
    tn = h1_ref.shape[0]
    rows = TOP_K * tn
    slot = i % 2
    has_next = i + 1 < pl.num_programs(0)

    def issue_group(d_ref, s, j):
        srcs = [[d_ref[kk, j * SUBLANES + r] for kk in range(TOP_K)] for r in range(SUBLANES)]
        for r in range(SUBLANES):
            for kk in range(TOP_K):
                pltpu.make_async_copy(_packed_row(ys_ref, srcs[r][kk]),
                                      _packed_row(ybuf_ref, s * rows + kk * tn + j * SUBLANES + r),
                                      sem.at[s]).start(priority=kk % 2)

    @pl.when(i == 0)
    def _():
        def first(j, c):
            issue_group(dest_ref, 0, j)
            return c

        lax.fori_loop(0, tn // SUBLANES, first, 0)

    base = slot * rows
    pltpu.make_async_copy(
        ys_ref.at[pl.ds(0, rows * ROW_CHUNKS), :],
        ybuf_ref.at[pl.ds(pl.multiple_of(base * ROW_CHUNKS, rows * ROW_CHUNKS), rows * ROW_CHUNKS), :],
        sem.at[slot]).wait()

    def reduce_group(j):
        t0 = pl.multiple_of(j * SUBLANES, SUBLANES)
        acc = [jnp.zeros((SUBLANES, LANES), F32) for _ in range(2 * ROW_CHUNKS)]
        for kk in range(TOP_K):
            wk = wts_ref[pl.ds(t0, SUBLANES), kk:kk + 1]
            for cc in range(ROW_CHUNKS):
                start = (base + kk * tn + t0) * ROW_CHUNKS + cc
                lo, hi = _unpack_rows(ybuf_ref[pl.ds(start, SUBLANES, stride=ROW_CHUNKS), :])
                acc[cc] = acc[cc] + wk * lo
                acc[ROW_CHUNKS + cc] = acc[ROW_CHUNKS + cc] + wk * hi
        routed_ref[pl.ds(t0, SUBLANES), :] = jnp.concatenate(acc, axis=1)

    @pl.when(has_next)
    def _():
        def group(j, c):
            reduce_group(j)
            issue_group(dest_next_ref, 1 - slot, j)
            return c

        lax.fori_loop(0, tn // SUBLANES, group, 0)

    @pl.when(jnp.logical_not(has_next))
    def _():
        def group(j, c):
            reduce_group(j)
            return c

        lax.fori_loop(0, tn // SUBLANES, group, 0)

    h1 = h1_ref[...]
    hb = h1.astype(BF16)
    sg = jnp.dot(hb, wsg_ref[...], preferred_element_type=F32)
    su = jnp.dot(hb, wsu_ref[...], preferred_element_type=F32)
    ff = jnp.dot((sg * _sigmoid(sg) * su).astype(BF16), wsd_ref[...], preferred_element_type=F32)
    o_ref[...] = _layer_norm(ALPHA * h1 + ff + routed_ref[...], g2_ref[...], b2_ref[...])


def _combine_call(dest, wts, h1, ys, wsg, wsu, wsd, g2, b2):
    t = h1.shape[0]
    tn = TN_COMB
    nsteps = t // tn
    col = lambda i: (0, i)
    col_next = lambda i: (0, jnp.minimum(i + 1, nsteps - 1))
    row = lambda i: (i, 0)
    fix = lambda i: (0, 0)
    return pl.pallas_call(
        _combine_kernel,
        grid=(nsteps,),
        in_specs=[pl.BlockSpec((TOP_K, tn), col, memory_space=pltpu.SMEM),
                  pl.BlockSpec((TOP_K, tn), col_next, memory_space=pltpu.SMEM),
                  pl.BlockSpec((tn, TOP_K), row),
                  pl.BlockSpec((tn, D_MODEL), row),
                  pl.BlockSpec(memory_space=pl.ANY),
                  pl.BlockSpec((D_MODEL, SHARED_FF), fix), pl.BlockSpec((D_MODEL, SHARED_FF), fix),
                  pl.BlockSpec((SHARED_FF, D_MODEL), fix),
                  pl.BlockSpec((1, D_MODEL), fix), pl.BlockSpec((1, D_MODEL), fix)],
        out_specs=pl.BlockSpec((tn, D_MODEL), row),
        out_shape=SDS((t, D_MODEL), F32),
        scratch_shapes=[pltpu.VMEM((2 * TOP_K * tn * ROW_CHUNKS, LANES), U32), pltpu.VMEM((tn, D_MODEL), F32),
                        pltpu.SemaphoreType.DMA((2,))],
        compiler_params=_cparams("arbitrary"),
        name="combine_ln2",
    )(dest, dest, wts.T, h1, ys, wsg, wsu, wsd, g2, b2)


def kernel(x, meta_tokens, ln_in_g, ln_in_b, rel_bias, w_in, conv_w, conv_b, conv_ln_g, conv_ln_b, sinks,
           w_out, ln1_g, ln1_b, w_router, router_bias, w_gate, w_up, w_down, ws_gate, ws_up, ws_down,
           ln2_g, ln2_b):
    nbatch, seq, d = x.shape
    assert d == D_MODEL and seq % TQ_PROJ == 0 and w_in.shape[0] == DEPTH
    t = nbatch * seq
    x2d = x.reshape(t, D_MODEL)
    vec = lambda a: a.reshape(1, -1).astype(F32)
    gin, bin_ = vec(ln_in_g), vec(ln_in_b)
    w_in_b = w_in[0].astype(BF16)

    q, k, v, u = _proj_call(x2d, gin, bin_, w_in_b, TQ_PROJ)
    meta_blk = jnp.concatenate([jnp.zeros((PAD_FRONT, D_MODEL), F32), meta_tokens.astype(F32)], axis=0)
    _, k_meta, v_meta, u_meta = _proj_call(meta_blk, gin, bin_, w_in_b, BLOCK)

    attn = _attn_call(q, k, v, k_meta, v_meta, _rel_bias_table(rel_bias), sinks[0].astype(F32),
                      nbatch, seq // BLOCK)

    u_halo = jnp.concatenate([jnp.zeros((CONV_HALO - N_META, CONV_CH), F32), u_meta[PAD_FRONT:]], axis=0)
    conv = _conv_call(u, u_halo, conv_w[0].astype(F32), vec(conv_b[0]), vec(conv_ln_g[0]), vec(conv_ln_b[0]),
                      nbatch, seq)

    w_out_b = w_out[0].astype(BF16)
    wr_t = w_router[0].astype(F32).T
    wr_hi = wr_t.astype(BF16)
    wr_lo = (wr_t - wr_hi.astype(F32)).astype(BF16)
    h1, h1rows, logits = _mix_call(x2d, attn, conv, gin, bin_, w_out_b[:ATTN_W], w_out_b[ATTN_W:],
                                   vec(ln1_g[0]), vec(ln1_b[0]), wr_hi, wr_lo)

    idx, wts, rank, cnt = _route_call(logits, router_bias[0].astype(F32).reshape(N_EXPERTS, 1))

    tm = TM_EXP
    n_tiles = (t * TOP_K) // tm + N_EXPERTS
    counts = cnt[:, 0]
    tiles_e = (counts + tm - 1) // tm
    tile_end = jnp.cumsum(tiles_e).astype(I32)
    tile_start = (tile_end - tiles_e).astype(I32)
    offs = tile_start * tm
    tile_id = jnp.arange(n_tiles, dtype=I32)
    lo = jnp.maximum(tile_id[:, None] * tm, offs[None, :])
    hi = jnp.minimum((tile_id[:, None] + 1) * tm, (offs + counts)[None, :])
    tile_rows = jnp.sum(jnp.clip(hi - lo, 0, tm), axis=1).astype(I32)
    n_valid = tile_end[-1:]

    dest = _dest_call(idx, rank, offs.astype(F32).reshape(N_EXPERTS, 1))
    xs = _sc_dispatch_call(dest, h1rows.reshape(t, ROW_CHUNKS, LANES), n_tiles * tm)
    xs = xs.reshape(n_tiles * tm * ROW_CHUNKS, LANES)
    ys = _expert_call(tile_start, tile_end, tile_rows, n_valid, xs, w_gate[0], w_up[0], w_down[0], n_tiles * tm)
    out = _combine_call(dest, wts, h1, ys, ws_gate[0].astype(BF16), ws_up[0].astype(BF16),
                        ws_down[0].astype(BF16), vec(ln2_g[0]), vec(ln2_b[0]))
    return out.reshape(nbatch, seq, D_MODEL)
```

```python
import functools
import math

import numpy as np
import jax
import jax.numpy as jnp
from jax import lax
from jax.experimental import pallas as pl
from jax.experimental.pallas import tpu as pltpu
from jax.experimental.pallas import tpu_sc as plsc

F32 = jnp.float32
BF16 = jnp.bfloat16
I32 = jnp.int32
U32 = jnp.uint32
SDS = jax.ShapeDtypeStruct

D_MODEL = 1024
HALF = D_MODEL // 2
LANES = 128
SUBLANES = 8
ROW_CHUNKS = HALF // LANES
N_META = 16
HEAD_DIM = 64
N_Q_HEADS = 8
N_KV_HEADS = 2
GQA_GROUP = N_Q_HEADS // N_KV_HEADS
ATTN_W = N_Q_HEADS * HEAD_DIM
KV_W = N_KV_HEADS * HEAD_DIM
WINDOW = 128
BLOCK = 128
CONV_CH = D_MODEL - ATTN_W
CONV_K = 31
IN_W = ATTN_W + 2 * KV_W + 2 * CONV_CH
NUM_BUCKETS = 32
MAX_EXACT = NUM_BUCKETS // 2
REL_MAX_DIST = 128
N_EXPERTS = 256
TOP_K = 8
N_GROUPS = 8
GROUP_SIZE = N_EXPERTS // N_GROUPS
TOPK_GROUPS = 4
EXPERT_FF = 256
SHARED_FF = 256
ROUTED_SCALE = 2.5
DEPTH = 1
ALPHA = (2.0 * DEPTH) ** 0.25
LN_EPS = 1e-5
NEG = -1e30
PAD_FRONT = (-N_META) % BLOCK

VMEM_LIMIT = 48 * 1024 * 1024

TQ_PROJ = 512
ATTN_QBLOCKS = 2
T_CONV = 256
CONV_HALO = 32
R_CONV = 64
TQ_MIX = 512
MIX_CHAINS = 2
TN_ROUTE = 256
TM_EXP = 256
X_SLOTS = 8
X_AHEAD = 4
Y_SLOTS = 4
TN_COMB = 256


def _cparams(*sem):
    return pltpu.CompilerParams(dimension_semantics=sem, vmem_limit_bytes=VMEM_LIMIT)


def _layer_norm(x, g, b):
    mu = jnp.mean(x, axis=-1, keepdims=True)
    xc = x - mu
    var = jnp.mean(xc * xc, axis=-1, keepdims=True)
    return xc * lax.rsqrt(var + LN_EPS) * g + b


def _sigmoid(x):
    return 1.0 / (1.0 + jnp.exp(-x))


def _pack_rows(lo_half, hi_half):
    lo = lax.bitcast_convert_type(lo_half.astype(BF16).astype(F32), U32)
    hi = lax.bitcast_convert_type(hi_half.astype(BF16).astype(F32), U32)
    return lax.shift_right_logical(lo, jnp.uint32(16)) | hi


def _unpack_rows(p):
    lo = lax.bitcast_convert_type(lax.shift_left(p, jnp.uint32(16)), F32)
    hi = lax.bitcast_convert_type(p & jnp.uint32(0xFFFF0000), F32)
    return lo, hi


def _chunk_index(start, j, n, lead):
    rows = pl.ds(start + j, n, stride=ROW_CHUNKS)
    return (rows, slice(None)) if lead is None else (lead, rows, slice(None))


def _store_packed(ref, start, n, packed, lead=None):
    for j in range(ROW_CHUNKS):
        ref[_chunk_index(start, j, n, lead)] = packed[:, j * LANES:(j + 1) * LANES]


def _load_packed_bf16(ref, start, n, lead=None):
    halves = [_unpack_rows(ref[_chunk_index(start, j, n, lead)]) for j in range(ROW_CHUNKS)]
    return jnp.concatenate([h[0] for h in halves] + [h[1] for h in halves], axis=1).astype(BF16)


def _proj_kernel(x_ref, g_ref, b_ref, w_ref, q_ref, k_ref, v_ref, u_ref):
    h = _layer_norm(x_ref[...], g_ref[...], b_ref[...])
    p = jnp.dot(h.astype(BF16), w_ref[...], preferred_element_type=F32)
    q_ref[...] = (p[:, :ATTN_W] * (HEAD_DIM ** -0.5)).astype(BF16)
    k_ref[...] = p[:, ATTN_W:ATTN_W + KV_W].astype(BF16)
    v_ref[...] = p[:, ATTN_W + KV_W:ATTN_W + 2 * KV_W].astype(BF16)
    a = p[:, ATTN_W + 2 * KV_W:ATTN_W + 2 * KV_W + CONV_CH]
    gate = p[:, ATTN_W + 2 * KV_W + CONV_CH:]
    u_ref[...] = a * _sigmoid(gate)


def _proj_call(x2d, gin, bin_, w_in_b, tq):
    t = x2d.shape[0]
    row = lambda i: (i, 0)
    fix = lambda i: (0, 0)
    return pl.pallas_call(
        _proj_kernel,
        grid=(t // tq,),
        in_specs=[pl.BlockSpec((tq, D_MODEL), row), pl.BlockSpec((1, D_MODEL), fix),
                  pl.BlockSpec((1, D_MODEL), fix), pl.BlockSpec((D_MODEL, IN_W), fix)],
        out_specs=[pl.BlockSpec((tq, ATTN_W), row), pl.BlockSpec((tq, KV_W), row),
                   pl.BlockSpec((tq, KV_W), row), pl.BlockSpec((tq, CONV_CH), row)],
        out_shape=[SDS((t, ATTN_W), BF16), SDS((t, KV_W), BF16), SDS((t, KV_W), BF16), SDS((t, CONV_CH), F32)],
        compiler_params=_cparams("arbitrary"),
        name="ln_in_proj",
    )(x2d, gin, bin_, w_in_b)


def _attn_kernel(sinks_ref, q_ref, kc_ref, kp_ref, vc_ref, vp_ref, km_ref, vm_ref, bias_ref, o_ref):
    first = pl.program_id(1) == 0
    kp = jnp.where(first, km_ref[...], kp_ref[...])
    vp = jnp.where(first, vm_ref[...], vp_ref[...])
    k = jnp.concatenate([kp, kc_ref[...]], axis=0)
    v = jnp.concatenate([vp, vc_ref[...]], axis=0)
    col = lax.broadcasted_iota(I32, (BLOCK, 2 * BLOCK), 1)
    pad_bias = jnp.where(jnp.logical_and(first, col < PAD_FRONT), NEG, 0.0).astype(F32)
    for a in range(ATTN_QBLOCKS):
        q = q_ref[a * BLOCK:(a + 1) * BLOCK, :]
        kw = k[a * BLOCK:(a + 2) * BLOCK, :]
        vw = v[a * BLOCK:(a + 2) * BLOCK, :]
        outs = []
        for h in range(N_Q_HEADS):
            g = h // GQA_GROUP
            qh = q[:, h * HEAD_DIM:(h + 1) * HEAD_DIM]
            kg = kw[:, g * HEAD_DIM:(g + 1) * HEAD_DIM]
            vg = vw[:, g * HEAD_DIM:(g + 1) * HEAD_DIM]
            s = lax.dot_general(qh, kg, (((1,), (1,)), ((), ())), preferred_element_type=F32)
            s = s + bias_ref[h]
            if a == 0:
                s = s + pad_bias
            sink = sinks_ref[h]
            m = jnp.maximum(jnp.max(s, axis=-1, keepdims=True), sink)
            p = jnp.exp(s - m)
            den = jnp.sum(p, axis=-1, keepdims=True) + jnp.exp(sink - m)
            o = jnp.dot(p.astype(BF16), vg, preferred_element_type=F32)
            outs.append(o / den)
        o_ref[a * BLOCK:(a + 1) * BLOCK, :] = jnp.concatenate(outs, axis=1).astype(BF16)


def _attn_call(q, k, v, k_meta, v_meta, bias, sinks, nbatch, nblk):
    t = q.shape[0]
    nq = ATTN_QBLOCKS
    assert nblk % nq == 0
    nstep = nblk // nq
    cur = lambda b, j: (b * nstep + j, 0)
    prev = lambda b, j: (jnp.maximum((b * nstep + j) * nq - 1, 0), 0)
    fix2 = lambda b, j: (0, 0)
    return pl.pallas_call(
        _attn_kernel,
        grid=(nbatch, nstep),
        in_specs=[pl.BlockSpec(memory_space=pltpu.SMEM),
                  pl.BlockSpec((nq * BLOCK, ATTN_W), cur),
                  pl.BlockSpec((nq * BLOCK, KV_W), cur), pl.BlockSpec((BLOCK, KV_W), prev),
                  pl.BlockSpec((nq * BLOCK, KV_W), cur), pl.BlockSpec((BLOCK, KV_W), prev),
                  pl.BlockSpec((BLOCK, KV_W), fix2), pl.BlockSpec((BLOCK, KV_W), fix2),
                  pl.BlockSpec((N_Q_HEADS, BLOCK, 2 * BLOCK), lambda b, j: (0, 0, 0))],
        out_specs=pl.BlockSpec((nq * BLOCK, ATTN_W), cur),
        out_shape=SDS((t, ATTN_W), BF16),
        compiler_params=_cparams("arbitrary", "arbitrary"),
        name="swa_attn",
    )(sinks, q, k, k, v, v, k_meta, v_meta, bias)


def _rel_bias_table(rel_bias):
    qi = np.arange(BLOCK, dtype=np.int32)[:, None]
    kj = np.arange(2 * BLOCK, dtype=np.int32)[None, :]
    dist = BLOCK + qi - kj
    dc = np.clip(dist, 0, WINDOW - 1)
    nf = np.maximum(dc, 1).astype(np.float32)
    large = MAX_EXACT + (np.log(nf / np.float32(MAX_EXACT)) / np.float32(math.log(REL_MAX_DIST / MAX_EXACT))
                         * np.float32(NUM_BUCKETS - MAX_EXACT)).astype(np.int32)
    large = np.minimum(large, NUM_BUCKETS - 1)
    bucket = np.where(dc < MAX_EXACT, dc, large)
    in_window = (dist >= 0) & (dist < WINDOW)
    onehot = (bucket.reshape(-1, 1) == np.arange(NUM_BUCKETS)[None, :]).astype(np.float32)
    bias = jnp.dot(jnp.asarray(onehot), rel_bias.astype(F32), precision=lax.Precision.HIGHEST)
    bias = jnp.transpose(bias.reshape(BLOCK, 2 * BLOCK, N_Q_HEADS), (2, 0, 1))
    return jnp.where(in_window[None], bias, NEG)


def _conv_kernel(uc_ref, up_ref, um_ref, w_ref, cb_ref, g_ref, b_ref, o_ref, s_ref, sh_ref):
    first = pl.program_id(1) == 0
    s_ref[0:CONV_HALO, :] = jnp.where(first, um_ref[...], up_ref[...])
    s_ref[CONV_HALO:CONV_HALO + T_CONV, :] = uc_ref[...]
    off = CONV_HALO - (CONV_K - 1)
    span = sh_ref.shape[1]
    for c in range(0, T_CONV, R_CONV):
        for p in range(1, SUBLANES):
            sh_ref[p] = s_ref[c + p:c + p + span, :]
        acc = jnp.zeros((R_CONV, CONV_CH), F32) + cb_ref[...]
        for kk in range(CONV_K):
            p, a = (off + kk) % SUBLANES, (off + kk) // SUBLANES * SUBLANES
            if p == 0:
                win = s_ref[c + a:c + a + R_CONV, :]
            else:
                win = sh_ref[p, a:a + R_CONV, :]
            acc = acc + win * w_ref[kk:kk + 1, :]
        y = _layer_norm(acc, g_ref[...], b_ref[...])
        o_ref[c:c + R_CONV, :] = (y * _sigmoid(y)).astype(BF16)


def _conv_call(u, u_meta_halo, conv_w, conv_b, g, b, nbatch, seq):
    t = u.shape[0]
    nj = seq // T_CONV
    per = T_CONV // CONV_HALO
    cur = lambda bb, j: (bb * nj + j, 0)
    prev = lambda bb, j: (jnp.maximum((bb * nj + j) * per - 1, 0), 0)
    fix = lambda bb, j: (0, 0)
    return pl.pallas_call(
        _conv_kernel,
        grid=(nbatch, nj),
        in_specs=[pl.BlockSpec((T_CONV, CONV_CH), cur), pl.BlockSpec((CONV_HALO, CONV_CH), prev),
                  pl.BlockSpec((CONV_HALO, CONV_CH), fix), pl.BlockSpec((CONV_K, CONV_CH), fix),
                  pl.BlockSpec((1, CONV_CH), fix), pl.BlockSpec((1, CONV_CH), fix), pl.BlockSpec((1, CONV_CH), fix)],
        out_specs=pl.BlockSpec((T_CONV, CONV_CH), cur),
        out_shape=SDS((t, CONV_CH), BF16),
        scratch_shapes=[pltpu.VMEM((CONV_HALO + T_CONV, CONV_CH), F32),
                        pltpu.VMEM((SUBLANES, R_CONV + CONV_HALO - SUBLANES, CONV_CH), F32)],
        compiler_params=_cparams("arbitrary", "arbitrary"),
        name="conv_ln",
    )(u, u, u_meta_halo, conv_w, conv_b, g, b)


def _mix_kernel(x_ref, at_ref, cv_ref, gin_ref, bin_ref, woa_ref, woc_ref, g1_ref, b1_ref,
                wrh_ref, wrl_ref, h1_ref, h1r_ref, lg_ref):
    rows = x_ref.shape[0] // MIX_CHAINS
    nt = (((1,), (1,)), ((), ()))
    for c in range(MIX_CHAINS):
        r = slice(c * rows, (c + 1) * rows)
        h = _layer_norm(x_ref[r, :], gin_ref[...], bin_ref[...])
        mix = (jnp.dot(at_ref[r, :], woa_ref[...], preferred_element_type=F32)
               + jnp.dot(cv_ref[r, :], woc_ref[...], preferred_element_type=F32))
        h1 = _layer_norm(ALPHA * h + mix, g1_ref[...], b1_ref[...])
        h1_ref[r, :] = h1
        _store_packed(h1r_ref, c * rows * ROW_CHUNKS, rows, _pack_rows(h1[:, :HALF], h1[:, HALF:]))
        hh = h1.astype(BF16)
        hl = (h1 - hh.astype(F32)).astype(BF16)
        lg = lax.dot_general(wrh_ref[...], hh, nt, preferred_element_type=F32)
        lg = lg + lax.dot_general(wrh_ref[...], hl, nt, preferred_element_type=F32)
        lg = lg + lax.dot_general(wrl_ref[...], hh, nt, preferred_element_type=F32)
        lg_ref[:, r] = lg


def _mix_call(x2d, attn, conv, gin, bin_, woa, woc, g1, b1, wrh, wrl):
    t = x2d.shape[0]
    tq = TQ_MIX
    row = lambda i: (i, 0)
    fix = lambda i: (0, 0)
    return pl.pallas_call(
        _mix_kernel,
        grid=(t // tq,),
        in_specs=[pl.BlockSpec((tq, D_MODEL), row), pl.BlockSpec((tq, ATTN_W), row), pl.BlockSpec((tq, CONV_CH), row),
                  pl.BlockSpec((1, D_MODEL), fix), pl.BlockSpec((1, D_MODEL), fix),
                  pl.BlockSpec((ATTN_W, D_MODEL), fix), pl.BlockSpec((CONV_CH, D_MODEL), fix),
                  pl.BlockSpec((1, D_MODEL), fix), pl.BlockSpec((1, D_MODEL), fix),
                  pl.BlockSpec((N_EXPERTS, D_MODEL), fix), pl.BlockSpec((N_EXPERTS, D_MODEL), fix)],
        out_specs=[pl.BlockSpec((tq, D_MODEL), row), pl.BlockSpec((tq * ROW_CHUNKS, LANES), row),
                   pl.BlockSpec((N_EXPERTS, tq), lambda i: (0, i))],
        out_shape=[SDS((t, D_MODEL), F32), SDS((t * ROW_CHUNKS, LANES), U32), SDS((N_EXPERTS, t), F32)],
        compiler_params=_cparams("arbitrary"),
        name="mix_ln1",
    )(x2d, attn, conv, gin, bin_, woa, woc, g1, b1, wrh, wrl)


def _first_argmax(x, rows, nrows):
    m = jnp.max(x, axis=0, keepdims=True)
    idx = jnp.min(jnp.where(x == m, rows, nrows), axis=0, keepdims=True)
    return m, idx


def _route_kernel(lg_ref, rb_ref, idx_ref, wts_ref, rank_ref, cnt_ref, carry_ref):
    tn = lg_ref.shape[1]

    @pl.when(pl.program_id(0) == 0)
    def _():
        carry_ref[...] = jnp.zeros_like(carry_ref)

    scores = _sigmoid(lg_ref[...])
    choice = scores + rb_ref[...]
    rows = lax.broadcasted_iota(I32, (N_EXPERTS, tn), 0)
    rows_g = lax.broadcasted_iota(I32, (GROUP_SIZE, tn), 0)
    rows_8 = lax.broadcasted_iota(I32, (N_GROUPS, tn), 0)

    gs = []
    for g in range(N_GROUPS):
        xg = choice[g * GROUP_SIZE:(g + 1) * GROUP_SIZE, :]
        m1, i1 = _first_argmax(xg, rows_g, GROUP_SIZE)
        m2 = jnp.max(jnp.where(rows_g == i1, -jnp.inf, xg), axis=0, keepdims=True)
        gs.append(m1 + m2)
    gsc = jnp.concatenate(gs, axis=0)
    gsel = jnp.zeros((N_GROUPS, tn), F32)
    for _ in range(TOPK_GROUPS):
        _, gi = _first_argmax(gsc, rows_8, N_GROUPS)
        hit = rows_8 == gi
        gsel = jnp.where(hit, 1.0, gsel)
        gsc = jnp.where(hit, -jnp.inf, gsc)
    emask = jnp.concatenate(
        [jnp.broadcast_to(gsel[g:g + 1, :], (GROUP_SIZE, tn)) for g in range(N_GROUPS)], axis=0)
    masked = jnp.where(emask > 0.5, choice, NEG)

    sel_all = jnp.zeros((N_EXPERTS, tn), F32)
    hits, idxs, ws = [], [], []
    for _ in range(TOP_K):
        _, ii = _first_argmax(masked, rows, N_EXPERTS)
        hit = rows == ii
        hits.append(hit)
        idxs.append(ii)
        ws.append(jnp.sum(jnp.where(hit, scores, 0.0), axis=0, keepdims=True))
        sel_all = jnp.where(hit, 1.0, sel_all)
        masked = jnp.where(hit, -jnp.inf, masked)
    wsum = ws[0]
    for w in ws[1:]:
        wsum = wsum + w
    idx_ref[...] = jnp.concatenate(idxs, axis=0)
    wts_ref[...] = jnp.concatenate([w / wsum * ROUTED_SCALE for w in ws], axis=0)

    r_i = lax.broadcasted_iota(I32, (tn, tn), 0)
    c_i = lax.broadcasted_iota(I32, (tn, tn), 1)
    upper = jnp.where(r_i < c_i, 1.0, 0.0).astype(BF16)
    sel_b = sel_all.astype(BF16)
    carry = carry_ref[...]
    before = jnp.dot(sel_b, upper, preferred_element_type=F32)
    before = before + jnp.concatenate([carry] * (tn // 128), axis=1)
    rank_ref[...] = jnp.concatenate(
        [jnp.sum(jnp.where(h, before, 0.0), axis=0, keepdims=True) for h in hits], axis=0).astype(I32)
    carry = carry + jnp.dot(sel_b, jnp.ones((tn, 128), BF16), preferred_element_type=F32)
    carry_ref[...] = carry
    cnt_ref[...] = carry.astype(I32)


def _route_call(lg, rbias):
    t = lg.shape[1]
    tn = TN_ROUTE
    col = lambda i: (0, i)
    return pl.pallas_call(
        _route_kernel,
        grid=(t // tn,),
        in_specs=[pl.BlockSpec((N_EXPERTS, tn), col), pl.BlockSpec((N_EXPERTS, 1), lambda i: (0, 0))],
        out_specs=[pl.BlockSpec((TOP_K, tn), col), pl.BlockSpec((TOP_K, tn), col), pl.BlockSpec((TOP_K, tn), col),
                   pl.BlockSpec((N_EXPERTS, 128), lambda i: (0, 0))],
        out_shape=[SDS((TOP_K, t), I32), SDS((TOP_K, t), F32), SDS((TOP_K, t), I32), SDS((N_EXPERTS, 128), I32)],
        scratch_shapes=[pltpu.VMEM((N_EXPERTS, 128), F32)],
        compiler_params=_cparams("arbitrary"),
        name="route",
    )(lg, rbias)


def _dest_kernel(idx_ref, rank_ref, offs_ref, dest_ref):
    tn = idx_ref.shape[1]
    rows = lax.broadcasted_iota(I32, (N_EXPERTS, tn), 0)
    offs = offs_ref[...]
    out = []
    for kk in range(TOP_K):
        hit = rows == idx_ref[kk:kk + 1, :]
        out.append(jnp.sum(jnp.where(hit, offs, 0.0), axis=0, keepdims=True))
    dest_ref[...] = jnp.concatenate(out, axis=0).astype(I32) + rank_ref[...]


def _dest_call(idx, rank, offs_col):
    t = idx.shape[1]
    tn = TN_ROUTE
    col = lambda i: (0, i)
    return pl.pallas_call(
        _dest_kernel,
        grid=(t // tn,),
        in_specs=[pl.BlockSpec((TOP_K, tn), col), pl.BlockSpec((TOP_K, tn), col),
                  pl.BlockSpec((N_EXPERTS, 1), lambda i: (0, 0))],
        out_specs=pl.BlockSpec((TOP_K, tn), col),
        out_shape=SDS((TOP_K, t), I32),
        compiler_params=_cparams("arbitrary"),
        name="dest",
    )(idx, rank, offs_col)


SC_CORES = 2
SC_SUBCORES = 16
SC_CHUNK = 128


def _sc_worker_chunks(t):
    per_worker = t // (SC_CORES * SC_SUBCORES)
    assert per_worker % SC_CHUNK == 0
    return per_worker


def _sc_dispatch_call(dest, h1rows3, n_rows):
    per_worker = _sc_worker_chunks(dest.shape[1])
    mesh = plsc.VectorSubcoreMesh(core_axis_name="c", subcore_axis_name="s")

    @functools.partial(
        pl.kernel, mesh=mesh, out_type=SDS((n_rows, ROW_CHUNKS, LANES), U32),
        scratch_types=[pltpu.VMEM((TOP_K, SC_CHUNK), I32), pltpu.VMEM((SC_CHUNK, ROW_CHUNKS, LANES), U32),
                       pltpu.SemaphoreType.DMA],
        name="sc_dispatch")
    def body(h_hbm, dest_hbm, xs_hbm, idx_v, rows_v, sem):
        wid = lax.axis_index("s") * SC_CORES + lax.axis_index("c")

        @pl.loop(0, per_worker // SC_CHUNK)
        def _(i):
            t0 = wid * per_worker + i * SC_CHUNK
            pltpu.sync_copy(dest_hbm.at[:, pl.ds(t0, SC_CHUNK)], idx_v)
            pltpu.sync_copy(h_hbm.at[pl.ds(t0, SC_CHUNK)], rows_v)
            copies = [pltpu.async_copy(rows_v, xs_hbm.at[idx_v.at[kk]], sem) for kk in range(TOP_K)]
            for c in copies:
                c.wait()

    return body(h1rows3, dest)


def _sc_gather_call(dest, ys3):
    t = dest.shape[1]
    per_worker = _sc_worker_chunks(t)
    mesh = plsc.VectorSubcoreMesh(core_axis_name="c", subcore_axis_name="s")

    @functools.partial(
        pl.kernel, mesh=mesh, out_type=SDS((TOP_K * t, ROW_CHUNKS, LANES), U32),
        scratch_types=[pltpu.VMEM((TOP_K, SC_CHUNK), I32), pltpu.VMEM((SC_CHUNK, ROW_CHUNKS, LANES), U32),
                       pltpu.SemaphoreType.DMA],
        name="sc_gather")
    def body(ys_hbm, dest_hbm, out_hbm, idx_v, rows_v, sem):
        wid = lax.axis_index("s") * SC_CORES + lax.axis_index("c")

        @pl.loop(0, per_worker // SC_CHUNK)
        def _(i):
            t0 = wid * per_worker + i * SC_CHUNK
            pltpu.sync_copy(dest_hbm.at[:, pl.ds(t0, SC_CHUNK)], idx_v)
            for kk in range(TOP_K):
                pltpu.async_copy(ys_hbm.at[idx_v.at[kk]], rows_v, sem).wait()
                pltpu.sync_copy(rows_v, out_hbm.at[pl.ds(kk * t + t0, SC_CHUNK)])

    return body(ys3, dest)


def _expert_kernel(ts_ref, te_ref, tr_ref, nv_ref, wg_ref, wu_ref, wd_ref, xs_hbm, ys_hbm,
                   xbuf, ybuf, wg_b, wu_b, wd_b, xsem, ysem):
    e = pl.program_id(0)
    rows = xbuf.shape[1]
    tm = rows // ROW_CHUNKS
    g0, g1, nv = ts_ref[e], te_ref[e], nv_ref[0]

    def x_copy(g):
        s = g % X_SLOTS
        return pltpu.make_async_copy(xs_hbm.at[pl.ds(pl.multiple_of(g * rows, rows), rows), :], xbuf.at[s], xsem.at[s])

    def y_copy(g):
        s = g % Y_SLOTS
        return pltpu.make_async_copy(ybuf.at[s], ys_hbm.at[pl.ds(pl.multiple_of(g * rows, rows), rows), :], ysem.at[s])

    def compute_tile(g):
        x = _load_packed_bf16(xbuf, 0, tm, lead=g % X_SLOTS)
        gate = jnp.dot(x, wg_b[...], preferred_element_type=F32)
        up = jnp.dot(x, wu_b[...], preferred_element_type=F32)
        live = lax.broadcasted_iota(I32, (tm, EXPERT_FF), 0) < tr_ref[g]
        hid = jnp.where(live, gate * _sigmoid(gate) * up, 0.0).astype(BF16)
        y = jnp.dot(hid, wd_b[...], preferred_element_type=F32)
        return _pack_rows(y[:, :HALF], y[:, HALF:])

    def run_tiles(g, n):
        for r in range(n):
            x_copy(g + r).wait()

            @pl.when(g + r + X_AHEAD < nv)
            def _():
                x_copy(g + r + X_AHEAD).start(priority=1)

            @pl.when(g + r >= Y_SLOTS)
            def _():
                y_copy(g + r - Y_SLOTS).wait()

        packed = [compute_tile(g + r) for r in range(n)]
        for r in range(n):
            _store_packed(ybuf, 0, tm, packed[r], lead=(g + r) % Y_SLOTS)
        for r in range(n):
            y_copy(g + r).start(priority=1)

    @pl.when(e == 0)
    def _():
        for g in range(X_AHEAD):
            @pl.when(g < nv)
            def _():
                x_copy(g).start(priority=1)

    @pl.when(g1 > g0)
    def _():
        wg_b[...] = wg_ref[0].astype(BF16)
        wu_b[...] = wu_ref[0].astype(BF16)
        wd_b[...] = wd_ref[0].astype(BF16)
        n_tiles = g1 - g0

        def pair(p, c):
            run_tiles(g0 + 2 * p, 2)
            return c

        lax.fori_loop(0, n_tiles // 2, pair, 0)

        @pl.when(n_tiles % 2 == 1)
        def _():
            run_tiles(g1 - 1, 1)

    @pl.when(e == pl.num_programs(0) - 1)
    def _():
        for back in range(1, Y_SLOTS + 1):
            @pl.when(nv >= back)
            def _():
                y_copy(nv - back).wait()


def _expert_call(tile_start, tile_end, tile_rows, n_valid, xs, w_gate, w_up, w_down, n_rows):
    tm = TM_EXP
    w_map = lambda e, *_: (e, 0, 0)
    hbm = pl.BlockSpec(memory_space=pl.ANY)
    return pl.pallas_call(
        _expert_kernel,
        grid_spec=pltpu.PrefetchScalarGridSpec(
            num_scalar_prefetch=4,
            grid=(N_EXPERTS,),
            in_specs=[pl.BlockSpec((1, D_MODEL, EXPERT_FF), w_map), pl.BlockSpec((1, D_MODEL, EXPERT_FF), w_map),
                      pl.BlockSpec((1, EXPERT_FF, D_MODEL), w_map), hbm],
            out_specs=hbm,
            scratch_shapes=[pltpu.VMEM((X_SLOTS, tm * ROW_CHUNKS, LANES), U32),
                            pltpu.VMEM((Y_SLOTS, tm * ROW_CHUNKS, LANES), U32),
                            pltpu.VMEM((D_MODEL, EXPERT_FF), BF16), pltpu.VMEM((D_MODEL, EXPERT_FF), BF16),
                            pltpu.VMEM((EXPERT_FF, D_MODEL), BF16),
                            pltpu.SemaphoreType.DMA((X_SLOTS,)), pltpu.SemaphoreType.DMA((Y_SLOTS,))],
        ),
        out_shape=SDS((n_rows * ROW_CHUNKS, LANES), U32),
        compiler_params=_cparams("arbitrary"),
        name="experts",
    )(tile_start, tile_end, tile_rows, n_valid, w_gate, w_up, w_down, xs)


COMB_SUB = 32


def _combine_kernel(wts_ref, h1_ref, g_ref, wsg_ref, wsu_ref, wsd_ref, g2_ref, b2_ref, o_ref, routed_ref):
    tn = h1_ref.shape[0]
    for s0 in range(0, tn, COMB_SUB):
        acc = [jnp.zeros((COMB_SUB, LANES), F32) for _ in range(2 * ROW_CHUNKS)]
        for kk in range(TOP_K):
            wk = jnp.broadcast_to(wts_ref[s0:s0 + COMB_SUB, kk:kk + 1], (COMB_SUB, LANES))
            for cc in range(ROW_CHUNKS):
                lo, hi = _unpack_rows(g_ref[kk, pl.ds(s0 * ROW_CHUNKS + cc, COMB_SUB, stride=ROW_CHUNKS), :])
                acc[cc] = acc[cc] + wk * lo
                acc[ROW_CHUNKS + cc] = acc[ROW_CHUNKS + cc] + wk * hi
        routed_ref[s0:s0 + COMB_SUB, :] = jnp.concatenate(acc, axis=1)

    h1 = h1_ref[...]
    hb = h1.astype(BF16)
    sg = jnp.dot(hb, wsg_ref[...], preferred_element_type=F32)
    su = jnp.dot(hb, wsu_ref[...], preferred_element_type=F32)
    ff = jnp.dot((sg * _sigmoid(sg) * su).astype(BF16), wsd_ref[...], preferred_element_type=F32)
    o_ref[...] = _layer_norm(ALPHA * h1 + ff + routed_ref[...], g2_ref[...], b2_ref[...])


def _combine_call(wts_t, h1, gathered, wsg, wsu, wsd, g2, b2):
    t = h1.shape[0]
    tn = TN_COMB
    row = lambda i: (i, 0)
    fix = lambda i: (0, 0)
    return pl.pallas_call(
        _combine_kernel,
        grid=(t // tn,),
        in_specs=[pl.BlockSpec((tn, TOP_K), row),
                  pl.BlockSpec((tn, D_MODEL), row),
                  pl.BlockSpec((TOP_K, tn * ROW_CHUNKS, LANES), lambda i: (0, i, 0)),
                  pl.BlockSpec((D_MODEL, SHARED_FF), fix), pl.BlockSpec((D_MODEL, SHARED_FF), fix),
                  pl.BlockSpec((SHARED_FF, D_MODEL), fix),
                  pl.BlockSpec((1, D_MODEL), fix), pl.BlockSpec((1, D_MODEL), fix)],
        out_specs=pl.BlockSpec((tn, D_MODEL), row),
        out_shape=SDS((t, D_MODEL), F32),
        scratch_shapes=[pltpu.VMEM((tn, D_MODEL), F32)],
        compiler_params=_cparams("arbitrary"),
        name="combine_ln2",
    )(wts_t, h1, gathered, wsg, wsu, wsd, g2, b2)


def kernel(x, meta_tokens, ln_in_g, ln_in_b, rel_bias, w_in, conv_w, conv_b, conv_ln_g, conv_ln_b, sinks,
           w_out, ln1_g, ln1_b, w_router, router_bias, w_gate, w_up, w_down, ws_gate, ws_up, ws_down,
           ln2_g, ln2_b):
    nbatch, seq, d = x.shape
    assert d == D_MODEL and seq % TQ_PROJ == 0 and w_in.shape[0] == DEPTH
    t = nbatch * seq
    x2d = x.reshape(t, D_MODEL)
    vec = lambda a: a.reshape(1, -1).astype(F32)
    gin, bin_ = vec(ln_in_g), vec(ln_in_b)
    w_in_b = w_in[0].astype(BF16)

    q, k, v, u = _proj_call(x2d, gin, bin_, w_in_b, TQ_PROJ)
    meta_blk = jnp.concatenate([jnp.zeros((PAD_FRONT, D_MODEL), F32), meta_tokens.astype(F32)], axis=0)
    _, k_meta, v_meta, u_meta = _proj_call(meta_blk, gin, bin_, w_in_b, BLOCK)

    attn = _attn_call(q, k, v, k_meta, v_meta, _rel_bias_table(rel_bias), sinks[0].astype(F32),
                      nbatch, seq // BLOCK)

    u_halo = jnp.concatenate([jnp.zeros((CONV_HALO - N_META, CONV_CH), F32), u_meta[PAD_FRONT:]], axis=0)
    conv = _conv_call(u, u_halo, conv_w[0].astype(F32), vec(conv_b[0]), vec(conv_ln_g[0]), vec(conv_ln_b[0]),
                      nbatch, seq)

    w_out_b = w_out[0].astype(BF16)
    wr_t = w_router[0].astype(F32).T
    wr_hi = wr_t.astype(BF16)
    wr_lo = (wr_t - wr_hi.astype(F32)).astype(BF16)
    h1, h1rows, logits = _mix_call(x2d, attn, conv, gin, bin_, w_out_b[:ATTN_W], w_out_b[ATTN_W:],
                                   vec(ln1_g[0]), vec(ln1_b[0]), wr_hi, wr_lo)

    idx, wts, rank, cnt = _route_call(logits, router_bias[0].astype(F32).reshape(N_EXPERTS, 1))

    tm = TM_EXP
    n_tiles = (t * TOP_K) // tm + N_EXPERTS
    counts = cnt[:, 0]
    tiles_e = (counts + tm - 1) // tm
    tile_end = jnp.cumsum(tiles_e).astype(I32)
    tile_start = (tile_end - tiles_e).astype(I32)
    offs = tile_start * tm
    tile_id = jnp.arange(n_tiles, dtype=I32)
    lo = jnp.maximum(tile_id[:, None] * tm, offs[None, :])
    hi = jnp.minimum((tile_id[:, None] + 1) * tm, (offs + counts)[None, :])
    tile_rows = jnp.sum(jnp.clip(hi - lo, 0, tm), axis=1).astype(I32)
    n_valid = tile_end[-1:]

    dest = _dest_call(idx, rank, offs.astype(F32).reshape(N_EXPERTS, 1))
    xs = _sc_dispatch_call(dest, h1rows.reshape(t, ROW_CHUNKS, LANES), n_tiles * tm)
    xs = xs.reshape(n_tiles * tm * ROW_CHUNKS, LANES)
    ys = _expert_call(tile_start, tile_end, tile_rows, n_valid, xs, w_gate[0], w_up[0], w_down[0], n_tiles * tm)
    gathered = _sc_gather_call(dest, ys.reshape(n_tiles * tm, ROW_CHUNKS, LANES))
    gathered = gathered.reshape(TOP_K, t * ROW_CHUNKS, LANES)
    out = _combine_call(wts.T, h1, gathered, ws_gate[0].astype(BF16), ws_up[0].astype(BF16),
                        ws_down[0].astype(BF16), vec(ln2_g[0]), vec(ln2_b[0]))
    return out.reshape(nbatch, seq, D_MODEL)
```

```python
import functools
import math

import numpy as np
import jax
import jax.numpy as jnp
from jax import lax
from jax.experimental import pallas as pl
from jax.experimental.pallas import tpu as pltpu
from jax.experimental.pallas import tpu_sc as plsc

F32 = jnp.float32
BF16 = jnp.bfloat16
I32 = jnp.int32
U32 = jnp.uint32
SDS = jax.ShapeDtypeStruct

D_MODEL = 1024
HALF = D_MODEL // 2
LANES = 128
SUBLANES = 8
ROW_CHUNKS = HALF // LANES
N_META = 16
HEAD_DIM = 64
N_Q_HEADS = 8
N_KV_HEADS = 2
GQA_GROUP = N_Q_HEADS // N_KV_HEADS
ATTN_W = N_Q_HEADS * HEAD_DIM
KV_W = N_KV_HEADS * HEAD_DIM
WINDOW = 128
BLOCK = 128
CONV_CH = D_MODEL - ATTN_W
CONV_K = 31
IN_W = ATTN_W + 2 * KV_W + 2 * CONV_CH
NUM_BUCKETS = 32
MAX_EXACT = NUM_BUCKETS // 2
REL_MAX_DIST = 128
N_EXPERTS = 256
TOP_K = 8
N_GROUPS = 8
GROUP_SIZE = N_EXPERTS // N_GROUPS
TOPK_GROUPS = 4
EXPERT_FF = 256
SHARED_FF = 256
ROUTED_SCALE = 2.5
DEPTH = 1
ALPHA = (2.0 * DEPTH) ** 0.25
LN_EPS = 1e-5
NEG = -1e30
PAD_FRONT = (-N_META) % BLOCK

VMEM_LIMIT = 48 * 1024 * 1024

TQ_PROJ = 512
ATTN_QBLOCKS = 2
T_CONV = 256
CONV_HALO = 32
R_CONV = 64
TQ_MIX = 512
MIX_CHAINS = 2
TN_ROUTE = 256
TM_EXP = 256
X_SLOTS = 8
X_AHEAD = 4
Y_SLOTS = 4
TN_COMB = 256
TN_SHARED = 512
SHARED_CHAINS = 2
GATHER_CHUNKS = 4


def _cparams(*sem):
    return pltpu.CompilerParams(dimension_semantics=sem, vmem_limit_bytes=VMEM_LIMIT)


def _layer_norm(x, g, b):
    mu = jnp.mean(x, axis=-1, keepdims=True)
    xc = x - mu
    var = jnp.mean(xc * xc, axis=-1, keepdims=True)
    return xc * lax.rsqrt(var + LN_EPS) * g + b


def _sigmoid(x):
    return 1.0 / (1.0 + jnp.exp(-x))


def _pack_rows(lo_half, hi_half):
    lo = lax.bitcast_convert_type(lo_half.astype(BF16).astype(F32), U32)
    hi = lax.bitcast_convert_type(hi_half.astype(BF16).astype(F32), U32)
    return lax.shift_right_logical(lo, jnp.uint32(16)) | hi


def _unpack_rows(p):
    lo = lax.bitcast_convert_type(lax.shift_left(p, jnp.uint32(16)), F32)
    hi = lax.bitcast_convert_type(p & jnp.uint32(0xFFFF0000), F32)
    return lo, hi


def _chunk_index(start, j, n, lead):
    rows = pl.ds(start + j, n, stride=ROW_CHUNKS)
    return (rows, slice(None)) if lead is None else (lead, rows, slice(None))


def _store_packed(ref, start, n, packed, lead=None):
    for j in range(ROW_CHUNKS):
        ref[_chunk_index(start, j, n, lead)] = packed[:, j * LANES:(j + 1) * LANES]


def _load_packed_bf16(ref, start, n, lead=None):
    halves = [_unpack_rows(ref[_chunk_index(start, j, n, lead)]) for j in range(ROW_CHUNKS)]
    return jnp.concatenate([h[0] for h in halves] + [h[1] for h in halves], axis=1).astype(BF16)


def _proj_kernel(x_ref, g_ref, b_ref, w_ref, q_ref, k_ref, v_ref, u_ref):
    h = _layer_norm(x_ref[...], g_ref[...], b_ref[...])
    p = jnp.dot(h.astype(BF16), w_ref[...], preferred_element_type=F32)
    q_ref[...] = (p[:, :ATTN_W] * (HEAD_DIM ** -0.5)).astype(BF16)
    k_ref[...] = p[:, ATTN_W:ATTN_W + KV_W].astype(BF16)
    v_ref[...] = p[:, ATTN_W + KV_W:ATTN_W + 2 * KV_W].astype(BF16)
    a = p[:, ATTN_W + 2 * KV_W:ATTN_W + 2 * KV_W + CONV_CH]
    gate = p[:, ATTN_W + 2 * KV_W + CONV_CH:]
    u_ref[...] = a * _sigmoid(gate)


def _proj_call(x2d, gin, bin_, w_in_b, tq):
    t = x2d.shape[0]
    row = lambda i: (i, 0)
    fix = lambda i: (0, 0)
    return pl.pallas_call(
        _proj_kernel,
        grid=(t // tq,),
        in_specs=[pl.BlockSpec((tq, D_MODEL), row), pl.BlockSpec((1, D_MODEL), fix),
                  pl.BlockSpec((1, D_MODEL), fix), pl.BlockSpec((D_MODEL, IN_W), fix)],
        out_specs=[pl.BlockSpec((tq, ATTN_W), row), pl.BlockSpec((tq, KV_W), row),
                   pl.BlockSpec((tq, KV_W), row), pl.BlockSpec((tq, CONV_CH), row)],
        out_shape=[SDS((t, ATTN_W), BF16), SDS((t, KV_W), BF16), SDS((t, KV_W), BF16), SDS((t, CONV_CH), F32)],
        compiler_params=_cparams("arbitrary"),
        name="ln_in_proj",
    )(x2d, gin, bin_, w_in_b)


def _attn_kernel(sinks_ref, q_ref, kc_ref, kp_ref, vc_ref, vp_ref, km_ref, vm_ref, bias_ref, o_ref):
    first = pl.program_id(1) == 0
    kp = jnp.where(first, km_ref[...], kp_ref[...])
    vp = jnp.where(first, vm_ref[...], vp_ref[...])
    k = jnp.concatenate([kp, kc_ref[...]], axis=0)
    v = jnp.concatenate([vp, vc_ref[...]], axis=0)
    col = lax.broadcasted_iota(I32, (BLOCK, 2 * BLOCK), 1)
    pad_bias = jnp.where(jnp.logical_and(first, col < PAD_FRONT), NEG, 0.0).astype(F32)
    for a in range(ATTN_QBLOCKS):
        q = q_ref[a * BLOCK:(a + 1) * BLOCK, :]
        kw = k[a * BLOCK:(a + 2) * BLOCK, :]
        vw = v[a * BLOCK:(a + 2) * BLOCK, :]
        outs = []
        for h in range(N_Q_HEADS):
            g = h // GQA_GROUP
            qh = q[:, h * HEAD_DIM:(h + 1) * HEAD_DIM]
            kg = kw[:, g * HEAD_DIM:(g + 1) * HEAD_DIM]
            vg = vw[:, g * HEAD_DIM:(g + 1) * HEAD_DIM]
            s = lax.dot_general(qh, kg, (((1,), (1,)), ((), ())), preferred_element_type=F32)
            s = s + bias_ref[h]
            if a == 0:
                s = s + pad_bias
            sink = sinks_ref[h]
            m = jnp.maximum(jnp.max(s, axis=-1, keepdims=True), sink)
            p = jnp.exp(s - m)
            den = jnp.sum(p, axis=-1, keepdims=True) + jnp.exp(sink - m)
            o = jnp.dot(p.astype(BF16), vg, preferred_element_type=F32)
            outs.append(o / den)
        o_ref[a * BLOCK:(a + 1) * BLOCK, :] = jnp.concatenate(outs, axis=1).astype(BF16)


def _attn_call(q, k, v, k_meta, v_meta, bias, sinks, nbatch, nblk):
    t = q.shape[0]
    nq = ATTN_QBLOCKS
    assert nblk % nq == 0
    nstep = nblk // nq
    cur = lambda b, j: (b * nstep + j, 0)
    prev = lambda b, j: (jnp.maximum((b * nstep + j) * nq - 1, 0), 0)
    fix2 = lambda b, j: (0, 0)
    return pl.pallas_call(
        _attn_kernel,
        grid=(nbatch, nstep),
        in_specs=[pl.BlockSpec(memory_space=pltpu.SMEM),
                  pl.BlockSpec((nq * BLOCK, ATTN_W), cur),
                  pl.BlockSpec((nq * BLOCK, KV_W), cur), pl.BlockSpec((BLOCK, KV_W), prev),
                  pl.BlockSpec((nq * BLOCK, KV_W), cur), pl.BlockSpec((BLOCK, KV_W), prev),
                  pl.BlockSpec((BLOCK, KV_W), fix2), pl.BlockSpec((BLOCK, KV_W), fix2),
                  pl.BlockSpec((N_Q_HEADS, BLOCK, 2 * BLOCK), lambda b, j: (0, 0, 0))],
        out_specs=pl.BlockSpec((nq * BLOCK, ATTN_W), cur),
        out_shape=SDS((t, ATTN_W), BF16),
        compiler_params=_cparams("arbitrary", "arbitrary"),
        name="swa_attn",
    )(sinks, q, k, k, v, v, k_meta, v_meta, bias)


def _rel_bias_table(rel_bias):
    qi = np.arange(BLOCK, dtype=np.int32)[:, None]
    kj = np.arange(2 * BLOCK, dtype=np.int32)[None, :]
    dist = BLOCK + qi - kj
    dc = np.clip(dist, 0, WINDOW - 1)
    nf = np.maximum(dc, 1).astype(np.float32)
    large = MAX_EXACT + (np.log(nf / np.float32(MAX_EXACT)) / np.float32(math.log(REL_MAX_DIST / MAX_EXACT))
                         * np.float32(NUM_BUCKETS - MAX_EXACT)).astype(np.int32)
    large = np.minimum(large, NUM_BUCKETS - 1)
    bucket = np.where(dc < MAX_EXACT, dc, large)
    in_window = (dist >= 0) & (dist < WINDOW)
    onehot = (bucket.reshape(-1, 1) == np.arange(NUM_BUCKETS)[None, :]).astype(np.float32)
    bias = jnp.dot(jnp.asarray(onehot), rel_bias.astype(F32), precision=lax.Precision.HIGHEST)
    bias = jnp.transpose(bias.reshape(BLOCK, 2 * BLOCK, N_Q_HEADS), (2, 0, 1))
    return jnp.where(in_window[None], bias, NEG)


def _conv_kernel(uc_ref, up_ref, um_ref, w_ref, cb_ref, g_ref, b_ref, o_ref, s_ref, sh_ref):
    first = pl.program_id(1) == 0
    s_ref[0:CONV_HALO, :] = jnp.where(first, um_ref[...], up_ref[...])
    s_ref[CONV_HALO:CONV_HALO + T_CONV, :] = uc_ref[...]
    off = CONV_HALO - (CONV_K - 1)
    span = sh_ref.shape[1]
    for c in range(0, T_CONV, R_CONV):
        for p in range(1, SUBLANES):
            sh_ref[p] = s_ref[c + p:c + p + span, :]
        acc = jnp.zeros((R_CONV, CONV_CH), F32) + cb_ref[...]
        for kk in range(CONV_K):
            p, a = (off + kk) % SUBLANES, (off + kk) // SUBLANES * SUBLANES
            if p == 0:
                win = s_ref[c + a:c + a + R_CONV, :]
            else:
                win = sh_ref[p, a:a + R_CONV, :]
            acc = acc + win * w_ref[kk:kk + 1, :]
        y = _layer_norm(acc, g_ref[...], b_ref[...])
        o_ref[c:c + R_CONV, :] = (y * _sigmoid(y)).astype(BF16)


def _conv_call(u, u_meta_halo, conv_w, conv_b, g, b, nbatch, seq):
    t = u.shape[0]
    nj = seq // T_CONV
    per = T_CONV // CONV_HALO
    cur = lambda bb, j: (bb * nj + j, 0)
    prev = lambda bb, j: (jnp.maximum((bb * nj + j) * per - 1, 0), 0)
    fix = lambda bb, j: (0, 0)
    return pl.pallas_call(
        _conv_kernel,
        grid=(nbatch, nj),
        in_specs=[pl.BlockSpec((T_CONV, CONV_CH), cur), pl.BlockSpec((CONV_HALO, CONV_CH), prev),
                  pl.BlockSpec((CONV_HALO, CONV_CH), fix), pl.BlockSpec((CONV_K, CONV_CH), fix),
                  pl.BlockSpec((1, CONV_CH), fix), pl.BlockSpec((1, CONV_CH), fix), pl.BlockSpec((1, CONV_CH), fix)],
        out_specs=pl.BlockSpec((T_CONV, CONV_CH), cur),
        out_shape=SDS((t, CONV_CH), BF16),
        scratch_shapes=[pltpu.VMEM((CONV_HALO + T_CONV, CONV_CH), F32),
                        pltpu.VMEM((SUBLANES, R_CONV + CONV_HALO - SUBLANES, CONV_CH), F32)],
        compiler_params=_cparams("arbitrary", "arbitrary"),
        name="conv_ln",
    )(u, u, u_meta_halo, conv_w, conv_b, g, b)


def _mix_kernel(x_ref, at_ref, cv_ref, gin_ref, bin_ref, woa_ref, woc_ref, g1_ref, b1_ref,
                wrh_ref, wrl_ref, h1_ref, h1r_ref, lg_ref):
    rows = x_ref.shape[0] // MIX_CHAINS
    nt = (((1,), (1,)), ((), ()))
    for c in range(MIX_CHAINS):
        r = slice(c * rows, (c + 1) * rows)
        h = _layer_norm(x_ref[r, :], gin_ref[...], bin_ref[...])
        mix = (jnp.dot(at_ref[r, :], woa_ref[...], preferred_element_type=F32)
               + jnp.dot(cv_ref[r, :], woc_ref[...], preferred_element_type=F32))
        h1 = _layer_norm(ALPHA * h + mix, g1_ref[...], b1_ref[...])
        h1_ref[r, :] = h1
        _store_packed(h1r_ref, c * rows * ROW_CHUNKS, rows, _pack_rows(h1[:, :HALF], h1[:, HALF:]))
        hh = h1.astype(BF16)
        hl = (h1 - hh.astype(F32)).astype(BF16)
        lg = lax.dot_general(wrh_ref[...], hh, nt, preferred_element_type=F32)
        lg = lg + lax.dot_general(wrh_ref[...], hl, nt, preferred_element_type=F32)
        lg = lg + lax.dot_general(wrl_ref[...], hh, nt, preferred_element_type=F32)
        lg_ref[:, r] = lg


def _mix_call(x2d, attn, conv, gin, bin_, woa, woc, g1, b1, wrh, wrl):
    t = x2d.shape[0]
    tq = TQ_MIX
    row = lambda i: (i, 0)
    fix = lambda i: (0, 0)
    return pl.pallas_call(
        _mix_kernel,
        grid=(t // tq,),
        in_specs=[pl.BlockSpec((tq, D_MODEL), row), pl.BlockSpec((tq, ATTN_W), row), pl.BlockSpec((tq, CONV_CH), row),
                  pl.BlockSpec((1, D_MODEL), fix), pl.BlockSpec((1, D_MODEL), fix),
                  pl.BlockSpec((ATTN_W, D_MODEL), fix), pl.BlockSpec((CONV_CH, D_MODEL), fix),
                  pl.BlockSpec((1, D_MODEL), fix), pl.BlockSpec((1, D_MODEL), fix),
                  pl.BlockSpec((N_EXPERTS, D_MODEL), fix), pl.BlockSpec((N_EXPERTS, D_MODEL), fix)],
        out_specs=[pl.BlockSpec((tq, D_MODEL), row), pl.BlockSpec((tq * ROW_CHUNKS, LANES), row),
                   pl.BlockSpec((N_EXPERTS, tq), lambda i: (0, i))],
        out_shape=[SDS((t, D_MODEL), F32), SDS((t * ROW_CHUNKS, LANES), U32), SDS((N_EXPERTS, t), F32)],
        compiler_params=_cparams("arbitrary"),
        name="mix_ln1",
    )(x2d, attn, conv, gin, bin_, woa, woc, g1, b1, wrh, wrl)


def _first_argmax(x, rows, nrows):
    m = jnp.max(x, axis=0, keepdims=True)
    idx = jnp.min(jnp.where(x == m, rows, nrows), axis=0, keepdims=True)
    return m, idx


def _route_kernel(lg_ref, rb_ref, idx_ref, wts_ref, rank_ref, cnt_ref, carry_ref):
    tn = lg_ref.shape[1]

    @pl.when(pl.program_id(0) == 0)
    def _():
        carry_ref[...] = jnp.zeros_like(carry_ref)

    scores = _sigmoid(lg_ref[...])
    choice = scores + rb_ref[...]
    rows = lax.broadcasted_iota(I32, (N_EXPERTS, tn), 0)
    rows_g = lax.broadcasted_iota(I32, (GROUP_SIZE, tn), 0)
    rows_8 = lax.broadcasted_iota(I32, (N_GROUPS, tn), 0)

    gs = []
    for g in range(N_GROUPS):
        xg = choice[g * GROUP_SIZE:(g + 1) * GROUP_SIZE, :]
        m1, i1 = _first_argmax(xg, rows_g, GROUP_SIZE)
        m2 = jnp.max(jnp.where(rows_g == i1, -jnp.inf, xg), axis=0, keepdims=True)
        gs.append(m1 + m2)
    gsc = jnp.concatenate(gs, axis=0)
    gsel = jnp.zeros((N_GROUPS, tn), F32)
    for _ in range(TOPK_GROUPS):
        _, gi = _first_argmax(gsc, rows_8, N_GROUPS)
        hit = rows_8 == gi
        gsel = jnp.where(hit, 1.0, gsel)
        gsc = jnp.where(hit, -jnp.inf, gsc)
    emask = jnp.concatenate(
        [jnp.broadcast_to(gsel[g:g + 1, :], (GROUP_SIZE, tn)) for g in range(N_GROUPS)], axis=0)
    masked = jnp.where(emask > 0.5, choice, NEG)

    sel_all = jnp.zeros((N_EXPERTS, tn), F32)
    hits, idxs, ws = [], [], []
    for _ in range(TOP_K):
        _, ii = _first_argmax(masked, rows, N_EXPERTS)
        hit = rows == ii
        hits.append(hit)
        idxs.append(ii)
        ws.append(jnp.sum(jnp.where(hit, scores, 0.0), axis=0, keepdims=True))
        sel_all = jnp.where(hit, 1.0, sel_all)
        masked = jnp.where(hit, -jnp.inf, masked)
    wsum = ws[0]
    for w in ws[1:]:
        wsum = wsum + w
    idx_ref[...] = jnp.concatenate(idxs, axis=0)
    wts_ref[...] = jnp.concatenate([w / wsum * ROUTED_SCALE for w in ws], axis=0)

    r_i = lax.broadcasted_iota(I32, (tn, tn), 0)
    c_i = lax.broadcasted_iota(I32, (tn, tn), 1)
    upper = jnp.where(r_i < c_i, 1.0, 0.0).astype(BF16)
    sel_b = sel_all.astype(BF16)
    carry = carry_ref[...]
    before = jnp.dot(sel_b, upper, preferred_element_type=F32)
    before = before + jnp.concatenate([carry] * (tn // 128), axis=1)
    rank_ref[...] = jnp.concatenate(
        [jnp.sum(jnp.where(h, before, 0.0), axis=0, keepdims=True) for h in hits], axis=0).astype(I32)
    carry = carry + jnp.dot(sel_b, jnp.ones((tn, 128), BF16), preferred_element_type=F32)
    carry_ref[...] = carry
    cnt_ref[...] = carry.astype(I32)


def _route_call(lg, rbias):
    t = lg.shape[1]
    tn = TN_ROUTE
    col = lambda i: (0, i)
    return pl.pallas_call(
        _route_kernel,
        grid=(t // tn,),
        in_specs=[pl.BlockSpec((N_EXPERTS, tn), col), pl.BlockSpec((N_EXPERTS, 1), lambda i: (0, 0))],
        out_specs=[pl.BlockSpec((TOP_K, tn), col), pl.BlockSpec((TOP_K, tn), col), pl.BlockSpec((TOP_K, tn), col),
                   pl.BlockSpec((N_EXPERTS, 128), lambda i: (0, 0))],
        out_shape=[SDS((TOP_K, t), I32), SDS((TOP_K, t), F32), SDS((TOP_K, t), I32), SDS((N_EXPERTS, 128), I32)],
        scratch_shapes=[pltpu.VMEM((N_EXPERTS, 128), F32)],
        compiler_params=_cparams("arbitrary"),
        name="route",
    )(lg, rbias)


def _dest_kernel(idx_ref, rank_ref, offs_ref, dest_ref):
    tn = idx_ref.shape[1]
    rows = lax.broadcasted_iota(I32, (N_EXPERTS, tn), 0)
    offs = offs_ref[...]
    out = []
    for kk in range(TOP_K):
        hit = rows == idx_ref[kk:kk + 1, :]
        out.append(jnp.sum(jnp.where(hit, offs, 0.0), axis=0, keepdims=True))
    dest_ref[...] = jnp.concatenate(out, axis=0).astype(I32) + rank_ref[...]


def _dest_call(idx, rank, offs_col):
    t = idx.shape[1]
    tn = TN_ROUTE
    col = lambda i: (0, i)
    return pl.pallas_call(
        _dest_kernel,
        grid=(t // tn,),
        in_specs=[pl.BlockSpec((TOP_K, tn), col), pl.BlockSpec((TOP_K, tn), col),
                  pl.BlockSpec((N_EXPERTS, 1), lambda i: (0, 0))],
        out_specs=pl.BlockSpec((TOP_K, tn), col),
        out_shape=SDS((TOP_K, t), I32),
        compiler_params=_cparams("arbitrary"),
        name="dest",
    )(idx, rank, offs_col)


SC_CORES = 2
SC_SUBCORES = 16
SC_CHUNK = 128


def _sc_worker_chunks(t):
    per_worker = t // (SC_CORES * SC_SUBCORES)
    assert per_worker % SC_CHUNK == 0
    return per_worker


def _sc_dispatch_call(dest, h1rows3, n_rows):
    per_worker = _sc_worker_chunks(dest.shape[1])
    mesh = plsc.VectorSubcoreMesh(core_axis_name="c", subcore_axis_name="s")

    @functools.partial(
        pl.kernel, mesh=mesh, out_type=SDS((n_rows, ROW_CHUNKS, LANES), U32),
        scratch_types=[pltpu.VMEM((TOP_K, SC_CHUNK), I32), pltpu.VMEM((SC_CHUNK, ROW_CHUNKS, LANES), U32),
                       pltpu.SemaphoreType.DMA],
        name="sc_dispatch")
    def body(h_hbm, dest_hbm, xs_hbm, idx_v, rows_v, sem):
        wid = lax.axis_index("s") * SC_CORES + lax.axis_index("c")

        @pl.loop(0, per_worker // SC_CHUNK)
        def _(i):
            t0 = wid * per_worker + i * SC_CHUNK
            pltpu.sync_copy(dest_hbm.at[:, pl.ds(t0, SC_CHUNK)], idx_v)
            pltpu.sync_copy(h_hbm.at[pl.ds(t0, SC_CHUNK)], rows_v)
            copies = [pltpu.async_copy(rows_v, xs_hbm.at[idx_v.at[kk]], sem) for kk in range(TOP_K)]
            for c in copies:
                c.wait()

    return body(h1rows3, dest)


def _sc_gather_call(dest, ys3):
    t = dest.shape[1]
    per_worker = _sc_worker_chunks(t)
    mesh = plsc.VectorSubcoreMesh(core_axis_name="c", subcore_axis_name="s")

    @functools.partial(
        pl.kernel, mesh=mesh, out_type=SDS((TOP_K * t, ROW_CHUNKS, LANES), U32),
        scratch_types=[pltpu.VMEM((TOP_K, SC_CHUNK), I32), pltpu.VMEM((SC_CHUNK, ROW_CHUNKS, LANES), U32),
                       pltpu.SemaphoreType.DMA],
        name="sc_gather")
    def body(ys_hbm, dest_hbm, out_hbm, idx_v, rows_v, sem):
        wid = lax.axis_index("s") * SC_CORES + lax.axis_index("c")

        @pl.loop(0, per_worker // SC_CHUNK)
        def _(i):
            t0 = wid * per_worker + i * SC_CHUNK
            pltpu.sync_copy(dest_hbm.at[:, pl.ds(t0, SC_CHUNK)], idx_v)
            for kk in range(TOP_K):
                pltpu.async_copy(ys_hbm.at[idx_v.at[kk]], rows_v, sem).wait()
                pltpu.sync_copy(rows_v, out_hbm.at[pl.ds(kk * t + t0, SC_CHUNK)])

    return body(ys3, dest)


def _expert_kernel(ts_ref, te_ref, tr_ref, nv_ref, wg_ref, wu_ref, wd_ref, xs_hbm, ys_hbm,
                   xbuf, ybuf, wg_b, wu_b, wd_b, xsem, ysem):
    e = pl.program_id(0)
    rows = xbuf.shape[1]
    tm = rows // ROW_CHUNKS
    g0, g1, nv = ts_ref[e], te_ref[e], nv_ref[0]

    def x_copy(g):
        s = g % X_SLOTS
        return pltpu.make_async_copy(xs_hbm.at[pl.ds(pl.multiple_of(g * rows, rows), rows), :], xbuf.at[s], xsem.at[s])

    def y_copy(g):
        s = g % Y_SLOTS
        return pltpu.make_async_copy(ybuf.at[s], ys_hbm.at[pl.ds(pl.multiple_of(g * rows, rows), rows), :], ysem.at[s])

    def compute_tile(g):
        x = _load_packed_bf16(xbuf, 0, tm, lead=g % X_SLOTS)
        gate = jnp.dot(x, wg_b[...], preferred_element_type=F32)
        up = jnp.dot(x, wu_b[...], preferred_element_type=F32)
        live = lax.broadcasted_iota(I32, (tm, EXPERT_FF), 0) < tr_ref[g]
        hid = jnp.where(live, gate * _sigmoid(gate) * up, 0.0).astype(BF16)
        y = jnp.dot(hid, wd_b[...], preferred_element_type=F32)
        return _pack_rows(y[:, :HALF], y[:, HALF:])

    def run_tiles(g, n):
        for r in range(n):
            x_copy(g + r).wait()

            @pl.when(g + r + X_AHEAD < nv)
            def _():
                x_copy(g + r + X_AHEAD).start(priority=1)

            @pl.when(g + r >= Y_SLOTS)
            def _():
                y_copy(g + r - Y_SLOTS).wait()

        packed = [compute_tile(g + r) for r in range(n)]
        for r in range(n):
            _store_packed(ybuf, 0, tm, packed[r], lead=(g + r) % Y_SLOTS)
        for r in range(n):
            y_copy(g + r).start(priority=1)

    @pl.when(e == 0)
    def _():
        for g in range(X_AHEAD):
            @pl.when(g < nv)
            def _():
                x_copy(g).start(priority=1)

    @pl.when(g1 > g0)
    def _():
        wg_b[...] = wg_ref[0].astype(BF16)
        wu_b[...] = wu_ref[0].astype(BF16)
        wd_b[...] = wd_ref[0].astype(BF16)
        n_tiles = g1 - g0

        def pair(p, c):
            run_tiles(g0 + 2 * p, 2)
            return c

        lax.fori_loop(0, n_tiles // 2, pair, 0)

        @pl.when(n_tiles % 2 == 1)
        def _():
            run_tiles(g1 - 1, 1)

    @pl.when(e == pl.num_programs(0) - 1)
    def _():
        for back in range(1, Y_SLOTS + 1):
            @pl.when(nv >= back)
            def _():
                y_copy(nv - back).wait()


def _expert_call(tile_start, tile_end, tile_rows, n_valid, xs, w_gate, w_up, w_down, n_rows):
    tm = TM_EXP
    w_map = lambda e, *_: (e, 0, 0)
    hbm = pl.BlockSpec(memory_space=pl.ANY)
    return pl.pallas_call(
        _expert_kernel,
        grid_spec=pltpu.PrefetchScalarGridSpec(
            num_scalar_prefetch=4,
            grid=(N_EXPERTS,),
            in_specs=[pl.BlockSpec((1, D_MODEL, EXPERT_FF), w_map), pl.BlockSpec((1, D_MODEL, EXPERT_FF), w_map),
                      pl.BlockSpec((1, EXPERT_FF, D_MODEL), w_map), hbm],
            out_specs=hbm,
            scratch_shapes=[pltpu.VMEM((X_SLOTS, tm * ROW_CHUNKS, LANES), U32),
                            pltpu.VMEM((Y_SLOTS, tm * ROW_CHUNKS, LANES), U32),
                            pltpu.VMEM((D_MODEL, EXPERT_FF), BF16), pltpu.VMEM((D_MODEL, EXPERT_FF), BF16),
                            pltpu.VMEM((EXPERT_FF, D_MODEL), BF16),
                            pltpu.SemaphoreType.DMA((X_SLOTS,)), pltpu.SemaphoreType.DMA((Y_SLOTS,))],
        ),
        out_shape=SDS((n_rows * ROW_CHUNKS, LANES), U32),
        compiler_params=_cparams("arbitrary"),
        name="experts",
    )(tile_start, tile_end, tile_rows, n_valid, w_gate, w_up, w_down, xs)


COMB_SUB = 32


def _shared_kernel(h1_ref, wsg_ref, wsu_ref, wsd_ref, o_ref):
    rows = h1_ref.shape[0] // SHARED_CHAINS
    for c in range(SHARED_CHAINS):
        r = slice(c * rows, (c + 1) * rows)
        h1 = h1_ref[r, :]
        hb = h1.astype(BF16)
        sg = jnp.dot(hb, wsg_ref[...], preferred_element_type=F32)
        su = jnp.dot(hb, wsu_ref[...], preferred_element_type=F32)
        ff = jnp.dot((sg * _sigmoid(sg) * su).astype(BF16), wsd_ref[...], preferred_element_type=F32)
        o_ref[r, :] = ALPHA * h1 + ff


def _shared_call(h1, wsg, wsu, wsd):
    t = h1.shape[0]
    tn = TN_SHARED
    row = lambda i: (i, 0)
    fix = lambda i: (0, 0)
    return pl.pallas_call(
        _shared_kernel,
        grid=(t // tn,),
        in_specs=[pl.BlockSpec((tn, D_MODEL), row),
                  pl.BlockSpec((D_MODEL, SHARED_FF), fix), pl.BlockSpec((D_MODEL, SHARED_FF), fix),
                  pl.BlockSpec((SHARED_FF, D_MODEL), fix)],
        out_specs=pl.BlockSpec((tn, D_MODEL), row),
        out_shape=SDS((t, D_MODEL), F32),
        compiler_params=_cparams("arbitrary"),
        name="shared_resid",
    )(h1, wsg, wsu, wsd)


def _combine_kernel(wts_ref, resid_ref, g_ref, g2_ref, b2_ref, *rest):
    o_ref, routed_ref = rest[-2:]
    tn = resid_ref.shape[0]
    for s0 in range(0, tn, COMB_SUB):
        acc = [jnp.zeros((COMB_SUB, LANES), F32) for _ in range(2 * ROW_CHUNKS)]
        for kk in range(TOP_K):
            wk = jnp.broadcast_to(wts_ref[s0:s0 + COMB_SUB, kk:kk + 1], (COMB_SUB, LANES))
            for cc in range(ROW_CHUNKS):
                lo, hi = _unpack_rows(g_ref[kk, pl.ds(s0 * ROW_CHUNKS + cc, COMB_SUB, stride=ROW_CHUNKS), :])
                acc[cc] = acc[cc] + wk * lo
                acc[ROW_CHUNKS + cc] = acc[ROW_CHUNKS + cc] + wk * hi
        routed_ref[s0:s0 + COMB_SUB, :] = jnp.concatenate(acc, axis=1)
    o_ref[...] = _layer_norm(resid_ref[...] + routed_ref[...], g2_ref[...], b2_ref[...])


def _combine_call(wts_t, resid, gathered, g2, b2, out_prev, first_tile):
    t = resid.shape[0]
    tn = TN_COMB
    n_steps = gathered.shape[1] // (tn * ROW_CHUNKS)
    row = lambda i: (first_tile + i, 0)
    fix = lambda i: (0, 0)
    in_specs = [pl.BlockSpec((tn, TOP_K), row),
                pl.BlockSpec((tn, D_MODEL), row),
                pl.BlockSpec((TOP_K, tn * ROW_CHUNKS, LANES), lambda i: (0, i, 0)),
                pl.BlockSpec((1, D_MODEL), fix), pl.BlockSpec((1, D_MODEL), fix)]
    args = [wts_t, resid, gathered, g2, b2]
    aliases = {}
    if out_prev is not None:
        in_specs.append(pl.BlockSpec(memory_space=pl.ANY))
        args.append(out_prev)
        aliases = {len(args) - 1: 0}
    return pl.pallas_call(
        _combine_kernel,
        grid=(n_steps,),
        in_specs=in_specs,
        out_specs=pl.BlockSpec((tn, D_MODEL), row),
        out_shape=SDS((t, D_MODEL), F32),
        scratch_shapes=[pltpu.VMEM((tn, D_MODEL), F32)],
        input_output_aliases=aliases,
        compiler_params=_cparams("arbitrary"),
        name="combine_ln2",
    )(*args)


def kernel(x, meta_tokens, ln_in_g, ln_in_b, rel_bias, w_in, conv_w, conv_b, conv_ln_g, conv_ln_b, sinks,
           w_out, ln1_g, ln1_b, w_router, router_bias, w_gate, w_up, w_down, ws_gate, ws_up, ws_down,
           ln2_g, ln2_b):
    nbatch, seq, d = x.shape
    assert d == D_MODEL and seq % TQ_PROJ == 0 and w_in.shape[0] == DEPTH
    t = nbatch * seq
    x2d = x.reshape(t, D_MODEL)
    vec = lambda a: a.reshape(1, -1).astype(F32)
    gin, bin_ = vec(ln_in_g), vec(ln_in_b)
    w_in_b = w_in[0].astype(BF16)

    q, k, v, u = _proj_call(x2d, gin, bin_, w_in_b, TQ_PROJ)
    meta_blk = jnp.concatenate([jnp.zeros((PAD_FRONT, D_MODEL), F32), meta_tokens.astype(F32)], axis=0)
    _, k_meta, v_meta, u_meta = _proj_call(meta_blk, gin, bin_, w_in_b, BLOCK)

    attn = _attn_call(q, k, v, k_meta, v_meta, _rel_bias_table(rel_bias), sinks[0].astype(F32),
                      nbatch, seq // BLOCK)

    u_halo = jnp.concatenate([jnp.zeros((CONV_HALO - N_META, CONV_CH), F32), u_meta[PAD_FRONT:]], axis=0)
    conv = _conv_call(u, u_halo, conv_w[0].astype(F32), vec(conv_b[0]), vec(conv_ln_g[0]), vec(conv_ln_b[0]),
                      nbatch, seq)

    w_out_b = w_out[0].astype(BF16)
    wr_t = w_router[0].astype(F32).T
    wr_hi = wr_t.astype(BF16)
    wr_lo = (wr_t - wr_hi.astype(F32)).astype(BF16)
    h1, h1rows, logits = _mix_call(x2d, attn, conv, gin, bin_, w_out_b[:ATTN_W], w_out_b[ATTN_W:],
                                   vec(ln1_g[0]), vec(ln1_b[0]), wr_hi, wr_lo)

    idx, wts, rank, cnt = _route_call(logits, router_bias[0].astype(F32).reshape(N_EXPERTS, 1))

    tm = TM_EXP
    n_tiles = (t * TOP_K) // tm + N_EXPERTS
    counts = cnt[:, 0]
    tiles_e = (counts + tm - 1) // tm
    tile_end = jnp.cumsum(tiles_e).astype(I32)
    tile_start = (tile_end - tiles_e).astype(I32)
    offs = tile_start * tm
    tile_id = jnp.arange(n_tiles, dtype=I32)
    lo = jnp.maximum(tile_id[:, None] * tm, offs[None, :])
    hi = jnp.minimum((tile_id[:, None] + 1) * tm, (offs + counts)[None, :])
    tile_rows = jnp.sum(jnp.clip(hi - lo, 0, tm), axis=1).astype(I32)
    n_valid = tile_end[-1:]

    dest = _dest_call(idx, rank, offs.astype(F32).reshape(N_EXPERTS, 1))
    xs = _sc_dispatch_call(dest, h1rows.reshape(t, ROW_CHUNKS, LANES), n_tiles * tm)
    xs = xs.reshape(n_tiles * tm * ROW_CHUNKS, LANES)
    resid = _shared_call(h1, ws_gate[0].astype(BF16), ws_up[0].astype(BF16), ws_down[0].astype(BF16))
    ys = _expert_call(tile_start, tile_end, tile_rows, n_valid, xs, w_gate[0], w_up[0], w_down[0], n_tiles * tm)
    ys3 = ys.reshape(n_tiles * tm, ROW_CHUNKS, LANES)

    wts_t = wts.T
    g2, b2 = vec(ln2_g[0]), vec(ln2_b[0])
    tc = t // GATHER_CHUNKS
    assert tc % TN_COMB == 0
    out = None
    for c in range(GATHER_CHUNKS):
        gathered = _sc_gather_call(dest[:, c * tc:(c + 1) * tc], ys3).reshape(TOP_K, tc * ROW_CHUNKS, LANES)
        out = _combine_call(wts_t, resid, gathered, g2, b2, out, c * tc // TN_COMB)
    return out.reshape(nbatch, seq, D_MODEL)
```

```python
import functools
import math

import numpy as np
import jax
import jax.numpy as jnp
from jax import lax
from jax.experimental import pallas as pl
from jax.experimental.pallas import tpu as pltpu
from jax.experimental.pallas import tpu_sc as plsc

F32 = jnp.float32
BF16 = jnp.bfloat16
I32 = jnp.int32
U32 = jnp.uint32
SDS = jax.ShapeDtypeStruct

D_MODEL = 1024
HALF = D_MODEL // 2
LANES = 128
SUBLANES = 8
ROW_CHUNKS = HALF // LANES
N_META = 16
HEAD_DIM = 64
N_Q_HEADS = 8
N_KV_HEADS = 2
GQA_GROUP = N_Q_HEADS // N_KV_HEADS
ATTN_W = N_Q_HEADS * HEAD_DIM
KV_W = N_KV_HEADS * HEAD_DIM
WINDOW = 128
BLOCK = 128
CONV_CH = D_MODEL - ATTN_W
CONV_K = 31
IN_W = ATTN_W + 2 * KV_W + 2 * CONV_CH
NUM_BUCKETS = 32
MAX_EXACT = NUM_BUCKETS // 2
REL_MAX_DIST = 128
N_EXPERTS = 256
TOP_K = 8
N_GROUPS = 8
GROUP_SIZE = N_EXPERTS // N_GROUPS
TOPK_GROUPS = 4
EXPERT_FF = 256
SHARED_FF = 256
ROUTED_SCALE = 2.5
DEPTH = 1
ALPHA = (2.0 * DEPTH) ** 0.25
LN_EPS = 1e-5
NEG = -1e30
PAD_FRONT = (-N_META) % BLOCK

VMEM_LIMIT = 48 * 1024 * 1024

TQ_PROJ = 512
PROJ_CHAINS = 2
ATTN_QBLOCKS = 2
T_CONV = 256
CONV_HALO = 32
R_CONV = 64
TQ_MIX = 512
MIX_CHAINS = 2
TN_ROUTE = 256
TM_EXP = 256
X_SLOTS = 8
X_AHEAD = 4
Y_SLOTS = 4
TN_COMB = 256
TN_SHARED = 512
SHARED_CHAINS = 2
GATHER_CHUNKS = 1


def _cparams(*sem):
    return pltpu.CompilerParams(dimension_semantics=sem, vmem_limit_bytes=VMEM_LIMIT)


def _layer_norm(x, g, b):
    mu = jnp.mean(x, axis=-1, keepdims=True)
    xc = x - mu
    var = jnp.mean(xc * xc, axis=-1, keepdims=True)
    return xc * lax.rsqrt(var + LN_EPS) * g + b


def _sigmoid(x):
    return 1.0 / (1.0 + jnp.exp(-x))


def _pack_rows(lo_half, hi_half):
    lo = lax.bitcast_convert_type(lo_half.astype(BF16).astype(F32), U32)
    hi = lax.bitcast_convert_type(hi_half.astype(BF16).astype(F32), U32)
    return lax.shift_right_logical(lo, jnp.uint32(16)) | hi


def _unpack_rows(p):
    lo = lax.bitcast_convert_type(lax.shift_left(p, jnp.uint32(16)), F32)
    hi = lax.bitcast_convert_type(p & jnp.uint32(0xFFFF0000), F32)
    return lo, hi


def _chunk_index(start, j, n, lead):
    rows = pl.ds(start + j, n, stride=ROW_CHUNKS)
    return (rows, slice(None)) if lead is None else (lead, rows, slice(None))


def _store_packed(ref, start, n, packed, lead=None):
    for j in range(ROW_CHUNKS):
        ref[_chunk_index(start, j, n, lead)] = packed[:, j * LANES:(j + 1) * LANES]


def _load_packed_bf16(ref, start, n, lead=None):
    halves = [_unpack_rows(ref[_chunk_index(start, j, n, lead)]) for j in range(ROW_CHUNKS)]
    return jnp.concatenate([h[0] for h in halves] + [h[1] for h in halves], axis=1).astype(BF16)


def _proj_kernel(chains, x_ref, g_ref, b_ref, w_ref, q_ref, k_ref, v_ref, u_ref):
    rows = x_ref.shape[0] // chains
    for c in range(chains):
        r = slice(c * rows, (c + 1) * rows)
        h = _layer_norm(x_ref[r, :], g_ref[...], b_ref[...])
        p = jnp.dot(h.astype(BF16), w_ref[...], preferred_element_type=F32)
        q_ref[r, :] = (p[:, :ATTN_W] * (HEAD_DIM ** -0.5)).astype(BF16)
        k_ref[r, :] = p[:, ATTN_W:ATTN_W + KV_W].astype(BF16)
        v_ref[r, :] = p[:, ATTN_W + KV_W:ATTN_W + 2 * KV_W].astype(BF16)
        a = p[:, ATTN_W + 2 * KV_W:ATTN_W + 2 * KV_W + CONV_CH]
        gate = p[:, ATTN_W + 2 * KV_W + CONV_CH:]
        u_ref[r, :] = a * _sigmoid(gate)


def _proj_call(x2d, gin, bin_, w_in_b, tq):
    t = x2d.shape[0]
    row = lambda i: (i, 0)
    fix = lambda i: (0, 0)
    chains = PROJ_CHAINS if tq % (PROJ_CHAINS * BLOCK) == 0 else 1
    return pl.pallas_call(
        functools.partial(_proj_kernel, chains),
        grid=(t // tq,),
        in_specs=[pl.BlockSpec((tq, D_MODEL), row), pl.BlockSpec((1, D_MODEL), fix),
                  pl.BlockSpec((1, D_MODEL), fix), pl.BlockSpec((D_MODEL, IN_W), fix)],
        out_specs=[pl.BlockSpec((tq, ATTN_W), row), pl.BlockSpec((tq, KV_W), row),
                   pl.BlockSpec((tq, KV_W), row), pl.BlockSpec((tq, CONV_CH), row)],
        out_shape=[SDS((t, ATTN_W), BF16), SDS((t, KV_W), BF16), SDS((t, KV_W), BF16), SDS((t, CONV_CH), F32)],
        compiler_params=_cparams("arbitrary"),
        name="ln_in_proj",
    )(x2d, gin, bin_, w_in_b)


def _attn_kernel(sinks_ref, q_ref, kc_ref, kp_ref, vc_ref, vp_ref, km_ref, vm_ref, bias_ref, o_ref):
    first = pl.program_id(1) == 0
    kp = jnp.where(first, km_ref[...], kp_ref[...])
    vp = jnp.where(first, vm_ref[...], vp_ref[...])
    k = jnp.concatenate([kp, kc_ref[...]], axis=0)
    v = jnp.concatenate([vp, vc_ref[...]], axis=0)
    col = lax.broadcasted_iota(I32, (BLOCK, 2 * BLOCK), 1)
    pad_bias = jnp.where(jnp.logical_and(first, col < PAD_FRONT), NEG, 0.0).astype(F32)
    for a in range(ATTN_QBLOCKS):
        q = q_ref[a * BLOCK:(a + 1) * BLOCK, :]
        kw = k[a * BLOCK:(a + 2) * BLOCK, :]
        vw = v[a * BLOCK:(a + 2) * BLOCK, :]
        outs = []
        for h in range(N_Q_HEADS):
            g = h // GQA_GROUP
            qh = q[:, h * HEAD_DIM:(h + 1) * HEAD_DIM]
            kg = kw[:, g * HEAD_DIM:(g + 1) * HEAD_DIM]
            vg = vw[:, g * HEAD_DIM:(g + 1) * HEAD_DIM]
            s = lax.dot_general(qh, kg, (((1,), (1,)), ((), ())), preferred_element_type=F32)
            s = s + bias_ref[h]
            if a == 0:
                s = s + pad_bias
            sink = sinks_ref[h]
            m = jnp.maximum(jnp.max(s, axis=-1, keepdims=True), sink)
            p = jnp.exp(s - m)
            den = jnp.sum(p, axis=-1, keepdims=True) + jnp.exp(sink - m)
            o = jnp.dot(p.astype(BF16), vg, preferred_element_type=F32)
            outs.append(o / den)
        o_ref[a * BLOCK:(a + 1) * BLOCK, :] = jnp.concatenate(outs, axis=1).astype(BF16)


def _attn_call(q, k, v, k_meta, v_meta, bias, sinks, nbatch, nblk):
    t = q.shape[0]
    nq = ATTN_QBLOCKS
    assert nblk % nq == 0
    nstep = nblk // nq
    cur = lambda b, j: (b * nstep + j, 0)
    prev = lambda b, j: (jnp.maximum((b * nstep + j) * nq - 1, 0), 0)
    fix2 = lambda b, j: (0, 0)
    return pl.pallas_call(
        _attn_kernel,
        grid=(nbatch, nstep),
        in_specs=[pl.BlockSpec(memory_space=pltpu.SMEM),
                  pl.BlockSpec((nq * BLOCK, ATTN_W), cur),
                  pl.BlockSpec((nq * BLOCK, KV_W), cur), pl.BlockSpec((BLOCK, KV_W), prev),
                  pl.BlockSpec((nq * BLOCK, KV_W), cur), pl.BlockSpec((BLOCK, KV_W), prev),
                  pl.BlockSpec((BLOCK, KV_W), fix2), pl.BlockSpec((BLOCK, KV_W), fix2),
                  pl.BlockSpec((N_Q_HEADS, BLOCK, 2 * BLOCK), lambda b, j: (0, 0, 0))],
        out_specs=pl.BlockSpec((nq * BLOCK, ATTN_W), cur),
        out_shape=SDS((t, ATTN_W), BF16),
        compiler_params=_cparams("arbitrary", "arbitrary"),
        name="swa_attn",
    )(sinks, q, k, k, v, v, k_meta, v_meta, bias)


def _rel_bias_table(rel_bias):
    qi = np.arange(BLOCK, dtype=np.int32)[:, None]
    kj = np.arange(2 * BLOCK, dtype=np.int32)[None, :]
    dist = BLOCK + qi - kj
    dc = np.clip(dist, 0, WINDOW - 1)
    nf = np.maximum(dc, 1).astype(np.float32)
    large = MAX_EXACT + (np.log(nf / np.float32(MAX_EXACT)) / np.float32(math.log(REL_MAX_DIST / MAX_EXACT))
                         * np.float32(NUM_BUCKETS - MAX_EXACT)).astype(np.int32)
    large = np.minimum(large, NUM_BUCKETS - 1)
    bucket = np.where(dc < MAX_EXACT, dc, large)
    in_window = (dist >= 0) & (dist < WINDOW)
    onehot = (bucket.reshape(-1, 1) == np.arange(NUM_BUCKETS)[None, :]).astype(np.float32)
    bias = jnp.dot(jnp.asarray(onehot), rel_bias.astype(F32), precision=lax.Precision.HIGHEST)
    bias = jnp.transpose(bias.reshape(BLOCK, 2 * BLOCK, N_Q_HEADS), (2, 0, 1))
    return jnp.where(in_window[None], bias, NEG)


def _conv_kernel(uc_ref, up_ref, um_ref, w_ref, cb_ref, g_ref, b_ref, o_ref, s_ref, sh_ref):
    first = pl.program_id(1) == 0
    s_ref[0:CONV_HALO, :] = jnp.where(first, um_ref[...], up_ref[...])
    s_ref[CONV_HALO:CONV_HALO + T_CONV, :] = uc_ref[...]
    off = CONV_HALO - (CONV_K - 1)
    span = sh_ref.shape[1]
    for c in range(0, T_CONV, R_CONV):
        for p in range(1, SUBLANES):
            sh_ref[p] = s_ref[c + p:c + p + span, :]
        acc = jnp.zeros((R_CONV, CONV_CH), F32) + cb_ref[...]
        for kk in range(CONV_K):
            p, a = (off + kk) % SUBLANES, (off + kk) // SUBLANES * SUBLANES
            if p == 0:
                win = s_ref[c + a:c + a + R_CONV, :]
            else:
                win = sh_ref[p, a:a + R_CONV, :]
            acc = acc + win * w_ref[kk:kk + 1, :]
        y = _layer_norm(acc, g_ref[...], b_ref[...])
        o_ref[c:c + R_CONV, :] = (y * _sigmoid(y)).astype(BF16)


def _conv_call(u, u_meta_halo, conv_w, conv_b, g, b, nbatch, seq):
    t = u.shape[0]
    nj = seq // T_CONV
    per = T_CONV // CONV_HALO
    cur = lambda bb, j: (bb * nj + j, 0)
    prev = lambda bb, j: (jnp.maximum((bb * nj + j) * per - 1, 0), 0)
    fix = lambda bb, j: (0, 0)
    return pl.pallas_call(
        _conv_kernel,
        grid=(nbatch, nj),
        in_specs=[pl.BlockSpec((T_CONV, CONV_CH), cur), pl.BlockSpec((CONV_HALO, CONV_CH), prev),
                  pl.BlockSpec((CONV_HALO, CONV_CH), fix), pl.BlockSpec((CONV_K, CONV_CH), fix),
                  pl.BlockSpec((1, CONV_CH), fix), pl.BlockSpec((1, CONV_CH), fix), pl.BlockSpec((1, CONV_CH), fix)],
        out_specs=pl.BlockSpec((T_CONV, CONV_CH), cur),
        out_shape=SDS((t, CONV_CH), BF16),
        scratch_shapes=[pltpu.VMEM((CONV_HALO + T_CONV, CONV_CH), F32),
                        pltpu.VMEM((SUBLANES, R_CONV + CONV_HALO - SUBLANES, CONV_CH), F32)],
        compiler_params=_cparams("arbitrary", "arbitrary"),
        name="conv_ln",
    )(u, u, u_meta_halo, conv_w, conv_b, g, b)


def _mix_kernel(x_ref, at_ref, cv_ref, gin_ref, bin_ref, woa_ref, woc_ref, g1_ref, b1_ref,
                wrh_ref, wrl_ref, h1_ref, h1r_ref, lg_ref):
    rows = x_ref.shape[0] // MIX_CHAINS
    nt = (((1,), (1,)), ((), ()))
    for c in range(MIX_CHAINS):
        r = slice(c * rows, (c + 1) * rows)
        h = _layer_norm(x_ref[r, :], gin_ref[...], bin_ref[...])
        mix = (jnp.dot(at_ref[r, :], woa_ref[...], preferred_element_type=F32)
               + jnp.dot(cv_ref[r, :], woc_ref[...], preferred_element_type=F32))
        h1 = _layer_norm(ALPHA * h + mix, g1_ref[...], b1_ref[...])
        h1_ref[r, :] = h1
        _store_packed(h1r_ref, c * rows * ROW_CHUNKS, rows, _pack_rows(h1[:, :HALF], h1[:, HALF:]))
        hh = h1.astype(BF16)
        hl = (h1 - hh.astype(F32)).astype(BF16)
        lg = lax.dot_general(wrh_ref[...], hh, nt, preferred_element_type=F32)
        lg = lg + lax.dot_general(wrh_ref[...], hl, nt, preferred_element_type=F32)
        lg = lg + lax.dot_general(wrl_ref[...], hh, nt, preferred_element_type=F32)
        lg_ref[:, r] = lg


def _mix_call(x2d, attn, conv, gin, bin_, woa, woc, g1, b1, wrh, wrl):
    t = x2d.shape[0]
    tq = TQ_MIX
    row = lambda i: (i, 0)
    fix = lambda i: (0, 0)
    return pl.pallas_call(
        _mix_kernel,
        grid=(t // tq,),
        in_specs=[pl.BlockSpec((tq, D_MODEL), row), pl.BlockSpec((tq, ATTN_W), row), pl.BlockSpec((tq, CONV_CH), row),
                  pl.BlockSpec((1, D_MODEL), fix), pl.BlockSpec((1, D_MODEL), fix),
                  pl.BlockSpec((ATTN_W, D_MODEL), fix), pl.BlockSpec((CONV_CH, D_MODEL), fix),
                  pl.BlockSpec((1, D_MODEL), fix), pl.BlockSpec((1, D_MODEL), fix),
                  pl.BlockSpec((N_EXPERTS, D_MODEL), fix), pl.BlockSpec((N_EXPERTS, D_MODEL), fix)],
        out_specs=[pl.BlockSpec((tq, D_MODEL), row), pl.BlockSpec((tq * ROW_CHUNKS, LANES), row),
                   pl.BlockSpec((N_EXPERTS, tq), lambda i: (0, i))],
        out_shape=[SDS((t, D_MODEL), F32), SDS((t * ROW_CHUNKS, LANES), U32), SDS((N_EXPERTS, t), F32)],
        compiler_params=_cparams("arbitrary"),
        name="mix_ln1",
    )(x2d, attn, conv, gin, bin_, woa, woc, g1, b1, wrh, wrl)


def _first_argmax(x, rows, nrows):
    m = jnp.max(x, axis=0, keepdims=True)
    idx = jnp.min(jnp.where(x == m, rows, nrows), axis=0, keepdims=True)
    return m, idx


def _route_kernel(lg_ref, rb_ref, idx_ref, wts_ref, rank_ref, cnt_ref, carry_ref):
    tn = lg_ref.shape[1]

    @pl.when(pl.program_id(0) == 0)
    def _():
        carry_ref[...] = jnp.zeros_like(carry_ref)

    scores = _sigmoid(lg_ref[...])
    choice = scores + rb_ref[...]
    rows = lax.broadcasted_iota(I32, (N_EXPERTS, tn), 0)
    rows_g = lax.broadcasted_iota(I32, (GROUP_SIZE, tn), 0)
    rows_8 = lax.broadcasted_iota(I32, (N_GROUPS, tn), 0)

    gs = []
    for g in range(N_GROUPS):
        xg = choice[g * GROUP_SIZE:(g + 1) * GROUP_SIZE, :]
        m1, i1 = _first_argmax(xg, rows_g, GROUP_SIZE)
        m2 = jnp.max(jnp.where(rows_g == i1, -jnp.inf, xg), axis=0, keepdims=True)
        gs.append(m1 + m2)
    gsc = jnp.concatenate(gs, axis=0)
    gsel = jnp.zeros((N_GROUPS, tn), F32)
    for _ in range(TOPK_GROUPS):
        _, gi = _first_argmax(gsc, rows_8, N_GROUPS)
        hit = rows_8 == gi
        gsel = jnp.where(hit, 1.0, gsel)
        gsc = jnp.where(hit, -jnp.inf, gsc)
    emask = jnp.concatenate(
        [jnp.broadcast_to(gsel[g:g + 1, :], (GROUP_SIZE, tn)) for g in range(N_GROUPS)], axis=0)
    masked = jnp.where(emask > 0.5, choice, NEG)

    sel_all = jnp.zeros((N_EXPERTS, tn), F32)
    hits, idxs, ws = [], [], []
    for _ in range(TOP_K):
        _, ii = _first_argmax(masked, rows, N_EXPERTS)
        hit = rows == ii
        hits.append(hit)
        idxs.append(ii)
        ws.append(jnp.sum(jnp.where(hit, scores, 0.0), axis=0, keepdims=True))
        sel_all = jnp.where(hit, 1.0, sel_all)
        masked = jnp.where(hit, -jnp.inf, masked)
    wsum = ws[0]
    for w in ws[1:]:
        wsum = wsum + w
    idx_ref[...] = jnp.concatenate(idxs, axis=0)
    wts_ref[...] = jnp.concatenate([w / wsum * ROUTED_SCALE for w in ws], axis=0)

    r_i = lax.broadcasted_iota(I32, (tn, tn), 0)
    c_i = lax.broadcasted_iota(I32, (tn, tn), 1)
    upper = jnp.where(r_i < c_i, 1.0, 0.0).astype(BF16)
    sel_b = sel_all.astype(BF16)
    carry = carry_ref[...]
    before = jnp.dot(sel_b, upper, preferred_element_type=F32)
    before = before + jnp.concatenate([carry] * (tn // 128), axis=1)
    rank_ref[...] = jnp.concatenate(
        [jnp.sum(jnp.where(h, before, 0.0), axis=0, keepdims=True) for h in hits], axis=0).astype(I32)
    carry = carry + jnp.dot(sel_b, jnp.ones((tn, 128), BF16), preferred_element_type=F32)
    carry_ref[...] = carry
    cnt_ref[...] = carry.astype(I32)


def _route_call(lg, rbias):
    t = lg.shape[1]
    tn = TN_ROUTE
    col = lambda i: (0, i)
    return pl.pallas_call(
        _route_kernel,
        grid=(t // tn,),
        in_specs=[pl.BlockSpec((N_EXPERTS, tn), col), pl.BlockSpec((N_EXPERTS, 1), lambda i: (0, 0))],
        out_specs=[pl.BlockSpec((TOP_K, tn), col), pl.BlockSpec((TOP_K, tn), col), pl.BlockSpec((TOP_K, tn), col),
                   pl.BlockSpec((N_EXPERTS, 128), lambda i: (0, 0))],
        out_shape=[SDS((TOP_K, t), I32), SDS((TOP_K, t), F32), SDS((TOP_K, t), I32), SDS((N_EXPERTS, 128), I32)],
        scratch_shapes=[pltpu.VMEM((N_EXPERTS, 128), F32)],
        compiler_params=_cparams("arbitrary"),
        name="route",
    )(lg, rbias)


def _dest_kernel(idx_ref, rank_ref, offs_ref, dest_ref):
    tn = idx_ref.shape[1]
    rows = lax.broadcasted_iota(I32, (N_EXPERTS, tn), 0)
    offs = offs_ref[...]
    out = []
    for kk in range(TOP_K):
        hit = rows == idx_ref[kk:kk + 1, :]
        out.append(jnp.sum(jnp.where(hit, offs, 0.0), axis=0, keepdims=True))
    dest_ref[...] = jnp.concatenate(out, axis=0).astype(I32) + rank_ref[...]


def _dest_call(idx, rank, offs_col):
    t = idx.shape[1]
    tn = TN_ROUTE
    col = lambda i: (0, i)
    return pl.pallas_call(
        _dest_kernel,
        grid=(t // tn,),
        in_specs=[pl.BlockSpec((TOP_K, tn), col), pl.BlockSpec((TOP_K, tn), col),
                  pl.BlockSpec((N_EXPERTS, 1), lambda i: (0, 0))],
        out_specs=pl.BlockSpec((TOP_K, tn), col),
        out_shape=SDS((TOP_K, t), I32),
        compiler_params=_cparams("arbitrary"),
        name="dest",
    )(idx, rank, offs_col)


SC_CORES = 2
SC_SUBCORES = 16
SC_CHUNK = 128


def _sc_worker_chunks(t):
    per_worker = t // (SC_CORES * SC_SUBCORES)
    assert per_worker % SC_CHUNK == 0
    return per_worker


def _sc_dispatch_call(dest, h1rows3, n_rows):
    per_worker = _sc_worker_chunks(dest.shape[1])
    mesh = plsc.VectorSubcoreMesh(core_axis_name="c", subcore_axis_name="s")

    @functools.partial(
        pl.kernel, mesh=mesh, out_type=SDS((n_rows, ROW_CHUNKS, LANES), U32),
        scratch_types=[pltpu.VMEM((TOP_K, SC_CHUNK), I32), pltpu.VMEM((SC_CHUNK, ROW_CHUNKS, LANES), U32),
                       pltpu.SemaphoreType.DMA],
        name="sc_dispatch")
    def body(h_hbm, dest_hbm, xs_hbm, idx_v, rows_v, sem):
        wid = lax.axis_index("s") * SC_CORES + lax.axis_index("c")

        @pl.loop(0, per_worker // SC_CHUNK)
        def _(i):
            t0 = wid * per_worker + i * SC_CHUNK
            pltpu.sync_copy(dest_hbm.at[:, pl.ds(t0, SC_CHUNK)], idx_v)
            pltpu.sync_copy(h_hbm.at[pl.ds(t0, SC_CHUNK)], rows_v)
            copies = [pltpu.async_copy(rows_v, xs_hbm.at[idx_v.at[kk]], sem) for kk in range(TOP_K)]
            for c in copies:
                c.wait()

    return body(h1rows3, dest)


def _sc_gather_call(dest, ys3):
    t = dest.shape[1]
    per_worker = _sc_worker_chunks(t)
    mesh = plsc.VectorSubcoreMesh(core_axis_name="c", subcore_axis_name="s")

    @functools.partial(
        pl.kernel, mesh=mesh, out_type=SDS((TOP_K * t, ROW_CHUNKS, LANES), U32),
        scratch_types=[pltpu.VMEM((TOP_K, SC_CHUNK), I32), pltpu.VMEM((SC_CHUNK, ROW_CHUNKS, LANES), U32),
                       pltpu.SemaphoreType.DMA],
        name="sc_gather")
    def body(ys_hbm, dest_hbm, out_hbm, idx_v, rows_v, sem):
        wid = lax.axis_index("s") * SC_CORES + lax.axis_index("c")

        @pl.loop(0, per_worker // SC_CHUNK)
        def _(i):
            t0 = wid * per_worker + i * SC_CHUNK
            pltpu.sync_copy(dest_hbm.at[:, pl.ds(t0, SC_CHUNK)], idx_v)
            for kk in range(TOP_K):
                pltpu.async_copy(ys_hbm.at[idx_v.at[kk]], rows_v, sem).wait()
                pltpu.sync_copy(rows_v, out_hbm.at[pl.ds(kk * t + t0, SC_CHUNK)])

    return body(ys3, dest)


def _expert_kernel(ts_ref, te_ref, tr_ref, nv_ref, wg_ref, wu_ref, wd_ref, xs_hbm, _run_after_hbm, ys_hbm,
                   xbuf, ybuf, wg_b, wu_b, wd_b, xsem, ysem):
    e = pl.program_id(0)
    rows = xbuf.shape[1]
    tm = rows // ROW_CHUNKS
    g0, g1, nv = ts_ref[e], te_ref[e], nv_ref[0]

    def x_copy(g):
        s = g % X_SLOTS
        return pltpu.make_async_copy(xs_hbm.at[pl.ds(pl.multiple_of(g * rows, rows), rows), :], xbuf.at[s], xsem.at[s])

    def y_copy(g):
        s = g % Y_SLOTS
        return pltpu.make_async_copy(ybuf.at[s], ys_hbm.at[pl.ds(pl.multiple_of(g * rows, rows), rows), :], ysem.at[s])

    def compute_tile(g):
        x = _load_packed_bf16(xbuf, 0, tm, lead=g % X_SLOTS)
        gate = jnp.dot(x, wg_b[...], preferred_element_type=F32)
        up = jnp.dot(x, wu_b[...], preferred_element_type=F32)
        live = lax.broadcasted_iota(I32, (tm, EXPERT_FF), 0) < tr_ref[g]
        hid = jnp.where(live, gate * _sigmoid(gate) * up, 0.0).astype(BF16)
        y = jnp.dot(hid, wd_b[...], preferred_element_type=F32)
        return _pack_rows(y[:, :HALF], y[:, HALF:])

    def run_tiles(g, n):
        for r in range(n):
            x_copy(g + r).wait()

            @pl.when(g + r + X_AHEAD < nv)
            def _():
                x_copy(g + r + X_AHEAD).start(priority=1)

            @pl.when(g + r >= Y_SLOTS)
            def _():
                y_copy(g + r - Y_SLOTS).wait()

        packed = [compute_tile(g + r) for r in range(n)]
        for r in range(n):
            _store_packed(ybuf, 0, tm, packed[r], lead=(g + r) % Y_SLOTS)
        for r in range(n):
            y_copy(g + r).start(priority=1)

    @pl.when(e == 0)
    def _():
        for g in range(X_AHEAD):
            @pl.when(g < nv)
            def _():
                x_copy(g).start(priority=1)

    @pl.when(g1 > g0)
    def _():
        wg_b[...] = wg_ref[0].astype(BF16)
        wu_b[...] = wu_ref[0].astype(BF16)
        wd_b[...] = wd_ref[0].astype(BF16)
        n_tiles = g1 - g0

        def pair(p, c):
            run_tiles(g0 + 2 * p, 2)
            return c

        lax.fori_loop(0, n_tiles // 2, pair, 0)

        @pl.when(n_tiles % 2 == 1)
        def _():
            run_tiles(g1 - 1, 1)

    @pl.when(e == pl.num_programs(0) - 1)
    def _():
        for back in range(1, Y_SLOTS + 1):
            @pl.when(nv >= back)
            def _():
                y_copy(nv - back).wait()


def _expert_call(tile_start, tile_end, tile_rows, n_valid, xs, w_gate, w_up, w_down, n_rows, run_after):
    tm = TM_EXP
    w_map = lambda e, *_: (e, 0, 0)
    hbm = pl.BlockSpec(memory_space=pl.ANY)
    return pl.pallas_call(
        _expert_kernel,
        grid_spec=pltpu.PrefetchScalarGridSpec(
            num_scalar_prefetch=4,
            grid=(N_EXPERTS,),
            in_specs=[pl.BlockSpec((1, D_MODEL, EXPERT_FF), w_map), pl.BlockSpec((1, D_MODEL, EXPERT_FF), w_map),
                      pl.BlockSpec((1, EXPERT_FF, D_MODEL), w_map), hbm, hbm],
            out_specs=hbm,
            scratch_shapes=[pltpu.VMEM((X_SLOTS, tm * ROW_CHUNKS, LANES), U32),
                            pltpu.VMEM((Y_SLOTS, tm * ROW_CHUNKS, LANES), U32),
                            pltpu.VMEM((D_MODEL, EXPERT_FF), BF16), pltpu.VMEM((D_MODEL, EXPERT_FF), BF16),
                            pltpu.VMEM((EXPERT_FF, D_MODEL), BF16),
                            pltpu.SemaphoreType.DMA((X_SLOTS,)), pltpu.SemaphoreType.DMA((Y_SLOTS,))],
        ),
        out_shape=SDS((n_rows * ROW_CHUNKS, LANES), U32),
        compiler_params=_cparams("arbitrary"),
        name="experts",
    )(tile_start, tile_end, tile_rows, n_valid, w_gate, w_up, w_down, xs, run_after)


COMB_SUB = 32


def _shared_kernel(h1_ref, wsg_ref, wsu_ref, wsd_ref, o_ref):
    rows = h1_ref.shape[0] // SHARED_CHAINS
    for c in range(SHARED_CHAINS):
        r = slice(c * rows, (c + 1) * rows)
        h1 = h1_ref[r, :]
        hb = h1.astype(BF16)
        sg = jnp.dot(hb, wsg_ref[...], preferred_element_type=F32)
        su = jnp.dot(hb, wsu_ref[...], preferred_element_type=F32)
        ff = jnp.dot((sg * _sigmoid(sg) * su).astype(BF16), wsd_ref[...], preferred_element_type=F32)
        o_ref[r, :] = ALPHA * h1 + ff


def _shared_call(h1, wsg, wsu, wsd):
    t = h1.shape[0]
    tn = TN_SHARED
    row = lambda i: (i, 0)
    fix = lambda i: (0, 0)
    return pl.pallas_call(
        _shared_kernel,
        grid=(t // tn,),
        in_specs=[pl.BlockSpec((tn, D_MODEL), row),
                  pl.BlockSpec((D_MODEL, SHARED_FF), fix), pl.BlockSpec((D_MODEL, SHARED_FF), fix),
                  pl.BlockSpec((SHARED_FF, D_MODEL), fix)],
        out_specs=pl.BlockSpec((tn, D_MODEL), row),
        out_shape=SDS((t, D_MODEL), F32),
        compiler_params=_cparams("arbitrary"),
        name="shared_resid",
    )(h1, wsg, wsu, wsd)


def _combine_kernel(wts_ref, resid_ref, g_ref, g2_ref, b2_ref, *rest):
    o_ref, routed_ref = rest[-2:]
    tn = resid_ref.shape[0]
    for s0 in range(0, tn, COMB_SUB):
        acc = [jnp.zeros((COMB_SUB, LANES), F32) for _ in range(2 * ROW_CHUNKS)]
        for kk in range(TOP_K):
            wk = jnp.broadcast_to(wts_ref[s0:s0 + COMB_SUB, kk:kk + 1], (COMB_SUB, LANES))
            for cc in range(ROW_CHUNKS):
                lo, hi = _unpack_rows(g_ref[kk, pl.ds(s0 * ROW_CHUNKS + cc, COMB_SUB, stride=ROW_CHUNKS), :])
                acc[cc] = acc[cc] + wk * lo
                acc[ROW_CHUNKS + cc] = acc[ROW_CHUNKS + cc] + wk * hi
        routed_ref[s0:s0 + COMB_SUB, :] = jnp.concatenate(acc, axis=1)
    o_ref[...] = _layer_norm(resid_ref[...] + routed_ref[...], g2_ref[...], b2_ref[...])


def _combine_call(wts_t, resid, gathered, g2, b2, out_prev, first_tile):
    t = resid.shape[0]
    tn = TN_COMB
    n_steps = gathered.shape[1] // (tn * ROW_CHUNKS)
    row = lambda i: (first_tile + i, 0)
    fix = lambda i: (0, 0)
    in_specs = [pl.BlockSpec((tn, TOP_K), row),
                pl.BlockSpec((tn, D_MODEL), row),
                pl.BlockSpec((TOP_K, tn * ROW_CHUNKS, LANES), lambda i: (0, i, 0)),
                pl.BlockSpec((1, D_MODEL), fix), pl.BlockSpec((1, D_MODEL), fix)]
    args = [wts_t, resid, gathered, g2, b2]
    aliases = {}
    if out_prev is not None:
        in_specs.append(pl.BlockSpec(memory_space=pl.ANY))
        args.append(out_prev)
        aliases = {len(args) - 1: 0}
    return pl.pallas_call(
        _combine_kernel,
        grid=(n_steps,),
        in_specs=in_specs,
        out_specs=pl.BlockSpec((tn, D_MODEL), row),
        out_shape=SDS((t, D_MODEL), F32),
        scratch_shapes=[pltpu.VMEM((tn, D_MODEL), F32)],
        input_output_aliases=aliases,
        compiler_params=_cparams("arbitrary"),
        name="combine_ln2",
    )(*args)


def kernel(x, meta_tokens, ln_in_g, ln_in_b, rel_bias, w_in, conv_w, conv_b, conv_ln_g, conv_ln_b, sinks,
           w_out, ln1_g, ln1_b, w_router, router_bias, w_gate, w_up, w_down, ws_gate, ws_up, ws_down,
           ln2_g, ln2_b):
    nbatch, seq, d = x.shape
    assert d == D_MODEL and seq % TQ_PROJ == 0 and w_in.shape[0] == DEPTH
    t = nbatch * seq
    x2d = x.reshape(t, D_MODEL)
    vec = lambda a: a.reshape(1, -1).astype(F32)
    gin, bin_ = vec(ln_in_g), vec(ln_in_b)
    w_in_b = w_in[0].astype(BF16)

    q, k, v, u = _proj_call(x2d, gin, bin_, w_in_b, TQ_PROJ)
    meta_blk = jnp.concatenate([jnp.zeros((PAD_FRONT, D_MODEL), F32), meta_tokens.astype(F32)], axis=0)
    _, k_meta, v_meta, u_meta = _proj_call(meta_blk, gin, bin_, w_in_b, BLOCK)

    attn = _attn_call(q, k, v, k_meta, v_meta, _rel_bias_table(rel_bias), sinks[0].astype(F32),
                      nbatch, seq // BLOCK)

    u_halo = jnp.concatenate([jnp.zeros((CONV_HALO - N_META, CONV_CH), F32), u_meta[PAD_FRONT:]], axis=0)
    conv = _conv_call(u, u_halo, conv_w[0].astype(F32), vec(conv_b[0]), vec(conv_ln_g[0]), vec(conv_ln_b[0]),
                      nbatch, seq)

    w_out_b = w_out[0].astype(BF16)
    wr_t = w_router[0].astype(F32).T
    wr_hi = wr_t.astype(BF16)
    wr_lo = (wr_t - wr_hi.astype(F32)).astype(BF16)
    h1, h1rows, logits = _mix_call(x2d, attn, conv, gin, bin_, w_out_b[:ATTN_W], w_out_b[ATTN_W:],
                                   vec(ln1_g[0]), vec(ln1_b[0]), wr_hi, wr_lo)

    idx, wts, rank, cnt = _route_call(logits, router_bias[0].astype(F32).reshape(N_EXPERTS, 1))

    tm = TM_EXP
    n_tiles = (t * TOP_K) // tm + N_EXPERTS
    counts = cnt[:, 0]
    tiles_e = (counts + tm - 1) // tm
    tile_end = jnp.cumsum(tiles_e).astype(I32)
    tile_start = (tile_end - tiles_e).astype(I32)
    offs = tile_start * tm
    tile_id = jnp.arange(n_tiles, dtype=I32)
    lo = jnp.maximum(tile_id[:, None] * tm, offs[None, :])
    hi = jnp.minimum((tile_id[:, None] + 1) * tm, (offs + counts)[None, :])
    tile_rows = jnp.sum(jnp.clip(hi - lo, 0, tm), axis=1).astype(I32)
    n_valid = tile_end[-1:]

    dest = _dest_call(idx, rank, offs.astype(F32).reshape(N_EXPERTS, 1))
    xs = _sc_dispatch_call(dest, h1rows.reshape(t, ROW_CHUNKS, LANES), n_tiles * tm)
    xs = xs.reshape(n_tiles * tm * ROW_CHUNKS, LANES)
    resid = _shared_call(h1, ws_gate[0].astype(BF16), ws_up[0].astype(BF16), ws_down[0].astype(BF16))
    ys = _expert_call(tile_start, tile_end, tile_rows, n_valid, xs, w_gate[0], w_up[0], w_down[0], n_tiles * tm,
                      run_after=resid)
    ys3 = ys.reshape(n_tiles * tm, ROW_CHUNKS, LANES)

    wts_t = wts.T
    g2, b2 = vec(ln2_g[0]), vec(ln2_b[0])
    tc = t // GATHER_CHUNKS
    assert tc % TN_COMB == 0
    out = None
    for c in range(GATHER_CHUNKS):
        gathered = _sc_gather_call(dest[:, c * tc:(c + 1) * tc], ys3).reshape(TOP_K, tc * ROW_CHUNKS, LANES)
        out = _combine_call(wts_t, resid, gathered, g2, b2, out, c * tc // TN_COMB)
    return out.reshape(nbatch, seq, D_MODEL)
```

```python
import functools
import math

import numpy as np
import jax
import jax.numpy as jnp
from jax import lax
from jax.experimental import pallas as pl
from jax.experimental.pallas import tpu as pltpu
from jax.experimental.pallas import tpu_sc as plsc

F32 = jnp.float32
BF16 = jnp.bfloat16
I32 = jnp.int32
U32 = jnp.uint32
SDS = jax.ShapeDtypeStruct

D_MODEL = 1024
HALF = D_MODEL // 2
LANES = 128
SUBLANES = 8
ROW_CHUNKS = HALF // LANES
N_META = 16
HEAD_DIM = 64
N_Q_HEADS = 8
N_KV_HEADS = 2
GQA_GROUP = N_Q_HEADS // N_KV_HEADS
ATTN_W = N_Q_HEADS * HEAD_DIM
KV_W = N_KV_HEADS * HEAD_DIM
WINDOW = 128
BLOCK = 128
CONV_CH = D_MODEL - ATTN_W
CONV_K = 31
IN_W = ATTN_W + 2 * KV_W + 2 * CONV_CH
NUM_BUCKETS = 32
MAX_EXACT = NUM_BUCKETS // 2
REL_MAX_DIST = 128
N_EXPERTS = 256
TOP_K = 8
N_GROUPS = 8
GROUP_SIZE = N_EXPERTS // N_GROUPS
TOPK_GROUPS = 4
EXPERT_FF = 256
SHARED_FF = 256
ROUTED_SCALE = 2.5
DEPTH = 1
ALPHA = (2.0 * DEPTH) ** 0.25
LN_EPS = 1e-5
NEG = -1e30
PAD_FRONT = (-N_META) % BLOCK

VMEM_LIMIT = 48 * 1024 * 1024

TQ_PROJ = 512
PROJ_CHAINS = 2
ATTN_QBLOCKS = 2
T_CONV = 256
CONV_HALO = 32
R_CONV = 64
TQ_MIX = 512
MIX_CHAINS = 2
TN_ROUTE = 256
TM_EXP = 256
X_SLOTS = 8
X_AHEAD = 4
Y_SLOTS = 4
W_SLOTS = 3
W_AHEAD = 2
TN_COMB = 256


def _cparams(*sem):
    return pltpu.CompilerParams(dimension_semantics=sem, vmem_limit_bytes=VMEM_LIMIT)


def _layer_norm(x, g, b):
    mu = jnp.mean(x, axis=-1, keepdims=True)
    xc = x - mu
    var = jnp.mean(xc * xc, axis=-1, keepdims=True)
    return xc * lax.rsqrt(var + LN_EPS) * g + b


def _sigmoid(x):
    return 1.0 / (1.0 + jnp.exp(-x))


def _pack_rows(lo_half, hi_half):
    lo = lax.bitcast_convert_type(lo_half.astype(BF16).astype(F32), U32)
    hi = lax.bitcast_convert_type(hi_half.astype(BF16).astype(F32), U32)
    return lax.shift_right_logical(lo, jnp.uint32(16)) | hi


def _unpack_rows(p):
    lo = lax.bitcast_convert_type(lax.shift_left(p, jnp.uint32(16)), F32)
    hi = lax.bitcast_convert_type(p & jnp.uint32(0xFFFF0000), F32)
    return lo, hi


def _chunk_index(start, j, n, lead):
    rows = pl.ds(start + j, n, stride=ROW_CHUNKS)
    return (rows, slice(None)) if lead is None else (lead, rows, slice(None))


def _store_packed(ref, start, n, packed, lead=None):
    for j in range(ROW_CHUNKS):
        ref[_chunk_index(start, j, n, lead)] = packed[:, j * LANES:(j + 1) * LANES]


def _load_packed_bf16(ref, start, n, lead=None):
    halves = [_unpack_rows(ref[_chunk_index(start, j, n, lead)]) for j in range(ROW_CHUNKS)]
    return jnp.concatenate([h[0] for h in halves] + [h[1] for h in halves], axis=1).astype(BF16)


def _proj_kernel(chains, x_ref, g_ref, b_ref, w_ref, q_ref, k_ref, v_ref, u_ref):
    rows = x_ref.shape[0] // chains
    for c in range(chains):
        r = slice(c * rows, (c + 1) * rows)
        h = _layer_norm(x_ref[r, :], g_ref[...], b_ref[...])
        p = jnp.dot(h.astype(BF16), w_ref[...], preferred_element_type=F32)
        q_ref[r, :] = (p[:, :ATTN_W] * (HEAD_DIM ** -0.5)).astype(BF16)
        k_ref[r, :] = p[:, ATTN_W:ATTN_W + KV_W].astype(BF16)
        v_ref[r, :] = p[:, ATTN_W + KV_W:ATTN_W + 2 * KV_W].astype(BF16)
        a = p[:, ATTN_W + 2 * KV_W:ATTN_W + 2 * KV_W + CONV_CH]
        gate = p[:, ATTN_W + 2 * KV_W + CONV_CH:]
        u_ref[r, :] = a * _sigmoid(gate)


def _proj_call(x2d, gin, bin_, w_in_b, tq):
    t = x2d.shape[0]
    row = lambda i: (i, 0)
    fix = lambda i: (0, 0)
    chains = PROJ_CHAINS if tq % (PROJ_CHAINS * BLOCK) == 0 else 1
    return pl.pallas_call(
        functools.partial(_proj_kernel, chains),
        grid=(t // tq,),
        in_specs=[pl.BlockSpec((tq, D_MODEL), row), pl.BlockSpec((1, D_MODEL), fix),
                  pl.BlockSpec((1, D_MODEL), fix), pl.BlockSpec((D_MODEL, IN_W), fix)],
        out_specs=[pl.BlockSpec((tq, ATTN_W), row), pl.BlockSpec((tq, KV_W), row),
                   pl.BlockSpec((tq, KV_W), row), pl.BlockSpec((tq, CONV_CH), row)],
        out_shape=[SDS((t, ATTN_W), BF16), SDS((t, KV_W), BF16), SDS((t, KV_W), BF16), SDS((t, CONV_CH), F32)],
        compiler_params=_cparams("arbitrary"),
        name="ln_in_proj",
    )(x2d, gin, bin_, w_in_b)


def _attn_kernel(sinks_ref, q_ref, kc_ref, kp_ref, vc_ref, vp_ref, km_ref, vm_ref, bias_ref, o_ref):
    first = pl.program_id(1) == 0
    kp = jnp.where(first, km_ref[...], kp_ref[...])
    vp = jnp.where(first, vm_ref[...], vp_ref[...])
    k = jnp.concatenate([kp, kc_ref[...]], axis=0)
    v = jnp.concatenate([vp, vc_ref[...]], axis=0)
    col = lax.broadcasted_iota(I32, (BLOCK, 2 * BLOCK), 1)
    pad_bias = jnp.where(jnp.logical_and(first, col < PAD_FRONT), NEG, 0.0).astype(F32)
    for a in range(ATTN_QBLOCKS):
        q = q_ref[a * BLOCK:(a + 1) * BLOCK, :]
        kw = k[a * BLOCK:(a + 2) * BLOCK, :]
        vw = v[a * BLOCK:(a + 2) * BLOCK, :]
        outs = []
        for h in range(N_Q_HEADS):
            g = h // GQA_GROUP
            qh = q[:, h * HEAD_DIM:(h + 1) * HEAD_DIM]
            kg = kw[:, g * HEAD_DIM:(g + 1) * HEAD_DIM]
            vg = vw[:, g * HEAD_DIM:(g + 1) * HEAD_DIM]
            s = lax.dot_general(qh, kg, (((1,), (1,)), ((), ())), preferred_element_type=F32)
            s = s + bias_ref[h]
            if a == 0:
                s = s + pad_bias
            sink = sinks_ref[h]
            m = jnp.maximum(jnp.max(s, axis=-1, keepdims=True), sink)
            p = jnp.exp(s - m)
            den = jnp.sum(p, axis=-1, keepdims=True) + jnp.exp(sink - m)
            o = jnp.dot(p.astype(BF16), vg, preferred_element_type=F32)
            outs.append(o / den)
        o_ref[a * BLOCK:(a + 1) * BLOCK, :] = jnp.concatenate(outs, axis=1).astype(BF16)


def _attn_call(q, k, v, k_meta, v_meta, bias, sinks, nbatch, nblk):
    t = q.shape[0]
    nq = ATTN_QBLOCKS
    assert nblk % nq == 0
    nstep = nblk // nq
    cur = lambda b, j: (b * nstep + j, 0)
    prev = lambda b, j: (jnp.maximum((b * nstep + j) * nq - 1, 0), 0)
    fix2 = lambda b, j: (0, 0)
    return pl.pallas_call(
        _attn_kernel,
        grid=(nbatch, nstep),
        in_specs=[pl.BlockSpec(memory_space=pltpu.SMEM),
                  pl.BlockSpec((nq * BLOCK, ATTN_W), cur),
                  pl.BlockSpec((nq * BLOCK, KV_W), cur), pl.BlockSpec((BLOCK, KV_W), prev),
                  pl.BlockSpec((nq * BLOCK, KV_W), cur), pl.BlockSpec((BLOCK, KV_W), prev),
                  pl.BlockSpec((BLOCK, KV_W), fix2), pl.BlockSpec((BLOCK, KV_W), fix2),
                  pl.BlockSpec((N_Q_HEADS, BLOCK, 2 * BLOCK), lambda b, j: (0, 0, 0))],
        out_specs=pl.BlockSpec((nq * BLOCK, ATTN_W), cur),
        out_shape=SDS((t, ATTN_W), BF16),
        compiler_params=_cparams("arbitrary", "arbitrary"),
        name="swa_attn",
    )(sinks, q, k, k, v, v, k_meta, v_meta, bias)


def _rel_bias_table(rel_bias):
    qi = np.arange(BLOCK, dtype=np.int32)[:, None]
    kj = np.arange(2 * BLOCK, dtype=np.int32)[None, :]
    dist = BLOCK + qi - kj
    dc = np.clip(dist, 0, WINDOW - 1)
    nf = np.maximum(dc, 1).astype(np.float32)
    large = MAX_EXACT + (np.log(nf / np.float32(MAX_EXACT)) / np.float32(math.log(REL_MAX_DIST / MAX_EXACT))
                         * np.float32(NUM_BUCKETS - MAX_EXACT)).astype(np.int32)
    large = np.minimum(large, NUM_BUCKETS - 1)
    bucket = np.where(dc < MAX_EXACT, dc, large)
    in_window = (dist >= 0) & (dist < WINDOW)
    onehot = (bucket.reshape(-1, 1) == np.arange(NUM_BUCKETS)[None, :]).astype(np.float32)
    bias = jnp.dot(jnp.asarray(onehot), rel_bias.astype(F32), precision=lax.Precision.HIGHEST)
    bias = jnp.transpose(bias.reshape(BLOCK, 2 * BLOCK, N_Q_HEADS), (2, 0, 1))
    return jnp.where(in_window[None], bias, NEG)


def _conv_kernel(uc_ref, up_ref, um_ref, w_ref, cb_ref, g_ref, b_ref, o_ref, s_ref, sh_ref):
    first = pl.program_id(1) == 0
    s_ref[0:CONV_HALO, :] = jnp.where(first, um_ref[...], up_ref[...])
    s_ref[CONV_HALO:CONV_HALO + T_CONV, :] = uc_ref[...]
    off = CONV_HALO - (CONV_K - 1)
    span = sh_ref.shape[1]
    for c in range(0, T_CONV, R_CONV):
        for p in range(1, SUBLANES):
            sh_ref[p] = s_ref[c + p:c + p + span, :]
        acc = jnp.zeros((R_CONV, CONV_CH), F32) + cb_ref[...]
        for kk in range(CONV_K):
            p, a = (off + kk) % SUBLANES, (off + kk) // SUBLANES * SUBLANES
            if p == 0:
                win = s_ref[c + a:c + a + R_CONV, :]
            else:
                win = sh_ref[p, a:a + R_CONV, :]
            acc = acc + win * w_ref[kk:kk + 1, :]
        y = _layer_norm(acc, g_ref[...], b_ref[...])
        o_ref[c:c + R_CONV, :] = (y * _sigmoid(y)).astype(BF16)


def _conv_call(u, u_meta_halo, conv_w, conv_b, g, b, nbatch, seq):
    t = u.shape[0]
    nj = seq // T_CONV
    per = T_CONV // CONV_HALO
    cur = lambda bb, j: (bb * nj + j, 0)
    prev = lambda bb, j: (jnp.maximum((bb * nj + j) * per - 1, 0), 0)
    fix = lambda bb, j: (0, 0)
    return pl.pallas_call(
        _conv_kernel,
        grid=(nbatch, nj),
        in_specs=[pl.BlockSpec((T_CONV, CONV_CH), cur), pl.BlockSpec((CONV_HALO, CONV_CH), prev),
                  pl.BlockSpec((CONV_HALO, CONV_CH), fix), pl.BlockSpec((CONV_K, CONV_CH), fix),
                  pl.BlockSpec((1, CONV_CH), fix), pl.BlockSpec((1, CONV_CH), fix), pl.BlockSpec((1, CONV_CH), fix)],
        out_specs=pl.BlockSpec((T_CONV, CONV_CH), cur),
        out_shape=SDS((t, CONV_CH), BF16),
        scratch_shapes=[pltpu.VMEM((CONV_HALO + T_CONV, CONV_CH), F32),
                        pltpu.VMEM((SUBLANES, R_CONV + CONV_HALO - SUBLANES, CONV_CH), F32)],
        compiler_params=_cparams("arbitrary", "arbitrary"),
        name="conv_ln",
    )(u, u, u_meta_halo, conv_w, conv_b, g, b)


def _mix_kernel(x_ref, at_ref, cv_ref, gin_ref, bin_ref, woa_ref, woc_ref, g1_ref, b1_ref,
                wrh_ref, wrl_ref, h1_ref, h1r_ref, lg_ref):
    rows = x_ref.shape[0] // MIX_CHAINS
    nt = (((1,), (1,)), ((), ()))
    for c in range(MIX_CHAINS):
        r = slice(c * rows, (c + 1) * rows)
        h = _layer_norm(x_ref[r, :], gin_ref[...], bin_ref[...])
        mix = (jnp.dot(at_ref[r, :], woa_ref[...], preferred_element_type=F32)
               + jnp.dot(cv_ref[r, :], woc_ref[...], preferred_element_type=F32))
        h1 = _layer_norm(ALPHA * h + mix, g1_ref[...], b1_ref[...])
        h1_ref[r, :] = h1
        _store_packed(h1r_ref, c * rows * ROW_CHUNKS, rows, _pack_rows(h1[:, :HALF], h1[:, HALF:]))
        hh = h1.astype(BF16)
        hl = (h1 - hh.astype(F32)).astype(BF16)
        lg = lax.dot_general(wrh_ref[...], hh, nt, preferred_element_type=F32)
        lg = lg + lax.dot_general(wrh_ref[...], hl, nt, preferred_element_type=F32)
        lg = lg + lax.dot_general(wrl_ref[...], hh, nt, preferred_element_type=F32)
        lg_ref[:, r] = lg


def _mix_call(x2d, attn, conv, gin, bin_, woa, woc, g1, b1, wrh, wrl):
    t = x2d.shape[0]
    tq = TQ_MIX
    row = lambda i: (i, 0)
    fix = lambda i: (0, 0)
    return pl.pallas_call(
        _mix_kernel,
        grid=(t // tq,),
        in_specs=[pl.BlockSpec((tq, D_MODEL), row), pl.BlockSpec((tq, ATTN_W), row), pl.BlockSpec((tq, CONV_CH), row),
                  pl.BlockSpec((1, D_MODEL), fix), pl.BlockSpec((1, D_MODEL), fix),
                  pl.BlockSpec((ATTN_W, D_MODEL), fix), pl.BlockSpec((CONV_CH, D_MODEL), fix),
                  pl.BlockSpec((1, D_MODEL), fix), pl.BlockSpec((1, D_MODEL), fix),
                  pl.BlockSpec((N_EXPERTS, D_MODEL), fix), pl.BlockSpec((N_EXPERTS, D_MODEL), fix)],
        out_specs=[pl.BlockSpec((tq, D_MODEL), row), pl.BlockSpec((tq * ROW_CHUNKS, LANES), row),
                   pl.BlockSpec((N_EXPERTS, tq), lambda i: (0, i))],
        out_shape=[SDS((t, D_MODEL), F32), SDS((t * ROW_CHUNKS, LANES), U32), SDS((N_EXPERTS, t), F32)],
        compiler_params=_cparams("arbitrary"),
        name="mix_ln1",
    )(x2d, attn, conv, gin, bin_, woa, woc, g1, b1, wrh, wrl)


def _first_argmax(x, rows, nrows):
    m = jnp.max(x, axis=0, keepdims=True)
    idx = jnp.min(jnp.where(x == m, rows, nrows), axis=0, keepdims=True)
    return m, idx


def _route_kernel(lg_ref, rb_ref, idx_ref, wts_ref, rank_ref, cnt_ref, carry_ref):
    tn = lg_ref.shape[1]

    @pl.when(pl.program_id(0) == 0)
    def _():
        carry_ref[...] = jnp.zeros_like(carry_ref)

    scores = _sigmoid(lg_ref[...])
    choice = scores + rb_ref[...]
    rows = lax.broadcasted_iota(I32, (N_EXPERTS, tn), 0)
    rows_g = lax.broadcasted_iota(I32, (GROUP_SIZE, tn), 0)
    rows_8 = lax.broadcasted_iota(I32, (N_GROUPS, tn), 0)

    gs = []
    for g in range(N_GROUPS):
        xg = choice[g * GROUP_SIZE:(g + 1) * GROUP_SIZE, :]
        m1, i1 = _first_argmax(xg, rows_g, GROUP_SIZE)
        m2 = jnp.max(jnp.where(rows_g == i1, -jnp.inf, xg), axis=0, keepdims=True)
        gs.append(m1 + m2)
    gsc = jnp.concatenate(gs, axis=0)
    gsel = jnp.zeros((N_GROUPS, tn), F32)
    for _ in range(TOPK_GROUPS):
        _, gi = _first_argmax(gsc, rows_8, N_GROUPS)
        hit = rows_8 == gi
        gsel = jnp.where(hit, 1.0, gsel)
        gsc = jnp.where(hit, -jnp.inf, gsc)
    emask = jnp.concatenate(
        [jnp.broadcast_to(gsel[g:g + 1, :], (GROUP_SIZE, tn)) for g in range(N_GROUPS)], axis=0)
    masked = jnp.where(emask > 0.5, choice, NEG)

    sel_all = jnp.zeros((N_EXPERTS, tn), F32)
    hits, idxs, ws = [], [], []
    for _ in range(TOP_K):
        _, ii = _first_argmax(masked, rows, N_EXPERTS)
        hit = rows == ii
        hits.append(hit)
        idxs.append(ii)
        ws.append(jnp.sum(jnp.where(hit, scores, 0.0), axis=0, keepdims=True))
        sel_all = jnp.where(hit, 1.0, sel_all)
        masked = jnp.where(hit, -jnp.inf, masked)
    wsum = ws[0]
    for w in ws[1:]:
        wsum = wsum + w
    idx_ref[...] = jnp.concatenate(idxs, axis=0)
    wts_ref[...] = jnp.concatenate([w / wsum * ROUTED_SCALE for w in ws], axis=0)

    r_i = lax.broadcasted_iota(I32, (tn, tn), 0)
    c_i = lax.broadcasted_iota(I32, (tn, tn), 1)
    upper = jnp.where(r_i < c_i, 1.0, 0.0).astype(BF16)
    sel_b = sel_all.astype(BF16)
    carry = carry_ref[...]
    before = jnp.dot(sel_b, upper, preferred_element_type=F32)
    before = before + jnp.concatenate([carry] * (tn // 128), axis=1)
    rank_ref[...] = jnp.concatenate(
        [jnp.sum(jnp.where(h, before, 0.0), axis=0, keepdims=True) for h in hits], axis=0).astype(I32)
    carry = carry + jnp.dot(sel_b, jnp.ones((tn, 128), BF16), preferred_element_type=F32)
    carry_ref[...] = carry
    cnt_ref[...] = carry.astype(I32)


def _route_call(lg, rbias):
    t = lg.shape[1]
    tn = TN_ROUTE
    col = lambda i: (0, i)
    return pl.pallas_call(
        _route_kernel,
        grid=(t // tn,),
        in_specs=[pl.BlockSpec((N_EXPERTS, tn), col), pl.BlockSpec((N_EXPERTS, 1), lambda i: (0, 0))],
        out_specs=[pl.BlockSpec((TOP_K, tn), col), pl.BlockSpec((TOP_K, tn), col), pl.BlockSpec((TOP_K, tn), col),
                   pl.BlockSpec((N_EXPERTS, 128), lambda i: (0, 0))],
        out_shape=[SDS((TOP_K, t), I32), SDS((TOP_K, t), F32), SDS((TOP_K, t), I32), SDS((N_EXPERTS, 128), I32)],
        scratch_shapes=[pltpu.VMEM((N_EXPERTS, 128), F32)],
        compiler_params=_cparams("arbitrary"),
        name="route",
    )(lg, rbias)


def _dest_kernel(idx_ref, rank_ref, offs_ref, dest_ref):
    tn = idx_ref.shape[1]
    rows = lax.broadcasted_iota(I32, (N_EXPERTS, tn), 0)
    offs = offs_ref[...]
    out = []
    for kk in range(TOP_K):
        hit = rows == idx_ref[kk:kk + 1, :]
        out.append(jnp.sum(jnp.where(hit, offs, 0.0), axis=0, keepdims=True))
    dest_ref[...] = jnp.concatenate(out, axis=0).astype(I32) + rank_ref[...]


def _dest_call(idx, rank, offs_col):
    t = idx.shape[1]
    tn = TN_ROUTE
    col = lambda i: (0, i)
    return pl.pallas_call(
        _dest_kernel,
        grid=(t // tn,),
        in_specs=[pl.BlockSpec((TOP_K, tn), col), pl.BlockSpec((TOP_K, tn), col),
                  pl.BlockSpec((N_EXPERTS, 1), lambda i: (0, 0))],
        out_specs=pl.BlockSpec((TOP_K, tn), col),
        out_shape=SDS((TOP_K, t), I32),
        compiler_params=_cparams("arbitrary"),
        name="dest",
    )(idx, rank, offs_col)


SC_CORES = 2
SC_SUBCORES = 16
SC_CHUNK = 128


def _sc_worker_chunks(t):
    per_worker = t // (SC_CORES * SC_SUBCORES)
    assert per_worker % SC_CHUNK == 0
    return per_worker


def _sc_dispatch_call(dest, h1rows3, n_rows):
    per_worker = _sc_worker_chunks(dest.shape[1])
    mesh = plsc.VectorSubcoreMesh(core_axis_name="c", subcore_axis_name="s")

    @functools.partial(
        pl.kernel, mesh=mesh, out_type=SDS((n_rows, ROW_CHUNKS, LANES), U32),
        scratch_types=[pltpu.VMEM((TOP_K, SC_CHUNK), I32), pltpu.VMEM((SC_CHUNK, ROW_CHUNKS, LANES), U32),
                       pltpu.SemaphoreType.DMA],
        name="sc_dispatch")
    def body(h_hbm, dest_hbm, xs_hbm, idx_v, rows_v, sem):
        wid = lax.axis_index("s") * SC_CORES + lax.axis_index("c")

        @pl.loop(0, per_worker // SC_CHUNK)
        def _(i):
            t0 = wid * per_worker + i * SC_CHUNK
            pltpu.sync_copy(dest_hbm.at[:, pl.ds(t0, SC_CHUNK)], idx_v)
            pltpu.sync_copy(h_hbm.at[pl.ds(t0, SC_CHUNK)], rows_v)
            copies = [pltpu.async_copy(rows_v, xs_hbm.at[idx_v.at[kk]], sem) for kk in range(TOP_K)]
            for c in copies:
                c.wait()

    return body(h1rows3, dest)


def _sc_gather_call(dest, ys3):
    t = dest.shape[1]
    per_worker = _sc_worker_chunks(t)
    mesh = plsc.VectorSubcoreMesh(core_axis_name="c", subcore_axis_name="s")

    @functools.partial(
        pl.kernel, mesh=mesh, out_type=SDS((TOP_K * t, ROW_CHUNKS, LANES), U32),
        scratch_types=[pltpu.VMEM((TOP_K, SC_CHUNK), I32), pltpu.VMEM((SC_CHUNK, ROW_CHUNKS, LANES), U32),
                       pltpu.SemaphoreType.DMA],
        name="sc_gather")
    def body(ys_hbm, dest_hbm, out_hbm, idx_v, rows_v, sem):
        wid = lax.axis_index("s") * SC_CORES + lax.axis_index("c")

        @pl.loop(0, per_worker // SC_CHUNK)
        def _(i):
            t0 = wid * per_worker + i * SC_CHUNK
            pltpu.sync_copy(dest_hbm.at[:, pl.ds(t0, SC_CHUNK)], idx_v)
            for kk in range(TOP_K):
                pltpu.async_copy(ys_hbm.at[idx_v.at[kk]], rows_v, sem).wait()
                pltpu.sync_copy(rows_v, out_hbm.at[pl.ds(kk * t + t0, SC_CHUNK)])

    return body(ys3, dest)


def _expert_kernel(ts_ref, te_ref, tr_ref, nv_ref, wg_hbm, wu_hbm, wd_hbm, xs_hbm, ys_hbm,
                   xbuf, ybuf, wg_f, wu_f, wd_f, wg_b, wu_b, wd_b, xsem, ysem, wsem):
    e = pl.program_id(0)
    rows = xbuf.shape[1]
    tm = rows // ROW_CHUNKS
    g0, g1, nv = ts_ref[e], te_ref[e], nv_ref[0]

    def x_copy(g):
        s = g % X_SLOTS
        return pltpu.make_async_copy(xs_hbm.at[pl.ds(pl.multiple_of(g * rows, rows), rows), :], xbuf.at[s], xsem.at[s])

    def y_copy(g):
        s = g % Y_SLOTS
        return pltpu.make_async_copy(ybuf.at[s], ys_hbm.at[pl.ds(pl.multiple_of(g * rows, rows), rows), :], ysem.at[s])

    def w_copies(ex):
        s = ex % W_SLOTS
        return (pltpu.make_async_copy(wg_hbm.at[ex], wg_f.at[s], wsem.at[s]),
                pltpu.make_async_copy(wu_hbm.at[ex], wu_f.at[s], wsem.at[s]),
                pltpu.make_async_copy(wd_hbm.at[ex], wd_f.at[s], wsem.at[s]))

    n_exp = pl.num_programs(0)

    @pl.when(e == 0)
    def _():
        for ex in range(W_AHEAD):
            for c in w_copies(ex):
                c.start()

    @pl.when(e + W_AHEAD < n_exp)
    def _():
        for c in w_copies(e + W_AHEAD):
            c.start()

    for c in w_copies(e):
        c.wait()

    def compute_tile(g):
        x = _load_packed_bf16(xbuf, 0, tm, lead=g % X_SLOTS)
        gate = jnp.dot(x, wg_b[...], preferred_element_type=F32)
        up = jnp.dot(x, wu_b[...], preferred_element_type=F32)
        live = lax.broadcasted_iota(I32, (tm, EXPERT_FF), 0) < tr_ref[g]
        hid = jnp.where(live, gate * _sigmoid(gate) * up, 0.0).astype(BF16)
        y = jnp.dot(hid, wd_b[...], preferred_element_type=F32)
        return _pack_rows(y[:, :HALF], y[:, HALF:])

    def run_tiles(g, n):
        for r in range(n):
            x_copy(g + r).wait()

            @pl.when(g + r + X_AHEAD < nv)
            def _():
                x_copy(g + r + X_AHEAD).start(priority=1)

            @pl.when(g + r >= Y_SLOTS)
            def _():
                y_copy(g + r - Y_SLOTS).wait()

        packed = [compute_tile(g + r) for r in range(n)]
        for r in range(n):
            _store_packed(ybuf, 0, tm, packed[r], lead=(g + r) % Y_SLOTS)
        for r in range(n):
            y_copy(g + r).start(priority=1)

    @pl.when(e == 0)
    def _():
        for g in range(X_AHEAD):
            @pl.when(g < nv)
            def _():
                x_copy(g).start(priority=1)

    @pl.when(g1 > g0)
    def _():
        ws = e % W_SLOTS
        wg_b[...] = wg_f[ws].astype(BF16)
        wu_b[...] = wu_f[ws].astype(BF16)
        wd_b[...] = wd_f[ws].astype(BF16)
        n_tiles = g1 - g0

        def pair(p, c):
            run_tiles(g0 + 2 * p, 2)
            return c

        lax.fori_loop(0, n_tiles // 2, pair, 0)

        @pl.when(n_tiles % 2 == 1)
        def _():
            run_tiles(g1 - 1, 1)

    @pl.when(e == pl.num_programs(0) - 1)
    def _():
        for back in range(1, Y_SLOTS + 1):
            @pl.when(nv >= back)
            def _():
                y_copy(nv - back).wait()


def _expert_call(tile_start, tile_end, tile_rows, n_valid, xs, w_gate, w_up, w_down, n_rows):
    tm = TM_EXP
    hbm = pl.BlockSpec(memory_space=pl.ANY)
    return pl.pallas_call(
        _expert_kernel,
        grid_spec=pltpu.PrefetchScalarGridSpec(
            num_scalar_prefetch=4,
            grid=(N_EXPERTS,),
            in_specs=[hbm, hbm, hbm, hbm],
            out_specs=hbm,
            scratch_shapes=[pltpu.VMEM((X_SLOTS, tm * ROW_CHUNKS, LANES), U32),
                            pltpu.VMEM((Y_SLOTS, tm * ROW_CHUNKS, LANES), U32),
                            pltpu.VMEM((W_SLOTS, D_MODEL, EXPERT_FF), F32), pltpu.VMEM((W_SLOTS, D_MODEL, EXPERT_FF), F32),
                            pltpu.VMEM((W_SLOTS, EXPERT_FF, D_MODEL), F32),
                            pltpu.VMEM((D_MODEL, EXPERT_FF), BF16), pltpu.VMEM((D_MODEL, EXPERT_FF), BF16),
                            pltpu.VMEM((EXPERT_FF, D_MODEL), BF16),
                            pltpu.SemaphoreType.DMA((X_SLOTS,)), pltpu.SemaphoreType.DMA((Y_SLOTS,)),
                            pltpu.SemaphoreType.DMA((W_SLOTS,))],
        ),
        out_shape=SDS((n_rows * ROW_CHUNKS, LANES), U32),
        compiler_params=_cparams("arbitrary"),
        name="experts",
    )(tile_start, tile_end, tile_rows, n_valid, w_gate, w_up, w_down, xs)


COMB_SUB = 32


def _combine_kernel(wts_ref, h1_ref, g_ref, wsg_ref, wsu_ref, wsd_ref, g2_ref, b2_ref, o_ref, routed_ref):
    tn = h1_ref.shape[0]
    for s0 in range(0, tn, COMB_SUB):
        acc = [jnp.zeros((COMB_SUB, LANES), F32) for _ in range(2 * ROW_CHUNKS)]
        for kk in range(TOP_K):
            wk = jnp.broadcast_to(wts_ref[s0:s0 + COMB_SUB, kk:kk + 1], (COMB_SUB, LANES))
            for cc in range(ROW_CHUNKS):
                lo, hi = _unpack_rows(g_ref[kk, pl.ds(s0 * ROW_CHUNKS + cc, COMB_SUB, stride=ROW_CHUNKS), :])
                acc[cc] = acc[cc] + wk * lo
                acc[ROW_CHUNKS + cc] = acc[ROW_CHUNKS + cc] + wk * hi
        routed_ref[s0:s0 + COMB_SUB, :] = jnp.concatenate(acc, axis=1)

    h1 = h1_ref[...]
    hb = h1.astype(BF16)
    sg = jnp.dot(hb, wsg_ref[...], preferred_element_type=F32)
    su = jnp.dot(hb, wsu_ref[...], preferred_element_type=F32)
    ff = jnp.dot((sg * _sigmoid(sg) * su).astype(BF16), wsd_ref[...], preferred_element_type=F32)
    o_ref[...] = _layer_norm(ALPHA * h1 + ff + routed_ref[...], g2_ref[...], b2_ref[...])


def _combine_call(wts_t, h1, gathered, wsg, wsu, wsd, g2, b2):
    t = h1.shape[0]
    tn = TN_COMB
    row = lambda i: (i, 0)
    fix = lambda i: (0, 0)
    return pl.pallas_call(
        _combine_kernel,
        grid=(t // tn,),
        in_specs=[pl.BlockSpec((tn, TOP_K), row),
                  pl.BlockSpec((tn, D_MODEL), row),
                  pl.BlockSpec((TOP_K, tn * ROW_CHUNKS, LANES), lambda i: (0, i, 0)),
                  pl.BlockSpec((D_MODEL, SHARED_FF), fix), pl.BlockSpec((D_MODEL, SHARED_FF), fix),
                  pl.BlockSpec((SHARED_FF, D_MODEL), fix),
                  pl.BlockSpec((1, D_MODEL), fix), pl.BlockSpec((1, D_MODEL), fix)],
        out_specs=pl.BlockSpec((tn, D_MODEL), row),
        out_shape=SDS((t, D_MODEL), F32),
        scratch_shapes=[pltpu.VMEM((tn, D_MODEL), F32)],
        compiler_params=_cparams("arbitrary"),
        name="combine_ln2",
    )(wts_t, h1, gathered, wsg, wsu, wsd, g2, b2)


def kernel(x, meta_tokens, ln_in_g, ln_in_b, rel_bias, w_in, conv_w, conv_b, conv_ln_g, conv_ln_b, sinks,
           w_out, ln1_g, ln1_b, w_router, router_bias, w_gate, w_up, w_down, ws_gate, ws_up, ws_down,
           ln2_g, ln2_b):
    nbatch, seq, d = x.shape
    assert d == D_MODEL and seq % TQ_PROJ == 0 and w_in.shape[0] == DEPTH
    t = nbatch * seq
    x2d = x.reshape(t, D_MODEL)
    vec = lambda a: a.reshape(1, -1).astype(F32)
    gin, bin_ = vec(ln_in_g), vec(ln_in_b)
    w_in_b = w_in[0].astype(BF16)

    q, k, v, u = _proj_call(x2d, gin, bin_, w_in_b, TQ_PROJ)
    meta_blk = jnp.concatenate([jnp.zeros((PAD_FRONT, D_MODEL), F32), meta_tokens.astype(F32)], axis=0)
    _, k_meta, v_meta, u_meta = _proj_call(meta_blk, gin, bin_, w_in_b, BLOCK)

    attn = _attn_call(q, k, v, k_meta, v_meta, _rel_bias_table(rel_bias), sinks[0].astype(F32),
                      nbatch, seq // BLOCK)

    u_halo = jnp.concatenate([jnp.zeros((CONV_HALO - N_META, CONV_CH), F32), u_meta[PAD_FRONT:]], axis=0)
    conv = _conv_call(u, u_halo, conv_w[0].astype(F32), vec(conv_b[0]), vec(conv_ln_g[0]), vec(conv_ln_b[0]),
                      nbatch, seq)

    w_out_b = w_out[0].astype(BF16)
    wr_t = w_router[0].astype(F32).T
    wr_hi = wr_t.astype(BF16)
    wr_lo = (wr_t - wr_hi.astype(F32)).astype(BF16)
    h1, h1rows, logits = _mix_call(x2d, attn, conv, gin, bin_, w_out_b[:ATTN_W], w_out_b[ATTN_W:],
                                   vec(ln1_g[0]), vec(ln1_b[0]), wr_hi, wr_lo)

    idx, wts, rank, cnt = _route_call(logits, router_bias[0].astype(F32).reshape(N_EXPERTS, 1))

    tm = TM_EXP
    n_tiles = (t * TOP_K) // tm + N_EXPERTS
    counts = cnt[:, 0]
    tiles_e = (counts + tm - 1) // tm
    tile_end = jnp.cumsum(tiles_e).astype(I32)
    tile_start = (tile_end - tiles_e).astype(I32)
    offs = tile_start * tm
    tile_id = jnp.arange(n_tiles, dtype=I32)
    lo = jnp.maximum(tile_id[:, None] * tm, offs[None, :])
    hi = jnp.minimum((tile_id[:, None] + 1) * tm, (offs + counts)[None, :])
    tile_rows = jnp.sum(jnp.clip(hi - lo, 0, tm), axis=1).astype(I32)
    n_valid = tile_end[-1:]

    dest = _dest_call(idx, rank, offs.astype(F32).reshape(N_EXPERTS, 1))
    xs = _sc_dispatch_call(dest, h1rows.reshape(t, ROW_CHUNKS, LANES), n_tiles * tm)
    xs = xs.reshape(n_tiles * tm * ROW_CHUNKS, LANES)
    ys = _expert_call(tile_start, tile_end, tile_rows, n_valid, xs, w_gate[0], w_up[0], w_down[0], n_tiles * tm)
    gathered = _sc_gather_call(dest, ys.reshape(n_tiles * tm, ROW_CHUNKS, LANES))
    gathered = gathered.reshape(TOP_K, t * ROW_CHUNKS, LANES)
    out = _combine_call(wts.T, h1, gathered, ws_gate[0].astype(BF16), ws_up[0].astype(BF16),
                        ws_down[0].astype(BF16), vec(ln2_g[0]), vec(ln2_b[0]))
    return out.reshape(nbatch, seq, D_MODEL)
```

```python
import functools
import math

import numpy as np
import jax
import jax.numpy as jnp
from jax import lax
from jax.experimental import pallas as pl
from jax.experimental.pallas import tpu as pltpu
from jax.experimental.pallas import tpu_sc as plsc

F32 = jnp.float32
BF16 = jnp.bfloat16
I32 = jnp.int32
U32 = jnp.uint32
SDS = jax.ShapeDtypeStruct

D_MODEL = 1024
HALF = D_MODEL // 2
LANES = 128
SUBLANES = 8
ROW_CHUNKS = HALF // LANES
N_META = 16
HEAD_DIM = 64
N_Q_HEADS = 8
N_KV_HEADS = 2
GQA_GROUP = N_Q_HEADS // N_KV_HEADS
ATTN_W = N_Q_HEADS * HEAD_DIM
KV_W = N_KV_HEADS * HEAD_DIM
WINDOW = 128
BLOCK = 128
CONV_CH = D_MODEL - ATTN_W
CONV_K = 31
IN_W = ATTN_W + 2 * KV_W + 2 * CONV_CH
NUM_BUCKETS = 32
MAX_EXACT = NUM_BUCKETS // 2
REL_MAX_DIST = 128
N_EXPERTS = 256
TOP_K = 8
N_GROUPS = 8
GROUP_SIZE = N_EXPERTS // N_GROUPS
TOPK_GROUPS = 4
EXPERT_FF = 256
SHARED_FF = 256
ROUTED_SCALE = 2.5
DEPTH = 1
ALPHA = (2.0 * DEPTH) ** 0.25
LN_EPS = 1e-5
NEG = -1e30
PAD_FRONT = (-N_META) % BLOCK

VMEM_LIMIT = 48 * 1024 * 1024

TQ_PROJ = 512
PROJ_CHAINS = 2
ATTN_QBLOCKS = 2
T_CONV = 256
CONV_HALO = 32
R_CONV = 64
TQ_MIX = 512
MIX_CHAINS = 2
TN_ROUTE = 256
TM_EXP = 256
X_SLOTS = 8
X_AHEAD = 4
Y_SLOTS = 4
W_SLOTS = 4
W_AHEAD = 3
TN_COMB = 256


def _cparams(*sem):
    return pltpu.CompilerParams(dimension_semantics=sem, vmem_limit_bytes=VMEM_LIMIT)


def _layer_norm(x, g, b):
    mu = jnp.mean(x, axis=-1, keepdims=True)
    xc = x - mu
    var = jnp.mean(xc * xc, axis=-1, keepdims=True)
    return xc * lax.rsqrt(var + LN_EPS) * g + b


def _sigmoid(x):
    return 1.0 / (1.0 + jnp.exp(-x))


def _pack_rows(lo_half, hi_half):
    lo = lax.bitcast_convert_type(lo_half.astype(BF16).astype(F32), U32)
    hi = lax.bitcast_convert_type(hi_half.astype(BF16).astype(F32), U32)
    return lax.shift_right_logical(lo, jnp.uint32(16)) | hi


def _unpack_rows(p):
    lo = lax.bitcast_convert_type(lax.shift_left(p, jnp.uint32(16)), F32)
    hi = lax.bitcast_convert_type(p & jnp.uint32(0xFFFF0000), F32)
    return lo, hi


def _chunk_index(start, j, n, lead):
    rows = pl.ds(start + j, n, stride=ROW_CHUNKS)
    return (rows, slice(None)) if lead is None else (lead, rows, slice(None))


def _store_packed(ref, start, n, packed, lead=None):
    for j in range(ROW_CHUNKS):
        ref[_chunk_index(start, j, n, lead)] = packed[:, j * LANES:(j + 1) * LANES]


def _load_packed_bf16(ref, start, n, lead=None):
    halves = [_unpack_rows(ref[_chunk_index(start, j, n, lead)]) for j in range(ROW_CHUNKS)]
    return jnp.concatenate([h[0] for h in halves] + [h[1] for h in halves], axis=1).astype(BF16)


def _proj_kernel(chains, x_ref, g_ref, b_ref, w_ref, q_ref, k_ref, v_ref, u_ref):
    rows = x_ref.shape[0] // chains
    for c in range(chains):
        r = slice(c * rows, (c + 1) * rows)
        h = _layer_norm(x_ref[r, :], g_ref[...], b_ref[...])
        p = jnp.dot(h.astype(BF16), w_ref[...], preferred_element_type=F32)
        q_ref[r, :] = (p[:, :ATTN_W] * (HEAD_DIM ** -0.5)).astype(BF16)
        k_ref[r, :] = p[:, ATTN_W:ATTN_W + KV_W].astype(BF16)
        v_ref[r, :] = p[:, ATTN_W + KV_W:ATTN_W + 2 * KV_W].astype(BF16)
        a = p[:, ATTN_W + 2 * KV_W:ATTN_W + 2 * KV_W + CONV_CH]
        gate = p[:, ATTN_W + 2 * KV_W + CONV_CH:]
        u_ref[r, :] = a * _sigmoid(gate)


def _proj_call(x2d, gin, bin_, w_in_b, tq):
    t = x2d.shape[0]
    row = lambda i: (i, 0)
    fix = lambda i: (0, 0)
    chains = PROJ_CHAINS if tq % (PROJ_CHAINS * BLOCK) == 0 else 1
    return pl.pallas_call(
        functools.partial(_proj_kernel, chains),
        grid=(t // tq,),
        in_specs=[pl.BlockSpec((tq, D_MODEL), row), pl.BlockSpec((1, D_MODEL), fix),
                  pl.BlockSpec((1, D_MODEL), fix), pl.BlockSpec((D_MODEL, IN_W), fix)],
        out_specs=[pl.BlockSpec((tq, ATTN_W), row), pl.BlockSpec((tq, KV_W), row),
                   pl.BlockSpec((tq, KV_W), row), pl.BlockSpec((tq, CONV_CH), row)],
        out_shape=[SDS((t, ATTN_W), BF16), SDS((t, KV_W), BF16), SDS((t, KV_W), BF16), SDS((t, CONV_CH), F32)],
        compiler_params=_cparams("arbitrary"),
        name="ln_in_proj",
    )(x2d, gin, bin_, w_in_b)


def _attn_kernel(sinks_ref, q_ref, kc_ref, kp_ref, vc_ref, vp_ref, km_ref, vm_ref, bias_ref, o_ref):
    first = pl.program_id(1) == 0
    kp = jnp.where(first, km_ref[...], kp_ref[...])
    vp = jnp.where(first, vm_ref[...], vp_ref[...])
    k = jnp.concatenate([kp, kc_ref[...]], axis=0)
    v = jnp.concatenate([vp, vc_ref[...]], axis=0)
    col = lax.broadcasted_iota(I32, (BLOCK, 2 * BLOCK), 1)
    pad_bias = jnp.where(jnp.logical_and(first, col < PAD_FRONT), NEG, 0.0).astype(F32)
    for a in range(ATTN_QBLOCKS):
        q = q_ref[a * BLOCK:(a + 1) * BLOCK, :]
        kw = k[a * BLOCK:(a + 2) * BLOCK, :]
        vw = v[a * BLOCK:(a + 2) * BLOCK, :]
        outs = []
        for h in range(N_Q_HEADS):
            g = h // GQA_GROUP
            qh = q[:, h * HEAD_DIM:(h + 1) * HEAD_DIM]
            kg = kw[:, g * HEAD_DIM:(g + 1) * HEAD_DIM]
            vg = vw[:, g * HEAD_DIM:(g + 1) * HEAD_DIM]
            s = lax.dot_general(qh, kg, (((1,), (1,)), ((), ())), preferred_element_type=F32)
            s = s + bias_ref[h]
            if a == 0:
                s = s + pad_bias
            sink = sinks_ref[h]
            m = jnp.maximum(jnp.max(s, axis=-1, keepdims=True), sink)
            p = jnp.exp(s - m)
            den = jnp.sum(p, axis=-1, keepdims=True) + jnp.exp(sink - m)
            o = jnp.dot(p.astype(BF16), vg, preferred_element_type=F32)
            outs.append(o / den)
        o_ref[a * BLOCK:(a + 1) * BLOCK, :] = jnp.concatenate(outs, axis=1).astype(BF16)


def _attn_call(q, k, v, k_meta, v_meta, bias, sinks, nbatch, nblk):
    t = q.shape[0]
    nq = ATTN_QBLOCKS
    assert nblk % nq == 0
    nstep = nblk // nq
    cur = lambda b, j: (b * nstep + j, 0)
    prev = lambda b, j: (jnp.maximum((b * nstep + j) * nq - 1, 0), 0)
    fix2 = lambda b, j: (0, 0)
    return pl.pallas_call(
        _attn_kernel,
        grid=(nbatch, nstep),
        in_specs=[pl.BlockSpec(memory_space=pltpu.SMEM),
                  pl.BlockSpec((nq * BLOCK, ATTN_W), cur),
                  pl.BlockSpec((nq * BLOCK, KV_W), cur), pl.BlockSpec((BLOCK, KV_W), prev),
                  pl.BlockSpec((nq * BLOCK, KV_W), cur), pl.BlockSpec((BLOCK, KV_W), prev),
                  pl.BlockSpec((BLOCK, KV_W), fix2), pl.BlockSpec((BLOCK, KV_W), fix2),
                  pl.BlockSpec((N_Q_HEADS, BLOCK, 2 * BLOCK), lambda b, j: (0, 0, 0))],
        out_specs=pl.BlockSpec((nq * BLOCK, ATTN_W), cur),
        out_shape=SDS((t, ATTN_W), BF16),
        compiler_params=_cparams("arbitrary", "arbitrary"),
        name="swa_attn",
    )(sinks, q, k, k, v, v, k_meta, v_meta, bias)


def _rel_bias_table(rel_bias):
    qi = np.arange(BLOCK, dtype=np.int32)[:, None]
    kj = np.arange(2 * BLOCK, dtype=np.int32)[None, :]
    dist = BLOCK + qi - kj
    dc = np.clip(dist, 0, WINDOW - 1)
    nf = np.maximum(dc, 1).astype(np.float32)
    large = MAX_EXACT + (np.log(nf / np.float32(MAX_EXACT)) / np.float32(math.log(REL_MAX_DIST / MAX_EXACT))
                         * np.float32(NUM_BUCKETS - MAX_EXACT)).astype(np.int32)
    large = np.minimum(large, NUM_BUCKETS - 1)
    bucket = np.where(dc < MAX_EXACT, dc, large)
    in_window = (dist >= 0) & (dist < WINDOW)
    onehot = (bucket.reshape(-1, 1) == np.arange(NUM_BUCKETS)[None, :]).astype(np.float32)
    bias = jnp.dot(jnp.asarray(onehot), rel_bias.astype(F32), precision=lax.Precision.HIGHEST)
    bias = jnp.transpose(bias.reshape(BLOCK, 2 * BLOCK, N_Q_HEADS), (2, 0, 1))
    return jnp.where(in_window[None], bias, NEG)


def _conv_kernel(uc_ref, up_ref, um_ref, w_ref, cb_ref, g_ref, b_ref, o_ref, s_ref, sh_ref):
    first = pl.program_id(1) == 0
    s_ref[0:CONV_HALO, :] = jnp.where(first, um_ref[...], up_ref[...])
    s_ref[CONV_HALO:CONV_HALO + T_CONV, :] = uc_ref[...]
    off = CONV_HALO - (CONV_K - 1)
    span = sh_ref.shape[1]
    for p in range(1, SUBLANES):
        sh_ref[p] = s_ref[p:p + span, :]
    for c in range(0, T_CONV, R_CONV):
        acc = jnp.zeros((R_CONV, CONV_CH), F32) + cb_ref[...]
        for kk in range(CONV_K):
            p, a = (off + kk) % SUBLANES, (off + kk) // SUBLANES * SUBLANES
            if p == 0:
                win = s_ref[c + a:c + a + R_CONV, :]
            else:
                win = sh_ref[p, c + a:c + a + R_CONV, :]
            acc = acc + win * w_ref[kk:kk + 1, :]
        y = _layer_norm(acc, g_ref[...], b_ref[...])
        o_ref[c:c + R_CONV, :] = (y * _sigmoid(y)).astype(BF16)


def _conv_call(u, u_meta_halo, conv_w, conv_b, g, b, nbatch, seq):
    t = u.shape[0]
    nj = seq // T_CONV
    per = T_CONV // CONV_HALO
    cur = lambda bb, j: (bb * nj + j, 0)
    prev = lambda bb, j: (jnp.maximum((bb * nj + j) * per - 1, 0), 0)
    fix = lambda bb, j: (0, 0)
    return pl.pallas_call(
        _conv_kernel,
        grid=(nbatch, nj),
        in_specs=[pl.BlockSpec((T_CONV, CONV_CH), cur), pl.BlockSpec((CONV_HALO, CONV_CH), prev),
                  pl.BlockSpec((CONV_HALO, CONV_CH), fix), pl.BlockSpec((CONV_K, CONV_CH), fix),
                  pl.BlockSpec((1, CONV_CH), fix), pl.BlockSpec((1, CONV_CH), fix), pl.BlockSpec((1, CONV_CH), fix)],
        out_specs=pl.BlockSpec((T_CONV, CONV_CH), cur),
        out_shape=SDS((t, CONV_CH), BF16),
        scratch_shapes=[pltpu.VMEM((CONV_HALO + T_CONV, CONV_CH), F32),
                        pltpu.VMEM((SUBLANES, T_CONV + CONV_HALO - SUBLANES, CONV_CH), F32)],
        compiler_params=_cparams("arbitrary", "arbitrary"),
        name="conv_ln",
    )(u, u, u_meta_halo, conv_w, conv_b, g, b)


def _mix_kernel(x_ref, at_ref, cv_ref, gin_ref, bin_ref, woa_ref, woc_ref, g1_ref, b1_ref,
                wrh_ref, wrl_ref, h1_ref, h1r_ref, lg_ref):
    rows = x_ref.shape[0] // MIX_CHAINS
    nt = (((1,), (1,)), ((), ()))
    for c in range(MIX_CHAINS):
        r = slice(c * rows, (c + 1) * rows)
        h = _layer_norm(x_ref[r, :], gin_ref[...], bin_ref[...])
        mix = (jnp.dot(at_ref[r, :], woa_ref[...], preferred_element_type=F32)
               + jnp.dot(cv_ref[r, :], woc_ref[...], preferred_element_type=F32))
        h1 = _layer_norm(ALPHA * h + mix, g1_ref[...], b1_ref[...])
        h1_ref[r, :] = h1
        _store_packed(h1r_ref, c * rows * ROW_CHUNKS, rows, _pack_rows(h1[:, :HALF], h1[:, HALF:]))
        hh = h1.astype(BF16)
        hl = (h1 - hh.astype(F32)).astype(BF16)
        lg = lax.dot_general(wrh_ref[...], hh, nt, preferred_element_type=F32)
        lg = lg + lax.dot_general(wrh_ref[...], hl, nt, preferred_element_type=F32)
        lg = lg + lax.dot_general(wrl_ref[...], hh, nt, preferred_element_type=F32)
        lg_ref[:, r] = lg


def _mix_call(x2d, attn, conv, gin, bin_, woa, woc, g1, b1, wrh, wrl):
    t = x2d.shape[0]
    tq = TQ_MIX
    row = lambda i: (i, 0)
    fix = lambda i: (0, 0)
    return pl.pallas_call(
        _mix_kernel,
        grid=(t // tq,),
        in_specs=[pl.BlockSpec((tq, D_MODEL), row), pl.BlockSpec((tq, ATTN_W), row), pl.BlockSpec((tq, CONV_CH), row),
                  pl.BlockSpec((1, D_MODEL), fix), pl.BlockSpec((1, D_MODEL), fix),
                  pl.BlockSpec((ATTN_W, D_MODEL), fix), pl.BlockSpec((CONV_CH, D_MODEL), fix),
                  pl.BlockSpec((1, D_MODEL), fix), pl.BlockSpec((1, D_MODEL), fix),
                  pl.BlockSpec((N_EXPERTS, D_MODEL), fix), pl.BlockSpec((N_EXPERTS, D_MODEL), fix)],
        out_specs=[pl.BlockSpec((tq, D_MODEL), row), pl.BlockSpec((tq * ROW_CHUNKS, LANES), row),
                   pl.BlockSpec((N_EXPERTS, tq), lambda i: (0, i))],
        out_shape=[SDS((t, D_MODEL), F32), SDS((t * ROW_CHUNKS, LANES), U32), SDS((N_EXPERTS, t), F32)],
        compiler_params=_cparams("arbitrary"),
        name="mix_ln1",
    )(x2d, attn, conv, gin, bin_, woa, woc, g1, b1, wrh, wrl)


def _first_argmax(x, rows, nrows):
    m = jnp.max(x, axis=0, keepdims=True)
    idx = jnp.min(jnp.where(x == m, rows, nrows), axis=0, keepdims=True)
    return m, idx


def _route_kernel(lg_ref, rb_ref, idx_ref, wts_ref, rank_ref, cnt_ref, carry_ref):
    tn = lg_ref.shape[1]

    @pl.when(pl.program_id(0) == 0)
    def _():
        carry_ref[...] = jnp.zeros_like(carry_ref)

    scores = _sigmoid(lg_ref[...])
    choice = scores + rb_ref[...]
    rows = lax.broadcasted_iota(I32, (N_EXPERTS, tn), 0)
    rows_g = lax.broadcasted_iota(I32, (GROUP_SIZE, tn), 0)
    rows_8 = lax.broadcasted_iota(I32, (N_GROUPS, tn), 0)

    gs = []
    for g in range(N_GROUPS):
        xg = choice[g * GROUP_SIZE:(g + 1) * GROUP_SIZE, :]
        m1, i1 = _first_argmax(xg, rows_g, GROUP_SIZE)
        m2 = jnp.max(jnp.where(rows_g == i1, -jnp.inf, xg), axis=0, keepdims=True)
        gs.append(m1 + m2)
    gsc = jnp.concatenate(gs, axis=0)
    gsel = jnp.zeros((N_GROUPS, tn), F32)
    for _ in range(TOPK_GROUPS):
        _, gi = _first_argmax(gsc, rows_8, N_GROUPS)
        hit = rows_8 == gi
        gsel = jnp.where(hit, 1.0, gsel)
        gsc = jnp.where(hit, -jnp.inf, gsc)
    emask = jnp.concatenate(
        [jnp.broadcast_to(gsel[g:g + 1, :], (GROUP_SIZE, tn)) for g in range(N_GROUPS)], axis=0)
    masked = jnp.where(emask > 0.5, choice, NEG)

    sel_all = jnp.zeros((N_EXPERTS, tn), F32)
    hits, idxs, ws = [], [], []
    for _ in range(TOP_K):
        _, ii = _first_argmax(masked, rows, N_EXPERTS)
        hit = rows == ii
        hits.append(hit)
        idxs.append(ii)
        ws.append(jnp.sum(jnp.where(hit, scores, 0.0), axis=0, keepdims=True))
        sel_all = jnp.where(hit, 1.0, sel_all)
        masked = jnp.where(hit, -jnp.inf, masked)
    wsum = ws[0]
    for w in ws[1:]:
        wsum = wsum + w
    idx_ref[...] = jnp.concatenate(idxs, axis=0)
    wts_ref[...] = jnp.concatenate([w / wsum * ROUTED_SCALE for w in ws], axis=0)

    r_i = lax.broadcasted_iota(I32, (tn, tn), 0)
    c_i = lax.broadcasted_iota(I32, (tn, tn), 1)
    upper = jnp.where(r_i < c_i, 1.0, 0.0).astype(BF16)
    sel_b = sel_all.astype(BF16)
    carry = carry_ref[...]
    before = jnp.dot(sel_b, upper, preferred_element_type=F32)
    before = before + jnp.concatenate([carry] * (tn // 128), axis=1)
    rank_ref[...] = jnp.concatenate(
        [jnp.sum(jnp.where(h, before, 0.0), axis=0, keepdims=True) for h in hits], axis=0).astype(I32)
    carry = carry + jnp.dot(sel_b, jnp.ones((tn, 128), BF16), preferred_element_type=F32)
    carry_ref[...] = carry
    cnt_ref[...] = carry.astype(I32)


def _route_call(lg, rbias):
    t = lg.shape[1]
    tn = TN_ROUTE
    col = lambda i: (0, i)
    return pl.pallas_call(
        _route_kernel,
        grid=(t // tn,),
        in_specs=[pl.BlockSpec((N_EXPERTS, tn), col), pl.BlockSpec((N_EXPERTS, 1), lambda i: (0, 0))],
        out_specs=[pl.BlockSpec((TOP_K, tn), col), pl.BlockSpec((TOP_K, tn), col), pl.BlockSpec((TOP_K, tn), col),
                   pl.BlockSpec((N_EXPERTS, 128), lambda i: (0, 0))],
        out_shape=[SDS((TOP_K, t), I32), SDS((TOP_K, t), F32), SDS((TOP_K, t), I32), SDS((N_EXPERTS, 128), I32)],
        scratch_shapes=[pltpu.VMEM((N_EXPERTS, 128), F32)],
        compiler_params=_cparams("arbitrary"),
        name="route",
    )(lg, rbias)


def _dest_kernel(idx_ref, rank_ref, offs_ref, dest_ref):
    tn = idx_ref.shape[1]
    rows = lax.broadcasted_iota(I32, (N_EXPERTS, tn), 0)
    offs = offs_ref[...]
    out = []
    for kk in range(TOP_K):
        hit = rows == idx_ref[kk:kk + 1, :]
        out.append(jnp.sum(jnp.where(hit, offs, 0.0), axis=0, keepdims=True))
    dest_ref[...] = jnp.concatenate(out, axis=0).astype(I32) + rank_ref[...]


def _dest_call(idx, rank, offs_col):
    t = idx.shape[1]
    tn = TN_ROUTE
    col = lambda i: (0, i)
    return pl.pallas_call(
        _dest_kernel,
        grid=(t // tn,),
        in_specs=[pl.BlockSpec((TOP_K, tn), col), pl.BlockSpec((TOP_K, tn), col),
                  pl.BlockSpec((N_EXPERTS, 1), lambda i: (0, 0))],
        out_specs=pl.BlockSpec((TOP_K, tn), col),
        out_shape=SDS((TOP_K, t), I32),
        compiler_params=_cparams("arbitrary"),
        name="dest",
    )(idx, rank, offs_col)


SC_CORES = 2
SC_SUBCORES = 16
SC_CHUNK = 128


def _sc_worker_chunks(t):
    per_worker = t // (SC_CORES * SC_SUBCORES)
    assert per_worker % SC_CHUNK == 0
    return per_worker


def _sc_dispatch_call(dest, h1rows3, n_rows):
    per_worker = _sc_worker_chunks(dest.shape[1])
    mesh = plsc.VectorSubcoreMesh(core_axis_name="c", subcore_axis_name="s")

    @functools.partial(
        pl.kernel, mesh=mesh, out_type=SDS((n_rows, ROW_CHUNKS, LANES), U32),
        scratch_types=[pltpu.VMEM((TOP_K, SC_CHUNK), I32), pltpu.VMEM((SC_CHUNK, ROW_CHUNKS, LANES), U32),
                       pltpu.SemaphoreType.DMA],
        name="sc_dispatch")
    def body(h_hbm, dest_hbm, xs_hbm, idx_v, rows_v, sem):
        wid = lax.axis_index("s") * SC_CORES + lax.axis_index("c")

        @pl.loop(0, per_worker // SC_CHUNK)
        def _(i):
            t0 = wid * per_worker + i * SC_CHUNK
            pltpu.sync_copy(dest_hbm.at[:, pl.ds(t0, SC_CHUNK)], idx_v)
            pltpu.sync_copy(h_hbm.at[pl.ds(t0, SC_CHUNK)], rows_v)
            copies = [pltpu.async_copy(rows_v, xs_hbm.at[idx_v.at[kk]], sem) for kk in range(TOP_K)]
            for c in copies:
                c.wait()

    return body(h1rows3, dest)


def _sc_gather_call(dest, ys3):
    t = dest.shape[1]
    per_worker = _sc_worker_chunks(t)
    mesh = plsc.VectorSubcoreMesh(core_axis_name="c", subcore_axis_name="s")

    @functools.partial(
        pl.kernel, mesh=mesh, out_type=SDS((TOP_K * t, ROW_CHUNKS, LANES), U32),
        scratch_types=[pltpu.VMEM((TOP_K, SC_CHUNK), I32), pltpu.VMEM((SC_CHUNK, ROW_CHUNKS, LANES), U32),
                       pltpu.SemaphoreType.DMA],
        name="sc_gather")
    def body(ys_hbm, dest_hbm, out_hbm, idx_v, rows_v, sem):
        wid = lax.axis_index("s") * SC_CORES + lax.axis_index("c")

        @pl.loop(0, per_worker // SC_CHUNK)
        def _(i):
            t0 = wid * per_worker + i * SC_CHUNK
            pltpu.sync_copy(dest_hbm.at[:, pl.ds(t0, SC_CHUNK)], idx_v)
            for kk in range(TOP_K):
                pltpu.async_copy(ys_hbm.at[idx_v.at[kk]], rows_v, sem).wait()
                pltpu.sync_copy(rows_v, out_hbm.at[pl.ds(kk * t + t0, SC_CHUNK)])

    return body(ys3, dest)


def _expert_kernel(ts_ref, te_ref, tr_ref, nv_ref, wg_hbm, wu_hbm, wd_hbm, xs_hbm, ys_hbm,
                   xbuf, ybuf, wg_f, wu_f, wd_f, wg_b, wu_b, wd_b, xsem, ysem, wsem):
    e = pl.program_id(0)
    rows = xbuf.shape[1]
    tm = rows // ROW_CHUNKS
    g0, g1, nv = ts_ref[e], te_ref[e], nv_ref[0]

    def x_copy(g):
        s = g % X_SLOTS
        return pltpu.make_async_copy(xs_hbm.at[pl.ds(pl.multiple_of(g * rows, rows), rows), :], xbuf.at[s], xsem.at[s])

    def y_copy(g):
        s = g % Y_SLOTS
        return pltpu.make_async_copy(ybuf.at[s], ys_hbm.at[pl.ds(pl.multiple_of(g * rows, rows), rows), :], ysem.at[s])

    def w_copies(ex):
        s = ex % W_SLOTS
        return (pltpu.make_async_copy(wg_hbm.at[ex], wg_f.at[s], wsem.at[s]),
                pltpu.make_async_copy(wu_hbm.at[ex], wu_f.at[s], wsem.at[s]),
                pltpu.make_async_copy(wd_hbm.at[ex], wd_f.at[s], wsem.at[s]))

    n_exp = pl.num_programs(0)

    @pl.when(e == 0)
    def _():
        for ex in range(W_AHEAD):
            for c in w_copies(ex):
                c.start()

    @pl.when(e + W_AHEAD < n_exp)
    def _():
        for c in w_copies(e + W_AHEAD):
            c.start()

    for c in w_copies(e):
        c.wait()

    def compute_tile(g):
        x = _load_packed_bf16(xbuf, 0, tm, lead=g % X_SLOTS)
        gate = jnp.dot(x, wg_b[...], preferred_element_type=F32)
        up = jnp.dot(x, wu_b[...], preferred_element_type=F32)
        live = lax.broadcasted_iota(I32, (tm, EXPERT_FF), 0) < tr_ref[g]
        hid = jnp.where(live, gate * _sigmoid(gate) * up, 0.0).astype(BF16)
        y = jnp.dot(hid, wd_b[...], preferred_element_type=F32)
        return _pack_rows(y[:, :HALF], y[:, HALF:])

    def run_tiles(g, n):
        for r in range(n):
            x_copy(g + r).wait()

            @pl.when(g + r + X_AHEAD < nv)
            def _():
                x_copy(g + r + X_AHEAD).start(priority=1)

            @pl.when(g + r >= Y_SLOTS)
            def _():
                y_copy(g + r - Y_SLOTS).wait()

        packed = [compute_tile(g + r) for r in range(n)]
        for r in range(n):
            _store_packed(ybuf, 0, tm, packed[r], lead=(g + r) % Y_SLOTS)
        for r in range(n):
            y_copy(g + r).start(priority=1)

    @pl.when(e == 0)
    def _():
        for g in range(X_AHEAD):
            @pl.when(g < nv)
            def _():
                x_copy(g).start(priority=1)

    @pl.when(g1 > g0)
    def _():
        ws = e % W_SLOTS
        wg_b[...] = wg_f[ws].astype(BF16)
        wu_b[...] = wu_f[ws].astype(BF16)
        wd_b[...] = wd_f[ws].astype(BF16)
        n_tiles = g1 - g0

        def pair(p, c):
            run_tiles(g0 + 2 * p, 2)
            return c

        lax.fori_loop(0, n_tiles // 2, pair, 0)

        @pl.when(n_tiles % 2 == 1)
        def _():
            run_tiles(g1 - 1, 1)

    @pl.when(e == pl.num_programs(0) - 1)
    def _():
        for back in range(1, Y_SLOTS + 1):
            @pl.when(nv >= back)
            def _():
                y_copy(nv - back).wait()


def _expert_call(tile_start, tile_end, tile_rows, n_valid, xs, w_gate, w_up, w_down, n_rows):
    tm = TM_EXP
    hbm = pl.BlockSpec(memory_space=pl.ANY)
    return pl.pallas_call(
        _expert_kernel,
        grid_spec=pltpu.PrefetchScalarGridSpec(
            num_scalar_prefetch=4,
            grid=(N_EXPERTS,),
            in_specs=[hbm, hbm, hbm, hbm],
            out_specs=hbm,
            scratch_shapes=[pltpu.VMEM((X_SLOTS, tm * ROW_CHUNKS, LANES), U32),
                            pltpu.VMEM((Y_SLOTS, tm * ROW_CHUNKS, LANES), U32),
                            pltpu.VMEM((W_SLOTS, D_MODEL, EXPERT_FF), F32), pltpu.VMEM((W_SLOTS, D_MODEL, EXPERT_FF), F32),
                            pltpu.VMEM((W_SLOTS, EXPERT_FF, D_MODEL), F32),
                            pltpu.VMEM((D_MODEL, EXPERT_FF), BF16), pltpu.VMEM((D_MODEL, EXPERT_FF), BF16),
                            pltpu.VMEM((EXPERT_FF, D_MODEL), BF16),
                            pltpu.SemaphoreType.DMA((X_SLOTS,)), pltpu.SemaphoreType.DMA((Y_SLOTS,)),
                            pltpu.SemaphoreType.DMA((W_SLOTS,))],
        ),
        out_shape=SDS((n_rows * ROW_CHUNKS, LANES), U32),
        compiler_params=_cparams("arbitrary"),
        name="experts",
    )(tile_start, tile_end, tile_rows, n_valid, w_gate, w_up, w_down, xs)


COMB_SUB = 32


def _combine_kernel(wts_ref, h1_ref, g_ref, wsg_ref, wsu_ref, wsd_ref, g2_ref, b2_ref, o_ref, routed_ref):
    tn = h1_ref.shape[0]
    for s0 in range(0, tn, COMB_SUB):
        acc = [jnp.zeros((COMB_SUB, LANES), F32) for _ in range(2 * ROW_CHUNKS)]
        for kk in range(TOP_K):
            wk = jnp.broadcast_to(wts_ref[s0:s0 + COMB_SUB, kk:kk + 1], (COMB_SUB, LANES))
            for cc in range(ROW_CHUNKS):
                lo, hi = _unpack_rows(g_ref[kk, pl.ds(s0 * ROW_CHUNKS + cc, COMB_SUB, stride=ROW_CHUNKS), :])
                acc[cc] = acc[cc] + wk * lo
                acc[ROW_CHUNKS + cc] = acc[ROW_CHUNKS + cc] + wk * hi
        routed_ref[s0:s0 + COMB_SUB, :] = jnp.concatenate(acc, axis=1)

    h1 = h1_ref[...]
    hb = h1.astype(BF16)
    sg = jnp.dot(hb, wsg_ref[...], preferred_element_type=F32)
    su = jnp.dot(hb, wsu_ref[...], preferred_element_type=F32)
    ff = jnp.dot((sg * _sigmoid(sg) * su).astype(BF16), wsd_ref[...], preferred_element_type=F32)
    o_ref[...] = _layer_norm(ALPHA * h1 + ff + routed_ref[...], g2_ref[...], b2_ref[...])


def _combine_call(wts_t, h1, gathered, wsg, wsu, wsd, g2, b2):
    t = h1.shape[0]
    tn = TN_COMB
    row = lambda i: (i, 0)
    fix = lambda i: (0, 0)
    return pl.pallas_call(
        _combine_kernel,
        grid=(t // tn,),
        in_specs=[pl.BlockSpec((tn, TOP_K), row),
                  pl.BlockSpec((tn, D_MODEL), row),
                  pl.BlockSpec((TOP_K, tn * ROW_CHUNKS, LANES), lambda i: (0, i, 0)),
                  pl.BlockSpec((D_MODEL, SHARED_FF), fix), pl.BlockSpec((D_MODEL, SHARED_FF), fix),
                  pl.BlockSpec((SHARED_FF, D_MODEL), fix),
                  pl.BlockSpec((1, D_MODEL), fix), pl.BlockSpec((1, D_MODEL), fix)],
        out_specs=pl.BlockSpec((tn, D_MODEL), row),
        out_shape=SDS((t, D_MODEL), F32),
        scratch_shapes=[pltpu.VMEM((tn, D_MODEL), F32)],
        compiler_params=_cparams("arbitrary"),
        name="combine_ln2",
    )(wts_t, h1, gathered, wsg, wsu, wsd, g2, b2)


def kernel(x, meta_tokens, ln_in_g, ln_in_b, rel_bias, w_in, conv_w, conv_b, conv_ln_g, conv_ln_b, sinks,
           w_out, ln1_g, ln1_b, w_router, router_bias, w_gate, w_up, w_down, ws_gate, ws_up, ws_down,
           ln2_g, ln2_b):
    nbatch, seq, d = x.shape
    assert d == D_MODEL and seq % TQ_PROJ == 0 and w_in.shape[0] == DEPTH
    t = nbatch * seq
    x2d = x.reshape(t, D_MODEL)
    vec = lambda a: a.reshape(1, -1).astype(F32)
    gin, bin_ = vec(ln_in_g), vec(ln_in_b)
    w_in_b = w_in[0].astype(BF16)

    q, k, v, u = _proj_call(x2d, gin, bin_, w_in_b, TQ_PROJ)
    meta_blk = jnp.concatenate([jnp.zeros((PAD_FRONT, D_MODEL), F32), meta_tokens.astype(F32)], axis=0)
    _, k_meta, v_meta, u_meta = _proj_call(meta_blk, gin, bin_, w_in_b, BLOCK)

    attn = _attn_call(q, k, v, k_meta, v_meta, _rel_bias_table(rel_bias), sinks[0].astype(F32),
                      nbatch, seq // BLOCK)

    u_halo = jnp.concatenate([jnp.zeros((CONV_HALO - N_META, CONV_CH), F32), u_meta[PAD_FRONT:]], axis=0)
    conv = _conv_call(u, u_halo, conv_w[0].astype(F32), vec(conv_b[0]), vec(conv_ln_g[0]), vec(conv_ln_b[0]),
                      nbatch, seq)

    w_out_b = w_out[0].astype(BF16)
    wr_t = w_router[0].astype(F32).T
    wr_hi = wr_t.astype(BF16)
    wr_lo = (wr_t - wr_hi.astype(F32)).astype(BF16)
    h1, h1rows, logits = _mix_call(x2d, attn, conv, gin, bin_, w_out_b[:ATTN_W], w_out_b[ATTN_W:],
                                   vec(ln1_g[0]), vec(ln1_b[0]), wr_hi, wr_lo)

    idx, wts, rank, cnt = _route_call(logits, router_bias[0].astype(F32).reshape(N_EXPERTS, 1))

    tm = TM_EXP
    n_tiles = (t * TOP_K) // tm + N_EXPERTS
    counts = cnt[:, 0]
    tiles_e = (counts + tm - 1) // tm
    tile_end = jnp.cumsum(tiles_e).astype(I32)
    tile_start = (tile_end - tiles_e).astype(I32)
    offs = tile_start * tm
    tile_id = jnp.arange(n_tiles, dtype=I32)
    lo = jnp.maximum(tile_id[:, None] * tm, offs[None, :])
    hi = jnp.minimum((tile_id[:, None] + 1) * tm, (offs + counts)[None, :])
    tile_rows = jnp.sum(jnp.clip(hi - lo, 0, tm), axis=1).astype(I32)
    n_valid = tile_end[-1:]

    dest = _dest_call(idx, rank, offs.astype(F32).reshape(N_EXPERTS, 1))
    xs = _sc_dispatch_call(dest, h1rows.reshape(t, ROW_CHUNKS, LANES), n_tiles * tm)
    xs = xs.reshape(n_tiles * tm * ROW_CHUNKS, LANES)
    ys = _expert_call(tile_start, tile_end, tile_rows, n_valid, xs, w_gate[0], w_up[0], w_down[0], n_tiles * tm)
    gathered = _sc_gather_call(dest, ys.reshape(n_tiles * tm, ROW_CHUNKS, LANES))
    gathered = gathered.reshape(TOP_K, t * ROW_CHUNKS, LANES)
    out = _combine_call(wts.T, h1, gathered, ws_gate[0].astype(BF16), ws_up[0].astype(BF16),
                        ws_down[0].astype(BF16), vec(ln2_g[0]), vec(ln2_b[0]))
    return out.reshape(nbatch, seq, D_MODEL)
```

```python
import functools
import math

import numpy as np
import jax
import jax.numpy as jnp
from jax import lax
from jax.experimental import pallas as pl
from jax.experimental.pallas import tpu as pltpu
from jax.experimental.pallas import tpu_sc as plsc

F32 = jnp.float32
BF16 = jnp.bfloat16
I32 = jnp.int32
U32 = jnp.uint32
SDS = jax.ShapeDtypeStruct

D_MODEL = 1024
HALF = D_MODEL // 2
LANES = 128
SUBLANES = 8
ROW_CHUNKS = HALF // LANES
N_META = 16
HEAD_DIM = 64
N_Q_HEADS = 8
N_KV_HEADS = 2
GQA_GROUP = N_Q_HEADS // N_KV_HEADS
ATTN_W = N_Q_HEADS * HEAD_DIM
KV_W = N_KV_HEADS * HEAD_DIM
WINDOW = 128
BLOCK = 128
CONV_CH = D_MODEL - ATTN_W
CONV_K = 31
IN_W = ATTN_W + 2 * KV_W + 2 * CONV_CH
NUM_BUCKETS = 32
MAX_EXACT = NUM_BUCKETS // 2
REL_MAX_DIST = 128
N_EXPERTS = 256
TOP_K = 8
N_GROUPS = 8
GROUP_SIZE = N_EXPERTS // N_GROUPS
TOPK_GROUPS = 4
EXPERT_FF = 256
SHARED_FF = 256
ROUTED_SCALE = 2.5
DEPTH = 1
ALPHA = (2.0 * DEPTH) ** 0.25
LN_EPS = 1e-5
NEG = -1e30
PAD_FRONT = (-N_META) % BLOCK

VMEM_LIMIT = 48 * 1024 * 1024

TQ_PROJ = 1024
PROJ_CHAINS = 4
ATTN_QBLOCKS = 2
T_CONV = 256
CONV_HALO = 32
R_CONV = 64
TQ_MIX = 1024
MIX_CHAINS = 4
TN_ROUTE = 256
TM_EXP = 256
X_SLOTS = 8
X_AHEAD = 4
Y_SLOTS = 4
W_SLOTS = 3
W_AHEAD = 2
TN_COMB = 256


def _cparams(*sem):
    return pltpu.CompilerParams(dimension_semantics=sem, vmem_limit_bytes=VMEM_LIMIT)


def _layer_norm(x, g, b):
    mu = jnp.mean(x, axis=-1, keepdims=True)
    xc = x - mu
    var = jnp.mean(xc * xc, axis=-1, keepdims=True)
    return xc * lax.rsqrt(var + LN_EPS) * g + b


def _sigmoid(x):
    return 1.0 / (1.0 + jnp.exp(-x))


def _pack_rows(lo_half, hi_half):
    lo = lax.bitcast_convert_type(lo_half.astype(BF16).astype(F32), U32)
    hi = lax.bitcast_convert_type(hi_half.astype(BF16).astype(F32), U32)
    return lax.shift_right_logical(lo, jnp.uint32(16)) | hi


def _unpack_rows(p):
    lo = lax.bitcast_convert_type(lax.shift_left(p, jnp.uint32(16)), F32)
    hi = lax.bitcast_convert_type(p & jnp.uint32(0xFFFF0000), F32)
    return lo, hi


def _chunk_index(start, j, n, lead):
    rows = pl.ds(start + j, n, stride=ROW_CHUNKS)
    return (rows, slice(None)) if lead is None else (lead, rows, slice(None))


def _store_packed(ref, start, n, packed, lead=None):
    for j in range(ROW_CHUNKS):
        ref[_chunk_index(start, j, n, lead)] = packed[:, j * LANES:(j + 1) * LANES]


def _load_packed_bf16(ref, start, n, lead=None):
    halves = [_unpack_rows(ref[_chunk_index(start, j, n, lead)]) for j in range(ROW_CHUNKS)]
    return jnp.concatenate([h[0] for h in halves] + [h[1] for h in halves], axis=1).astype(BF16)


def _proj_kernel(chains, x_ref, g_ref, b_ref, w_ref, q_ref, k_ref, v_ref, u_ref):
    rows = x_ref.shape[0] // chains
    for c in range(chains):
        r = slice(c * rows, (c + 1) * rows)
        h = _layer_norm(x_ref[r, :], g_ref[...], b_ref[...])
        p = jnp.dot(h.astype(BF16), w_ref[...], preferred_element_type=F32)
        q_ref[r, :] = (p[:, :ATTN_W] * (HEAD_DIM ** -0.5)).astype(BF16)
        k_ref[r, :] = p[:, ATTN_W:ATTN_W + KV_W].astype(BF16)
        v_ref[r, :] = p[:, ATTN_W + KV_W:ATTN_W + 2 * KV_W].astype(BF16)
        a = p[:, ATTN_W + 2 * KV_W:ATTN_W + 2 * KV_W + CONV_CH]
        gate = p[:, ATTN_W + 2 * KV_W + CONV_CH:]
        u_ref[r, :] = a * _sigmoid(gate)


def _proj_call(x2d, gin, bin_, w_in_b, tq):
    t = x2d.shape[0]
    row = lambda i: (i, 0)
    fix = lambda i: (0, 0)
    chains = PROJ_CHAINS if tq % (PROJ_CHAINS * BLOCK) == 0 else 1
    return pl.pallas_call(
        functools.partial(_proj_kernel, chains),
        grid=(t // tq,),
        in_specs=[pl.BlockSpec((tq, D_MODEL), row), pl.BlockSpec((1, D_MODEL), fix),
                  pl.BlockSpec((1, D_MODEL), fix), pl.BlockSpec((D_MODEL, IN_W), fix)],
        out_specs=[pl.BlockSpec((tq, ATTN_W), row), pl.BlockSpec((tq, KV_W), row),
                   pl.BlockSpec((tq, KV_W), row), pl.BlockSpec((tq, CONV_CH), row)],
        out_shape=[SDS((t, ATTN_W), BF16), SDS((t, KV_W), BF16), SDS((t, KV_W), BF16), SDS((t, CONV_CH), F32)],
        compiler_params=_cparams("arbitrary"),
        name="ln_in_proj",
    )(x2d, gin, bin_, w_in_b)


def _attn_kernel(sinks_ref, q_ref, kc_ref, kp_ref, vc_ref, vp_ref, km_ref, vm_ref, bias_ref, o_ref):
    first = pl.program_id(1) == 0
    kp = jnp.where(first, km_ref[...], kp_ref[...])
    vp = jnp.where(first, vm_ref[...], vp_ref[...])
    k = jnp.concatenate([kp, kc_ref[...]], axis=0)
    v = jnp.concatenate([vp, vc_ref[...]], axis=0)
    col = lax.broadcasted_iota(I32, (BLOCK, 2 * BLOCK), 1)
    pad_bias = jnp.where(jnp.logical_and(first, col < PAD_FRONT), NEG, 0.0).astype(F32)
    for a in range(ATTN_QBLOCKS):
        q = q_ref[a * BLOCK:(a + 1) * BLOCK, :]
        kw = k[a * BLOCK:(a + 2) * BLOCK, :]
        vw = v[a * BLOCK:(a + 2) * BLOCK, :]
        outs = []
        for h in range(N_Q_HEADS):
            g = h // GQA_GROUP
            qh = q[:, h * HEAD_DIM:(h + 1) * HEAD_DIM]
            kg = kw[:, g * HEAD_DIM:(g + 1) * HEAD_DIM]
            vg = vw[:, g * HEAD_DIM:(g + 1) * HEAD_DIM]
            s = lax.dot_general(qh, kg, (((1,), (1,)), ((), ())), preferred_element_type=F32)
            s = s + bias_ref[h]
            if a == 0:
                s = s + pad_bias
            sink = sinks_ref[h]
            m = jnp.maximum(jnp.max(s, axis=-1, keepdims=True), sink)
            p = jnp.exp(s - m)
            den = jnp.sum(p, axis=-1, keepdims=True) + jnp.exp(sink - m)
            o = jnp.dot(p.astype(BF16), vg, preferred_element_type=F32)
            outs.append(o / den)
        o_ref[a * BLOCK:(a + 1) * BLOCK, :] = jnp.concatenate(outs, axis=1).astype(BF16)


def _attn_call(q, k, v, k_meta, v_meta, bias, sinks, nbatch, nblk):
    t = q.shape[0]
    nq = ATTN_QBLOCKS
    assert nblk % nq == 0
    nstep = nblk // nq
    cur = lambda b, j: (b * nstep + j, 0)
    prev = lambda b, j: (jnp.maximum((b * nstep + j) * nq - 1, 0), 0)
    fix2 = lambda b, j: (0, 0)
    return pl.pallas_call(
        _attn_kernel,
        grid=(nbatch, nstep),
        in_specs=[pl.BlockSpec(memory_space=pltpu.SMEM),
                  pl.BlockSpec((nq * BLOCK, ATTN_W), cur),
                  pl.BlockSpec((nq * BLOCK, KV_W), cur), pl.BlockSpec((BLOCK, KV_W), prev),
                  pl.BlockSpec((nq * BLOCK, KV_W), cur), pl.BlockSpec((BLOCK, KV_W), prev),
                  pl.BlockSpec((BLOCK, KV_W), fix2), pl.BlockSpec((BLOCK, KV_W), fix2),
                  pl.BlockSpec((N_Q_HEADS, BLOCK, 2 * BLOCK), lambda b, j: (0, 0, 0))],
        out_specs=pl.BlockSpec((nq * BLOCK, ATTN_W), cur),
        out_shape=SDS((t, ATTN_W), BF16),
        compiler_params=_cparams("arbitrary", "arbitrary"),
        name="swa_attn",
    )(sinks, q, k, k, v, v, k_meta, v_meta, bias)


def _rel_bias_table(rel_bias):
    qi = np.arange(BLOCK, dtype=np.int32)[:, None]
    kj = np.arange(2 * BLOCK, dtype=np.int32)[None, :]
    dist = BLOCK + qi - kj
    dc = np.clip(dist, 0, WINDOW - 1)
    nf = np.maximum(dc, 1).astype(np.float32)
    large = MAX_EXACT + (np.log(nf / np.float32(MAX_EXACT)) / np.float32(math.log(REL_MAX_DIST / MAX_EXACT))
                         * np.float32(NUM_BUCKETS - MAX_EXACT)).astype(np.int32)
    large = np.minimum(large, NUM_BUCKETS - 1)
    bucket = np.where(dc < MAX_EXACT, dc, large)
    in_window = (dist >= 0) & (dist < WINDOW)
    onehot = (bucket.reshape(-1, 1) == np.arange(NUM_BUCKETS)[None, :]).astype(np.float32)
    bias = jnp.dot(jnp.asarray(onehot), rel_bias.astype(F32), precision=lax.Precision.HIGHEST)
    bias = jnp.transpose(bias.reshape(BLOCK, 2 * BLOCK, N_Q_HEADS), (2, 0, 1))
    return jnp.where(in_window[None], bias, NEG)


def _conv_kernel(uc_ref, up_ref, um_ref, w_ref, cb_ref, g_ref, b_ref, o_ref, s_ref, sh_ref):
    first = pl.program_id(1) == 0
    s_ref[0:CONV_HALO, :] = jnp.where(first, um_ref[...], up_ref[...])
    s_ref[CONV_HALO:CONV_HALO + T_CONV, :] = uc_ref[...]
    off = CONV_HALO - (CONV_K - 1)
    span = sh_ref.shape[1]
    for p in range(1, SUBLANES):
        sh_ref[p] = s_ref[p:p + span, :]
    for c in range(0, T_CONV, R_CONV):
        acc = jnp.zeros((R_CONV, CONV_CH), F32) + cb_ref[...]
        for kk in range(CONV_K):
            p, a = (off + kk) % SUBLANES, (off + kk) // SUBLANES * SUBLANES
            if p == 0:
                win = s_ref[c + a:c + a + R_CONV, :]
            else:
                win = sh_ref[p, c + a:c + a + R_CONV, :]
            acc = acc + win * w_ref[kk:kk + 1, :]
        y = _layer_norm(acc, g_ref[...], b_ref[...])
        o_ref[c:c + R_CONV, :] = (y * _sigmoid(y)).astype(BF16)


def _conv_call(u, u_meta_halo, conv_w, conv_b, g, b, nbatch, seq):
    t = u.shape[0]
    nj = seq // T_CONV
    per = T_CONV // CONV_HALO
    cur = lambda bb, j: (bb * nj + j, 0)
    prev = lambda bb, j: (jnp.maximum((bb * nj + j) * per - 1, 0), 0)
    fix = lambda bb, j: (0, 0)
    return pl.pallas_call(
        _conv_kernel,
        grid=(nbatch, nj),
        in_specs=[pl.BlockSpec((T_CONV, CONV_CH), cur), pl.BlockSpec((CONV_HALO, CONV_CH), prev),
                  pl.BlockSpec((CONV_HALO, CONV_CH), fix), pl.BlockSpec((CONV_K, CONV_CH), fix),
                  pl.BlockSpec((1, CONV_CH), fix), pl.BlockSpec((1, CONV_CH), fix), pl.BlockSpec((1, CONV_CH), fix)],
        out_specs=pl.BlockSpec((T_CONV, CONV_CH), cur),
        out_shape=SDS((t, CONV_CH), BF16),
        scratch_shapes=[pltpu.VMEM((CONV_HALO + T_CONV, CONV_CH), F32),
                        pltpu.VMEM((SUBLANES, T_CONV + CONV_HALO - SUBLANES, CONV_CH), F32)],
        compiler_params=_cparams("arbitrary", "arbitrary"),
        name="conv_ln",
    )(u, u, u_meta_halo, conv_w, conv_b, g, b)


def _mix_kernel(x_ref, at_ref, cv_ref, gin_ref, bin_ref, woa_ref, woc_ref, g1_ref, b1_ref,
                wrh_ref, wrl_ref, h1_ref, h1r_ref, lg_ref):
    rows = x_ref.shape[0] // MIX_CHAINS
    nt = (((1,), (1,)), ((), ()))
    for c in range(MIX_CHAINS):
        r = slice(c * rows, (c + 1) * rows)
        h = _layer_norm(x_ref[r, :], gin_ref[...], bin_ref[...])
        mix = (jnp.dot(at_ref[r, :], woa_ref[...], preferred_element_type=F32)
               + jnp.dot(cv_ref[r, :], woc_ref[...], preferred_element_type=F32))
        h1 = _layer_norm(ALPHA * h + mix, g1_ref[...], b1_ref[...])
        h1_ref[r, :] = h1
        _store_packed(h1r_ref, c * rows * ROW_CHUNKS, rows, _pack_rows(h1[:, :HALF], h1[:, HALF:]))
        hh = h1.astype(BF16)
        hl = (h1 - hh.astype(F32)).astype(BF16)
        lg = lax.dot_general(wrh_ref[...], hh, nt, preferred_element_type=F32)
        lg = lg + lax.dot_general(wrh_ref[...], hl, nt, preferred_element_type=F32)
        lg = lg + lax.dot_general(wrl_ref[...], hh, nt, preferred_element_type=F32)
        lg_ref[:, r] = lg


def _mix_call(x2d, attn, conv, gin, bin_, woa, woc, g1, b1, wrh, wrl):
    t = x2d.shape[0]
    tq = TQ_MIX
    row = lambda i: (i, 0)
    fix = lambda i: (0, 0)
    return pl.pallas_call(
        _mix_kernel,
        grid=(t // tq,),
        in_specs=[pl.BlockSpec((tq, D_MODEL), row), pl.BlockSpec((tq, ATTN_W), row), pl.BlockSpec((tq, CONV_CH), row),
                  pl.BlockSpec((1, D_MODEL), fix), pl.BlockSpec((1, D_MODEL), fix),
                  pl.BlockSpec((ATTN_W, D_MODEL), fix), pl.BlockSpec((CONV_CH, D_MODEL), fix),
                  pl.BlockSpec((1, D_MODEL), fix), pl.BlockSpec((1, D_MODEL), fix),
                  pl.BlockSpec((N_EXPERTS, D_MODEL), fix), pl.BlockSpec((N_EXPERTS, D_MODEL), fix)],
        out_specs=[pl.BlockSpec((tq, D_MODEL), row), pl.BlockSpec((tq * ROW_CHUNKS, LANES), row),
                   pl.BlockSpec((N_EXPERTS, tq), lambda i: (0, i))],
        out_shape=[SDS((t, D_MODEL), F32), SDS((t * ROW_CHUNKS, LANES), U32), SDS((N_EXPERTS, t), F32)],
        compiler_params=_cparams("arbitrary"),
        name="mix_ln1",
    )(x2d, attn, conv, gin, bin_, woa, woc, g1, b1, wrh, wrl)


def _first_argmax(x, rows, nrows):
    m = jnp.max(x, axis=0, keepdims=True)
    idx = jnp.min(jnp.where(x == m, rows, nrows), axis=0, keepdims=True)
    return m, idx


def _route_kernel(lg_ref, rb_ref, idx_ref, wts_ref, rank_ref, cnt_ref, carry_ref):
    tn = lg_ref.shape[1]

    @pl.when(pl.program_id(0) == 0)
    def _():
        carry_ref[...] = jnp.zeros_like(carry_ref)

    scores = _sigmoid(lg_ref[...])
    choice = scores + rb_ref[...]
    rows = lax.broadcasted_iota(I32, (N_EXPERTS, tn), 0)
    rows_g = lax.broadcasted_iota(I32, (GROUP_SIZE, tn), 0)
    rows_8 = lax.broadcasted_iota(I32, (N_GROUPS, tn), 0)

    gs = []
    for g in range(N_GROUPS):
        xg = choice[g * GROUP_SIZE:(g + 1) * GROUP_SIZE, :]
        m1, i1 = _first_argmax(xg, rows_g, GROUP_SIZE)
        m2 = jnp.max(jnp.where(rows_g == i1, -jnp.inf, xg), axis=0, keepdims=True)
        gs.append(m1 + m2)
    gsc = jnp.concatenate(gs, axis=0)
    gsel = jnp.zeros((N_GROUPS, tn), F32)
    for _ in range(TOPK_GROUPS):
        _, gi = _first_argmax(gsc, rows_8, N_GROUPS)
        hit = rows_8 == gi
        gsel = jnp.where(hit, 1.0, gsel)
        gsc = jnp.where(hit, -jnp.inf, gsc)
    emask = jnp.concatenate(
        [jnp.broadcast_to(gsel[g:g + 1, :], (GROUP_SIZE, tn)) for g in range(N_GROUPS)], axis=0)
    masked = jnp.where(emask > 0.5, choice, NEG)

    sel_all = jnp.zeros((N_EXPERTS, tn), F32)
    hits, idxs, ws = [], [], []
    for _ in range(TOP_K):
        _, ii = _first_argmax(masked, rows, N_EXPERTS)
        hit = rows == ii
        hits.append(hit)
        idxs.append(ii)
        ws.append(jnp.sum(jnp.where(hit, scores, 0.0), axis=0, keepdims=True))
        sel_all = jnp.where(hit, 1.0, sel_all)
        masked = jnp.where(hit, -jnp.inf, masked)
    wsum = ws[0]
    for w in ws[1:]:
        wsum = wsum + w
    idx_ref[...] = jnp.concatenate(idxs, axis=0)
    wts_ref[...] = jnp.concatenate([w / wsum * ROUTED_SCALE for w in ws], axis=0)

    r_i = lax.broadcasted_iota(I32, (tn, tn), 0)
    c_i = lax.broadcasted_iota(I32, (tn, tn), 1)
    upper = jnp.where(r_i < c_i, 1.0, 0.0).astype(BF16)
    sel_b = sel_all.astype(BF16)
    carry = carry_ref[...]
    before = jnp.dot(sel_b, upper, preferred_element_type=F32)
    before = before + jnp.concatenate([carry] * (tn // 128), axis=1)
    rank_ref[...] = jnp.concatenate(
        [jnp.sum(jnp.where(h, before, 0.0), axis=0, keepdims=True) for h in hits], axis=0).astype(I32)
    carry = carry + jnp.dot(sel_b, jnp.ones((tn, 128), BF16), preferred_element_type=F32)
    carry_ref[...] = carry
    cnt_ref[...] = carry.astype(I32)


def _route_call(lg, rbias):
    t = lg.shape[1]
    tn = TN_ROUTE
    col = lambda i: (0, i)
    return pl.pallas_call(
        _route_kernel,
        grid=(t // tn,),
        in_specs=[pl.BlockSpec((N_EXPERTS, tn), col), pl.BlockSpec((N_EXPERTS, 1), lambda i: (0, 0))],
        out_specs=[pl.BlockSpec((TOP_K, tn), col), pl.BlockSpec((TOP_K, tn), col), pl.BlockSpec((TOP_K, tn), col),
                   pl.BlockSpec((N_EXPERTS, 128), lambda i: (0, 0))],
        out_shape=[SDS((TOP_K, t), I32), SDS((TOP_K, t), F32), SDS((TOP_K, t), I32), SDS((N_EXPERTS, 128), I32)],
        scratch_shapes=[pltpu.VMEM((N_EXPERTS, 128), F32)],
        compiler_params=_cparams("arbitrary"),
        name="route",
    )(lg, rbias)


def _dest_kernel(idx_ref, rank_ref, offs_ref, dest_ref):
    tn = idx_ref.shape[1]
    rows = lax.broadcasted_iota(I32, (N_EXPERTS, tn), 0)
    offs = offs_ref[...]
    out = []
    for kk in range(TOP_K):
        hit = rows == idx_ref[kk:kk + 1, :]
        out.append(jnp.sum(jnp.where(hit, offs, 0.0), axis=0, keepdims=True))
    dest_ref[...] = jnp.concatenate(out, axis=0).astype(I32) + rank_ref[...]


def _dest_call(idx, rank, offs_col):
    t = idx.shape[1]
    tn = TN_ROUTE
    col = lambda i: (0, i)
    return pl.pallas_call(
        _dest_kernel,
        grid=(t // tn,),
        in_specs=[pl.BlockSpec((TOP_K, tn), col), pl.BlockSpec((TOP_K, tn), col),
                  pl.BlockSpec((N_EXPERTS, 1), lambda i: (0, 0))],
        out_specs=pl.BlockSpec((TOP_K, tn), col),
        out_shape=SDS((TOP_K, t), I32),
        compiler_params=_cparams("arbitrary"),
        name="dest",
    )(idx, rank, offs_col)


SC_CORES = 2
SC_SUBCORES = 16
SC_CHUNK = 128


def _sc_worker_chunks(t):
    per_worker = t // (SC_CORES * SC_SUBCORES)
    assert per_worker % SC_CHUNK == 0
    return per_worker


def _sc_dispatch_call(dest, h1rows3, n_rows):
    per_worker = _sc_worker_chunks(dest.shape[1])
    mesh = plsc.VectorSubcoreMesh(core_axis_name="c", subcore_axis_name="s")

    @functools.partial(
        pl.kernel, mesh=mesh, out_type=SDS((n_rows, ROW_CHUNKS, LANES), U32),
        scratch_types=[pltpu.VMEM((TOP_K, SC_CHUNK), I32), pltpu.VMEM((SC_CHUNK, ROW_CHUNKS, LANES), U32),
                       pltpu.SemaphoreType.DMA],
        name="sc_dispatch")
    def body(h_hbm, dest_hbm, xs_hbm, idx_v, rows_v, sem):
        wid = lax.axis_index("s") * SC_CORES + lax.axis_index("c")

        @pl.loop(0, per_worker // SC_CHUNK)
        def _(i):
            t0 = wid * per_worker + i * SC_CHUNK
            pltpu.sync_copy(dest_hbm.at[:, pl.ds(t0, SC_CHUNK)], idx_v)
            pltpu.sync_copy(h_hbm.at[pl.ds(t0, SC_CHUNK)], rows_v)
            copies = [pltpu.async_copy(rows_v, xs_hbm.at[idx_v.at[kk]], sem) for kk in range(TOP_K)]
            for c in copies:
                c.wait()

    return body(h1rows3, dest)


def _sc_gather_call(dest, ys3):
    t = dest.shape[1]
    per_worker = _sc_worker_chunks(t)
    mesh = plsc.VectorSubcoreMesh(core_axis_name="c", subcore_axis_name="s")

    @functools.partial(
        pl.kernel, mesh=mesh, out_type=SDS((TOP_K * t, ROW_CHUNKS, LANES), U32),
        scratch_types=[pltpu.VMEM((TOP_K, SC_CHUNK), I32), pltpu.VMEM((SC_CHUNK, ROW_CHUNKS, LANES), U32),
                       pltpu.SemaphoreType.DMA],
        name="sc_gather")
    def body(ys_hbm, dest_hbm, out_hbm, idx_v, rows_v, sem):
        wid = lax.axis_index("s") * SC_CORES + lax.axis_index("c")

        @pl.loop(0, per_worker // SC_CHUNK)
        def _(i):
            t0 = wid * per_worker + i * SC_CHUNK
            pltpu.sync_copy(dest_hbm.at[:, pl.ds(t0, SC_CHUNK)], idx_v)
            for kk in range(TOP_K):
                pltpu.async_copy(ys_hbm.at[idx_v.at[kk]], rows_v, sem).wait()
                pltpu.sync_copy(rows_v, out_hbm.at[pl.ds(kk * t + t0, SC_CHUNK)])

    return body(ys3, dest)


def _expert_kernel(ts_ref, te_ref, tr_ref, nv_ref, wg_hbm, wu_hbm, wd_hbm, xs_hbm, ys_hbm,
                   xbuf, ybuf, wg_f, wu_f, wd_f, wg_b, wu_b, wd_b, xsem, ysem, wsem):
    e = pl.program_id(0)
    rows = xbuf.shape[1]
    tm = rows // ROW_CHUNKS
    g0, g1, nv = ts_ref[e], te_ref[e], nv_ref[0]

    def x_copy(g):
        s = g % X_SLOTS
        return pltpu.make_async_copy(xs_hbm.at[pl.ds(pl.multiple_of(g * rows, rows), rows), :], xbuf.at[s], xsem.at[s])

    def y_copy(g):
        s = g % Y_SLOTS
        return pltpu.make_async_copy(ybuf.at[s], ys_hbm.at[pl.ds(pl.multiple_of(g * rows, rows), rows), :], ysem.at[s])

    def w_copies(ex):
        s = ex % W_SLOTS
        return (pltpu.make_async_copy(wg_hbm.at[ex], wg_f.at[s], wsem.at[s]),
                pltpu.make_async_copy(wu_hbm.at[ex], wu_f.at[s], wsem.at[s]),
                pltpu.make_async_copy(wd_hbm.at[ex], wd_f.at[s], wsem.at[s]))

    n_exp = pl.num_programs(0)

    @pl.when(e == 0)
    def _():
        for ex in range(W_AHEAD):
            for c in w_copies(ex):
                c.start()

    @pl.when(e + W_AHEAD < n_exp)
    def _():
        for c in w_copies(e + W_AHEAD):
            c.start()

    for c in w_copies(e):
        c.wait()

    def compute_tile(g):
        x = _load_packed_bf16(xbuf, 0, tm, lead=g % X_SLOTS)
        gate = jnp.dot(x, wg_b[...], preferred_element_type=F32)
        up = jnp.dot(x, wu_b[...], preferred_element_type=F32)
        live = lax.broadcasted_iota(I32, (tm, EXPERT_FF), 0) < tr_ref[g]
        hid = jnp.where(live, gate * _sigmoid(gate) * up, 0.0).astype(BF16)
        y = jnp.dot(hid, wd_b[...], preferred_element_type=F32)
        return _pack_rows(y[:, :HALF], y[:, HALF:])

    def run_tiles(g, n):
        for r in range(n):
            x_copy(g + r).wait()

            @pl.when(g + r + X_AHEAD < nv)
            def _():
                x_copy(g + r + X_AHEAD).start(priority=1)

            @pl.when(g + r >= Y_SLOTS)
            def _():
                y_copy(g + r - Y_SLOTS).wait()

        packed = [compute_tile(g + r) for r in range(n)]
        for r in range(n):
            _store_packed(ybuf, 0, tm, packed[r], lead=(g + r) % Y_SLOTS)
        for r in range(n):
            y_copy(g + r).start(priority=1)

    @pl.when(e == 0)
    def _():
        for g in range(X_AHEAD):
            @pl.when(g < nv)
            def _():
                x_copy(g).start(priority=1)

    @pl.when(g1 > g0)
    def _():
        ws = e % W_SLOTS
        wg_b[...] = wg_f[ws].astype(BF16)
        wu_b[...] = wu_f[ws].astype(BF16)
        wd_b[...] = wd_f[ws].astype(BF16)
        n_tiles = g1 - g0

        def pair(p, c):
            run_tiles(g0 + 2 * p, 2)
            return c

        lax.fori_loop(0, n_tiles // 2, pair, 0)

        @pl.when(n_tiles % 2 == 1)
        def _():
            run_tiles(g1 - 1, 1)

    @pl.when(e == pl.num_programs(0) - 1)
    def _():
        for back in range(1, Y_SLOTS + 1):
            @pl.when(nv >= back)
            def _():
                y_copy(nv - back).wait()


def _expert_call(tile_start, tile_end, tile_rows, n_valid, xs, w_gate, w_up, w_down, n_rows):
    tm = TM_EXP
    hbm = pl.BlockSpec(memory_space=pl.ANY)
    return pl.pallas_call(
        _expert_kernel,
        grid_spec=pltpu.PrefetchScalarGridSpec(
            num_scalar_prefetch=4,
            grid=(N_EXPERTS,),
            in_specs=[hbm, hbm, hbm, hbm],
            out_specs=hbm,
            scratch_shapes=[pltpu.VMEM((X_SLOTS, tm * ROW_CHUNKS, LANES), U32),
                            pltpu.VMEM((Y_SLOTS, tm * ROW_CHUNKS, LANES), U32),
                            pltpu.VMEM((W_SLOTS, D_MODEL, EXPERT_FF), F32), pltpu.VMEM((W_SLOTS, D_MODEL, EXPERT_FF), F32),
                            pltpu.VMEM((W_SLOTS, EXPERT_FF, D_MODEL), F32),
                            pltpu.VMEM((D_MODEL, EXPERT_FF), BF16), pltpu.VMEM((D_MODEL, EXPERT_FF), BF16),
                            pltpu.VMEM((EXPERT_FF, D_MODEL), BF16),
                            pltpu.SemaphoreType.DMA((X_SLOTS,)), pltpu.SemaphoreType.DMA((Y_SLOTS,)),
                            pltpu.SemaphoreType.DMA((W_SLOTS,))],
        ),
        out_shape=SDS((n_rows * ROW_CHUNKS, LANES), U32),
        compiler_params=_cparams("arbitrary"),
        name="experts",
    )(tile_start, tile_end, tile_rows, n_valid, w_gate, w_up, w_down, xs)


COMB_SUB = 32


def _combine_kernel(wts_ref, h1_ref, g_ref, wsg_ref, wsu_ref, wsd_ref, g2_ref, b2_ref, o_ref, routed_ref):
    tn = h1_ref.shape[0]
    for s0 in range(0, tn, COMB_SUB):
        acc = [jnp.zeros((COMB_SUB, LANES), F32) for _ in range(2 * ROW_CHUNKS)]
        for kk in range(TOP_K):
            wk = jnp.broadcast_to(wts_ref[s0:s0 + COMB_SUB, kk:kk + 1], (COMB_SUB, LANES))
            for cc in range(ROW_CHUNKS):
                lo, hi = _unpack_rows(g_ref[kk, pl.ds(s0 * ROW_CHUNKS + cc, COMB_SUB, stride=ROW_CHUNKS), :])
                acc[cc] = acc[cc] + wk * lo
                acc[ROW_CHUNKS + cc] = acc[ROW_CHUNKS + cc] + wk * hi
        routed_ref[s0:s0 + COMB_SUB, :] = jnp.concatenate(acc, axis=1)

    h1 = h1_ref[...]
    hb = h1.astype(BF16)
    sg = jnp.dot(hb, wsg_ref[...], preferred_element_type=F32)
    su = jnp.dot(hb, wsu_ref[...], preferred_element_type=F32)
    ff = jnp.dot((sg * _sigmoid(sg) * su).astype(BF16), wsd_ref[...], preferred_element_type=F32)
    o_ref[...] = _layer_norm(ALPHA * h1 + ff + routed_ref[...], g2_ref[...], b2_ref[...])


def _combine_call(wts_t, h1, gathered, wsg, wsu, wsd, g2, b2):
    t = h1.shape[0]
    tn = TN_COMB
    row = lambda i: (i, 0)
    fix = lambda i: (0, 0)
    return pl.pallas_call(
        _combine_kernel,
        grid=(t // tn,),
        in_specs=[pl.BlockSpec((tn, TOP_K), row),
                  pl.BlockSpec((tn, D_MODEL), row),
                  pl.BlockSpec((TOP_K, tn * ROW_CHUNKS, LANES), lambda i: (0, i, 0)),
                  pl.BlockSpec((D_MODEL, SHARED_FF), fix), pl.BlockSpec((D_MODEL, SHARED_FF), fix),
                  pl.BlockSpec((SHARED_FF, D_MODEL), fix),
                  pl.BlockSpec((1, D_MODEL), fix), pl.BlockSpec((1, D_MODEL), fix)],
        out_specs=pl.BlockSpec((tn, D_MODEL), row),
        out_shape=SDS((t, D_MODEL), F32),
        scratch_shapes=[pltpu.VMEM((tn, D_MODEL), F32)],
        compiler_params=_cparams("arbitrary"),
        name="combine_ln2",
    )(wts_t, h1, gathered, wsg, wsu, wsd, g2, b2)


def kernel(x, meta_tokens, ln_in_g, ln_in_b, rel_bias, w_in, conv_w, conv_b, conv_ln_g, conv_ln_b, sinks,
           w_out, ln1_g, ln1_b, w_router, router_bias, w_gate, w_up, w_down, ws_gate, ws_up, ws_down,
           ln2_g, ln2_b):
    nbatch, seq, d = x.shape
    t = nbatch * seq
    assert d == D_MODEL and w_in.shape[0] == DEPTH
    assert seq % (ATTN_QBLOCKS * BLOCK) == 0 and seq % T_CONV == 0
    assert all(t % tile == 0 for tile in (TQ_PROJ, TQ_MIX, TN_ROUTE, TN_COMB))
    x2d = x.reshape(t, D_MODEL)
    vec = lambda a: a.reshape(1, -1).astype(F32)
    gin, bin_ = vec(ln_in_g), vec(ln_in_b)
    w_in_b = w_in[0].astype(BF16)

    q, k, v, u = _proj_call(x2d, gin, bin_, w_in_b, TQ_PROJ)
    meta_blk = jnp.concatenate([jnp.zeros((PAD_FRONT, D_MODEL), F32), meta_tokens.astype(F32)], axis=0)
    _, k_meta, v_meta, u_meta = _proj_call(meta_blk, gin, bin_, w_in_b, BLOCK)

    attn = _attn_call(q, k, v, k_meta, v_meta, _rel_bias_table(rel_bias), sinks[0].astype(F32),
                      nbatch, seq // BLOCK)

    u_halo = jnp.concatenate([jnp.zeros((CONV_HALO - N_META, CONV_CH), F32), u_meta[PAD_FRONT:]], axis=0)
    conv = _conv_call(u, u_halo, conv_w[0].astype(F32), vec(conv_b[0]), vec(conv_ln_g[0]), vec(conv_ln_b[0]),
                      nbatch, seq)

    w_out_b = w_out[0].astype(BF16)
    wr_t = w_router[0].astype(F32).T
    wr_hi = wr_t.astype(BF16)
    wr_lo = (wr_t - wr_hi.astype(F32)).astype(BF16)
    h1, h1rows, logits = _mix_call(x2d, attn, conv, gin, bin_, w_out_b[:ATTN_W], w_out_b[ATTN_W:],
                                   vec(ln1_g[0]), vec(ln1_b[0]), wr_hi, wr_lo)

    idx, wts, rank, cnt = _route_call(logits, router_bias[0].astype(F32).reshape(N_EXPERTS, 1))

    tm = TM_EXP
    n_tiles = (t * TOP_K) // tm + N_EXPERTS
    counts = cnt[:, 0]
    tiles_e = (counts + tm - 1) // tm
    tile_end = jnp.cumsum(tiles_e).astype(I32)
    tile_start = (tile_end - tiles_e).astype(I32)
    offs = tile_start * tm
    tile_id = jnp.arange(n_tiles, dtype=I32)
    lo = jnp.maximum(tile_id[:, None] * tm, offs[None, :])
    hi = jnp.minimum((tile_id[:, None] + 1) * tm, (offs + counts)[None, :])
    tile_rows = jnp.sum(jnp.clip(hi - lo, 0, tm), axis=1).astype(I32)
    n_valid = tile_end[-1:]

    dest = _dest_call(idx, rank, offs.astype(F32).reshape(N_EXPERTS, 1))
    xs = _sc_dispatch_call(dest, h1rows.reshape(t, ROW_CHUNKS, LANES), n_tiles * tm)
    xs = xs.reshape(n_tiles * tm * ROW_CHUNKS, LANES)
    ys = _expert_call(tile_start, tile_end, tile_rows, n_valid, xs, w_gate[0], w_up[0], w_down[0], n_tiles * tm)
    gathered = _sc_gather_call(dest, ys.reshape(n_tiles * tm, ROW_CHUNKS, LANES))
    gathered = gathered.reshape(TOP_K, t * ROW_CHUNKS, LANES)
    out = _combine_call(wts.T, h1, gathered, ws_gate[0].astype(BF16), ws_up[0].astype(BF16),
                        ws_down[0].astype(BF16), vec(ln2_g[0]), vec(ln2_b[0]))
    return out.reshape(nbatch, seq, D_MODEL)
```

```python
import functools
import math

import numpy as np
import jax
import jax.numpy as jnp
from jax import lax
from jax.experimental import pallas as pl
from jax.experimental.pallas import tpu as pltpu
from jax.experimental.pallas import tpu_sc as plsc

F32 = jnp.float32
BF16 = jnp.bfloat16
I32 = jnp.int32
U32 = jnp.uint32
SDS = jax.ShapeDtypeStruct

D_MODEL = 1024
HALF = D_MODEL // 2
LANES = 128
SUBLANES = 8
ROW_CHUNKS = HALF // LANES
N_META = 16
HEAD_DIM = 64
N_Q_HEADS = 8
N_KV_HEADS = 2
GQA_GROUP = N_Q_HEADS // N_KV_HEADS
ATTN_W = N_Q_HEADS * HEAD_DIM
KV_W = N_KV_HEADS * HEAD_DIM
WINDOW = 128
BLOCK = 128
CONV_CH = D_MODEL - ATTN_W
CONV_K = 31
IN_W = ATTN_W + 2 * KV_W + 2 * CONV_CH
NUM_BUCKETS = 32
MAX_EXACT = NUM_BUCKETS // 2
REL_MAX_DIST = 128
N_EXPERTS = 256
TOP_K = 8
N_GROUPS = 8
GROUP_SIZE = N_EXPERTS // N_GROUPS
TOPK_GROUPS = 4
EXPERT_FF = 256
SHARED_FF = 256
ROUTED_SCALE = 2.5
DEPTH = 1
ALPHA = (2.0 * DEPTH) ** 0.25
LN_EPS = 1e-5
NEG = -1e30
PAD_FRONT = (-N_META) % BLOCK

VMEM_LIMIT = 48 * 1024 * 1024

TQ_PROJ = 1024
PROJ_CHAINS = 4
ATTN_QBLOCKS = 2
T_CONV = 256
CONV_HALO = 32
R_CONV = 64
TQ_MIX = 1024
MIX_CHAINS = 4
TN_ROUTE = 256
TM_EXP = 256
X_SLOTS = 8
X_AHEAD = 4
Y_SLOTS = 4
W_SLOTS = 3
W_AHEAD = 2
TN_COMB = 256


def _cparams(*sem):
    return pltpu.CompilerParams(dimension_semantics=sem, vmem_limit_bytes=VMEM_LIMIT)


def _layer_norm(x, g, b):
    mu = jnp.mean(x, axis=-1, keepdims=True)
    xc = x - mu
    var = jnp.mean(xc * xc, axis=-1, keepdims=True)
    return xc * lax.rsqrt(var + LN_EPS) * g + b


def _sigmoid(x):
    return 1.0 / (1.0 + jnp.exp(-x))


def _pack_rows(lo_half, hi_half):
    lo = lax.bitcast_convert_type(lo_half.astype(BF16).astype(F32), U32)
    hi = lax.bitcast_convert_type(hi_half.astype(BF16).astype(F32), U32)
    return lax.shift_right_logical(lo, jnp.uint32(16)) | hi


def _unpack_rows(p):
    lo = lax.bitcast_convert_type(lax.shift_left(p, jnp.uint32(16)), F32)
    hi = lax.bitcast_convert_type(p & jnp.uint32(0xFFFF0000), F32)
    return lo, hi


def _chunk_index(start, j, n, lead):
    rows = pl.ds(start + j, n, stride=ROW_CHUNKS)
    return (rows, slice(None)) if lead is None else (lead, rows, slice(None))


def _store_packed(ref, start, n, packed, lead=None):
    for j in range(ROW_CHUNKS):
        ref[_chunk_index(start, j, n, lead)] = packed[:, j * LANES:(j + 1) * LANES]


def _load_packed_bf16(ref, start, n, lead=None):
    halves = [_unpack_rows(ref[_chunk_index(start, j, n, lead)]) for j in range(ROW_CHUNKS)]
    return jnp.concatenate([h[0] for h in halves] + [h[1] for h in halves], axis=1).astype(BF16)


def _proj_kernel(chains, x_ref, g_ref, b_ref, w_ref, q_ref, k_ref, v_ref, u_ref):
    rows = x_ref.shape[0] // chains
    for c in range(chains):
        r = slice(c * rows, (c + 1) * rows)
        h = _layer_norm(x_ref[r, :], g_ref[...], b_ref[...])
        p = jnp.dot(h.astype(BF16), w_ref[...], preferred_element_type=F32)
        q_ref[r, :] = (p[:, :ATTN_W] * (HEAD_DIM ** -0.5)).astype(BF16)
        k_ref[r, :] = p[:, ATTN_W:ATTN_W + KV_W].astype(BF16)
        v_ref[r, :] = p[:, ATTN_W + KV_W:ATTN_W + 2 * KV_W].astype(BF16)
        a = p[:, ATTN_W + 2 * KV_W:ATTN_W + 2 * KV_W + CONV_CH]
        gate = p[:, ATTN_W + 2 * KV_W + CONV_CH:]
        u_ref[r, :] = a * _sigmoid(gate)


def _proj_call(x2d, gin, bin_, w_in_b, tq):
    t = x2d.shape[0]
    row = lambda i: (i, 0)
    fix = lambda i: (0, 0)
    chains = PROJ_CHAINS if tq % (PROJ_CHAINS * BLOCK) == 0 else 1
    return pl.pallas_call(
        functools.partial(_proj_kernel, chains),
        grid=(t // tq,),
        in_specs=[pl.BlockSpec((tq, D_MODEL), row), pl.BlockSpec((1, D_MODEL), fix),
                  pl.BlockSpec((1, D_MODEL), fix), pl.BlockSpec((D_MODEL, IN_W), fix)],
        out_specs=[pl.BlockSpec((tq, ATTN_W), row), pl.BlockSpec((tq, KV_W), row),
                   pl.BlockSpec((tq, KV_W), row), pl.BlockSpec((tq, CONV_CH), row)],
        out_shape=[SDS((t, ATTN_W), BF16), SDS((t, KV_W), BF16), SDS((t, KV_W), BF16), SDS((t, CONV_CH), F32)],
        compiler_params=_cparams("arbitrary"),
        name="ln_in_proj",
    )(x2d, gin, bin_, w_in_b)


def _attn_kernel(sinks_ref, q_ref, kc_ref, kp_ref, vc_ref, vp_ref, km_ref, vm_ref, bias_ref, o_ref):
    first = pl.program_id(1) == 0
    kp = jnp.where(first, km_ref[...], kp_ref[...])
    vp = jnp.where(first, vm_ref[...], vp_ref[...])
    k = jnp.concatenate([kp, kc_ref[...]], axis=0)
    v = jnp.concatenate([vp, vc_ref[...]], axis=0)
    col = lax.broadcasted_iota(I32, (BLOCK, 2 * BLOCK), 1)
    pad_bias = jnp.where(jnp.logical_and(first, col < PAD_FRONT), NEG, 0.0).astype(F32)
    for a in range(ATTN_QBLOCKS):
        q = q_ref[a * BLOCK:(a + 1) * BLOCK, :]
        kw = k[a * BLOCK:(a + 2) * BLOCK, :]
        vw = v[a * BLOCK:(a + 2) * BLOCK, :]
        outs = []
        for h in range(N_Q_HEADS):
            g = h // GQA_GROUP
            qh = q[:, h * HEAD_DIM:(h + 1) * HEAD_DIM]
            kg = kw[:, g * HEAD_DIM:(g + 1) * HEAD_DIM]
            vg = vw[:, g * HEAD_DIM:(g + 1) * HEAD_DIM]
            s = lax.dot_general(qh, kg, (((1,), (1,)), ((), ())), preferred_element_type=F32)
            s = s + bias_ref[h]
            if a == 0:
                s = s + pad_bias
            sink = sinks_ref[h]
            m = jnp.maximum(jnp.max(s, axis=-1, keepdims=True), sink)
            p = jnp.exp(s - m)
            den = jnp.sum(p, axis=-1, keepdims=True) + jnp.exp(sink - m)
            o = jnp.dot(p.astype(BF16), vg, preferred_element_type=F32)
            outs.append(o / den)
        o_ref[a * BLOCK:(a + 1) * BLOCK, :] = jnp.concatenate(outs, axis=1).astype(BF16)


def _attn_call(q, k, v, k_meta, v_meta, bias, sinks, nbatch, nblk):
    t = q.shape[0]
    nq = ATTN_QBLOCKS
    assert nblk % nq == 0
    nstep = nblk // nq
    cur = lambda b, j: (b * nstep + j, 0)
    prev = lambda b, j: (jnp.maximum((b * nstep + j) * nq - 1, 0), 0)
    fix2 = lambda b, j: (0, 0)
    return pl.pallas_call(
        _attn_kernel,
        grid=(nbatch, nstep),
        in_specs=[pl.BlockSpec(memory_space=pltpu.SMEM),
                  pl.BlockSpec((nq * BLOCK, ATTN_W), cur),
                  pl.BlockSpec((nq * BLOCK, KV_W), cur), pl.BlockSpec((BLOCK, KV_W), prev),
                  pl.BlockSpec((nq * BLOCK, KV_W), cur), pl.BlockSpec((BLOCK, KV_W), prev),
                  pl.BlockSpec((BLOCK, KV_W), fix2), pl.BlockSpec((BLOCK, KV_W), fix2),
                  pl.BlockSpec((N_Q_HEADS, BLOCK, 2 * BLOCK), lambda b, j: (0, 0, 0))],
        out_specs=pl.BlockSpec((nq * BLOCK, ATTN_W), cur),
        out_shape=SDS((t, ATTN_W), BF16),
        compiler_params=_cparams("arbitrary", "arbitrary"),
        name="swa_attn",
    )(sinks, q, k, k, v, v, k_meta, v_meta, bias)


def _rel_bias_table(rel_bias):
    qi = np.arange(BLOCK, dtype=np.int32)[:, None]
    kj = np.arange(2 * BLOCK, dtype=np.int32)[None, :]
    dist = BLOCK + qi - kj
    dc = np.clip(dist, 0, WINDOW - 1)
    nf = np.maximum(dc, 1).astype(np.float32)
    large = MAX_EXACT + (np.log(nf / np.float32(MAX_EXACT)) / np.float32(math.log(REL_MAX_DIST / MAX_EXACT))
                         * np.float32(NUM_BUCKETS - MAX_EXACT)).astype(np.int32)
    large = np.minimum(large, NUM_BUCKETS - 1)
    bucket = np.where(dc < MAX_EXACT, dc, large)
    in_window = (dist >= 0) & (dist < WINDOW)
    onehot = (bucket.reshape(-1, 1) == np.arange(NUM_BUCKETS)[None, :]).astype(np.float32)
    bias = jnp.dot(jnp.asarray(onehot), rel_bias.astype(F32), precision=lax.Precision.HIGHEST)
    bias = jnp.transpose(bias.reshape(BLOCK, 2 * BLOCK, N_Q_HEADS), (2, 0, 1))
    return jnp.where(in_window[None], bias, NEG)


def _conv_kernel(uc_ref, up_ref, um_ref, w_ref, cb_ref, g_ref, b_ref, o_ref, s_ref, sh_ref):
    first = pl.program_id(1) == 0
    s_ref[0:CONV_HALO, :] = jnp.where(first, um_ref[...], up_ref[...])
    s_ref[CONV_HALO:CONV_HALO + T_CONV, :] = uc_ref[...]
    off = CONV_HALO - (CONV_K - 1)
    span = sh_ref.shape[1]
    for p in range(1, SUBLANES):
        sh_ref[p] = s_ref[p:p + span, :]
    for c in range(0, T_CONV, R_CONV):
        acc = jnp.zeros((R_CONV, CONV_CH), F32) + cb_ref[...]
        for kk in range(CONV_K):
            p, a = (off + kk) % SUBLANES, (off + kk) // SUBLANES * SUBLANES
            if p == 0:
                win = s_ref[c + a:c + a + R_CONV, :]
            else:
                win = sh_ref[p, c + a:c + a + R_CONV, :]
            acc = acc + win * w_ref[kk:kk + 1, :]
        y = _layer_norm(acc, g_ref[...], b_ref[...])
        o_ref[c:c + R_CONV, :] = (y * _sigmoid(y)).astype(BF16)


def _conv_call(u, u_meta_halo, conv_w, conv_b, g, b, nbatch, seq):
    t = u.shape[0]
    nj = seq // T_CONV
    per = T_CONV // CONV_HALO
    cur = lambda bb, j: (bb * nj + j, 0)
    prev = lambda bb, j: (jnp.maximum((bb * nj + j) * per - 1, 0), 0)
    fix = lambda bb, j: (0, 0)
    return pl.pallas_call(
        _conv_kernel,
        grid=(nbatch, nj),
        in_specs=[pl.BlockSpec((T_CONV, CONV_CH), cur), pl.BlockSpec((CONV_HALO, CONV_CH), prev),
                  pl.BlockSpec((CONV_HALO, CONV_CH), fix), pl.BlockSpec((CONV_K, CONV_CH), fix),
                  pl.BlockSpec((1, CONV_CH), fix), pl.BlockSpec((1, CONV_CH), fix), pl.BlockSpec((1, CONV_CH), fix)],
        out_specs=pl.BlockSpec((T_CONV, CONV_CH), cur),
        out_shape=SDS((t, CONV_CH), BF16),
        scratch_shapes=[pltpu.VMEM((CONV_HALO + T_CONV, CONV_CH), F32),
                        pltpu.VMEM((SUBLANES, T_CONV + CONV_HALO - SUBLANES, CONV_CH), F32)],
        compiler_params=_cparams("arbitrary", "arbitrary"),
        name="conv_ln",
    )(u, u, u_meta_halo, conv_w, conv_b, g, b)


def _mix_kernel(x_ref, at_ref, cv_ref, gin_ref, bin_ref, woa_ref, woc_ref, g1_ref, b1_ref,
                wrh_ref, wrl_ref, h1_ref, h1r_ref, lg_ref):
    rows = x_ref.shape[0] // MIX_CHAINS
    nt = (((1,), (1,)), ((), ()))
    for c in range(MIX_CHAINS):
        r = slice(c * rows, (c + 1) * rows)
        h = _layer_norm(x_ref[r, :], gin_ref[...], bin_ref[...])
        mix = (jnp.dot(at_ref[r, :], woa_ref[...], preferred_element_type=F32)
               + jnp.dot(cv_ref[r, :], woc_ref[...], preferred_element_type=F32))
        h1 = _layer_norm(ALPHA * h + mix, g1_ref[...], b1_ref[...])
        h1_ref[r, :] = h1
        _store_packed(h1r_ref, c * rows * ROW_CHUNKS, rows, _pack_rows(h1[:, :HALF], h1[:, HALF:]))
        hh = h1.astype(BF16)
        hl = (h1 - hh.astype(F32)).astype(BF16)
        lg = lax.dot_general(wrh_ref[...], hh, nt, preferred_element_type=F32)
        lg = lg + lax.dot_general(wrh_ref[...], hl, nt, preferred_element_type=F32)
        lg = lg + lax.dot_general(wrl_ref[...], hh, nt, preferred_element_type=F32)
        lg_ref[:, r] = lg


def _mix_call(x2d, attn, conv, gin, bin_, woa, woc, g1, b1, wrh, wrl):
    t = x2d.shape[0]
    tq = TQ_MIX
    row = lambda i: (i, 0)
    fix = lambda i: (0, 0)
    return pl.pallas_call(
        _mix_kernel,
        grid=(t // tq,),
        in_specs=[pl.BlockSpec((tq, D_MODEL), row), pl.BlockSpec((tq, ATTN_W), row), pl.BlockSpec((tq, CONV_CH), row),
                  pl.BlockSpec((1, D_MODEL), fix), pl.BlockSpec((1, D_MODEL), fix),
                  pl.BlockSpec((ATTN_W, D_MODEL), fix), pl.BlockSpec((CONV_CH, D_MODEL), fix),
                  pl.BlockSpec((1, D_MODEL), fix), pl.BlockSpec((1, D_MODEL), fix),
                  pl.BlockSpec((N_EXPERTS, D_MODEL), fix), pl.BlockSpec((N_EXPERTS, D_MODEL), fix)],
        out_specs=[pl.BlockSpec((tq, D_MODEL), row), pl.BlockSpec((tq * ROW_CHUNKS, LANES), row),
                   pl.BlockSpec((N_EXPERTS, tq), lambda i: (0, i))],
        out_shape=[SDS((t, D_MODEL), F32), SDS((t * ROW_CHUNKS, LANES), U32), SDS((N_EXPERTS, t), F32)],
        compiler_params=_cparams("arbitrary"),
        name="mix_ln1",
    )(x2d, attn, conv, gin, bin_, woa, woc, g1, b1, wrh, wrl)


def _first_argmax(x, rows, nrows):
    m = jnp.max(x, axis=0, keepdims=True)
    idx = jnp.min(jnp.where(x == m, rows, nrows), axis=0, keepdims=True)
    return m, idx


def _route_tile(logits, rbias, carry):
    tn = logits.shape[1]
    scores = _sigmoid(logits)
    choice = scores + rbias
    rows = lax.broadcasted_iota(I32, (N_EXPERTS, tn), 0)
    rows_g = lax.broadcasted_iota(I32, (GROUP_SIZE, tn), 0)
    rows_8 = lax.broadcasted_iota(I32, (N_GROUPS, tn), 0)

    gs = []
    for g in range(N_GROUPS):
        xg = choice[g * GROUP_SIZE:(g + 1) * GROUP_SIZE, :]
        m1, i1 = _first_argmax(xg, rows_g, GROUP_SIZE)
        m2 = jnp.max(jnp.where(rows_g == i1, -jnp.inf, xg), axis=0, keepdims=True)
        gs.append(m1 + m2)
    gsc = jnp.concatenate(gs, axis=0)
    gsel = jnp.zeros((N_GROUPS, tn), F32)
    for _ in range(TOPK_GROUPS):
        _, gi = _first_argmax(gsc, rows_8, N_GROUPS)
        hit = rows_8 == gi
        gsel = jnp.where(hit, 1.0, gsel)
        gsc = jnp.where(hit, -jnp.inf, gsc)
    emask = jnp.concatenate(
        [jnp.broadcast_to(gsel[g:g + 1, :], (GROUP_SIZE, tn)) for g in range(N_GROUPS)], axis=0)
    masked = jnp.where(emask > 0.5, choice, NEG)

    sel_all = jnp.zeros((N_EXPERTS, tn), F32)
    hits, idxs, ws = [], [], []
    for _ in range(TOP_K):
        _, ii = _first_argmax(masked, rows, N_EXPERTS)
        hit = rows == ii
        hits.append(hit)
        idxs.append(ii)
        ws.append(jnp.sum(jnp.where(hit, scores, 0.0), axis=0, keepdims=True))
        sel_all = jnp.where(hit, 1.0, sel_all)
        masked = jnp.where(hit, -jnp.inf, masked)
    wsum = ws[0]
    for w in ws[1:]:
        wsum = wsum + w
    idx = jnp.concatenate(idxs, axis=0)
    wts = jnp.concatenate([w / wsum * ROUTED_SCALE for w in ws], axis=0)

    r_i = lax.broadcasted_iota(I32, (tn, tn), 0)
    c_i = lax.broadcasted_iota(I32, (tn, tn), 1)
    upper = jnp.where(r_i < c_i, 1.0, 0.0).astype(BF16)
    sel_b = sel_all.astype(BF16)
    before = jnp.dot(sel_b, upper, preferred_element_type=F32)
    before = before + jnp.concatenate([carry] * (tn // LANES), axis=1)
    rank = jnp.concatenate(
        [jnp.sum(jnp.where(h, before, 0.0), axis=0, keepdims=True) for h in hits], axis=0).astype(I32)
    carry = carry + jnp.dot(sel_b, jnp.ones((tn, LANES), BF16), preferred_element_type=F32)
    return idx, wts, rank, carry


def _route_kernel(lg_ref, rb_ref, idx_ref, wts_ref, rank_ref, cnt_ref, carry_ref):
    @pl.when(pl.program_id(0) == 0)
    def _():
        carry_ref[...] = jnp.zeros_like(carry_ref)

    idx, wts, rank, carry = _route_tile(lg_ref[...], rb_ref[...], carry_ref[...])
    idx_ref[...] = idx
    wts_ref[...] = wts.T
    rank_ref[...] = rank
    carry_ref[...] = carry
    cnt_ref[...] = carry.astype(I32)


def _route_call(lg, rbias):
    t = lg.shape[1]
    tn = TN_ROUTE
    col = lambda i: (0, i)
    return pl.pallas_call(
        _route_kernel,
        grid=(t // tn,),
        in_specs=[pl.BlockSpec((N_EXPERTS, tn), col), pl.BlockSpec((N_EXPERTS, 1), lambda i: (0, 0))],
        out_specs=[pl.BlockSpec((TOP_K, tn), col), pl.BlockSpec((tn, TOP_K), lambda i: (i, 0)),
                   pl.BlockSpec((TOP_K, tn), col), pl.BlockSpec((N_EXPERTS, LANES), lambda i: (0, 0))],
        out_shape=[SDS((TOP_K, t), I32), SDS((t, TOP_K), F32), SDS((TOP_K, t), I32), SDS((N_EXPERTS, LANES), I32)],
        scratch_shapes=[pltpu.VMEM((N_EXPERTS, LANES), F32)],
        compiler_params=_cparams("arbitrary"),
        name="route",
    )(lg, rbias)


def _dest_kernel(idx_ref, rank_ref, offs_ref, dest_ref):
    tn = idx_ref.shape[1]
    rows = lax.broadcasted_iota(I32, (N_EXPERTS, tn), 0)
    offs = offs_ref[...]
    out = []
    for kk in range(TOP_K):
        hit = rows == idx_ref[kk:kk + 1, :]
        out.append(jnp.sum(jnp.where(hit, offs, 0.0), axis=0, keepdims=True))
    dest_ref[...] = jnp.concatenate(out, axis=0).astype(I32) + rank_ref[...]


def _dest_call(idx, rank, offs_col):
    t = idx.shape[1]
    tn = TN_ROUTE
    col = lambda i: (0, i)
    return pl.pallas_call(
        _dest_kernel,
        grid=(t // tn,),
        in_specs=[pl.BlockSpec((TOP_K, tn), col), pl.BlockSpec((TOP_K, tn), col),
                  pl.BlockSpec((N_EXPERTS, 1), lambda i: (0, 0))],
        out_specs=pl.BlockSpec((TOP_K, tn), col),
        out_shape=SDS((TOP_K, t), I32),
        compiler_params=_cparams("arbitrary"),
        name="dest",
    )(idx, rank, offs_col)


SC_CORES = 2
SC_SUBCORES = 16
SC_CHUNK = 128


def _sc_worker_chunks(t):
    per_worker = t // (SC_CORES * SC_SUBCORES)
    assert per_worker % SC_CHUNK == 0
    return per_worker


def _sc_dispatch_call(dest, h1rows3, n_rows):
    per_worker = _sc_worker_chunks(dest.shape[1])
    mesh = plsc.VectorSubcoreMesh(core_axis_name="c", subcore_axis_name="s")

    @functools.partial(
        pl.kernel, mesh=mesh, out_type=SDS((n_rows, ROW_CHUNKS, LANES), U32),
        scratch_types=[pltpu.VMEM((TOP_K, SC_CHUNK), I32), pltpu.VMEM((SC_CHUNK, ROW_CHUNKS, LANES), U32),
                       pltpu.SemaphoreType.DMA],
        name="sc_dispatch")
    def body(h_hbm, dest_hbm, xs_hbm, idx_v, rows_v, sem):
        wid = lax.axis_index("s") * SC_CORES + lax.axis_index("c")

        @pl.loop(0, per_worker // SC_CHUNK)
        def _(i):
            t0 = wid * per_worker + i * SC_CHUNK
            pltpu.sync_copy(dest_hbm.at[:, pl.ds(t0, SC_CHUNK)], idx_v)
            pltpu.sync_copy(h_hbm.at[pl.ds(t0, SC_CHUNK)], rows_v)
            copies = [pltpu.async_copy(rows_v, xs_hbm.at[idx_v.at[kk]], sem) for kk in range(TOP_K)]
            for c in copies:
                c.wait()

    return body(h1rows3, dest)


def _sc_gather_call(dest, ys3):
    t = dest.shape[1]
    per_worker = _sc_worker_chunks(t)
    mesh = plsc.VectorSubcoreMesh(core_axis_name="c", subcore_axis_name="s")

    @functools.partial(
        pl.kernel, mesh=mesh, out_type=SDS((TOP_K * t, ROW_CHUNKS, LANES), U32),
        scratch_types=[pltpu.VMEM((TOP_K, SC_CHUNK), I32), pltpu.VMEM((SC_CHUNK, ROW_CHUNKS, LANES), U32),
                       pltpu.SemaphoreType.DMA],
        name="sc_gather")
    def body(ys_hbm, dest_hbm, out_hbm, idx_v, rows_v, sem):
        wid = lax.axis_index("s") * SC_CORES + lax.axis_index("c")

        @pl.loop(0, per_worker // SC_CHUNK)
        def _(i):
            t0 = wid * per_worker + i * SC_CHUNK
            pltpu.sync_copy(dest_hbm.at[:, pl.ds(t0, SC_CHUNK)], idx_v)
            for kk in range(TOP_K):
                pltpu.async_copy(ys_hbm.at[idx_v.at[kk]], rows_v, sem).wait()
                pltpu.sync_copy(rows_v, out_hbm.at[pl.ds(kk * t + t0, SC_CHUNK)])

    return body(ys3, dest)


def _expert_kernel(ts_ref, te_ref, tr_ref, nv_ref, wg_hbm, wu_hbm, wd_hbm, xs_hbm, ys_hbm,
                   xbuf, ybuf, wg_f, wu_f, wd_f, wg_b, wu_b, wd_b, xsem, ysem, wsem):
    e = pl.program_id(0)
    rows = xbuf.shape[1]
    tm = rows // ROW_CHUNKS
    g0, g1, nv = ts_ref[e], te_ref[e], nv_ref[0]

    half = rows // 2

    def halves(g, hbm, buf, sem, slots, inbound):
        s = g % slots
        out = []
        for h in range(2):
            far = hbm.at[pl.ds(pl.multiple_of(g * rows + h * half, half), half), :]
            near = buf.at[s, pl.ds(h * half, half), :]
            out.append(pltpu.make_async_copy(far, near, sem.at[s]) if inbound
                       else pltpu.make_async_copy(near, far, sem.at[s]))
        return out

    def tile_op(g, op, hbm, buf, sem, slots, inbound):
        lower, upper = halves(g, hbm, buf, sem, slots, inbound)
        op(lower)

        @pl.when(tr_ref[g] > tm // 2)
        def _():
            op(upper)

    start = lambda c: c.start(priority=1)
    wait = lambda c: c.wait()
    x_start = lambda g: tile_op(g, start, xs_hbm, xbuf, xsem, X_SLOTS, True)
    x_wait = lambda g: tile_op(g, wait, xs_hbm, xbuf, xsem, X_SLOTS, True)
    y_start = lambda g: tile_op(g, start, ys_hbm, ybuf, ysem, Y_SLOTS, False)
    y_wait = lambda g: tile_op(g, wait, ys_hbm, ybuf, ysem, Y_SLOTS, False)

    def w_copies(ex):
        s = ex % W_SLOTS
        return (pltpu.make_async_copy(wg_hbm.at[ex], wg_f.at[s], wsem.at[s]),
                pltpu.make_async_copy(wu_hbm.at[ex], wu_f.at[s], wsem.at[s]),
                pltpu.make_async_copy(wd_hbm.at[ex], wd_f.at[s], wsem.at[s]))

    n_exp = pl.num_programs(0)

    @pl.when(e == 0)
    def _():
        for ex in range(W_AHEAD):
            for c in w_copies(ex):
                c.start()

    @pl.when(e + W_AHEAD < n_exp)
    def _():
        for c in w_copies(e + W_AHEAD):
            c.start()

    for c in w_copies(e):
        c.wait()

    def compute_tile(g):
        x = _load_packed_bf16(xbuf, 0, tm, lead=g % X_SLOTS)
        gate = jnp.dot(x, wg_b[...], preferred_element_type=F32)
        up = jnp.dot(x, wu_b[...], preferred_element_type=F32)
        live = lax.broadcasted_iota(I32, (tm, EXPERT_FF), 0) < tr_ref[g]
        hid = jnp.where(live, gate * _sigmoid(gate) * up, 0.0).astype(BF16)
        y = jnp.dot(hid, wd_b[...], preferred_element_type=F32)
        return _pack_rows(y[:, :HALF], y[:, HALF:])

    def run_tiles(g, n):
        for r in range(n):
            x_wait(g + r)

            @pl.when(g + r + X_AHEAD < nv)
            def _():
                x_start(g + r + X_AHEAD)

            @pl.when(g + r >= Y_SLOTS)
            def _():
                y_wait(g + r - Y_SLOTS)

        packed = [compute_tile(g + r) for r in range(n)]
        for r in range(n):
            _store_packed(ybuf, 0, tm, packed[r], lead=(g + r) % Y_SLOTS)
        for r in range(n):
            y_start(g + r)

    @pl.when(e == 0)
    def _():
        for g in range(X_AHEAD):
            @pl.when(g < nv)
            def _():
                x_start(g)

    @pl.when(g1 > g0)
    def _():
        ws = e % W_SLOTS
        wg_b[...] = wg_f[ws].astype(BF16)
        wu_b[...] = wu_f[ws].astype(BF16)
        wd_b[...] = wd_f[ws].astype(BF16)
        n_tiles = g1 - g0

        def pair(p, c):
            run_tiles(g0 + 2 * p, 2)
            return c

        lax.fori_loop(0, n_tiles // 2, pair, 0)

        @pl.when(n_tiles % 2 == 1)
        def _():
            run_tiles(g1 - 1, 1)

    @pl.when(e == pl.num_programs(0) - 1)
    def _():
        for back in range(1, Y_SLOTS + 1):
            @pl.when(nv >= back)
            def _():
                y_wait(nv - back)


def _expert_call(tile_start, tile_end, tile_rows, n_valid, xs, w_gate, w_up, w_down, n_rows):
    tm = TM_EXP
    hbm = pl.BlockSpec(memory_space=pl.ANY)
    return pl.pallas_call(
        _expert_kernel,
        grid_spec=pltpu.PrefetchScalarGridSpec(
            num_scalar_prefetch=4,
            grid=(N_EXPERTS,),
            in_specs=[hbm, hbm, hbm, hbm],
            out_specs=hbm,
            scratch_shapes=[pltpu.VMEM((X_SLOTS, tm * ROW_CHUNKS, LANES), U32),
                            pltpu.VMEM((Y_SLOTS, tm * ROW_CHUNKS, LANES), U32),
                            pltpu.VMEM((W_SLOTS, D_MODEL, EXPERT_FF), F32), pltpu.VMEM((W_SLOTS, D_MODEL, EXPERT_FF), F32),
                            pltpu.VMEM((W_SLOTS, EXPERT_FF, D_MODEL), F32),
                            pltpu.VMEM((D_MODEL, EXPERT_FF), BF16), pltpu.VMEM((D_MODEL, EXPERT_FF), BF16),
                            pltpu.VMEM((EXPERT_FF, D_MODEL), BF16),
                            pltpu.SemaphoreType.DMA((X_SLOTS,)), pltpu.SemaphoreType.DMA((Y_SLOTS,)),
                            pltpu.SemaphoreType.DMA((W_SLOTS,))],
        ),
        out_shape=SDS((n_rows * ROW_CHUNKS, LANES), U32),
        compiler_params=_cparams("arbitrary"),
        name="experts",
    )(tile_start, tile_end, tile_rows, n_valid, w_gate, w_up, w_down, xs)


COMB_SUB = 32


def _combine_kernel(wts_ref, h1_ref, g_ref, wsg_ref, wsu_ref, wsd_ref, g2_ref, b2_ref, o_ref, routed_ref):
    tn = h1_ref.shape[0]
    for s0 in range(0, tn, COMB_SUB):
        acc = [jnp.zeros((COMB_SUB, LANES), F32) for _ in range(2 * ROW_CHUNKS)]
        for kk in range(TOP_K):
            wk = jnp.broadcast_to(wts_ref[s0:s0 + COMB_SUB, kk:kk + 1], (COMB_SUB, LANES))
            for cc in range(ROW_CHUNKS):
                lo, hi = _unpack_rows(g_ref[kk, pl.ds(s0 * ROW_CHUNKS + cc, COMB_SUB, stride=ROW_CHUNKS), :])
                acc[cc] = acc[cc] + wk * lo
                acc[ROW_CHUNKS + cc] = acc[ROW_CHUNKS + cc] + wk * hi
        routed_ref[s0:s0 + COMB_SUB, :] = jnp.concatenate(acc, axis=1)

    h1 = h1_ref[...]
    hb = h1.astype(BF16)
    sg = jnp.dot(hb, wsg_ref[...], preferred_element_type=F32)
    su = jnp.dot(hb, wsu_ref[...], preferred_element_type=F32)
    ff = jnp.dot((sg * _sigmoid(sg) * su).astype(BF16), wsd_ref[...], preferred_element_type=F32)
    o_ref[...] = _layer_norm(ALPHA * h1 + ff + routed_ref[...], g2_ref[...], b2_ref[...])


def _combine_call(wts_t, h1, gathered, wsg, wsu, wsd, g2, b2):
    t = h1.shape[0]
    tn = TN_COMB
    row = lambda i: (i, 0)
    fix = lambda i: (0, 0)
    return pl.pallas_call(
        _combine_kernel,
        grid=(t // tn,),
        in_specs=[pl.BlockSpec((tn, TOP_K), row),
                  pl.BlockSpec((tn, D_MODEL), row),
                  pl.BlockSpec((TOP_K, tn * ROW_CHUNKS, LANES), lambda i: (0, i, 0)),
                  pl.BlockSpec((D_MODEL, SHARED_FF), fix), pl.BlockSpec((D_MODEL, SHARED_FF), fix),
                  pl.BlockSpec((SHARED_FF, D_MODEL), fix),
                  pl.BlockSpec((1, D_MODEL), fix), pl.BlockSpec((1, D_MODEL), fix)],
        out_specs=pl.BlockSpec((tn, D_MODEL), row),
        out_shape=SDS((t, D_MODEL), F32),
        scratch_shapes=[pltpu.VMEM((tn, D_MODEL), F32)],
        compiler_params=_cparams("arbitrary"),
        name="combine_ln2",
    )(wts_t, h1, gathered, wsg, wsu, wsd, g2, b2)


def kernel(x, meta_tokens, ln_in_g, ln_in_b, rel_bias, w_in, conv_w, conv_b, conv_ln_g, conv_ln_b, sinks,
           w_out, ln1_g, ln1_b, w_router, router_bias, w_gate, w_up, w_down, ws_gate, ws_up, ws_down,
           ln2_g, ln2_b):
    nbatch, seq, d = x.shape
    t = nbatch * seq
    assert d == D_MODEL and w_in.shape[0] == DEPTH
    assert seq % (ATTN_QBLOCKS * BLOCK) == 0 and seq % T_CONV == 0
    assert all(t % tile == 0 for tile in (TQ_PROJ, TQ_MIX, TN_ROUTE, TN_COMB))
    x2d = x.reshape(t, D_MODEL)
    vec = lambda a: a.reshape(1, -1).astype(F32)
    gin, bin_ = vec(ln_in_g), vec(ln_in_b)
    w_in_b = w_in[0].astype(BF16)

    q, k, v, u = _proj_call(x2d, gin, bin_, w_in_b, TQ_PROJ)
    meta_blk = jnp.concatenate([jnp.zeros((PAD_FRONT, D_MODEL), F32), meta_tokens.astype(F32)], axis=0)
    _, k_meta, v_meta, u_meta = _proj_call(meta_blk, gin, bin_, w_in_b, BLOCK)

    attn = _attn_call(q, k, v, k_meta, v_meta, _rel_bias_table(rel_bias), sinks[0].astype(F32),
                      nbatch, seq // BLOCK)

    u_halo = jnp.concatenate([jnp.zeros((CONV_HALO - N_META, CONV_CH), F32), u_meta[PAD_FRONT:]], axis=0)
    conv = _conv_call(u, u_halo, conv_w[0].astype(F32), vec(conv_b[0]), vec(conv_ln_g[0]), vec(conv_ln_b[0]),
                      nbatch, seq)

    w_out_b = w_out[0].astype(BF16)
    wr_t = w_router[0].astype(F32).T
    wr_hi = wr_t.astype(BF16)
    wr_lo = (wr_t - wr_hi.astype(F32)).astype(BF16)
    h1, h1rows, logits = _mix_call(x2d, attn, conv, gin, bin_, w_out_b[:ATTN_W], w_out_b[ATTN_W:],
                                   vec(ln1_g[0]), vec(ln1_b[0]), wr_hi, wr_lo)

    idx, wts_t, rank, cnt = _route_call(logits, router_bias[0].astype(F32).reshape(N_EXPERTS, 1))

    tm = TM_EXP
    n_tiles = (t * TOP_K) // tm + N_EXPERTS
    counts = cnt[:, 0]
    tiles_e = (counts + tm - 1) // tm
    tile_end = jnp.cumsum(tiles_e).astype(I32)
    tile_start = (tile_end - tiles_e).astype(I32)
    offs = tile_start * tm
    tile_id = jnp.arange(n_tiles, dtype=I32)
    lo = jnp.maximum(tile_id[:, None] * tm, offs[None, :])
    hi = jnp.minimum((tile_id[:, None] + 1) * tm, (offs + counts)[None, :])
    tile_rows = jnp.sum(jnp.clip(hi - lo, 0, tm), axis=1).astype(I32)
    n_valid = tile_end[-1:]

    dest = _dest_call(idx, rank, offs.astype(F32).reshape(N_EXPERTS, 1))
    xs = _sc_dispatch_call(dest, h1rows.reshape(t, ROW_CHUNKS, LANES), n_tiles * tm)
    xs = xs.reshape(n_tiles * tm * ROW_CHUNKS, LANES)
    ys = _expert_call(tile_start, tile_end, tile_rows, n_valid, xs, w_gate[0], w_up[0], w_down[0], n_tiles * tm)
    gathered = _sc_gather_call(dest, ys.reshape(n_tiles * tm, ROW_CHUNKS, LANES))
    gathered = gathered.reshape(TOP_K, t * ROW_CHUNKS, LANES)
    out = _combine_call(wts_t, h1, gathered, ws_gate[0].astype(BF16), ws_up[0].astype(BF16),
                        ws_down[0].astype(BF16), vec(ln2_g[0]), vec(ln2_b[0]))
    return out.reshape(nbatch, seq, D_MODEL)
```

```python
import functools
import math

import numpy as np
import jax
import jax.numpy as jnp
from jax import lax
from jax.experimental import pallas as pl
from jax.experimental.pallas import tpu as pltpu
from jax.experimental.pallas import tpu_sc as plsc

F32 = jnp.float32
BF16 = jnp.bfloat16
I32 = jnp.int32
U32 = jnp.uint32
SDS = jax.ShapeDtypeStruct

D_MODEL = 1024
HALF = D_MODEL // 2
LANES = 128
SUBLANES = 8
ROW_CHUNKS = HALF // LANES
N_META = 16
HEAD_DIM = 64
N_Q_HEADS = 8
N_KV_HEADS = 2
GQA_GROUP = N_Q_HEADS // N_KV_HEADS
ATTN_W = N_Q_HEADS * HEAD_DIM
KV_W = N_KV_HEADS * HEAD_DIM
WINDOW = 128
BLOCK = 128
CONV_CH = D_MODEL - ATTN_W
CONV_K = 31
IN_W = ATTN_W + 2 * KV_W + 2 * CONV_CH
NUM_BUCKETS = 32
MAX_EXACT = NUM_BUCKETS // 2
REL_MAX_DIST = 128
N_EXPERTS = 256
TOP_K = 8
N_GROUPS = 8
GROUP_SIZE = N_EXPERTS // N_GROUPS
TOPK_GROUPS = 4
EXPERT_FF = 256
SHARED_FF = 256
ROUTED_SCALE = 2.5
DEPTH = 1
ALPHA = (2.0 * DEPTH) ** 0.25
LN_EPS = 1e-5
NEG = -1e30
PAD_FRONT = (-N_META) % BLOCK

VMEM_LIMIT = 48 * 1024 * 1024

TQ_PROJ = 1024
PROJ_CHAINS = 4
ATTN_QBLOCKS = 2
T_CONV = 256
CONV_HALO = 32
R_CONV = 64
TQ_MIX = 1024
MIX_CHAINS = 4
TN_ROUTE = 256
TM_EXP = 256
X_SLOTS = 8
X_AHEAD = 4
Y_SLOTS = 4
W_SLOTS = 3
W_AHEAD = 2
TN_COMB = 256


def _cparams(*sem):
    return pltpu.CompilerParams(dimension_semantics=sem, vmem_limit_bytes=VMEM_LIMIT)


def _layer_norm(x, g, b):
    mu = jnp.mean(x, axis=-1, keepdims=True)
    xc = x - mu
    var = jnp.mean(xc * xc, axis=-1, keepdims=True)
    return xc * lax.rsqrt(var + LN_EPS) * g + b


def _sigmoid(x):
    return 1.0 / (1.0 + jnp.exp(-x))


def _pack_rows(lo_half, hi_half):
    lo = lax.bitcast_convert_type(lo_half.astype(BF16).astype(F32), U32)
    hi = lax.bitcast_convert_type(hi_half.astype(BF16).astype(F32), U32)
    return lax.shift_right_logical(lo, jnp.uint32(16)) | hi


def _unpack_rows(p):
    lo = lax.bitcast_convert_type(lax.shift_left(p, jnp.uint32(16)), F32)
    hi = lax.bitcast_convert_type(p & jnp.uint32(0xFFFF0000), F32)
    return lo, hi


def _chunk_index(start, j, n, lead):
    rows = pl.ds(start + j, n, stride=ROW_CHUNKS)
    return (rows, slice(None)) if lead is None else (lead, rows, slice(None))


def _store_packed(ref, start, n, packed, lead=None):
    for j in range(ROW_CHUNKS):
        ref[_chunk_index(start, j, n, lead)] = packed[:, j * LANES:(j + 1) * LANES]


def _load_packed_bf16(ref, start, n, lead=None):
    halves = [_unpack_rows(ref[_chunk_index(start, j, n, lead)]) for j in range(ROW_CHUNKS)]
    return jnp.concatenate([h[0] for h in halves] + [h[1] for h in halves], axis=1).astype(BF16)


def _proj_kernel(chains, x_ref, g_ref, b_ref, w_ref, q_ref, k_ref, v_ref, u_ref):
    rows = x_ref.shape[0] // chains
    for c in range(chains):
        r = slice(c * rows, (c + 1) * rows)
        h = _layer_norm(x_ref[r, :], g_ref[...], b_ref[...])
        p = jnp.dot(h.astype(BF16), w_ref[...], preferred_element_type=F32)
        q_ref[r, :] = (p[:, :ATTN_W] * (HEAD_DIM ** -0.5)).astype(BF16)
        k_ref[r, :] = p[:, ATTN_W:ATTN_W + KV_W].astype(BF16)
        v_ref[r, :] = p[:, ATTN_W + KV_W:ATTN_W + 2 * KV_W].astype(BF16)
        a = p[:, ATTN_W + 2 * KV_W:ATTN_W + 2 * KV_W + CONV_CH]
        gate = p[:, ATTN_W + 2 * KV_W + CONV_CH:]
        u_ref[r, :] = a * _sigmoid(gate)


def _proj_call(x2d, gin, bin_, w_in_b, tq):
    t = x2d.shape[0]
    row = lambda i: (i, 0)
    fix = lambda i: (0, 0)
    chains = PROJ_CHAINS if tq % (PROJ_CHAINS * BLOCK) == 0 else 1
    return pl.pallas_call(
        functools.partial(_proj_kernel, chains),
        grid=(t // tq,),
        in_specs=[pl.BlockSpec((tq, D_MODEL), row), pl.BlockSpec((1, D_MODEL), fix),
                  pl.BlockSpec((1, D_MODEL), fix), pl.BlockSpec((D_MODEL, IN_W), fix)],
        out_specs=[pl.BlockSpec((tq, ATTN_W), row), pl.BlockSpec((tq, KV_W), row),
                   pl.BlockSpec((tq, KV_W), row), pl.BlockSpec((tq, CONV_CH), row)],
        out_shape=[SDS((t, ATTN_W), BF16), SDS((t, KV_W), BF16), SDS((t, KV_W), BF16), SDS((t, CONV_CH), F32)],
        compiler_params=_cparams("arbitrary"),
        name="ln_in_proj",
    )(x2d, gin, bin_, w_in_b)


def _attn_kernel(sinks_ref, q_ref, kc_ref, kp_ref, vc_ref, vp_ref, km_ref, vm_ref, bias_ref, o_ref):
    first = pl.program_id(1) == 0
    kp = jnp.where(first, km_ref[...], kp_ref[...])
    vp = jnp.where(first, vm_ref[...], vp_ref[...])
    k = jnp.concatenate([kp, kc_ref[...]], axis=0)
    v = jnp.concatenate([vp, vc_ref[...]], axis=0)
    col = lax.broadcasted_iota(I32, (BLOCK, 2 * BLOCK), 1)
    pad_bias = jnp.where(jnp.logical_and(first, col < PAD_FRONT), NEG, 0.0).astype(F32)
    for a in range(ATTN_QBLOCKS):
        q = q_ref[a * BLOCK:(a + 1) * BLOCK, :]
        kw = k[a * BLOCK:(a + 2) * BLOCK, :]
        vw = v[a * BLOCK:(a + 2) * BLOCK, :]
        outs = []
        for h in range(N_Q_HEADS):
            g = h // GQA_GROUP
            qh = q[:, h * HEAD_DIM:(h + 1) * HEAD_DIM]
            kg = kw[:, g * HEAD_DIM:(g + 1) * HEAD_DIM]
            vg = vw[:, g * HEAD_DIM:(g + 1) * HEAD_DIM]
            s = lax.dot_general(qh, kg, (((1,), (1,)), ((), ())), preferred_element_type=F32)
            s = s + bias_ref[h]
            if a == 0:
                s = s + pad_bias
            sink = sinks_ref[h]
            m = jnp.maximum(jnp.max(s, axis=-1, keepdims=True), sink)
            p = jnp.exp(s - m)
            den = jnp.sum(p, axis=-1, keepdims=True) + jnp.exp(sink - m)
            o = jnp.dot(p.astype(BF16), vg, preferred_element_type=F32)
            outs.append(o / den)
        o_ref[a * BLOCK:(a + 1) * BLOCK, :] = jnp.concatenate(outs, axis=1).astype(BF16)


def _attn_call(q, k, v, k_meta, v_meta, bias, sinks, nbatch, nblk):
    t = q.shape[0]
    nq = ATTN_QBLOCKS
    assert nblk % nq == 0
    nstep = nblk // nq
    cur = lambda b, j: (b * nstep + j, 0)
    prev = lambda b, j: (jnp.maximum((b * nstep + j) * nq - 1, 0), 0)
    fix2 = lambda b, j: (0, 0)
    return pl.pallas_call(
        _attn_kernel,
        grid=(nbatch, nstep),
        in_specs=[pl.BlockSpec(memory_space=pltpu.SMEM),
                  pl.BlockSpec((nq * BLOCK, ATTN_W), cur),
                  pl.BlockSpec((nq * BLOCK, KV_W), cur), pl.BlockSpec((BLOCK, KV_W), prev),
                  pl.BlockSpec((nq * BLOCK, KV_W), cur), pl.BlockSpec((BLOCK, KV_W), prev),
                  pl.BlockSpec((BLOCK, KV_W), fix2), pl.BlockSpec((BLOCK, KV_W), fix2),
                  pl.BlockSpec((N_Q_HEADS, BLOCK, 2 * BLOCK), lambda b, j: (0, 0, 0))],
        out_specs=pl.BlockSpec((nq * BLOCK, ATTN_W), cur),
        out_shape=SDS((t, ATTN_W), BF16),
        compiler_params=_cparams("arbitrary", "arbitrary"),
        name="swa_attn",
    )(sinks, q, k, k, v, v, k_meta, v_meta, bias)


def _rel_bias_table(rel_bias):
    qi = np.arange(BLOCK, dtype=np.int32)[:, None]
    kj = np.arange(2 * BLOCK, dtype=np.int32)[None, :]
    dist = BLOCK + qi - kj
    dc = np.clip(dist, 0, WINDOW - 1)
    nf = np.maximum(dc, 1).astype(np.float32)
    large = MAX_EXACT + (np.log(nf / np.float32(MAX_EXACT)) / np.float32(math.log(REL_MAX_DIST / MAX_EXACT))
                         * np.float32(NUM_BUCKETS - MAX_EXACT)).astype(np.int32)
    large = np.minimum(large, NUM_BUCKETS - 1)
    bucket = np.where(dc < MAX_EXACT, dc, large)
    in_window = (dist >= 0) & (dist < WINDOW)
    onehot = (bucket.reshape(-1, 1) == np.arange(NUM_BUCKETS)[None, :]).astype(np.float32)
    bias = jnp.dot(jnp.asarray(onehot), rel_bias.astype(F32), precision=lax.Precision.HIGHEST)
    bias = jnp.transpose(bias.reshape(BLOCK, 2 * BLOCK, N_Q_HEADS), (2, 0, 1))
    return jnp.where(in_window[None], bias, NEG)


def _conv_kernel(uc_ref, up_ref, um_ref, w_ref, cb_ref, g_ref, b_ref, o_ref, s_ref, sh_ref):
    first = pl.program_id(1) == 0
    s_ref[0:CONV_HALO, :] = jnp.where(first, um_ref[...], up_ref[...])
    s_ref[CONV_HALO:CONV_HALO + T_CONV, :] = uc_ref[...]
    off = CONV_HALO - (CONV_K - 1)
    span = sh_ref.shape[1]
    for p in range(1, SUBLANES):
        sh_ref[p] = s_ref[p:p + span, :]
    for c in range(0, T_CONV, R_CONV):
        acc = jnp.zeros((R_CONV, CONV_CH), F32) + cb_ref[...]
        for kk in range(CONV_K):
            p, a = (off + kk) % SUBLANES, (off + kk) // SUBLANES * SUBLANES
            if p == 0:
                win = s_ref[c + a:c + a + R_CONV, :]
            else:
                win = sh_ref[p, c + a:c + a + R_CONV, :]
            acc = acc + win * w_ref[kk:kk + 1, :]
        y = _layer_norm(acc, g_ref[...], b_ref[...])
        o_ref[c:c + R_CONV, :] = (y * _sigmoid(y)).astype(BF16)


def _conv_call(u, u_meta_halo, conv_w, conv_b, g, b, nbatch, seq):
    t = u.shape[0]
    nj = seq // T_CONV
    per = T_CONV // CONV_HALO
    cur = lambda bb, j: (bb * nj + j, 0)
    prev = lambda bb, j: (jnp.maximum((bb * nj + j) * per - 1, 0), 0)
    fix = lambda bb, j: (0, 0)
    return pl.pallas_call(
        _conv_kernel,
        grid=(nbatch, nj),
        in_specs=[pl.BlockSpec((T_CONV, CONV_CH), cur), pl.BlockSpec((CONV_HALO, CONV_CH), prev),
                  pl.BlockSpec((CONV_HALO, CONV_CH), fix), pl.BlockSpec((CONV_K, CONV_CH), fix),
                  pl.BlockSpec((1, CONV_CH), fix), pl.BlockSpec((1, CONV_CH), fix), pl.BlockSpec((1, CONV_CH), fix)],
        out_specs=pl.BlockSpec((T_CONV, CONV_CH), cur),
        out_shape=SDS((t, CONV_CH), BF16),
        scratch_shapes=[pltpu.VMEM((CONV_HALO + T_CONV, CONV_CH), F32),
                        pltpu.VMEM((SUBLANES, T_CONV + CONV_HALO - SUBLANES, CONV_CH), F32)],
        compiler_params=_cparams("arbitrary", "arbitrary"),
        name="conv_ln",
    )(u, u, u_meta_halo, conv_w, conv_b, g, b)


def _mix_kernel(x_ref, at_ref, cv_ref, gin_ref, bin_ref, woa_ref, woc_ref, g1_ref, b1_ref,
                wrh_ref, wrl_ref, h1_ref, h1r_ref, lg_ref):
    rows = x_ref.shape[0] // MIX_CHAINS
    nt = (((1,), (1,)), ((), ()))
    for c in range(MIX_CHAINS):
        r = slice(c * rows, (c + 1) * rows)
        h = _layer_norm(x_ref[r, :], gin_ref[...], bin_ref[...])
        mix = (jnp.dot(at_ref[r, :], woa_ref[...], preferred_element_type=F32)
               + jnp.dot(cv_ref[r, :], woc_ref[...], preferred_element_type=F32))
        h1 = _layer_norm(ALPHA * h + mix, g1_ref[...], b1_ref[...])
        h1_ref[r, :] = h1
        _store_packed(h1r_ref, c * rows * ROW_CHUNKS, rows, _pack_rows(h1[:, :HALF], h1[:, HALF:]))
        hh = h1.astype(BF16)
        hl = (h1 - hh.astype(F32)).astype(BF16)
        lg = lax.dot_general(wrh_ref[...], hh, nt, preferred_element_type=F32)
        lg = lg + lax.dot_general(wrh_ref[...], hl, nt, preferred_element_type=F32)
        lg = lg + lax.dot_general(wrl_ref[...], hh, nt, preferred_element_type=F32)
        lg_ref[:, r] = lg


def _mix_call(x2d, attn, conv, gin, bin_, woa, woc, g1, b1, wrh, wrl):
    t = x2d.shape[0]
    tq = TQ_MIX
    row = lambda i: (i, 0)
    fix = lambda i: (0, 0)
    return pl.pallas_call(
        _mix_kernel,
        grid=(t // tq,),
        in_specs=[pl.BlockSpec((tq, D_MODEL), row), pl.BlockSpec((tq, ATTN_W), row), pl.BlockSpec((tq, CONV_CH), row),
                  pl.BlockSpec((1, D_MODEL), fix), pl.BlockSpec((1, D_MODEL), fix),
                  pl.BlockSpec((ATTN_W, D_MODEL), fix), pl.BlockSpec((CONV_CH, D_MODEL), fix),
                  pl.BlockSpec((1, D_MODEL), fix), pl.BlockSpec((1, D_MODEL), fix),
                  pl.BlockSpec((N_EXPERTS, D_MODEL), fix), pl.BlockSpec((N_EXPERTS, D_MODEL), fix)],
        out_specs=[pl.BlockSpec((tq, D_MODEL), row), pl.BlockSpec((tq * ROW_CHUNKS, LANES), row),
                   pl.BlockSpec((N_EXPERTS, tq), lambda i: (0, i))],
        out_shape=[SDS((t, D_MODEL), F32), SDS((t * ROW_CHUNKS, LANES), U32), SDS((N_EXPERTS, t), F32)],
        compiler_params=_cparams("arbitrary"),
        name="mix_ln1",
    )(x2d, attn, conv, gin, bin_, woa, woc, g1, b1, wrh, wrl)


def _first_argmax(x, rows, nrows):
    m = jnp.max(x, axis=0, keepdims=True)
    idx = jnp.min(jnp.where(x == m, rows, nrows), axis=0, keepdims=True)
    return m, idx


def _route_tile(logits, rbias, carry):
    tn = logits.shape[1]
    scores = _sigmoid(logits)
    choice = scores + rbias
    rows = lax.broadcasted_iota(I32, (N_EXPERTS, tn), 0)
    rows_g = lax.broadcasted_iota(I32, (GROUP_SIZE, tn), 0)
    rows_8 = lax.broadcasted_iota(I32, (N_GROUPS, tn), 0)

    gs = []
    for g in range(N_GROUPS):
        xg = choice[g * GROUP_SIZE:(g + 1) * GROUP_SIZE, :]
        m1, i1 = _first_argmax(xg, rows_g, GROUP_SIZE)
        m2 = jnp.max(jnp.where(rows_g == i1, -jnp.inf, xg), axis=0, keepdims=True)
        gs.append(m1 + m2)
    gsc = jnp.concatenate(gs, axis=0)
    gsel = jnp.zeros((N_GROUPS, tn), F32)
    for _ in range(TOPK_GROUPS):
        _, gi = _first_argmax(gsc, rows_8, N_GROUPS)
        hit = rows_8 == gi
        gsel = jnp.where(hit, 1.0, gsel)
        gsc = jnp.where(hit, -jnp.inf, gsc)
    emask = jnp.concatenate(
        [jnp.broadcast_to(gsel[g:g + 1, :], (GROUP_SIZE, tn)) for g in range(N_GROUPS)], axis=0)
    masked = jnp.where(emask > 0.5, choice, NEG)

    sel_all = jnp.zeros((N_EXPERTS, tn), F32)
    hits, idxs, ws = [], [], []
    for _ in range(TOP_K):
        _, ii = _first_argmax(masked, rows, N_EXPERTS)
        hit = rows == ii
        hits.append(hit)
        idxs.append(ii)
        ws.append(jnp.sum(jnp.where(hit, scores, 0.0), axis=0, keepdims=True))
        sel_all = jnp.where(hit, 1.0, sel_all)
        masked = jnp.where(hit, -jnp.inf, masked)
    wsum = ws[0]
    for w in ws[1:]:
        wsum = wsum + w
    idx = jnp.concatenate(idxs, axis=0)
    wts = jnp.concatenate([w / wsum * ROUTED_SCALE for w in ws], axis=0)

    r_i = lax.broadcasted_iota(I32, (tn, tn), 0)
    c_i = lax.broadcasted_iota(I32, (tn, tn), 1)
    upper = jnp.where(r_i < c_i, 1.0, 0.0).astype(BF16)
    sel_b = sel_all.astype(BF16)
    before = jnp.dot(sel_b, upper, preferred_element_type=F32)
    before = before + jnp.concatenate([carry] * (tn // LANES), axis=1)
    rank = jnp.concatenate(
        [jnp.sum(jnp.where(h, before, 0.0), axis=0, keepdims=True) for h in hits], axis=0).astype(I32)
    carry = carry + jnp.dot(sel_b, jnp.ones((tn, LANES), BF16), preferred_element_type=F32)
    return idx, wts, rank, carry


def _route_kernel(lg_ref, rb_ref, idx_ref, wts_ref, rank_ref, cnt_ref, carry_ref):
    @pl.when(pl.program_id(0) == 0)
    def _():
        carry_ref[...] = jnp.zeros_like(carry_ref)

    idx, wts, rank, carry = _route_tile(lg_ref[...], rb_ref[...], carry_ref[...])
    idx_ref[...] = idx
    wts_ref[...] = wts.T
    rank_ref[...] = rank
    carry_ref[...] = carry
    cnt_ref[...] = carry.astype(I32)


def _route_call(lg, rbias):
    t = lg.shape[1]
    tn = TN_ROUTE
    col = lambda i: (0, i)
    return pl.pallas_call(
        _route_kernel,
        grid=(t // tn,),
        in_specs=[pl.BlockSpec((N_EXPERTS, tn), col), pl.BlockSpec((N_EXPERTS, 1), lambda i: (0, 0))],
        out_specs=[pl.BlockSpec((TOP_K, tn), col), pl.BlockSpec((tn, TOP_K), lambda i: (i, 0)),
                   pl.BlockSpec((TOP_K, tn), col), pl.BlockSpec((N_EXPERTS, LANES), lambda i: (0, 0))],
        out_shape=[SDS((TOP_K, t), I32), SDS((t, TOP_K), F32), SDS((TOP_K, t), I32), SDS((N_EXPERTS, LANES), I32)],
        scratch_shapes=[pltpu.VMEM((N_EXPERTS, LANES), F32)],
        compiler_params=_cparams("arbitrary"),
        name="route",
    )(lg, rbias)


SC_CORES = 2
SC_SUBCORES = 16
SC_CHUNK = 128
SC_LANES = 16


def _sc_worker_chunks(t):
    per_worker = t // (SC_CORES * SC_SUBCORES)
    assert per_worker % SC_CHUNK == 0
    return per_worker


def _sc_dispatch_call(idx, rank, offs, h1rows3, n_rows):
    t = idx.shape[1]
    per_worker = _sc_worker_chunks(t)
    mesh = plsc.VectorSubcoreMesh(core_axis_name="c", subcore_axis_name="s")

    @functools.partial(
        pl.kernel, mesh=mesh, out_type=[SDS((n_rows, ROW_CHUNKS, LANES), U32), SDS((TOP_K, t), I32)],
        scratch_types=[pltpu.VMEM((TOP_K, SC_CHUNK), I32), pltpu.VMEM((TOP_K, SC_CHUNK), I32),
                       pltpu.VMEM((N_EXPERTS,), I32), pltpu.VMEM((SC_CHUNK, ROW_CHUNKS, LANES), U32),
                       pltpu.SemaphoreType.DMA],
        compiler_params=pltpu.CompilerParams(needs_layout_passes=False),
        name="sc_dispatch")
    def body(h_hbm, idx_hbm, rank_hbm, offs_hbm, xs_hbm, dest_hbm, idx_v, rank_v, offs_v, rows_v, sem):
        wid = lax.axis_index("s") * SC_CORES + lax.axis_index("c")
        pltpu.sync_copy(offs_hbm, offs_v)

        @pl.loop(0, per_worker // SC_CHUNK)
        def _(i):
            t0 = wid * per_worker + i * SC_CHUNK
            pltpu.sync_copy(idx_hbm.at[:, pl.ds(t0, SC_CHUNK)], idx_v)
            pltpu.sync_copy(rank_hbm.at[:, pl.ds(t0, SC_CHUNK)], rank_v)
            pltpu.sync_copy(h_hbm.at[pl.ds(t0, SC_CHUNK)], rows_v)
            for kk in range(TOP_K):
                @pl.loop(0, SC_CHUNK // SC_LANES)
                def _(c):
                    lanes = pl.ds(c * SC_LANES, SC_LANES)
                    idx_v[kk, lanes] = plsc.load_gather(offs_v, [idx_v[kk, lanes]]) + rank_v[kk, lanes]
            pltpu.sync_copy(idx_v, dest_hbm.at[:, pl.ds(t0, SC_CHUNK)])
            copies = [pltpu.async_copy(rows_v, xs_hbm.at[idx_v.at[kk]], sem) for kk in range(TOP_K)]
            for c in copies:
                c.wait()

    return body(h1rows3, idx, rank, offs)


def _sc_gather_call(dest, ys3):
    t = dest.shape[1]
    per_worker = _sc_worker_chunks(t)
    mesh = plsc.VectorSubcoreMesh(core_axis_name="c", subcore_axis_name="s")

    @functools.partial(
        pl.kernel, mesh=mesh, out_type=SDS((TOP_K * t, ROW_CHUNKS, LANES), U32),
        scratch_types=[pltpu.VMEM((TOP_K, SC_CHUNK), I32), pltpu.VMEM((SC_CHUNK, ROW_CHUNKS, LANES), U32),
                       pltpu.SemaphoreType.DMA],
        name="sc_gather")
    def body(ys_hbm, dest_hbm, out_hbm, idx_v, rows_v, sem):
        wid = lax.axis_index("s") * SC_CORES + lax.axis_index("c")

        @pl.loop(0, per_worker // SC_CHUNK)
        def _(i):
            t0 = wid * per_worker + i * SC_CHUNK
            pltpu.sync_copy(dest_hbm.at[:, pl.ds(t0, SC_CHUNK)], idx_v)
            for kk in range(TOP_K):
                pltpu.async_copy(ys_hbm.at[idx_v.at[kk]], rows_v, sem).wait()
                pltpu.sync_copy(rows_v, out_hbm.at[pl.ds(kk * t + t0, SC_CHUNK)])

    return body(ys3, dest)


def _expert_kernel(ts_ref, te_ref, tr_ref, nv_ref, wg_hbm, wu_hbm, wd_hbm, xs_hbm, ys_hbm,
                   xbuf, ybuf, wg_f, wu_f, wd_f, wg_b, wu_b, wd_b, xsem, ysem, wsem):
    e = pl.program_id(0)
    rows = xbuf.shape[1]
    tm = rows // ROW_CHUNKS
    g0, g1, nv = ts_ref[e], te_ref[e], nv_ref[0]

    def x_copy(g):
        s = g % X_SLOTS
        return pltpu.make_async_copy(xs_hbm.at[pl.ds(pl.multiple_of(g * rows, rows), rows), :], xbuf.at[s], xsem.at[s])

    def y_copy(g):
        s = g % Y_SLOTS
        return pltpu.make_async_copy(ybuf.at[s], ys_hbm.at[pl.ds(pl.multiple_of(g * rows, rows), rows), :], ysem.at[s])

    def w_copies(ex):
        s = ex % W_SLOTS
        return (pltpu.make_async_copy(wg_hbm.at[ex], wg_f.at[s], wsem.at[s]),
                pltpu.make_async_copy(wu_hbm.at[ex], wu_f.at[s], wsem.at[s]),
                pltpu.make_async_copy(wd_hbm.at[ex], wd_f.at[s], wsem.at[s]))

    n_exp = pl.num_programs(0)

    @pl.when(e == 0)
    def _():
        for ex in range(W_AHEAD):
            for c in w_copies(ex):
                c.start()

    @pl.when(e + W_AHEAD < n_exp)
    def _():
        for c in w_copies(e + W_AHEAD):
            c.start()

    for c in w_copies(e):
        c.wait()

    def compute_tile(g):
        x = _load_packed_bf16(xbuf, 0, tm, lead=g % X_SLOTS)
        gate = jnp.dot(x, wg_b[...], preferred_element_type=F32)
        up = jnp.dot(x, wu_b[...], preferred_element_type=F32)
        live = lax.broadcasted_iota(I32, (tm, EXPERT_FF), 0) < tr_ref[g]
        hid = jnp.where(live, gate * _sigmoid(gate) * up, 0.0).astype(BF16)
        y = jnp.dot(hid, wd_b[...], preferred_element_type=F32)
        return _pack_rows(y[:, :HALF], y[:, HALF:])

    def run_tiles(g, n):
        for r in range(n):
            x_copy(g + r).wait()

            @pl.when(g + r + X_AHEAD < nv)
            def _():
                x_copy(g + r + X_AHEAD).start(priority=1)

            @pl.when(g + r >= Y_SLOTS)
            def _():
                y_copy(g + r - Y_SLOTS).wait()

        packed = [compute_tile(g + r) for r in range(n)]
        for r in range(n):
            _store_packed(ybuf, 0, tm, packed[r], lead=(g + r) % Y_SLOTS)
        for r in range(n):
            y_copy(g + r).start(priority=1)

    @pl.when(e == 0)
    def _():
        for g in range(X_AHEAD):
            @pl.when(g < nv)
            def _():
                x_copy(g).start(priority=1)

    @pl.when(g1 > g0)
    def _():
        ws = e % W_SLOTS
        wg_b[...] = wg_f[ws].astype(BF16)
        wu_b[...] = wu_f[ws].astype(BF16)
        wd_b[...] = wd_f[ws].astype(BF16)
        n_tiles = g1 - g0

        def pair(p, c):
            run_tiles(g0 + 2 * p, 2)
            return c

        lax.fori_loop(0, n_tiles // 2, pair, 0)

        @pl.when(n_tiles % 2 == 1)
        def _():
            run_tiles(g1 - 1, 1)

    @pl.when(e == pl.num_programs(0) - 1)
    def _():
        for back in range(1, Y_SLOTS + 1):
            @pl.when(nv >= back)
            def _():
                y_copy(nv - back).wait()


def _expert_call(tile_start, tile_end, tile_rows, n_valid, xs, w_gate, w_up, w_down, n_rows):
    tm = TM_EXP
    hbm = pl.BlockSpec(memory_space=pl.ANY)
    return pl.pallas_call(
        _expert_kernel,
        grid_spec=pltpu.PrefetchScalarGridSpec(
            num_scalar_prefetch=4,
            grid=(N_EXPERTS,),
            in_specs=[hbm, hbm, hbm, hbm],
            out_specs=hbm,
            scratch_shapes=[pltpu.VMEM((X_SLOTS, tm * ROW_CHUNKS, LANES), U32),
                            pltpu.VMEM((Y_SLOTS, tm * ROW_CHUNKS, LANES), U32),
                            pltpu.VMEM((W_SLOTS, D_MODEL, EXPERT_FF), F32), pltpu.VMEM((W_SLOTS, D_MODEL, EXPERT_FF), F32),
                            pltpu.VMEM((W_SLOTS, EXPERT_FF, D_MODEL), F32),
                            pltpu.VMEM((D_MODEL, EXPERT_FF), BF16), pltpu.VMEM((D_MODEL, EXPERT_FF), BF16),
                            pltpu.VMEM((EXPERT_FF, D_MODEL), BF16),
                            pltpu.SemaphoreType.DMA((X_SLOTS,)), pltpu.SemaphoreType.DMA((Y_SLOTS,)),
                            pltpu.SemaphoreType.DMA((W_SLOTS,))],
        ),
        out_shape=SDS((n_rows * ROW_CHUNKS, LANES), U32),
        compiler_params=_cparams("arbitrary"),
        name="experts",
    )(tile_start, tile_end, tile_rows, n_valid, w_gate, w_up, w_down, xs)


COMB_SUB = 32


def _combine_kernel(wts_ref, h1_ref, g_ref, wsg_ref, wsu_ref, wsd_ref, g2_ref, b2_ref, o_ref, routed_ref):
    tn = h1_ref.shape[0]
    for s0 in range(0, tn, COMB_SUB):
        acc = [jnp.zeros((COMB_SUB, LANES), F32) for _ in range(2 * ROW_CHUNKS)]
        for kk in range(TOP_K):
            wk = jnp.broadcast_to(wts_ref[s0:s0 + COMB_SUB, kk:kk + 1], (COMB_SUB, LANES))
            for cc in range(ROW_CHUNKS):
                lo, hi = _unpack_rows(g_ref[kk, pl.ds(s0 * ROW_CHUNKS + cc, COMB_SUB, stride=ROW_CHUNKS), :])
                acc[cc] = acc[cc] + wk * lo
                acc[ROW_CHUNKS + cc] = acc[ROW_CHUNKS + cc] + wk * hi
        routed_ref[s0:s0 + COMB_SUB, :] = jnp.concatenate(acc, axis=1)

    h1 = h1_ref[...]
    hb = h1.astype(BF16)
    sg = jnp.dot(hb, wsg_ref[...], preferred_element_type=F32)
    su = jnp.dot(hb, wsu_ref[...], preferred_element_type=F32)
    ff = jnp.dot((sg * _sigmoid(sg) * su).astype(BF16), wsd_ref[...], preferred_element_type=F32)
    o_ref[...] = _layer_norm(ALPHA * h1 + ff + routed_ref[...], g2_ref[...], b2_ref[...])


def _combine_call(wts_t, h1, gathered, wsg, wsu, wsd, g2, b2):
    t = h1.shape[0]
    tn = TN_COMB
    row = lambda i: (i, 0)
    fix = lambda i: (0, 0)
    return pl.pallas_call(
        _combine_kernel,
        grid=(t // tn,),
        in_specs=[pl.BlockSpec((tn, TOP_K), row),
                  pl.BlockSpec((tn, D_MODEL), row),
                  pl.BlockSpec((TOP_K, tn * ROW_CHUNKS, LANES), lambda i: (0, i, 0)),
                  pl.BlockSpec((D_MODEL, SHARED_FF), fix), pl.BlockSpec((D_MODEL, SHARED_FF), fix),
                  pl.BlockSpec((SHARED_FF, D_MODEL), fix),
                  pl.BlockSpec((1, D_MODEL), fix), pl.BlockSpec((1, D_MODEL), fix)],
        out_specs=pl.BlockSpec((tn, D_MODEL), row),
        out_shape=SDS((t, D_MODEL), F32),
        scratch_shapes=[pltpu.VMEM((tn, D_MODEL), F32)],
        compiler_params=_cparams("arbitrary"),
        name="combine_ln2",
    )(wts_t, h1, gathered, wsg, wsu, wsd, g2, b2)


def kernel(x, meta_tokens, ln_in_g, ln_in_b, rel_bias, w_in, conv_w, conv_b, conv_ln_g, conv_ln_b, sinks,
           w_out, ln1_g, ln1_b, w_router, router_bias, w_gate, w_up, w_down, ws_gate, ws_up, ws_down,
           ln2_g, ln2_b):
    nbatch, seq, d = x.shape
    t = nbatch * seq
    assert d == D_MODEL and w_in.shape[0] == DEPTH
    assert seq % (ATTN_QBLOCKS * BLOCK) == 0 and seq % T_CONV == 0
    assert all(t % tile == 0 for tile in (TQ_PROJ, TQ_MIX, TN_ROUTE, TN_COMB))
    x2d = x.reshape(t, D_MODEL)
    vec = lambda a: a.reshape(1, -1).astype(F32)
    gin, bin_ = vec(ln_in_g), vec(ln_in_b)
    w_in_b = w_in[0].astype(BF16)

    q, k, v, u = _proj_call(x2d, gin, bin_, w_in_b, TQ_PROJ)
    meta_blk = jnp.concatenate([jnp.zeros((PAD_FRONT, D_MODEL), F32), meta_tokens.astype(F32)], axis=0)
    _, k_meta, v_meta, u_meta = _proj_call(meta_blk, gin, bin_, w_in_b, BLOCK)

    attn = _attn_call(q, k, v, k_meta, v_meta, _rel_bias_table(rel_bias), sinks[0].astype(F32),
                      nbatch, seq // BLOCK)

    u_halo = jnp.concatenate([jnp.zeros((CONV_HALO - N_META, CONV_CH), F32), u_meta[PAD_FRONT:]], axis=0)
    conv = _conv_call(u, u_halo, conv_w[0].astype(F32), vec(conv_b[0]), vec(conv_ln_g[0]), vec(conv_ln_b[0]),
                      nbatch, seq)

    w_out_b = w_out[0].astype(BF16)
    wr_t = w_router[0].astype(F32).T
    wr_hi = wr_t.astype(BF16)
    wr_lo = (wr_t - wr_hi.astype(F32)).astype(BF16)
    h1, h1rows, logits = _mix_call(x2d, attn, conv, gin, bin_, w_out_b[:ATTN_W], w_out_b[ATTN_W:],
                                   vec(ln1_g[0]), vec(ln1_b[0]), wr_hi, wr_lo)

    idx, wts_t, rank, cnt = _route_call(logits, router_bias[0].astype(F32).reshape(N_EXPERTS, 1))

    tm = TM_EXP
    n_tiles = (t * TOP_K) // tm + N_EXPERTS
    counts = cnt[:, 0]
    tiles_e = (counts + tm - 1) // tm
    tile_end = jnp.cumsum(tiles_e).astype(I32)
    tile_start = (tile_end - tiles_e).astype(I32)
    offs = tile_start * tm
    tile_id = jnp.arange(n_tiles, dtype=I32)
    lo = jnp.maximum(tile_id[:, None] * tm, offs[None, :])
    hi = jnp.minimum((tile_id[:, None] + 1) * tm, (offs + counts)[None, :])
    tile_rows = jnp.sum(jnp.clip(hi - lo, 0, tm), axis=1).astype(I32)
    n_valid = tile_end[-1:]

    xs, dest = _sc_dispatch_call(idx, rank, offs, h1rows.reshape(t, ROW_CHUNKS, LANES), n_tiles * tm)
    xs = xs.reshape(n_tiles * tm * ROW_CHUNKS, LANES)
    ys = _expert_call(tile_start, tile_end, tile_rows, n_valid, xs, w_gate[0], w_up[0], w_down[0], n_tiles * tm)
    gathered = _sc_gather_call(dest, ys.reshape(n_tiles * tm, ROW_CHUNKS, LANES))
    gathered = gathered.reshape(TOP_K, t * ROW_CHUNKS, LANES)
    out = _combine_call(wts_t, h1, gathered, ws_gate[0].astype(BF16), ws_up[0].astype(BF16),
                        ws_down[0].astype(BF16), vec(ln2_g[0]), vec(ln2_b[0]))
    return out.reshape(nbatch, seq, D_MODEL)
```

```python
import functools
import math

import numpy as np
import jax
import jax.numpy as jnp
from jax import lax
from jax.experimental import pallas as pl
from jax.experimental.pallas import tpu as pltpu
from jax.experimental.pallas import tpu_sc as plsc

F32 = jnp.float32
BF16 = jnp.bfloat16
I32 = jnp.int32
U32 = jnp.uint32
SDS = jax.ShapeDtypeStruct

D_MODEL = 1024
HALF = D_MODEL // 2
LANES = 128
SUBLANES = 8
ROW_CHUNKS = HALF // LANES
N_META = 16
HEAD_DIM = 64
N_Q_HEADS = 8
N_KV_HEADS = 2
GQA_GROUP = N_Q_HEADS // N_KV_HEADS
ATTN_W = N_Q_HEADS * HEAD_DIM
KV_W = N_KV_HEADS * HEAD_DIM
WINDOW = 128
BLOCK = 128
CONV_CH = D_MODEL - ATTN_W
CONV_K = 31
IN_W = ATTN_W + 2 * KV_W + 2 * CONV_CH
NUM_BUCKETS = 32
MAX_EXACT = NUM_BUCKETS // 2
REL_MAX_DIST = 128
N_EXPERTS = 256
TOP_K = 8
N_GROUPS = 8
GROUP_SIZE = N_EXPERTS // N_GROUPS
TOPK_GROUPS = 4
EXPERT_FF = 256
SHARED_FF = 256
ROUTED_SCALE = 2.5
DEPTH = 1
ALPHA = (2.0 * DEPTH) ** 0.25
LN_EPS = 1e-5
NEG = -1e30
PAD_FRONT = (-N_META) % BLOCK

VMEM_LIMIT = 48 * 1024 * 1024

TQ_PROJ = 1024
PROJ_CHAINS = 4
ATTN_QBLOCKS = 2
T_CONV = 256
CONV_HALO = 32
R_CONV = 64
TQ_MIX = 1024
MIX_CHAINS = 4
TN_ROUTE = 256
TM_EXP = 256
X_SLOTS = 8
X_AHEAD = 4
Y_SLOTS = 4
W_SLOTS = 3
W_AHEAD = 2
TN_COMB = 512


def _cparams(*sem):
    return pltpu.CompilerParams(dimension_semantics=sem, vmem_limit_bytes=VMEM_LIMIT)


def _layer_norm(x, g, b):
    mu = jnp.mean(x, axis=-1, keepdims=True)
    xc = x - mu
    var = jnp.mean(xc * xc, axis=-1, keepdims=True)
    return xc * lax.rsqrt(var + LN_EPS) * g + b


def _sigmoid(x):
    return 1.0 / (1.0 + jnp.exp(-x))


def _pack_rows(lo_half, hi_half):
    lo = lax.bitcast_convert_type(lo_half.astype(BF16).astype(F32), U32)
    hi = lax.bitcast_convert_type(hi_half.astype(BF16).astype(F32), U32)
    return lax.shift_right_logical(lo, jnp.uint32(16)) | hi


def _unpack_rows(p):
    lo = lax.bitcast_convert_type(lax.shift_left(p, jnp.uint32(16)), F32)
    hi = lax.bitcast_convert_type(p & jnp.uint32(0xFFFF0000), F32)
    return lo, hi


def _chunk_index(start, j, n, lead):
    rows = pl.ds(start + j, n, stride=ROW_CHUNKS)
    return (rows, slice(None)) if lead is None else (lead, rows, slice(None))


def _store_packed(ref, start, n, packed, lead=None):
    for j in range(ROW_CHUNKS):
        ref[_chunk_index(start, j, n, lead)] = packed[:, j * LANES:(j + 1) * LANES]


def _load_packed_bf16(ref, start, n, lead=None):
    halves = [_unpack_rows(ref[_chunk_index(start, j, n, lead)]) for j in range(ROW_CHUNKS)]
    return jnp.concatenate([h[0] for h in halves] + [h[1] for h in halves], axis=1).astype(BF16)


def _proj_kernel(chains, x_ref, g_ref, b_ref, w_ref, q_ref, k_ref, v_ref, u_ref):
    rows = x_ref.shape[0] // chains
    for c in range(chains):
        r = slice(c * rows, (c + 1) * rows)
        h = _layer_norm(x_ref[r, :], g_ref[...], b_ref[...])
        p = jnp.dot(h.astype(BF16), w_ref[...], preferred_element_type=F32)
        q_ref[r, :] = (p[:, :ATTN_W] * (HEAD_DIM ** -0.5)).astype(BF16)
        k_ref[r, :] = p[:, ATTN_W:ATTN_W + KV_W].astype(BF16)
        v_ref[r, :] = p[:, ATTN_W + KV_W:ATTN_W + 2 * KV_W].astype(BF16)
        a = p[:, ATTN_W + 2 * KV_W:ATTN_W + 2 * KV_W + CONV_CH]
        gate = p[:, ATTN_W + 2 * KV_W + CONV_CH:]
        u_ref[r, :] = a * _sigmoid(gate)


def _proj_call(x2d, gin, bin_, w_in_b, tq):
    t = x2d.shape[0]
    row = lambda i: (i, 0)
    fix = lambda i: (0, 0)
    chains = PROJ_CHAINS if tq % (PROJ_CHAINS * BLOCK) == 0 else 1
    return pl.pallas_call(
        functools.partial(_proj_kernel, chains),
        grid=(t // tq,),
        in_specs=[pl.BlockSpec((tq, D_MODEL), row), pl.BlockSpec((1, D_MODEL), fix),
                  pl.BlockSpec((1, D_MODEL), fix), pl.BlockSpec((D_MODEL, IN_W), fix)],
        out_specs=[pl.BlockSpec((tq, ATTN_W), row), pl.BlockSpec((tq, KV_W), row),
                   pl.BlockSpec((tq, KV_W), row), pl.BlockSpec((tq, CONV_CH), row)],
        out_shape=[SDS((t, ATTN_W), BF16), SDS((t, KV_W), BF16), SDS((t, KV_W), BF16), SDS((t, CONV_CH), F32)],
        compiler_params=_cparams("arbitrary"),
        name="ln_in_proj",
    )(x2d, gin, bin_, w_in_b)


def _attn_kernel(sinks_ref, q_ref, kc_ref, kp_ref, vc_ref, vp_ref, km_ref, vm_ref, bias_ref, o_ref):
    first = pl.program_id(1) == 0
    kp = jnp.where(first, km_ref[...], kp_ref[...])
    vp = jnp.where(first, vm_ref[...], vp_ref[...])
    k = jnp.concatenate([kp, kc_ref[...]], axis=0)
    v = jnp.concatenate([vp, vc_ref[...]], axis=0)
    col = lax.broadcasted_iota(I32, (BLOCK, 2 * BLOCK), 1)
    pad_bias = jnp.where(jnp.logical_and(first, col < PAD_FRONT), NEG, 0.0).astype(F32)
    for a in range(ATTN_QBLOCKS):
        q = q_ref[a * BLOCK:(a + 1) * BLOCK, :]
        kw = k[a * BLOCK:(a + 2) * BLOCK, :]
        vw = v[a * BLOCK:(a + 2) * BLOCK, :]
        outs = []
        for h in range(N_Q_HEADS):
            g = h // GQA_GROUP
            qh = q[:, h * HEAD_DIM:(h + 1) * HEAD_DIM]
            kg = kw[:, g * HEAD_DIM:(g + 1) * HEAD_DIM]
            vg = vw[:, g * HEAD_DIM:(g + 1) * HEAD_DIM]
            s = lax.dot_general(qh, kg, (((1,), (1,)), ((), ())), preferred_element_type=F32)
            s = s + bias_ref[h]
            if a == 0:
                s = s + pad_bias
            sink = sinks_ref[h]
            m = jnp.maximum(jnp.max(s, axis=-1, keepdims=True), sink)
            p = jnp.exp(s - m)
            den = jnp.sum(p, axis=-1, keepdims=True) + jnp.exp(sink - m)
            o = jnp.dot(p.astype(BF16), vg, preferred_element_type=F32)
            outs.append(o / den)
        o_ref[a * BLOCK:(a + 1) * BLOCK, :] = jnp.concatenate(outs, axis=1).astype(BF16)


def _attn_call(q, k, v, k_meta, v_meta, bias, sinks, nbatch, nblk):
    t = q.shape[0]
    nq = ATTN_QBLOCKS
    assert nblk % nq == 0
    nstep = nblk // nq
    cur = lambda b, j: (b * nstep + j, 0)
    prev = lambda b, j: (jnp.maximum((b * nstep + j) * nq - 1, 0), 0)
    fix2 = lambda b, j: (0, 0)
    return pl.pallas_call(
        _attn_kernel,
        grid=(nbatch, nstep),
        in_specs=[pl.BlockSpec(memory_space=pltpu.SMEM),
                  pl.BlockSpec((nq * BLOCK, ATTN_W), cur),
                  pl.BlockSpec((nq * BLOCK, KV_W), cur), pl.BlockSpec((BLOCK, KV_W), prev),
                  pl.BlockSpec((nq * BLOCK, KV_W), cur), pl.BlockSpec((BLOCK, KV_W), prev),
                  pl.BlockSpec((BLOCK, KV_W), fix2), pl.BlockSpec((BLOCK, KV_W), fix2),
                  pl.BlockSpec((N_Q_HEADS, BLOCK, 2 * BLOCK), lambda b, j: (0, 0, 0))],
        out_specs=pl.BlockSpec((nq * BLOCK, ATTN_W), cur),
        out_shape=SDS((t, ATTN_W), BF16),
        compiler_params=_cparams("arbitrary", "arbitrary"),
        name="swa_attn",
    )(sinks, q, k, k, v, v, k_meta, v_meta, bias)


def _rel_bias_table(rel_bias):
    qi = np.arange(BLOCK, dtype=np.int32)[:, None]
    kj = np.arange(2 * BLOCK, dtype=np.int32)[None, :]
    dist = BLOCK + qi - kj
    dc = np.clip(dist, 0, WINDOW - 1)
    nf = np.maximum(dc, 1).astype(np.float32)
    large = MAX_EXACT + (np.log(nf / np.float32(MAX_EXACT)) / np.float32(math.log(REL_MAX_DIST / MAX_EXACT))
                         * np.float32(NUM_BUCKETS - MAX_EXACT)).astype(np.int32)
    large = np.minimum(large, NUM_BUCKETS - 1)
    bucket = np.where(dc < MAX_EXACT, dc, large)
    in_window = (dist >= 0) & (dist < WINDOW)
    onehot = (bucket.reshape(-1, 1) == np.arange(NUM_BUCKETS)[None, :]).astype(np.float32)
    bias = jnp.dot(jnp.asarray(onehot), rel_bias.astype(F32), precision=lax.Precision.HIGHEST)
    bias = jnp.transpose(bias.reshape(BLOCK, 2 * BLOCK, N_Q_HEADS), (2, 0, 1))
    return jnp.where(in_window[None], bias, NEG)


def _conv_kernel(uc_ref, up_ref, um_ref, w_ref, cb_ref, g_ref, b_ref, o_ref, s_ref, sh_ref):
    first = pl.program_id(1) == 0
    s_ref[0:CONV_HALO, :] = jnp.where(first, um_ref[...], up_ref[...])
    s_ref[CONV_HALO:CONV_HALO + T_CONV, :] = uc_ref[...]
    off = CONV_HALO - (CONV_K - 1)
    span = sh_ref.shape[1]
    for p in range(1, SUBLANES):
        sh_ref[p] = s_ref[p:p + span, :]
    for c in range(0, T_CONV, R_CONV):
        acc = jnp.zeros((R_CONV, CONV_CH), F32) + cb_ref[...]
        for kk in range(CONV_K):
            p, a = (off + kk) % SUBLANES, (off + kk) // SUBLANES * SUBLANES
            if p == 0:
                win = s_ref[c + a:c + a + R_CONV, :]
            else:
                win = sh_ref[p, c + a:c + a + R_CONV, :]
            acc = acc + win * w_ref[kk:kk + 1, :]
        y = _layer_norm(acc, g_ref[...], b_ref[...])
        o_ref[c:c + R_CONV, :] = (y * _sigmoid(y)).astype(BF16)


def _conv_call(u, u_meta_halo, conv_w, conv_b, g, b, nbatch, seq):
    t = u.shape[0]
    nj = seq // T_CONV
    per = T_CONV // CONV_HALO
    cur = lambda bb, j: (bb * nj + j, 0)
    prev = lambda bb, j: (jnp.maximum((bb * nj + j) * per - 1, 0), 0)
    fix = lambda bb, j: (0, 0)
    return pl.pallas_call(
        _conv_kernel,
        grid=(nbatch, nj),
        in_specs=[pl.BlockSpec((T_CONV, CONV_CH), cur), pl.BlockSpec((CONV_HALO, CONV_CH), prev),
                  pl.BlockSpec((CONV_HALO, CONV_CH), fix), pl.BlockSpec((CONV_K, CONV_CH), fix),
                  pl.BlockSpec((1, CONV_CH), fix), pl.BlockSpec((1, CONV_CH), fix), pl.BlockSpec((1, CONV_CH), fix)],
        out_specs=pl.BlockSpec((T_CONV, CONV_CH), cur),
        out_shape=SDS((t, CONV_CH), BF16),
        scratch_shapes=[pltpu.VMEM((CONV_HALO + T_CONV, CONV_CH), F32),
                        pltpu.VMEM((SUBLANES, T_CONV + CONV_HALO - SUBLANES, CONV_CH), F32)],
        compiler_params=_cparams("arbitrary", "arbitrary"),
        name="conv_ln",
    )(u, u, u_meta_halo, conv_w, conv_b, g, b)


def _mix_kernel(x_ref, at_ref, cv_ref, gin_ref, bin_ref, woa_ref, woc_ref, g1_ref, b1_ref,
                wrh_ref, wrl_ref, h1_ref, h1r_ref, lg_ref):
    rows = x_ref.shape[0] // MIX_CHAINS
    nt = (((1,), (1,)), ((), ()))
    for c in range(MIX_CHAINS):
        r = slice(c * rows, (c + 1) * rows)
        h = _layer_norm(x_ref[r, :], gin_ref[...], bin_ref[...])
        mix = (jnp.dot(at_ref[r, :], woa_ref[...], preferred_element_type=F32)
               + jnp.dot(cv_ref[r, :], woc_ref[...], preferred_element_type=F32))
        h1 = _layer_norm(ALPHA * h + mix, g1_ref[...], b1_ref[...])
        h1_ref[r, :] = h1
        _store_packed(h1r_ref, c * rows * ROW_CHUNKS, rows, _pack_rows(h1[:, :HALF], h1[:, HALF:]))
        hh = h1.astype(BF16)
        hl = (h1 - hh.astype(F32)).astype(BF16)
        lg = lax.dot_general(wrh_ref[...], hh, nt, preferred_element_type=F32)
        lg = lg + lax.dot_general(wrh_ref[...], hl, nt, preferred_element_type=F32)
        lg = lg + lax.dot_general(wrl_ref[...], hh, nt, preferred_element_type=F32)
        lg_ref[:, r] = lg


def _mix_call(x2d, attn, conv, gin, bin_, woa, woc, g1, b1, wrh, wrl):
    t = x2d.shape[0]
    tq = TQ_MIX
    row = lambda i: (i, 0)
    fix = lambda i: (0, 0)
    return pl.pallas_call(
        _mix_kernel,
        grid=(t // tq,),
        in_specs=[pl.BlockSpec((tq, D_MODEL), row), pl.BlockSpec((tq, ATTN_W), row), pl.BlockSpec((tq, CONV_CH), row),
                  pl.BlockSpec((1, D_MODEL), fix), pl.BlockSpec((1, D_MODEL), fix),
                  pl.BlockSpec((ATTN_W, D_MODEL), fix), pl.BlockSpec((CONV_CH, D_MODEL), fix),
                  pl.BlockSpec((1, D_MODEL), fix), pl.BlockSpec((1, D_MODEL), fix),
                  pl.BlockSpec((N_EXPERTS, D_MODEL), fix), pl.BlockSpec((N_EXPERTS, D_MODEL), fix)],
        out_specs=[pl.BlockSpec((tq, D_MODEL), row), pl.BlockSpec((tq * ROW_CHUNKS, LANES), row),
                   pl.BlockSpec((N_EXPERTS, tq), lambda i: (0, i))],
        out_shape=[SDS((t, D_MODEL), F32), SDS((t * ROW_CHUNKS, LANES), U32), SDS((N_EXPERTS, t), F32)],
        compiler_params=_cparams("arbitrary"),
        name="mix_ln1",
    )(x2d, attn, conv, gin, bin_, woa, woc, g1, b1, wrh, wrl)


def _first_argmax(x, rows, nrows):
    m = jnp.max(x, axis=0, keepdims=True)
    idx = jnp.min(jnp.where(x == m, rows, nrows), axis=0, keepdims=True)
    return m, idx


def _route_tile(logits, rbias, carry):
    tn = logits.shape[1]
    scores = _sigmoid(logits)
    choice = scores + rbias
    rows = lax.broadcasted_iota(I32, (N_EXPERTS, tn), 0)
    rows_g = lax.broadcasted_iota(I32, (GROUP_SIZE, tn), 0)
    rows_8 = lax.broadcasted_iota(I32, (N_GROUPS, tn), 0)

    gs = []
    for g in range(N_GROUPS):
        xg = choice[g * GROUP_SIZE:(g + 1) * GROUP_SIZE, :]
        m1, i1 = _first_argmax(xg, rows_g, GROUP_SIZE)
        m2 = jnp.max(jnp.where(rows_g == i1, -jnp.inf, xg), axis=0, keepdims=True)
        gs.append(m1 + m2)
    gsc = jnp.concatenate(gs, axis=0)
    gsel = jnp.zeros((N_GROUPS, tn), F32)
    for _ in range(TOPK_GROUPS):
        _, gi = _first_argmax(gsc, rows_8, N_GROUPS)
        hit = rows_8 == gi
        gsel = jnp.where(hit, 1.0, gsel)
        gsc = jnp.where(hit, -jnp.inf, gsc)
    emask = jnp.concatenate(
        [jnp.broadcast_to(gsel[g:g + 1, :], (GROUP_SIZE, tn)) for g in range(N_GROUPS)], axis=0)
    masked = jnp.where(emask > 0.5, choice, NEG)

    sel_all = jnp.zeros((N_EXPERTS, tn), F32)
    hits, idxs, ws = [], [], []
    for _ in range(TOP_K):
        _, ii = _first_argmax(masked, rows, N_EXPERTS)
        hit = rows == ii
        hits.append(hit)
        idxs.append(ii)
        ws.append(jnp.sum(jnp.where(hit, scores, 0.0), axis=0, keepdims=True))
        sel_all = jnp.where(hit, 1.0, sel_all)
        masked = jnp.where(hit, -jnp.inf, masked)
    wsum = ws[0]
    for w in ws[1:]:
        wsum = wsum + w
    idx = jnp.concatenate(idxs, axis=0)
    wts = jnp.concatenate([w / wsum * ROUTED_SCALE for w in ws], axis=0)

    r_i = lax.broadcasted_iota(I32, (tn, tn), 0)
    c_i = lax.broadcasted_iota(I32, (tn, tn), 1)
    upper = jnp.where(r_i < c_i, 1.0, 0.0).astype(BF16)
    sel_b = sel_all.astype(BF16)
    before = jnp.dot(sel_b, upper, preferred_element_type=F32)
    before = before + jnp.concatenate([carry] * (tn // LANES), axis=1)
    rank = jnp.concatenate(
        [jnp.sum(jnp.where(h, before, 0.0), axis=0, keepdims=True) for h in hits], axis=0).astype(I32)
    carry = carry + jnp.dot(sel_b, jnp.ones((tn, LANES), BF16), preferred_element_type=F32)
    return idx, wts, rank, carry


def _route_kernel(lg_ref, rb_ref, idx_ref, wts_ref, rank_ref, cnt_ref, carry_ref):
    @pl.when(pl.program_id(0) == 0)
    def _():
        carry_ref[...] = jnp.zeros_like(carry_ref)

    idx, wts, rank, carry = _route_tile(lg_ref[...], rb_ref[...], carry_ref[...])
    idx_ref[...] = idx
    wts_ref[...] = wts.T
    rank_ref[...] = rank
    carry_ref[...] = carry
    cnt_ref[...] = carry.astype(I32)


def _route_call(lg, rbias):
    t = lg.shape[1]
    tn = TN_ROUTE
    col = lambda i: (0, i)
    return pl.pallas_call(
        _route_kernel,
        grid=(t // tn,),
        in_specs=[pl.BlockSpec((N_EXPERTS, tn), col), pl.BlockSpec((N_EXPERTS, 1), lambda i: (0, 0))],
        out_specs=[pl.BlockSpec((TOP_K, tn), col), pl.BlockSpec((tn, TOP_K), lambda i: (i, 0)),
                   pl.BlockSpec((TOP_K, tn), col), pl.BlockSpec((N_EXPERTS, LANES), lambda i: (0, 0))],
        out_shape=[SDS((TOP_K, t), I32), SDS((t, TOP_K), F32), SDS((TOP_K, t), I32), SDS((N_EXPERTS, LANES), I32)],
        scratch_shapes=[pltpu.VMEM((N_EXPERTS, LANES), F32)],
        compiler_params=_cparams("arbitrary"),
        name="route",
    )(lg, rbias)


SC_CORES = 2
SC_SUBCORES = 16
SC_CHUNK = 128
SC_LANES = 16


def _sc_worker_chunks(t):
    per_worker = t // (SC_CORES * SC_SUBCORES)
    assert per_worker % SC_CHUNK == 0
    return per_worker


def _sc_dispatch_call(idx, rank, offs, h1rows3, n_rows):
    t = idx.shape[1]
    per_worker = _sc_worker_chunks(t)
    mesh = plsc.VectorSubcoreMesh(core_axis_name="c", subcore_axis_name="s")

    @functools.partial(
        pl.kernel, mesh=mesh, out_type=[SDS((n_rows, ROW_CHUNKS, LANES), U32), SDS((TOP_K, t), I32)],
        scratch_types=[pltpu.VMEM((TOP_K, SC_CHUNK), I32), pltpu.VMEM((TOP_K, SC_CHUNK), I32),
                       pltpu.VMEM((N_EXPERTS,), I32), pltpu.VMEM((SC_CHUNK, ROW_CHUNKS, LANES), U32),
                       pltpu.SemaphoreType.DMA],
        compiler_params=pltpu.CompilerParams(needs_layout_passes=False),
        name="sc_dispatch")
    def body(h_hbm, idx_hbm, rank_hbm, offs_hbm, xs_hbm, dest_hbm, idx_v, rank_v, offs_v, rows_v, sem):
        wid = lax.axis_index("s") * SC_CORES + lax.axis_index("c")
        pltpu.sync_copy(offs_hbm, offs_v)

        @pl.loop(0, per_worker // SC_CHUNK)
        def _(i):
            t0 = wid * per_worker + i * SC_CHUNK
            pltpu.sync_copy(idx_hbm.at[:, pl.ds(t0, SC_CHUNK)], idx_v)
            pltpu.sync_copy(rank_hbm.at[:, pl.ds(t0, SC_CHUNK)], rank_v)
            pltpu.sync_copy(h_hbm.at[pl.ds(t0, SC_CHUNK)], rows_v)
            for kk in range(TOP_K):
                @pl.loop(0, SC_CHUNK // SC_LANES)
                def _(c):
                    lanes = pl.ds(c * SC_LANES, SC_LANES)
                    idx_v[kk, lanes] = plsc.load_gather(offs_v, [idx_v[kk, lanes]]) + rank_v[kk, lanes]
            pltpu.sync_copy(idx_v, dest_hbm.at[:, pl.ds(t0, SC_CHUNK)])
            copies = [pltpu.async_copy(rows_v, xs_hbm.at[idx_v.at[kk]], sem) for kk in range(TOP_K)]
            for c in copies:
                c.wait()

    return body(h1rows3, idx, rank, offs)


def _sc_gather_call(dest, ys3):
    t = dest.shape[1]
    per_worker = _sc_worker_chunks(t)
    mesh = plsc.VectorSubcoreMesh(core_axis_name="c", subcore_axis_name="s")

    @functools.partial(
        pl.kernel, mesh=mesh, out_type=SDS((TOP_K * t, ROW_CHUNKS, LANES), U32),
        scratch_types=[pltpu.VMEM((TOP_K, SC_CHUNK), I32), pltpu.VMEM((SC_CHUNK, ROW_CHUNKS, LANES), U32),
                       pltpu.SemaphoreType.DMA],
        name="sc_gather")
    def body(ys_hbm, dest_hbm, out_hbm, idx_v, rows_v, sem):
        wid = lax.axis_index("s") * SC_CORES + lax.axis_index("c")

        @pl.loop(0, per_worker // SC_CHUNK)
        def _(i):
            t0 = wid * per_worker + i * SC_CHUNK
            pltpu.sync_copy(dest_hbm.at[:, pl.ds(t0, SC_CHUNK)], idx_v)
            for kk in range(TOP_K):
                pltpu.async_copy(ys_hbm.at[idx_v.at[kk]], rows_v, sem).wait()
                pltpu.sync_copy(rows_v, out_hbm.at[pl.ds(kk * t + t0, SC_CHUNK)])

    return body(ys3, dest)


def _expert_kernel(ts_ref, te_ref, tr_ref, nv_ref, wg_hbm, wu_hbm, wd_hbm, xs_hbm, ys_hbm,
                   xbuf, ybuf, wg_f, wu_f, wd_f, wg_b, wu_b, wd_b, xsem, ysem, wsem):
    e = pl.program_id(0)
    rows = xbuf.shape[1]
    tm = rows // ROW_CHUNKS
    g0, g1, nv = ts_ref[e], te_ref[e], nv_ref[0]

    def x_copy(g):
        s = g % X_SLOTS
        return pltpu.make_async_copy(xs_hbm.at[pl.ds(pl.multiple_of(g * rows, rows), rows), :], xbuf.at[s], xsem.at[s])

    def y_copy(g):
        s = g % Y_SLOTS
        return pltpu.make_async_copy(ybuf.at[s], ys_hbm.at[pl.ds(pl.multiple_of(g * rows, rows), rows), :], ysem.at[s])

    def w_copies(ex):
        s = ex % W_SLOTS
        return (pltpu.make_async_copy(wg_hbm.at[ex], wg_f.at[s], wsem.at[s]),
                pltpu.make_async_copy(wu_hbm.at[ex], wu_f.at[s], wsem.at[s]),
                pltpu.make_async_copy(wd_hbm.at[ex], wd_f.at[s], wsem.at[s]))

    n_exp = pl.num_programs(0)

    @pl.when(e == 0)
    def _():
        for ex in range(W_AHEAD):
            for c in w_copies(ex):
                c.start()

    @pl.when(e + W_AHEAD < n_exp)
    def _():
        for c in w_copies(e + W_AHEAD):
            c.start()

    for c in w_copies(e):
        c.wait()

    def compute_tile(g):
        x = _load_packed_bf16(xbuf, 0, tm, lead=g % X_SLOTS)
        gate = jnp.dot(x, wg_b[...], preferred_element_type=F32)
        up = jnp.dot(x, wu_b[...], preferred_element_type=F32)
        live = lax.broadcasted_iota(I32, (tm, EXPERT_FF), 0) < tr_ref[g]
        hid = jnp.where(live, gate * _sigmoid(gate) * up, 0.0).astype(BF16)
        y = jnp.dot(hid, wd_b[...], preferred_element_type=F32)
        return _pack_rows(y[:, :HALF], y[:, HALF:])

    def run_tiles(g, n):
        for r in range(n):
            x_copy(g + r).wait()

            @pl.when(g + r + X_AHEAD < nv)
            def _():
                x_copy(g + r + X_AHEAD).start(priority=1)

            @pl.when(g + r >= Y_SLOTS)
            def _():
                y_copy(g + r - Y_SLOTS).wait()

        packed = [compute_tile(g + r) for r in range(n)]
        for r in range(n):
            _store_packed(ybuf, 0, tm, packed[r], lead=(g + r) % Y_SLOTS)
        for r in range(n):
            y_copy(g + r).start(priority=1)

    @pl.when(e == 0)
    def _():
        for g in range(X_AHEAD):
            @pl.when(g < nv)
            def _():
                x_copy(g).start(priority=1)

    @pl.when(g1 > g0)
    def _():
        ws = e % W_SLOTS
        wg_b[...] = wg_f[ws].astype(BF16)
        wu_b[...] = wu_f[ws].astype(BF16)
        wd_b[...] = wd_f[ws].astype(BF16)
        n_tiles = g1 - g0

        def pair(p, c):
            run_tiles(g0 + 2 * p, 2)
            return c

        lax.fori_loop(0, n_tiles // 2, pair, 0)

        @pl.when(n_tiles % 2 == 1)
        def _():
            run_tiles(g1 - 1, 1)

    @pl.when(e == pl.num_programs(0) - 1)
    def _():
        for back in range(1, Y_SLOTS + 1):
            @pl.when(nv >= back)
            def _():
                y_copy(nv - back).wait()


def _expert_call(tile_start, tile_end, tile_rows, n_valid, xs, w_gate, w_up, w_down, n_rows):
    tm = TM_EXP
    hbm = pl.BlockSpec(memory_space=pl.ANY)
    return pl.pallas_call(
        _expert_kernel,
        grid_spec=pltpu.PrefetchScalarGridSpec(
            num_scalar_prefetch=4,
            grid=(N_EXPERTS,),
            in_specs=[hbm, hbm, hbm, hbm],
            out_specs=hbm,
            scratch_shapes=[pltpu.VMEM((X_SLOTS, tm * ROW_CHUNKS, LANES), U32),
                            pltpu.VMEM((Y_SLOTS, tm * ROW_CHUNKS, LANES), U32),
                            pltpu.VMEM((W_SLOTS, D_MODEL, EXPERT_FF), F32), pltpu.VMEM((W_SLOTS, D_MODEL, EXPERT_FF), F32),
                            pltpu.VMEM((W_SLOTS, EXPERT_FF, D_MODEL), F32),
                            pltpu.VMEM((D_MODEL, EXPERT_FF), BF16), pltpu.VMEM((D_MODEL, EXPERT_FF), BF16),
                            pltpu.VMEM((EXPERT_FF, D_MODEL), BF16),
                            pltpu.SemaphoreType.DMA((X_SLOTS,)), pltpu.SemaphoreType.DMA((Y_SLOTS,)),
                            pltpu.SemaphoreType.DMA((W_SLOTS,))],
        ),
        out_shape=SDS((n_rows * ROW_CHUNKS, LANES), U32),
        compiler_params=_cparams("arbitrary"),
        name="experts",
    )(tile_start, tile_end, tile_rows, n_valid, w_gate, w_up, w_down, xs)


COMB_SUB = 32


def _combine_kernel(wts_ref, h1_ref, g_ref, wsg_ref, wsu_ref, wsd_ref, g2_ref, b2_ref, o_ref, routed_ref):
    tn = h1_ref.shape[0]
    for s0 in range(0, tn, COMB_SUB):
        acc = [jnp.zeros((COMB_SUB, LANES), F32) for _ in range(2 * ROW_CHUNKS)]
        for kk in range(TOP_K):
            wk = jnp.broadcast_to(wts_ref[s0:s0 + COMB_SUB, kk:kk + 1], (COMB_SUB, LANES))
            for cc in range(ROW_CHUNKS):
                lo, hi = _unpack_rows(g_ref[kk, pl.ds(s0 * ROW_CHUNKS + cc, COMB_SUB, stride=ROW_CHUNKS), :])
                acc[cc] = acc[cc] + wk * lo
                acc[ROW_CHUNKS + cc] = acc[ROW_CHUNKS + cc] + wk * hi
        routed_ref[s0:s0 + COMB_SUB, :] = jnp.concatenate(acc, axis=1)

    h1 = h1_ref[...]
    hb = h1.astype(BF16)
    sg = jnp.dot(hb, wsg_ref[...], preferred_element_type=F32)
    su = jnp.dot(hb, wsu_ref[...], preferred_element_type=F32)
    ff = jnp.dot((sg * _sigmoid(sg) * su).astype(BF16), wsd_ref[...], preferred_element_type=F32)
    o_ref[...] = _layer_norm(ALPHA * h1 + ff + routed_ref[...], g2_ref[...], b2_ref[...])


def _combine_call(wts_t, h1, gathered, wsg, wsu, wsd, g2, b2):
    t = h1.shape[0]
    tn = TN_COMB
    row = lambda i: (i, 0)
    fix = lambda i: (0, 0)
    return pl.pallas_call(
        _combine_kernel,
        grid=(t // tn,),
        in_specs=[pl.BlockSpec((tn, TOP_K), row),
                  pl.BlockSpec((tn, D_MODEL), row),
                  pl.BlockSpec((TOP_K, tn * ROW_CHUNKS, LANES), lambda i: (0, i, 0)),
                  pl.BlockSpec((D_MODEL, SHARED_FF), fix), pl.BlockSpec((D_MODEL, SHARED_FF), fix),
                  pl.BlockSpec((SHARED_FF, D_MODEL), fix),
                  pl.BlockSpec((1, D_MODEL), fix), pl.BlockSpec((1, D_MODEL), fix)],
        out_specs=pl.BlockSpec((tn, D_MODEL), row),
        out_shape=SDS((t, D_MODEL), F32),
        scratch_shapes=[pltpu.VMEM((tn, D_MODEL), F32)],
        compiler_params=_cparams("arbitrary"),
        name="combine_ln2",
    )(wts_t, h1, gathered, wsg, wsu, wsd, g2, b2)


def kernel(x, meta_tokens, ln_in_g, ln_in_b, rel_bias, w_in, conv_w, conv_b, conv_ln_g, conv_ln_b, sinks,
           w_out, ln1_g, ln1_b, w_router, router_bias, w_gate, w_up, w_down, ws_gate, ws_up, ws_down,
           ln2_g, ln2_b):
    nbatch, seq, d = x.shape
    t = nbatch * seq
    assert d == D_MODEL and w_in.shape[0] == DEPTH
    assert seq % (ATTN_QBLOCKS * BLOCK) == 0 and seq % T_CONV == 0
    assert all(t % tile == 0 for tile in (TQ_PROJ, TQ_MIX, TN_ROUTE, TN_COMB))
    x2d = x.reshape(t, D_MODEL)
    vec = lambda a: a.reshape(1, -1).astype(F32)
    gin, bin_ = vec(ln_in_g), vec(ln_in_b)
    w_in_b = w_in[0].astype(BF16)

    q, k, v, u = _proj_call(x2d, gin, bin_, w_in_b, TQ_PROJ)
    meta_blk = jnp.concatenate([jnp.zeros((PAD_FRONT, D_MODEL), F32), meta_tokens.astype(F32)], axis=0)
    _, k_meta, v_meta, u_meta = _proj_call(meta_blk, gin, bin_, w_in_b, BLOCK)

    attn = _attn_call(q, k, v, k_meta, v_meta, _rel_bias_table(rel_bias), sinks[0].astype(F32),
                      nbatch, seq // BLOCK)

    u_halo = jnp.concatenate([jnp.zeros((CONV_HALO - N_META, CONV_CH), F32), u_meta[PAD_FRONT:]], axis=0)
    conv = _conv_call(u, u_halo, conv_w[0].astype(F32), vec(conv_b[0]), vec(conv_ln_g[0]), vec(conv_ln_b[0]),
                      nbatch, seq)

    w_out_b = w_out[0].astype(BF16)
    wr_t = w_router[0].astype(F32).T
    wr_hi = wr_t.astype(BF16)
    wr_lo = (wr_t - wr_hi.astype(F32)).astype(BF16)
    h1, h1rows, logits = _mix_call(x2d, attn, conv, gin, bin_, w_out_b[:ATTN_W], w_out_b[ATTN_W:],
                                   vec(ln1_g[0]), vec(ln1_b[0]), wr_hi, wr_lo)

    idx, wts_t, rank, cnt = _route_call(logits, router_bias[0].astype(F32).reshape(N_EXPERTS, 1))

    tm = TM_EXP
    n_tiles = (t * TOP_K) // tm + N_EXPERTS
    counts = cnt[:, 0]
    tiles_e = (counts + tm - 1) // tm
    tile_end = jnp.cumsum(tiles_e).astype(I32)
    tile_start = (tile_end - tiles_e).astype(I32)
    offs = tile_start * tm
    tile_id = jnp.arange(n_tiles, dtype=I32)
    lo = jnp.maximum(tile_id[:, None] * tm, offs[None, :])
    hi = jnp.minimum((tile_id[:, None] + 1) * tm, (offs + counts)[None, :])
    tile_rows = jnp.sum(jnp.clip(hi - lo, 0, tm), axis=1).astype(I32)
    n_valid = tile_end[-1:]

    xs, dest = _sc_dispatch_call(idx, rank, offs, h1rows.reshape(t, ROW_CHUNKS, LANES), n_tiles * tm)
    xs = xs.reshape(n_tiles * tm * ROW_CHUNKS, LANES)
    ys = _expert_call(tile_start, tile_end, tile_rows, n_valid, xs, w_gate[0], w_up[0], w_down[0], n_tiles * tm)
    gathered = _sc_gather_call(dest, ys.reshape(n_tiles * tm, ROW_CHUNKS, LANES))
    gathered = gathered.reshape(TOP_K, t * ROW_CHUNKS, LANES)
    out = _combine_call(wts_t, h1, gathered, ws_gate[0].astype(BF16), ws_up[0].astype(BF16),
                        ws_down[0].astype(BF16), vec(ln2_g[0]), vec(ln2_b[0]))
    return out.reshape(nbatch, seq, D_MODEL)
```

```python
import functools
import math

import numpy as np
import jax
import jax.numpy as jnp
from jax import lax
from jax.experimental import pallas as pl
from jax.experimental.pallas import tpu as pltpu
from jax.experimental.pallas import tpu_sc as plsc

F32 = jnp.float32
BF16 = jnp.bfloat16
I32 = jnp.int32
U32 = jnp.uint32
SDS = jax.ShapeDtypeStruct

D_MODEL = 1024
HALF = D_MODEL // 2
LANES = 128
SUBLANES = 8
ROW_CHUNKS = HALF // LANES
N_META = 16
HEAD_DIM = 64
N_Q_HEADS = 8
N_KV_HEADS = 2
GQA_GROUP = N_Q_HEADS // N_KV_HEADS
ATTN_W = N_Q_HEADS * HEAD_DIM
KV_W = N_KV_HEADS * HEAD_DIM
WINDOW = 128
BLOCK = 128
CONV_CH = D_MODEL - ATTN_W
CONV_K = 31
IN_W = ATTN_W + 2 * KV_W + 2 * CONV_CH
NUM_BUCKETS = 32
MAX_EXACT = NUM_BUCKETS // 2
REL_MAX_DIST = 128
N_EXPERTS = 256
TOP_K = 8
N_GROUPS = 8
GROUP_SIZE = N_EXPERTS // N_GROUPS
TOPK_GROUPS = 4
EXPERT_FF = 256
SHARED_FF = 256
ROUTED_SCALE = 2.5
DEPTH = 1
ALPHA = (2.0 * DEPTH) ** 0.25
LN_EPS = 1e-5
NEG = -1e30
PAD_FRONT = (-N_META) % BLOCK

VMEM_LIMIT = 48 * 1024 * 1024

TQ_PROJ = 1024
PROJ_CHAINS = 4
ATTN_QBLOCKS = 2
T_CONV = 256
CONV_HALO = 32
R_CONV = 64
TQ_MIX = 1024
MIX_CHAINS = 4
TN_ROUTE = 256
TM_EXP = 256
X_SLOTS = 8
X_AHEAD = 6
Y_SLOTS = 8
W_SLOTS = 3
W_AHEAD = 2
TN_COMB = 512


def _cparams(*sem):
    return pltpu.CompilerParams(dimension_semantics=sem, vmem_limit_bytes=VMEM_LIMIT)


def _layer_norm(x, g, b):
    mu = jnp.mean(x, axis=-1, keepdims=True)
    xc = x - mu
    var = jnp.mean(xc * xc, axis=-1, keepdims=True)
    return xc * lax.rsqrt(var + LN_EPS) * g + b


def _sigmoid(x):
    return 1.0 / (1.0 + jnp.exp(-x))


def _pack_rows(lo_half, hi_half):
    lo = lax.bitcast_convert_type(lo_half.astype(BF16).astype(F32), U32)
    hi = lax.bitcast_convert_type(hi_half.astype(BF16).astype(F32), U32)
    return lax.shift_right_logical(lo, jnp.uint32(16)) | hi


def _unpack_rows(p):
    lo = lax.bitcast_convert_type(lax.shift_left(p, jnp.uint32(16)), F32)
    hi = lax.bitcast_convert_type(p & jnp.uint32(0xFFFF0000), F32)
    return lo, hi


def _chunk_index(start, j, n, lead):
    rows = pl.ds(start + j, n, stride=ROW_CHUNKS)
    return (rows, slice(None)) if lead is None else (lead, rows, slice(None))


def _store_packed(ref, start, n, packed, lead=None):
    for j in range(ROW_CHUNKS):
        ref[_chunk_index(start, j, n, lead)] = packed[:, j * LANES:(j + 1) * LANES]


def _load_packed_bf16(ref, start, n, lead=None):
    halves = [_unpack_rows(ref[_chunk_index(start, j, n, lead)]) for j in range(ROW_CHUNKS)]
    return jnp.concatenate([h[0] for h in halves] + [h[1] for h in halves], axis=1).astype(BF16)


def _proj_kernel(chains, x_ref, g_ref, b_ref, w_ref, q_ref, k_ref, v_ref, u_ref):
    rows = x_ref.shape[0] // chains
    for c in range(chains):
        r = slice(c * rows, (c + 1) * rows)
        h = _layer_norm(x_ref[r, :], g_ref[...], b_ref[...])
        p = jnp.dot(h.astype(BF16), w_ref[...], preferred_element_type=F32)
        q_ref[r, :] = (p[:, :ATTN_W] * (HEAD_DIM ** -0.5)).astype(BF16)
        k_ref[r, :] = p[:, ATTN_W:ATTN_W + KV_W].astype(BF16)
        v_ref[r, :] = p[:, ATTN_W + KV_W:ATTN_W + 2 * KV_W].astype(BF16)
        a = p[:, ATTN_W + 2 * KV_W:ATTN_W + 2 * KV_W + CONV_CH]
        gate = p[:, ATTN_W + 2 * KV_W + CONV_CH:]
        u_ref[r, :] = a * _sigmoid(gate)


def _proj_call(x2d, gin, bin_, w_in_b, tq):
    t = x2d.shape[0]
    row = lambda i: (i, 0)
    fix = lambda i: (0, 0)
    chains = PROJ_CHAINS if tq % (PROJ_CHAINS * BLOCK) == 0 else 1
    return pl.pallas_call(
        functools.partial(_proj_kernel, chains),
        grid=(t // tq,),
        in_specs=[pl.BlockSpec((tq, D_MODEL), row), pl.BlockSpec((1, D_MODEL), fix),
                  pl.BlockSpec((1, D_MODEL), fix), pl.BlockSpec((D_MODEL, IN_W), fix)],
        out_specs=[pl.BlockSpec((tq, ATTN_W), row), pl.BlockSpec((tq, KV_W), row),
                   pl.BlockSpec((tq, KV_W), row), pl.BlockSpec((tq, CONV_CH), row)],
        out_shape=[SDS((t, ATTN_W), BF16), SDS((t, KV_W), BF16), SDS((t, KV_W), BF16), SDS((t, CONV_CH), F32)],
        compiler_params=_cparams("arbitrary"),
        name="ln_in_proj",
    )(x2d, gin, bin_, w_in_b)


def _attn_kernel(sinks_ref, q_ref, kc_ref, kp_ref, vc_ref, vp_ref, km_ref, vm_ref, bias_ref, o_ref):
    first = pl.program_id(1) == 0
    kp = jnp.where(first, km_ref[...], kp_ref[...])
    vp = jnp.where(first, vm_ref[...], vp_ref[...])
    k = jnp.concatenate([kp, kc_ref[...]], axis=0)
    v = jnp.concatenate([vp, vc_ref[...]], axis=0)
    col = lax.broadcasted_iota(I32, (BLOCK, 2 * BLOCK), 1)
    pad_bias = jnp.where(jnp.logical_and(first, col < PAD_FRONT), NEG, 0.0).astype(F32)
    for a in range(ATTN_QBLOCKS):
        q = q_ref[a * BLOCK:(a + 1) * BLOCK, :]
        kw = k[a * BLOCK:(a + 2) * BLOCK, :]
        vw = v[a * BLOCK:(a + 2) * BLOCK, :]
        outs = []
        for h in range(N_Q_HEADS):
            g = h // GQA_GROUP
            qh = q[:, h * HEAD_DIM:(h + 1) * HEAD_DIM]
            kg = kw[:, g * HEAD_DIM:(g + 1) * HEAD_DIM]
            vg = vw[:, g * HEAD_DIM:(g + 1) * HEAD_DIM]
            s = lax.dot_general(qh, kg, (((1,), (1,)), ((), ())), preferred_element_type=F32)
            s = s + bias_ref[h]
            if a == 0:
                s = s + pad_bias
            sink = sinks_ref[h]
            m = jnp.maximum(jnp.max(s, axis=-1, keepdims=True), sink)
            p = jnp.exp(s - m)
            den = jnp.sum(p, axis=-1, keepdims=True) + jnp.exp(sink - m)
            o = jnp.dot(p.astype(BF16), vg, preferred_element_type=F32)
            outs.append(o / den)
        o_ref[a * BLOCK:(a + 1) * BLOCK, :] = jnp.concatenate(outs, axis=1).astype(BF16)


def _attn_call(q, k, v, k_meta, v_meta, bias, sinks, nbatch, nblk):
    t = q.shape[0]
    nq = ATTN_QBLOCKS
    assert nblk % nq == 0
    nstep = nblk // nq
    cur = lambda b, j: (b * nstep + j, 0)
    prev = lambda b, j: (jnp.maximum((b * nstep + j) * nq - 1, 0), 0)
    fix2 = lambda b, j: (0, 0)
    return pl.pallas_call(
        _attn_kernel,
        grid=(nbatch, nstep),
        in_specs=[pl.BlockSpec(memory_space=pltpu.SMEM),
                  pl.BlockSpec((nq * BLOCK, ATTN_W), cur),
                  pl.BlockSpec((nq * BLOCK, KV_W), cur), pl.BlockSpec((BLOCK, KV_W), prev),
                  pl.BlockSpec((nq * BLOCK, KV_W), cur), pl.BlockSpec((BLOCK, KV_W), prev),
                  pl.BlockSpec((BLOCK, KV_W), fix2), pl.BlockSpec((BLOCK, KV_W), fix2),
                  pl.BlockSpec((N_Q_HEADS, BLOCK, 2 * BLOCK), lambda b, j: (0, 0, 0))],
        out_specs=pl.BlockSpec((nq * BLOCK, ATTN_W), cur),
        out_shape=SDS((t, ATTN_W), BF16),
        compiler_params=_cparams("arbitrary", "arbitrary"),
        name="swa_attn",
    )(sinks, q, k, k, v, v, k_meta, v_meta, bias)


def _rel_bias_table(rel_bias):
    qi = np.arange(BLOCK, dtype=np.int32)[:, None]
    kj = np.arange(2 * BLOCK, dtype=np.int32)[None, :]
    dist = BLOCK + qi - kj
    dc = np.clip(dist, 0, WINDOW - 1)
    nf = np.maximum(dc, 1).astype(np.float32)
    large = MAX_EXACT + (np.log(nf / np.float32(MAX_EXACT)) / np.float32(math.log(REL_MAX_DIST / MAX_EXACT))
                         * np.float32(NUM_BUCKETS - MAX_EXACT)).astype(np.int32)
    large = np.minimum(large, NUM_BUCKETS - 1)
    bucket = np.where(dc < MAX_EXACT, dc, large)
    in_window = (dist >= 0) & (dist < WINDOW)
    onehot = (bucket.reshape(-1, 1) == np.arange(NUM_BUCKETS)[None, :]).astype(np.float32)
    bias = jnp.dot(jnp.asarray(onehot), rel_bias.astype(F32), precision=lax.Precision.HIGHEST)
    bias = jnp.transpose(bias.reshape(BLOCK, 2 * BLOCK, N_Q_HEADS), (2, 0, 1))
    return jnp.where(in_window[None], bias, NEG)


def _conv_kernel(uc_ref, up_ref, um_ref, w_ref, cb_ref, g_ref, b_ref, o_ref, s_ref, sh_ref):
    first = pl.program_id(1) == 0
    s_ref[0:CONV_HALO, :] = jnp.where(first, um_ref[...], up_ref[...])
    s_ref[CONV_HALO:CONV_HALO + T_CONV, :] = uc_ref[...]
    off = CONV_HALO - (CONV_K - 1)
    span = sh_ref.shape[1]
    for p in range(1, SUBLANES):
        sh_ref[p] = s_ref[p:p + span, :]
    for c in range(0, T_CONV, R_CONV):
        acc = jnp.zeros((R_CONV, CONV_CH), F32) + cb_ref[...]
        for kk in range(CONV_K):
            p, a = (off + kk) % SUBLANES, (off + kk) // SUBLANES * SUBLANES
            if p == 0:
                win = s_ref[c + a:c + a + R_CONV, :]
            else:
                win = sh_ref[p, c + a:c + a + R_CONV, :]
            acc = acc + win * w_ref[kk:kk + 1, :]
        y = _layer_norm(acc, g_ref[...], b_ref[...])
        o_ref[c:c + R_CONV, :] = (y * _sigmoid(y)).astype(BF16)


def _conv_call(u, u_meta_halo, conv_w, conv_b, g, b, nbatch, seq):
    t = u.shape[0]
    nj = seq // T_CONV
    per = T_CONV // CONV_HALO
    cur = lambda bb, j: (bb * nj + j, 0)
    prev = lambda bb, j: (jnp.maximum((bb * nj + j) * per - 1, 0), 0)
    fix = lambda bb, j: (0, 0)
    return pl.pallas_call(
        _conv_kernel,
        grid=(nbatch, nj),
        in_specs=[pl.BlockSpec((T_CONV, CONV_CH), cur), pl.BlockSpec((CONV_HALO, CONV_CH), prev),
                  pl.BlockSpec((CONV_HALO, CONV_CH), fix), pl.BlockSpec((CONV_K, CONV_CH), fix),
                  pl.BlockSpec((1, CONV_CH), fix), pl.BlockSpec((1, CONV_CH), fix), pl.BlockSpec((1, CONV_CH), fix)],
        out_specs=pl.BlockSpec((T_CONV, CONV_CH), cur),
        out_shape=SDS((t, CONV_CH), BF16),
        scratch_shapes=[pltpu.VMEM((CONV_HALO + T_CONV, CONV_CH), F32),
                        pltpu.VMEM((SUBLANES, T_CONV + CONV_HALO - SUBLANES, CONV_CH), F32)],
        compiler_params=_cparams("arbitrary", "arbitrary"),
        name="conv_ln",
    )(u, u, u_meta_halo, conv_w, conv_b, g, b)


def _mix_kernel(x_ref, at_ref, cv_ref, gin_ref, bin_ref, woa_ref, woc_ref, g1_ref, b1_ref,
                wrh_ref, wrl_ref, h1_ref, h1r_ref, lg_ref):
    rows = x_ref.shape[0] // MIX_CHAINS
    nt = (((1,), (1,)), ((), ()))
    for c in range(MIX_CHAINS):
        r = slice(c * rows, (c + 1) * rows)
        h = _layer_norm(x_ref[r, :], gin_ref[...], bin_ref[...])
        mix = (jnp.dot(at_ref[r, :], woa_ref[...], preferred_element_type=F32)
               + jnp.dot(cv_ref[r, :], woc_ref[...], preferred_element_type=F32))
        h1 = _layer_norm(ALPHA * h + mix, g1_ref[...], b1_ref[...])
        h1_ref[r, :] = h1
        _store_packed(h1r_ref, c * rows * ROW_CHUNKS, rows, _pack_rows(h1[:, :HALF], h1[:, HALF:]))
        hh = h1.astype(BF16)
        hl = (h1 - hh.astype(F32)).astype(BF16)
        lg = lax.dot_general(wrh_ref[...], hh, nt, preferred_element_type=F32)
        lg = lg + lax.dot_general(wrh_ref[...], hl, nt, preferred_element_type=F32)
        lg = lg + lax.dot_general(wrl_ref[...], hh, nt, preferred_element_type=F32)
        lg_ref[:, r] = lg


def _mix_call(x2d, attn, conv, gin, bin_, woa, woc, g1, b1, wrh, wrl):
    t = x2d.shape[0]
    tq = TQ_MIX
    row = lambda i: (i, 0)
    fix = lambda i: (0, 0)
    return pl.pallas_call(
        _mix_kernel,
        grid=(t // tq,),
        in_specs=[pl.BlockSpec((tq, D_MODEL), row), pl.BlockSpec((tq, ATTN_W), row), pl.BlockSpec((tq, CONV_CH), row),
                  pl.BlockSpec((1, D_MODEL), fix), pl.BlockSpec((1, D_MODEL), fix),
                  pl.BlockSpec((ATTN_W, D_MODEL), fix), pl.BlockSpec((CONV_CH, D_MODEL), fix),
                  pl.BlockSpec((1, D_MODEL), fix), pl.BlockSpec((1, D_MODEL), fix),
                  pl.BlockSpec((N_EXPERTS, D_MODEL), fix), pl.BlockSpec((N_EXPERTS, D_MODEL), fix)],
        out_specs=[pl.BlockSpec((tq, D_MODEL), row), pl.BlockSpec((tq * ROW_CHUNKS, LANES), row),
                   pl.BlockSpec((N_EXPERTS, tq), lambda i: (0, i))],
        out_shape=[SDS((t, D_MODEL), F32), SDS((t * ROW_CHUNKS, LANES), U32), SDS((N_EXPERTS, t), F32)],
        compiler_params=_cparams("arbitrary"),
        name="mix_ln1",
    )(x2d, attn, conv, gin, bin_, woa, woc, g1, b1, wrh, wrl)


def _first_argmax(x, rows, nrows):
    m = jnp.max(x, axis=0, keepdims=True)
    idx = jnp.min(jnp.where(x == m, rows, nrows), axis=0, keepdims=True)
    return m, idx


def _route_tile(logits, rbias, carry):
    tn = logits.shape[1]
    scores = _sigmoid(logits)
    choice = scores + rbias
    rows = lax.broadcasted_iota(I32, (N_EXPERTS, tn), 0)
    rows_g = lax.broadcasted_iota(I32, (GROUP_SIZE, tn), 0)
    rows_8 = lax.broadcasted_iota(I32, (N_GROUPS, tn), 0)

    gs = []
    for g in range(N_GROUPS):
        xg = choice[g * GROUP_SIZE:(g + 1) * GROUP_SIZE, :]
        m1, i1 = _first_argmax(xg, rows_g, GROUP_SIZE)
        m2 = jnp.max(jnp.where(rows_g == i1, -jnp.inf, xg), axis=0, keepdims=True)
        gs.append(m1 + m2)
    gsc = jnp.concatenate(gs, axis=0)
    gsel = jnp.zeros((N_GROUPS, tn), F32)
    for _ in range(TOPK_GROUPS):
        _, gi = _first_argmax(gsc, rows_8, N_GROUPS)
        hit = rows_8 == gi
        gsel = jnp.where(hit, 1.0, gsel)
        gsc = jnp.where(hit, -jnp.inf, gsc)
    emask = jnp.concatenate(
        [jnp.broadcast_to(gsel[g:g + 1, :], (GROUP_SIZE, tn)) for g in range(N_GROUPS)], axis=0)
    masked = jnp.where(emask > 0.5, choice, NEG)

    sel_all = jnp.zeros((N_EXPERTS, tn), F32)
    hits, idxs, ws = [], [], []
    for _ in range(TOP_K):
        _, ii = _first_argmax(masked, rows, N_EXPERTS)
        hit = rows == ii
        hits.append(hit)
        idxs.append(ii)
        ws.append(jnp.sum(jnp.where(hit, scores, 0.0), axis=0, keepdims=True))
        sel_all = jnp.where(hit, 1.0, sel_all)
        masked = jnp.where(hit, -jnp.inf, masked)
    wsum = ws[0]
    for w in ws[1:]:
        wsum = wsum + w
    idx = jnp.concatenate(idxs, axis=0)
    wts = jnp.concatenate([w / wsum * ROUTED_SCALE for w in ws], axis=0)

    r_i = lax.broadcasted_iota(I32, (tn, tn), 0)
    c_i = lax.broadcasted_iota(I32, (tn, tn), 1)
    upper = jnp.where(r_i < c_i, 1.0, 0.0).astype(BF16)
    sel_b = sel_all.astype(BF16)
    before = jnp.dot(sel_b, upper, preferred_element_type=F32)
    before = before + jnp.concatenate([carry] * (tn // LANES), axis=1)
    rank = jnp.concatenate(
        [jnp.sum(jnp.where(h, before, 0.0), axis=0, keepdims=True) for h in hits], axis=0).astype(I32)
    carry = carry + jnp.dot(sel_b, jnp.ones((tn, LANES), BF16), preferred_element_type=F32)
    return idx, wts, rank, carry


def _route_kernel(lg_ref, rb_ref, idx_ref, wts_ref, rank_ref, cnt_ref, carry_ref):
    @pl.when(pl.program_id(0) == 0)
    def _():
        carry_ref[...] = jnp.zeros_like(carry_ref)

    idx, wts, rank, carry = _route_tile(lg_ref[...], rb_ref[...], carry_ref[...])
    idx_ref[...] = idx
    wts_ref[...] = wts.T
    rank_ref[...] = rank
    carry_ref[...] = carry
    cnt_ref[...] = carry.astype(I32)


def _route_call(lg, rbias):
    t = lg.shape[1]
    tn = TN_ROUTE
    col = lambda i: (0, i)
    return pl.pallas_call(
        _route_kernel,
        grid=(t // tn,),
        in_specs=[pl.BlockSpec((N_EXPERTS, tn), col), pl.BlockSpec((N_EXPERTS, 1), lambda i: (0, 0))],
        out_specs=[pl.BlockSpec((TOP_K, tn), col), pl.BlockSpec((tn, TOP_K), lambda i: (i, 0)),
                   pl.BlockSpec((TOP_K, tn), col), pl.BlockSpec((N_EXPERTS, LANES), lambda i: (0, 0))],
        out_shape=[SDS((TOP_K, t), I32), SDS((t, TOP_K), F32), SDS((TOP_K, t), I32), SDS((N_EXPERTS, LANES), I32)],
        scratch_shapes=[pltpu.VMEM((N_EXPERTS, LANES), F32)],
        compiler_params=_cparams("arbitrary"),
        name="route",
    )(lg, rbias)


SC_CORES = 2
SC_SUBCORES = 16
SC_CHUNK = 128
SC_LANES = 16


def _sc_worker_chunks(t):
    per_worker = t // (SC_CORES * SC_SUBCORES)
    assert per_worker % SC_CHUNK == 0
    return per_worker


def _sc_dispatch_call(idx, rank, offs, h1rows3, n_rows):
    t = idx.shape[1]
    per_worker = _sc_worker_chunks(t)
    mesh = plsc.VectorSubcoreMesh(core_axis_name="c", subcore_axis_name="s")

    @functools.partial(
        pl.kernel, mesh=mesh, out_type=[SDS((n_rows, ROW_CHUNKS, LANES), U32), SDS((TOP_K, t), I32)],
        scratch_types=[pltpu.VMEM((TOP_K, SC_CHUNK), I32), pltpu.VMEM((TOP_K, SC_CHUNK), I32),
                       pltpu.VMEM((N_EXPERTS,), I32), pltpu.VMEM((SC_CHUNK, ROW_CHUNKS, LANES), U32),
                       pltpu.SemaphoreType.DMA],
        compiler_params=pltpu.CompilerParams(needs_layout_passes=False),
        name="sc_dispatch")
    def body(h_hbm, idx_hbm, rank_hbm, offs_hbm, xs_hbm, dest_hbm, idx_v, rank_v, offs_v, rows_v, sem):
        wid = lax.axis_index("s") * SC_CORES + lax.axis_index("c")
        pltpu.sync_copy(offs_hbm, offs_v)

        @pl.loop(0, per_worker // SC_CHUNK)
        def _(i):
            t0 = wid * per_worker + i * SC_CHUNK
            pltpu.sync_copy(idx_hbm.at[:, pl.ds(t0, SC_CHUNK)], idx_v)
            pltpu.sync_copy(rank_hbm.at[:, pl.ds(t0, SC_CHUNK)], rank_v)
            pltpu.sync_copy(h_hbm.at[pl.ds(t0, SC_CHUNK)], rows_v)
            for kk in range(TOP_K):
                @pl.loop(0, SC_CHUNK // SC_LANES)
                def _(c):
                    lanes = pl.ds(c * SC_LANES, SC_LANES)
                    idx_v[kk, lanes] = plsc.load_gather(offs_v, [idx_v[kk, lanes]]) + rank_v[kk, lanes]
            pltpu.sync_copy(idx_v, dest_hbm.at[:, pl.ds(t0, SC_CHUNK)])
            copies = [pltpu.async_copy(rows_v, xs_hbm.at[idx_v.at[kk]], sem) for kk in range(TOP_K)]
            for c in copies:
                c.wait()

    return body(h1rows3, idx, rank, offs)


def _sc_gather_call(dest, ys3):
    t = dest.shape[1]
    per_worker = _sc_worker_chunks(t)
    mesh = plsc.VectorSubcoreMesh(core_axis_name="c", subcore_axis_name="s")

    half = SC_CHUNK // 2
    items = [(kk, h) for kk in range(TOP_K) for h in range(2)]

    @functools.partial(
        pl.kernel, mesh=mesh, out_type=SDS((TOP_K * t, ROW_CHUNKS, LANES), U32),
        scratch_types=[pltpu.VMEM((TOP_K, SC_CHUNK), I32), pltpu.VMEM((2, half, ROW_CHUNKS, LANES), U32),
                       pltpu.SemaphoreType.DMA((2,)), pltpu.SemaphoreType.DMA((2,))],
        name="sc_gather")
    def body(ys_hbm, dest_hbm, out_hbm, idx_v, rows_v, gsem, wsem):
        wid = lax.axis_index("s") * SC_CORES + lax.axis_index("c")

        @pl.loop(0, per_worker // SC_CHUNK)
        def _(i):
            t0 = wid * per_worker + i * SC_CHUNK
            pltpu.sync_copy(dest_hbm.at[:, pl.ds(t0, SC_CHUNK)], idx_v)

            def gather(j):
                kk, h = items[j]
                return pltpu.make_async_copy(ys_hbm.at[idx_v.at[kk, pl.ds(h * half, half)]], rows_v.at[j % 2],
                                             gsem.at[j % 2])

            def put(j):
                kk, h = items[j]
                return pltpu.make_async_copy(rows_v.at[j % 2], out_hbm.at[pl.ds(kk * t + t0 + h * half, half)],
                                             wsem.at[j % 2])

            gather(0).start()
            for j in range(len(items)):
                if j + 1 < len(items):
                    if j >= 1:
                        put(j - 1).wait()
                    gather(j + 1).start()
                gather(j).wait()
                put(j).start()
            put(len(items) - 2).wait()
            put(len(items) - 1).wait()

    return body(ys3, dest)


def _expert_kernel(ts_ref, te_ref, tr_ref, nv_ref, wg_hbm, wu_hbm, wd_hbm, xs_hbm, ys_hbm,
                   xbuf, ybuf, wg_f, wu_f, wd_f, wg_b, wu_b, wd_b, xsem, ysem, wsem):
    e = pl.program_id(0)
    rows = xbuf.shape[1]
    tm = rows // ROW_CHUNKS
    g0, g1, nv = ts_ref[e], te_ref[e], nv_ref[0]

    def x_copy(g):
        s = g % X_SLOTS
        return pltpu.make_async_copy(xs_hbm.at[pl.ds(pl.multiple_of(g * rows, rows), rows), :], xbuf.at[s], xsem.at[s])

    def y_copy(g):
        s = g % Y_SLOTS
        return pltpu.make_async_copy(ybuf.at[s], ys_hbm.at[pl.ds(pl.multiple_of(g * rows, rows), rows), :], ysem.at[s])

    def w_copies(ex):
        s = ex % W_SLOTS
        return (pltpu.make_async_copy(wg_hbm.at[ex], wg_f.at[s], wsem.at[s]),
                pltpu.make_async_copy(wu_hbm.at[ex], wu_f.at[s], wsem.at[s]),
                pltpu.make_async_copy(wd_hbm.at[ex], wd_f.at[s], wsem.at[s]))

    n_exp = pl.num_programs(0)

    @pl.when(e == 0)
    def _():
        for ex in range(W_AHEAD):
            for c in w_copies(ex):
                c.start()

    @pl.when(e + W_AHEAD < n_exp)
    def _():
        for c in w_copies(e + W_AHEAD):
            c.start()

    for c in w_copies(e):
        c.wait()

    def compute_tile(g):
        x = _load_packed_bf16(xbuf, 0, tm, lead=g % X_SLOTS)
        gate = jnp.dot(x, wg_b[...], preferred_element_type=F32)
        up = jnp.dot(x, wu_b[...], preferred_element_type=F32)
        live = lax.broadcasted_iota(I32, (tm, EXPERT_FF), 0) < tr_ref[g]
        hid = jnp.where(live, gate * _sigmoid(gate) * up, 0.0).astype(BF16)
        y = jnp.dot(hid, wd_b[...], preferred_element_type=F32)
        return _pack_rows(y[:, :HALF], y[:, HALF:])

    def run_tiles(g, n):
        for r in range(n):
            x_copy(g + r).wait()

            @pl.when(g + r + X_AHEAD < nv)
            def _():
                x_copy(g + r + X_AHEAD).start(priority=1)

            @pl.when(g + r >= Y_SLOTS)
            def _():
                y_copy(g + r - Y_SLOTS).wait()

        packed = [compute_tile(g + r) for r in range(n)]
        for r in range(n):
            _store_packed(ybuf, 0, tm, packed[r], lead=(g + r) % Y_SLOTS)
        for r in range(n):
            y_copy(g + r).start(priority=1)

    @pl.when(e == 0)
    def _():
        for g in range(X_AHEAD):
            @pl.when(g < nv)
            def _():
                x_copy(g).start(priority=1)

    @pl.when(g1 > g0)
    def _():
        ws = e % W_SLOTS
        wg_b[...] = wg_f[ws].astype(BF16)
        wu_b[...] = wu_f[ws].astype(BF16)
        wd_b[...] = wd_f[ws].astype(BF16)
        n_tiles = g1 - g0

        def pair(p, c):
            run_tiles(g0 + 2 * p, 2)
            return c

        lax.fori_loop(0, n_tiles // 2, pair, 0)

        @pl.when(n_tiles % 2 == 1)
        def _():
            run_tiles(g1 - 1, 1)

    @pl.when(e == pl.num_programs(0) - 1)
    def _():
        for back in range(1, Y_SLOTS + 1):
            @pl.when(nv >= back)
            def _():
                y_copy(nv - back).wait()


def _expert_call(tile_start, tile_end, tile_rows, n_valid, xs, w_gate, w_up, w_down, n_rows):
    tm = TM_EXP
    hbm = pl.BlockSpec(memory_space=pl.ANY)
    return pl.pallas_call(
        _expert_kernel,
        grid_spec=pltpu.PrefetchScalarGridSpec(
            num_scalar_prefetch=4,
            grid=(N_EXPERTS,),
            in_specs=[hbm, hbm, hbm, hbm],
            out_specs=hbm,
            scratch_shapes=[pltpu.VMEM((X_SLOTS, tm * ROW_CHUNKS, LANES), U32),
                            pltpu.VMEM((Y_SLOTS, tm * ROW_CHUNKS, LANES), U32),
                            pltpu.VMEM((W_SLOTS, D_MODEL, EXPERT_FF), F32), pltpu.VMEM((W_SLOTS, D_MODEL, EXPERT_FF), F32),
                            pltpu.VMEM((W_SLOTS, EXPERT_FF, D_MODEL), F32),
                            pltpu.VMEM((D_MODEL, EXPERT_FF), BF16), pltpu.VMEM((D_MODEL, EXPERT_FF), BF16),
                            pltpu.VMEM((EXPERT_FF, D_MODEL), BF16),
                            pltpu.SemaphoreType.DMA((X_SLOTS,)), pltpu.SemaphoreType.DMA((Y_SLOTS,)),
                            pltpu.SemaphoreType.DMA((W_SLOTS,))],
        ),
        out_shape=SDS((n_rows * ROW_CHUNKS, LANES), U32),
        compiler_params=_cparams("arbitrary"),
        name="experts",
    )(tile_start, tile_end, tile_rows, n_valid, w_gate, w_up, w_down, xs)


COMB_SUB = 32


def _combine_kernel(wts_ref, h1_ref, g_ref, wsg_ref, wsu_ref, wsd_ref, g2_ref, b2_ref, o_ref, routed_ref):
    tn = h1_ref.shape[0]
    for s0 in range(0, tn, COMB_SUB):
        acc = [jnp.zeros((COMB_SUB, LANES), F32) for _ in range(2 * ROW_CHUNKS)]
        for kk in range(TOP_K):
            wk = jnp.broadcast_to(wts_ref[s0:s0 + COMB_SUB, kk:kk + 1], (COMB_SUB, LANES))
            for cc in range(ROW_CHUNKS):
                lo, hi = _unpack_rows(g_ref[kk, pl.ds(s0 * ROW_CHUNKS + cc, COMB_SUB, stride=ROW_CHUNKS), :])
                acc[cc] = acc[cc] + wk * lo
                acc[ROW_CHUNKS + cc] = acc[ROW_CHUNKS + cc] + wk * hi
        routed_ref[s0:s0 + COMB_SUB, :] = jnp.concatenate(acc, axis=1)

    h1 = h1_ref[...]
    hb = h1.astype(BF16)
    sg = jnp.dot(hb, wsg_ref[...], preferred_element_type=F32)
    su = jnp.dot(hb, wsu_ref[...], preferred_element_type=F32)
    ff = jnp.dot((sg * _sigmoid(sg) * su).astype(BF16), wsd_ref[...], preferred_element_type=F32)
    o_ref[...] = _layer_norm(ALPHA * h1 + ff + routed_ref[...], g2_ref[...], b2_ref[...])


def _combine_call(wts_t, h1, gathered, wsg, wsu, wsd, g2, b2):
    t = h1.shape[0]
    tn = TN_COMB
    row = lambda i: (i, 0)
    fix = lambda i: (0, 0)
    return pl.pallas_call(
        _combine_kernel,
        grid=(t // tn,),
        in_specs=[pl.BlockSpec((tn, TOP_K), row),
                  pl.BlockSpec((tn, D_MODEL), row),
                  pl.BlockSpec((TOP_K, tn * ROW_CHUNKS, LANES), lambda i: (0, i, 0)),
                  pl.BlockSpec((D_MODEL, SHARED_FF), fix), pl.BlockSpec((D_MODEL, SHARED_FF), fix),
                  pl.BlockSpec((SHARED_FF, D_MODEL), fix),
                  pl.BlockSpec((1, D_MODEL), fix), pl.BlockSpec((1, D_MODEL), fix)],
        out_specs=pl.BlockSpec((tn, D_MODEL), row),
        out_shape=SDS((t, D_MODEL), F32),
        scratch_shapes=[pltpu.VMEM((tn, D_MODEL), F32)],
        compiler_params=_cparams("arbitrary"),
        name="combine_ln2",
    )(wts_t, h1, gathered, wsg, wsu, wsd, g2, b2)


def kernel(x, meta_tokens, ln_in_g, ln_in_b, rel_bias, w_in, conv_w, conv_b, conv_ln_g, conv_ln_b, sinks,
           w_out, ln1_g, ln1_b, w_router, router_bias, w_gate, w_up, w_down, ws_gate, ws_up, ws_down,
           ln2_g, ln2_b):
    nbatch, seq, d = x.shape
    t = nbatch * seq
    assert d == D_MODEL and w_in.shape[0] == DEPTH
    assert seq % (ATTN_QBLOCKS * BLOCK) == 0 and seq % T_CONV == 0
    assert all(t % tile == 0 for tile in (TQ_PROJ, TQ_MIX, TN_ROUTE, TN_COMB))
    x2d = x.reshape(t, D_MODEL)
    vec = lambda a: a.reshape(1, -1).astype(F32)
    gin, bin_ = vec(ln_in_g), vec(ln_in_b)
    w_in_b = w_in[0].astype(BF16)

    q, k, v, u = _proj_call(x2d, gin, bin_, w_in_b, TQ_PROJ)
    meta_blk = jnp.concatenate([jnp.zeros((PAD_FRONT, D_MODEL), F32), meta_tokens.astype(F32)], axis=0)
    _, k_meta, v_meta, u_meta = _proj_call(meta_blk, gin, bin_, w_in_b, BLOCK)

    attn = _attn_call(q, k, v, k_meta, v_meta, _rel_bias_table(rel_bias), sinks[0].astype(F32),
                      nbatch, seq // BLOCK)

    u_halo = jnp.concatenate([jnp.zeros((CONV_HALO - N_META, CONV_CH), F32), u_meta[PAD_FRONT:]], axis=0)
    conv = _conv_call(u, u_halo, conv_w[0].astype(F32), vec(conv_b[0]), vec(conv_ln_g[0]), vec(conv_ln_b[0]),
                      nbatch, seq)

    w_out_b = w_out[0].astype(BF16)
    wr_t = w_router[0].astype(F32).T
    wr_hi = wr_t.astype(BF16)
    wr_lo = (wr_t - wr_hi.astype(F32)).astype(BF16)
    h1, h1rows, logits = _mix_call(x2d, attn, conv, gin, bin_, w_out_b[:ATTN_W], w_out_b[ATTN_W:],
                                   vec(ln1_g[0]), vec(ln1_b[0]), wr_hi, wr_lo)

    idx, wts_t, rank, cnt = _route_call(logits, router_bias[0].astype(F32).reshape(N_EXPERTS, 1))

    tm = TM_EXP
    n_tiles = (t * TOP_K) // tm + N_EXPERTS
    counts = cnt[:, 0]
    tiles_e = (counts + tm - 1) // tm
    tile_end = jnp.cumsum(tiles_e).astype(I32)
    tile_start = (tile_end - tiles_e).astype(I32)
    offs = tile_start * tm
    tile_id = jnp.arange(n_tiles, dtype=I32)
    lo = jnp.maximum(tile_id[:, None] * tm, offs[None, :])
    hi = jnp.minimum((tile_id[:, None] + 1) * tm, (offs + counts)[None, :])
    tile_rows = jnp.sum(jnp.clip(hi - lo, 0, tm), axis=1).astype(I32)
    n_valid = tile_end[-1:]

    xs, dest = _sc_dispatch_call(idx, rank, offs, h1rows.reshape(t, ROW_CHUNKS, LANES), n_tiles * tm)
    xs = xs.reshape(n_tiles * tm * ROW_CHUNKS, LANES)
    ys = _expert_call(tile_start, tile_end, tile_rows, n_valid, xs, w_gate[0], w_up[0], w_down[0], n_tiles * tm)
    gathered = _sc_gather_call(dest, ys.reshape(n_tiles * tm, ROW_CHUNKS, LANES))
    gathered = gathered.reshape(TOP_K, t * ROW_CHUNKS, LANES)
    out = _combine_call(wts_t, h1, gathered, ws_gate[0].astype(BF16), ws_up[0].astype(BF16),
                        ws_down[0].astype(BF16), vec(ln2_g[0]), vec(ln2_b[0]))
    return out.reshape(nbatch, seq, D_MODEL)
```

```python
import functools
import math

import numpy as np
import jax
import jax.numpy as jnp
from jax import lax
from jax.experimental import pallas as pl
from jax.experimental.pallas import tpu as pltpu
from jax.experimental.pallas import tpu_sc as plsc

F32 = jnp.float32
BF16 = jnp.bfloat16
I32 = jnp.int32
U32 = jnp.uint32
SDS = jax.ShapeDtypeStruct

D_MODEL = 1024
HALF = D_MODEL // 2
LANES = 128
SUBLANES = 8
ROW_CHUNKS = HALF // LANES
N_META = 16
HEAD_DIM = 64
N_Q_HEADS = 8
N_KV_HEADS = 2
GQA_GROUP = N_Q_HEADS // N_KV_HEADS
ATTN_W = N_Q_HEADS * HEAD_DIM
KV_W = N_KV_HEADS * HEAD_DIM
WINDOW = 128
BLOCK = 128
CONV_CH = D_MODEL - ATTN_W
CONV_K = 31
IN_W = ATTN_W + 2 * KV_W + 2 * CONV_CH
NUM_BUCKETS = 32
MAX_EXACT = NUM_BUCKETS // 2
REL_MAX_DIST = 128
N_EXPERTS = 256
TOP_K = 8
N_GROUPS = 8
GROUP_SIZE = N_EXPERTS // N_GROUPS
TOPK_GROUPS = 4
EXPERT_FF = 256
SHARED_FF = 256
ROUTED_SCALE = 2.5
DEPTH = 1
ALPHA = (2.0 * DEPTH) ** 0.25
LN_EPS = 1e-5
NEG = -1e30
PAD_FRONT = (-N_META) % BLOCK

VMEM_LIMIT = 48 * 1024 * 1024

TQ_PROJ = 1024
PROJ_CHAINS = 4
ATTN_QBLOCKS = 2
T_CONV = 256
CONV_HALO = 32
R_CONV = 64
TQ_MIX = 1024
MIX_CHAINS = 4
TN_ROUTE = 256
TM_EXP = 256
X_SLOTS = 8
X_AHEAD = 4
Y_SLOTS = 4
W_SLOTS = 3
W_AHEAD = 2
TN_COMB = 512


def _cparams(*sem):
    return pltpu.CompilerParams(dimension_semantics=sem, vmem_limit_bytes=VMEM_LIMIT)


def _layer_norm(x, g, b):
    mu = jnp.mean(x, axis=-1, keepdims=True)
    xc = x - mu
    var = jnp.mean(xc * xc, axis=-1, keepdims=True)
    return xc * lax.rsqrt(var + LN_EPS) * g + b


def _sigmoid(x):
    return 1.0 / (1.0 + jnp.exp(-x))


def _pack_rows(lo_half, hi_half):
    lo = lax.bitcast_convert_type(lo_half.astype(BF16).astype(F32), U32)
    hi = lax.bitcast_convert_type(hi_half.astype(BF16).astype(F32), U32)
    return lax.shift_right_logical(lo, jnp.uint32(16)) | hi


def _unpack_rows(p):
    lo = lax.bitcast_convert_type(lax.shift_left(p, jnp.uint32(16)), F32)
    hi = lax.bitcast_convert_type(p & jnp.uint32(0xFFFF0000), F32)
    return lo, hi


def _chunk_index(start, j, n, lead):
    rows = pl.ds(start + j, n, stride=ROW_CHUNKS)
    return (rows, slice(None)) if lead is None else (lead, rows, slice(None))


def _store_packed(ref, start, n, packed, lead=None):
    for j in range(ROW_CHUNKS):
        ref[_chunk_index(start, j, n, lead)] = packed[:, j * LANES:(j + 1) * LANES]


def _load_packed_bf16(ref, start, n, lead=None):
    halves = [_unpack_rows(ref[_chunk_index(start, j, n, lead)]) for j in range(ROW_CHUNKS)]
    return jnp.concatenate([h[0] for h in halves] + [h[1] for h in halves], axis=1).astype(BF16)


def _proj_kernel(chains, x_ref, g_ref, b_ref, w_ref, q_ref, k_ref, v_ref, u_ref):
    rows = x_ref.shape[0] // chains
    for c in range(chains):
        r = slice(c * rows, (c + 1) * rows)
        h = _layer_norm(x_ref[r, :], g_ref[...], b_ref[...])
        p = jnp.dot(h.astype(BF16), w_ref[...], preferred_element_type=F32)
        q_ref[r, :] = (p[:, :ATTN_W] * (HEAD_DIM ** -0.5)).astype(BF16)
        k_ref[r, :] = p[:, ATTN_W:ATTN_W + KV_W].astype(BF16)
        v_ref[r, :] = p[:, ATTN_W + KV_W:ATTN_W + 2 * KV_W].astype(BF16)
        a = p[:, ATTN_W + 2 * KV_W:ATTN_W + 2 * KV_W + CONV_CH]
        gate = p[:, ATTN_W + 2 * KV_W + CONV_CH:]
        u_ref[r, :] = a * _sigmoid(gate)


def _proj_call(x2d, gin, bin_, w_in_b, tq):
    t = x2d.shape[0]
    row = lambda i: (i, 0)
    fix = lambda i: (0, 0)
    chains = PROJ_CHAINS if tq % (PROJ_CHAINS * BLOCK) == 0 else 1
    return pl.pallas_call(
        functools.partial(_proj_kernel, chains),
        grid=(t // tq,),
        in_specs=[pl.BlockSpec((tq, D_MODEL), row), pl.BlockSpec((1, D_MODEL), fix),
                  pl.BlockSpec((1, D_MODEL), fix), pl.BlockSpec((D_MODEL, IN_W), fix)],
        out_specs=[pl.BlockSpec((tq, ATTN_W), row), pl.BlockSpec((tq, KV_W), row),
                   pl.BlockSpec((tq, KV_W), row), pl.BlockSpec((tq, CONV_CH), row)],
        out_shape=[SDS((t, ATTN_W), BF16), SDS((t, KV_W), BF16), SDS((t, KV_W), BF16), SDS((t, CONV_CH), F32)],
        compiler_params=_cparams("arbitrary"),
        name="ln_in_proj",
    )(x2d, gin, bin_, w_in_b)


def _attn_kernel(sinks_ref, q_ref, kc_ref, kp_ref, vc_ref, vp_ref, km_ref, vm_ref, bias_ref, o_ref):
    first = pl.program_id(1) == 0
    kp = jnp.where(first, km_ref[...], kp_ref[...])
    vp = jnp.where(first, vm_ref[...], vp_ref[...])
    k = jnp.concatenate([kp, kc_ref[...]], axis=0)
    v = jnp.concatenate([vp, vc_ref[...]], axis=0)
    col = lax.broadcasted_iota(I32, (BLOCK, 2 * BLOCK), 1)
    pad_bias = jnp.where(jnp.logical_and(first, col < PAD_FRONT), NEG, 0.0).astype(F32)
    for a in range(ATTN_QBLOCKS):
        q = q_ref[a * BLOCK:(a + 1) * BLOCK, :]
        kw = k[a * BLOCK:(a + 2) * BLOCK, :]
        vw = v[a * BLOCK:(a + 2) * BLOCK, :]
        outs = []
        for h in range(N_Q_HEADS):
            g = h // GQA_GROUP
            qh = q[:, h * HEAD_DIM:(h + 1) * HEAD_DIM]
            kg = kw[:, g * HEAD_DIM:(g + 1) * HEAD_DIM]
            vg = vw[:, g * HEAD_DIM:(g + 1) * HEAD_DIM]
            s = lax.dot_general(qh, kg, (((1,), (1,)), ((), ())), preferred_element_type=F32)
            s = s + bias_ref[h]
            if a == 0:
                s = s + pad_bias
            sink = sinks_ref[h]
            m = jnp.maximum(jnp.max(s, axis=-1, keepdims=True), sink)
            p = jnp.exp(s - m)
            den = jnp.sum(p, axis=-1, keepdims=True) + jnp.exp(sink - m)
            o = jnp.dot(p.astype(BF16), vg, preferred_element_type=F32)
            outs.append(o / den)
        o_ref[a * BLOCK:(a + 1) * BLOCK, :] = jnp.concatenate(outs, axis=1).astype(BF16)


def _attn_call(q, k, v, k_meta, v_meta, bias, sinks, nbatch, nblk):
    t = q.shape[0]
    nq = ATTN_QBLOCKS
    assert nblk % nq == 0
    nstep = nblk // nq
    cur = lambda b, j: (b * nstep + j, 0)
    prev = lambda b, j: (jnp.maximum((b * nstep + j) * nq - 1, 0), 0)
    fix2 = lambda b, j: (0, 0)
    return pl.pallas_call(
        _attn_kernel,
        grid=(nbatch, nstep),
        in_specs=[pl.BlockSpec(memory_space=pltpu.SMEM),
                  pl.BlockSpec((nq * BLOCK, ATTN_W), cur),
                  pl.BlockSpec((nq * BLOCK, KV_W), cur), pl.BlockSpec((BLOCK, KV_W), prev),
                  pl.BlockSpec((nq * BLOCK, KV_W), cur), pl.BlockSpec((BLOCK, KV_W), prev),
                  pl.BlockSpec((BLOCK, KV_W), fix2), pl.BlockSpec((BLOCK, KV_W), fix2),
                  pl.BlockSpec((N_Q_HEADS, BLOCK, 2 * BLOCK), lambda b, j: (0, 0, 0))],
        out_specs=pl.BlockSpec((nq * BLOCK, ATTN_W), cur),
        out_shape=SDS((t, ATTN_W), BF16),
        compiler_params=_cparams("arbitrary", "arbitrary"),
        name="swa_attn",
    )(sinks, q, k, k, v, v, k_meta, v_meta, bias)


def _rel_bias_table(rel_bias):
    qi = np.arange(BLOCK, dtype=np.int32)[:, None]
    kj = np.arange(2 * BLOCK, dtype=np.int32)[None, :]
    dist = BLOCK + qi - kj
    dc = np.clip(dist, 0, WINDOW - 1)
    nf = np.maximum(dc, 1).astype(np.float32)
    large = MAX_EXACT + (np.log(nf / np.float32(MAX_EXACT)) / np.float32(math.log(REL_MAX_DIST / MAX_EXACT))
                         * np.float32(NUM_BUCKETS - MAX_EXACT)).astype(np.int32)
    large = np.minimum(large, NUM_BUCKETS - 1)
    bucket = np.where(dc < MAX_EXACT, dc, large)
    in_window = (dist >= 0) & (dist < WINDOW)
    onehot = (bucket.reshape(-1, 1) == np.arange(NUM_BUCKETS)[None, :]).astype(np.float32)
    bias = jnp.dot(jnp.asarray(onehot), rel_bias.astype(F32), precision=lax.Precision.HIGHEST)
    bias = jnp.transpose(bias.reshape(BLOCK, 2 * BLOCK, N_Q_HEADS), (2, 0, 1))
    return jnp.where(in_window[None], bias, NEG)


def _conv_kernel(uc_ref, up_ref, um_ref, w_ref, cb_ref, g_ref, b_ref, o_ref, s_ref, sh_ref):
    first = pl.program_id(1) == 0
    s_ref[0:CONV_HALO, :] = jnp.where(first, um_ref[...], up_ref[...])
    s_ref[CONV_HALO:CONV_HALO + T_CONV, :] = uc_ref[...]
    off = CONV_HALO - (CONV_K - 1)
    span = sh_ref.shape[1]
    for p in range(1, SUBLANES):
        sh_ref[p] = s_ref[p:p + span, :]
    for c in range(0, T_CONV, R_CONV):
        acc = jnp.zeros((R_CONV, CONV_CH), F32) + cb_ref[...]
        for kk in range(CONV_K):
            p, a = (off + kk) % SUBLANES, (off + kk) // SUBLANES * SUBLANES
            if p == 0:
                win = s_ref[c + a:c + a + R_CONV, :]
            else:
                win = sh_ref[p, c + a:c + a + R_CONV, :]
            acc = acc + win * w_ref[kk:kk + 1, :]
        y = _layer_norm(acc, g_ref[...], b_ref[...])
        o_ref[c:c + R_CONV, :] = (y * _sigmoid(y)).astype(BF16)


def _conv_call(u, u_meta_halo, conv_w, conv_b, g, b, nbatch, seq):
    t = u.shape[0]
    nj = seq // T_CONV
    per = T_CONV // CONV_HALO
    cur = lambda bb, j: (bb * nj + j, 0)
    prev = lambda bb, j: (jnp.maximum((bb * nj + j) * per - 1, 0), 0)
    fix = lambda bb, j: (0, 0)
    return pl.pallas_call(
        _conv_kernel,
        grid=(nbatch, nj),
        in_specs=[pl.BlockSpec((T_CONV, CONV_CH), cur), pl.BlockSpec((CONV_HALO, CONV_CH), prev),
                  pl.BlockSpec((CONV_HALO, CONV_CH), fix), pl.BlockSpec((CONV_K, CONV_CH), fix),
                  pl.BlockSpec((1, CONV_CH), fix), pl.BlockSpec((1, CONV_CH), fix), pl.BlockSpec((1, CONV_CH), fix)],
        out_specs=pl.BlockSpec((T_CONV, CONV_CH), cur),
        out_shape=SDS((t, CONV_CH), BF16),
        scratch_shapes=[pltpu.VMEM((CONV_HALO + T_CONV, CONV_CH), F32),
                        pltpu.VMEM((SUBLANES, T_CONV + CONV_HALO - SUBLANES, CONV_CH), F32)],
        compiler_params=_cparams("arbitrary", "arbitrary"),
        name="conv_ln",
    )(u, u, u_meta_halo, conv_w, conv_b, g, b)


def _mix_kernel(x_ref, at_ref, cv_ref, gin_ref, bin_ref, woa_ref, woc_ref, g1_ref, b1_ref,
                wrh_ref, wrl_ref, h1_ref, h1r_ref, lg_ref):
    rows = x_ref.shape[0] // MIX_CHAINS
    nt = (((1,), (1,)), ((), ()))
    for c in range(MIX_CHAINS):
        r = slice(c * rows, (c + 1) * rows)
        h = _layer_norm(x_ref[r, :], gin_ref[...], bin_ref[...])
        mix = (jnp.dot(at_ref[r, :], woa_ref[...], preferred_element_type=F32)
               + jnp.dot(cv_ref[r, :], woc_ref[...], preferred_element_type=F32))
        h1 = _layer_norm(ALPHA * h + mix, g1_ref[...], b1_ref[...])
        h1_ref[r, :] = h1
        _store_packed(h1r_ref, c * rows * ROW_CHUNKS, rows, _pack_rows(h1[:, :HALF], h1[:, HALF:]))
        hh = h1.astype(BF16)
        hl = (h1 - hh.astype(F32)).astype(BF16)
        lg = lax.dot_general(wrh_ref[...], hh, nt, preferred_element_type=F32)
        lg = lg + lax.dot_general(wrh_ref[...], hl, nt, preferred_element_type=F32)
        lg = lg + lax.dot_general(wrl_ref[...], hh, nt, preferred_element_type=F32)
        lg_ref[:, r] = lg


def _mix_call(x2d, attn, conv, gin, bin_, woa, woc, g1, b1, wrh, wrl):
    t = x2d.shape[0]
    tq = TQ_MIX
    row = lambda i: (i, 0)
    fix = lambda i: (0, 0)
    return pl.pallas_call(
        _mix_kernel,
        grid=(t // tq,),
        in_specs=[pl.BlockSpec((tq, D_MODEL), row), pl.BlockSpec((tq, ATTN_W), row), pl.BlockSpec((tq, CONV_CH), row),
                  pl.BlockSpec((1, D_MODEL), fix), pl.BlockSpec((1, D_MODEL), fix),
                  pl.BlockSpec((ATTN_W, D_MODEL), fix), pl.BlockSpec((CONV_CH, D_MODEL), fix),
                  pl.BlockSpec((1, D_MODEL), fix), pl.BlockSpec((1, D_MODEL), fix),
                  pl.BlockSpec((N_EXPERTS, D_MODEL), fix), pl.BlockSpec((N_EXPERTS, D_MODEL), fix)],
        out_specs=[pl.BlockSpec((tq, D_MODEL), row), pl.BlockSpec((tq * ROW_CHUNKS, LANES), row),
                   pl.BlockSpec((N_EXPERTS, tq), lambda i: (0, i))],
        out_shape=[SDS((t, D_MODEL), F32), SDS((t * ROW_CHUNKS, LANES), U32), SDS((N_EXPERTS, t), F32)],
        compiler_params=_cparams("arbitrary"),
        name="mix_ln1",
    )(x2d, attn, conv, gin, bin_, woa, woc, g1, b1, wrh, wrl)


def _first_argmax(x, rows, nrows):
    m = jnp.max(x, axis=0, keepdims=True)
    idx = jnp.min(jnp.where(x == m, rows, nrows), axis=0, keepdims=True)
    return m, idx


def _route_tile(logits, rbias, carry):
    tn = logits.shape[1]
    scores = _sigmoid(logits)
    choice = scores + rbias
    rows = lax.broadcasted_iota(I32, (N_EXPERTS, tn), 0)
    rows_g = lax.broadcasted_iota(I32, (GROUP_SIZE, tn), 0)
    rows_8 = lax.broadcasted_iota(I32, (N_GROUPS, tn), 0)

    gs = []
    for g in range(N_GROUPS):
        xg = choice[g * GROUP_SIZE:(g + 1) * GROUP_SIZE, :]
        m1, i1 = _first_argmax(xg, rows_g, GROUP_SIZE)
        m2 = jnp.max(jnp.where(rows_g == i1, -jnp.inf, xg), axis=0, keepdims=True)
        gs.append(m1 + m2)
    gsc = jnp.concatenate(gs, axis=0)
    gsel = jnp.zeros((N_GROUPS, tn), F32)
    for _ in range(TOPK_GROUPS):
        _, gi = _first_argmax(gsc, rows_8, N_GROUPS)
        hit = rows_8 == gi
        gsel = jnp.where(hit, 1.0, gsel)
        gsc = jnp.where(hit, -jnp.inf, gsc)
    emask = jnp.concatenate(
        [jnp.broadcast_to(gsel[g:g + 1, :], (GROUP_SIZE, tn)) for g in range(N_GROUPS)], axis=0)
    masked = jnp.where(emask > 0.5, choice, NEG)

    sel_all = jnp.zeros((N_EXPERTS, tn), F32)
    hits, idxs, ws = [], [], []
    for _ in range(TOP_K):
        _, ii = _first_argmax(masked, rows, N_EXPERTS)
        hit = rows == ii
        hits.append(hit)
        idxs.append(ii)
        ws.append(jnp.sum(jnp.where(hit, scores, 0.0), axis=0, keepdims=True))
        sel_all = jnp.where(hit, 1.0, sel_all)
        masked = jnp.where(hit, -jnp.inf, masked)
    wsum = ws[0]
    for w in ws[1:]:
        wsum = wsum + w
    idx = jnp.concatenate(idxs, axis=0)
    wts = jnp.concatenate([w / wsum * ROUTED_SCALE for w in ws], axis=0)

    r_i = lax.broadcasted_iota(I32, (tn, tn), 0)
    c_i = lax.broadcasted_iota(I32, (tn, tn), 1)
    upper = jnp.where(r_i < c_i, 1.0, 0.0).astype(BF16)
    sel_b = sel_all.astype(BF16)
    before = jnp.dot(sel_b, upper, preferred_element_type=F32)
    before = before + jnp.concatenate([carry] * (tn // LANES), axis=1)
    rank = jnp.concatenate(
        [jnp.sum(jnp.where(h, before, 0.0), axis=0, keepdims=True) for h in hits], axis=0).astype(I32)
    carry = carry + jnp.dot(sel_b, jnp.ones((tn, LANES), BF16), preferred_element_type=F32)
    return idx, wts, rank, carry


def _route_kernel(lg_ref, rb_ref, idx_ref, wts_ref, rank_ref, cnt_ref, carry_ref):
    @pl.when(pl.program_id(0) == 0)
    def _():
        carry_ref[...] = jnp.zeros_like(carry_ref)

    idx, wts, rank, carry = _route_tile(lg_ref[...], rb_ref[...], carry_ref[...])
    idx_ref[...] = idx
    wts_ref[...] = wts.T
    rank_ref[...] = rank
    carry_ref[...] = carry
    cnt_ref[...] = carry.astype(I32)


def _route_call(lg, rbias):
    t = lg.shape[1]
    tn = TN_ROUTE
    col = lambda i: (0, i)
    return pl.pallas_call(
        _route_kernel,
        grid=(t // tn,),
        in_specs=[pl.BlockSpec((N_EXPERTS, tn), col), pl.BlockSpec((N_EXPERTS, 1), lambda i: (0, 0))],
        out_specs=[pl.BlockSpec((TOP_K, tn), col), pl.BlockSpec((tn, TOP_K), lambda i: (i, 0)),
                   pl.BlockSpec((TOP_K, tn), col), pl.BlockSpec((N_EXPERTS, LANES), lambda i: (0, 0))],
        out_shape=[SDS((TOP_K, t), I32), SDS((t, TOP_K), F32), SDS((TOP_K, t), I32), SDS((N_EXPERTS, LANES), I32)],
        scratch_shapes=[pltpu.VMEM((N_EXPERTS, LANES), F32)],
        compiler_params=_cparams("arbitrary"),
        name="route",
    )(lg, rbias)


SC_CORES = 2
SC_SUBCORES = 16
SC_CHUNK = 128
SC_LANES = 16
SC_BUFS = 3


def _sc_worker_chunks(t):
    per_worker = t // (SC_CORES * SC_SUBCORES)
    assert per_worker % SC_CHUNK == 0
    return per_worker


def _sc_dispatch_call(idx, rank, offs, h1rows3, n_rows):
    t = idx.shape[1]
    per_worker = _sc_worker_chunks(t)
    mesh = plsc.VectorSubcoreMesh(core_axis_name="c", subcore_axis_name="s")

    @functools.partial(
        pl.kernel, mesh=mesh, out_type=[SDS((n_rows, ROW_CHUNKS, LANES), U32), SDS((TOP_K, t), I32)],
        scratch_types=[pltpu.VMEM((TOP_K, SC_CHUNK), I32), pltpu.VMEM((TOP_K, SC_CHUNK), I32),
                       pltpu.VMEM((N_EXPERTS,), I32), pltpu.VMEM((2, SC_CHUNK // 2, ROW_CHUNKS, LANES), U32),
                       pltpu.SemaphoreType.DMA((2,)), pltpu.SemaphoreType.DMA],
        compiler_params=pltpu.CompilerParams(needs_layout_passes=False),
        name="sc_dispatch")
    def body(h_hbm, idx_hbm, rank_hbm, offs_hbm, xs_hbm, dest_hbm, idx_v, rank_v, offs_v, rows_v, rsem, sem):
        wid = lax.axis_index("s") * SC_CORES + lax.axis_index("c")
        half = SC_CHUNK // 2
        pltpu.sync_copy(offs_hbm, offs_v)

        @pl.loop(0, per_worker // SC_CHUNK)
        def _(i):
            t0 = wid * per_worker + i * SC_CHUNK
            reads = [pltpu.async_copy(h_hbm.at[pl.ds(t0 + h * half, half)], rows_v.at[h], rsem.at[h])
                     for h in range(2)]
            pltpu.sync_copy(idx_hbm.at[:, pl.ds(t0, SC_CHUNK)], idx_v)
            pltpu.sync_copy(rank_hbm.at[:, pl.ds(t0, SC_CHUNK)], rank_v)
            for kk in range(TOP_K):
                @pl.loop(0, SC_CHUNK // SC_LANES)
                def _(c):
                    lanes = pl.ds(c * SC_LANES, SC_LANES)
                    idx_v[kk, lanes] = plsc.load_gather(offs_v, [idx_v[kk, lanes]]) + rank_v[kk, lanes]
            pltpu.sync_copy(idx_v, dest_hbm.at[:, pl.ds(t0, SC_CHUNK)])
            copies = []
            for h in range(2):
                reads[h].wait()
                copies += [pltpu.async_copy(rows_v.at[h], xs_hbm.at[idx_v.at[kk, pl.ds(h * half, half)]], sem)
                           for kk in range(TOP_K)]
            for c in copies:
                c.wait()

    return body(h1rows3, idx, rank, offs)


def _sc_gather_call(dest, ys3):
    t = dest.shape[1]
    per_worker = _sc_worker_chunks(t)
    mesh = plsc.VectorSubcoreMesh(core_axis_name="c", subcore_axis_name="s")

    half = SC_CHUNK // 2
    items = [(kk, h) for kk in range(TOP_K) for h in range(2)]

    @functools.partial(
        pl.kernel, mesh=mesh, out_type=SDS((TOP_K * t, ROW_CHUNKS, LANES), U32),
        scratch_types=[pltpu.VMEM((TOP_K, SC_CHUNK), I32), pltpu.VMEM((SC_BUFS, half, ROW_CHUNKS, LANES), U32),
                       pltpu.SemaphoreType.DMA((SC_BUFS,)), pltpu.SemaphoreType.DMA((SC_BUFS,))],
        name="sc_gather")
    def body(ys_hbm, dest_hbm, out_hbm, idx_v, rows_v, gsem, wsem):
        wid = lax.axis_index("s") * SC_CORES + lax.axis_index("c")

        @pl.loop(0, per_worker // SC_CHUNK)
        def _(i):
            t0 = wid * per_worker + i * SC_CHUNK
            pltpu.sync_copy(dest_hbm.at[:, pl.ds(t0, SC_CHUNK)], idx_v)

            def gather(j):
                kk, h = items[j]
                b = j % SC_BUFS
                return pltpu.make_async_copy(ys_hbm.at[idx_v.at[kk, pl.ds(h * half, half)]], rows_v.at[b], gsem.at[b])

            def put(j):
                kk, h = items[j]
                b = j % SC_BUFS
                return pltpu.make_async_copy(rows_v.at[b], out_hbm.at[pl.ds(kk * t + t0 + h * half, half)], wsem.at[b])

            ahead = SC_BUFS - 1
            n = len(items)
            for j in range(ahead):
                gather(j).start()
            for j in range(n):
                if j + ahead < n:
                    if j >= 1:
                        put(j - 1).wait()
                    gather(j + ahead).start()
                gather(j).wait()
                put(j).start()
            for j in range(max(n - ahead - 1, 0), n):
                put(j).wait()

    return body(ys3, dest)


def _expert_kernel(ts_ref, te_ref, tr_ref, nv_ref, wg_hbm, wu_hbm, wd_hbm, xs_hbm, ys_hbm,
                   xbuf, ybuf, wg_f, wu_f, wd_f, wg_b, wu_b, wd_b, xsem, ysem, wsem):
    e = pl.program_id(0)
    rows = xbuf.shape[1]
    tm = rows // ROW_CHUNKS
    g0, g1, nv = ts_ref[e], te_ref[e], nv_ref[0]

    def x_copy(g):
        s = g % X_SLOTS
        return pltpu.make_async_copy(xs_hbm.at[pl.ds(pl.multiple_of(g * rows, rows), rows), :], xbuf.at[s], xsem.at[s])

    def y_copy(g):
        s = g % Y_SLOTS
        return pltpu.make_async_copy(ybuf.at[s], ys_hbm.at[pl.ds(pl.multiple_of(g * rows, rows), rows), :], ysem.at[s])

    def w_copies(ex):
        s = ex % W_SLOTS
        return (pltpu.make_async_copy(wg_hbm.at[ex], wg_f.at[s], wsem.at[s]),
                pltpu.make_async_copy(wu_hbm.at[ex], wu_f.at[s], wsem.at[s]),
                pltpu.make_async_copy(wd_hbm.at[ex], wd_f.at[s], wsem.at[s]))

    n_exp = pl.num_programs(0)

    @pl.when(e == 0)
    def _():
        for ex in range(W_AHEAD):
            for c in w_copies(ex):
                c.start()

    @pl.when(e + W_AHEAD < n_exp)
    def _():
        for c in w_copies(e + W_AHEAD):
            c.start()

    for c in w_copies(e):
        c.wait()

    def compute_tile(g):
        x = _load_packed_bf16(xbuf, 0, tm, lead=g % X_SLOTS)
        gate = jnp.dot(x, wg_b[...], preferred_element_type=F32)
        up = jnp.dot(x, wu_b[...], preferred_element_type=F32)
        live = lax.broadcasted_iota(I32, (tm, EXPERT_FF), 0) < tr_ref[g]
        hid = jnp.where(live, gate * _sigmoid(gate) * up, 0.0).astype(BF16)
        y = jnp.dot(hid, wd_b[...], preferred_element_type=F32)
        return _pack_rows(y[:, :HALF], y[:, HALF:])

    def run_tiles(g, n):
        for r in range(n):
            x_copy(g + r).wait()

            @pl.when(g + r + X_AHEAD < nv)
            def _():
                x_copy(g + r + X_AHEAD).start(priority=1)

            @pl.when(g + r >= Y_SLOTS)
            def _():
                y_copy(g + r - Y_SLOTS).wait()

        packed = [compute_tile(g + r) for r in range(n)]
        for r in range(n):
            _store_packed(ybuf, 0, tm, packed[r], lead=(g + r) % Y_SLOTS)
        for r in range(n):
            y_copy(g + r).start(priority=1)

    @pl.when(e == 0)
    def _():
        for g in range(X_AHEAD):
            @pl.when(g < nv)
            def _():
                x_copy(g).start(priority=1)

    @pl.when(g1 > g0)
    def _():
        ws = e % W_SLOTS
        wg_b[...] = wg_f[ws].astype(BF16)
        wu_b[...] = wu_f[ws].astype(BF16)
        wd_b[...] = wd_f[ws].astype(BF16)
        n_tiles = g1 - g0

        def pair(p, c):
            run_tiles(g0 + 2 * p, 2)
            return c

        lax.fori_loop(0, n_tiles // 2, pair, 0)

        @pl.when(n_tiles % 2 == 1)
        def _():
            run_tiles(g1 - 1, 1)

    @pl.when(e == pl.num_programs(0) - 1)
    def _():
        for back in range(1, Y_SLOTS + 1):
            @pl.when(nv >= back)
            def _():
                y_copy(nv - back).wait()


def _expert_call(tile_start, tile_end, tile_rows, n_valid, xs, w_gate, w_up, w_down, n_rows):
    tm = TM_EXP
    hbm = pl.BlockSpec(memory_space=pl.ANY)
    return pl.pallas_call(
        _expert_kernel,
        grid_spec=pltpu.PrefetchScalarGridSpec(
            num_scalar_prefetch=4,
            grid=(N_EXPERTS,),
            in_specs=[hbm, hbm, hbm, hbm],
            out_specs=hbm,
            scratch_shapes=[pltpu.VMEM((X_SLOTS, tm * ROW_CHUNKS, LANES), U32),
                            pltpu.VMEM((Y_SLOTS, tm * ROW_CHUNKS, LANES), U32),
                            pltpu.VMEM((W_SLOTS, D_MODEL, EXPERT_FF), F32), pltpu.VMEM((W_SLOTS, D_MODEL, EXPERT_FF), F32),
                            pltpu.VMEM((W_SLOTS, EXPERT_FF, D_MODEL), F32),
                            pltpu.VMEM((D_MODEL, EXPERT_FF), BF16), pltpu.VMEM((D_MODEL, EXPERT_FF), BF16),
                            pltpu.VMEM((EXPERT_FF, D_MODEL), BF16),
                            pltpu.SemaphoreType.DMA((X_SLOTS,)), pltpu.SemaphoreType.DMA((Y_SLOTS,)),
                            pltpu.SemaphoreType.DMA((W_SLOTS,))],
        ),
        out_shape=SDS((n_rows * ROW_CHUNKS, LANES), U32),
        compiler_params=_cparams("arbitrary"),
        name="experts",
    )(tile_start, tile_end, tile_rows, n_valid, w_gate, w_up, w_down, xs)


COMB_SUB = 32


def _combine_kernel(wts_ref, h1_ref, g_ref, wsg_ref, wsu_ref, wsd_ref, g2_ref, b2_ref, o_ref, routed_ref):
    tn = h1_ref.shape[0]
    for s0 in range(0, tn, COMB_SUB):
        acc = [jnp.zeros((COMB_SUB, LANES), F32) for _ in range(2 * ROW_CHUNKS)]
        for kk in range(TOP_K):
            wk = jnp.broadcast_to(wts_ref[s0:s0 + COMB_SUB, kk:kk + 1], (COMB_SUB, LANES))
            for cc in range(ROW_CHUNKS):
                lo, hi = _unpack_rows(g_ref[kk, pl.ds(s0 * ROW_CHUNKS + cc, COMB_SUB, stride=ROW_CHUNKS), :])
                acc[cc] = acc[cc] + wk * lo
                acc[ROW_CHUNKS + cc] = acc[ROW_CHUNKS + cc] + wk * hi
        routed_ref[s0:s0 + COMB_SUB, :] = jnp.concatenate(acc, axis=1)

    h1 = h1_ref[...]
    hb = h1.astype(BF16)
    sg = jnp.dot(hb, wsg_ref[...], preferred_element_type=F32)
    su = jnp.dot(hb, wsu_ref[...], preferred_element_type=F32)
    ff = jnp.dot((sg * _sigmoid(sg) * su).astype(BF16), wsd_ref[...], preferred_element_type=F32)
    o_ref[...] = _layer_norm(ALPHA * h1 + ff + routed_ref[...], g2_ref[...], b2_ref[...])


def _combine_call(wts_t, h1, gathered, wsg, wsu, wsd, g2, b2):
    t = h1.shape[0]
    tn = TN_COMB
    row = lambda i: (i, 0)
    fix = lambda i: (0, 0)
    return pl.pallas_call(
        _combine_kernel,
        grid=(t // tn,),
        in_specs=[pl.BlockSpec((tn, TOP_K), row),
                  pl.BlockSpec((tn, D_MODEL), row),
                  pl.BlockSpec((TOP_K, tn * ROW_CHUNKS, LANES), lambda i: (0, i, 0)),
                  pl.BlockSpec((D_MODEL, SHARED_FF), fix), pl.BlockSpec((D_MODEL, SHARED_FF), fix),
                  pl.BlockSpec((SHARED_FF, D_MODEL), fix),
                  pl.BlockSpec((1, D_MODEL), fix), pl.BlockSpec((1, D_MODEL), fix)],
        out_specs=pl.BlockSpec((tn, D_MODEL), row),
        out_shape=SDS((t, D_MODEL), F32),
        scratch_shapes=[pltpu.VMEM((tn, D_MODEL), F32)],
        compiler_params=_cparams("arbitrary"),
        name="combine_ln2",
    )(wts_t, h1, gathered, wsg, wsu, wsd, g2, b2)


def kernel(x, meta_tokens, ln_in_g, ln_in_b, rel_bias, w_in, conv_w, conv_b, conv_ln_g, conv_ln_b, sinks,
           w_out, ln1_g, ln1_b, w_router, router_bias, w_gate, w_up, w_down, ws_gate, ws_up, ws_down,
           ln2_g, ln2_b):
    nbatch, seq, d = x.shape
    t = nbatch * seq
    assert d == D_MODEL and w_in.shape[0] == DEPTH
    assert seq % (ATTN_QBLOCKS * BLOCK) == 0 and seq % T_CONV == 0
    assert all(t % tile == 0 for tile in (TQ_PROJ, TQ_MIX, TN_ROUTE, TN_COMB))
    x2d = x.reshape(t, D_MODEL)
    vec = lambda a: a.reshape(1, -1).astype(F32)
    gin, bin_ = vec(ln_in_g), vec(ln_in_b)
    w_in_b = w_in[0].astype(BF16)

    q, k, v, u = _proj_call(x2d, gin, bin_, w_in_b, TQ_PROJ)
    meta_blk = jnp.concatenate([jnp.zeros((PAD_FRONT, D_MODEL), F32), meta_tokens.astype(F32)], axis=0)
    _, k_meta, v_meta, u_meta = _proj_call(meta_blk, gin, bin_, w_in_b, BLOCK)

    attn = _attn_call(q, k, v, k_meta, v_meta, _rel_bias_table(rel_bias), sinks[0].astype(F32),
                      nbatch, seq // BLOCK)

    u_halo = jnp.concatenate([jnp.zeros((CONV_HALO - N_META, CONV_CH), F32), u_meta[PAD_FRONT:]], axis=0)
    conv = _conv_call(u, u_halo, conv_w[0].astype(F32), vec(conv_b[0]), vec(conv_ln_g[0]), vec(conv_ln_b[0]),
                      nbatch, seq)

    w_out_b = w_out[0].astype(BF16)
    wr_t = w_router[0].astype(F32).T
    wr_hi = wr_t.astype(BF16)
    wr_lo = (wr_t - wr_hi.astype(F32)).astype(BF16)
    h1, h1rows, logits = _mix_call(x2d, attn, conv, gin, bin_, w_out_b[:ATTN_W], w_out_b[ATTN_W:],
                                   vec(ln1_g[0]), vec(ln1_b[0]), wr_hi, wr_lo)

    idx, wts_t, rank, cnt = _route_call(logits, router_bias[0].astype(F32).reshape(N_EXPERTS, 1))

    tm = TM_EXP
    n_tiles = (t * TOP_K) // tm + N_EXPERTS
    counts = cnt[:, 0]
    tiles_e = (counts + tm - 1) // tm
    tile_end = jnp.cumsum(tiles_e).astype(I32)
    tile_start = (tile_end - tiles_e).astype(I32)
    offs = tile_start * tm
    tile_id = jnp.arange(n_tiles, dtype=I32)
    lo = jnp.maximum(tile_id[:, None] * tm, offs[None, :])
    hi = jnp.minimum((tile_id[:, None] + 1) * tm, (offs + counts)[None, :])
    tile_rows = jnp.sum(jnp.clip(hi - lo, 0, tm), axis=1).astype(I32)
    n_valid = tile_end[-1:]

    xs, dest = _sc_dispatch_call(idx, rank, offs, h1rows.reshape(t, ROW_CHUNKS, LANES), n_tiles * tm)
    xs = xs.reshape(n_tiles * tm * ROW_CHUNKS, LANES)
    ys = _expert_call(tile_start, tile_end, tile_rows, n_valid, xs, w_gate[0], w_up[0], w_down[0], n_tiles * tm)
    gathered = _sc_gather_call(dest, ys.reshape(n_tiles * tm, ROW_CHUNKS, LANES))
    gathered = gathered.reshape(TOP_K, t * ROW_CHUNKS, LANES)
    out = _combine_call(wts_t, h1, gathered, ws_gate[0].astype(BF16), ws_up[0].astype(BF16),
                        ws_down[0].astype(BF16), vec(ln2_g[0]), vec(ln2_b[0]))
    return out.reshape(nbatch, seq, D_MODEL)
```

```python
import functools
import math

import numpy as np
import jax
import jax.numpy as jnp
from jax import lax
from jax.experimental import pallas as pl
from jax.experimental.pallas import tpu as pltpu
from jax.experimental.pallas import tpu_sc as plsc

F32 = jnp.float32
BF16 = jnp.bfloat16
I32 = jnp.int32
U32 = jnp.uint32
SDS = jax.ShapeDtypeStruct

D_MODEL = 1024
HALF = D_MODEL // 2
LANES = 128
SUBLANES = 8
ROW_CHUNKS = HALF // LANES
N_META = 16
HEAD_DIM = 64
N_Q_HEADS = 8
N_KV_HEADS = 2
GQA_GROUP = N_Q_HEADS // N_KV_HEADS
ATTN_W = N_Q_HEADS * HEAD_DIM
KV_W = N_KV_HEADS * HEAD_DIM
WINDOW = 128
BLOCK = 128
CONV_CH = D_MODEL - ATTN_W
CONV_K = 31
IN_W = ATTN_W + 2 * KV_W + 2 * CONV_CH
NUM_BUCKETS = 32
MAX_EXACT = NUM_BUCKETS // 2
REL_MAX_DIST = 128
N_EXPERTS = 256
TOP_K = 8
N_GROUPS = 8
GROUP_SIZE = N_EXPERTS // N_GROUPS
TOPK_GROUPS = 4
EXPERT_FF = 256
SHARED_FF = 256
ROUTED_SCALE = 2.5
DEPTH = 1
ALPHA = (2.0 * DEPTH) ** 0.25
LN_EPS = 1e-5
NEG = -1e30
PAD_FRONT = (-N_META) % BLOCK

VMEM_LIMIT = 48 * 1024 * 1024

TQ_PROJ = 1024
PROJ_CHAINS = 4
ATTN_QBLOCKS = 2
T_CONV = 256
CONV_HALO = 32
R_CONV = 64
TQ_MIX = 1024
MIX_CHAINS = 4
TN_ROUTE = 256
TM_EXP = 256
X_SLOTS = 8
X_AHEAD = 4
Y_SLOTS = 4
W_SLOTS = 3
W_AHEAD = 2
TN_COMB = 512


def _cparams(*sem):
    return pltpu.CompilerParams(dimension_semantics=sem, vmem_limit_bytes=VMEM_LIMIT)


def _layer_norm(x, g, b):
    mu = jnp.mean(x, axis=-1, keepdims=True)
    xc = x - mu
    var = jnp.mean(xc * xc, axis=-1, keepdims=True)
    return xc * lax.rsqrt(var + LN_EPS) * g + b


def _sigmoid(x):
    return 1.0 / (1.0 + jnp.exp(-x))


def _pack_rows(lo_half, hi_half):
    lo = lax.bitcast_convert_type(lo_half.astype(BF16).astype(F32), U32)
    hi = lax.bitcast_convert_type(hi_half.astype(BF16).astype(F32), U32)
    return lax.shift_right_logical(lo, jnp.uint32(16)) | hi


def _unpack_rows(p):
    lo = lax.bitcast_convert_type(lax.shift_left(p, jnp.uint32(16)), F32)
    hi = lax.bitcast_convert_type(p & jnp.uint32(0xFFFF0000), F32)
    return lo, hi


def _chunk_index(start, j, n, lead):
    rows = pl.ds(start + j, n, stride=ROW_CHUNKS)
    return (rows, slice(None)) if lead is None else (lead, rows, slice(None))


def _store_packed(ref, start, n, packed, lead=None):
    for j in range(ROW_CHUNKS):
        ref[_chunk_index(start, j, n, lead)] = packed[:, j * LANES:(j + 1) * LANES]


def _load_packed_bf16(ref, start, n, lead=None):
    halves = [_unpack_rows(ref[_chunk_index(start, j, n, lead)]) for j in range(ROW_CHUNKS)]
    return jnp.concatenate([h[0] for h in halves] + [h[1] for h in halves], axis=1).astype(BF16)


def _proj_kernel(chains, x_ref, g_ref, b_ref, w_ref, q_ref, k_ref, v_ref, u_ref):
    rows = x_ref.shape[0] // chains
    for c in range(chains):
        r = slice(c * rows, (c + 1) * rows)
        h = _layer_norm(x_ref[r, :], g_ref[...], b_ref[...])
        p = jnp.dot(h.astype(BF16), w_ref[...], preferred_element_type=F32)
        q_ref[r, :] = (p[:, :ATTN_W] * (HEAD_DIM ** -0.5)).astype(BF16)
        k_ref[r, :] = p[:, ATTN_W:ATTN_W + KV_W].astype(BF16)
        v_ref[r, :] = p[:, ATTN_W + KV_W:ATTN_W + 2 * KV_W].astype(BF16)
        a = p[:, ATTN_W + 2 * KV_W:ATTN_W + 2 * KV_W + CONV_CH]
        gate = p[:, ATTN_W + 2 * KV_W + CONV_CH:]
        u_ref[r, :] = a * _sigmoid(gate)


def _proj_call(x2d, gin, bin_, w_in_b, tq):
    t = x2d.shape[0]
    row = lambda i: (i, 0)
    fix = lambda i: (0, 0)
    chains = PROJ_CHAINS if tq % (PROJ_CHAINS * BLOCK) == 0 else 1
    return pl.pallas_call(
        functools.partial(_proj_kernel, chains),
        grid=(t // tq,),
        in_specs=[pl.BlockSpec((tq, D_MODEL), row), pl.BlockSpec((1, D_MODEL), fix),
                  pl.BlockSpec((1, D_MODEL), fix), pl.BlockSpec((D_MODEL, IN_W), fix)],
        out_specs=[pl.BlockSpec((tq, ATTN_W), row), pl.BlockSpec((tq, KV_W), row),
                   pl.BlockSpec((tq, KV_W), row), pl.BlockSpec((tq, CONV_CH), row)],
        out_shape=[SDS((t, ATTN_W), BF16), SDS((t, KV_W), BF16), SDS((t, KV_W), BF16), SDS((t, CONV_CH), F32)],
        compiler_params=_cparams("arbitrary"),
        name="ln_in_proj",
    )(x2d, gin, bin_, w_in_b)


def _attn_kernel(sinks_ref, q_ref, kc_ref, kp_ref, vc_ref, vp_ref, km_ref, vm_ref, bias_ref, o_ref):
    first = pl.program_id(1) == 0
    kp = jnp.where(first, km_ref[...], kp_ref[...])
    vp = jnp.where(first, vm_ref[...], vp_ref[...])
    k = jnp.concatenate([kp, kc_ref[...]], axis=0)
    v = jnp.concatenate([vp, vc_ref[...]], axis=0)
    col = lax.broadcasted_iota(I32, (BLOCK, 2 * BLOCK), 1)
    pad_bias = jnp.where(jnp.logical_and(first, col < PAD_FRONT), NEG, 0.0).astype(F32)
    for a in range(ATTN_QBLOCKS):
        q = q_ref[a * BLOCK:(a + 1) * BLOCK, :]
        kw = k[a * BLOCK:(a + 2) * BLOCK, :]
        vw = v[a * BLOCK:(a + 2) * BLOCK, :]
        outs = []
        for h in range(N_Q_HEADS):
            g = h // GQA_GROUP
            qh = q[:, h * HEAD_DIM:(h + 1) * HEAD_DIM]
            kg = kw[:, g * HEAD_DIM:(g + 1) * HEAD_DIM]
            vg = vw[:, g * HEAD_DIM:(g + 1) * HEAD_DIM]
            s = lax.dot_general(qh, kg, (((1,), (1,)), ((), ())), preferred_element_type=F32)
            s = s + bias_ref[h]
            if a == 0:
                s = s + pad_bias
            sink = sinks_ref[h]
            m = jnp.maximum(jnp.max(s, axis=-1, keepdims=True), sink)
            p = jnp.exp(s - m)
            den = jnp.sum(p, axis=-1, keepdims=True) + jnp.exp(sink - m)
            o = jnp.dot(p.astype(BF16), vg, preferred_element_type=F32)
            outs.append(o / den)
        o_ref[a * BLOCK:(a + 1) * BLOCK, :] = jnp.concatenate(outs, axis=1).astype(BF16)


def _attn_call(q, k, v, k_meta, v_meta, bias, sinks, nbatch, nblk):
    t = q.shape[0]
    nq = ATTN_QBLOCKS
    assert nblk % nq == 0
    nstep = nblk // nq
    cur = lambda b, j: (b * nstep + j, 0)
    prev = lambda b, j: (jnp.maximum((b * nstep + j) * nq - 1, 0), 0)
    fix2 = lambda b, j: (0, 0)
    return pl.pallas_call(
        _attn_kernel,
        grid=(nbatch, nstep),
        in_specs=[pl.BlockSpec(memory_space=pltpu.SMEM),
                  pl.BlockSpec((nq * BLOCK, ATTN_W), cur),
                  pl.BlockSpec((nq * BLOCK, KV_W), cur), pl.BlockSpec((BLOCK, KV_W), prev),
                  pl.BlockSpec((nq * BLOCK, KV_W), cur), pl.BlockSpec((BLOCK, KV_W), prev),
                  pl.BlockSpec((BLOCK, KV_W), fix2), pl.BlockSpec((BLOCK, KV_W), fix2),
                  pl.BlockSpec((N_Q_HEADS, BLOCK, 2 * BLOCK), lambda b, j: (0, 0, 0))],
        out_specs=pl.BlockSpec((nq * BLOCK, ATTN_W), cur),
        out_shape=SDS((t, ATTN_W), BF16),
        compiler_params=_cparams("arbitrary", "arbitrary"),
        name="swa_attn",
    )(sinks, q, k, k, v, v, k_meta, v_meta, bias)


def _rel_bias_table(rel_bias):
    qi = np.arange(BLOCK, dtype=np.int32)[:, None]
    kj = np.arange(2 * BLOCK, dtype=np.int32)[None, :]
    dist = BLOCK + qi - kj
    dc = np.clip(dist, 0, WINDOW - 1)
    nf = np.maximum(dc, 1).astype(np.float32)
    large = MAX_EXACT + (np.log(nf / np.float32(MAX_EXACT)) / np.float32(math.log(REL_MAX_DIST / MAX_EXACT))
                         * np.float32(NUM_BUCKETS - MAX_EXACT)).astype(np.int32)
    large = np.minimum(large, NUM_BUCKETS - 1)
    bucket = np.where(dc < MAX_EXACT, dc, large)
    in_window = (dist >= 0) & (dist < WINDOW)
    onehot = (bucket.reshape(-1, 1) == np.arange(NUM_BUCKETS)[None, :]).astype(np.float32)
    bias = jnp.dot(jnp.asarray(onehot), rel_bias.astype(F32), precision=lax.Precision.HIGHEST)
    bias = jnp.transpose(bias.reshape(BLOCK, 2 * BLOCK, N_Q_HEADS), (2, 0, 1))
    return jnp.where(in_window[None], bias, NEG)


def _conv_kernel(uc_ref, up_ref, um_ref, w_ref, cb_ref, g_ref, b_ref, o_ref, s_ref, sh_ref):
    first = pl.program_id(1) == 0
    s_ref[0:CONV_HALO, :] = jnp.where(first, um_ref[...], up_ref[...])
    s_ref[CONV_HALO:CONV_HALO + T_CONV, :] = uc_ref[...]
    off = CONV_HALO - (CONV_K - 1)
    span = sh_ref.shape[1]
    for p in range(1, SUBLANES):
        sh_ref[p] = s_ref[p:p + span, :]
    for c in range(0, T_CONV, R_CONV):
        acc = jnp.zeros((R_CONV, CONV_CH), F32) + cb_ref[...]
        for kk in range(CONV_K):
            p, a = (off + kk) % SUBLANES, (off + kk) // SUBLANES * SUBLANES
            if p == 0:
                win = s_ref[c + a:c + a + R_CONV, :]
            else:
                win = sh_ref[p, c + a:c + a + R_CONV, :]
            acc = acc + win * w_ref[kk:kk + 1, :]
        y = _layer_norm(acc, g_ref[...], b_ref[...])
        o_ref[c:c + R_CONV, :] = (y * _sigmoid(y)).astype(BF16)


def _conv_call(u, u_meta_halo, conv_w, conv_b, g, b, nbatch, seq):
    t = u.shape[0]
    nj = seq // T_CONV
    per = T_CONV // CONV_HALO
    cur = lambda bb, j: (bb * nj + j, 0)
    prev = lambda bb, j: (jnp.maximum((bb * nj + j) * per - 1, 0), 0)
    fix = lambda bb, j: (0, 0)
    return pl.pallas_call(
        _conv_kernel,
        grid=(nbatch, nj),
        in_specs=[pl.BlockSpec((T_CONV, CONV_CH), cur), pl.BlockSpec((CONV_HALO, CONV_CH), prev),
                  pl.BlockSpec((CONV_HALO, CONV_CH), fix), pl.BlockSpec((CONV_K, CONV_CH), fix),
                  pl.BlockSpec((1, CONV_CH), fix), pl.BlockSpec((1, CONV_CH), fix), pl.BlockSpec((1, CONV_CH), fix)],
        out_specs=pl.BlockSpec((T_CONV, CONV_CH), cur),
        out_shape=SDS((t, CONV_CH), BF16),
        scratch_shapes=[pltpu.VMEM((CONV_HALO + T_CONV, CONV_CH), F32),
                        pltpu.VMEM((SUBLANES, T_CONV + CONV_HALO - SUBLANES, CONV_CH), F32)],
        compiler_params=_cparams("arbitrary", "arbitrary"),
        name="conv_ln",
    )(u, u, u_meta_halo, conv_w, conv_b, g, b)


def _mix_kernel(x_ref, at_ref, cv_ref, gin_ref, bin_ref, woa_ref, woc_ref, g1_ref, b1_ref,
                wrh_ref, wrl_ref, h1_ref, h1r_ref, lg_ref):
    rows = x_ref.shape[0] // MIX_CHAINS
    nt = (((1,), (1,)), ((), ()))
    for c in range(MIX_CHAINS):
        r = slice(c * rows, (c + 1) * rows)
        h = _layer_norm(x_ref[r, :], gin_ref[...], bin_ref[...])
        mix = (jnp.dot(at_ref[r, :], woa_ref[...], preferred_element_type=F32)
               + jnp.dot(cv_ref[r, :], woc_ref[...], preferred_element_type=F32))
        h1 = _layer_norm(ALPHA * h + mix, g1_ref[...], b1_ref[...])
        h1_ref[r, :] = h1
        _store_packed(h1r_ref, c * rows * ROW_CHUNKS, rows, _pack_rows(h1[:, :HALF], h1[:, HALF:]))
        hh = h1.astype(BF16)
        hl = (h1 - hh.astype(F32)).astype(BF16)
        lg = lax.dot_general(wrh_ref[...], hh, nt, preferred_element_type=F32)
        lg = lg + lax.dot_general(wrh_ref[...], hl, nt, preferred_element_type=F32)
        lg = lg + lax.dot_general(wrl_ref[...], hh, nt, preferred_element_type=F32)
        lg_ref[:, r] = lg


def _mix_call(x2d, attn, conv, gin, bin_, woa, woc, g1, b1, wrh, wrl):
    t = x2d.shape[0]
    tq = TQ_MIX
    row = lambda i: (i, 0)
    fix = lambda i: (0, 0)
    return pl.pallas_call(
        _mix_kernel,
        grid=(t // tq,),
        in_specs=[pl.BlockSpec((tq, D_MODEL), row), pl.BlockSpec((tq, ATTN_W), row), pl.BlockSpec((tq, CONV_CH), row),
                  pl.BlockSpec((1, D_MODEL), fix), pl.BlockSpec((1, D_MODEL), fix),
                  pl.BlockSpec((ATTN_W, D_MODEL), fix), pl.BlockSpec((CONV_CH, D_MODEL), fix),
                  pl.BlockSpec((1, D_MODEL), fix), pl.BlockSpec((1, D_MODEL), fix),
                  pl.BlockSpec((N_EXPERTS, D_MODEL), fix), pl.BlockSpec((N_EXPERTS, D_MODEL), fix)],
        out_specs=[pl.BlockSpec((tq, D_MODEL), row), pl.BlockSpec((tq * ROW_CHUNKS, LANES), row),
                   pl.BlockSpec((N_EXPERTS, tq), lambda i: (0, i))],
        out_shape=[SDS((t, D_MODEL), F32), SDS((t * ROW_CHUNKS, LANES), U32), SDS((N_EXPERTS, t), F32)],
        compiler_params=_cparams("arbitrary"),
        name="mix_ln1",
    )(x2d, attn, conv, gin, bin_, woa, woc, g1, b1, wrh, wrl)


def _first_argmax(x, rows, nrows):
    m = jnp.max(x, axis=0, keepdims=True)
    idx = jnp.min(jnp.where(x == m, rows, nrows), axis=0, keepdims=True)
    return m, idx


def _route_tile(logits, rbias, carry):
    tn = logits.shape[1]
    scores = _sigmoid(logits)
    choice = scores + rbias
    rows = lax.broadcasted_iota(I32, (N_EXPERTS, tn), 0)
    rows_g = lax.broadcasted_iota(I32, (GROUP_SIZE, tn), 0)
    rows_8 = lax.broadcasted_iota(I32, (N_GROUPS, tn), 0)

    gs = []
    for g in range(N_GROUPS):
        xg = choice[g * GROUP_SIZE:(g + 1) * GROUP_SIZE, :]
        m1, i1 = _first_argmax(xg, rows_g, GROUP_SIZE)
        m2 = jnp.max(jnp.where(rows_g == i1, -jnp.inf, xg), axis=0, keepdims=True)
        gs.append(m1 + m2)
    gsc = jnp.concatenate(gs, axis=0)
    gsel = jnp.zeros((N_GROUPS, tn), F32)
    for _ in range(TOPK_GROUPS):
        _, gi = _first_argmax(gsc, rows_8, N_GROUPS)
        hit = rows_8 == gi
        gsel = jnp.where(hit, 1.0, gsel)
        gsc = jnp.where(hit, -jnp.inf, gsc)
    emask = jnp.concatenate(
        [jnp.broadcast_to(gsel[g:g + 1, :], (GROUP_SIZE, tn)) for g in range(N_GROUPS)], axis=0)
    masked = jnp.where(emask > 0.5, choice, NEG)

    sel_all = jnp.zeros((N_EXPERTS, tn), F32)
    hits, idxs, ws = [], [], []
    for _ in range(TOP_K):
        _, ii = _first_argmax(masked, rows, N_EXPERTS)
        hit = rows == ii
        hits.append(hit)
        idxs.append(ii)
        ws.append(jnp.sum(jnp.where(hit, scores, 0.0), axis=0, keepdims=True))
        sel_all = jnp.where(hit, 1.0, sel_all)
        masked = jnp.where(hit, -jnp.inf, masked)
    wsum = ws[0]
    for w in ws[1:]:
        wsum = wsum + w
    idx = jnp.concatenate(idxs, axis=0)
    wts = jnp.concatenate([w / wsum * ROUTED_SCALE for w in ws], axis=0)

    r_i = lax.broadcasted_iota(I32, (tn, tn), 0)
    c_i = lax.broadcasted_iota(I32, (tn, tn), 1)
    upper = jnp.where(r_i < c_i, 1.0, 0.0).astype(BF16)
    sel_b = sel_all.astype(BF16)
    before = jnp.dot(sel_b, upper, preferred_element_type=F32)
    before = before + jnp.concatenate([carry] * (tn // LANES), axis=1)
    rank = jnp.concatenate(
        [jnp.sum(jnp.where(h, before, 0.0), axis=0, keepdims=True) for h in hits], axis=0).astype(I32)
    carry = carry + jnp.dot(sel_b, jnp.ones((tn, LANES), BF16), preferred_element_type=F32)
    return idx, wts, rank, carry


def _route_kernel(lg_ref, rb_ref, idx_ref, wts_ref, rank_ref, cnt_ref, carry_ref):
    @pl.when(pl.program_id(0) == 0)
    def _():
        carry_ref[...] = jnp.zeros_like(carry_ref)

    idx, wts, rank, carry = _route_tile(lg_ref[...], rb_ref[...], carry_ref[...])
    idx_ref[...] = idx
    wts_ref[...] = wts.T
    rank_ref[...] = rank
    carry_ref[...] = carry
    cnt_ref[...] = carry.astype(I32)


def _route_call(lg, rbias):
    t = lg.shape[1]
    tn = TN_ROUTE
    col = lambda i: (0, i)
    return pl.pallas_call(
        _route_kernel,
        grid=(t // tn,),
        in_specs=[pl.BlockSpec((N_EXPERTS, tn), col), pl.BlockSpec((N_EXPERTS, 1), lambda i: (0, 0))],
        out_specs=[pl.BlockSpec((TOP_K, tn), col), pl.BlockSpec((tn, TOP_K), lambda i: (i, 0)),
                   pl.BlockSpec((TOP_K, tn), col), pl.BlockSpec((N_EXPERTS, LANES), lambda i: (0, 0))],
        out_shape=[SDS((TOP_K, t), I32), SDS((t, TOP_K), F32), SDS((TOP_K, t), I32), SDS((N_EXPERTS, LANES), I32)],
        scratch_shapes=[pltpu.VMEM((N_EXPERTS, LANES), F32)],
        compiler_params=_cparams("arbitrary"),
        name="route",
    )(lg, rbias)


SC_CORES = 2
SC_SUBCORES = 16
SC_CHUNK = 128
SC_LANES = 16
SC_BUFS = 2


def _sc_worker_chunks(t):
    per_worker = t // (SC_CORES * SC_SUBCORES)
    assert per_worker % SC_CHUNK == 0
    return per_worker


def _sc_dispatch_call(idx, rank, offs, h1rows3, n_rows):
    t = idx.shape[1]
    per_worker = _sc_worker_chunks(t)
    mesh = plsc.VectorSubcoreMesh(core_axis_name="c", subcore_axis_name="s")

    @functools.partial(
        pl.kernel, mesh=mesh, out_type=[SDS((n_rows, ROW_CHUNKS, LANES), U32), SDS((TOP_K, t), I32)],
        scratch_types=[pltpu.VMEM((TOP_K, SC_CHUNK), I32), pltpu.VMEM((TOP_K, SC_CHUNK), I32),
                       pltpu.VMEM((N_EXPERTS,), I32), pltpu.VMEM((SC_CHUNK, ROW_CHUNKS, LANES), U32),
                       pltpu.SemaphoreType.DMA],
        compiler_params=pltpu.CompilerParams(needs_layout_passes=False),
        name="sc_dispatch")
    def body(h_hbm, idx_hbm, rank_hbm, offs_hbm, xs_hbm, dest_hbm, idx_v, rank_v, offs_v, rows_v, sem):
        wid = lax.axis_index("s") * SC_CORES + lax.axis_index("c")
        pltpu.sync_copy(offs_hbm, offs_v)

        @pl.loop(0, per_worker // SC_CHUNK)
        def _(i):
            t0 = wid * per_worker + i * SC_CHUNK
            pltpu.sync_copy(idx_hbm.at[:, pl.ds(t0, SC_CHUNK)], idx_v)
            pltpu.sync_copy(rank_hbm.at[:, pl.ds(t0, SC_CHUNK)], rank_v)
            pltpu.sync_copy(h_hbm.at[pl.ds(t0, SC_CHUNK)], rows_v)
            for kk in range(TOP_K):
                @pl.loop(0, SC_CHUNK // SC_LANES)
                def _(c):
                    lanes = pl.ds(c * SC_LANES, SC_LANES)
                    idx_v[kk, lanes] = plsc.load_gather(offs_v, [idx_v[kk, lanes]]) + rank_v[kk, lanes]
            pltpu.sync_copy(idx_v, dest_hbm.at[:, pl.ds(t0, SC_CHUNK)])
            copies = [pltpu.async_copy(rows_v, xs_hbm.at[idx_v.at[kk]], sem) for kk in range(TOP_K)]
            for c in copies:
                c.wait()

    return body(h1rows3, idx, rank, offs)


def _sc_gather_call(dest, ys3):
    t = dest.shape[1]
    per_worker = _sc_worker_chunks(t)
    mesh = plsc.VectorSubcoreMesh(core_axis_name="c", subcore_axis_name="s")

    half = SC_CHUNK // 2
    items = [(kk, h) for kk in range(TOP_K) for h in range(2)]

    @functools.partial(
        pl.kernel, mesh=mesh, out_type=SDS((TOP_K * t, ROW_CHUNKS, LANES), U32),
        scratch_types=[pltpu.VMEM((TOP_K, SC_CHUNK), I32), pltpu.VMEM((SC_BUFS, half, ROW_CHUNKS, LANES), U32),
                       pltpu.SemaphoreType.DMA((SC_BUFS,)), pltpu.SemaphoreType.DMA((SC_BUFS,))],
        name="sc_gather")
    def body(ys_hbm, dest_hbm, out_hbm, idx_v, rows_v, gsem, wsem):
        wid = lax.axis_index("s") * SC_CORES + lax.axis_index("c")

        @pl.loop(0, per_worker // SC_CHUNK)
        def _(i):
            t0 = wid * per_worker + i * SC_CHUNK
            pltpu.sync_copy(dest_hbm.at[:, pl.ds(t0, SC_CHUNK)], idx_v)

            def gather(j):
                kk, h = items[j]
                b = j % SC_BUFS
                return pltpu.make_async_copy(ys_hbm.at[idx_v.at[kk, pl.ds(h * half, half)]], rows_v.at[b], gsem.at[b])

            def put(j):
                kk, h = items[j]
                b = j % SC_BUFS
                return pltpu.make_async_copy(rows_v.at[b], out_hbm.at[pl.ds(kk * t + t0 + h * half, half)], wsem.at[b])

            ahead = SC_BUFS - 1
            n = len(items)
            for j in range(ahead):
                gather(j).start()
            for j in range(n):
                if j + ahead < n:
                    if j >= 1:
                        put(j - 1).wait()
                    gather(j + ahead).start()
                gather(j).wait()
                put(j).start()
            for j in range(max(n - ahead - 1, 0), n):
                put(j).wait()

    return body(ys3, dest)


def _expert_kernel(ts_ref, te_ref, tr_ref, nv_ref, wg_hbm, wu_hbm, wd_hbm, xs_hbm, ys_hbm,
                   xbuf, ybuf, wg_f, wu_f, wd_f, wg_b, wu_b, wd_b, xsem, ysem, wsem):
    e = pl.program_id(0)
    rows = xbuf.shape[1]
    tm = rows // ROW_CHUNKS
    g0, g1, nv = ts_ref[e], te_ref[e], nv_ref[0]

    def x_copy(g):
        s = g % X_SLOTS
        return pltpu.make_async_copy(xs_hbm.at[pl.ds(pl.multiple_of(g * rows, rows), rows), :], xbuf.at[s], xsem.at[s])

    def y_copy(g):
        s = g % Y_SLOTS
        return pltpu.make_async_copy(ybuf.at[s], ys_hbm.at[pl.ds(pl.multiple_of(g * rows, rows), rows), :], ysem.at[s])

    def w_copies(ex):
        s = ex % W_SLOTS
        return (pltpu.make_async_copy(wg_hbm.at[ex], wg_f.at[s], wsem.at[s]),
                pltpu.make_async_copy(wu_hbm.at[ex], wu_f.at[s], wsem.at[s]),
                pltpu.make_async_copy(wd_hbm.at[ex], wd_f.at[s], wsem.at[s]))

    n_exp = pl.num_programs(0)

    @pl.when(e == 0)
    def _():
        for ex in range(W_AHEAD):
            for c in w_copies(ex):
                c.start()

    @pl.when(e + W_AHEAD < n_exp)
    def _():
        for c in w_copies(e + W_AHEAD):
            c.start()

    for c in w_copies(e):
        c.wait()

    def compute_tile(g):
        x = _load_packed_bf16(xbuf, 0, tm, lead=g % X_SLOTS)
        gate = jnp.dot(x, wg_b[...], preferred_element_type=F32)
        up = jnp.dot(x, wu_b[...], preferred_element_type=F32)
        live = lax.broadcasted_iota(I32, (tm, EXPERT_FF), 0) < tr_ref[g]
        hid = jnp.where(live, gate * _sigmoid(gate) * up, 0.0).astype(BF16)
        y = jnp.dot(hid, wd_b[...], preferred_element_type=F32)
        return _pack_rows(y[:, :HALF], y[:, HALF:])

    def run_tiles(g, n):
        for r in range(n):
            x_copy(g + r).wait()

            @pl.when(g + r + X_AHEAD < nv)
            def _():
                x_copy(g + r + X_AHEAD).start(priority=1)

            @pl.when(g + r >= Y_SLOTS)
            def _():
                y_copy(g + r - Y_SLOTS).wait()

        packed = [compute_tile(g + r) for r in range(n)]
        for r in range(n):
            _store_packed(ybuf, 0, tm, packed[r], lead=(g + r) % Y_SLOTS)
        for r in range(n):
            y_copy(g + r).start(priority=1)

    @pl.when(e == 0)
    def _():
        for g in range(X_AHEAD):
            @pl.when(g < nv)
            def _():
                x_copy(g).start(priority=1)

    @pl.when(g1 > g0)
    def _():
        ws = e % W_SLOTS
        wg_b[...] = wg_f[ws].astype(BF16)
        wu_b[...] = wu_f[ws].astype(BF16)
        wd_b[...] = wd_f[ws].astype(BF16)
        n_tiles = g1 - g0

        def pair(p, c):
            run_tiles(g0 + 2 * p, 2)
            return c

        lax.fori_loop(0, n_tiles // 2, pair, 0)

        @pl.when(n_tiles % 2 == 1)
        def _():
            run_tiles(g1 - 1, 1)

    @pl.when(e == pl.num_programs(0) - 1)
    def _():
        for back in range(1, Y_SLOTS + 1):
            @pl.when(nv >= back)
            def _():
                y_copy(nv - back).wait()


def _expert_call(tile_start, tile_end, tile_rows, n_valid, xs, w_gate, w_up, w_down, n_rows):
    tm = TM_EXP
    hbm = pl.BlockSpec(memory_space=pl.ANY)
    return pl.pallas_call(
        _expert_kernel,
        grid_spec=pltpu.PrefetchScalarGridSpec(
            num_scalar_prefetch=4,
            grid=(N_EXPERTS,),
            in_specs=[hbm, hbm, hbm, hbm],
            out_specs=hbm,
            scratch_shapes=[pltpu.VMEM((X_SLOTS, tm * ROW_CHUNKS, LANES), U32),
                            pltpu.VMEM((Y_SLOTS, tm * ROW_CHUNKS, LANES), U32),
                            pltpu.VMEM((W_SLOTS, D_MODEL, EXPERT_FF), F32), pltpu.VMEM((W_SLOTS, D_MODEL, EXPERT_FF), F32),
                            pltpu.VMEM((W_SLOTS, EXPERT_FF, D_MODEL), F32),
                            pltpu.VMEM((D_MODEL, EXPERT_FF), BF16), pltpu.VMEM((D_MODEL, EXPERT_FF), BF16),
                            pltpu.VMEM((EXPERT_FF, D_MODEL), BF16),
                            pltpu.SemaphoreType.DMA((X_SLOTS,)), pltpu.SemaphoreType.DMA((Y_SLOTS,)),
                            pltpu.SemaphoreType.DMA((W_SLOTS,))],
        ),
        out_shape=SDS((n_rows * ROW_CHUNKS, LANES), U32),
        compiler_params=_cparams("arbitrary"),
        name="experts",
    )(tile_start, tile_end, tile_rows, n_valid, w_gate, w_up, w_down, xs)


COMB_SUB = 32


def _combine_kernel(wts_ref, h1_ref, g_ref, wsg_ref, wsu_ref, wsd_ref, g2_ref, b2_ref, o_ref, routed_ref):
    tn = h1_ref.shape[0]
    for s0 in range(0, tn, COMB_SUB):
        acc = [jnp.zeros((COMB_SUB, LANES), F32) for _ in range(2 * ROW_CHUNKS)]
        for kk in range(TOP_K):
            wk = jnp.broadcast_to(wts_ref[s0:s0 + COMB_SUB, kk:kk + 1], (COMB_SUB, LANES))
            for cc in range(ROW_CHUNKS):
                lo, hi = _unpack_rows(g_ref[kk, pl.ds(s0 * ROW_CHUNKS + cc, COMB_SUB, stride=ROW_CHUNKS), :])
                acc[cc] = acc[cc] + wk * lo
                acc[ROW_CHUNKS + cc] = acc[ROW_CHUNKS + cc] + wk * hi
        routed_ref[s0:s0 + COMB_SUB, :] = jnp.concatenate(acc, axis=1)

    h1 = h1_ref[...]
    hb = h1.astype(BF16)
    sg = jnp.dot(hb, wsg_ref[...], preferred_element_type=F32)
    su = jnp.dot(hb, wsu_ref[...], preferred_element_type=F32)
    ff = jnp.dot((sg * _sigmoid(sg) * su).astype(BF16), wsd_ref[...], preferred_element_type=F32)
    o_ref[...] = _layer_norm(ALPHA * h1 + ff + routed_ref[...], g2_ref[...], b2_ref[...])


def _combine_call(wts_t, h1, gathered, wsg, wsu, wsd, g2, b2):
    t = h1.shape[0]
    tn = TN_COMB
    row = lambda i: (i, 0)
    fix = lambda i: (0, 0)
    return pl.pallas_call(
        _combine_kernel,
        grid=(t // tn,),
        in_specs=[pl.BlockSpec((tn, TOP_K), row),
                  pl.BlockSpec((tn, D_MODEL), row),
                  pl.BlockSpec((TOP_K, tn * ROW_CHUNKS, LANES), lambda i: (0, i, 0)),
                  pl.BlockSpec((D_MODEL, SHARED_FF), fix), pl.BlockSpec((D_MODEL, SHARED_FF), fix),
                  pl.BlockSpec((SHARED_FF, D_MODEL), fix),
                  pl.BlockSpec((1, D_MODEL), fix), pl.BlockSpec((1, D_MODEL), fix)],
        out_specs=pl.BlockSpec((tn, D_MODEL), row),
        out_shape=SDS((t, D_MODEL), F32),
        scratch_shapes=[pltpu.VMEM((tn, D_MODEL), F32)],
        compiler_params=_cparams("arbitrary"),
        name="combine_ln2",
    )(wts_t, h1, gathered, wsg, wsu, wsd, g2, b2)


def kernel(x, meta_tokens, ln_in_g, ln_in_b, rel_bias, w_in, conv_w, conv_b, conv_ln_g, conv_ln_b, sinks,
           w_out, ln1_g, ln1_b, w_router, router_bias, w_gate, w_up, w_down, ws_gate, ws_up, ws_down,
           ln2_g, ln2_b):
    nbatch, seq, d = x.shape
    t = nbatch * seq
    assert d == D_MODEL and w_in.shape[0] == DEPTH
    assert seq % (ATTN_QBLOCKS * BLOCK) == 0 and seq % T_CONV == 0
    assert all(t % tile == 0 for tile in (TQ_PROJ, TQ_MIX, TN_ROUTE, TN_COMB))
    x2d = x.reshape(t, D_MODEL)
    vec = lambda a: a.reshape(1, -1).astype(F32)
    gin, bin_ = vec(ln_in_g), vec(ln_in_b)
    w_in_b = w_in[0].astype(BF16)

    q, k, v, u = _proj_call(x2d, gin, bin_, w_in_b, TQ_PROJ)
    meta_blk = jnp.concatenate([jnp.zeros((PAD_FRONT, D_MODEL), F32), meta_tokens.astype(F32)], axis=0)
    _, k_meta, v_meta, u_meta = _proj_call(meta_blk, gin, bin_, w_in_b, BLOCK)

    attn = _attn_call(q, k, v, k_meta, v_meta, _rel_bias_table(rel_bias), sinks[0].astype(F32),
                      nbatch, seq // BLOCK)

    u_halo = jnp.concatenate([jnp.zeros((CONV_HALO - N_META, CONV_CH), F32), u_meta[PAD_FRONT:]], axis=0)
    conv = _conv_call(u, u_halo, conv_w[0].astype(F32), vec(conv_b[0]), vec(conv_ln_g[0]), vec(conv_ln_b[0]),
                      nbatch, seq)

    w_out_b = w_out[0].astype(BF16)
    wr_t = w_router[0].astype(F32).T
    wr_hi = wr_t.astype(BF16)
    wr_lo = (wr_t - wr_hi.astype(F32)).astype(BF16)
    h1, h1rows, logits = _mix_call(x2d, attn, conv, gin, bin_, w_out_b[:ATTN_W], w_out_b[ATTN_W:],
                                   vec(ln1_g[0]), vec(ln1_b[0]), wr_hi, wr_lo)

    idx, wts_t, rank, cnt = _route_call(logits, router_bias[0].astype(F32).reshape(N_EXPERTS, 1))

    tm = TM_EXP
    n_tiles = (t * TOP_K) // tm + N_EXPERTS
    counts = cnt[:, 0]
    tiles_e = (counts + tm - 1) // tm
    tile_end = jnp.cumsum(tiles_e).astype(I32)
    tile_start = (tile_end - tiles_e).astype(I32)
    offs = tile_start * tm
    tile_id = jnp.arange(n_tiles, dtype=I32)
    lo = jnp.maximum(tile_id[:, None] * tm, offs[None, :])
    hi = jnp.minimum((tile_id[:, None] + 1) * tm, (offs + counts)[None, :])
    tile_rows = jnp.sum(jnp.clip(hi - lo, 0, tm), axis=1).astype(I32)
    n_valid = tile_end[-1:]

    xs, dest = _sc_dispatch_call(idx, rank, offs, h1rows.reshape(t, ROW_CHUNKS, LANES), n_tiles * tm)
    xs = xs.reshape(n_tiles * tm * ROW_CHUNKS, LANES)
    ys = _expert_call(tile_start, tile_end, tile_rows, n_valid, xs, w_gate[0], w_up[0], w_down[0], n_tiles * tm)
    gathered = _sc_gather_call(dest, ys.reshape(n_tiles * tm, ROW_CHUNKS, LANES))
    gathered = gathered.reshape(TOP_K, t * ROW_CHUNKS, LANES)
    out = _combine_call(wts_t, h1, gathered, ws_gate[0].astype(BF16), ws_up[0].astype(BF16),
                        ws_down[0].astype(BF16), vec(ln2_g[0]), vec(ln2_b[0]))
    return out.reshape(nbatch, seq, D_MODEL)
```

```python
import functools
import math

import numpy as np
import jax
import jax.numpy as jnp
from jax import lax
from jax.experimental import pallas as pl
from jax.experimental.pallas import tpu as pltpu
from jax.experimental.pallas import tpu_sc as plsc

F32 = jnp.float32
BF16 = jnp.bfloat16
I32 = jnp.int32
U32 = jnp.uint32
SDS = jax.ShapeDtypeStruct

D_MODEL = 1024
HALF = D_MODEL // 2
LANES = 128
SUBLANES = 8
ROW_CHUNKS = HALF // LANES
N_META = 16
HEAD_DIM = 64
N_Q_HEADS = 8
N_KV_HEADS = 2
GQA_GROUP = N_Q_HEADS // N_KV_HEADS
ATTN_W = N_Q_HEADS * HEAD_DIM
KV_W = N_KV_HEADS * HEAD_DIM
WINDOW = 128
BLOCK = 128
CONV_CH = D_MODEL - ATTN_W
CONV_K = 31
IN_W = ATTN_W + 2 * KV_W + 2 * CONV_CH
NUM_BUCKETS = 32
MAX_EXACT = NUM_BUCKETS // 2
REL_MAX_DIST = 128
N_EXPERTS = 256
TOP_K = 8
N_GROUPS = 8
GROUP_SIZE = N_EXPERTS // N_GROUPS
TOPK_GROUPS = 4
EXPERT_FF = 256
SHARED_FF = 256
ROUTED_SCALE = 2.5
DEPTH = 1
ALPHA = (2.0 * DEPTH) ** 0.25
LN_EPS = 1e-5
NEG = -1e30
PAD_FRONT = (-N_META) % BLOCK

VMEM_LIMIT = 48 * 1024 * 1024

TQ_PROJ = 1024
PROJ_CHAINS = 4
ATTN_QBLOCKS = 2
T_CONV = 256
CONV_HALO = 32
R_CONV = 64
TQ_MIX = 1024
MIX_CHAINS = 4
TN_ROUTE = 256
TM_EXP = 256
X_SLOTS = 8
X_AHEAD = 4
Y_SLOTS = 4
W_SLOTS = 4
W_AHEAD = 3
TN_COMB = 512


def _cparams(*sem):
    return pltpu.CompilerParams(dimension_semantics=sem, vmem_limit_bytes=VMEM_LIMIT)


def _layer_norm(x, g, b):
    mu = jnp.mean(x, axis=-1, keepdims=True)
    xc = x - mu
    var = jnp.mean(xc * xc, axis=-1, keepdims=True)
    return xc * lax.rsqrt(var + LN_EPS) * g + b


def _sigmoid(x):
    return 1.0 / (1.0 + jnp.exp(-x))


def _pack_rows(lo_half, hi_half):
    lo = lax.bitcast_convert_type(lo_half.astype(BF16).astype(F32), U32)
    hi = lax.bitcast_convert_type(hi_half.astype(BF16).astype(F32), U32)
    return lax.shift_right_logical(lo, jnp.uint32(16)) | hi


def _unpack_rows(p):
    lo = lax.bitcast_convert_type(lax.shift_left(p, jnp.uint32(16)), F32)
    hi = lax.bitcast_convert_type(p & jnp.uint32(0xFFFF0000), F32)
    return lo, hi


def _chunk_index(start, j, n, lead):
    rows = pl.ds(start + j, n, stride=ROW_CHUNKS)
    return (rows, slice(None)) if lead is None else (lead, rows, slice(None))


def _store_packed(ref, start, n, packed, lead=None):
    for j in range(ROW_CHUNKS):
        ref[_chunk_index(start, j, n, lead)] = packed[:, j * LANES:(j + 1) * LANES]


def _load_packed_bf16(ref, start, n, lead=None):
    halves = [_unpack_rows(ref[_chunk_index(start, j, n, lead)]) for j in range(ROW_CHUNKS)]
    return jnp.concatenate([h[0] for h in halves] + [h[1] for h in halves], axis=1).astype(BF16)


def _proj_kernel(chains, x_ref, g_ref, b_ref, w_ref, q_ref, k_ref, v_ref, u_ref):
    rows = x_ref.shape[0] // chains
    for c in range(chains):
        r = slice(c * rows, (c + 1) * rows)
        h = _layer_norm(x_ref[r, :], g_ref[...], b_ref[...])
        p = jnp.dot(h.astype(BF16), w_ref[...], preferred_element_type=F32)
        q_ref[r, :] = (p[:, :ATTN_W] * (HEAD_DIM ** -0.5)).astype(BF16)
        k_ref[r, :] = p[:, ATTN_W:ATTN_W + KV_W].astype(BF16)
        v_ref[r, :] = p[:, ATTN_W + KV_W:ATTN_W + 2 * KV_W].astype(BF16)
        a = p[:, ATTN_W + 2 * KV_W:ATTN_W + 2 * KV_W + CONV_CH]
        gate = p[:, ATTN_W + 2 * KV_W + CONV_CH:]
        u_ref[r, :] = a * _sigmoid(gate)


def _proj_call(x2d, gin, bin_, w_in_b, tq):
    t = x2d.shape[0]
    row = lambda i: (i, 0)
    fix = lambda i: (0, 0)
    chains = PROJ_CHAINS if tq % (PROJ_CHAINS * BLOCK) == 0 else 1
    return pl.pallas_call(
        functools.partial(_proj_kernel, chains),
        grid=(t // tq,),
        in_specs=[pl.BlockSpec((tq, D_MODEL), row), pl.BlockSpec((1, D_MODEL), fix),
                  pl.BlockSpec((1, D_MODEL), fix), pl.BlockSpec((D_MODEL, IN_W), fix)],
        out_specs=[pl.BlockSpec((tq, ATTN_W), row), pl.BlockSpec((tq, KV_W), row),
                   pl.BlockSpec((tq, KV_W), row), pl.BlockSpec((tq, CONV_CH), row)],
        out_shape=[SDS((t, ATTN_W), BF16), SDS((t, KV_W), BF16), SDS((t, KV_W), BF16), SDS((t, CONV_CH), F32)],
        compiler_params=_cparams("arbitrary"),
        name="ln_in_proj",
    )(x2d, gin, bin_, w_in_b)


def _attn_kernel(sinks_ref, q_ref, kc_ref, kp_ref, vc_ref, vp_ref, km_ref, vm_ref, bias_ref, o_ref):
    first = pl.program_id(1) == 0
    kp = jnp.where(first, km_ref[...], kp_ref[...])
    vp = jnp.where(first, vm_ref[...], vp_ref[...])
    k = jnp.concatenate([kp, kc_ref[...]], axis=0)
    v = jnp.concatenate([vp, vc_ref[...]], axis=0)
    col = lax.broadcasted_iota(I32, (BLOCK, 2 * BLOCK), 1)
    pad_bias = jnp.where(jnp.logical_and(first, col < PAD_FRONT), NEG, 0.0).astype(F32)
    for a in range(ATTN_QBLOCKS):
        q = q_ref[a * BLOCK:(a + 1) * BLOCK, :]
        kw = k[a * BLOCK:(a + 2) * BLOCK, :]
        vw = v[a * BLOCK:(a + 2) * BLOCK, :]
        outs = []
        for h in range(N_Q_HEADS):
            g = h // GQA_GROUP
            qh = q[:, h * HEAD_DIM:(h + 1) * HEAD_DIM]
            kg = kw[:, g * HEAD_DIM:(g + 1) * HEAD_DIM]
            vg = vw[:, g * HEAD_DIM:(g + 1) * HEAD_DIM]
            s = lax.dot_general(qh, kg, (((1,), (1,)), ((), ())), preferred_element_type=F32)
            s = s + bias_ref[h]
            if a == 0:
                s = s + pad_bias
            sink = sinks_ref[h]
            m = jnp.maximum(jnp.max(s, axis=-1, keepdims=True), sink)
            p = jnp.exp(s - m)
            den = jnp.sum(p, axis=-1, keepdims=True) + jnp.exp(sink - m)
            o = jnp.dot(p.astype(BF16), vg, preferred_element_type=F32)
            outs.append(o / den)
        o_ref[a * BLOCK:(a + 1) * BLOCK, :] = jnp.concatenate(outs, axis=1).astype(BF16)


def _attn_call(q, k, v, k_meta, v_meta, bias, sinks, nbatch, nblk):
    t = q.shape[0]
    nq = ATTN_QBLOCKS
    assert nblk % nq == 0
    nstep = nblk // nq
    cur = lambda b, j: (b * nstep + j, 0)
    prev = lambda b, j: (jnp.maximum((b * nstep + j) * nq - 1, 0), 0)
    fix2 = lambda b, j: (0, 0)
    return pl.pallas_call(
        _attn_kernel,
        grid=(nbatch, nstep),
        in_specs=[pl.BlockSpec(memory_space=pltpu.SMEM),
                  pl.BlockSpec((nq * BLOCK, ATTN_W), cur),
                  pl.BlockSpec((nq * BLOCK, KV_W), cur), pl.BlockSpec((BLOCK, KV_W), prev),
                  pl.BlockSpec((nq * BLOCK, KV_W), cur), pl.BlockSpec((BLOCK, KV_W), prev),
                  pl.BlockSpec((BLOCK, KV_W), fix2), pl.BlockSpec((BLOCK, KV_W), fix2),
                  pl.BlockSpec((N_Q_HEADS, BLOCK, 2 * BLOCK), lambda b, j: (0, 0, 0))],
        out_specs=pl.BlockSpec((nq * BLOCK, ATTN_W), cur),
        out_shape=SDS((t, ATTN_W), BF16),
        compiler_params=_cparams("arbitrary", "arbitrary"),
        name="swa_attn",
    )(sinks, q, k, k, v, v, k_meta, v_meta, bias)


def _rel_bias_table(rel_bias):
    qi = np.arange(BLOCK, dtype=np.int32)[:, None]
    kj = np.arange(2 * BLOCK, dtype=np.int32)[None, :]
    dist = BLOCK + qi - kj
    dc = np.clip(dist, 0, WINDOW - 1)
    nf = np.maximum(dc, 1).astype(np.float32)
    large = MAX_EXACT + (np.log(nf / np.float32(MAX_EXACT)) / np.float32(math.log(REL_MAX_DIST / MAX_EXACT))
                         * np.float32(NUM_BUCKETS - MAX_EXACT)).astype(np.int32)
    large = np.minimum(large, NUM_BUCKETS - 1)
    bucket = np.where(dc < MAX_EXACT, dc, large)
    in_window = (dist >= 0) & (dist < WINDOW)
    onehot = (bucket.reshape(-1, 1) == np.arange(NUM_BUCKETS)[None, :]).astype(np.float32)
    bias = jnp.dot(jnp.asarray(onehot), rel_bias.astype(F32), precision=lax.Precision.HIGHEST)
    bias = jnp.transpose(bias.reshape(BLOCK, 2 * BLOCK, N_Q_HEADS), (2, 0, 1))
    return jnp.where(in_window[None], bias, NEG)


def _conv_kernel(uc_ref, up_ref, um_ref, w_ref, cb_ref, g_ref, b_ref, o_ref, s_ref, sh_ref):
    first = pl.program_id(1) == 0
    s_ref[0:CONV_HALO, :] = jnp.where(first, um_ref[...], up_ref[...])
    s_ref[CONV_HALO:CONV_HALO + T_CONV, :] = uc_ref[...]
    off = CONV_HALO - (CONV_K - 1)
    span = sh_ref.shape[1]
    for p in range(1, SUBLANES):
        sh_ref[p] = s_ref[p:p + span, :]
    for c in range(0, T_CONV, R_CONV):
        acc = jnp.zeros((R_CONV, CONV_CH), F32) + cb_ref[...]
        for kk in range(CONV_K):
            p, a = (off + kk) % SUBLANES, (off + kk) // SUBLANES * SUBLANES
            if p == 0:
                win = s_ref[c + a:c + a + R_CONV, :]
            else:
                win = sh_ref[p, c + a:c + a + R_CONV, :]
            acc = acc + win * w_ref[kk:kk + 1, :]
        y = _layer_norm(acc, g_ref[...], b_ref[...])
        o_ref[c:c + R_CONV, :] = (y * _sigmoid(y)).astype(BF16)


def _conv_call(u, u_meta_halo, conv_w, conv_b, g, b, nbatch, seq):
    t = u.shape[0]
    nj = seq // T_CONV
    per = T_CONV // CONV_HALO
    cur = lambda bb, j: (bb * nj + j, 0)
    prev = lambda bb, j: (jnp.maximum((bb * nj + j) * per - 1, 0), 0)
    fix = lambda bb, j: (0, 0)
    return pl.pallas_call(
        _conv_kernel,
        grid=(nbatch, nj),
        in_specs=[pl.BlockSpec((T_CONV, CONV_CH), cur), pl.BlockSpec((CONV_HALO, CONV_CH), prev),
                  pl.BlockSpec((CONV_HALO, CONV_CH), fix), pl.BlockSpec((CONV_K, CONV_CH), fix),
                  pl.BlockSpec((1, CONV_CH), fix), pl.BlockSpec((1, CONV_CH), fix), pl.BlockSpec((1, CONV_CH), fix)],
        out_specs=pl.BlockSpec((T_CONV, CONV_CH), cur),
        out_shape=SDS((t, CONV_CH), BF16),
        scratch_shapes=[pltpu.VMEM((CONV_HALO + T_CONV, CONV_CH), F32),
                        pltpu.VMEM((SUBLANES, T_CONV + CONV_HALO - SUBLANES, CONV_CH), F32)],
        compiler_params=_cparams("arbitrary", "arbitrary"),
        name="conv_ln",
    )(u, u, u_meta_halo, conv_w, conv_b, g, b)


def _mix_kernel(x_ref, at_ref, cv_ref, gin_ref, bin_ref, woa_ref, woc_ref, g1_ref, b1_ref,
                wrh_ref, wrl_ref, h1_ref, h1r_ref, lg_ref):
    rows = x_ref.shape[0] // MIX_CHAINS
    nt = (((1,), (1,)), ((), ()))
    for c in range(MIX_CHAINS):
        r = slice(c * rows, (c + 1) * rows)
        h = _layer_norm(x_ref[r, :], gin_ref[...], bin_ref[...])
        mix = (jnp.dot(at_ref[r, :], woa_ref[...], preferred_element_type=F32)
               + jnp.dot(cv_ref[r, :], woc_ref[...], preferred_element_type=F32))
        h1 = _layer_norm(ALPHA * h + mix, g1_ref[...], b1_ref[...])
        h1_ref[r, :] = h1
        _store_packed(h1r_ref, c * rows * ROW_CHUNKS, rows, _pack_rows(h1[:, :HALF], h1[:, HALF:]))
        hh = h1.astype(BF16)
        hl = (h1 - hh.astype(F32)).astype(BF16)
        lg = lax.dot_general(wrh_ref[...], hh, nt, preferred_element_type=F32)
        lg = lg + lax.dot_general(wrh_ref[...], hl, nt, preferred_element_type=F32)
        lg = lg + lax.dot_general(wrl_ref[...], hh, nt, preferred_element_type=F32)
        lg_ref[:, r] = lg


def _mix_call(x2d, attn, conv, gin, bin_, woa, woc, g1, b1, wrh, wrl):
    t = x2d.shape[0]
    tq = TQ_MIX
    row = lambda i: (i, 0)
    fix = lambda i: (0, 0)
    return pl.pallas_call(
        _mix_kernel,
        grid=(t // tq,),
        in_specs=[pl.BlockSpec((tq, D_MODEL), row), pl.BlockSpec((tq, ATTN_W), row), pl.BlockSpec((tq, CONV_CH), row),
                  pl.BlockSpec((1, D_MODEL), fix), pl.BlockSpec((1, D_MODEL), fix),
                  pl.BlockSpec((ATTN_W, D_MODEL), fix), pl.BlockSpec((CONV_CH, D_MODEL), fix),
                  pl.BlockSpec((1, D_MODEL), fix), pl.BlockSpec((1, D_MODEL), fix),
                  pl.BlockSpec((N_EXPERTS, D_MODEL), fix), pl.BlockSpec((N_EXPERTS, D_MODEL), fix)],
        out_specs=[pl.BlockSpec((tq, D_MODEL), row), pl.BlockSpec((tq * ROW_CHUNKS, LANES), row),
                   pl.BlockSpec((N_EXPERTS, tq), lambda i: (0, i))],
        out_shape=[SDS((t, D_MODEL), F32), SDS((t * ROW_CHUNKS, LANES), U32), SDS((N_EXPERTS, t), F32)],
        compiler_params=_cparams("arbitrary"),
        name="mix_ln1",
    )(x2d, attn, conv, gin, bin_, woa, woc, g1, b1, wrh, wrl)


def _first_argmax(x, rows, nrows):
    m = jnp.max(x, axis=0, keepdims=True)
    idx = jnp.min(jnp.where(x == m, rows, nrows), axis=0, keepdims=True)
    return m, idx


def _route_tile(logits, rbias, carry):
    tn = logits.shape[1]
    scores = _sigmoid(logits)
    choice = scores + rbias
    rows = lax.broadcasted_iota(I32, (N_EXPERTS, tn), 0)
    rows_g = lax.broadcasted_iota(I32, (GROUP_SIZE, tn), 0)
    rows_8 = lax.broadcasted_iota(I32, (N_GROUPS, tn), 0)

    gs = []
    for g in range(N_GROUPS):
        xg = choice[g * GROUP_SIZE:(g + 1) * GROUP_SIZE, :]
        m1, i1 = _first_argmax(xg, rows_g, GROUP_SIZE)
        m2 = jnp.max(jnp.where(rows_g == i1, -jnp.inf, xg), axis=0, keepdims=True)
        gs.append(m1 + m2)
    gsc = jnp.concatenate(gs, axis=0)
    gsel = jnp.zeros((N_GROUPS, tn), F32)
    for _ in range(TOPK_GROUPS):
        _, gi = _first_argmax(gsc, rows_8, N_GROUPS)
        hit = rows_8 == gi
        gsel = jnp.where(hit, 1.0, gsel)
        gsc = jnp.where(hit, -jnp.inf, gsc)
    emask = jnp.concatenate(
        [jnp.broadcast_to(gsel[g:g + 1, :], (GROUP_SIZE, tn)) for g in range(N_GROUPS)], axis=0)
    masked = jnp.where(emask > 0.5, choice, NEG)

    sel_all = jnp.zeros((N_EXPERTS, tn), F32)
    hits, idxs, ws = [], [], []
    for _ in range(TOP_K):
        _, ii = _first_argmax(masked, rows, N_EXPERTS)
        hit = rows == ii
        hits.append(hit)
        idxs.append(ii)
        ws.append(jnp.sum(jnp.where(hit, scores, 0.0), axis=0, keepdims=True))
        sel_all = jnp.where(hit, 1.0, sel_all)
        masked = jnp.where(hit, -jnp.inf, masked)
    wsum = ws[0]
    for w in ws[1:]:
        wsum = wsum + w
    idx = jnp.concatenate(idxs, axis=0)
    wts = jnp.concatenate([w / wsum * ROUTED_SCALE for w in ws], axis=0)

    r_i = lax.broadcasted_iota(I32, (tn, tn), 0)
    c_i = lax.broadcasted_iota(I32, (tn, tn), 1)
    upper = jnp.where(r_i < c_i, 1.0, 0.0).astype(BF16)
    sel_b = sel_all.astype(BF16)
    before = jnp.dot(sel_b, upper, preferred_element_type=F32)
    before = before + jnp.concatenate([carry] * (tn // LANES), axis=1)
    rank = jnp.concatenate(
        [jnp.sum(jnp.where(h, before, 0.0), axis=0, keepdims=True) for h in hits], axis=0).astype(I32)
    carry = carry + jnp.dot(sel_b, jnp.ones((tn, LANES), BF16), preferred_element_type=F32)
    return idx, wts, rank, carry


def _route_kernel(lg_ref, rb_ref, idx_ref, wts_ref, rank_ref, cnt_ref, carry_ref):
    @pl.when(pl.program_id(0) == 0)
    def _():
        carry_ref[...] = jnp.zeros_like(carry_ref)

    idx, wts, rank, carry = _route_tile(lg_ref[...], rb_ref[...], carry_ref[...])
    idx_ref[...] = idx
    wts_ref[...] = wts.T
    rank_ref[...] = rank
    carry_ref[...] = carry
    cnt_ref[...] = carry.astype(I32)


def _route_call(lg, rbias):
    t = lg.shape[1]
    tn = TN_ROUTE
    col = lambda i: (0, i)
    return pl.pallas_call(
        _route_kernel,
        grid=(t // tn,),
        in_specs=[pl.BlockSpec((N_EXPERTS, tn), col), pl.BlockSpec((N_EXPERTS, 1), lambda i: (0, 0))],
        out_specs=[pl.BlockSpec((TOP_K, tn), col), pl.BlockSpec((tn, TOP_K), lambda i: (i, 0)),
                   pl.BlockSpec((TOP_K, tn), col), pl.BlockSpec((N_EXPERTS, LANES), lambda i: (0, 0))],
        out_shape=[SDS((TOP_K, t), I32), SDS((t, TOP_K), F32), SDS((TOP_K, t), I32), SDS((N_EXPERTS, LANES), I32)],
        scratch_shapes=[pltpu.VMEM((N_EXPERTS, LANES), F32)],
        compiler_params=_cparams("arbitrary"),
        name="route",
    )(lg, rbias)


SC_CORES = 2
SC_SUBCORES = 16
SC_CHUNK = 128
SC_LANES = 16
SC_BUFS = 2


def _sc_worker_chunks(t):
    per_worker = t // (SC_CORES * SC_SUBCORES)
    assert per_worker % SC_CHUNK == 0
    return per_worker


def _sc_dispatch_call(idx, rank, offs, h1rows3, n_rows):
    t = idx.shape[1]
    per_worker = _sc_worker_chunks(t)
    mesh = plsc.VectorSubcoreMesh(core_axis_name="c", subcore_axis_name="s")

    @functools.partial(
        pl.kernel, mesh=mesh, out_type=[SDS((n_rows, ROW_CHUNKS, LANES), U32), SDS((TOP_K, t), I32)],
        scratch_types=[pltpu.VMEM((TOP_K, SC_CHUNK), I32), pltpu.VMEM((TOP_K, SC_CHUNK), I32),
                       pltpu.VMEM((N_EXPERTS,), I32), pltpu.VMEM((SC_CHUNK, ROW_CHUNKS, LANES), U32),
                       pltpu.SemaphoreType.DMA],
        compiler_params=pltpu.CompilerParams(needs_layout_passes=False),
        name="sc_dispatch")
    def body(h_hbm, idx_hbm, rank_hbm, offs_hbm, xs_hbm, dest_hbm, idx_v, rank_v, offs_v, rows_v, sem):
        wid = lax.axis_index("s") * SC_CORES + lax.axis_index("c")
        pltpu.sync_copy(offs_hbm, offs_v)

        @pl.loop(0, per_worker // SC_CHUNK)
        def _(i):
            t0 = wid * per_worker + i * SC_CHUNK
            pltpu.sync_copy(idx_hbm.at[:, pl.ds(t0, SC_CHUNK)], idx_v)
            pltpu.sync_copy(rank_hbm.at[:, pl.ds(t0, SC_CHUNK)], rank_v)
            pltpu.sync_copy(h_hbm.at[pl.ds(t0, SC_CHUNK)], rows_v)
            for kk in range(TOP_K):
                @pl.loop(0, SC_CHUNK // SC_LANES)
                def _(c):
                    lanes = pl.ds(c * SC_LANES, SC_LANES)
                    idx_v[kk, lanes] = plsc.load_gather(offs_v, [idx_v[kk, lanes]]) + rank_v[kk, lanes]
            pltpu.sync_copy(idx_v, dest_hbm.at[:, pl.ds(t0, SC_CHUNK)])
            copies = [pltpu.async_copy(rows_v, xs_hbm.at[idx_v.at[kk]], sem) for kk in range(TOP_K)]
            for c in copies:
                c.wait()

    return body(h1rows3, idx, rank, offs)


def _sc_gather_call(dest, ys3):
    t = dest.shape[1]
    per_worker = _sc_worker_chunks(t)
    mesh = plsc.VectorSubcoreMesh(core_axis_name="c", subcore_axis_name="s")

    half = SC_CHUNK // 2
    items = [(kk, h) for kk in range(TOP_K) for h in range(2)]

    @functools.partial(
        pl.kernel, mesh=mesh, out_type=SDS((TOP_K * t, ROW_CHUNKS, LANES), U32),
        scratch_types=[pltpu.VMEM((TOP_K, SC_CHUNK), I32), pltpu.VMEM((SC_BUFS, half, ROW_CHUNKS, LANES), U32),
                       pltpu.SemaphoreType.DMA((SC_BUFS,)), pltpu.SemaphoreType.DMA((SC_BUFS,))],
        name="sc_gather")
    def body(ys_hbm, dest_hbm, out_hbm, idx_v, rows_v, gsem, wsem):
        wid = lax.axis_index("s") * SC_CORES + lax.axis_index("c")

        @pl.loop(0, per_worker // SC_CHUNK)
        def _(i):
            t0 = wid * per_worker + i * SC_CHUNK
            pltpu.sync_copy(dest_hbm.at[:, pl.ds(t0, SC_CHUNK)], idx_v)

            def gather(j):
                kk, h = items[j]
                b = j % SC_BUFS
                return pltpu.make_async_copy(ys_hbm.at[idx_v.at[kk, pl.ds(h * half, half)]], rows_v.at[b], gsem.at[b])

            def put(j):
                kk, h = items[j]
                b = j % SC_BUFS
                return pltpu.make_async_copy(rows_v.at[b], out_hbm.at[pl.ds(kk * t + t0 + h * half, half)], wsem.at[b])

            ahead = SC_BUFS - 1
            n = len(items)
            for j in range(ahead):
                gather(j).start()
            for j in range(n):
                if j + ahead < n:
                    if j >= 1:
                        put(j - 1).wait()
                    gather(j + ahead).start()
                gather(j).wait()
                put(j).start()
            for j in range(max(n - ahead - 1, 0), n):
                put(j).wait()

    return body(ys3, dest)


def _expert_kernel(ts_ref, te_ref, tr_ref, nv_ref, wg_hbm, wu_hbm, wd_hbm, xs_hbm, ys_hbm,
                   xbuf, ybuf, wg_f, wu_f, wd_f, wg_b0, wu_b0, wd_b0, wg_b1, wu_b1, wd_b1, xsem, ysem, wsem):
    e = pl.program_id(0)
    rows = xbuf.shape[1]
    tm = rows // ROW_CHUNKS
    g0, g1, nv = ts_ref[e], te_ref[e], nv_ref[0]

    def x_copy(g):
        s = g % X_SLOTS
        return pltpu.make_async_copy(xs_hbm.at[pl.ds(pl.multiple_of(g * rows, rows), rows), :], xbuf.at[s], xsem.at[s])

    def y_copy(g):
        s = g % Y_SLOTS
        return pltpu.make_async_copy(ybuf.at[s], ys_hbm.at[pl.ds(pl.multiple_of(g * rows, rows), rows), :], ysem.at[s])

    def w_copies(ex):
        s = ex % W_SLOTS
        return (pltpu.make_async_copy(wg_hbm.at[ex], wg_f.at[s], wsem.at[s]),
                pltpu.make_async_copy(wu_hbm.at[ex], wu_f.at[s], wsem.at[s]),
                pltpu.make_async_copy(wd_hbm.at[ex], wd_f.at[s], wsem.at[s]))

    n_exp = pl.num_programs(0)
    nxt = jnp.minimum(e + 1, n_exp - 1)

    def round_weights(ex, dst):
        s = ex % W_SLOTS
        dst[0][...] = wg_f[s].astype(BF16)
        dst[1][...] = wu_f[s].astype(BF16)
        dst[2][...] = wd_f[s].astype(BF16)

    @pl.when(e == 0)
    def _():
        for ex in range(W_AHEAD):
            for c in w_copies(ex):
                c.start()
        for c in w_copies(0):
            c.wait()
        round_weights(0, (wg_b0, wu_b0, wd_b0))

    @pl.when(e + W_AHEAD < n_exp)
    def _():
        for c in w_copies(e + W_AHEAD):
            c.start()

    @pl.when(e + 1 < n_exp)
    def _():
        for c in w_copies(e + 1):
            c.wait()

    @pl.when(e == 0)
    def _():
        for g in range(X_AHEAD):
            @pl.when(g < nv)
            def _():
                x_copy(g).start(priority=1)

    def expert_step(w_cur, w_nxt):
        def compute_tile(g):
            x = _load_packed_bf16(xbuf, 0, tm, lead=g % X_SLOTS)
            gate = jnp.dot(x, w_cur[0][...], preferred_element_type=F32)
            up = jnp.dot(x, w_cur[1][...], preferred_element_type=F32)
            live = lax.broadcasted_iota(I32, (tm, EXPERT_FF), 0) < tr_ref[g]
            hid = jnp.where(live, gate * _sigmoid(gate) * up, 0.0).astype(BF16)
            y = jnp.dot(hid, w_cur[2][...], preferred_element_type=F32)
            return _pack_rows(y[:, :HALF], y[:, HALF:])

        def run_tiles(g, n, round_next):
            for r in range(n):
                x_copy(g + r).wait()

                @pl.when(g + r + X_AHEAD < nv)
                def _():
                    x_copy(g + r + X_AHEAD).start(priority=1)

                @pl.when(g + r >= Y_SLOTS)
                def _():
                    y_copy(g + r - Y_SLOTS).wait()

            packed = [compute_tile(g + r) for r in range(n)]
            if round_next:
                round_weights(nxt, w_nxt)
            for r in range(n):
                _store_packed(ybuf, 0, tm, packed[r], lead=(g + r) % Y_SLOTS)
            for r in range(n):
                y_copy(g + r).start(priority=1)

        n_tiles = g1 - g0

        @pl.when(n_tiles == 0)
        def _():
            round_weights(nxt, w_nxt)

        @pl.when(n_tiles == 1)
        def _():
            run_tiles(g0, 1, True)

        @pl.when(n_tiles >= 2)
        def _():
            run_tiles(g0, 2, True)

            def pair(p, c):
                run_tiles(g0 + 2 * p, 2, False)
                return c

            lax.fori_loop(1, n_tiles // 2, pair, 0)

            @pl.when(n_tiles % 2 == 1)
            def _():
                run_tiles(g1 - 1, 1, False)

    even, odd = (wg_b0, wu_b0, wd_b0), (wg_b1, wu_b1, wd_b1)

    @pl.when(e % 2 == 0)
    def _():
        expert_step(even, odd)

    @pl.when(e % 2 == 1)
    def _():
        expert_step(odd, even)

    @pl.when(e == pl.num_programs(0) - 1)
    def _():
        for back in range(1, Y_SLOTS + 1):
            @pl.when(nv >= back)
            def _():
                y_copy(nv - back).wait()


def _expert_call(tile_start, tile_end, tile_rows, n_valid, xs, w_gate, w_up, w_down, n_rows):
    tm = TM_EXP
    hbm = pl.BlockSpec(memory_space=pl.ANY)
    return pl.pallas_call(
        _expert_kernel,
        grid_spec=pltpu.PrefetchScalarGridSpec(
            num_scalar_prefetch=4,
            grid=(N_EXPERTS,),
            in_specs=[hbm, hbm, hbm, hbm],
            out_specs=hbm,
            scratch_shapes=[pltpu.VMEM((X_SLOTS, tm * ROW_CHUNKS, LANES), U32),
                            pltpu.VMEM((Y_SLOTS, tm * ROW_CHUNKS, LANES), U32),
                            pltpu.VMEM((W_SLOTS, D_MODEL, EXPERT_FF), F32), pltpu.VMEM((W_SLOTS, D_MODEL, EXPERT_FF), F32),
                            pltpu.VMEM((W_SLOTS, EXPERT_FF, D_MODEL), F32),
                            pltpu.VMEM((D_MODEL, EXPERT_FF), BF16), pltpu.VMEM((D_MODEL, EXPERT_FF), BF16),
                            pltpu.VMEM((EXPERT_FF, D_MODEL), BF16),
                            pltpu.VMEM((D_MODEL, EXPERT_FF), BF16), pltpu.VMEM((D_MODEL, EXPERT_FF), BF16),
                            pltpu.VMEM((EXPERT_FF, D_MODEL), BF16),
                            pltpu.SemaphoreType.DMA((X_SLOTS,)), pltpu.SemaphoreType.DMA((Y_SLOTS,)),
                            pltpu.SemaphoreType.DMA((W_SLOTS,))],
        ),
        out_shape=SDS((n_rows * ROW_CHUNKS, LANES), U32),
        compiler_params=_cparams("arbitrary"),
        name="experts",
    )(tile_start, tile_end, tile_rows, n_valid, w_gate, w_up, w_down, xs)


COMB_SUB = 32


def _combine_kernel(wts_ref, h1_ref, g_ref, wsg_ref, wsu_ref, wsd_ref, g2_ref, b2_ref, o_ref, routed_ref):
    tn = h1_ref.shape[0]
    for s0 in range(0, tn, COMB_SUB):
        acc = [jnp.zeros((COMB_SUB, LANES), F32) for _ in range(2 * ROW_CHUNKS)]
        for kk in range(TOP_K):
            wk = jnp.broadcast_to(wts_ref[s0:s0 + COMB_SUB, kk:kk + 1], (COMB_SUB, LANES))
            for cc in range(ROW_CHUNKS):
                lo, hi = _unpack_rows(g_ref[kk, pl.ds(s0 * ROW_CHUNKS + cc, COMB_SUB, stride=ROW_CHUNKS), :])
                acc[cc] = acc[cc] + wk * lo
                acc[ROW_CHUNKS + cc] = acc[ROW_CHUNKS + cc] + wk * hi
        routed_ref[s0:s0 + COMB_SUB, :] = jnp.concatenate(acc, axis=1)

    h1 = h1_ref[...]
    hb = h1.astype(BF16)
    sg = jnp.dot(hb, wsg_ref[...], preferred_element_type=F32)
    su = jnp.dot(hb, wsu_ref[...], preferred_element_type=F32)
    ff = jnp.dot((sg * _sigmoid(sg) * su).astype(BF16), wsd_ref[...], preferred_element_type=F32)
    o_ref[...] = _layer_norm(ALPHA * h1 + ff + routed_ref[...], g2_ref[...], b2_ref[...])


def _combine_call(wts_t, h1, gathered, wsg, wsu, wsd, g2, b2):
    t = h1.shape[0]
    tn = TN_COMB
    row = lambda i: (i, 0)
    fix = lambda i: (0, 0)
    return pl.pallas_call(
        _combine_kernel,
        grid=(t // tn,),
        in_specs=[pl.BlockSpec((tn, TOP_K), row),
                  pl.BlockSpec((tn, D_MODEL), row),
                  pl.BlockSpec((TOP_K, tn * ROW_CHUNKS, LANES), lambda i: (0, i, 0)),
                  pl.BlockSpec((D_MODEL, SHARED_FF), fix), pl.BlockSpec((D_MODEL, SHARED_FF), fix),
                  pl.BlockSpec((SHARED_FF, D_MODEL), fix),
                  pl.BlockSpec((1, D_MODEL), fix), pl.BlockSpec((1, D_MODEL), fix)],
        out_specs=pl.BlockSpec((tn, D_MODEL), row),
        out_shape=SDS((t, D_MODEL), F32),
        scratch_shapes=[pltpu.VMEM((tn, D_MODEL), F32)],
        compiler_params=_cparams("arbitrary"),
        name="combine_ln2",
    )(wts_t, h1, gathered, wsg, wsu, wsd, g2, b2)


def kernel(x, meta_tokens, ln_in_g, ln_in_b, rel_bias, w_in, conv_w, conv_b, conv_ln_g, conv_ln_b, sinks,
           w_out, ln1_g, ln1_b, w_router, router_bias, w_gate, w_up, w_down, ws_gate, ws_up, ws_down,
           ln2_g, ln2_b):
    nbatch, seq, d = x.shape
    t = nbatch * seq
    assert d == D_MODEL and w_in.shape[0] == DEPTH
    assert seq % (ATTN_QBLOCKS * BLOCK) == 0 and seq % T_CONV == 0
    assert all(t % tile == 0 for tile in (TQ_PROJ, TQ_MIX, TN_ROUTE, TN_COMB))
    x2d = x.reshape(t, D_MODEL)
    vec = lambda a: a.reshape(1, -1).astype(F32)
    gin, bin_ = vec(ln_in_g), vec(ln_in_b)
    w_in_b = w_in[0].astype(BF16)

    q, k, v, u = _proj_call(x2d, gin, bin_, w_in_b, TQ_PROJ)
    meta_blk = jnp.concatenate([jnp.zeros((PAD_FRONT, D_MODEL), F32), meta_tokens.astype(F32)], axis=0)
    _, k_meta, v_meta, u_meta = _proj_call(meta_blk, gin, bin_, w_in_b, BLOCK)

    attn = _attn_call(q, k, v, k_meta, v_meta, _rel_bias_table(rel_bias), sinks[0].astype(F32),
                      nbatch, seq // BLOCK)

    u_halo = jnp.concatenate([jnp.zeros((CONV_HALO - N_META, CONV_CH), F32), u_meta[PAD_FRONT:]], axis=0)
    conv = _conv_call(u, u_halo, conv_w[0].astype(F32), vec(conv_b[0]), vec(conv_ln_g[0]), vec(conv_ln_b[0]),
                      nbatch, seq)

    w_out_b = w_out[0].astype(BF16)
    wr_t = w_router[0].astype(F32).T
    wr_hi = wr_t.astype(BF16)
    wr_lo = (wr_t - wr_hi.astype(F32)).astype(BF16)
    h1, h1rows, logits = _mix_call(x2d, attn, conv, gin, bin_, w_out_b[:ATTN_W], w_out_b[ATTN_W:],
                                   vec(ln1_g[0]), vec(ln1_b[0]), wr_hi, wr_lo)

    idx, wts_t, rank, cnt = _route_call(logits, router_bias[0].astype(F32).reshape(N_EXPERTS, 1))

    tm = TM_EXP
    n_tiles = (t * TOP_K) // tm + N_EXPERTS
    counts = cnt[:, 0]
    tiles_e = (counts + tm - 1) // tm
    tile_end = jnp.cumsum(tiles_e).astype(I32)
    tile_start = (tile_end - tiles_e).astype(I32)
    offs = tile_start * tm
    tile_id = jnp.arange(n_tiles, dtype=I32)
    lo = jnp.maximum(tile_id[:, None] * tm, offs[None, :])
    hi = jnp.minimum((tile_id[:, None] + 1) * tm, (offs + counts)[None, :])
    tile_rows = jnp.sum(jnp.clip(hi - lo, 0, tm), axis=1).astype(I32)
    n_valid = tile_end[-1:]

    xs, dest = _sc_dispatch_call(idx, rank, offs, h1rows.reshape(t, ROW_CHUNKS, LANES), n_tiles * tm)
    xs = xs.reshape(n_tiles * tm * ROW_CHUNKS, LANES)
    ys = _expert_call(tile_start, tile_end, tile_rows, n_valid, xs, w_gate[0], w_up[0], w_down[0], n_tiles * tm)
    gathered = _sc_gather_call(dest, ys.reshape(n_tiles * tm, ROW_CHUNKS, LANES))
    gathered = gathered.reshape(TOP_K, t * ROW_CHUNKS, LANES)
    out = _combine_call(wts_t, h1, gathered, ws_gate[0].astype(BF16), ws_up[0].astype(BF16),
                        ws_down[0].astype(BF16), vec(ln2_g[0]), vec(ln2_b[0]))
    return out.reshape(nbatch, seq, D_MODEL)
```

```python
import functools
import math

import numpy as np
import jax
import jax.numpy as jnp
from jax import lax
from jax.experimental import pallas as pl
from jax.experimental.pallas import tpu as pltpu
from jax.experimental.pallas import tpu_sc as plsc

F32 = jnp.float32
BF16 = jnp.bfloat16
I32 = jnp.int32
U32 = jnp.uint32
SDS = jax.ShapeDtypeStruct

D_MODEL = 1024
HALF = D_MODEL // 2
LANES = 128
SUBLANES = 8
ROW_CHUNKS = HALF // LANES
N_META = 16
HEAD_DIM = 64
N_Q_HEADS = 8
N_KV_HEADS = 2
GQA_GROUP = N_Q_HEADS // N_KV_HEADS
ATTN_W = N_Q_HEADS * HEAD_DIM
KV_W = N_KV_HEADS * HEAD_DIM
WINDOW = 128
BLOCK = 128
CONV_CH = D_MODEL - ATTN_W
CONV_K = 31
IN_W = ATTN_W + 2 * KV_W + 2 * CONV_CH
NUM_BUCKETS = 32
MAX_EXACT = NUM_BUCKETS // 2
REL_MAX_DIST = 128
N_EXPERTS = 256
TOP_K = 8
N_GROUPS = 8
GROUP_SIZE = N_EXPERTS // N_GROUPS
TOPK_GROUPS = 4
EXPERT_FF = 256
SHARED_FF = 256
ROUTED_SCALE = 2.5
DEPTH = 1
ALPHA = (2.0 * DEPTH) ** 0.25
LN_EPS = 1e-5
NEG = -1e30
PAD_FRONT = (-N_META) % BLOCK

VMEM_LIMIT = 48 * 1024 * 1024

TQ_PROJ = 1024
PROJ_CHAINS = 4
ATTN_QBLOCKS = 2
T_CONV = 256
CONV_HALO = 32
R_CONV = 64
TQ_MIX = 1024
MIX_CHAINS = 4
TN_ROUTE = 256
TM_EXP = 256
X_SLOTS = 8
X_AHEAD = 4
Y_SLOTS = 4
W_SLOTS = 3
W_AHEAD = 2
TN_COMB = 512


def _cparams(*sem):
    return pltpu.CompilerParams(dimension_semantics=sem, vmem_limit_bytes=VMEM_LIMIT)


def _layer_norm(x, g, b):
    mu = jnp.mean(x, axis=-1, keepdims=True)
    xc = x - mu
    var = jnp.mean(xc * xc, axis=-1, keepdims=True)
    return xc * lax.rsqrt(var + LN_EPS) * g + b


def _sigmoid(x):
    return 1.0 / (1.0 + jnp.exp(-x))


def _pack_rows(lo_half, hi_half):
    lo = lax.bitcast_convert_type(lo_half.astype(BF16).astype(F32), U32)
    hi = lax.bitcast_convert_type(hi_half.astype(BF16).astype(F32), U32)
    return lax.shift_right_logical(lo, jnp.uint32(16)) | hi


def _unpack_rows(p):
    lo = lax.bitcast_convert_type(lax.shift_left(p, jnp.uint32(16)), F32)
    hi = lax.bitcast_convert_type(p & jnp.uint32(0xFFFF0000), F32)
    return lo, hi


def _chunk_index(start, j, n, lead):
    rows = pl.ds(start + j, n, stride=ROW_CHUNKS)
    return (rows, slice(None)) if lead is None else (lead, rows, slice(None))


def _store_packed(ref, start, n, packed, lead=None):
    for j in range(ROW_CHUNKS):
        ref[_chunk_index(start, j, n, lead)] = packed[:, j * LANES:(j + 1) * LANES]


def _load_packed_bf16(ref, start, n, lead=None):
    halves = [_unpack_rows(ref[_chunk_index(start, j, n, lead)]) for j in range(ROW_CHUNKS)]
    return jnp.concatenate([h[0] for h in halves] + [h[1] for h in halves], axis=1).astype(BF16)


def _proj_kernel(chains, x_ref, g_ref, b_ref, w_ref, q_ref, k_ref, v_ref, u_ref):
    rows = x_ref.shape[0] // chains
    for c in range(chains):
        r = slice(c * rows, (c + 1) * rows)
        h = _layer_norm(x_ref[r, :], g_ref[...], b_ref[...])
        p = jnp.dot(h.astype(BF16), w_ref[...], preferred_element_type=F32)
        q_ref[r, :] = (p[:, :ATTN_W] * (HEAD_DIM ** -0.5)).astype(BF16)
        k_ref[r, :] = p[:, ATTN_W:ATTN_W + KV_W].astype(BF16)
        v_ref[r, :] = p[:, ATTN_W + KV_W:ATTN_W + 2 * KV_W].astype(BF16)
        a = p[:, ATTN_W + 2 * KV_W:ATTN_W + 2 * KV_W + CONV_CH]
        gate = p[:, ATTN_W + 2 * KV_W + CONV_CH:]
        u_ref[r, :] = a * _sigmoid(gate)


def _proj_call(x2d, gin, bin_, w_in_b, tq):
    t = x2d.shape[0]
    row = lambda i: (i, 0)
    fix = lambda i: (0, 0)
    chains = PROJ_CHAINS if tq % (PROJ_CHAINS * BLOCK) == 0 else 1
    return pl.pallas_call(
        functools.partial(_proj_kernel, chains),
        grid=(t // tq,),
        in_specs=[pl.BlockSpec((tq, D_MODEL), row), pl.BlockSpec((1, D_MODEL), fix),
                  pl.BlockSpec((1, D_MODEL), fix), pl.BlockSpec((D_MODEL, IN_W), fix)],
        out_specs=[pl.BlockSpec((tq, ATTN_W), row), pl.BlockSpec((tq, KV_W), row),
                   pl.BlockSpec((tq, KV_W), row), pl.BlockSpec((tq, CONV_CH), row)],
        out_shape=[SDS((t, ATTN_W), BF16), SDS((t, KV_W), BF16), SDS((t, KV_W), BF16), SDS((t, CONV_CH), F32)],
        compiler_params=_cparams("arbitrary"),
        name="ln_in_proj",
    )(x2d, gin, bin_, w_in_b)


def _attn_stages(sinks_ref, q_ref, kc_ref, kp_ref, vc_ref, vp_ref, km_ref, vm_ref, bias_ref, o_ref):
    first = pl.program_id(1) == 0
    kp = jnp.where(first, km_ref[...], kp_ref[...])
    vp = jnp.where(first, vm_ref[...], vp_ref[...])
    k = jnp.concatenate([kp, kc_ref[...]], axis=0)
    v = jnp.concatenate([vp, vc_ref[...]], axis=0)
    col = lax.broadcasted_iota(I32, (BLOCK, 2 * BLOCK), 1)
    pad_bias = jnp.where(jnp.logical_and(first, col < PAD_FRONT), NEG, 0.0).astype(F32)
    def block(a):
        q = q_ref[a * BLOCK:(a + 1) * BLOCK, :]
        kw = k[a * BLOCK:(a + 2) * BLOCK, :]
        vw = v[a * BLOCK:(a + 2) * BLOCK, :]
        outs = []
        for h in range(N_Q_HEADS):
            g = h // GQA_GROUP
            qh = q[:, h * HEAD_DIM:(h + 1) * HEAD_DIM]
            kg = kw[:, g * HEAD_DIM:(g + 1) * HEAD_DIM]
            vg = vw[:, g * HEAD_DIM:(g + 1) * HEAD_DIM]
            s = lax.dot_general(qh, kg, (((1,), (1,)), ((), ())), preferred_element_type=F32)
            s = s + bias_ref[h]
            if a == 0:
                s = s + pad_bias
            sink = sinks_ref[h]
            m = jnp.maximum(jnp.max(s, axis=-1, keepdims=True), sink)
            p = jnp.exp(s - m)
            den = jnp.sum(p, axis=-1, keepdims=True) + jnp.exp(sink - m)
            o = jnp.dot(p.astype(BF16), vg, preferred_element_type=F32)
            outs.append(o / den)
        o_ref[a * BLOCK:(a + 1) * BLOCK, :] = jnp.concatenate(outs, axis=1).astype(BF16)

    return [functools.partial(block, a) for a in range(ATTN_QBLOCKS)]


def _rel_bias_table(rel_bias):
    qi = np.arange(BLOCK, dtype=np.int32)[:, None]
    kj = np.arange(2 * BLOCK, dtype=np.int32)[None, :]
    dist = BLOCK + qi - kj
    dc = np.clip(dist, 0, WINDOW - 1)
    nf = np.maximum(dc, 1).astype(np.float32)
    large = MAX_EXACT + (np.log(nf / np.float32(MAX_EXACT)) / np.float32(math.log(REL_MAX_DIST / MAX_EXACT))
                         * np.float32(NUM_BUCKETS - MAX_EXACT)).astype(np.int32)
    large = np.minimum(large, NUM_BUCKETS - 1)
    bucket = np.where(dc < MAX_EXACT, dc, large)
    in_window = (dist >= 0) & (dist < WINDOW)
    onehot = (bucket.reshape(-1, 1) == np.arange(NUM_BUCKETS)[None, :]).astype(np.float32)
    bias = jnp.dot(jnp.asarray(onehot), rel_bias.astype(F32), precision=lax.Precision.HIGHEST)
    bias = jnp.transpose(bias.reshape(BLOCK, 2 * BLOCK, N_Q_HEADS), (2, 0, 1))
    return jnp.where(in_window[None], bias, NEG)


def _conv_stages(uc_ref, up_ref, um_ref, w_ref, cb_ref, g_ref, b_ref, o_ref, s_ref, sh_ref):
    first = pl.program_id(1) == 0
    s_ref[0:CONV_HALO, :] = jnp.where(first, um_ref[...], up_ref[...])
    s_ref[CONV_HALO:CONV_HALO + T_CONV, :] = uc_ref[...]
    off = CONV_HALO - (CONV_K - 1)
    span = sh_ref.shape[1]
    for p in range(1, SUBLANES):
        sh_ref[p] = s_ref[p:p + span, :]
    def chunk(c):
        acc = jnp.zeros((R_CONV, CONV_CH), F32) + cb_ref[...]
        for kk in range(CONV_K):
            p, a = (off + kk) % SUBLANES, (off + kk) // SUBLANES * SUBLANES
            if p == 0:
                win = s_ref[c + a:c + a + R_CONV, :]
            else:
                win = sh_ref[p, c + a:c + a + R_CONV, :]
            acc = acc + win * w_ref[kk:kk + 1, :]
        y = _layer_norm(acc, g_ref[...], b_ref[...])
        o_ref[c:c + R_CONV, :] = (y * _sigmoid(y)).astype(BF16)

    return [functools.partial(chunk, c) for c in range(0, T_CONV, R_CONV)]


def _attn_conv_kernel(sinks_ref, q_ref, kc_ref, kp_ref, vc_ref, vp_ref, km_ref, vm_ref, bias_ref,
                      uc_ref, up_ref, um_ref, w_ref, cb_ref, g_ref, b_ref, ao_ref, co_ref, s_ref, sh_ref):
    conv = _conv_stages(uc_ref, up_ref, um_ref, w_ref, cb_ref, g_ref, b_ref, co_ref, s_ref, sh_ref)
    attn = _attn_stages(sinks_ref, q_ref, kc_ref, kp_ref, vc_ref, vp_ref, km_ref, vm_ref, bias_ref, ao_ref)
    per = len(conv) // len(attn)
    for a, attn_block in enumerate(attn):
        for chunk in conv[a * per:(a + 1) * per]:
            chunk()
        attn_block()


def _attn_conv_call(q, k, v, k_meta, v_meta, bias, sinks, u, u_meta_halo, conv_w, conv_b, g, b, nbatch, seq):
    t = q.shape[0]
    rows = ATTN_QBLOCKS * BLOCK
    assert rows == T_CONV
    nstep = seq // rows
    per_blk = rows // BLOCK
    per_halo = rows // CONV_HALO
    cur = lambda bb, j: (bb * nstep + j, 0)
    prev_blk = lambda bb, j: (jnp.maximum((bb * nstep + j) * per_blk - 1, 0), 0)
    prev_halo = lambda bb, j: (jnp.maximum((bb * nstep + j) * per_halo - 1, 0), 0)
    fix = lambda bb, j: (0, 0)
    return pl.pallas_call(
        _attn_conv_kernel,
        grid=(nbatch, nstep),
        in_specs=[pl.BlockSpec(memory_space=pltpu.SMEM),
                  pl.BlockSpec((rows, ATTN_W), cur),
                  pl.BlockSpec((rows, KV_W), cur), pl.BlockSpec((BLOCK, KV_W), prev_blk),
                  pl.BlockSpec((rows, KV_W), cur), pl.BlockSpec((BLOCK, KV_W), prev_blk),
                  pl.BlockSpec((BLOCK, KV_W), fix), pl.BlockSpec((BLOCK, KV_W), fix),
                  pl.BlockSpec((N_Q_HEADS, BLOCK, 2 * BLOCK), lambda bb, j: (0, 0, 0)),
                  pl.BlockSpec((rows, CONV_CH), cur), pl.BlockSpec((CONV_HALO, CONV_CH), prev_halo),
                  pl.BlockSpec((CONV_HALO, CONV_CH), fix), pl.BlockSpec((CONV_K, CONV_CH), fix),
                  pl.BlockSpec((1, CONV_CH), fix), pl.BlockSpec((1, CONV_CH), fix), pl.BlockSpec((1, CONV_CH), fix)],
        out_specs=[pl.BlockSpec((rows, ATTN_W), cur), pl.BlockSpec((rows, CONV_CH), cur)],
        out_shape=[SDS((t, ATTN_W), BF16), SDS((t, CONV_CH), BF16)],
        scratch_shapes=[pltpu.VMEM((CONV_HALO + T_CONV, CONV_CH), F32),
                        pltpu.VMEM((SUBLANES, T_CONV + CONV_HALO - SUBLANES, CONV_CH), F32)],
        compiler_params=_cparams("arbitrary", "arbitrary"),
        name="attn_conv",
    )(sinks, q, k, k, v, v, k_meta, v_meta, bias, u, u, u_meta_halo, conv_w, conv_b, g, b)


def _mix_kernel(x_ref, at_ref, cv_ref, gin_ref, bin_ref, woa_ref, woc_ref, g1_ref, b1_ref,
                wrh_ref, wrl_ref, h1_ref, h1r_ref, lg_ref):
    rows = x_ref.shape[0] // MIX_CHAINS
    nt = (((1,), (1,)), ((), ()))
    for c in range(MIX_CHAINS):
        r = slice(c * rows, (c + 1) * rows)
        h = _layer_norm(x_ref[r, :], gin_ref[...], bin_ref[...])
        mix = (jnp.dot(at_ref[r, :], woa_ref[...], preferred_element_type=F32)
               + jnp.dot(cv_ref[r, :], woc_ref[...], preferred_element_type=F32))
        h1 = _layer_norm(ALPHA * h + mix, g1_ref[...], b1_ref[...])
        h1_ref[r, :] = h1
        _store_packed(h1r_ref, c * rows * ROW_CHUNKS, rows, _pack_rows(h1[:, :HALF], h1[:, HALF:]))
        hh = h1.astype(BF16)
        hl = (h1 - hh.astype(F32)).astype(BF16)
        lg = lax.dot_general(wrh_ref[...], hh, nt, preferred_element_type=F32)
        lg = lg + lax.dot_general(wrh_ref[...], hl, nt, preferred_element_type=F32)
        lg = lg + lax.dot_general(wrl_ref[...], hh, nt, preferred_element_type=F32)
        lg_ref[:, r] = lg


def _mix_call(x2d, attn, conv, gin, bin_, woa, woc, g1, b1, wrh, wrl):
    t = x2d.shape[0]
    tq = TQ_MIX
    row = lambda i: (i, 0)
    fix = lambda i: (0, 0)
    return pl.pallas_call(
        _mix_kernel,
        grid=(t // tq,),
        in_specs=[pl.BlockSpec((tq, D_MODEL), row), pl.BlockSpec((tq, ATTN_W), row), pl.BlockSpec((tq, CONV_CH), row),
                  pl.BlockSpec((1, D_MODEL), fix), pl.BlockSpec((1, D_MODEL), fix),
                  pl.BlockSpec((ATTN_W, D_MODEL), fix), pl.BlockSpec((CONV_CH, D_MODEL), fix),
                  pl.BlockSpec((1, D_MODEL), fix), pl.BlockSpec((1, D_MODEL), fix),
                  pl.BlockSpec((N_EXPERTS, D_MODEL), fix), pl.BlockSpec((N_EXPERTS, D_MODEL), fix)],
        out_specs=[pl.BlockSpec((tq, D_MODEL), row), pl.BlockSpec((tq * ROW_CHUNKS, LANES), row),
                   pl.BlockSpec((N_EXPERTS, tq), lambda i: (0, i))],
        out_shape=[SDS((t, D_MODEL), F32), SDS((t * ROW_CHUNKS, LANES), U32), SDS((N_EXPERTS, t), F32)],
        compiler_params=_cparams("arbitrary"),
        name="mix_ln1",
    )(x2d, attn, conv, gin, bin_, woa, woc, g1, b1, wrh, wrl)


def _first_argmax(x, rows, nrows):
    m = jnp.max(x, axis=0, keepdims=True)
    idx = jnp.min(jnp.where(x == m, rows, nrows), axis=0, keepdims=True)
    return m, idx


def _route_tile(logits, rbias, carry):
    tn = logits.shape[1]
    scores = _sigmoid(logits)
    choice = scores + rbias
    rows = lax.broadcasted_iota(I32, (N_EXPERTS, tn), 0)
    rows_g = lax.broadcasted_iota(I32, (GROUP_SIZE, tn), 0)
    rows_8 = lax.broadcasted_iota(I32, (N_GROUPS, tn), 0)

    gs = []
    for g in range(N_GROUPS):
        xg = choice[g * GROUP_SIZE:(g + 1) * GROUP_SIZE, :]
        m1, i1 = _first_argmax(xg, rows_g, GROUP_SIZE)
        m2 = jnp.max(jnp.where(rows_g == i1, -jnp.inf, xg), axis=0, keepdims=True)
        gs.append(m1 + m2)
    gsc = jnp.concatenate(gs, axis=0)
    gsel = jnp.zeros((N_GROUPS, tn), F32)
    for _ in range(TOPK_GROUPS):
        _, gi = _first_argmax(gsc, rows_8, N_GROUPS)
        hit = rows_8 == gi
        gsel = jnp.where(hit, 1.0, gsel)
        gsc = jnp.where(hit, -jnp.inf, gsc)
    emask = jnp.concatenate(
        [jnp.broadcast_to(gsel[g:g + 1, :], (GROUP_SIZE, tn)) for g in range(N_GROUPS)], axis=0)
    masked = jnp.where(emask > 0.5, choice, NEG)

    sel_all = jnp.zeros((N_EXPERTS, tn), F32)
    hits, idxs, ws = [], [], []
    for _ in range(TOP_K):
        _, ii = _first_argmax(masked, rows, N_EXPERTS)
        hit = rows == ii
        hits.append(hit)
        idxs.append(ii)
        ws.append(jnp.sum(jnp.where(hit, scores, 0.0), axis=0, keepdims=True))
        sel_all = jnp.where(hit, 1.0, sel_all)
        masked = jnp.where(hit, -jnp.inf, masked)
    wsum = ws[0]
    for w in ws[1:]:
        wsum = wsum + w
    idx = jnp.concatenate(idxs, axis=0)
    wts = jnp.concatenate([w / wsum * ROUTED_SCALE for w in ws], axis=0)

    r_i = lax.broadcasted_iota(I32, (tn, tn), 0)
    c_i = lax.broadcasted_iota(I32, (tn, tn), 1)
    upper = jnp.where(r_i < c_i, 1.0, 0.0).astype(BF16)
    sel_b = sel_all.astype(BF16)
    before = jnp.dot(sel_b, upper, preferred_element_type=F32)
    before = before + jnp.concatenate([carry] * (tn // LANES), axis=1)
    rank = jnp.concatenate(
        [jnp.sum(jnp.where(h, before, 0.0), axis=0, keepdims=True) for h in hits], axis=0).astype(I32)
    carry = carry + jnp.dot(sel_b, jnp.ones((tn, LANES), BF16), preferred_element_type=F32)
    return idx, wts, rank, carry


def _route_kernel(lg_ref, rb_ref, idx_ref, wts_ref, rank_ref, cnt_ref, carry_ref):
    @pl.when(pl.program_id(0) == 0)
    def _():
        carry_ref[...] = jnp.zeros_like(carry_ref)

    idx, wts, rank, carry = _route_tile(lg_ref[...], rb_ref[...], carry_ref[...])
    idx_ref[...] = idx
    wts_ref[...] = wts.T
    rank_ref[...] = rank
    carry_ref[...] = carry
    cnt_ref[...] = carry.astype(I32)


def _route_call(lg, rbias):
    t = lg.shape[1]
    tn = TN_ROUTE
    col = lambda i: (0, i)
    return pl.pallas_call(
        _route_kernel,
        grid=(t // tn,),
        in_specs=[pl.BlockSpec((N_EXPERTS, tn), col), pl.BlockSpec((N_EXPERTS, 1), lambda i: (0, 0))],
        out_specs=[pl.BlockSpec((TOP_K, tn), col), pl.BlockSpec((tn, TOP_K), lambda i: (i, 0)),
                   pl.BlockSpec((TOP_K, tn), col), pl.BlockSpec((N_EXPERTS, LANES), lambda i: (0, 0))],
        out_shape=[SDS((TOP_K, t), I32), SDS((t, TOP_K), F32), SDS((TOP_K, t), I32), SDS((N_EXPERTS, LANES), I32)],
        scratch_shapes=[pltpu.VMEM((N_EXPERTS, LANES), F32)],
        compiler_params=_cparams("arbitrary"),
        name="route",
    )(lg, rbias)


SC_CORES = 2
SC_SUBCORES = 16
SC_CHUNK = 128
SC_LANES = 16
SC_BUFS = 2


def _sc_worker_chunks(t):
    per_worker = t // (SC_CORES * SC_SUBCORES)
    assert per_worker % SC_CHUNK == 0
    return per_worker


def _sc_dispatch_call(idx, rank, offs, h1rows3, n_rows):
    t = idx.shape[1]
    per_worker = _sc_worker_chunks(t)
    mesh = plsc.VectorSubcoreMesh(core_axis_name="c", subcore_axis_name="s")

    @functools.partial(
        pl.kernel, mesh=mesh, out_type=[SDS((n_rows, ROW_CHUNKS, LANES), U32), SDS((TOP_K, t), I32)],
        scratch_types=[pltpu.VMEM((TOP_K, SC_CHUNK), I32), pltpu.VMEM((TOP_K, SC_CHUNK), I32),
                       pltpu.VMEM((N_EXPERTS,), I32), pltpu.VMEM((SC_CHUNK, ROW_CHUNKS, LANES), U32),
                       pltpu.SemaphoreType.DMA],
        compiler_params=pltpu.CompilerParams(needs_layout_passes=False),
        name="sc_dispatch")
    def body(h_hbm, idx_hbm, rank_hbm, offs_hbm, xs_hbm, dest_hbm, idx_v, rank_v, offs_v, rows_v, sem):
        wid = lax.axis_index("s") * SC_CORES + lax.axis_index("c")
        pltpu.sync_copy(offs_hbm, offs_v)

        @pl.loop(0, per_worker // SC_CHUNK)
        def _(i):
            t0 = wid * per_worker + i * SC_CHUNK
            pltpu.sync_copy(idx_hbm.at[:, pl.ds(t0, SC_CHUNK)], idx_v)
            pltpu.sync_copy(rank_hbm.at[:, pl.ds(t0, SC_CHUNK)], rank_v)
            pltpu.sync_copy(h_hbm.at[pl.ds(t0, SC_CHUNK)], rows_v)
            for kk in range(TOP_K):
                @pl.loop(0, SC_CHUNK // SC_LANES)
                def _(c):
                    lanes = pl.ds(c * SC_LANES, SC_LANES)
                    idx_v[kk, lanes] = plsc.load_gather(offs_v, [idx_v[kk, lanes]]) + rank_v[kk, lanes]
            pltpu.sync_copy(idx_v, dest_hbm.at[:, pl.ds(t0, SC_CHUNK)])
            copies = [pltpu.async_copy(rows_v, xs_hbm.at[idx_v.at[kk]], sem) for kk in range(TOP_K)]
            for c in copies:
                c.wait()

    return body(h1rows3, idx, rank, offs)


def _sc_gather_call(dest, ys3):
    t = dest.shape[1]
    per_worker = _sc_worker_chunks(t)
    mesh = plsc.VectorSubcoreMesh(core_axis_name="c", subcore_axis_name="s")

    half = SC_CHUNK // 2
    items = [(kk, h) for kk in range(TOP_K) for h in range(2)]

    @functools.partial(
        pl.kernel, mesh=mesh, out_type=SDS((TOP_K * t, ROW_CHUNKS, LANES), U32),
        scratch_types=[pltpu.VMEM((TOP_K, SC_CHUNK), I32), pltpu.VMEM((SC_BUFS, half, ROW_CHUNKS, LANES), U32),
                       pltpu.SemaphoreType.DMA((SC_BUFS,)), pltpu.SemaphoreType.DMA((SC_BUFS,))],
        name="sc_gather")
    def body(ys_hbm, dest_hbm, out_hbm, idx_v, rows_v, gsem, wsem):
        wid = lax.axis_index("s") * SC_CORES + lax.axis_index("c")

        @pl.loop(0, per_worker // SC_CHUNK)
        def _(i):
            t0 = wid * per_worker + i * SC_CHUNK
            pltpu.sync_copy(dest_hbm.at[:, pl.ds(t0, SC_CHUNK)], idx_v)

            def gather(j):
                kk, h = items[j]
                b = j % SC_BUFS
                return pltpu.make_async_copy(ys_hbm.at[idx_v.at[kk, pl.ds(h * half, half)]], rows_v.at[b], gsem.at[b])

            def put(j):
                kk, h = items[j]
                b = j % SC_BUFS
                return pltpu.make_async_copy(rows_v.at[b], out_hbm.at[pl.ds(kk * t + t0 + h * half, half)], wsem.at[b])

            ahead = SC_BUFS - 1
            n = len(items)
            for j in range(ahead):
                gather(j).start()
            for j in range(n):
                if j + ahead < n:
                    if j >= 1:
                        put(j - 1).wait()
                    gather(j + ahead).start()
                gather(j).wait()
                put(j).start()
            for j in range(max(n - ahead - 1, 0), n):
                put(j).wait()

    return body(ys3, dest)


def _expert_kernel(ts_ref, te_ref, tr_ref, nv_ref, wg_hbm, wu_hbm, wd_hbm, xs_hbm, ys_hbm,
                   xbuf, ybuf, wg_f, wu_f, wd_f, wg_b, wu_b, wd_b, xsem, ysem, wsem):
    e = pl.program_id(0)
    rows = xbuf.shape[1]
    tm = rows // ROW_CHUNKS
    g0, g1, nv = ts_ref[e], te_ref[e], nv_ref[0]

    def x_copy(g):
        s = g % X_SLOTS
        return pltpu.make_async_copy(xs_hbm.at[pl.ds(pl.multiple_of(g * rows, rows), rows), :], xbuf.at[s], xsem.at[s])

    def y_copy(g):
        s = g % Y_SLOTS
        return pltpu.make_async_copy(ybuf.at[s], ys_hbm.at[pl.ds(pl.multiple_of(g * rows, rows), rows), :], ysem.at[s])

    def w_copies(ex):
        s = ex % W_SLOTS
        return (pltpu.make_async_copy(wg_hbm.at[ex], wg_f.at[s], wsem.at[s]),
                pltpu.make_async_copy(wu_hbm.at[ex], wu_f.at[s], wsem.at[s]),
                pltpu.make_async_copy(wd_hbm.at[ex], wd_f.at[s], wsem.at[s]))

    n_exp = pl.num_programs(0)

    @pl.when(e == 0)
    def _():
        for ex in range(W_AHEAD):
            for c in w_copies(ex):
                c.start()

    @pl.when(e + W_AHEAD < n_exp)
    def _():
        for c in w_copies(e + W_AHEAD):
            c.start()

    for c in w_copies(e):
        c.wait()

    def compute_tile(g):
        x = _load_packed_bf16(xbuf, 0, tm, lead=g % X_SLOTS)
        gate = jnp.dot(x, wg_b[...], preferred_element_type=F32)
        up = jnp.dot(x, wu_b[...], preferred_element_type=F32)
        live = lax.broadcasted_iota(I32, (tm, EXPERT_FF), 0) < tr_ref[g]
        hid = jnp.where(live, gate * _sigmoid(gate) * up, 0.0).astype(BF16)
        y = jnp.dot(hid, wd_b[...], preferred_element_type=F32)
        return _pack_rows(y[:, :HALF], y[:, HALF:])

    def run_tiles(g, n):
        for r in range(n):
            x_copy(g + r).wait()

            @pl.when(g + r + X_AHEAD < nv)
            def _():
                x_copy(g + r + X_AHEAD).start(priority=1)

            @pl.when(g + r >= Y_SLOTS)
            def _():
                y_copy(g + r - Y_SLOTS).wait()

        packed = [compute_tile(g + r) for r in range(n)]
        for r in range(n):
            _store_packed(ybuf, 0, tm, packed[r], lead=(g + r) % Y_SLOTS)
        for r in range(n):
            y_copy(g + r).start(priority=1)

    @pl.when(e == 0)
    def _():
        for g in range(X_AHEAD):
            @pl.when(g < nv)
            def _():
                x_copy(g).start(priority=1)

    @pl.when(g1 > g0)
    def _():
        ws = e % W_SLOTS
        wg_b[...] = wg_f[ws].astype(BF16)
        wu_b[...] = wu_f[ws].astype(BF16)
        wd_b[...] = wd_f[ws].astype(BF16)
        n_tiles = g1 - g0

        def pair(p, c):
            run_tiles(g0 + 2 * p, 2)
            return c

        lax.fori_loop(0, n_tiles // 2, pair, 0)

        @pl.when(n_tiles % 2 == 1)
        def _():
            run_tiles(g1 - 1, 1)

    @pl.when(e == pl.num_programs(0) - 1)
    def _():
        for back in range(1, Y_SLOTS + 1):
            @pl.when(nv >= back)
            def _():
                y_copy(nv - back).wait()


def _expert_call(tile_start, tile_end, tile_rows, n_valid, xs, w_gate, w_up, w_down, n_rows):
    tm = TM_EXP
    hbm = pl.BlockSpec(memory_space=pl.ANY)
    return pl.pallas_call(
        _expert_kernel,
        grid_spec=pltpu.PrefetchScalarGridSpec(
            num_scalar_prefetch=4,
            grid=(N_EXPERTS,),
            in_specs=[hbm, hbm, hbm, hbm],
            out_specs=hbm,
            scratch_shapes=[pltpu.VMEM((X_SLOTS, tm * ROW_CHUNKS, LANES), U32),
                            pltpu.VMEM((Y_SLOTS, tm * ROW_CHUNKS, LANES), U32),
                            pltpu.VMEM((W_SLOTS, D_MODEL, EXPERT_FF), F32), pltpu.VMEM((W_SLOTS, D_MODEL, EXPERT_FF), F32),
                            pltpu.VMEM((W_SLOTS, EXPERT_FF, D_MODEL), F32),
                            pltpu.VMEM((D_MODEL, EXPERT_FF), BF16), pltpu.VMEM((D_MODEL, EXPERT_FF), BF16),
                            pltpu.VMEM((EXPERT_FF, D_MODEL), BF16),
                            pltpu.SemaphoreType.DMA((X_SLOTS,)), pltpu.SemaphoreType.DMA((Y_SLOTS,)),
                            pltpu.SemaphoreType.DMA((W_SLOTS,))],
        ),
        out_shape=SDS((n_rows * ROW_CHUNKS, LANES), U32),
        compiler_params=_cparams("arbitrary"),
        name="experts",
    )(tile_start, tile_end, tile_rows, n_valid, w_gate, w_up, w_down, xs)


COMB_SUB = 32


def _combine_kernel(wts_ref, h1_ref, g_ref, wsg_ref, wsu_ref, wsd_ref, g2_ref, b2_ref, o_ref, routed_ref):
    tn = h1_ref.shape[0]
    for s0 in range(0, tn, COMB_SUB):
        acc = [jnp.zeros((COMB_SUB, LANES), F32) for _ in range(2 * ROW_CHUNKS)]
        for kk in range(TOP_K):
            wk = jnp.broadcast_to(wts_ref[s0:s0 + COMB_SUB, kk:kk + 1], (COMB_SUB, LANES))
            for cc in range(ROW_CHUNKS):
                lo, hi = _unpack_rows(g_ref[kk, pl.ds(s0 * ROW_CHUNKS + cc, COMB_SUB, stride=ROW_CHUNKS), :])
                acc[cc] = acc[cc] + wk * lo
                acc[ROW_CHUNKS + cc] = acc[ROW_CHUNKS + cc] + wk * hi
        routed_ref[s0:s0 + COMB_SUB, :] = jnp.concatenate(acc, axis=1)

    h1 = h1_ref[...]
    hb = h1.astype(BF16)
    sg = jnp.dot(hb, wsg_ref[...], preferred_element_type=F32)
    su = jnp.dot(hb, wsu_ref[...], preferred_element_type=F32)
    ff = jnp.dot((sg * _sigmoid(sg) * su).astype(BF16), wsd_ref[...], preferred_element_type=F32)
    o_ref[...] = _layer_norm(ALPHA * h1 + ff + routed_ref[...], g2_ref[...], b2_ref[...])


def _combine_call(wts_t, h1, gathered, wsg, wsu, wsd, g2, b2):
    t = h1.shape[0]
    tn = TN_COMB
    row = lambda i: (i, 0)
    fix = lambda i: (0, 0)
    return pl.pallas_call(
        _combine_kernel,
        grid=(t // tn,),
        in_specs=[pl.BlockSpec((tn, TOP_K), row),
                  pl.BlockSpec((tn, D_MODEL), row),
                  pl.BlockSpec((TOP_K, tn * ROW_CHUNKS, LANES), lambda i: (0, i, 0)),
                  pl.BlockSpec((D_MODEL, SHARED_FF), fix), pl.BlockSpec((D_MODEL, SHARED_FF), fix),
                  pl.BlockSpec((SHARED_FF, D_MODEL), fix),
                  pl.BlockSpec((1, D_MODEL), fix), pl.BlockSpec((1, D_MODEL), fix)],
        out_specs=pl.BlockSpec((tn, D_MODEL), row),
        out_shape=SDS((t, D_MODEL), F32),
        scratch_shapes=[pltpu.VMEM((tn, D_MODEL), F32)],
        compiler_params=_cparams("arbitrary"),
        name="combine_ln2",
    )(wts_t, h1, gathered, wsg, wsu, wsd, g2, b2)


def kernel(x, meta_tokens, ln_in_g, ln_in_b, rel_bias, w_in, conv_w, conv_b, conv_ln_g, conv_ln_b, sinks,
           w_out, ln1_g, ln1_b, w_router, router_bias, w_gate, w_up, w_down, ws_gate, ws_up, ws_down,
           ln2_g, ln2_b):
    nbatch, seq, d = x.shape
    t = nbatch * seq
    assert d == D_MODEL and w_in.shape[0] == DEPTH
    assert seq % (ATTN_QBLOCKS * BLOCK) == 0 and seq % T_CONV == 0
    assert all(t % tile == 0 for tile in (TQ_PROJ, TQ_MIX, TN_ROUTE, TN_COMB))
    x2d = x.reshape(t, D_MODEL)
    vec = lambda a: a.reshape(1, -1).astype(F32)
    gin, bin_ = vec(ln_in_g), vec(ln_in_b)
    w_in_b = w_in[0].astype(BF16)

    q, k, v, u = _proj_call(x2d, gin, bin_, w_in_b, TQ_PROJ)
    meta_blk = jnp.concatenate([jnp.zeros((PAD_FRONT, D_MODEL), F32), meta_tokens.astype(F32)], axis=0)
    _, k_meta, v_meta, u_meta = _proj_call(meta_blk, gin, bin_, w_in_b, BLOCK)

    u_halo = jnp.concatenate([jnp.zeros((CONV_HALO - N_META, CONV_CH), F32), u_meta[PAD_FRONT:]], axis=0)
    attn, conv = _attn_conv_call(q, k, v, k_meta, v_meta, _rel_bias_table(rel_bias), sinks[0].astype(F32),
                                 u, u_halo, conv_w[0].astype(F32), vec(conv_b[0]), vec(conv_ln_g[0]),
                                 vec(conv_ln_b[0]), nbatch, seq)

    w_out_b = w_out[0].astype(BF16)
    wr_t = w_router[0].astype(F32).T
    wr_hi = wr_t.astype(BF16)
    wr_lo = (wr_t - wr_hi.astype(F32)).astype(BF16)
    h1, h1rows, logits = _mix_call(x2d, attn, conv, gin, bin_, w_out_b[:ATTN_W], w_out_b[ATTN_W:],
                                   vec(ln1_g[0]), vec(ln1_b[0]), wr_hi, wr_lo)

    idx, wts_t, rank, cnt = _route_call(logits, router_bias[0].astype(F32).reshape(N_EXPERTS, 1))

    tm = TM_EXP
    n_tiles = (t * TOP_K) // tm + N_EXPERTS
    counts = cnt[:, 0]
    tiles_e = (counts + tm - 1) // tm
    tile_end = jnp.cumsum(tiles_e).astype(I32)
    tile_start = (tile_end - tiles_e).astype(I32)
    offs = tile_start * tm
    tile_id = jnp.arange(n_tiles, dtype=I32)
    lo = jnp.maximum(tile_id[:, None] * tm, offs[None, :])
    hi = jnp.minimum((tile_id[:, None] + 1) * tm, (offs + counts)[None, :])
    tile_rows = jnp.sum(jnp.clip(hi - lo, 0, tm), axis=1).astype(I32)
    n_valid = tile_end[-1:]

    xs, dest = _sc_dispatch_call(idx, rank, offs, h1rows.reshape(t, ROW_CHUNKS, LANES), n_tiles * tm)
    xs = xs.reshape(n_tiles * tm * ROW_CHUNKS, LANES)
    ys = _expert_call(tile_start, tile_end, tile_rows, n_valid, xs, w_gate[0], w_up[0], w_down[0], n_tiles * tm)
    gathered = _sc_gather_call(dest, ys.reshape(n_tiles * tm, ROW_CHUNKS, LANES))
    gathered = gathered.reshape(TOP_K, t * ROW_CHUNKS, LANES)
    out = _combine_call(wts_t, h1, gathered, ws_gate[0].astype(BF16), ws_up[0].astype(BF16),
                        ws_down[0].astype(BF16), vec(ln2_g[0]), vec(ln2_b[0]))
    return out.reshape(nbatch, seq, D_MODEL)
```

```python
import functools
import math

import numpy as np
import jax
import jax.numpy as jnp
from jax import lax
from jax.experimental import pallas as pl
from jax.experimental.pallas import tpu as pltpu
from jax.experimental.pallas import tpu_sc as plsc

F32 = jnp.float32
BF16 = jnp.bfloat16
I32 = jnp.int32
U32 = jnp.uint32
SDS = jax.ShapeDtypeStruct

D_MODEL = 1024
HALF = D_MODEL // 2
LANES = 128
SUBLANES = 8
ROW_CHUNKS = HALF // LANES
N_META = 16
HEAD_DIM = 64
N_Q_HEADS = 8
N_KV_HEADS = 2
GQA_GROUP = N_Q_HEADS // N_KV_HEADS
ATTN_W = N_Q_HEADS * HEAD_DIM
KV_W = N_KV_HEADS * HEAD_DIM
WINDOW = 128
BLOCK = 128
CONV_CH = D_MODEL - ATTN_W
CONV_K = 31
IN_W = ATTN_W + 2 * KV_W + 2 * CONV_CH
NUM_BUCKETS = 32
MAX_EXACT = NUM_BUCKETS // 2
REL_MAX_DIST = 128
N_EXPERTS = 256
TOP_K = 8
N_GROUPS = 8
GROUP_SIZE = N_EXPERTS // N_GROUPS
TOPK_GROUPS = 4
EXPERT_FF = 256
SHARED_FF = 256
ROUTED_SCALE = 2.5
DEPTH = 1
ALPHA = (2.0 * DEPTH) ** 0.25
LN_EPS = 1e-5
NEG = -1e30
PAD_FRONT = (-N_META) % BLOCK

VMEM_LIMIT = 48 * 1024 * 1024

TQ_PROJ = 1024
PROJ_CHAINS = 4
ATTN_QBLOCKS = 2
T_CONV = 256
CONV_HALO = 32
R_CONV = 64
TQ_MIX = 1024
MIX_CHAINS = 4
TN_ROUTE = 256
TM_EXP = 256
X_SLOTS = 8
X_AHEAD = 4
Y_SLOTS = 8
Y_LAG = 2
W_SLOTS = 3
W_AHEAD = 2
TN_COMB = 512


def _cparams(*sem):
    return pltpu.CompilerParams(dimension_semantics=sem, vmem_limit_bytes=VMEM_LIMIT)


def _layer_norm(x, g, b):
    mu = jnp.mean(x, axis=-1, keepdims=True)
    xc = x - mu
    var = jnp.mean(xc * xc, axis=-1, keepdims=True)
    return xc * lax.rsqrt(var + LN_EPS) * g + b


def _sigmoid(x):
    return 1.0 / (1.0 + jnp.exp(-x))


def _pack_rows(lo_half, hi_half):
    lo = lax.bitcast_convert_type(lo_half.astype(BF16).astype(F32), U32)
    hi = lax.bitcast_convert_type(hi_half.astype(BF16).astype(F32), U32)
    return lax.shift_right_logical(lo, jnp.uint32(16)) | hi


def _unpack_rows(p):
    lo = lax.bitcast_convert_type(lax.shift_left(p, jnp.uint32(16)), F32)
    hi = lax.bitcast_convert_type(p & jnp.uint32(0xFFFF0000), F32)
    return lo, hi


def _chunk_index(start, j, n, lead):
    rows = pl.ds(start + j, n, stride=ROW_CHUNKS)
    return (rows, slice(None)) if lead is None else (lead, rows, slice(None))


def _store_packed(ref, start, n, packed, lead=None):
    for j in range(ROW_CHUNKS):
        ref[_chunk_index(start, j, n, lead)] = packed[:, j * LANES:(j + 1) * LANES]


def _load_packed_bf16(ref, start, n, lead=None):
    halves = [_unpack_rows(ref[_chunk_index(start, j, n, lead)]) for j in range(ROW_CHUNKS)]
    return jnp.concatenate([h[0] for h in halves] + [h[1] for h in halves], axis=1).astype(BF16)


def _proj_kernel(chains, x_ref, g_ref, b_ref, w_ref, q_ref, k_ref, v_ref, u_ref):
    rows = x_ref.shape[0] // chains
    for c in range(chains):
        r = slice(c * rows, (c + 1) * rows)
        h = _layer_norm(x_ref[r, :], g_ref[...], b_ref[...])
        p = jnp.dot(h.astype(BF16), w_ref[...], preferred_element_type=F32)
        q_ref[r, :] = (p[:, :ATTN_W] * (HEAD_DIM ** -0.5)).astype(BF16)
        k_ref[r, :] = p[:, ATTN_W:ATTN_W + KV_W].astype(BF16)
        v_ref[r, :] = p[:, ATTN_W + KV_W:ATTN_W + 2 * KV_W].astype(BF16)
        a = p[:, ATTN_W + 2 * KV_W:ATTN_W + 2 * KV_W + CONV_CH]
        gate = p[:, ATTN_W + 2 * KV_W + CONV_CH:]
        u_ref[r, :] = a * _sigmoid(gate)


def _proj_call(x2d, gin, bin_, w_in_b, tq):
    t = x2d.shape[0]
    row = lambda i: (i, 0)
    fix = lambda i: (0, 0)
    chains = PROJ_CHAINS if tq % (PROJ_CHAINS * BLOCK) == 0 else 1
    return pl.pallas_call(
        functools.partial(_proj_kernel, chains),
        grid=(t // tq,),
        in_specs=[pl.BlockSpec((tq, D_MODEL), row), pl.BlockSpec((1, D_MODEL), fix),
                  pl.BlockSpec((1, D_MODEL), fix), pl.BlockSpec((D_MODEL, IN_W), fix)],
        out_specs=[pl.BlockSpec((tq, ATTN_W), row), pl.BlockSpec((tq, KV_W), row),
                   pl.BlockSpec((tq, KV_W), row), pl.BlockSpec((tq, CONV_CH), row)],
        out_shape=[SDS((t, ATTN_W), BF16), SDS((t, KV_W), BF16), SDS((t, KV_W), BF16), SDS((t, CONV_CH), F32)],
        compiler_params=_cparams("arbitrary"),
        name="ln_in_proj",
    )(x2d, gin, bin_, w_in_b)


def _attn_stages(sinks_ref, q_ref, kc_ref, kp_ref, vc_ref, vp_ref, km_ref, vm_ref, bias_ref, o_ref):
    first = pl.program_id(1) == 0
    kp = jnp.where(first, km_ref[...], kp_ref[...])
    vp = jnp.where(first, vm_ref[...], vp_ref[...])
    k = jnp.concatenate([kp, kc_ref[...]], axis=0)
    v = jnp.concatenate([vp, vc_ref[...]], axis=0)
    col = lax.broadcasted_iota(I32, (BLOCK, 2 * BLOCK), 1)
    pad_bias = jnp.where(jnp.logical_and(first, col < PAD_FRONT), NEG, 0.0).astype(F32)
    def block(a):
        q = q_ref[a * BLOCK:(a + 1) * BLOCK, :]
        kw = k[a * BLOCK:(a + 2) * BLOCK, :]
        vw = v[a * BLOCK:(a + 2) * BLOCK, :]
        outs = []
        for h in range(N_Q_HEADS):
            g = h // GQA_GROUP
            qh = q[:, h * HEAD_DIM:(h + 1) * HEAD_DIM]
            kg = kw[:, g * HEAD_DIM:(g + 1) * HEAD_DIM]
            vg = vw[:, g * HEAD_DIM:(g + 1) * HEAD_DIM]
            s = lax.dot_general(qh, kg, (((1,), (1,)), ((), ())), preferred_element_type=F32)
            s = s + bias_ref[h]
            if a == 0:
                s = s + pad_bias
            sink = sinks_ref[h]
            m = jnp.maximum(jnp.max(s, axis=-1, keepdims=True), sink)
            p = jnp.exp(s - m)
            den = jnp.sum(p, axis=-1, keepdims=True) + jnp.exp(sink - m)
            o = jnp.dot(p.astype(BF16), vg, preferred_element_type=F32)
            outs.append(o / den)
        o_ref[a * BLOCK:(a + 1) * BLOCK, :] = jnp.concatenate(outs, axis=1).astype(BF16)

    return [functools.partial(block, a) for a in range(ATTN_QBLOCKS)]


def _rel_bias_table(rel_bias):
    qi = np.arange(BLOCK, dtype=np.int32)[:, None]
    kj = np.arange(2 * BLOCK, dtype=np.int32)[None, :]
    dist = BLOCK + qi - kj
    dc = np.clip(dist, 0, WINDOW - 1)
    nf = np.maximum(dc, 1).astype(np.float32)
    large = MAX_EXACT + (np.log(nf / np.float32(MAX_EXACT)) / np.float32(math.log(REL_MAX_DIST / MAX_EXACT))
                         * np.float32(NUM_BUCKETS - MAX_EXACT)).astype(np.int32)
    large = np.minimum(large, NUM_BUCKETS - 1)
    bucket = np.where(dc < MAX_EXACT, dc, large)
    in_window = (dist >= 0) & (dist < WINDOW)
    onehot = (bucket.reshape(-1, 1) == np.arange(NUM_BUCKETS)[None, :]).astype(np.float32)
    bias = jnp.dot(jnp.asarray(onehot), rel_bias.astype(F32), precision=lax.Precision.HIGHEST)
    bias = jnp.transpose(bias.reshape(BLOCK, 2 * BLOCK, N_Q_HEADS), (2, 0, 1))
    return jnp.where(in_window[None], bias, NEG)


def _conv_stages(uc_ref, up_ref, um_ref, w_ref, cb_ref, g_ref, b_ref, o_ref, s_ref, sh_ref):
    first = pl.program_id(1) == 0
    s_ref[0:CONV_HALO, :] = jnp.where(first, um_ref[...], up_ref[...])
    s_ref[CONV_HALO:CONV_HALO + T_CONV, :] = uc_ref[...]
    off = CONV_HALO - (CONV_K - 1)
    span = sh_ref.shape[1]
    for p in range(1, SUBLANES):
        sh_ref[p] = s_ref[p:p + span, :]
    def chunk(c):
        acc = jnp.zeros((R_CONV, CONV_CH), F32) + cb_ref[...]
        for kk in range(CONV_K):
            p, a = (off + kk) % SUBLANES, (off + kk) // SUBLANES * SUBLANES
            if p == 0:
                win = s_ref[c + a:c + a + R_CONV, :]
            else:
                win = sh_ref[p, c + a:c + a + R_CONV, :]
            acc = acc + win * w_ref[kk:kk + 1, :]
        y = _layer_norm(acc, g_ref[...], b_ref[...])
        o_ref[c:c + R_CONV, :] = (y * _sigmoid(y)).astype(BF16)

    return [functools.partial(chunk, c) for c in range(0, T_CONV, R_CONV)]


def _attn_conv_kernel(sinks_ref, q_ref, kc_ref, kp_ref, vc_ref, vp_ref, km_ref, vm_ref, bias_ref,
                      uc_ref, up_ref, um_ref, w_ref, cb_ref, g_ref, b_ref, ao_ref, co_ref, s_ref, sh_ref):
    conv = _conv_stages(uc_ref, up_ref, um_ref, w_ref, cb_ref, g_ref, b_ref, co_ref, s_ref, sh_ref)
    attn = _attn_stages(sinks_ref, q_ref, kc_ref, kp_ref, vc_ref, vp_ref, km_ref, vm_ref, bias_ref, ao_ref)
    per = len(conv) // len(attn)
    for a, attn_block in enumerate(attn):
        for chunk in conv[a * per:(a + 1) * per]:
            chunk()
        attn_block()


def _attn_conv_call(q, k, v, k_meta, v_meta, bias, sinks, u, u_meta_halo, conv_w, conv_b, g, b, nbatch, seq):
    t = q.shape[0]
    rows = ATTN_QBLOCKS * BLOCK
    assert rows == T_CONV
    nstep = seq // rows
    per_blk = rows // BLOCK
    per_halo = rows // CONV_HALO
    cur = lambda bb, j: (bb * nstep + j, 0)
    prev_blk = lambda bb, j: (jnp.maximum((bb * nstep + j) * per_blk - 1, 0), 0)
    prev_halo = lambda bb, j: (jnp.maximum((bb * nstep + j) * per_halo - 1, 0), 0)
    fix = lambda bb, j: (0, 0)
    return pl.pallas_call(
        _attn_conv_kernel,
        grid=(nbatch, nstep),
        in_specs=[pl.BlockSpec(memory_space=pltpu.SMEM),
                  pl.BlockSpec((rows, ATTN_W), cur),
                  pl.BlockSpec((rows, KV_W), cur), pl.BlockSpec((BLOCK, KV_W), prev_blk),
                  pl.BlockSpec((rows, KV_W), cur), pl.BlockSpec((BLOCK, KV_W), prev_blk),
                  pl.BlockSpec((BLOCK, KV_W), fix), pl.BlockSpec((BLOCK, KV_W), fix),
                  pl.BlockSpec((N_Q_HEADS, BLOCK, 2 * BLOCK), lambda bb, j: (0, 0, 0)),
                  pl.BlockSpec((rows, CONV_CH), cur), pl.BlockSpec((CONV_HALO, CONV_CH), prev_halo),
                  pl.BlockSpec((CONV_HALO, CONV_CH), fix), pl.BlockSpec((CONV_K, CONV_CH), fix),
                  pl.BlockSpec((1, CONV_CH), fix), pl.BlockSpec((1, CONV_CH), fix), pl.BlockSpec((1, CONV_CH), fix)],
        out_specs=[pl.BlockSpec((rows, ATTN_W), cur), pl.BlockSpec((rows, CONV_CH), cur)],
        out_shape=[SDS((t, ATTN_W), BF16), SDS((t, CONV_CH), BF16)],
        scratch_shapes=[pltpu.VMEM((CONV_HALO + T_CONV, CONV_CH), F32),
                        pltpu.VMEM((SUBLANES, T_CONV + CONV_HALO - SUBLANES, CONV_CH), F32)],
        compiler_params=_cparams("arbitrary", "arbitrary"),
        name="attn_conv",
    )(sinks, q, k, k, v, v, k_meta, v_meta, bias, u, u, u_meta_halo, conv_w, conv_b, g, b)


def _mix_kernel(x_ref, at_ref, cv_ref, gin_ref, bin_ref, woa_ref, woc_ref, g1_ref, b1_ref,
                wrh_ref, wrl_ref, h1_ref, h1r_ref, lg_ref):
    rows = x_ref.shape[0] // MIX_CHAINS
    nt = (((1,), (1,)), ((), ()))
    for c in range(MIX_CHAINS):
        r = slice(c * rows, (c + 1) * rows)
        h = _layer_norm(x_ref[r, :], gin_ref[...], bin_ref[...])
        mix = (jnp.dot(at_ref[r, :], woa_ref[...], preferred_element_type=F32)
               + jnp.dot(cv_ref[r, :], woc_ref[...], preferred_element_type=F32))
        h1 = _layer_norm(ALPHA * h + mix, g1_ref[...], b1_ref[...])
        h1_ref[r, :] = h1
        _store_packed(h1r_ref, c * rows * ROW_CHUNKS, rows, _pack_rows(h1[:, :HALF], h1[:, HALF:]))
        hh = h1.astype(BF16)
        hl = (h1 - hh.astype(F32)).astype(BF16)
        lg = lax.dot_general(wrh_ref[...], hh, nt, preferred_element_type=F32)
        lg = lg + lax.dot_general(wrh_ref[...], hl, nt, preferred_element_type=F32)
        lg = lg + lax.dot_general(wrl_ref[...], hh, nt, preferred_element_type=F32)
        lg_ref[:, r] = lg


def _mix_call(x2d, attn, conv, gin, bin_, woa, woc, g1, b1, wrh, wrl):
    t = x2d.shape[0]
    tq = TQ_MIX
    row = lambda i: (i, 0)
    fix = lambda i: (0, 0)
    return pl.pallas_call(
        _mix_kernel,
        grid=(t // tq,),
        in_specs=[pl.BlockSpec((tq, D_MODEL), row), pl.BlockSpec((tq, ATTN_W), row), pl.BlockSpec((tq, CONV_CH), row),
                  pl.BlockSpec((1, D_MODEL), fix), pl.BlockSpec((1, D_MODEL), fix),
                  pl.BlockSpec((ATTN_W, D_MODEL), fix), pl.BlockSpec((CONV_CH, D_MODEL), fix),
                  pl.BlockSpec((1, D_MODEL), fix), pl.BlockSpec((1, D_MODEL), fix),
                  pl.BlockSpec((N_EXPERTS, D_MODEL), fix), pl.BlockSpec((N_EXPERTS, D_MODEL), fix)],
        out_specs=[pl.BlockSpec((tq, D_MODEL), row), pl.BlockSpec((tq * ROW_CHUNKS, LANES), row),
                   pl.BlockSpec((N_EXPERTS, tq), lambda i: (0, i))],
        out_shape=[SDS((t, D_MODEL), F32), SDS((t * ROW_CHUNKS, LANES), U32), SDS((N_EXPERTS, t), F32)],
        compiler_params=_cparams("arbitrary"),
        name="mix_ln1",
    )(x2d, attn, conv, gin, bin_, woa, woc, g1, b1, wrh, wrl)


def _first_argmax(x, rows, nrows):
    m = jnp.max(x, axis=0, keepdims=True)
    idx = jnp.min(jnp.where(x == m, rows, nrows), axis=0, keepdims=True)
    return m, idx


def _route_tile(logits, rbias, carry):
    tn = logits.shape[1]
    scores = _sigmoid(logits)
    choice = scores + rbias
    rows = lax.broadcasted_iota(I32, (N_EXPERTS, tn), 0)
    rows_g = lax.broadcasted_iota(I32, (GROUP_SIZE, tn), 0)
    rows_8 = lax.broadcasted_iota(I32, (N_GROUPS, tn), 0)

    gs = []
    for g in range(N_GROUPS):
        xg = choice[g * GROUP_SIZE:(g + 1) * GROUP_SIZE, :]
        m1, i1 = _first_argmax(xg, rows_g, GROUP_SIZE)
        m2 = jnp.max(jnp.where(rows_g == i1, -jnp.inf, xg), axis=0, keepdims=True)
        gs.append(m1 + m2)
    gsc = jnp.concatenate(gs, axis=0)
    gsel = jnp.zeros((N_GROUPS, tn), F32)
    for _ in range(TOPK_GROUPS):
        _, gi = _first_argmax(gsc, rows_8, N_GROUPS)
        hit = rows_8 == gi
        gsel = jnp.where(hit, 1.0, gsel)
        gsc = jnp.where(hit, -jnp.inf, gsc)
    emask = jnp.concatenate(
        [jnp.broadcast_to(gsel[g:g + 1, :], (GROUP_SIZE, tn)) for g in range(N_GROUPS)], axis=0)
    masked = jnp.where(emask > 0.5, choice, NEG)

    sel_all = jnp.zeros((N_EXPERTS, tn), F32)
    hits, idxs, ws = [], [], []
    for _ in range(TOP_K):
        _, ii = _first_argmax(masked, rows, N_EXPERTS)
        hit = rows == ii
        hits.append(hit)
        idxs.append(ii)
        ws.append(jnp.sum(jnp.where(hit, scores, 0.0), axis=0, keepdims=True))
        sel_all = jnp.where(hit, 1.0, sel_all)
        masked = jnp.where(hit, -jnp.inf, masked)
    wsum = ws[0]
    for w in ws[1:]:
        wsum = wsum + w
    idx = jnp.concatenate(idxs, axis=0)
    wts = jnp.concatenate([w / wsum * ROUTED_SCALE for w in ws], axis=0)

    r_i = lax.broadcasted_iota(I32, (tn, tn), 0)
    c_i = lax.broadcasted_iota(I32, (tn, tn), 1)
    upper = jnp.where(r_i < c_i, 1.0, 0.0).astype(BF16)
    sel_b = sel_all.astype(BF16)
    before = jnp.dot(sel_b, upper, preferred_element_type=F32)
    before = before + jnp.concatenate([carry] * (tn // LANES), axis=1)
    rank = jnp.concatenate(
        [jnp.sum(jnp.where(h, before, 0.0), axis=0, keepdims=True) for h in hits], axis=0).astype(I32)
    carry = carry + jnp.dot(sel_b, jnp.ones((tn, LANES), BF16), preferred_element_type=F32)
    return idx, wts, rank, carry


def _route_kernel(lg_ref, rb_ref, idx_ref, wts_ref, rank_ref, cnt_ref, carry_ref):
    @pl.when(pl.program_id(0) == 0)
    def _():
        carry_ref[...] = jnp.zeros_like(carry_ref)

    idx, wts, rank, carry = _route_tile(lg_ref[...], rb_ref[...], carry_ref[...])
    idx_ref[...] = idx
    wts_ref[...] = wts.T
    rank_ref[...] = rank
    carry_ref[...] = carry
    cnt_ref[...] = carry.astype(I32)


def _route_call(lg, rbias):
    t = lg.shape[1]
    tn = TN_ROUTE
    col = lambda i: (0, i)
    return pl.pallas_call(
        _route_kernel,
        grid=(t // tn,),
        in_specs=[pl.BlockSpec((N_EXPERTS, tn), col), pl.BlockSpec((N_EXPERTS, 1), lambda i: (0, 0))],
        out_specs=[pl.BlockSpec((TOP_K, tn), col), pl.BlockSpec((tn, TOP_K), lambda i: (i, 0)),
                   pl.BlockSpec((TOP_K, tn), col), pl.BlockSpec((N_EXPERTS, LANES), lambda i: (0, 0))],
        out_shape=[SDS((TOP_K, t), I32), SDS((t, TOP_K), F32), SDS((TOP_K, t), I32), SDS((N_EXPERTS, LANES), I32)],
        scratch_shapes=[pltpu.VMEM((N_EXPERTS, LANES), F32)],
        compiler_params=_cparams("arbitrary"),
        name="route",
    )(lg, rbias)


SC_CORES = 2
SC_SUBCORES = 16
SC_CHUNK = 128
SC_LANES = 16
SC_BUFS = 2


def _sc_worker_chunks(t):
    per_worker = t // (SC_CORES * SC_SUBCORES)
    assert per_worker % SC_CHUNK == 0
    return per_worker


def _sc_dispatch_call(idx, rank, offs, h1rows3, n_rows):
    t = idx.shape[1]
    per_worker = _sc_worker_chunks(t)
    mesh = plsc.VectorSubcoreMesh(core_axis_name="c", subcore_axis_name="s")

    @functools.partial(
        pl.kernel, mesh=mesh, out_type=[SDS((n_rows, ROW_CHUNKS, LANES), U32), SDS((n_rows,), I32)],
        scratch_types=[pltpu.VMEM((TOP_K, SC_CHUNK), I32), pltpu.VMEM((TOP_K, SC_CHUNK), I32),
                       pltpu.VMEM((TOP_K, SC_CHUNK), I32),
                       pltpu.VMEM((N_EXPERTS,), I32), pltpu.VMEM((SC_CHUNK, ROW_CHUNKS, LANES), U32),
                       pltpu.SemaphoreType.DMA],
        compiler_params=pltpu.CompilerParams(needs_layout_passes=False),
        name="sc_dispatch")
    def body(h_hbm, idx_hbm, rank_hbm, offs_hbm, xs_hbm, back_hbm, idx_v, rank_v, home_v, offs_v, rows_v, sem):
        wid = lax.axis_index("s") * SC_CORES + lax.axis_index("c")
        pltpu.sync_copy(offs_hbm, offs_v)

        @pl.loop(0, per_worker // SC_CHUNK)
        def _(i):
            t0 = wid * per_worker + i * SC_CHUNK
            pltpu.sync_copy(idx_hbm.at[:, pl.ds(t0, SC_CHUNK)], idx_v)
            pltpu.sync_copy(rank_hbm.at[:, pl.ds(t0, SC_CHUNK)], rank_v)
            pltpu.sync_copy(h_hbm.at[pl.ds(t0, SC_CHUNK)], rows_v)
            for kk in range(TOP_K):
                @pl.loop(0, SC_CHUNK // SC_LANES)
                def _(c):
                    lanes = pl.ds(c * SC_LANES, SC_LANES)
                    idx_v[kk, lanes] = plsc.load_gather(offs_v, [idx_v[kk, lanes]]) + rank_v[kk, lanes]
                    home_v[kk, lanes] = kk * t + t0 + c * SC_LANES + lax.iota(I32, SC_LANES)
            copies = [pltpu.async_copy(rows_v, xs_hbm.at[idx_v.at[kk]], sem) for kk in range(TOP_K)]
            copies += [pltpu.async_copy(home_v.at[kk], back_hbm.at[idx_v.at[kk]], sem) for kk in range(TOP_K)]
            for c in copies:
                c.wait()

    return body(h1rows3, idx, rank, offs)


def _sc_gather_call(dest, ys3):
    t = dest.shape[1]
    per_worker = _sc_worker_chunks(t)
    mesh = plsc.VectorSubcoreMesh(core_axis_name="c", subcore_axis_name="s")

    half = SC_CHUNK // 2
    items = [(kk, h) for kk in range(TOP_K) for h in range(2)]

    @functools.partial(
        pl.kernel, mesh=mesh, out_type=SDS((TOP_K * t, ROW_CHUNKS, LANES), U32),
        scratch_types=[pltpu.VMEM((TOP_K, SC_CHUNK), I32), pltpu.VMEM((SC_BUFS, half, ROW_CHUNKS, LANES), U32),
                       pltpu.SemaphoreType.DMA((SC_BUFS,)), pltpu.SemaphoreType.DMA((SC_BUFS,))],
        name="sc_gather")
    def body(ys_hbm, dest_hbm, out_hbm, idx_v, rows_v, gsem, wsem):
        wid = lax.axis_index("s") * SC_CORES + lax.axis_index("c")

        @pl.loop(0, per_worker // SC_CHUNK)
        def _(i):
            t0 = wid * per_worker + i * SC_CHUNK
            pltpu.sync_copy(dest_hbm.at[:, pl.ds(t0, SC_CHUNK)], idx_v)

            def gather(j):
                kk, h = items[j]
                b = j % SC_BUFS
                return pltpu.make_async_copy(ys_hbm.at[idx_v.at[kk, pl.ds(h * half, half)]], rows_v.at[b], gsem.at[b])

            def put(j):
                kk, h = items[j]
                b = j % SC_BUFS
                return pltpu.make_async_copy(rows_v.at[b], out_hbm.at[pl.ds(kk * t + t0 + h * half, half)], wsem.at[b])

            ahead = SC_BUFS - 1
            n = len(items)
            for j in range(ahead):
                gather(j).start()
            for j in range(n):
                if j + ahead < n:
                    if j >= 1:
                        put(j - 1).wait()
                    gather(j + ahead).start()
                gather(j).wait()
                put(j).start()
            for j in range(max(n - ahead - 1, 0), n):
                put(j).wait()

    return body(ys3, dest)


def _expert_kernel(ts_ref, te_ref, tr_ref, nv_ref, wg_hbm, wu_hbm, wd_hbm, xs_hbm, back_hbm, out_hbm,
                   xbuf, ybuf, back_s, wg_f, wu_f, wd_f, wg_b, wu_b, wd_b, xsem, bsem, ysem, wsem):
    e = pl.program_id(0)
    rows = xbuf.shape[1]
    tm = rows // ROW_CHUNKS
    g0, g1, nv = ts_ref[e], te_ref[e], nv_ref[0]

    def x_copy(g):
        s = g % X_SLOTS
        return pltpu.make_async_copy(xs_hbm.at[pl.ds(pl.multiple_of(g * rows, rows), rows), :], xbuf.at[s], xsem.at[s])

    def back_copy(g):
        s = g % X_SLOTS
        return pltpu.make_async_copy(back_hbm.at[g], back_s.at[s], bsem.at[s])

    def rows_out_wait(g):
        pltpu.make_async_copy(ybuf.at[0], out_hbm.at[pl.ds(0, rows), :], ysem.at[g % Y_SLOTS]).wait()

    def rows_out_start(q, unrolled):
        s = q % Y_SLOTS
        sb = q % X_SLOTS

        def one(r):
            d = pl.multiple_of(back_s[sb, r], ROW_CHUNKS)
            pltpu.make_async_copy(ybuf.at[s, pl.ds(r * ROW_CHUNKS, ROW_CHUNKS), :],
                                  out_hbm.at[pl.ds(d, ROW_CHUNKS), :], ysem.at[s]).start(priority=1)

        if unrolled:
            for r in range(tm):
                one(r)
        else:
            pl.loop(0, tm)(one)

    def w_copies(ex):
        s = ex % W_SLOTS
        return (pltpu.make_async_copy(wg_hbm.at[ex], wg_f.at[s], wsem.at[s]),
                pltpu.make_async_copy(wu_hbm.at[ex], wu_f.at[s], wsem.at[s]),
                pltpu.make_async_copy(wd_hbm.at[ex], wd_f.at[s], wsem.at[s]))

    n_exp = pl.num_programs(0)

    @pl.when(e == 0)
    def _():
        for ex in range(W_AHEAD):
            for c in w_copies(ex):
                c.start()

    @pl.when(e + W_AHEAD < n_exp)
    def _():
        for c in w_copies(e + W_AHEAD):
            c.start()

    for c in w_copies(e):
        c.wait()

    def compute_tile(g):
        x = _load_packed_bf16(xbuf, 0, tm, lead=g % X_SLOTS)
        gate = jnp.dot(x, wg_b[...], preferred_element_type=F32)
        up = jnp.dot(x, wu_b[...], preferred_element_type=F32)
        live = lax.broadcasted_iota(I32, (tm, EXPERT_FF), 0) < tr_ref[g]
        hid = jnp.where(live, gate * _sigmoid(gate) * up, 0.0).astype(BF16)
        y = jnp.dot(hid, wd_b[...], preferred_element_type=F32)
        return _pack_rows(y[:, :HALF], y[:, HALF:])

    def run_tiles(g, n):
        for r in range(n):
            x_copy(g + r).wait()
            back_copy(g + r - Y_LAG).wait()

            @pl.when(g + r + X_AHEAD < nv)
            def _():
                x_copy(g + r + X_AHEAD).start(priority=1)

            @pl.when(g + r - Y_LAG + X_AHEAD < nv)
            def _():
                back_copy(g + r - Y_LAG + X_AHEAD).start(priority=1)

            @pl.when(g + r >= Y_SLOTS)
            def _():
                rows_out_wait(g + r - Y_SLOTS)

        for r in range(n):
            rows_out_start(g + r - Y_LAG, True)
        packed = [compute_tile(g + r) for r in range(n)]
        for r in range(n):
            _store_packed(ybuf, 0, tm, packed[r], lead=(g + r) % Y_SLOTS)

    @pl.when(e == 0)
    def _():
        for s in range(Y_LAG):
            ybuf[s] = jnp.zeros(ybuf.shape[1:], U32)
        for g in range(X_AHEAD):
            @pl.when(Y_LAG + g < nv)
            def _():
                x_copy(Y_LAG + g).start(priority=1)
            back_copy(g).start(priority=1)

    @pl.when(g1 > g0)
    def _():
        ws = e % W_SLOTS
        wg_b[...] = wg_f[ws].astype(BF16)
        wu_b[...] = wu_f[ws].astype(BF16)
        wd_b[...] = wd_f[ws].astype(BF16)
        n_tiles = g1 - g0

        def pair(p, c):
            run_tiles(g0 + 2 * p, 2)
            return c

        lax.fori_loop(0, n_tiles // 2, pair, 0)

        @pl.when(n_tiles % 2 == 1)
        def _():
            run_tiles(g1 - 1, 1)

    @pl.when(e == pl.num_programs(0) - 1)
    def _():
        for b in range(Y_LAG, 0, -1):
            back_copy(nv - b).wait()
            rows_out_start(nv - b, False)
        for b in range(1, Y_SLOTS + 1):
            rows_out_wait(nv - b)


def _expert_call(tile_start, tile_end, tile_rows, n_valid, xs, back, w_gate, w_up, w_down, t):
    tm = TM_EXP
    assert (t * TOP_K) // tm >= Y_SLOTS and Y_SLOTS * tm <= t
    hbm = pl.BlockSpec(memory_space=pl.ANY)
    return pl.pallas_call(
        _expert_kernel,
        grid_spec=pltpu.PrefetchScalarGridSpec(
            num_scalar_prefetch=4,
            grid=(N_EXPERTS,),
            in_specs=[hbm, hbm, hbm, hbm, hbm],
            out_specs=hbm,
            scratch_shapes=[pltpu.VMEM((X_SLOTS, tm * ROW_CHUNKS, LANES), U32),
                            pltpu.VMEM((Y_SLOTS, tm * ROW_CHUNKS, LANES), U32),
                            pltpu.SMEM((X_SLOTS, tm), I32),
                            pltpu.VMEM((W_SLOTS, D_MODEL, EXPERT_FF), F32), pltpu.VMEM((W_SLOTS, D_MODEL, EXPERT_FF), F32),
                            pltpu.VMEM((W_SLOTS, EXPERT_FF, D_MODEL), F32),
                            pltpu.VMEM((D_MODEL, EXPERT_FF), BF16), pltpu.VMEM((D_MODEL, EXPERT_FF), BF16),
                            pltpu.VMEM((EXPERT_FF, D_MODEL), BF16),
                            pltpu.SemaphoreType.DMA((X_SLOTS,)), pltpu.SemaphoreType.DMA((X_SLOTS,)),
                            pltpu.SemaphoreType.DMA((Y_SLOTS,)), pltpu.SemaphoreType.DMA((W_SLOTS,))],
        ),
        out_shape=SDS(((TOP_K + 1) * t * ROW_CHUNKS, LANES), U32),
        compiler_params=_cparams("arbitrary"),
        name="experts",
    )(tile_start, tile_end, tile_rows, n_valid, w_gate, w_up, w_down, xs, back)


COMB_SUB = 32


def _combine_kernel(wts_ref, h1_ref, g_ref, wsg_ref, wsu_ref, wsd_ref, g2_ref, b2_ref, o_ref, routed_ref):
    tn = h1_ref.shape[0]
    for s0 in range(0, tn, COMB_SUB):
        acc = [jnp.zeros((COMB_SUB, LANES), F32) for _ in range(2 * ROW_CHUNKS)]
        for kk in range(TOP_K):
            wk = jnp.broadcast_to(wts_ref[s0:s0 + COMB_SUB, kk:kk + 1], (COMB_SUB, LANES))
            for cc in range(ROW_CHUNKS):
                lo, hi = _unpack_rows(g_ref[kk, pl.ds(s0 * ROW_CHUNKS + cc, COMB_SUB, stride=ROW_CHUNKS), :])
                acc[cc] = acc[cc] + wk * lo
                acc[ROW_CHUNKS + cc] = acc[ROW_CHUNKS + cc] + wk * hi
        routed_ref[s0:s0 + COMB_SUB, :] = jnp.concatenate(acc, axis=1)

    h1 = h1_ref[...]
    hb = h1.astype(BF16)
    sg = jnp.dot(hb, wsg_ref[...], preferred_element_type=F32)
    su = jnp.dot(hb, wsu_ref[...], preferred_element_type=F32)
    ff = jnp.dot((sg * _sigmoid(sg) * su).astype(BF16), wsd_ref[...], preferred_element_type=F32)
    o_ref[...] = _layer_norm(ALPHA * h1 + ff + routed_ref[...], g2_ref[...], b2_ref[...])


def _combine_call(wts_t, h1, gathered, wsg, wsu, wsd, g2, b2):
    t = h1.shape[0]
    tn = TN_COMB
    row = lambda i: (i, 0)
    fix = lambda i: (0, 0)
    return pl.pallas_call(
        _combine_kernel,
        grid=(t // tn,),
        in_specs=[pl.BlockSpec((tn, TOP_K), row),
                  pl.BlockSpec((tn, D_MODEL), row),
                  pl.BlockSpec((TOP_K, tn * ROW_CHUNKS, LANES), lambda i: (0, i, 0)),
                  pl.BlockSpec((D_MODEL, SHARED_FF), fix), pl.BlockSpec((D_MODEL, SHARED_FF), fix),
                  pl.BlockSpec((SHARED_FF, D_MODEL), fix),
                  pl.BlockSpec((1, D_MODEL), fix), pl.BlockSpec((1, D_MODEL), fix)],
        out_specs=pl.BlockSpec((tn, D_MODEL), row),
        out_shape=SDS((t, D_MODEL), F32),
        scratch_shapes=[pltpu.VMEM((tn, D_MODEL), F32)],
        compiler_params=_cparams("arbitrary"),
        name="combine_ln2",
    )(wts_t, h1, gathered, wsg, wsu, wsd, g2, b2)


def kernel(x, meta_tokens, ln_in_g, ln_in_b, rel_bias, w_in, conv_w, conv_b, conv_ln_g, conv_ln_b, sinks,
           w_out, ln1_g, ln1_b, w_router, router_bias, w_gate, w_up, w_down, ws_gate, ws_up, ws_down,
           ln2_g, ln2_b):
    nbatch, seq, d = x.shape
    t = nbatch * seq
    assert d == D_MODEL and w_in.shape[0] == DEPTH
    assert seq % (ATTN_QBLOCKS * BLOCK) == 0 and seq % T_CONV == 0
    assert all(t % tile == 0 for tile in (TQ_PROJ, TQ_MIX, TN_ROUTE, TN_COMB))
    x2d = x.reshape(t, D_MODEL)
    vec = lambda a: a.reshape(1, -1).astype(F32)
    gin, bin_ = vec(ln_in_g), vec(ln_in_b)
    w_in_b = w_in[0].astype(BF16)

    q, k, v, u = _proj_call(x2d, gin, bin_, w_in_b, TQ_PROJ)
    meta_blk = jnp.concatenate([jnp.zeros((PAD_FRONT, D_MODEL), F32), meta_tokens.astype(F32)], axis=0)
    _, k_meta, v_meta, u_meta = _proj_call(meta_blk, gin, bin_, w_in_b, BLOCK)

    u_halo = jnp.concatenate([jnp.zeros((CONV_HALO - N_META, CONV_CH), F32), u_meta[PAD_FRONT:]], axis=0)
    attn, conv = _attn_conv_call(q, k, v, k_meta, v_meta, _rel_bias_table(rel_bias), sinks[0].astype(F32),
                                 u, u_halo, conv_w[0].astype(F32), vec(conv_b[0]), vec(conv_ln_g[0]),
                                 vec(conv_ln_b[0]), nbatch, seq)

    w_out_b = w_out[0].astype(BF16)
    wr_t = w_router[0].astype(F32).T
    wr_hi = wr_t.astype(BF16)
    wr_lo = (wr_t - wr_hi.astype(F32)).astype(BF16)
    h1, h1rows, logits = _mix_call(x2d, attn, conv, gin, bin_, w_out_b[:ATTN_W], w_out_b[ATTN_W:],
                                   vec(ln1_g[0]), vec(ln1_b[0]), wr_hi, wr_lo)

    idx, wts_t, rank, cnt = _route_call(logits, router_bias[0].astype(F32).reshape(N_EXPERTS, 1))

    tm = TM_EXP
    n_tiles = Y_LAG + (t * TOP_K) // tm + N_EXPERTS
    counts = cnt[:, 0]
    tiles_e = (counts + tm - 1) // tm
    tile_end = (Y_LAG + jnp.cumsum(tiles_e)).astype(I32)
    tile_start = (tile_end - tiles_e).astype(I32)
    offs = tile_start * tm
    tile_id = jnp.arange(n_tiles, dtype=I32)
    lo = jnp.maximum(tile_id[:, None] * tm, offs[None, :])
    hi = jnp.minimum((tile_id[:, None] + 1) * tm, (offs + counts)[None, :])
    tile_rows = jnp.sum(jnp.clip(hi - lo, 0, tm), axis=1).astype(I32)
    n_valid = tile_end[-1:]

    xs, back = _sc_dispatch_call(idx, rank, offs, h1rows.reshape(t, ROW_CHUNKS, LANES), n_tiles * tm)
    xs = xs.reshape(n_tiles * tm * ROW_CHUNKS, LANES)
    col = jnp.arange(tm, dtype=I32)[None, :]
    spare = TOP_K * t + (tile_id[:, None] % Y_SLOTS) * tm + col
    back = jnp.where(col < tile_rows[:, None], back.reshape(n_tiles, tm), spare) * ROW_CHUNKS
    gathered = _expert_call(tile_start, tile_end, tile_rows, n_valid, xs, back, w_gate[0], w_up[0], w_down[0], t)
    gathered = gathered.reshape(TOP_K + 1, t * ROW_CHUNKS, LANES)
    out = _combine_call(wts_t, h1, gathered, ws_gate[0].astype(BF16), ws_up[0].astype(BF16),
                        ws_down[0].astype(BF16), vec(ln2_g[0]), vec(ln2_b[0]))
    return out.reshape(nbatch, seq, D_MODEL)
```

```python
import functools
import math

import numpy as np
import jax
import jax.numpy as jnp
from jax import lax
from jax.experimental import pallas as pl
from jax.experimental.pallas import tpu as pltpu
from jax.experimental.pallas import tpu_sc as plsc

F32 = jnp.float32
BF16 = jnp.bfloat16
I32 = jnp.int32
U32 = jnp.uint32
SDS = jax.ShapeDtypeStruct

D_MODEL = 1024
HALF = D_MODEL // 2
LANES = 128
SUBLANES = 8
ROW_CHUNKS = HALF // LANES
N_META = 16
HEAD_DIM = 64
N_Q_HEADS = 8
N_KV_HEADS = 2
GQA_GROUP = N_Q_HEADS // N_KV_HEADS
ATTN_W = N_Q_HEADS * HEAD_DIM
KV_W = N_KV_HEADS * HEAD_DIM
WINDOW = 128
BLOCK = 128
CONV_CH = D_MODEL - ATTN_W
CONV_K = 31
IN_W = ATTN_W + 2 * KV_W + 2 * CONV_CH
NUM_BUCKETS = 32
MAX_EXACT = NUM_BUCKETS // 2
REL_MAX_DIST = 128
N_EXPERTS = 256
TOP_K = 8
N_GROUPS = 8
GROUP_SIZE = N_EXPERTS // N_GROUPS
TOPK_GROUPS = 4
EXPERT_FF = 256
SHARED_FF = 256
ROUTED_SCALE = 2.5
DEPTH = 1
ALPHA = (2.0 * DEPTH) ** 0.25
LN_EPS = 1e-5
NEG = -1e30
PAD_FRONT = (-N_META) % BLOCK

VMEM_LIMIT = 48 * 1024 * 1024

TQ_PROJ = 1024
PROJ_CHAINS = 4
ATTN_QBLOCKS = 2
T_CONV = 256
CONV_HALO = 32
R_CONV = 64
TQ_MIX = 1024
MIX_CHAINS = 4
TN_ROUTE = 256
TM_EXP = 256
X_SLOTS = 8
X_AHEAD = 4
Y_SLOTS = 4
W_SLOTS = 3
W_AHEAD = 2
TN_COMB = 512


def _cparams(*sem):
    return pltpu.CompilerParams(dimension_semantics=sem, vmem_limit_bytes=VMEM_LIMIT)


def _layer_norm(x, g, b):
    mu = jnp.mean(x, axis=-1, keepdims=True)
    xc = x - mu
    var = jnp.mean(xc * xc, axis=-1, keepdims=True)
    return xc * lax.rsqrt(var + LN_EPS) * g + b


def _sigmoid(x):
    return 1.0 / (1.0 + jnp.exp(-x))


def _pack_rows(lo_half, hi_half):
    lo = lax.bitcast_convert_type(lo_half.astype(BF16).astype(F32), U32)
    hi = lax.bitcast_convert_type(hi_half.astype(BF16).astype(F32), U32)
    return lax.shift_right_logical(lo, jnp.uint32(16)) | hi


def _unpack_rows(p):
    lo = lax.bitcast_convert_type(lax.shift_left(p, jnp.uint32(16)), F32)
    hi = lax.bitcast_convert_type(p & jnp.uint32(0xFFFF0000), F32)
    return lo, hi


def _chunk_index(start, j, n, lead):
    rows = pl.ds(start + j, n, stride=ROW_CHUNKS)
    return (rows, slice(None)) if lead is None else (lead, rows, slice(None))


def _store_packed(ref, start, n, packed, lead=None):
    for j in range(ROW_CHUNKS):
        ref[_chunk_index(start, j, n, lead)] = packed[:, j * LANES:(j + 1) * LANES]


def _load_packed_bf16(ref, start, n, lead=None):
    halves = [_unpack_rows(ref[_chunk_index(start, j, n, lead)]) for j in range(ROW_CHUNKS)]
    return jnp.concatenate([h[0] for h in halves] + [h[1] for h in halves], axis=1).astype(BF16)


def _proj_kernel(chains, x_ref, g_ref, b_ref, w_ref, q_ref, k_ref, v_ref, u_ref):
    rows = x_ref.shape[0] // chains
    for c in range(chains):
        r = slice(c * rows, (c + 1) * rows)
        h = _layer_norm(x_ref[r, :], g_ref[...], b_ref[...])
        p = jnp.dot(h.astype(BF16), w_ref[...], preferred_element_type=F32)
        q_ref[r, :] = (p[:, :ATTN_W] * (HEAD_DIM ** -0.5)).astype(BF16)
        k_ref[r, :] = p[:, ATTN_W:ATTN_W + KV_W].astype(BF16)
        v_ref[r, :] = p[:, ATTN_W + KV_W:ATTN_W + 2 * KV_W].astype(BF16)
        a = p[:, ATTN_W + 2 * KV_W:ATTN_W + 2 * KV_W + CONV_CH]
        gate = p[:, ATTN_W + 2 * KV_W + CONV_CH:]
        u_ref[r, :] = a * _sigmoid(gate)


def _proj_call(x2d, gin, bin_, w_in_b, tq):
    t = x2d.shape[0]
    row = lambda i: (i, 0)
    fix = lambda i: (0, 0)
    chains = PROJ_CHAINS if tq % (PROJ_CHAINS * BLOCK) == 0 else 1
    return pl.pallas_call(
        functools.partial(_proj_kernel, chains),
        grid=(t // tq,),
        in_specs=[pl.BlockSpec((tq, D_MODEL), row), pl.BlockSpec((1, D_MODEL), fix),
                  pl.BlockSpec((1, D_MODEL), fix), pl.BlockSpec((D_MODEL, IN_W), fix)],
        out_specs=[pl.BlockSpec((tq, ATTN_W), row), pl.BlockSpec((tq, KV_W), row),
                   pl.BlockSpec((tq, KV_W), row), pl.BlockSpec((tq, CONV_CH), row)],
        out_shape=[SDS((t, ATTN_W), BF16), SDS((t, KV_W), BF16), SDS((t, KV_W), BF16), SDS((t, CONV_CH), F32)],
        compiler_params=_cparams("arbitrary"),
        name="ln_in_proj",
    )(x2d, gin, bin_, w_in_b)


def _attn_stages(sinks_ref, q_ref, kc_ref, kp_ref, vc_ref, vp_ref, km_ref, vm_ref, bias_ref, o_ref):
    first = pl.program_id(1) == 0
    kp = jnp.where(first, km_ref[...], kp_ref[...])
    vp = jnp.where(first, vm_ref[...], vp_ref[...])
    k = jnp.concatenate([kp, kc_ref[...]], axis=0)
    v = jnp.concatenate([vp, vc_ref[...]], axis=0)
    col = lax.broadcasted_iota(I32, (BLOCK, 2 * BLOCK), 1)
    pad_bias = jnp.where(jnp.logical_and(first, col < PAD_FRONT), NEG, 0.0).astype(F32)
    def block(a):
        q = q_ref[a * BLOCK:(a + 1) * BLOCK, :]
        kw = k[a * BLOCK:(a + 2) * BLOCK, :]
        vw = v[a * BLOCK:(a + 2) * BLOCK, :]
        outs = []
        for h in range(N_Q_HEADS):
            g = h // GQA_GROUP
            qh = q[:, h * HEAD_DIM:(h + 1) * HEAD_DIM]
            kg = kw[:, g * HEAD_DIM:(g + 1) * HEAD_DIM]
            vg = vw[:, g * HEAD_DIM:(g + 1) * HEAD_DIM]
            s = lax.dot_general(qh, kg, (((1,), (1,)), ((), ())), preferred_element_type=F32)
            s = s + bias_ref[h]
            if a == 0:
                s = s + pad_bias
            sink = sinks_ref[h]
            m = jnp.maximum(jnp.max(s, axis=-1, keepdims=True), sink)
            p = jnp.exp(s - m)
            den = jnp.sum(p, axis=-1, keepdims=True) + jnp.exp(sink - m)
            o = jnp.dot(p.astype(BF16), vg, preferred_element_type=F32)
            outs.append(o / den)
        o_ref[a * BLOCK:(a + 1) * BLOCK, :] = jnp.concatenate(outs, axis=1).astype(BF16)

    return [functools.partial(block, a) for a in range(ATTN_QBLOCKS)]


def _rel_bias_table(rel_bias):
    qi = np.arange(BLOCK, dtype=np.int32)[:, None]
    kj = np.arange(2 * BLOCK, dtype=np.int32)[None, :]
    dist = BLOCK + qi - kj
    dc = np.clip(dist, 0, WINDOW - 1)
    nf = np.maximum(dc, 1).astype(np.float32)
    large = MAX_EXACT + (np.log(nf / np.float32(MAX_EXACT)) / np.float32(math.log(REL_MAX_DIST / MAX_EXACT))
                         * np.float32(NUM_BUCKETS - MAX_EXACT)).astype(np.int32)
    large = np.minimum(large, NUM_BUCKETS - 1)
    bucket = np.where(dc < MAX_EXACT, dc, large)
    in_window = (dist >= 0) & (dist < WINDOW)
    onehot = (bucket.reshape(-1, 1) == np.arange(NUM_BUCKETS)[None, :]).astype(np.float32)
    bias = jnp.dot(jnp.asarray(onehot), rel_bias.astype(F32), precision=lax.Precision.HIGHEST)
    bias = jnp.transpose(bias.reshape(BLOCK, 2 * BLOCK, N_Q_HEADS), (2, 0, 1))
    return jnp.where(in_window[None], bias, NEG)


def _conv_stages(uc_ref, up_ref, um_ref, w_ref, cb_ref, g_ref, b_ref, o_ref, s_ref, sh_ref):
    first = pl.program_id(1) == 0
    s_ref[0:CONV_HALO, :] = jnp.where(first, um_ref[...], up_ref[...])
    s_ref[CONV_HALO:CONV_HALO + T_CONV, :] = uc_ref[...]
    off = CONV_HALO - (CONV_K - 1)
    span = sh_ref.shape[1]
    for p in range(1, SUBLANES):
        sh_ref[p] = s_ref[p:p + span, :]
    def chunk(c):
        acc = jnp.zeros((R_CONV, CONV_CH), F32) + cb_ref[...]
        for kk in range(CONV_K):
            p, a = (off + kk) % SUBLANES, (off + kk) // SUBLANES * SUBLANES
            if p == 0:
                win = s_ref[c + a:c + a + R_CONV, :]
            else:
                win = sh_ref[p, c + a:c + a + R_CONV, :]
            acc = acc + win * w_ref[kk:kk + 1, :]
        y = _layer_norm(acc, g_ref[...], b_ref[...])
        o_ref[c:c + R_CONV, :] = (y * _sigmoid(y)).astype(BF16)

    return [functools.partial(chunk, c) for c in range(0, T_CONV, R_CONV)]


def _attn_conv_kernel(sinks_ref, q_ref, kc_ref, kp_ref, vc_ref, vp_ref, km_ref, vm_ref, bias_ref,
                      uc_ref, up_ref, um_ref, w_ref, cb_ref, g_ref, b_ref, ao_ref, co_ref, s_ref, sh_ref):
    conv = _conv_stages(uc_ref, up_ref, um_ref, w_ref, cb_ref, g_ref, b_ref, co_ref, s_ref, sh_ref)
    attn = _attn_stages(sinks_ref, q_ref, kc_ref, kp_ref, vc_ref, vp_ref, km_ref, vm_ref, bias_ref, ao_ref)
    per = len(conv) // len(attn)
    for a, attn_block in enumerate(attn):
        for chunk in conv[a * per:(a + 1) * per]:
            chunk()
        attn_block()


def _attn_conv_call(q, k, v, k_meta, v_meta, bias, sinks, u, u_meta_halo, conv_w, conv_b, g, b, nbatch, seq):
    t = q.shape[0]
    rows = ATTN_QBLOCKS * BLOCK
    assert rows == T_CONV
    nstep = seq // rows
    per_blk = rows // BLOCK
    per_halo = rows // CONV_HALO
    cur = lambda bb, j: (bb * nstep + j, 0)
    prev_blk = lambda bb, j: (jnp.maximum((bb * nstep + j) * per_blk - 1, 0), 0)
    prev_halo = lambda bb, j: (jnp.maximum((bb * nstep + j) * per_halo - 1, 0), 0)
    fix = lambda bb, j: (0, 0)
    return pl.pallas_call(
        _attn_conv_kernel,
        grid=(nbatch, nstep),
        in_specs=[pl.BlockSpec(memory_space=pltpu.SMEM),
                  pl.BlockSpec((rows, ATTN_W), cur),
                  pl.BlockSpec((rows, KV_W), cur), pl.BlockSpec((BLOCK, KV_W), prev_blk),
                  pl.BlockSpec((rows, KV_W), cur), pl.BlockSpec((BLOCK, KV_W), prev_blk),
                  pl.BlockSpec((BLOCK, KV_W), fix), pl.BlockSpec((BLOCK, KV_W), fix),
                  pl.BlockSpec((N_Q_HEADS, BLOCK, 2 * BLOCK), lambda bb, j: (0, 0, 0)),
                  pl.BlockSpec((rows, CONV_CH), cur), pl.BlockSpec((CONV_HALO, CONV_CH), prev_halo),
                  pl.BlockSpec((CONV_HALO, CONV_CH), fix), pl.BlockSpec((CONV_K, CONV_CH), fix),
                  pl.BlockSpec((1, CONV_CH), fix), pl.BlockSpec((1, CONV_CH), fix), pl.BlockSpec((1, CONV_CH), fix)],
        out_specs=[pl.BlockSpec((rows, ATTN_W), cur), pl.BlockSpec((rows, CONV_CH), cur)],
        out_shape=[SDS((t, ATTN_W), BF16), SDS((t, CONV_CH), BF16)],
        scratch_shapes=[pltpu.VMEM((CONV_HALO + T_CONV, CONV_CH), F32),
                        pltpu.VMEM((SUBLANES, T_CONV + CONV_HALO - SUBLANES, CONV_CH), F32)],
        compiler_params=_cparams("arbitrary", "arbitrary"),
        name="attn_conv",
    )(sinks, q, k, k, v, v, k_meta, v_meta, bias, u, u, u_meta_halo, conv_w, conv_b, g, b)


def _mix_kernel(x_ref, at_ref, cv_ref, gin_ref, bin_ref, woa_ref, woc_ref, g1_ref, b1_ref,
                wr_ref, h1_ref, h1r_ref, lg_ref):
    rows = x_ref.shape[0] // MIX_CHAINS
    nt = (((1,), (1,)), ((), ()))
    for c in range(MIX_CHAINS):
        r = slice(c * rows, (c + 1) * rows)
        h = _layer_norm(x_ref[r, :], gin_ref[...], bin_ref[...])
        mix = (jnp.dot(at_ref[r, :], woa_ref[...], preferred_element_type=F32)
               + jnp.dot(cv_ref[r, :], woc_ref[...], preferred_element_type=F32))
        h1 = _layer_norm(ALPHA * h + mix, g1_ref[...], b1_ref[...])
        h1_ref[r, :] = h1
        _store_packed(h1r_ref, c * rows * ROW_CHUNKS, rows, _pack_rows(h1[:, :HALF], h1[:, HALF:]))
        lg_ref[:, r] = lax.dot_general(wr_ref[...], h1.astype(BF16), nt, preferred_element_type=F32)


def _mix_call(x2d, attn, conv, gin, bin_, woa, woc, g1, b1, wr):
    t = x2d.shape[0]
    tq = TQ_MIX
    row = lambda i: (i, 0)
    fix = lambda i: (0, 0)
    return pl.pallas_call(
        _mix_kernel,
        grid=(t // tq,),
        in_specs=[pl.BlockSpec((tq, D_MODEL), row), pl.BlockSpec((tq, ATTN_W), row), pl.BlockSpec((tq, CONV_CH), row),
                  pl.BlockSpec((1, D_MODEL), fix), pl.BlockSpec((1, D_MODEL), fix),
                  pl.BlockSpec((ATTN_W, D_MODEL), fix), pl.BlockSpec((CONV_CH, D_MODEL), fix),
                  pl.BlockSpec((1, D_MODEL), fix), pl.BlockSpec((1, D_MODEL), fix),
                  pl.BlockSpec((N_EXPERTS, D_MODEL), fix)],
        out_specs=[pl.BlockSpec((tq, D_MODEL), row), pl.BlockSpec((tq * ROW_CHUNKS, LANES), row),
                   pl.BlockSpec((N_EXPERTS, tq), lambda i: (0, i))],
        out_shape=[SDS((t, D_MODEL), F32), SDS((t * ROW_CHUNKS, LANES), U32), SDS((N_EXPERTS, t), F32)],
        compiler_params=_cparams("arbitrary"),
        name="mix_ln1",
    )(x2d, attn, conv, gin, bin_, woa, woc, g1, b1, wr)


def _first_argmax(x, rows, nrows):
    m = jnp.max(x, axis=0, keepdims=True)
    idx = jnp.min(jnp.where(x == m, rows, nrows), axis=0, keepdims=True)
    return m, idx


def _route_tile(logits, rbias, carry):
    tn = logits.shape[1]
    scores = _sigmoid(logits)
    choice = scores + rbias
    rows = lax.broadcasted_iota(I32, (N_EXPERTS, tn), 0)
    rows_g = lax.broadcasted_iota(I32, (GROUP_SIZE, tn), 0)
    rows_8 = lax.broadcasted_iota(I32, (N_GROUPS, tn), 0)

    gs = []
    for g in range(N_GROUPS):
        xg = choice[g * GROUP_SIZE:(g + 1) * GROUP_SIZE, :]
        m1, i1 = _first_argmax(xg, rows_g, GROUP_SIZE)
        m2 = jnp.max(jnp.where(rows_g == i1, -jnp.inf, xg), axis=0, keepdims=True)
        gs.append(m1 + m2)
    gsc = jnp.concatenate(gs, axis=0)
    gsel = jnp.zeros((N_GROUPS, tn), F32)
    for _ in range(TOPK_GROUPS):
        _, gi = _first_argmax(gsc, rows_8, N_GROUPS)
        hit = rows_8 == gi
        gsel = jnp.where(hit, 1.0, gsel)
        gsc = jnp.where(hit, -jnp.inf, gsc)
    emask = jnp.concatenate(
        [jnp.broadcast_to(gsel[g:g + 1, :], (GROUP_SIZE, tn)) for g in range(N_GROUPS)], axis=0)
    masked = jnp.where(emask > 0.5, choice, NEG)

    sel_all = jnp.zeros((N_EXPERTS, tn), F32)
    hits, idxs, ws = [], [], []
    for _ in range(TOP_K):
        _, ii = _first_argmax(masked, rows, N_EXPERTS)
        hit = rows == ii
        hits.append(hit)
        idxs.append(ii)
        ws.append(jnp.sum(jnp.where(hit, scores, 0.0), axis=0, keepdims=True))
        sel_all = jnp.where(hit, 1.0, sel_all)
        masked = jnp.where(hit, -jnp.inf, masked)
    wsum = ws[0]
    for w in ws[1:]:
        wsum = wsum + w
    idx = jnp.concatenate(idxs, axis=0)
    wts = jnp.concatenate([w / wsum * ROUTED_SCALE for w in ws], axis=0)

    r_i = lax.broadcasted_iota(I32, (tn, tn), 0)
    c_i = lax.broadcasted_iota(I32, (tn, tn), 1)
    upper = jnp.where(r_i < c_i, 1.0, 0.0).astype(BF16)
    sel_b = sel_all.astype(BF16)
    before = jnp.dot(sel_b, upper, preferred_element_type=F32)
    before = before + jnp.concatenate([carry] * (tn // LANES), axis=1)
    rank = jnp.concatenate(
        [jnp.sum(jnp.where(h, before, 0.0), axis=0, keepdims=True) for h in hits], axis=0).astype(I32)
    carry = carry + jnp.dot(sel_b, jnp.ones((tn, LANES), BF16), preferred_element_type=F32)
    return idx, wts, rank, carry


def _route_kernel(lg_ref, rb_ref, idx_ref, wts_ref, rank_ref, cnt_ref, carry_ref):
    @pl.when(pl.program_id(0) == 0)
    def _():
        carry_ref[...] = jnp.zeros_like(carry_ref)

    idx, wts, rank, carry = _route_tile(lg_ref[...], rb_ref[...], carry_ref[...])
    idx_ref[...] = idx
    wts_ref[...] = wts.T
    rank_ref[...] = rank
    carry_ref[...] = carry
    cnt_ref[...] = carry.astype(I32)


def _route_call(lg, rbias):
    t = lg.shape[1]
    tn = TN_ROUTE
    col = lambda i: (0, i)
    return pl.pallas_call(
        _route_kernel,
        grid=(t // tn,),
        in_specs=[pl.BlockSpec((N_EXPERTS, tn), col), pl.BlockSpec((N_EXPERTS, 1), lambda i: (0, 0))],
        out_specs=[pl.BlockSpec((TOP_K, tn), col), pl.BlockSpec((tn, TOP_K), lambda i: (i, 0)),
                   pl.BlockSpec((TOP_K, tn), col), pl.BlockSpec((N_EXPERTS, LANES), lambda i: (0, 0))],
        out_shape=[SDS((TOP_K, t), I32), SDS((t, TOP_K), F32), SDS((TOP_K, t), I32), SDS((N_EXPERTS, LANES), I32)],
        scratch_shapes=[pltpu.VMEM((N_EXPERTS, LANES), F32)],
        compiler_params=_cparams("arbitrary"),
        name="route",
    )(lg, rbias)


SC_CORES = 2
SC_SUBCORES = 16
SC_CHUNK = 128
SC_LANES = 16
SC_BUFS = 2


def _sc_worker_chunks(t):
    per_worker = t // (SC_CORES * SC_SUBCORES)
    assert per_worker % SC_CHUNK == 0
    return per_worker


def _sc_dispatch_call(idx, rank, offs, h1rows3, n_rows):
    t = idx.shape[1]
    per_worker = _sc_worker_chunks(t)
    mesh = plsc.VectorSubcoreMesh(core_axis_name="c", subcore_axis_name="s")

    @functools.partial(
        pl.kernel, mesh=mesh, out_type=[SDS((n_rows, ROW_CHUNKS, LANES), U32), SDS((TOP_K, t), I32)],
        scratch_types=[pltpu.VMEM((TOP_K, SC_CHUNK), I32), pltpu.VMEM((TOP_K, SC_CHUNK), I32),
                       pltpu.VMEM((N_EXPERTS,), I32), pltpu.VMEM((SC_CHUNK, ROW_CHUNKS, LANES), U32),
                       pltpu.SemaphoreType.DMA],
        compiler_params=pltpu.CompilerParams(needs_layout_passes=False),
        name="sc_dispatch")
    def body(h_hbm, idx_hbm, rank_hbm, offs_hbm, xs_hbm, dest_hbm, idx_v, rank_v, offs_v, rows_v, sem):
        wid = lax.axis_index("s") * SC_CORES + lax.axis_index("c")
        pltpu.sync_copy(offs_hbm, offs_v)

        @pl.loop(0, per_worker // SC_CHUNK)
        def _(i):
            t0 = wid * per_worker + i * SC_CHUNK
            pltpu.sync_copy(idx_hbm.at[:, pl.ds(t0, SC_CHUNK)], idx_v)
            pltpu.sync_copy(rank_hbm.at[:, pl.ds(t0, SC_CHUNK)], rank_v)
            pltpu.sync_copy(h_hbm.at[pl.ds(t0, SC_CHUNK)], rows_v)
            for kk in range(TOP_K):
                @pl.loop(0, SC_CHUNK // SC_LANES)
                def _(c):
                    lanes = pl.ds(c * SC_LANES, SC_LANES)
                    idx_v[kk, lanes] = plsc.load_gather(offs_v, [idx_v[kk, lanes]]) + rank_v[kk, lanes]
            pltpu.sync_copy(idx_v, dest_hbm.at[:, pl.ds(t0, SC_CHUNK)])
            copies = [pltpu.async_copy(rows_v, xs_hbm.at[idx_v.at[kk]], sem) for kk in range(TOP_K)]
            for c in copies:
                c.wait()

    return body(h1rows3, idx, rank, offs)


def _sc_gather_call(dest, ys3):
    t = dest.shape[1]
    per_worker = _sc_worker_chunks(t)
    mesh = plsc.VectorSubcoreMesh(core_axis_name="c", subcore_axis_name="s")

    half = SC_CHUNK // 2
    items = [(kk, h) for kk in range(TOP_K) for h in range(2)]

    @functools.partial(
        pl.kernel, mesh=mesh, out_type=SDS((TOP_K * t, ROW_CHUNKS, LANES), U32),
        scratch_types=[pltpu.VMEM((TOP_K, SC_CHUNK), I32), pltpu.VMEM((SC_BUFS, half, ROW_CHUNKS, LANES), U32),
                       pltpu.SemaphoreType.DMA((SC_BUFS,)), pltpu.SemaphoreType.DMA((SC_BUFS,))],
        name="sc_gather")
    def body(ys_hbm, dest_hbm, out_hbm, idx_v, rows_v, gsem, wsem):
        wid = lax.axis_index("s") * SC_CORES + lax.axis_index("c")

        @pl.loop(0, per_worker // SC_CHUNK)
        def _(i):
            t0 = wid * per_worker + i * SC_CHUNK
            pltpu.sync_copy(dest_hbm.at[:, pl.ds(t0, SC_CHUNK)], idx_v)

            def gather(j):
                kk, h = items[j]
                b = j % SC_BUFS
                return pltpu.make_async_copy(ys_hbm.at[idx_v.at[kk, pl.ds(h * half, half)]], rows_v.at[b], gsem.at[b])

            def put(j):
                kk, h = items[j]
                b = j % SC_BUFS
                return pltpu.make_async_copy(rows_v.at[b], out_hbm.at[pl.ds(kk * t + t0 + h * half, half)], wsem.at[b])

            ahead = SC_BUFS - 1
            n = len(items)
            for j in range(ahead):
                gather(j).start()
            for j in range(n):
                if j + ahead < n:
                    if j >= 1:
                        put(j - 1).wait()
                    gather(j + ahead).start()
                gather(j).wait()
                put(j).start()
            for j in range(max(n - ahead - 1, 0), n):
                put(j).wait()

    return body(ys3, dest)


def _expert_kernel(ts_ref, te_ref, tr_ref, nv_ref, wg_hbm, wu_hbm, wd_hbm, xs_hbm, ys_hbm,
                   xbuf, ybuf, wg_f, wu_f, wd_f, wg_b, wu_b, wd_b, xsem, ysem, wsem):
    e = pl.program_id(0)
    rows = xbuf.shape[1]
    tm = rows // ROW_CHUNKS
    g0, g1, nv = ts_ref[e], te_ref[e], nv_ref[0]

    def x_copy(g):
        s = g % X_SLOTS
        return pltpu.make_async_copy(xs_hbm.at[pl.ds(pl.multiple_of(g * rows, rows), rows), :], xbuf.at[s], xsem.at[s])

    def y_copy(g):
        s = g % Y_SLOTS
        return pltpu.make_async_copy(ybuf.at[s], ys_hbm.at[pl.ds(pl.multiple_of(g * rows, rows), rows), :], ysem.at[s])

    def w_copies(ex):
        s = ex % W_SLOTS
        return (pltpu.make_async_copy(wg_hbm.at[ex], wg_f.at[s], wsem.at[s]),
                pltpu.make_async_copy(wu_hbm.at[ex], wu_f.at[s], wsem.at[s]),
                pltpu.make_async_copy(wd_hbm.at[ex], wd_f.at[s], wsem.at[s]))

    n_exp = pl.num_programs(0)

    @pl.when(e == 0)
    def _():
        for ex in range(W_AHEAD):
            for c in w_copies(ex):
                c.start()

    @pl.when(e + W_AHEAD < n_exp)
    def _():
        for c in w_copies(e + W_AHEAD):
            c.start()

    for c in w_copies(e):
        c.wait()

    def compute_tile(g):
        x = _load_packed_bf16(xbuf, 0, tm, lead=g % X_SLOTS)
        gate = jnp.dot(x, wg_b[...], preferred_element_type=F32)
        up = jnp.dot(x, wu_b[...], preferred_element_type=F32)
        live = lax.broadcasted_iota(I32, (tm, EXPERT_FF), 0) < tr_ref[g]
        hid = jnp.where(live, gate * _sigmoid(gate) * up, 0.0).astype(BF16)
        y = jnp.dot(hid, wd_b[...], preferred_element_type=F32)
        return _pack_rows(y[:, :HALF], y[:, HALF:])

    def run_tiles(g, n):
        for r in range(n):
            x_copy(g + r).wait()

            @pl.when(g + r + X_AHEAD < nv)
            def _():
                x_copy(g + r + X_AHEAD).start(priority=1)

            @pl.when(g + r >= Y_SLOTS)
            def _():
                y_copy(g + r - Y_SLOTS).wait()

        packed = [compute_tile(g + r) for r in range(n)]
        for r in range(n):
            _store_packed(ybuf, 0, tm, packed[r], lead=(g + r) % Y_SLOTS)
        for r in range(n):
            y_copy(g + r).start(priority=1)

    @pl.when(e == 0)
    def _():
        for g in range(X_AHEAD):
            @pl.when(g < nv)
            def _():
                x_copy(g).start(priority=1)

    @pl.when(g1 > g0)
    def _():
        ws = e % W_SLOTS
        wg_b[...] = wg_f[ws].astype(BF16)
        wu_b[...] = wu_f[ws].astype(BF16)
        wd_b[...] = wd_f[ws].astype(BF16)
        n_tiles = g1 - g0

        def pair(p, c):
            run_tiles(g0 + 2 * p, 2)
            return c

        lax.fori_loop(0, n_tiles // 2, pair, 0)

        @pl.when(n_tiles % 2 == 1)
        def _():
            run_tiles(g1 - 1, 1)

    @pl.when(e == pl.num_programs(0) - 1)
    def _():
        for back in range(1, Y_SLOTS + 1):
            @pl.when(nv >= back)
            def _():
                y_copy(nv - back).wait()


def _expert_call(tile_start, tile_end, tile_rows, n_valid, xs, w_gate, w_up, w_down, n_rows):
    tm = TM_EXP
    hbm = pl.BlockSpec(memory_space=pl.ANY)
    return pl.pallas_call(
        _expert_kernel,
        grid_spec=pltpu.PrefetchScalarGridSpec(
            num_scalar_prefetch=4,
            grid=(N_EXPERTS,),
            in_specs=[hbm, hbm, hbm, hbm],
            out_specs=hbm,
            scratch_shapes=[pltpu.VMEM((X_SLOTS, tm * ROW_CHUNKS, LANES), U32),
                            pltpu.VMEM((Y_SLOTS, tm * ROW_CHUNKS, LANES), U32),
                            pltpu.VMEM((W_SLOTS, D_MODEL, EXPERT_FF), F32), pltpu.VMEM((W_SLOTS, D_MODEL, EXPERT_FF), F32),
                            pltpu.VMEM((W_SLOTS, EXPERT_FF, D_MODEL), F32),
                            pltpu.VMEM((D_MODEL, EXPERT_FF), BF16), pltpu.VMEM((D_MODEL, EXPERT_FF), BF16),
                            pltpu.VMEM((EXPERT_FF, D_MODEL), BF16),
                            pltpu.SemaphoreType.DMA((X_SLOTS,)), pltpu.SemaphoreType.DMA((Y_SLOTS,)),
                            pltpu.SemaphoreType.DMA((W_SLOTS,))],
        ),
        out_shape=SDS((n_rows * ROW_CHUNKS, LANES), U32),
        compiler_params=_cparams("arbitrary"),
        name="experts",
    )(tile_start, tile_end, tile_rows, n_valid, w_gate, w_up, w_down, xs)


COMB_SUB = 32


def _combine_kernel(wts_ref, h1_ref, g_ref, wsg_ref, wsu_ref, wsd_ref, g2_ref, b2_ref, o_ref, routed_ref):
    tn = h1_ref.shape[0]
    for s0 in range(0, tn, COMB_SUB):
        acc = [jnp.zeros((COMB_SUB, LANES), F32) for _ in range(2 * ROW_CHUNKS)]
        for kk in range(TOP_K):
            wk = jnp.broadcast_to(wts_ref[s0:s0 + COMB_SUB, kk:kk + 1], (COMB_SUB, LANES))
            for cc in range(ROW_CHUNKS):
                lo, hi = _unpack_rows(g_ref[kk, pl.ds(s0 * ROW_CHUNKS + cc, COMB_SUB, stride=ROW_CHUNKS), :])
                acc[cc] = acc[cc] + wk * lo
                acc[ROW_CHUNKS + cc] = acc[ROW_CHUNKS + cc] + wk * hi
        routed_ref[s0:s0 + COMB_SUB, :] = jnp.concatenate(acc, axis=1)

    h1 = h1_ref[...]
    hb = h1.astype(BF16)
    sg = jnp.dot(hb, wsg_ref[...], preferred_element_type=F32)
    su = jnp.dot(hb, wsu_ref[...], preferred_element_type=F32)
    ff = jnp.dot((sg * _sigmoid(sg) * su).astype(BF16), wsd_ref[...], preferred_element_type=F32)
    o_ref[...] = _layer_norm(ALPHA * h1 + ff + routed_ref[...], g2_ref[...], b2_ref[...])


def _combine_call(wts_t, h1, gathered, wsg, wsu, wsd, g2, b2):
    t = h1.shape[0]
    tn = TN_COMB
    row = lambda i: (i, 0)
    fix = lambda i: (0, 0)
    return pl.pallas_call(
        _combine_kernel,
        grid=(t // tn,),
        in_specs=[pl.BlockSpec((tn, TOP_K), row),
                  pl.BlockSpec((tn, D_MODEL), row),
                  pl.BlockSpec((TOP_K, tn * ROW_CHUNKS, LANES), lambda i: (0, i, 0)),
                  pl.BlockSpec((D_MODEL, SHARED_FF), fix), pl.BlockSpec((D_MODEL, SHARED_FF), fix),
                  pl.BlockSpec((SHARED_FF, D_MODEL), fix),
                  pl.BlockSpec((1, D_MODEL), fix), pl.BlockSpec((1, D_MODEL), fix)],
        out_specs=pl.BlockSpec((tn, D_MODEL), row),
        out_shape=SDS((t, D_MODEL), F32),
        scratch_shapes=[pltpu.VMEM((tn, D_MODEL), F32)],
        compiler_params=_cparams("arbitrary"),
        name="combine_ln2",
    )(wts_t, h1, gathered, wsg, wsu, wsd, g2, b2)


def kernel(x, meta_tokens, ln_in_g, ln_in_b, rel_bias, w_in, conv_w, conv_b, conv_ln_g, conv_ln_b, sinks,
           w_out, ln1_g, ln1_b, w_router, router_bias, w_gate, w_up, w_down, ws_gate, ws_up, ws_down,
           ln2_g, ln2_b):
    nbatch, seq, d = x.shape
    t = nbatch * seq
    assert d == D_MODEL and w_in.shape[0] == DEPTH
    assert seq % (ATTN_QBLOCKS * BLOCK) == 0 and seq % T_CONV == 0
    assert all(t % tile == 0 for tile in (TQ_PROJ, TQ_MIX, TN_ROUTE, TN_COMB))
    x2d = x.reshape(t, D_MODEL)
    vec = lambda a: a.reshape(1, -1).astype(F32)
    gin, bin_ = vec(ln_in_g), vec(ln_in_b)
    w_in_b = w_in[0].astype(BF16)

    q, k, v, u = _proj_call(x2d, gin, bin_, w_in_b, TQ_PROJ)
    meta_blk = jnp.concatenate([jnp.zeros((PAD_FRONT, D_MODEL), F32), meta_tokens.astype(F32)], axis=0)
    _, k_meta, v_meta, u_meta = _proj_call(meta_blk, gin, bin_, w_in_b, BLOCK)

    u_halo = jnp.concatenate([jnp.zeros((CONV_HALO - N_META, CONV_CH), F32), u_meta[PAD_FRONT:]], axis=0)
    attn, conv = _attn_conv_call(q, k, v, k_meta, v_meta, _rel_bias_table(rel_bias), sinks[0].astype(F32),
                                 u, u_halo, conv_w[0].astype(F32), vec(conv_b[0]), vec(conv_ln_g[0]),
                                 vec(conv_ln_b[0]), nbatch, seq)

    w_out_b = w_out[0].astype(BF16)
    h1, h1rows, logits = _mix_call(x2d, attn, conv, gin, bin_, w_out_b[:ATTN_W], w_out_b[ATTN_W:],
                                   vec(ln1_g[0]), vec(ln1_b[0]), w_router[0].astype(BF16).T)

    idx, wts_t, rank, cnt = _route_call(logits, router_bias[0].astype(F32).reshape(N_EXPERTS, 1))

    tm = TM_EXP
    n_tiles = (t * TOP_K) // tm + N_EXPERTS
    counts = cnt[:, 0]
    tiles_e = (counts + tm - 1) // tm
    tile_end = jnp.cumsum(tiles_e).astype(I32)
    tile_start = (tile_end - tiles_e).astype(I32)
    offs = tile_start * tm
    tile_id = jnp.arange(n_tiles, dtype=I32)
    lo = jnp.maximum(tile_id[:, None] * tm, offs[None, :])
    hi = jnp.minimum((tile_id[:, None] + 1) * tm, (offs + counts)[None, :])
    tile_rows = jnp.sum(jnp.clip(hi - lo, 0, tm), axis=1).astype(I32)
    n_valid = tile_end[-1:]

    xs, dest = _sc_dispatch_call(idx, rank, offs, h1rows.reshape(t, ROW_CHUNKS, LANES), n_tiles * tm)
    xs = xs.reshape(n_tiles * tm * ROW_CHUNKS, LANES)
    ys = _expert_call(tile_start, tile_end, tile_rows, n_valid, xs, w_gate[0], w_up[0], w_down[0], n_tiles * tm)
    gathered = _sc_gather_call(dest, ys.reshape(n_tiles * tm, ROW_CHUNKS, LANES))
    gathered = gathered.reshape(TOP_K, t * ROW_CHUNKS, LANES)
    out = _combine_call(wts_t, h1, gathered, ws_gate[0].astype(BF16), ws_up[0].astype(BF16),
                        ws_down[0].astype(BF16), vec(ln2_g[0]), vec(ln2_b[0]))
    return out.reshape(nbatch, seq, D_MODEL)
```

```python
import functools
import math

import numpy as np
import jax
import jax.numpy as jnp
from jax import lax
from jax.experimental import pallas as pl
from jax.experimental.pallas import tpu as pltpu
from jax.experimental.pallas import tpu_sc as plsc

F32 = jnp.float32
BF16 = jnp.bfloat16
I32 = jnp.int32
U32 = jnp.uint32
SDS = jax.ShapeDtypeStruct

D_MODEL = 1024
HALF = D_MODEL // 2
LANES = 128
SUBLANES = 8
ROW_CHUNKS = HALF // LANES
N_META = 16
HEAD_DIM = 64
N_Q_HEADS = 8
N_KV_HEADS = 2
GQA_GROUP = N_Q_HEADS // N_KV_HEADS
ATTN_W = N_Q_HEADS * HEAD_DIM
KV_W = N_KV_HEADS * HEAD_DIM
WINDOW = 128
BLOCK = 128
CONV_CH = D_MODEL - ATTN_W
CONV_K = 31
IN_W = ATTN_W + 2 * KV_W + 2 * CONV_CH
NUM_BUCKETS = 32
MAX_EXACT = NUM_BUCKETS // 2
REL_MAX_DIST = 128
N_EXPERTS = 256
TOP_K = 8
N_GROUPS = 8
GROUP_SIZE = N_EXPERTS // N_GROUPS
TOPK_GROUPS = 4
EXPERT_FF = 256
SHARED_FF = 256
ROUTED_SCALE = 2.5
DEPTH = 1
ALPHA = (2.0 * DEPTH) ** 0.25
LN_EPS = 1e-5
NEG = -1e30
PAD_FRONT = (-N_META) % BLOCK

VMEM_LIMIT = 48 * 1024 * 1024

TQ_PROJ = 1024
PROJ_CHAINS = 4
ATTN_QBLOCKS = 2
T_CONV = 256
CONV_HALO = 32
R_CONV = 64
TQ_MIX = 1024
MIX_CHAINS = 4
TN_ROUTE = 512
ROUTE_CHAINS = 2
TM_EXP = 256
X_SLOTS = 8
X_AHEAD = 4
Y_SLOTS = 4
W_SLOTS = 3
W_AHEAD = 2
TN_COMB = 512


def _cparams(*sem):
    return pltpu.CompilerParams(dimension_semantics=sem, vmem_limit_bytes=VMEM_LIMIT)


def _layer_norm(x, g, b):
    mu = jnp.mean(x, axis=-1, keepdims=True)
    xc = x - mu
    var = jnp.mean(xc * xc, axis=-1, keepdims=True)
    return xc * lax.rsqrt(var + LN_EPS) * g + b


def _sigmoid(x):
    return 1.0 / (1.0 + jnp.exp(-x))


def _pack_rows(lo_half, hi_half):
    lo = lax.bitcast_convert_type(lo_half.astype(BF16).astype(F32), U32)
    hi = lax.bitcast_convert_type(hi_half.astype(BF16).astype(F32), U32)
    return lax.shift_right_logical(lo, jnp.uint32(16)) | hi


def _unpack_rows(p):
    lo = lax.bitcast_convert_type(lax.shift_left(p, jnp.uint32(16)), F32)
    hi = lax.bitcast_convert_type(p & jnp.uint32(0xFFFF0000), F32)
    return lo, hi


def _chunk_index(start, j, n, lead):
    rows = pl.ds(start + j, n, stride=ROW_CHUNKS)
    return (rows, slice(None)) if lead is None else (lead, rows, slice(None))


def _store_packed(ref, start, n, packed, lead=None):
    for j in range(ROW_CHUNKS):
        ref[_chunk_index(start, j, n, lead)] = packed[:, j * LANES:(j + 1) * LANES]


def _load_packed_bf16(ref, start, n, lead=None):
    halves = [_unpack_rows(ref[_chunk_index(start, j, n, lead)]) for j in range(ROW_CHUNKS)]
    return jnp.concatenate([h[0] for h in halves] + [h[1] for h in halves], axis=1).astype(BF16)


def _proj_kernel(chains, x_ref, g_ref, b_ref, w_ref, q_ref, k_ref, v_ref, u_ref):
    rows = x_ref.shape[0] // chains
    for c in range(chains):
        r = slice(c * rows, (c + 1) * rows)
        h = _layer_norm(x_ref[r, :], g_ref[...], b_ref[...])
        p = jnp.dot(h.astype(BF16), w_ref[...], preferred_element_type=F32)
        q_ref[r, :] = (p[:, :ATTN_W] * (HEAD_DIM ** -0.5)).astype(BF16)
        k_ref[r, :] = p[:, ATTN_W:ATTN_W + KV_W].astype(BF16)
        v_ref[r, :] = p[:, ATTN_W + KV_W:ATTN_W + 2 * KV_W].astype(BF16)
        a = p[:, ATTN_W + 2 * KV_W:ATTN_W + 2 * KV_W + CONV_CH]
        gate = p[:, ATTN_W + 2 * KV_W + CONV_CH:]
        u_ref[r, :] = a * _sigmoid(gate)


def _proj_call(x2d, gin, bin_, w_in_b, tq):
    t = x2d.shape[0]
    row = lambda i: (i, 0)
    fix = lambda i: (0, 0)
    chains = PROJ_CHAINS if tq % (PROJ_CHAINS * BLOCK) == 0 else 1
    return pl.pallas_call(
        functools.partial(_proj_kernel, chains),
        grid=(t // tq,),
        in_specs=[pl.BlockSpec((tq, D_MODEL), row), pl.BlockSpec((1, D_MODEL), fix),
                  pl.BlockSpec((1, D_MODEL), fix), pl.BlockSpec((D_MODEL, IN_W), fix)],
        out_specs=[pl.BlockSpec((tq, ATTN_W), row), pl.BlockSpec((tq, KV_W), row),
                   pl.BlockSpec((tq, KV_W), row), pl.BlockSpec((tq, CONV_CH), row)],
        out_shape=[SDS((t, ATTN_W), BF16), SDS((t, KV_W), BF16), SDS((t, KV_W), BF16), SDS((t, CONV_CH), F32)],
        compiler_params=_cparams("arbitrary"),
        name="ln_in_proj",
    )(x2d, gin, bin_, w_in_b)


def _attn_stages(sinks_ref, q_ref, kc_ref, kp_ref, vc_ref, vp_ref, km_ref, vm_ref, bias_ref, o_ref):
    first = pl.program_id(1) == 0
    kp = jnp.where(first, km_ref[...], kp_ref[...])
    vp = jnp.where(first, vm_ref[...], vp_ref[...])
    k = jnp.concatenate([kp, kc_ref[...]], axis=0)
    v = jnp.concatenate([vp, vc_ref[...]], axis=0)
    col = lax.broadcasted_iota(I32, (BLOCK, 2 * BLOCK), 1)
    pad_bias = jnp.where(jnp.logical_and(first, col < PAD_FRONT), NEG, 0.0).astype(F32)
    def block(a):
        q = q_ref[a * BLOCK:(a + 1) * BLOCK, :]
        kw = k[a * BLOCK:(a + 2) * BLOCK, :]
        vw = v[a * BLOCK:(a + 2) * BLOCK, :]
        outs = []
        for h in range(N_Q_HEADS):
            g = h // GQA_GROUP
            qh = q[:, h * HEAD_DIM:(h + 1) * HEAD_DIM]
            kg = kw[:, g * HEAD_DIM:(g + 1) * HEAD_DIM]
            vg = vw[:, g * HEAD_DIM:(g + 1) * HEAD_DIM]
            s = lax.dot_general(qh, kg, (((1,), (1,)), ((), ())), preferred_element_type=F32)
            s = s + bias_ref[h]
            if a == 0:
                s = s + pad_bias
            sink = sinks_ref[h]
            m = jnp.maximum(jnp.max(s, axis=-1, keepdims=True), sink)
            p = jnp.exp(s - m)
            den = jnp.sum(p, axis=-1, keepdims=True) + jnp.exp(sink - m)
            o = jnp.dot(p.astype(BF16), vg, preferred_element_type=F32)
            outs.append(o / den)
        o_ref[a * BLOCK:(a + 1) * BLOCK, :] = jnp.concatenate(outs, axis=1).astype(BF16)

    return [functools.partial(block, a) for a in range(ATTN_QBLOCKS)]


def _rel_bias_table(rel_bias):
    qi = np.arange(BLOCK, dtype=np.int32)[:, None]
    kj = np.arange(2 * BLOCK, dtype=np.int32)[None, :]
    dist = BLOCK + qi - kj
    dc = np.clip(dist, 0, WINDOW - 1)
    nf = np.maximum(dc, 1).astype(np.float32)
    large = MAX_EXACT + (np.log(nf / np.float32(MAX_EXACT)) / np.float32(math.log(REL_MAX_DIST / MAX_EXACT))
                         * np.float32(NUM_BUCKETS - MAX_EXACT)).astype(np.int32)
    large = np.minimum(large, NUM_BUCKETS - 1)
    bucket = np.where(dc < MAX_EXACT, dc, large)
    in_window = (dist >= 0) & (dist < WINDOW)
    onehot = (bucket.reshape(-1, 1) == np.arange(NUM_BUCKETS)[None, :]).astype(np.float32)
    bias = jnp.dot(jnp.asarray(onehot), rel_bias.astype(F32), precision=lax.Precision.HIGHEST)
    bias = jnp.transpose(bias.reshape(BLOCK, 2 * BLOCK, N_Q_HEADS), (2, 0, 1))
    return jnp.where(in_window[None], bias, NEG)


def _conv_stages(uc_ref, up_ref, um_ref, w_ref, cb_ref, g_ref, b_ref, o_ref, s_ref, sh_ref):
    first = pl.program_id(1) == 0
    s_ref[0:CONV_HALO, :] = jnp.where(first, um_ref[...], up_ref[...])
    s_ref[CONV_HALO:CONV_HALO + T_CONV, :] = uc_ref[...]
    off = CONV_HALO - (CONV_K - 1)
    span = sh_ref.shape[1]
    for p in range(1, SUBLANES):
        sh_ref[p] = s_ref[p:p + span, :]
    def chunk(c):
        acc = jnp.zeros((R_CONV, CONV_CH), F32) + cb_ref[...]
        for kk in range(CONV_K):
            p, a = (off + kk) % SUBLANES, (off + kk) // SUBLANES * SUBLANES
            if p == 0:
                win = s_ref[c + a:c + a + R_CONV, :]
            else:
                win = sh_ref[p, c + a:c + a + R_CONV, :]
            acc = acc + win * w_ref[kk:kk + 1, :]
        y = _layer_norm(acc, g_ref[...], b_ref[...])
        o_ref[c:c + R_CONV, :] = (y * _sigmoid(y)).astype(BF16)

    return [functools.partial(chunk, c) for c in range(0, T_CONV, R_CONV)]


def _attn_conv_kernel(sinks_ref, q_ref, kc_ref, kp_ref, vc_ref, vp_ref, km_ref, vm_ref, bias_ref,
                      uc_ref, up_ref, um_ref, w_ref, cb_ref, g_ref, b_ref, ao_ref, co_ref, s_ref, sh_ref):
    conv = _conv_stages(uc_ref, up_ref, um_ref, w_ref, cb_ref, g_ref, b_ref, co_ref, s_ref, sh_ref)
    attn = _attn_stages(sinks_ref, q_ref, kc_ref, kp_ref, vc_ref, vp_ref, km_ref, vm_ref, bias_ref, ao_ref)
    per = len(conv) // len(attn)
    for a, attn_block in enumerate(attn):
        for chunk in conv[a * per:(a + 1) * per]:
            chunk()
        attn_block()


def _attn_conv_call(q, k, v, k_meta, v_meta, bias, sinks, u, u_meta_halo, conv_w, conv_b, g, b, nbatch, seq):
    t = q.shape[0]
    rows = ATTN_QBLOCKS * BLOCK
    assert rows == T_CONV
    nstep = seq // rows
    per_blk = rows // BLOCK
    per_halo = rows // CONV_HALO
    cur = lambda bb, j: (bb * nstep + j, 0)
    prev_blk = lambda bb, j: (jnp.maximum((bb * nstep + j) * per_blk - 1, 0), 0)
    prev_halo = lambda bb, j: (jnp.maximum((bb * nstep + j) * per_halo - 1, 0), 0)
    fix = lambda bb, j: (0, 0)
    return pl.pallas_call(
        _attn_conv_kernel,
        grid=(nbatch, nstep),
        in_specs=[pl.BlockSpec(memory_space=pltpu.SMEM),
                  pl.BlockSpec((rows, ATTN_W), cur),
                  pl.BlockSpec((rows, KV_W), cur), pl.BlockSpec((BLOCK, KV_W), prev_blk),
                  pl.BlockSpec((rows, KV_W), cur), pl.BlockSpec((BLOCK, KV_W), prev_blk),
                  pl.BlockSpec((BLOCK, KV_W), fix), pl.BlockSpec((BLOCK, KV_W), fix),
                  pl.BlockSpec((N_Q_HEADS, BLOCK, 2 * BLOCK), lambda bb, j: (0, 0, 0)),
                  pl.BlockSpec((rows, CONV_CH), cur), pl.BlockSpec((CONV_HALO, CONV_CH), prev_halo),
                  pl.BlockSpec((CONV_HALO, CONV_CH), fix), pl.BlockSpec((CONV_K, CONV_CH), fix),
                  pl.BlockSpec((1, CONV_CH), fix), pl.BlockSpec((1, CONV_CH), fix), pl.BlockSpec((1, CONV_CH), fix)],
        out_specs=[pl.BlockSpec((rows, ATTN_W), cur), pl.BlockSpec((rows, CONV_CH), cur)],
        out_shape=[SDS((t, ATTN_W), BF16), SDS((t, CONV_CH), BF16)],
        scratch_shapes=[pltpu.VMEM((CONV_HALO + T_CONV, CONV_CH), F32),
                        pltpu.VMEM((SUBLANES, T_CONV + CONV_HALO - SUBLANES, CONV_CH), F32)],
        compiler_params=_cparams("arbitrary", "arbitrary"),
        name="attn_conv",
    )(sinks, q, k, k, v, v, k_meta, v_meta, bias, u, u, u_meta_halo, conv_w, conv_b, g, b)


def _mix_kernel(x_ref, at_ref, cv_ref, gin_ref, bin_ref, woa_ref, woc_ref, g1_ref, b1_ref,
                wr_ref, h1_ref, h1r_ref, lg_ref):
    rows = x_ref.shape[0] // MIX_CHAINS
    nt = (((1,), (1,)), ((), ()))
    for c in range(MIX_CHAINS):
        r = slice(c * rows, (c + 1) * rows)
        h = _layer_norm(x_ref[r, :], gin_ref[...], bin_ref[...])
        mix = (jnp.dot(at_ref[r, :], woa_ref[...], preferred_element_type=F32)
               + jnp.dot(cv_ref[r, :], woc_ref[...], preferred_element_type=F32))
        h1 = _layer_norm(ALPHA * h + mix, g1_ref[...], b1_ref[...])
        h1_ref[r, :] = h1
        _store_packed(h1r_ref, c * rows * ROW_CHUNKS, rows, _pack_rows(h1[:, :HALF], h1[:, HALF:]))
        lg_ref[:, r] = lax.dot_general(wr_ref[...], h1.astype(BF16), nt, preferred_element_type=F32)


def _mix_call(x2d, attn, conv, gin, bin_, woa, woc, g1, b1, wr):
    t = x2d.shape[0]
    tq = TQ_MIX
    row = lambda i: (i, 0)
    fix = lambda i: (0, 0)
    return pl.pallas_call(
        _mix_kernel,
        grid=(t // tq,),
        in_specs=[pl.BlockSpec((tq, D_MODEL), row), pl.BlockSpec((tq, ATTN_W), row), pl.BlockSpec((tq, CONV_CH), row),
                  pl.BlockSpec((1, D_MODEL), fix), pl.BlockSpec((1, D_MODEL), fix),
                  pl.BlockSpec((ATTN_W, D_MODEL), fix), pl.BlockSpec((CONV_CH, D_MODEL), fix),
                  pl.BlockSpec((1, D_MODEL), fix), pl.BlockSpec((1, D_MODEL), fix),
                  pl.BlockSpec((N_EXPERTS, D_MODEL), fix)],
        out_specs=[pl.BlockSpec((tq, D_MODEL), row), pl.BlockSpec((tq * ROW_CHUNKS, LANES), row),
                   pl.BlockSpec((N_EXPERTS, tq), lambda i: (0, i))],
        out_shape=[SDS((t, D_MODEL), F32), SDS((t * ROW_CHUNKS, LANES), U32), SDS((N_EXPERTS, t), F32)],
        compiler_params=_cparams("arbitrary"),
        name="mix_ln1",
    )(x2d, attn, conv, gin, bin_, woa, woc, g1, b1, wr)


def _row_numbers(nrows, n):
    return lax.broadcasted_iota(I32, (nrows, n), 0).astype(F32)


def _first_argmax(x, rows, nrows):
    m = jnp.max(x, axis=0, keepdims=True)
    idx = jnp.min(jnp.where(x == m, rows, float(nrows)), axis=0, keepdims=True)
    return m, idx


def _route_tile(logits, rbias, carry):
    tn = logits.shape[1]
    scores = _sigmoid(logits)
    choice = scores + rbias
    rows = _row_numbers(N_EXPERTS, tn)
    rows_g = _row_numbers(GROUP_SIZE, tn)
    rows_8 = _row_numbers(N_GROUPS, tn)

    gs = []
    for g in range(N_GROUPS):
        xg = choice[g * GROUP_SIZE:(g + 1) * GROUP_SIZE, :]
        m1, i1 = _first_argmax(xg, rows_g, GROUP_SIZE)
        m2 = jnp.max(jnp.where(rows_g == i1, -jnp.inf, xg), axis=0, keepdims=True)
        gs.append(m1 + m2)
    gsc = jnp.concatenate(gs, axis=0)
    gsel = jnp.zeros((N_GROUPS, tn), F32)
    for _ in range(TOPK_GROUPS):
        _, gi = _first_argmax(gsc, rows_8, N_GROUPS)
        hit = rows_8 == gi
        gsel = jnp.where(hit, 1.0, gsel)
        gsc = jnp.where(hit, -jnp.inf, gsc)
    emask = jnp.concatenate(
        [jnp.broadcast_to(gsel[g:g + 1, :], (GROUP_SIZE, tn)) for g in range(N_GROUPS)], axis=0)
    masked = jnp.where(emask > 0.5, jnp.maximum(choice, float(jnp.finfo(F32).min)), NEG)

    hits, idxs, ws = [], [], []
    for _ in range(TOP_K):
        _, ii = _first_argmax(masked, rows, N_EXPERTS)
        hit = rows == ii
        hits.append(hit)
        idxs.append(ii)
        ws.append(jnp.sum(jnp.where(hit, scores, 0.0), axis=0, keepdims=True))
        masked = jnp.where(hit, -jnp.inf, masked)
    wsum = ws[0]
    for w in ws[1:]:
        wsum = wsum + w
    idx = jnp.concatenate(idxs, axis=0).astype(I32)
    wts = jnp.concatenate([w / wsum * ROUTED_SCALE for w in ws], axis=0)

    r_i = lax.broadcasted_iota(I32, (tn, tn), 0)
    c_i = lax.broadcasted_iota(I32, (tn, tn), 1)
    upper = jnp.where(r_i < c_i, 1.0, 0.0).astype(BF16)
    sel_b = jnp.where(masked == -jnp.inf, 1.0, 0.0).astype(BF16)
    before = jnp.dot(sel_b, upper, preferred_element_type=F32)
    before = before + jnp.concatenate([carry] * (tn // LANES), axis=1)
    rank = jnp.concatenate(
        [jnp.sum(jnp.where(h, before, 0.0), axis=0, keepdims=True) for h in hits], axis=0).astype(I32)
    carry = carry + jnp.dot(sel_b, jnp.ones((tn, LANES), BF16), preferred_element_type=F32)
    return idx, wts, rank, carry


def _route_kernel(lg_ref, rb_ref, idx_ref, wts_ref, rank_ref, cnt_ref, carry_ref):
    @pl.when(pl.program_id(0) == 0)
    def _():
        carry_ref[...] = jnp.zeros_like(carry_ref)

    carry = carry_ref[...]
    tn = lg_ref.shape[1] // ROUTE_CHAINS
    for c in range(ROUTE_CHAINS):
        cols = slice(c * tn, (c + 1) * tn)
        idx, wts, rank, carry = _route_tile(lg_ref[:, cols], rb_ref[...], carry)
        idx_ref[:, cols] = idx
        wts_ref[cols, :] = wts.T
        rank_ref[:, cols] = rank
    carry_ref[...] = carry
    cnt_ref[...] = carry.astype(I32)


def _route_call(lg, rbias):
    t = lg.shape[1]
    tn = TN_ROUTE
    col = lambda i: (0, i)
    return pl.pallas_call(
        _route_kernel,
        grid=(t // tn,),
        in_specs=[pl.BlockSpec((N_EXPERTS, tn), col), pl.BlockSpec((N_EXPERTS, 1), lambda i: (0, 0))],
        out_specs=[pl.BlockSpec((TOP_K, tn), col), pl.BlockSpec((tn, TOP_K), lambda i: (i, 0)),
                   pl.BlockSpec((TOP_K, tn), col), pl.BlockSpec((N_EXPERTS, LANES), lambda i: (0, 0))],
        out_shape=[SDS((TOP_K, t), I32), SDS((t, TOP_K), F32), SDS((TOP_K, t), I32), SDS((N_EXPERTS, LANES), I32)],
        scratch_shapes=[pltpu.VMEM((N_EXPERTS, LANES), F32)],
        compiler_params=_cparams("arbitrary"),
        name="route",
    )(lg, rbias)


SC_CORES = 2
SC_SUBCORES = 16
SC_CHUNK = 128
SC_LANES = 16
SC_BUFS = 2


def _sc_worker_chunks(t):
    per_worker = t // (SC_CORES * SC_SUBCORES)
    assert per_worker % SC_CHUNK == 0
    return per_worker


def _sc_dispatch_call(idx, rank, offs, h1rows3, n_rows):
    t = idx.shape[1]
    per_worker = _sc_worker_chunks(t)
    mesh = plsc.VectorSubcoreMesh(core_axis_name="c", subcore_axis_name="s")

    @functools.partial(
        pl.kernel, mesh=mesh, out_type=[SDS((n_rows, ROW_CHUNKS, LANES), U32), SDS((TOP_K, t), I32)],
        scratch_types=[pltpu.VMEM((TOP_K, SC_CHUNK), I32), pltpu.VMEM((TOP_K, SC_CHUNK), I32),
                       pltpu.VMEM((N_EXPERTS,), I32), pltpu.VMEM((SC_CHUNK, ROW_CHUNKS, LANES), U32),
                       pltpu.SemaphoreType.DMA],
        compiler_params=pltpu.CompilerParams(needs_layout_passes=False),
        name="sc_dispatch")
    def body(h_hbm, idx_hbm, rank_hbm, offs_hbm, xs_hbm, dest_hbm, idx_v, rank_v, offs_v, rows_v, sem):
        wid = lax.axis_index("s") * SC_CORES + lax.axis_index("c")
        pltpu.sync_copy(offs_hbm, offs_v)

        @pl.loop(0, per_worker // SC_CHUNK)
        def _(i):
            t0 = wid * per_worker + i * SC_CHUNK
            pltpu.sync_copy(idx_hbm.at[:, pl.ds(t0, SC_CHUNK)], idx_v)
            pltpu.sync_copy(rank_hbm.at[:, pl.ds(t0, SC_CHUNK)], rank_v)
            pltpu.sync_copy(h_hbm.at[pl.ds(t0, SC_CHUNK)], rows_v)
            for kk in range(TOP_K):
                @pl.loop(0, SC_CHUNK // SC_LANES)
                def _(c):
                    lanes = pl.ds(c * SC_LANES, SC_LANES)
                    idx_v[kk, lanes] = plsc.load_gather(offs_v, [idx_v[kk, lanes]]) + rank_v[kk, lanes]
            pltpu.sync_copy(idx_v, dest_hbm.at[:, pl.ds(t0, SC_CHUNK)])
            copies = [pltpu.async_copy(rows_v, xs_hbm.at[idx_v.at[kk]], sem) for kk in range(TOP_K)]
            for c in copies:
                c.wait()

    return body(h1rows3, idx, rank, offs)


def _sc_gather_call(dest, ys3):
    t = dest.shape[1]
    per_worker = _sc_worker_chunks(t)
    mesh = plsc.VectorSubcoreMesh(core_axis_name="c", subcore_axis_name="s")

    half = SC_CHUNK // 2
    items = [(kk, h) for kk in range(TOP_K) for h in range(2)]

    @functools.partial(
        pl.kernel, mesh=mesh, out_type=SDS((TOP_K * t, ROW_CHUNKS, LANES), U32),
        scratch_types=[pltpu.VMEM((TOP_K, SC_CHUNK), I32), pltpu.VMEM((SC_BUFS, half, ROW_CHUNKS, LANES), U32),
                       pltpu.SemaphoreType.DMA((SC_BUFS,)), pltpu.SemaphoreType.DMA((SC_BUFS,))],
        name="sc_gather")
    def body(ys_hbm, dest_hbm, out_hbm, idx_v, rows_v, gsem, wsem):
        wid = lax.axis_index("s") * SC_CORES + lax.axis_index("c")

        @pl.loop(0, per_worker // SC_CHUNK)
        def _(i):
            t0 = wid * per_worker + i * SC_CHUNK
            pltpu.sync_copy(dest_hbm.at[:, pl.ds(t0, SC_CHUNK)], idx_v)

            def gather(j):
                kk, h = items[j]
                b = j % SC_BUFS
                return pltpu.make_async_copy(ys_hbm.at[idx_v.at[kk, pl.ds(h * half, half)]], rows_v.at[b], gsem.at[b])

            def put(j):
                kk, h = items[j]
                b = j % SC_BUFS
                return pltpu.make_async_copy(rows_v.at[b], out_hbm.at[pl.ds(kk * t + t0 + h * half, half)], wsem.at[b])

            ahead = SC_BUFS - 1
            n = len(items)
            for j in range(ahead):
                gather(j).start()
            for j in range(n):
                if j + ahead < n:
                    if j >= 1:
                        put(j - 1).wait()
                    gather(j + ahead).start()
                gather(j).wait()
                put(j).start()
            for j in range(max(n - ahead - 1, 0), n):
                put(j).wait()

    return body(ys3, dest)


def _expert_kernel(ts_ref, te_ref, tr_ref, nv_ref, wg_hbm, wu_hbm, wd_hbm, xs_hbm, ys_hbm,
                   xbuf, ybuf, wg_f, wu_f, wd_f, wg_b, wu_b, wd_b, xsem, ysem, wsem):
    e = pl.program_id(0)
    rows = xbuf.shape[1]
    tm = rows // ROW_CHUNKS
    g0, g1, nv = ts_ref[e], te_ref[e], nv_ref[0]

    def x_copy(g):
        s = g % X_SLOTS
        return pltpu.make_async_copy(xs_hbm.at[pl.ds(pl.multiple_of(g * rows, rows), rows), :], xbuf.at[s], xsem.at[s])

    def y_copy(g):
        s = g % Y_SLOTS
        return pltpu.make_async_copy(ybuf.at[s], ys_hbm.at[pl.ds(pl.multiple_of(g * rows, rows), rows), :], ysem.at[s])

    def w_copies(ex):
        s = ex % W_SLOTS
        return (pltpu.make_async_copy(wg_hbm.at[ex], wg_f.at[s], wsem.at[s]),
                pltpu.make_async_copy(wu_hbm.at[ex], wu_f.at[s], wsem.at[s]),
                pltpu.make_async_copy(wd_hbm.at[ex], wd_f.at[s], wsem.at[s]))

    n_exp = pl.num_programs(0)

    @pl.when(e == 0)
    def _():
        for ex in range(W_AHEAD):
            for c in w_copies(ex):
                c.start()

    @pl.when(e + W_AHEAD < n_exp)
    def _():
        for c in w_copies(e + W_AHEAD):
            c.start()

    for c in w_copies(e):
        c.wait()

    def compute_tile(g):
        x = _load_packed_bf16(xbuf, 0, tm, lead=g % X_SLOTS)
        gate = jnp.dot(x, wg_b[...], preferred_element_type=F32)
        up = jnp.dot(x, wu_b[...], preferred_element_type=F32)
        live = lax.broadcasted_iota(I32, (tm, EXPERT_FF), 0) < tr_ref[g]
        hid = jnp.where(live, gate * _sigmoid(gate) * up, 0.0).astype(BF16)
        y = jnp.dot(hid, wd_b[...], preferred_element_type=F32)
        return _pack_rows(y[:, :HALF], y[:, HALF:])

    def run_tiles(g, n):
        for r in range(n):
            x_copy(g + r).wait()

            @pl.when(g + r + X_AHEAD < nv)
            def _():
                x_copy(g + r + X_AHEAD).start(priority=1)

            @pl.when(g + r >= Y_SLOTS)
            def _():
                y_copy(g + r - Y_SLOTS).wait()

        packed = [compute_tile(g + r) for r in range(n)]
        for r in range(n):
            _store_packed(ybuf, 0, tm, packed[r], lead=(g + r) % Y_SLOTS)
        for r in range(n):
            y_copy(g + r).start(priority=1)

    @pl.when(e == 0)
    def _():
        for g in range(X_AHEAD):
            @pl.when(g < nv)
            def _():
                x_copy(g).start(priority=1)

    @pl.when(g1 > g0)
    def _():
        ws = e % W_SLOTS
        wg_b[...] = wg_f[ws].astype(BF16)
        wu_b[...] = wu_f[ws].astype(BF16)
        wd_b[...] = wd_f[ws].astype(BF16)
        n_tiles = g1 - g0

        def pair(p, c):
            run_tiles(g0 + 2 * p, 2)
            return c

        lax.fori_loop(0, n_tiles // 2, pair, 0)

        @pl.when(n_tiles % 2 == 1)
        def _():
            run_tiles(g1 - 1, 1)

    @pl.when(e == pl.num_programs(0) - 1)
    def _():
        for back in range(1, Y_SLOTS + 1):
            @pl.when(nv >= back)
            def _():
                y_copy(nv - back).wait()


def _expert_call(tile_start, tile_end, tile_rows, n_valid, xs, w_gate, w_up, w_down, n_rows):
    tm = TM_EXP
    hbm = pl.BlockSpec(memory_space=pl.ANY)
    return pl.pallas_call(
        _expert_kernel,
        grid_spec=pltpu.PrefetchScalarGridSpec(
            num_scalar_prefetch=4,
            grid=(N_EXPERTS,),
            in_specs=[hbm, hbm, hbm, hbm],
            out_specs=hbm,
            scratch_shapes=[pltpu.VMEM((X_SLOTS, tm * ROW_CHUNKS, LANES), U32),
                            pltpu.VMEM((Y_SLOTS, tm * ROW_CHUNKS, LANES), U32),
                            pltpu.VMEM((W_SLOTS, D_MODEL, EXPERT_FF), F32), pltpu.VMEM((W_SLOTS, D_MODEL, EXPERT_FF), F32),
                            pltpu.VMEM((W_SLOTS, EXPERT_FF, D_MODEL), F32),
                            pltpu.VMEM((D_MODEL, EXPERT_FF), BF16), pltpu.VMEM((D_MODEL, EXPERT_FF), BF16),
                            pltpu.VMEM((EXPERT_FF, D_MODEL), BF16),
                            pltpu.SemaphoreType.DMA((X_SLOTS,)), pltpu.SemaphoreType.DMA((Y_SLOTS,)),
                            pltpu.SemaphoreType.DMA((W_SLOTS,))],
        ),
        out_shape=SDS((n_rows * ROW_CHUNKS, LANES), U32),
        compiler_params=_cparams("arbitrary"),
        name="experts",
    )(tile_start, tile_end, tile_rows, n_valid, w_gate, w_up, w_down, xs)


COMB_SUB = 32


def _combine_kernel(wts_ref, h1_ref, g_ref, wsg_ref, wsu_ref, wsd_ref, g2_ref, b2_ref, o_ref, routed_ref):
    tn = h1_ref.shape[0]
    for s0 in range(0, tn, COMB_SUB):
        acc = [jnp.zeros((COMB_SUB, LANES), F32) for _ in range(2 * ROW_CHUNKS)]
        for kk in range(TOP_K):
            wk = jnp.broadcast_to(wts_ref[s0:s0 + COMB_SUB, kk:kk + 1], (COMB_SUB, LANES))
            for cc in range(ROW_CHUNKS):
                lo, hi = _unpack_rows(g_ref[kk, pl.ds(s0 * ROW_CHUNKS + cc, COMB_SUB, stride=ROW_CHUNKS), :])
                acc[cc] = acc[cc] + wk * lo
                acc[ROW_CHUNKS + cc] = acc[ROW_CHUNKS + cc] + wk * hi
        routed_ref[s0:s0 + COMB_SUB, :] = jnp.concatenate(acc, axis=1)

    h1 = h1_ref[...]
    hb = h1.astype(BF16)
    sg = jnp.dot(hb, wsg_ref[...], preferred_element_type=F32)
    su = jnp.dot(hb, wsu_ref[...], preferred_element_type=F32)
    ff = jnp.dot((sg * _sigmoid(sg) * su).astype(BF16), wsd_ref[...], preferred_element_type=F32)
    o_ref[...] = _layer_norm(ALPHA * h1 + ff + routed_ref[...], g2_ref[...], b2_ref[...])


def _combine_call(wts_t, h1, gathered, wsg, wsu, wsd, g2, b2):
    t = h1.shape[0]
    tn = TN_COMB
    row = lambda i: (i, 0)
    fix = lambda i: (0, 0)
    return pl.pallas_call(
        _combine_kernel,
        grid=(t // tn,),
        in_specs=[pl.BlockSpec((tn, TOP_K), row),
                  pl.BlockSpec((tn, D_MODEL), row),
                  pl.BlockSpec((TOP_K, tn * ROW_CHUNKS, LANES), lambda i: (0, i, 0)),
                  pl.BlockSpec((D_MODEL, SHARED_FF), fix), pl.BlockSpec((D_MODEL, SHARED_FF), fix),
                  pl.BlockSpec((SHARED_FF, D_MODEL), fix),
                  pl.BlockSpec((1, D_MODEL), fix), pl.BlockSpec((1, D_MODEL), fix)],
        out_specs=pl.BlockSpec((tn, D_MODEL), row),
        out_shape=SDS((t, D_MODEL), F32),
        scratch_shapes=[pltpu.VMEM((tn, D_MODEL), F32)],
        compiler_params=_cparams("arbitrary"),
        name="combine_ln2",
    )(wts_t, h1, gathered, wsg, wsu, wsd, g2, b2)


def kernel(x, meta_tokens, ln_in_g, ln_in_b, rel_bias, w_in, conv_w, conv_b, conv_ln_g, conv_ln_b, sinks,
           w_out, ln1_g, ln1_b, w_router, router_bias, w_gate, w_up, w_down, ws_gate, ws_up, ws_down,
           ln2_g, ln2_b):
    nbatch, seq, d = x.shape
    t = nbatch * seq
    assert d == D_MODEL and w_in.shape[0] == DEPTH
    assert seq % (ATTN_QBLOCKS * BLOCK) == 0 and seq % T_CONV == 0
    assert all(t % tile == 0 for tile in (TQ_PROJ, TQ_MIX, TN_ROUTE, TN_COMB))
    x2d = x.reshape(t, D_MODEL)
    vec = lambda a: a.reshape(1, -1).astype(F32)
    gin, bin_ = vec(ln_in_g), vec(ln_in_b)
    w_in_b = w_in[0].astype(BF16)

    q, k, v, u = _proj_call(x2d, gin, bin_, w_in_b, TQ_PROJ)
    meta_blk = jnp.concatenate([jnp.zeros((PAD_FRONT, D_MODEL), F32), meta_tokens.astype(F32)], axis=0)
    _, k_meta, v_meta, u_meta = _proj_call(meta_blk, gin, bin_, w_in_b, BLOCK)

    u_halo = jnp.concatenate([jnp.zeros((CONV_HALO - N_META, CONV_CH), F32), u_meta[PAD_FRONT:]], axis=0)
    attn, conv = _attn_conv_call(q, k, v, k_meta, v_meta, _rel_bias_table(rel_bias), sinks[0].astype(F32),
                                 u, u_halo, conv_w[0].astype(F32), vec(conv_b[0]), vec(conv_ln_g[0]),
                                 vec(conv_ln_b[0]), nbatch, seq)

    w_out_b = w_out[0].astype(BF16)
    h1, h1rows, logits = _mix_call(x2d, attn, conv, gin, bin_, w_out_b[:ATTN_W], w_out_b[ATTN_W:],
                                   vec(ln1_g[0]), vec(ln1_b[0]), w_router[0].astype(BF16).T)

    idx, wts_t, rank, cnt = _route_call(logits, router_bias[0].astype(F32).reshape(N_EXPERTS, 1))

    tm = TM_EXP
    n_tiles = (t * TOP_K) // tm + N_EXPERTS
    counts = cnt[:, 0]
    tiles_e = (counts + tm - 1) // tm
    tile_end = jnp.cumsum(tiles_e).astype(I32)
    tile_start = (tile_end - tiles_e).astype(I32)
    offs = tile_start * tm
    tile_id = jnp.arange(n_tiles, dtype=I32)
    lo = jnp.maximum(tile_id[:, None] * tm, offs[None, :])
    hi = jnp.minimum((tile_id[:, None] + 1) * tm, (offs + counts)[None, :])
    tile_rows = jnp.sum(jnp.clip(hi - lo, 0, tm), axis=1).astype(I32)
    n_valid = tile_end[-1:]

    xs, dest = _sc_dispatch_call(idx, rank, offs, h1rows.reshape(t, ROW_CHUNKS, LANES), n_tiles * tm)
    xs = xs.reshape(n_tiles * tm * ROW_CHUNKS, LANES)
    ys = _expert_call(tile_start, tile_end, tile_rows, n_valid, xs, w_gate[0], w_up[0], w_down[0], n_tiles * tm)
    gathered = _sc_gather_call(dest, ys.reshape(n_tiles * tm, ROW_CHUNKS, LANES))
    gathered = gathered.reshape(TOP_K, t * ROW_CHUNKS, LANES)
    out = _combine_call(wts_t, h1, gathered, ws_gate[0].astype(BF16), ws_up[0].astype(BF16),
                        ws_down[0].astype(BF16), vec(ln2_g[0]), vec(ln2_b[0]))
    return out.reshape(nbatch, seq, D_MODEL)
```

```python
import functools
import math

import numpy as np
import jax
import jax.numpy as jnp
from jax import lax
from jax.experimental import pallas as pl
from jax.experimental.pallas import tpu as pltpu
from jax.experimental.pallas import tpu_sc as plsc

F32 = jnp.float32
BF16 = jnp.bfloat16
I32 = jnp.int32
U32 = jnp.uint32
SDS = jax.ShapeDtypeStruct

D_MODEL = 1024
HALF = D_MODEL // 2
LANES = 128
SUBLANES = 8
ROW_CHUNKS = HALF // LANES
N_META = 16
HEAD_DIM = 64
N_Q_HEADS = 8
N_KV_HEADS = 2
GQA_GROUP = N_Q_HEADS // N_KV_HEADS
ATTN_W = N_Q_HEADS * HEAD_DIM
KV_W = N_KV_HEADS * HEAD_DIM
WINDOW = 128
BLOCK = 128
CONV_CH = D_MODEL - ATTN_W
CONV_K = 31
IN_W = ATTN_W + 2 * KV_W + 2 * CONV_CH
NUM_BUCKETS = 32
MAX_EXACT = NUM_BUCKETS // 2
REL_MAX_DIST = 128
N_EXPERTS = 256
TOP_K = 8
N_GROUPS = 8
GROUP_SIZE = N_EXPERTS // N_GROUPS
TOPK_GROUPS = 4
EXPERT_FF = 256
SHARED_FF = 256
ROUTED_SCALE = 2.5
DEPTH = 1
ALPHA = (2.0 * DEPTH) ** 0.25
LN_EPS = 1e-5
NEG = -1e30
PAD_FRONT = (-N_META) % BLOCK

VMEM_LIMIT = 48 * 1024 * 1024

TQ_PROJ = 1024
PROJ_CHAINS = 4
ATTN_QBLOCKS = 2
T_CONV = 256
CONV_HALO = 32
R_CONV = 64
TQ_MIX = 1024
MIX_CHAINS = 4
TN_ROUTE = 512
ROUTE_CHAINS = 2
TM_EXP = 256
X_SLOTS = 8
X_AHEAD = 4
Y_SLOTS = 4
W_SLOTS = 3
W_AHEAD = 2
TN_COMB = 512


def _cparams(*sem):
    return pltpu.CompilerParams(dimension_semantics=sem, vmem_limit_bytes=VMEM_LIMIT)


def _layer_norm(x, g, b):
    mu = jnp.mean(x, axis=-1, keepdims=True)
    xc = x - mu
    var = jnp.mean(xc * xc, axis=-1, keepdims=True)
    return xc * lax.rsqrt(var + LN_EPS) * g + b


def _sigmoid(x):
    return 1.0 / (1.0 + jnp.exp(-x))


def _pack_rows(lo_half, hi_half):
    lo = lax.bitcast_convert_type(lo_half.astype(BF16).astype(F32), U32)
    hi = lax.bitcast_convert_type(hi_half.astype(BF16).astype(F32), U32)
    return lax.shift_right_logical(lo, jnp.uint32(16)) | hi


def _unpack_rows(p):
    lo = lax.bitcast_convert_type(lax.shift_left(p, jnp.uint32(16)), F32)
    hi = lax.bitcast_convert_type(p & jnp.uint32(0xFFFF0000), F32)
    return lo, hi


def _chunk_index(start, j, n, lead):
    rows = pl.ds(start + j, n, stride=ROW_CHUNKS)
    return (rows, slice(None)) if lead is None else (lead, rows, slice(None))


def _store_packed(ref, start, n, packed, lead=None):
    for j in range(ROW_CHUNKS):
        ref[_chunk_index(start, j, n, lead)] = packed[:, j * LANES:(j + 1) * LANES]


def _load_packed_bf16(ref, start, n, lead=None):
    halves = [_unpack_rows(ref[_chunk_index(start, j, n, lead)]) for j in range(ROW_CHUNKS)]
    return jnp.concatenate([h[0] for h in halves] + [h[1] for h in halves], axis=1).astype(BF16)


def _proj_kernel(chains, x_ref, g_ref, b_ref, w_ref, h_ref, q_ref, k_ref, v_ref, u_ref):
    rows = x_ref.shape[0] // chains
    for c in range(chains):
        r = slice(c * rows, (c + 1) * rows)
        h = _layer_norm(x_ref[r, :], g_ref[...], b_ref[...])
        h_ref[r, :] = h
        p = jnp.dot(h.astype(BF16), w_ref[...], preferred_element_type=F32)
        q_ref[r, :] = (p[:, :ATTN_W] * (HEAD_DIM ** -0.5)).astype(BF16)
        k_ref[r, :] = p[:, ATTN_W:ATTN_W + KV_W].astype(BF16)
        v_ref[r, :] = p[:, ATTN_W + KV_W:ATTN_W + 2 * KV_W].astype(BF16)
        a = p[:, ATTN_W + 2 * KV_W:ATTN_W + 2 * KV_W + CONV_CH]
        gate = p[:, ATTN_W + 2 * KV_W + CONV_CH:]
        u_ref[r, :] = a * _sigmoid(gate)


def _proj_call(x2d, gin, bin_, w_in_b, tq):
    t = x2d.shape[0]
    row = lambda i: (i, 0)
    fix = lambda i: (0, 0)
    chains = PROJ_CHAINS if tq % (PROJ_CHAINS * BLOCK) == 0 else 1
    return pl.pallas_call(
        functools.partial(_proj_kernel, chains),
        grid=(t // tq,),
        in_specs=[pl.BlockSpec((tq, D_MODEL), row), pl.BlockSpec((1, D_MODEL), fix),
                  pl.BlockSpec((1, D_MODEL), fix), pl.BlockSpec((D_MODEL, IN_W), fix)],
        out_specs=[pl.BlockSpec((tq, D_MODEL), row), pl.BlockSpec((tq, ATTN_W), row), pl.BlockSpec((tq, KV_W), row),
                   pl.BlockSpec((tq, KV_W), row), pl.BlockSpec((tq, CONV_CH), row)],
        out_shape=[SDS((t, D_MODEL), F32), SDS((t, ATTN_W), BF16), SDS((t, KV_W), BF16), SDS((t, KV_W), BF16),
                   SDS((t, CONV_CH), F32)],
        compiler_params=_cparams("arbitrary"),
        name="ln_in_proj",
    )(x2d, gin, bin_, w_in_b)


def _attn_stages(sinks_ref, q_ref, kc_ref, kp_ref, vc_ref, vp_ref, km_ref, vm_ref, bias_ref, o_ref):
    first = pl.program_id(1) == 0
    kp = jnp.where(first, km_ref[...], kp_ref[...])
    vp = jnp.where(first, vm_ref[...], vp_ref[...])
    k = jnp.concatenate([kp, kc_ref[...]], axis=0)
    v = jnp.concatenate([vp, vc_ref[...]], axis=0)
    col = lax.broadcasted_iota(I32, (BLOCK, 2 * BLOCK), 1)
    pad_bias = jnp.where(jnp.logical_and(first, col < PAD_FRONT), NEG, 0.0).astype(F32)
    def block(a):
        q = q_ref[a * BLOCK:(a + 1) * BLOCK, :]
        kw = k[a * BLOCK:(a + 2) * BLOCK, :]
        vw = v[a * BLOCK:(a + 2) * BLOCK, :]
        outs = []
        for h in range(N_Q_HEADS):
            g = h // GQA_GROUP
            qh = q[:, h * HEAD_DIM:(h + 1) * HEAD_DIM]
            kg = kw[:, g * HEAD_DIM:(g + 1) * HEAD_DIM]
            vg = vw[:, g * HEAD_DIM:(g + 1) * HEAD_DIM]
            s = lax.dot_general(qh, kg, (((1,), (1,)), ((), ())), preferred_element_type=F32)
            s = s + bias_ref[h]
            if a == 0:
                s = s + pad_bias
            sink = sinks_ref[h]
            m = jnp.maximum(jnp.max(s, axis=-1, keepdims=True), sink)
            p = jnp.exp(s - m)
            den = jnp.sum(p, axis=-1, keepdims=True) + jnp.exp(sink - m)
            o = jnp.dot(p.astype(BF16), vg, preferred_element_type=F32)
            outs.append(o / den)
        o_ref[a * BLOCK:(a + 1) * BLOCK, :] = jnp.concatenate(outs, axis=1).astype(BF16)

    return [functools.partial(block, a) for a in range(ATTN_QBLOCKS)]


def _rel_bias_table(rel_bias):
    qi = np.arange(BLOCK, dtype=np.int32)[:, None]
    kj = np.arange(2 * BLOCK, dtype=np.int32)[None, :]
    dist = BLOCK + qi - kj
    dc = np.clip(dist, 0, WINDOW - 1)
    nf = np.maximum(dc, 1).astype(np.float32)
    large = MAX_EXACT + (np.log(nf / np.float32(MAX_EXACT)) / np.float32(math.log(REL_MAX_DIST / MAX_EXACT))
                         * np.float32(NUM_BUCKETS - MAX_EXACT)).astype(np.int32)
    large = np.minimum(large, NUM_BUCKETS - 1)
    bucket = np.where(dc < MAX_EXACT, dc, large)
    in_window = (dist >= 0) & (dist < WINDOW)
    onehot = (bucket.reshape(-1, 1) == np.arange(NUM_BUCKETS)[None, :]).astype(np.float32)
    bias = jnp.dot(jnp.asarray(onehot), rel_bias.astype(F32), precision=lax.Precision.HIGHEST)
    bias = jnp.transpose(bias.reshape(BLOCK, 2 * BLOCK, N_Q_HEADS), (2, 0, 1))
    return jnp.where(in_window[None], bias, NEG)


def _conv_stages(uc_ref, up_ref, um_ref, w_ref, cb_ref, g_ref, b_ref, o_ref, s_ref, sh_ref):
    first = pl.program_id(1) == 0
    s_ref[0:CONV_HALO, :] = jnp.where(first, um_ref[...], up_ref[...])
    s_ref[CONV_HALO:CONV_HALO + T_CONV, :] = uc_ref[...]
    off = CONV_HALO - (CONV_K - 1)
    span = sh_ref.shape[1]
    for p in range(1, SUBLANES):
        sh_ref[p] = s_ref[p:p + span, :]
    def chunk(c):
        acc = jnp.zeros((R_CONV, CONV_CH), F32) + cb_ref[...]
        for kk in range(CONV_K):
            p, a = (off + kk) % SUBLANES, (off + kk) // SUBLANES * SUBLANES
            if p == 0:
                win = s_ref[c + a:c + a + R_CONV, :]
            else:
                win = sh_ref[p, c + a:c + a + R_CONV, :]
            acc = acc + win * w_ref[kk:kk + 1, :]
        y = _layer_norm(acc, g_ref[...], b_ref[...])
        o_ref[c:c + R_CONV, :] = (y * _sigmoid(y)).astype(BF16)

    return [functools.partial(chunk, c) for c in range(0, T_CONV, R_CONV)]


def _attn_conv_kernel(sinks_ref, q_ref, kc_ref, kp_ref, vc_ref, vp_ref, km_ref, vm_ref, bias_ref,
                      uc_ref, up_ref, um_ref, w_ref, cb_ref, g_ref, b_ref, ao_ref, co_ref, s_ref, sh_ref):
    conv = _conv_stages(uc_ref, up_ref, um_ref, w_ref, cb_ref, g_ref, b_ref, co_ref, s_ref, sh_ref)
    attn = _attn_stages(sinks_ref, q_ref, kc_ref, kp_ref, vc_ref, vp_ref, km_ref, vm_ref, bias_ref, ao_ref)
    per = len(conv) // len(attn)
    for a, attn_block in enumerate(attn):
        for chunk in conv[a * per:(a + 1) * per]:
            chunk()
        attn_block()


def _attn_conv_call(q, k, v, k_meta, v_meta, bias, sinks, u, u_meta_halo, conv_w, conv_b, g, b, nbatch, seq):
    t = q.shape[0]
    rows = ATTN_QBLOCKS * BLOCK
    assert rows == T_CONV
    nstep = seq // rows
    per_blk = rows // BLOCK
    per_halo = rows // CONV_HALO
    cur = lambda bb, j: (bb * nstep + j, 0)
    prev_blk = lambda bb, j: (jnp.maximum((bb * nstep + j) * per_blk - 1, 0), 0)
    prev_halo = lambda bb, j: (jnp.maximum((bb * nstep + j) * per_halo - 1, 0), 0)
    fix = lambda bb, j: (0, 0)
    return pl.pallas_call(
        _attn_conv_kernel,
        grid=(nbatch, nstep),
        in_specs=[pl.BlockSpec(memory_space=pltpu.SMEM),
                  pl.BlockSpec((rows, ATTN_W), cur),
                  pl.BlockSpec((rows, KV_W), cur), pl.BlockSpec((BLOCK, KV_W), prev_blk),
                  pl.BlockSpec((rows, KV_W), cur), pl.BlockSpec((BLOCK, KV_W), prev_blk),
                  pl.BlockSpec((BLOCK, KV_W), fix), pl.BlockSpec((BLOCK, KV_W), fix),
                  pl.BlockSpec((N_Q_HEADS, BLOCK, 2 * BLOCK), lambda bb, j: (0, 0, 0)),
                  pl.BlockSpec((rows, CONV_CH), cur), pl.BlockSpec((CONV_HALO, CONV_CH), prev_halo),
                  pl.BlockSpec((CONV_HALO, CONV_CH), fix), pl.BlockSpec((CONV_K, CONV_CH), fix),
                  pl.BlockSpec((1, CONV_CH), fix), pl.BlockSpec((1, CONV_CH), fix), pl.BlockSpec((1, CONV_CH), fix)],
        out_specs=[pl.BlockSpec((rows, ATTN_W), cur), pl.BlockSpec((rows, CONV_CH), cur)],
        out_shape=[SDS((t, ATTN_W), BF16), SDS((t, CONV_CH), BF16)],
        scratch_shapes=[pltpu.VMEM((CONV_HALO + T_CONV, CONV_CH), F32),
                        pltpu.VMEM((SUBLANES, T_CONV + CONV_HALO - SUBLANES, CONV_CH), F32)],
        compiler_params=_cparams("arbitrary", "arbitrary"),
        name="attn_conv",
    )(sinks, q, k, k, v, v, k_meta, v_meta, bias, u, u, u_meta_halo, conv_w, conv_b, g, b)


def _mix_kernel(h_ref, at_ref, cv_ref, woa_ref, woc_ref, g1_ref, b1_ref, wr_ref, h1_ref, h1r_ref, lg_ref):
    rows = h_ref.shape[0] // MIX_CHAINS
    nt = (((1,), (1,)), ((), ()))
    for c in range(MIX_CHAINS):
        r = slice(c * rows, (c + 1) * rows)
        mix = (jnp.dot(at_ref[r, :], woa_ref[...], preferred_element_type=F32)
               + jnp.dot(cv_ref[r, :], woc_ref[...], preferred_element_type=F32))
        h1 = _layer_norm(ALPHA * h_ref[r, :] + mix, g1_ref[...], b1_ref[...])
        h1_ref[r, :] = h1
        _store_packed(h1r_ref, c * rows * ROW_CHUNKS, rows, _pack_rows(h1[:, :HALF], h1[:, HALF:]))
        lg_ref[:, r] = lax.dot_general(wr_ref[...], h1.astype(BF16), nt, preferred_element_type=F32)


def _mix_call(h, attn, conv, woa, woc, g1, b1, wr):
    t = h.shape[0]
    tq = TQ_MIX
    row = lambda i: (i, 0)
    fix = lambda i: (0, 0)
    return pl.pallas_call(
        _mix_kernel,
        grid=(t // tq,),
        in_specs=[pl.BlockSpec((tq, D_MODEL), row), pl.BlockSpec((tq, ATTN_W), row), pl.BlockSpec((tq, CONV_CH), row),
                  pl.BlockSpec((ATTN_W, D_MODEL), fix), pl.BlockSpec((CONV_CH, D_MODEL), fix),
                  pl.BlockSpec((1, D_MODEL), fix), pl.BlockSpec((1, D_MODEL), fix),
                  pl.BlockSpec((N_EXPERTS, D_MODEL), fix)],
        out_specs=[pl.BlockSpec((tq, D_MODEL), row), pl.BlockSpec((tq * ROW_CHUNKS, LANES), row),
                   pl.BlockSpec((N_EXPERTS, tq), lambda i: (0, i))],
        out_shape=[SDS((t, D_MODEL), F32), SDS((t * ROW_CHUNKS, LANES), U32), SDS((N_EXPERTS, t), F32)],
        compiler_params=_cparams("arbitrary"),
        name="mix_ln1",
    )(h, attn, conv, woa, woc, g1, b1, wr)


def _row_numbers(nrows, n):
    return lax.broadcasted_iota(I32, (nrows, n), 0).astype(F32)


def _first_argmax(x, rows, nrows):
    m = jnp.max(x, axis=0, keepdims=True)
    idx = jnp.min(jnp.where(x == m, rows, float(nrows)), axis=0, keepdims=True)
    return m, idx


def _route_tile(logits, rbias, carry):
    tn = logits.shape[1]
    scores = _sigmoid(logits)
    choice = scores + rbias
    rows = _row_numbers(N_EXPERTS, tn)
    rows_g = _row_numbers(GROUP_SIZE, tn)
    rows_8 = _row_numbers(N_GROUPS, tn)

    gs = []
    for g in range(N_GROUPS):
        xg = choice[g * GROUP_SIZE:(g + 1) * GROUP_SIZE, :]
        m1, i1 = _first_argmax(xg, rows_g, GROUP_SIZE)
        m2 = jnp.max(jnp.where(rows_g == i1, -jnp.inf, xg), axis=0, keepdims=True)
        gs.append(m1 + m2)
    gsc = jnp.concatenate(gs, axis=0)
    gsel = jnp.zeros((N_GROUPS, tn), F32)
    for _ in range(TOPK_GROUPS):
        _, gi = _first_argmax(gsc, rows_8, N_GROUPS)
        hit = rows_8 == gi
        gsel = jnp.where(hit, 1.0, gsel)
        gsc = jnp.where(hit, -jnp.inf, gsc)
    emask = jnp.concatenate(
        [jnp.broadcast_to(gsel[g:g + 1, :], (GROUP_SIZE, tn)) for g in range(N_GROUPS)], axis=0)
    masked = jnp.where(emask > 0.5, jnp.maximum(choice, float(jnp.finfo(F32).min)), NEG)

    hits, idxs, ws = [], [], []
    for _ in range(TOP_K):
        _, ii = _first_argmax(masked, rows, N_EXPERTS)
        hit = rows == ii
        hits.append(hit)
        idxs.append(ii)
        ws.append(jnp.sum(jnp.where(hit, scores, 0.0), axis=0, keepdims=True))
        masked = jnp.where(hit, -jnp.inf, masked)
    wsum = ws[0]
    for w in ws[1:]:
        wsum = wsum + w
    idx = jnp.concatenate(idxs, axis=0).astype(I32)
    wts = jnp.concatenate([w / wsum * ROUTED_SCALE for w in ws], axis=0)

    r_i = lax.broadcasted_iota(I32, (tn, tn), 0)
    c_i = lax.broadcasted_iota(I32, (tn, tn), 1)
    upper = jnp.where(r_i < c_i, 1.0, 0.0).astype(BF16)
    sel_b = jnp.where(masked == -jnp.inf, 1.0, 0.0).astype(BF16)
    before = jnp.dot(sel_b, upper, preferred_element_type=F32)
    before = before + jnp.concatenate([carry] * (tn // LANES), axis=1)
    rank = jnp.concatenate(
        [jnp.sum(jnp.where(h, before, 0.0), axis=0, keepdims=True) for h in hits], axis=0).astype(I32)
    carry = carry + jnp.dot(sel_b, jnp.ones((tn, LANES), BF16), preferred_element_type=F32)
    return idx, wts, rank, carry


def _route_kernel(lg_ref, rb_ref, idx_ref, wts_ref, rank_ref, cnt_ref, carry_ref):
    @pl.when(pl.program_id(0) == 0)
    def _():
        carry_ref[...] = jnp.zeros_like(carry_ref)

    carry = carry_ref[...]
    tn = lg_ref.shape[1] // ROUTE_CHAINS
    for c in range(ROUTE_CHAINS):
        cols = slice(c * tn, (c + 1) * tn)
        idx, wts, rank, carry = _route_tile(lg_ref[:, cols], rb_ref[...], carry)
        idx_ref[:, cols] = idx
        wts_ref[cols, :] = wts.T
        rank_ref[:, cols] = rank
    carry_ref[...] = carry
    cnt_ref[...] = carry.astype(I32)


def _route_call(lg, rbias):
    t = lg.shape[1]
    tn = TN_ROUTE
    col = lambda i: (0, i)
    return pl.pallas_call(
        _route_kernel,
        grid=(t // tn,),
        in_specs=[pl.BlockSpec((N_EXPERTS, tn), col), pl.BlockSpec((N_EXPERTS, 1), lambda i: (0, 0))],
        out_specs=[pl.BlockSpec((TOP_K, tn), col), pl.BlockSpec((tn, TOP_K), lambda i: (i, 0)),
                   pl.BlockSpec((TOP_K, tn), col), pl.BlockSpec((N_EXPERTS, LANES), lambda i: (0, 0))],
        out_shape=[SDS((TOP_K, t), I32), SDS((t, TOP_K), F32), SDS((TOP_K, t), I32), SDS((N_EXPERTS, LANES), I32)],
        scratch_shapes=[pltpu.VMEM((N_EXPERTS, LANES), F32)],
        compiler_params=_cparams("arbitrary"),
        name="route",
    )(lg, rbias)


SC_CORES = 2
SC_SUBCORES = 16
SC_CHUNK = 128
SC_LANES = 16
SC_BUFS = 2


def _sc_worker_chunks(t):
    per_worker = t // (SC_CORES * SC_SUBCORES)
    assert per_worker % SC_CHUNK == 0
    return per_worker


def _sc_dispatch_call(idx, rank, offs, h1rows3, n_rows):
    t = idx.shape[1]
    per_worker = _sc_worker_chunks(t)
    mesh = plsc.VectorSubcoreMesh(core_axis_name="c", subcore_axis_name="s")

    @functools.partial(
        pl.kernel, mesh=mesh, out_type=[SDS((n_rows, ROW_CHUNKS, LANES), U32), SDS((TOP_K, t), I32)],
        scratch_types=[pltpu.VMEM((TOP_K, SC_CHUNK), I32), pltpu.VMEM((TOP_K, SC_CHUNK), I32),
                       pltpu.VMEM((N_EXPERTS,), I32), pltpu.VMEM((SC_CHUNK, ROW_CHUNKS, LANES), U32),
                       pltpu.SemaphoreType.DMA],
        compiler_params=pltpu.CompilerParams(needs_layout_passes=False),
        name="sc_dispatch")
    def body(h_hbm, idx_hbm, rank_hbm, offs_hbm, xs_hbm, dest_hbm, idx_v, rank_v, offs_v, rows_v, sem):
        wid = lax.axis_index("s") * SC_CORES + lax.axis_index("c")
        pltpu.sync_copy(offs_hbm, offs_v)

        @pl.loop(0, per_worker // SC_CHUNK)
        def _(i):
            t0 = wid * per_worker + i * SC_CHUNK
            pltpu.sync_copy(idx_hbm.at[:, pl.ds(t0, SC_CHUNK)], idx_v)
            pltpu.sync_copy(rank_hbm.at[:, pl.ds(t0, SC_CHUNK)], rank_v)
            pltpu.sync_copy(h_hbm.at[pl.ds(t0, SC_CHUNK)], rows_v)
            for kk in range(TOP_K):
                @pl.loop(0, SC_CHUNK // SC_LANES)
                def _(c):
                    lanes = pl.ds(c * SC_LANES, SC_LANES)
                    idx_v[kk, lanes] = plsc.load_gather(offs_v, [idx_v[kk, lanes]]) + rank_v[kk, lanes]
            pltpu.sync_copy(idx_v, dest_hbm.at[:, pl.ds(t0, SC_CHUNK)])
            copies = [pltpu.async_copy(rows_v, xs_hbm.at[idx_v.at[kk]], sem) for kk in range(TOP_K)]
            for c in copies:
                c.wait()

    return body(h1rows3, idx, rank, offs)


def _sc_gather_call(dest, ys3):
    t = dest.shape[1]
    per_worker = _sc_worker_chunks(t)
    mesh = plsc.VectorSubcoreMesh(core_axis_name="c", subcore_axis_name="s")

    half = SC_CHUNK // 2
    items = [(kk, h) for kk in range(TOP_K) for h in range(2)]

    @functools.partial(
        pl.kernel, mesh=mesh, out_type=SDS((TOP_K * t, ROW_CHUNKS, LANES), U32),
        scratch_types=[pltpu.VMEM((TOP_K, SC_CHUNK), I32), pltpu.VMEM((SC_BUFS, half, ROW_CHUNKS, LANES), U32),
                       pltpu.SemaphoreType.DMA((SC_BUFS,)), pltpu.SemaphoreType.DMA((SC_BUFS,))],
        name="sc_gather")
    def body(ys_hbm, dest_hbm, out_hbm, idx_v, rows_v, gsem, wsem):
        wid = lax.axis_index("s") * SC_CORES + lax.axis_index("c")

        @pl.loop(0, per_worker // SC_CHUNK)
        def _(i):
            t0 = wid * per_worker + i * SC_CHUNK
            pltpu.sync_copy(dest_hbm.at[:, pl.ds(t0, SC_CHUNK)], idx_v)

            def gather(j):
                kk, h = items[j]
                b = j % SC_BUFS
                return pltpu.make_async_copy(ys_hbm.at[idx_v.at[kk, pl.ds(h * half, half)]], rows_v.at[b], gsem.at[b])

            def put(j):
                kk, h = items[j]
                b = j % SC_BUFS
                return pltpu.make_async_copy(rows_v.at[b], out_hbm.at[pl.ds(kk * t + t0 + h * half, half)], wsem.at[b])

            ahead = SC_BUFS - 1
            n = len(items)
            for j in range(ahead):
                gather(j).start()
            for j in range(n):
                if j + ahead < n:
                    if j >= 1:
                        put(j - 1).wait()
                    gather(j + ahead).start()
                gather(j).wait()
                put(j).start()
            for j in range(max(n - ahead - 1, 0), n):
                put(j).wait()

    return body(ys3, dest)


def _expert_kernel(ts_ref, te_ref, tr_ref, nv_ref, wg_hbm, wu_hbm, wd_hbm, xs_hbm, ys_hbm,
                   xbuf, ybuf, wg_f, wu_f, wd_f, wg_b, wu_b, wd_b, xsem, ysem, wsem):
    e = pl.program_id(0)
    rows = xbuf.shape[1]
    tm = rows // ROW_CHUNKS
    g0, g1, nv = ts_ref[e], te_ref[e], nv_ref[0]

    def x_copy(g):
        s = g % X_SLOTS
        return pltpu.make_async_copy(xs_hbm.at[pl.ds(pl.multiple_of(g * rows, rows), rows), :], xbuf.at[s], xsem.at[s])

    def y_copy(g):
        s = g % Y_SLOTS
        return pltpu.make_async_copy(ybuf.at[s], ys_hbm.at[pl.ds(pl.multiple_of(g * rows, rows), rows), :], ysem.at[s])

    def w_copies(ex):
        s = ex % W_SLOTS
        return (pltpu.make_async_copy(wg_hbm.at[ex], wg_f.at[s], wsem.at[s]),
                pltpu.make_async_copy(wu_hbm.at[ex], wu_f.at[s], wsem.at[s]),
                pltpu.make_async_copy(wd_hbm.at[ex], wd_f.at[s], wsem.at[s]))

    n_exp = pl.num_programs(0)

    @pl.when(e == 0)
    def _():
        for ex in range(W_AHEAD):
            for c in w_copies(ex):
                c.start()

    @pl.when(e + W_AHEAD < n_exp)
    def _():
        for c in w_copies(e + W_AHEAD):
            c.start()

    for c in w_copies(e):
        c.wait()

    def compute_tile(g):
        x = _load_packed_bf16(xbuf, 0, tm, lead=g % X_SLOTS)
        gate = jnp.dot(x, wg_b[...], preferred_element_type=F32)
        up = jnp.dot(x, wu_b[...], preferred_element_type=F32)
        live = lax.broadcasted_iota(I32, (tm, EXPERT_FF), 0) < tr_ref[g]
        hid = jnp.where(live, gate * _sigmoid(gate) * up, 0.0).astype(BF16)
        y = jnp.dot(hid, wd_b[...], preferred_element_type=F32)
        return _pack_rows(y[:, :HALF], y[:, HALF:])

    def run_tiles(g, n):
        for r in range(n):
            x_copy(g + r).wait()

            @pl.when(g + r + X_AHEAD < nv)
            def _():
                x_copy(g + r + X_AHEAD).start(priority=1)

            @pl.when(g + r >= Y_SLOTS)
            def _():
                y_copy(g + r - Y_SLOTS).wait()

        packed = [compute_tile(g + r) for r in range(n)]
        for r in range(n):
            _store_packed(ybuf, 0, tm, packed[r], lead=(g + r) % Y_SLOTS)
        for r in range(n):
            y_copy(g + r).start(priority=1)

    @pl.when(e == 0)
    def _():
        for g in range(X_AHEAD):
            @pl.when(g < nv)
            def _():
                x_copy(g).start(priority=1)

    @pl.when(g1 > g0)
    def _():
        ws = e % W_SLOTS
        wg_b[...] = wg_f[ws].astype(BF16)
        wu_b[...] = wu_f[ws].astype(BF16)
        wd_b[...] = wd_f[ws].astype(BF16)
        n_tiles = g1 - g0

        def pair(p, c):
            run_tiles(g0 + 2 * p, 2)
            return c

        lax.fori_loop(0, n_tiles // 2, pair, 0)

        @pl.when(n_tiles % 2 == 1)
        def _():
            run_tiles(g1 - 1, 1)

    @pl.when(e == pl.num_programs(0) - 1)
    def _():
        for back in range(1, Y_SLOTS + 1):
            @pl.when(nv >= back)
            def _():
                y_copy(nv - back).wait()


def _expert_call(tile_start, tile_end, tile_rows, n_valid, xs, w_gate, w_up, w_down, n_rows):
    tm = TM_EXP
    hbm = pl.BlockSpec(memory_space=pl.ANY)
    return pl.pallas_call(
        _expert_kernel,
        grid_spec=pltpu.PrefetchScalarGridSpec(
            num_scalar_prefetch=4,
            grid=(N_EXPERTS,),
            in_specs=[hbm, hbm, hbm, hbm],
            out_specs=hbm,
            scratch_shapes=[pltpu.VMEM((X_SLOTS, tm * ROW_CHUNKS, LANES), U32),
                            pltpu.VMEM((Y_SLOTS, tm * ROW_CHUNKS, LANES), U32),
                            pltpu.VMEM((W_SLOTS, D_MODEL, EXPERT_FF), F32), pltpu.VMEM((W_SLOTS, D_MODEL, EXPERT_FF), F32),
                            pltpu.VMEM((W_SLOTS, EXPERT_FF, D_MODEL), F32),
                            pltpu.VMEM((D_MODEL, EXPERT_FF), BF16), pltpu.VMEM((D_MODEL, EXPERT_FF), BF16),
                            pltpu.VMEM((EXPERT_FF, D_MODEL), BF16),
                            pltpu.SemaphoreType.DMA((X_SLOTS,)), pltpu.SemaphoreType.DMA((Y_SLOTS,)),
                            pltpu.SemaphoreType.DMA((W_SLOTS,))],
        ),
        out_shape=SDS((n_rows * ROW_CHUNKS, LANES), U32),
        compiler_params=_cparams("arbitrary"),
        name="experts",
    )(tile_start, tile_end, tile_rows, n_valid, w_gate, w_up, w_down, xs)


COMB_SUB = 32


def _combine_kernel(wts_ref, h1_ref, g_ref, wsg_ref, wsu_ref, wsd_ref, g2_ref, b2_ref, o_ref, routed_ref):
    tn = h1_ref.shape[0]
    for s0 in range(0, tn, COMB_SUB):
        acc = [jnp.zeros((COMB_SUB, LANES), F32) for _ in range(2 * ROW_CHUNKS)]
        for kk in range(TOP_K):
            wk = jnp.broadcast_to(wts_ref[s0:s0 + COMB_SUB, kk:kk + 1], (COMB_SUB, LANES))
            for cc in range(ROW_CHUNKS):
                lo, hi = _unpack_rows(g_ref[kk, pl.ds(s0 * ROW_CHUNKS + cc, COMB_SUB, stride=ROW_CHUNKS), :])
                acc[cc] = acc[cc] + wk * lo
                acc[ROW_CHUNKS + cc] = acc[ROW_CHUNKS + cc] + wk * hi
        routed_ref[s0:s0 + COMB_SUB, :] = jnp.concatenate(acc, axis=1)

    h1 = h1_ref[...]
    hb = h1.astype(BF16)
    sg = jnp.dot(hb, wsg_ref[...], preferred_element_type=F32)
    su = jnp.dot(hb, wsu_ref[...], preferred_element_type=F32)
    ff = jnp.dot((sg * _sigmoid(sg) * su).astype(BF16), wsd_ref[...], preferred_element_type=F32)
    o_ref[...] = _layer_norm(ALPHA * h1 + ff + routed_ref[...], g2_ref[...], b2_ref[...])


def _combine_call(wts_t, h1, gathered, wsg, wsu, wsd, g2, b2):
    t = h1.shape[0]
    tn = TN_COMB
    row = lambda i: (i, 0)
    fix = lambda i: (0, 0)
    return pl.pallas_call(
        _combine_kernel,
        grid=(t // tn,),
        in_specs=[pl.BlockSpec((tn, TOP_K), row),
                  pl.BlockSpec((tn, D_MODEL), row),
                  pl.BlockSpec((TOP_K, tn * ROW_CHUNKS, LANES), lambda i: (0, i, 0)),
                  pl.BlockSpec((D_MODEL, SHARED_FF), fix), pl.BlockSpec((D_MODEL, SHARED_FF), fix),
                  pl.BlockSpec((SHARED_FF, D_MODEL), fix),
                  pl.BlockSpec((1, D_MODEL), fix), pl.BlockSpec((1, D_MODEL), fix)],
        out_specs=pl.BlockSpec((tn, D_MODEL), row),
        out_shape=SDS((t, D_MODEL), F32),
        scratch_shapes=[pltpu.VMEM((tn, D_MODEL), F32)],
        compiler_params=_cparams("arbitrary"),
        name="combine_ln2",
    )(wts_t, h1, gathered, wsg, wsu, wsd, g2, b2)


def kernel(x, meta_tokens, ln_in_g, ln_in_b, rel_bias, w_in, conv_w, conv_b, conv_ln_g, conv_ln_b, sinks,
           w_out, ln1_g, ln1_b, w_router, router_bias, w_gate, w_up, w_down, ws_gate, ws_up, ws_down,
           ln2_g, ln2_b):
    nbatch, seq, d = x.shape
    t = nbatch * seq
    assert d == D_MODEL and w_in.shape[0] == DEPTH
    assert seq % (ATTN_QBLOCKS * BLOCK) == 0 and seq % T_CONV == 0
    assert all(t % tile == 0 for tile in (TQ_PROJ, TQ_MIX, TN_ROUTE, TN_COMB))
    x2d = x.reshape(t, D_MODEL)
    vec = lambda a: a.reshape(1, -1).astype(F32)
    gin, bin_ = vec(ln_in_g), vec(ln_in_b)
    w_in_b = w_in[0].astype(BF16)

    h, q, k, v, u = _proj_call(x2d, gin, bin_, w_in_b, TQ_PROJ)
    meta_blk = jnp.concatenate([jnp.zeros((PAD_FRONT, D_MODEL), F32), meta_tokens.astype(F32)], axis=0)
    _, _, k_meta, v_meta, u_meta = _proj_call(meta_blk, gin, bin_, w_in_b, BLOCK)

    u_halo = jnp.concatenate([jnp.zeros((CONV_HALO - N_META, CONV_CH), F32), u_meta[PAD_FRONT:]], axis=0)
    attn, conv = _attn_conv_call(q, k, v, k_meta, v_meta, _rel_bias_table(rel_bias), sinks[0].astype(F32),
                                 u, u_halo, conv_w[0].astype(F32), vec(conv_b[0]), vec(conv_ln_g[0]),
                                 vec(conv_ln_b[0]), nbatch, seq)

    w_out_b = w_out[0].astype(BF16)
    h1, h1rows, logits = _mix_call(h, attn, conv, w_out_b[:ATTN_W], w_out_b[ATTN_W:],
                                   vec(ln1_g[0]), vec(ln1_b[0]), w_router[0].astype(BF16).T)

    idx, wts_t, rank, cnt = _route_call(logits, router_bias[0].astype(F32).reshape(N_EXPERTS, 1))

    tm = TM_EXP
    n_tiles = (t * TOP_K) // tm + N_EXPERTS
    counts = cnt[:, 0]
    tiles_e = (counts + tm - 1) // tm
    tile_end = jnp.cumsum(tiles_e).astype(I32)
    tile_start = (tile_end - tiles_e).astype(I32)
    offs = tile_start * tm
    tile_id = jnp.arange(n_tiles, dtype=I32)
    lo = jnp.maximum(tile_id[:, None] * tm, offs[None, :])
    hi = jnp.minimum((tile_id[:, None] + 1) * tm, (offs + counts)[None, :])
    tile_rows = jnp.sum(jnp.clip(hi - lo, 0, tm), axis=1).astype(I32)
    n_valid = tile_end[-1:]

    xs, dest = _sc_dispatch_call(idx, rank, offs, h1rows.reshape(t, ROW_CHUNKS, LANES), n_tiles * tm)
    xs = xs.reshape(n_tiles * tm * ROW_CHUNKS, LANES)
    ys = _expert_call(tile_start, tile_end, tile_rows, n_valid, xs, w_gate[0], w_up[0], w_down[0], n_tiles * tm)
    gathered = _sc_gather_call(dest, ys.reshape(n_tiles * tm, ROW_CHUNKS, LANES))
    gathered = gathered.reshape(TOP_K, t * ROW_CHUNKS, LANES)
    out = _combine_call(wts_t, h1, gathered, ws_gate[0].astype(BF16), ws_up[0].astype(BF16),
                        ws_down[0].astype(BF16), vec(ln2_g[0]), vec(ln2_b[0]))
    return out.reshape(nbatch, seq, D_MODEL)
```

```python
import functools
import math

import numpy as np
import jax
import jax.numpy as jnp
from jax import lax
from jax.experimental import pallas as pl
from jax.experimental.pallas import tpu as pltpu
from jax.experimental.pallas import tpu_sc as plsc

F32 = jnp.float32
BF16 = jnp.bfloat16
I32 = jnp.int32
U32 = jnp.uint32
SDS = jax.ShapeDtypeStruct

D_MODEL = 1024
HALF = D_MODEL // 2
LANES = 128
SUBLANES = 8
ROW_CHUNKS = HALF // LANES
N_META = 16
HEAD_DIM = 64
N_Q_HEADS = 8
N_KV_HEADS = 2
GQA_GROUP = N_Q_HEADS // N_KV_HEADS
ATTN_W = N_Q_HEADS * HEAD_DIM
KV_W = N_KV_HEADS * HEAD_DIM
WINDOW = 128
BLOCK = 128
CONV_CH = D_MODEL - ATTN_W
CONV_K = 31
IN_W = ATTN_W + 2 * KV_W + 2 * CONV_CH
NUM_BUCKETS = 32
MAX_EXACT = NUM_BUCKETS // 2
REL_MAX_DIST = 128
N_EXPERTS = 256
TOP_K = 8
N_GROUPS = 8
GROUP_SIZE = N_EXPERTS // N_GROUPS
TOPK_GROUPS = 4
EXPERT_FF = 256
SHARED_FF = 256
ROUTED_SCALE = 2.5
DEPTH = 1
ALPHA = (2.0 * DEPTH) ** 0.25
LN_EPS = 1e-5
NEG = -1e30
PAD_FRONT = (-N_META) % BLOCK

VMEM_LIMIT = 48 * 1024 * 1024

TQ_PROJ = 1024
PROJ_CHAINS = 4
ATTN_QBLOCKS = 2
T_CONV = 256
CONV_HALO = 32
R_CONV = 64
TQ_MIX = 1024
MIX_CHAINS = 4
TN_ROUTE = 512
ROUTE_CHAINS = 2
TM_EXP = 256
X_SLOTS = 8
X_AHEAD = 4
Y_SLOTS = 4
W_SLOTS = 3
W_AHEAD = 2
TN_COMB = 512


def _cparams(*sem):
    return pltpu.CompilerParams(dimension_semantics=sem, vmem_limit_bytes=VMEM_LIMIT)


def _layer_norm(x, g, b):
    mu = jnp.mean(x, axis=-1, keepdims=True)
    xc = x - mu
    var = jnp.mean(xc * xc, axis=-1, keepdims=True)
    return xc * lax.rsqrt(var + LN_EPS) * g + b


def _sigmoid(x):
    return 1.0 / (1.0 + jnp.exp(-x))


def _pack_rows(lo_half, hi_half):
    lo = lax.bitcast_convert_type(lo_half.astype(BF16).astype(F32), U32)
    hi = lax.bitcast_convert_type(hi_half.astype(BF16).astype(F32), U32)
    return lax.shift_right_logical(lo, jnp.uint32(16)) | hi


def _unpack_rows(p):
    lo = lax.bitcast_convert_type(lax.shift_left(p, jnp.uint32(16)), F32)
    hi = lax.bitcast_convert_type(p & jnp.uint32(0xFFFF0000), F32)
    return lo, hi


def _chunk_index(start, j, n, lead):
    rows = pl.ds(start + j, n, stride=ROW_CHUNKS)
    return (rows, slice(None)) if lead is None else (lead, rows, slice(None))


def _store_packed(ref, start, n, packed, lead=None):
    for j in range(ROW_CHUNKS):
        ref[_chunk_index(start, j, n, lead)] = packed[:, j * LANES:(j + 1) * LANES]


def _load_packed_bf16(ref, start, n, lead=None):
    halves = [_unpack_rows(ref[_chunk_index(start, j, n, lead)]) for j in range(ROW_CHUNKS)]
    return jnp.concatenate([h[0] for h in halves] + [h[1] for h in halves], axis=1).astype(BF16)


def _proj_kernel(chains, x_ref, g_ref, b_ref, w_ref, q_ref, k_ref, v_ref, u_ref):
    rows = x_ref.shape[0] // chains
    for c in range(chains):
        r = slice(c * rows, (c + 1) * rows)
        h = _layer_norm(x_ref[r, :], g_ref[...], b_ref[...])
        p = jnp.dot(h.astype(BF16), w_ref[...], preferred_element_type=F32)
        q_ref[r, :] = (p[:, :ATTN_W] * (HEAD_DIM ** -0.5)).astype(BF16)
        k_ref[r, :] = p[:, ATTN_W:ATTN_W + KV_W].astype(BF16)
        v_ref[r, :] = p[:, ATTN_W + KV_W:ATTN_W + 2 * KV_W].astype(BF16)
        a = p[:, ATTN_W + 2 * KV_W:ATTN_W + 2 * KV_W + CONV_CH]
        gate = p[:, ATTN_W + 2 * KV_W + CONV_CH:]
        u_ref[r, :] = a * _sigmoid(gate)


def _proj_call(x2d, gin, bin_, w_in_b, tq):
    t = x2d.shape[0]
    row = lambda i: (i, 0)
    fix = lambda i: (0, 0)
    chains = PROJ_CHAINS if tq % (PROJ_CHAINS * BLOCK) == 0 else 1
    return pl.pallas_call(
        functools.partial(_proj_kernel, chains),
        grid=(t // tq,),
        in_specs=[pl.BlockSpec((tq, D_MODEL), row), pl.BlockSpec((1, D_MODEL), fix),
                  pl.BlockSpec((1, D_MODEL), fix), pl.BlockSpec((D_MODEL, IN_W), fix)],
        out_specs=[pl.BlockSpec((tq, ATTN_W), row), pl.BlockSpec((tq, KV_W), row),
                   pl.BlockSpec((tq, KV_W), row), pl.BlockSpec((tq, CONV_CH), row)],
        out_shape=[SDS((t, ATTN_W), BF16), SDS((t, KV_W), BF16), SDS((t, KV_W), BF16), SDS((t, CONV_CH), F32)],
        compiler_params=_cparams("arbitrary"),
        name="ln_in_proj",
    )(x2d, gin, bin_, w_in_b)


def _attn_stages(sinks_ref, q_ref, kc_ref, kp_ref, vc_ref, vp_ref, km_ref, vm_ref, bias_ref, o_ref):
    first = pl.program_id(1) == 0
    kp = jnp.where(first, km_ref[...], kp_ref[...])
    vp = jnp.where(first, vm_ref[...], vp_ref[...])
    k = jnp.concatenate([kp, kc_ref[...]], axis=0)
    v = jnp.concatenate([vp, vc_ref[...]], axis=0)
    col = lax.broadcasted_iota(I32, (BLOCK, 2 * BLOCK), 1)
    pad_bias = jnp.where(jnp.logical_and(first, col < PAD_FRONT), NEG, 0.0).astype(F32)
    def block(a):
        q = q_ref[a * BLOCK:(a + 1) * BLOCK, :]
        kw = k[a * BLOCK:(a + 2) * BLOCK, :]
        vw = v[a * BLOCK:(a + 2) * BLOCK, :]
        outs = []
        for h in range(N_Q_HEADS):
            g = h // GQA_GROUP
            qh = q[:, h * HEAD_DIM:(h + 1) * HEAD_DIM]
            kg = kw[:, g * HEAD_DIM:(g + 1) * HEAD_DIM]
            vg = vw[:, g * HEAD_DIM:(g + 1) * HEAD_DIM]
            s = lax.dot_general(qh, kg, (((1,), (1,)), ((), ())), preferred_element_type=F32)
            s = s + bias_ref[h]
            if a == 0:
                s = s + pad_bias
            sink = sinks_ref[h]
            m = jnp.maximum(jnp.max(s, axis=-1, keepdims=True), sink)
            p = jnp.exp(s - m)
            den = jnp.sum(p, axis=-1, keepdims=True) + jnp.exp(sink - m)
            o = jnp.dot(p.astype(BF16), vg, preferred_element_type=F32)
            outs.append(o / den)
        o_ref[a * BLOCK:(a + 1) * BLOCK, :] = jnp.concatenate(outs, axis=1).astype(BF16)

    return [functools.partial(block, a) for a in range(ATTN_QBLOCKS)]


def _rel_bias_table(rel_bias):
    qi = np.arange(BLOCK, dtype=np.int32)[:, None]
    kj = np.arange(2 * BLOCK, dtype=np.int32)[None, :]
    dist = BLOCK + qi - kj
    dc = np.clip(dist, 0, WINDOW - 1)
    nf = np.maximum(dc, 1).astype(np.float32)
    large = MAX_EXACT + (np.log(nf / np.float32(MAX_EXACT)) / np.float32(math.log(REL_MAX_DIST / MAX_EXACT))
                         * np.float32(NUM_BUCKETS - MAX_EXACT)).astype(np.int32)
    large = np.minimum(large, NUM_BUCKETS - 1)
    bucket = np.where(dc < MAX_EXACT, dc, large)
    in_window = (dist >= 0) & (dist < WINDOW)
    onehot = (bucket.reshape(-1, 1) == np.arange(NUM_BUCKETS)[None, :]).astype(np.float32)
    bias = jnp.dot(jnp.asarray(onehot), rel_bias.astype(F32), precision=lax.Precision.HIGHEST)
    bias = jnp.transpose(bias.reshape(BLOCK, 2 * BLOCK, N_Q_HEADS), (2, 0, 1))
    return jnp.where(in_window[None], bias, NEG)


def _conv_stages(uc_ref, up_ref, um_ref, w_ref, cb_ref, g_ref, b_ref, o_ref, s_ref, sh_ref):
    first = pl.program_id(1) == 0
    s_ref[0:CONV_HALO, :] = jnp.where(first, um_ref[...], up_ref[...])
    s_ref[CONV_HALO:CONV_HALO + T_CONV, :] = uc_ref[...]
    off = CONV_HALO - (CONV_K - 1)
    span = sh_ref.shape[1]
    for p in range(1, SUBLANES):
        sh_ref[p] = s_ref[p:p + span, :]
    def chunk(c):
        acc = jnp.zeros((R_CONV, CONV_CH), F32) + cb_ref[...]
        for kk in range(CONV_K):
            p, a = (off + kk) % SUBLANES, (off + kk) // SUBLANES * SUBLANES
            if p == 0:
                win = s_ref[c + a:c + a + R_CONV, :]
            else:
                win = sh_ref[p, c + a:c + a + R_CONV, :]
            acc = acc + win * w_ref[kk:kk + 1, :]
        y = _layer_norm(acc, g_ref[...], b_ref[...])
        o_ref[c:c + R_CONV, :] = (y * _sigmoid(y)).astype(BF16)

    return [functools.partial(chunk, c) for c in range(0, T_CONV, R_CONV)]


def _attn_conv_kernel(sinks_ref, q_ref, kc_ref, kp_ref, vc_ref, vp_ref, km_ref, vm_ref, bias_ref,
                      uc_ref, up_ref, um_ref, w_ref, cb_ref, g_ref, b_ref, ao_ref, co_ref, s_ref, sh_ref):
    conv = _conv_stages(uc_ref, up_ref, um_ref, w_ref, cb_ref, g_ref, b_ref, co_ref, s_ref, sh_ref)
    attn = _attn_stages(sinks_ref, q_ref, kc_ref, kp_ref, vc_ref, vp_ref, km_ref, vm_ref, bias_ref, ao_ref)
    per = len(conv) // len(attn)
    for a, attn_block in enumerate(attn):
        for chunk in conv[a * per:(a + 1) * per]:
            chunk()
        attn_block()


def _attn_conv_call(q, k, v, k_meta, v_meta, bias, sinks, u, u_meta_halo, conv_w, conv_b, g, b, nbatch, seq):
    t = q.shape[0]
    rows = ATTN_QBLOCKS * BLOCK
    assert rows == T_CONV
    nstep = seq // rows
    per_blk = rows // BLOCK
    per_halo = rows // CONV_HALO
    cur = lambda bb, j: (bb * nstep + j, 0)
    prev_blk = lambda bb, j: (jnp.maximum((bb * nstep + j) * per_blk - 1, 0), 0)
    prev_halo = lambda bb, j: (jnp.maximum((bb * nstep + j) * per_halo - 1, 0), 0)
    fix = lambda bb, j: (0, 0)
    return pl.pallas_call(
        _attn_conv_kernel,
        grid=(nbatch, nstep),
        in_specs=[pl.BlockSpec(memory_space=pltpu.SMEM),
                  pl.BlockSpec((rows, ATTN_W), cur),
                  pl.BlockSpec((rows, KV_W), cur), pl.BlockSpec((BLOCK, KV_W), prev_blk),
                  pl.BlockSpec((rows, KV_W), cur), pl.BlockSpec((BLOCK, KV_W), prev_blk),
                  pl.BlockSpec((BLOCK, KV_W), fix), pl.BlockSpec((BLOCK, KV_W), fix),
                  pl.BlockSpec((N_Q_HEADS, BLOCK, 2 * BLOCK), lambda bb, j: (0, 0, 0)),
                  pl.BlockSpec((rows, CONV_CH), cur), pl.BlockSpec((CONV_HALO, CONV_CH), prev_halo),
                  pl.BlockSpec((CONV_HALO, CONV_CH), fix), pl.BlockSpec((CONV_K, CONV_CH), fix),
                  pl.BlockSpec((1, CONV_CH), fix), pl.BlockSpec((1, CONV_CH), fix), pl.BlockSpec((1, CONV_CH), fix)],
        out_specs=[pl.BlockSpec((rows, ATTN_W), cur), pl.BlockSpec((rows, CONV_CH), cur)],
        out_shape=[SDS((t, ATTN_W), BF16), SDS((t, CONV_CH), BF16)],
        scratch_shapes=[pltpu.VMEM((CONV_HALO + T_CONV, CONV_CH), F32),
                        pltpu.VMEM((SUBLANES, T_CONV + CONV_HALO - SUBLANES, CONV_CH), F32)],
        compiler_params=_cparams("arbitrary", "arbitrary"),
        name="attn_conv",
    )(sinks, q, k, k, v, v, k_meta, v_meta, bias, u, u, u_meta_halo, conv_w, conv_b, g, b)


def _mix_kernel(x_ref, at_ref, cv_ref, gin_ref, bin_ref, woa_ref, woc_ref, g1_ref, b1_ref,
                wr_ref, h1_ref, h1r_ref, lg_ref):
    rows = x_ref.shape[0] // MIX_CHAINS
    nt = (((1,), (1,)), ((), ()))
    for c in range(MIX_CHAINS):
        r = slice(c * rows, (c + 1) * rows)
        h = _layer_norm(x_ref[r, :], gin_ref[...], bin_ref[...])
        mix = (jnp.dot(at_ref[r, :], woa_ref[...], preferred_element_type=F32)
               + jnp.dot(cv_ref[r, :], woc_ref[...], preferred_element_type=F32))
        h1 = _layer_norm(ALPHA * h + mix, g1_ref[...], b1_ref[...])
        h1_ref[r, :] = h1
        _store_packed(h1r_ref, c * rows * ROW_CHUNKS, rows, _pack_rows(h1[:, :HALF], h1[:, HALF:]))
        lg_ref[:, r] = lax.dot_general(wr_ref[...], h1.astype(BF16), nt, preferred_element_type=F32)


def _mix_call(x2d, attn, conv, gin, bin_, woa, woc, g1, b1, wr):
    t = x2d.shape[0]
    tq = TQ_MIX
    row = lambda i: (i, 0)
    fix = lambda i: (0, 0)
    return pl.pallas_call(
        _mix_kernel,
        grid=(t // tq,),
        in_specs=[pl.BlockSpec((tq, D_MODEL), row), pl.BlockSpec((tq, ATTN_W), row), pl.BlockSpec((tq, CONV_CH), row),
                  pl.BlockSpec((1, D_MODEL), fix), pl.BlockSpec((1, D_MODEL), fix),
                  pl.BlockSpec((ATTN_W, D_MODEL), fix), pl.BlockSpec((CONV_CH, D_MODEL), fix),
                  pl.BlockSpec((1, D_MODEL), fix), pl.BlockSpec((1, D_MODEL), fix),
                  pl.BlockSpec((N_EXPERTS, D_MODEL), fix)],
        out_specs=[pl.BlockSpec((tq, D_MODEL), row), pl.BlockSpec((tq * ROW_CHUNKS, LANES), row),
                   pl.BlockSpec((N_EXPERTS, tq), lambda i: (0, i))],
        out_shape=[SDS((t, D_MODEL), F32), SDS((t * ROW_CHUNKS, LANES), U32), SDS((N_EXPERTS, t), F32)],
        compiler_params=_cparams("arbitrary"),
        name="mix_ln1",
    )(x2d, attn, conv, gin, bin_, woa, woc, g1, b1, wr)


def _row_numbers(nrows, n):
    return lax.broadcasted_iota(I32, (nrows, n), 0).astype(F32)


def _first_argmax(x, rows, nrows):
    m = jnp.max(x, axis=0, keepdims=True)
    idx = jnp.min(jnp.where(x == m, rows, float(nrows)), axis=0, keepdims=True)
    return m, idx


def _route_tile(logits, rbias, carry):
    tn = logits.shape[1]
    scores = _sigmoid(logits)
    choice = scores + rbias
    rows_g = _row_numbers(GROUP_SIZE, tn)
    rows_8 = _row_numbers(N_GROUPS, tn)

    gs = []
    for g in range(N_GROUPS):
        xg = choice[g * GROUP_SIZE:(g + 1) * GROUP_SIZE, :]
        m1, i1 = _first_argmax(xg, rows_g, GROUP_SIZE)
        m2 = jnp.max(jnp.where(rows_g == i1, -jnp.inf, xg), axis=0, keepdims=True)
        gs.append(m1 + m2)
    gsc = jnp.concatenate(gs, axis=0)
    gsel = jnp.zeros((N_GROUPS, tn), F32)
    for _ in range(TOPK_GROUPS):
        _, gi = _first_argmax(gsc, rows_8, N_GROUPS)
        hit = rows_8 == gi
        gsel = jnp.where(hit, 1.0, gsel)
        gsc = jnp.where(hit, -jnp.inf, gsc)

    spread = N_GROUPS - TOPK_GROUPS
    g_on = [gsel[g:g + 1, :] > 0.5 for g in range(N_GROUPS)]
    pos, n_before = [], jnp.zeros((1, tn), F32)
    for g in range(N_GROUPS):
        pos.append(n_before)
        n_before = n_before + gsel[g:g + 1, :]
    in_slot = [{g: jnp.logical_and(g_on[g], pos[g] == float(j)) for g in range(j, j + spread + 1)}
               for j in range(TOPK_GROUPS)]

    def group_rows(x, g):
        return x[g * GROUP_SIZE:(g + 1) * GROUP_SIZE, :]

    def compact(x):
        slots = []
        for j in range(TOPK_GROUPS):
            v = group_rows(x, j + spread)
            for g in range(j + spread - 1, j - 1, -1):
                v = jnp.where(in_slot[j][g], group_rows(x, g), v)
            slots.append(v)
        return jnp.concatenate(slots, axis=0)

    n_c = TOPK_GROUPS * GROUP_SIZE
    rows_c = _row_numbers(n_c, tn)
    scores_c = compact(scores)
    masked = jnp.maximum(compact(choice), float(jnp.finfo(F32).min))
    shift = []
    for j in range(TOPK_GROUPS):
        s = jnp.full((1, tn), float((j + spread) * GROUP_SIZE), F32)
        for g in range(j + spread - 1, j - 1, -1):
            s = jnp.where(in_slot[j][g], float(g * GROUP_SIZE), s)
        shift.append(s - float(j * GROUP_SIZE))

    hits, idxs, ws = [], [], []
    for _ in range(TOP_K):
        _, ii = _first_argmax(masked, rows_c, n_c)
        hit = rows_c == ii
        hits.append(hit)
        e = ii + shift[TOPK_GROUPS - 1]
        for j in range(TOPK_GROUPS - 2, -1, -1):
            e = jnp.where(ii < float((j + 1) * GROUP_SIZE), ii + shift[j], e)
        idxs.append(e)
        ws.append(jnp.sum(jnp.where(hit, scores_c, 0.0), axis=0, keepdims=True))
        masked = jnp.where(hit, -jnp.inf, masked)
    wsum = ws[0]
    for w in ws[1:]:
        wsum = wsum + w
    idx = jnp.concatenate(idxs, axis=0).astype(I32)
    wts = jnp.concatenate([w / wsum * ROUTED_SCALE for w in ws], axis=0)

    picked_c = jnp.where(masked == -jnp.inf, 1.0, 0.0)
    sel = []
    for g in range(N_GROUPS):
        v = jnp.zeros((GROUP_SIZE, tn), F32)
        for j in range(max(0, g - spread), min(TOPK_GROUPS - 1, g) + 1):
            v = jnp.where(in_slot[j][g], group_rows(picked_c, j), v)
        sel.append(v)
    sel_b = jnp.concatenate(sel, axis=0).astype(BF16)

    r_i = lax.broadcasted_iota(I32, (tn, tn), 0)
    c_i = lax.broadcasted_iota(I32, (tn, tn), 1)
    upper = jnp.where(r_i < c_i, 1.0, 0.0).astype(BF16)
    before = jnp.dot(sel_b, upper, preferred_element_type=F32)
    before = compact(before + jnp.concatenate([carry] * (tn // LANES), axis=1))
    rank = jnp.concatenate(
        [jnp.sum(jnp.where(h, before, 0.0), axis=0, keepdims=True) for h in hits], axis=0).astype(I32)
    carry = carry + jnp.dot(sel_b, jnp.ones((tn, LANES), BF16), preferred_element_type=F32)
    return idx, wts, rank, carry


def _route_kernel(lg_ref, rb_ref, idx_ref, wts_ref, rank_ref, cnt_ref, carry_ref):
    @pl.when(pl.program_id(0) == 0)
    def _():
        carry_ref[...] = jnp.zeros_like(carry_ref)

    carry = carry_ref[...]
    tn = lg_ref.shape[1] // ROUTE_CHAINS
    for c in range(ROUTE_CHAINS):
        cols = slice(c * tn, (c + 1) * tn)
        idx, wts, rank, carry = _route_tile(lg_ref[:, cols], rb_ref[...], carry)
        idx_ref[:, cols] = idx
        wts_ref[cols, :] = wts.T
        rank_ref[:, cols] = rank
    carry_ref[...] = carry
    cnt_ref[...] = carry.astype(I32)


def _route_call(lg, rbias):
    t = lg.shape[1]
    tn = TN_ROUTE
    col = lambda i: (0, i)
    return pl.pallas_call(
        _route_kernel,
        grid=(t // tn,),
        in_specs=[pl.BlockSpec((N_EXPERTS, tn), col), pl.BlockSpec((N_EXPERTS, 1), lambda i: (0, 0))],
        out_specs=[pl.BlockSpec((TOP_K, tn), col), pl.BlockSpec((tn, TOP_K), lambda i: (i, 0)),
                   pl.BlockSpec((TOP_K, tn), col), pl.BlockSpec((N_EXPERTS, LANES), lambda i: (0, 0))],
        out_shape=[SDS((TOP_K, t), I32), SDS((t, TOP_K), F32), SDS((TOP_K, t), I32), SDS((N_EXPERTS, LANES), I32)],
        scratch_shapes=[pltpu.VMEM((N_EXPERTS, LANES), F32)],
        compiler_params=_cparams("arbitrary"),
        name="route",
    )(lg, rbias)


SC_CORES = 2
SC_SUBCORES = 16
SC_CHUNK = 128
SC_LANES = 16
SC_BUFS = 2


def _sc_worker_chunks(t):
    per_worker = t // (SC_CORES * SC_SUBCORES)
    assert per_worker % SC_CHUNK == 0
    return per_worker


def _sc_dispatch_call(idx, rank, offs, h1rows3, n_rows):
    t = idx.shape[1]
    per_worker = _sc_worker_chunks(t)
    mesh = plsc.VectorSubcoreMesh(core_axis_name="c", subcore_axis_name="s")

    @functools.partial(
        pl.kernel, mesh=mesh, out_type=[SDS((n_rows, ROW_CHUNKS, LANES), U32), SDS((TOP_K, t), I32)],
        scratch_types=[pltpu.VMEM((TOP_K, SC_CHUNK), I32), pltpu.VMEM((TOP_K, SC_CHUNK), I32),
                       pltpu.VMEM((N_EXPERTS,), I32), pltpu.VMEM((SC_CHUNK, ROW_CHUNKS, LANES), U32),
                       pltpu.SemaphoreType.DMA],
        compiler_params=pltpu.CompilerParams(needs_layout_passes=False),
        name="sc_dispatch")
    def body(h_hbm, idx_hbm, rank_hbm, offs_hbm, xs_hbm, dest_hbm, idx_v, rank_v, offs_v, rows_v, sem):
        wid = lax.axis_index("s") * SC_CORES + lax.axis_index("c")
        pltpu.sync_copy(offs_hbm, offs_v)

        @pl.loop(0, per_worker // SC_CHUNK)
        def _(i):
            t0 = wid * per_worker + i * SC_CHUNK
            pltpu.sync_copy(idx_hbm.at[:, pl.ds(t0, SC_CHUNK)], idx_v)
            pltpu.sync_copy(rank_hbm.at[:, pl.ds(t0, SC_CHUNK)], rank_v)
            pltpu.sync_copy(h_hbm.at[pl.ds(t0, SC_CHUNK)], rows_v)
            for kk in range(TOP_K):
                @pl.loop(0, SC_CHUNK // SC_LANES)
                def _(c):
                    lanes = pl.ds(c * SC_LANES, SC_LANES)
                    idx_v[kk, lanes] = plsc.load_gather(offs_v, [idx_v[kk, lanes]]) + rank_v[kk, lanes]
            pltpu.sync_copy(idx_v, dest_hbm.at[:, pl.ds(t0, SC_CHUNK)])
            copies = [pltpu.async_copy(rows_v, xs_hbm.at[idx_v.at[kk]], sem) for kk in range(TOP_K)]
            for c in copies:
                c.wait()

    return body(h1rows3, idx, rank, offs)


def _sc_gather_call(dest, ys3):
    t = dest.shape[1]
    per_worker = _sc_worker_chunks(t)
    mesh = plsc.VectorSubcoreMesh(core_axis_name="c", subcore_axis_name="s")

    half = SC_CHUNK // 2
    items = [(kk, h) for kk in range(TOP_K) for h in range(2)]

    @functools.partial(
        pl.kernel, mesh=mesh, out_type=SDS((TOP_K * t, ROW_CHUNKS, LANES), U32),
        scratch_types=[pltpu.VMEM((TOP_K, SC_CHUNK), I32), pltpu.VMEM((SC_BUFS, half, ROW_CHUNKS, LANES), U32),
                       pltpu.SemaphoreType.DMA((SC_BUFS,)), pltpu.SemaphoreType.DMA((SC_BUFS,))],
        name="sc_gather")
    def body(ys_hbm, dest_hbm, out_hbm, idx_v, rows_v, gsem, wsem):
        wid = lax.axis_index("s") * SC_CORES + lax.axis_index("c")

        @pl.loop(0, per_worker // SC_CHUNK)
        def _(i):
            t0 = wid * per_worker + i * SC_CHUNK
            pltpu.sync_copy(dest_hbm.at[:, pl.ds(t0, SC_CHUNK)], idx_v)

            def gather(j):
                kk, h = items[j]
                b = j % SC_BUFS
                return pltpu.make_async_copy(ys_hbm.at[idx_v.at[kk, pl.ds(h * half, half)]], rows_v.at[b], gsem.at[b])

            def put(j):
                kk, h = items[j]
                b = j % SC_BUFS
                return pltpu.make_async_copy(rows_v.at[b], out_hbm.at[pl.ds(kk * t + t0 + h * half, half)], wsem.at[b])

            ahead = SC_BUFS - 1
            n = len(items)
            for j in range(ahead):
                gather(j).start()
            for j in range(n):
                if j + ahead < n:
                    if j >= 1:
                        put(j - 1).wait()
                    gather(j + ahead).start()
                gather(j).wait()
                put(j).start()
            for j in range(max(n - ahead - 1, 0), n):
                put(j).wait()

    return body(ys3, dest)


def _expert_kernel(ts_ref, te_ref, tr_ref, nv_ref, wg_hbm, wu_hbm, wd_hbm, xs_hbm, ys_hbm,
                   xbuf, ybuf, wg_f, wu_f, wd_f, wg_b, wu_b, wd_b, xsem, ysem, wsem):
    e = pl.program_id(0)
    rows = xbuf.shape[1]
    tm = rows // ROW_CHUNKS
    g0, g1, nv = ts_ref[e], te_ref[e], nv_ref[0]

    def x_copy(g):
        s = g % X_SLOTS
        return pltpu.make_async_copy(xs_hbm.at[pl.ds(pl.multiple_of(g * rows, rows), rows), :], xbuf.at[s], xsem.at[s])

    def y_copy(g):
        s = g % Y_SLOTS
        return pltpu.make_async_copy(ybuf.at[s], ys_hbm.at[pl.ds(pl.multiple_of(g * rows, rows), rows), :], ysem.at[s])

    def w_copies(ex):
        s = ex % W_SLOTS
        return (pltpu.make_async_copy(wg_hbm.at[ex], wg_f.at[s], wsem.at[s]),
                pltpu.make_async_copy(wu_hbm.at[ex], wu_f.at[s], wsem.at[s]),
                pltpu.make_async_copy(wd_hbm.at[ex], wd_f.at[s], wsem.at[s]))

    n_exp = pl.num_programs(0)

    @pl.when(e == 0)
    def _():
        for ex in range(W_AHEAD):
            for c in w_copies(ex):
                c.start()

    @pl.when(e + W_AHEAD < n_exp)
    def _():
        for c in w_copies(e + W_AHEAD):
            c.start()

    for c in w_copies(e):
        c.wait()

    def compute_tile(g):
        x = _load_packed_bf16(xbuf, 0, tm, lead=g % X_SLOTS)
        gate = jnp.dot(x, wg_b[...], preferred_element_type=F32)
        up = jnp.dot(x, wu_b[...], preferred_element_type=F32)
        live = lax.broadcasted_iota(I32, (tm, EXPERT_FF), 0) < tr_ref[g]
        hid = jnp.where(live, gate * _sigmoid(gate) * up, 0.0).astype(BF16)
        y = jnp.dot(hid, wd_b[...], preferred_element_type=F32)
        return _pack_rows(y[:, :HALF], y[:, HALF:])

    def run_tiles(g, n):
        for r in range(n):
            x_copy(g + r).wait()

            @pl.when(g + r + X_AHEAD < nv)
            def _():
                x_copy(g + r + X_AHEAD).start(priority=1)

            @pl.when(g + r >= Y_SLOTS)
            def _():
                y_copy(g + r - Y_SLOTS).wait()

        packed = [compute_tile(g + r) for r in range(n)]
        for r in range(n):
            _store_packed(ybuf, 0, tm, packed[r], lead=(g + r) % Y_SLOTS)
        for r in range(n):
            y_copy(g + r).start(priority=1)

    @pl.when(e == 0)
    def _():
        for g in range(X_AHEAD):
            @pl.when(g < nv)
            def _():
                x_copy(g).start(priority=1)

    @pl.when(g1 > g0)
    def _():
        ws = e % W_SLOTS
        wg_b[...] = wg_f[ws].astype(BF16)
        wu_b[...] = wu_f[ws].astype(BF16)
        wd_b[...] = wd_f[ws].astype(BF16)
        n_tiles = g1 - g0

        def pair(p, c):
            run_tiles(g0 + 2 * p, 2)
            return c

        lax.fori_loop(0, n_tiles // 2, pair, 0)

        @pl.when(n_tiles % 2 == 1)
        def _():
            run_tiles(g1 - 1, 1)

    @pl.when(e == pl.num_programs(0) - 1)
    def _():
        for back in range(1, Y_SLOTS + 1):
            @pl.when(nv >= back)
            def _():
                y_copy(nv - back).wait()


def _expert_call(tile_start, tile_end, tile_rows, n_valid, xs, w_gate, w_up, w_down, n_rows):
    tm = TM_EXP
    hbm = pl.BlockSpec(memory_space=pl.ANY)
    return pl.pallas_call(
        _expert_kernel,
        grid_spec=pltpu.PrefetchScalarGridSpec(
            num_scalar_prefetch=4,
            grid=(N_EXPERTS,),
            in_specs=[hbm, hbm, hbm, hbm],
            out_specs=hbm,
            scratch_shapes=[pltpu.VMEM((X_SLOTS, tm * ROW_CHUNKS, LANES), U32),
                            pltpu.VMEM((Y_SLOTS, tm * ROW_CHUNKS, LANES), U32),
                            pltpu.VMEM((W_SLOTS, D_MODEL, EXPERT_FF), F32), pltpu.VMEM((W_SLOTS, D_MODEL, EXPERT_FF), F32),
                            pltpu.VMEM((W_SLOTS, EXPERT_FF, D_MODEL), F32),
                            pltpu.VMEM((D_MODEL, EXPERT_FF), BF16), pltpu.VMEM((D_MODEL, EXPERT_FF), BF16),
                            pltpu.VMEM((EXPERT_FF, D_MODEL), BF16),
                            pltpu.SemaphoreType.DMA((X_SLOTS,)), pltpu.SemaphoreType.DMA((Y_SLOTS,)),
                            pltpu.SemaphoreType.DMA((W_SLOTS,))],
        ),
        out_shape=SDS((n_rows * ROW_CHUNKS, LANES), U32),
        compiler_params=_cparams("arbitrary"),
        name="experts",
    )(tile_start, tile_end, tile_rows, n_valid, w_gate, w_up, w_down, xs)


COMB_SUB = 32


def _combine_kernel(wts_ref, h1_ref, g_ref, wsg_ref, wsu_ref, wsd_ref, g2_ref, b2_ref, o_ref, routed_ref):
    tn = h1_ref.shape[0]
    for s0 in range(0, tn, COMB_SUB):
        acc = [jnp.zeros((COMB_SUB, LANES), F32) for _ in range(2 * ROW_CHUNKS)]
        for kk in range(TOP_K):
            wk = jnp.broadcast_to(wts_ref[s0:s0 + COMB_SUB, kk:kk + 1], (COMB_SUB, LANES))
            for cc in range(ROW_CHUNKS):
                lo, hi = _unpack_rows(g_ref[kk, pl.ds(s0 * ROW_CHUNKS + cc, COMB_SUB, stride=ROW_CHUNKS), :])
                acc[cc] = acc[cc] + wk * lo
                acc[ROW_CHUNKS + cc] = acc[ROW_CHUNKS + cc] + wk * hi
        routed_ref[s0:s0 + COMB_SUB, :] = jnp.concatenate(acc, axis=1)

    h1 = h1_ref[...]
    hb = h1.astype(BF16)
    sg = jnp.dot(hb, wsg_ref[...], preferred_element_type=F32)
    su = jnp.dot(hb, wsu_ref[...], preferred_element_type=F32)
    ff = jnp.dot((sg * _sigmoid(sg) * su).astype(BF16), wsd_ref[...], preferred_element_type=F32)
    o_ref[...] = _layer_norm(ALPHA * h1 + ff + routed_ref[...], g2_ref[...], b2_ref[...])


def _combine_call(wts_t, h1, gathered, wsg, wsu, wsd, g2, b2):
    t = h1.shape[0]
    tn = TN_COMB
    row = lambda i: (i, 0)
    fix = lambda i: (0, 0)
    return pl.pallas_call(
        _combine_kernel,
        grid=(t // tn,),
        in_specs=[pl.BlockSpec((tn, TOP_K), row),
                  pl.BlockSpec((tn, D_MODEL), row),
                  pl.BlockSpec((TOP_K, tn * ROW_CHUNKS, LANES), lambda i: (0, i, 0)),
                  pl.BlockSpec((D_MODEL, SHARED_FF), fix), pl.BlockSpec((D_MODEL, SHARED_FF), fix),
                  pl.BlockSpec((SHARED_FF, D_MODEL), fix),
                  pl.BlockSpec((1, D_MODEL), fix), pl.BlockSpec((1, D_MODEL), fix)],
        out_specs=pl.BlockSpec((tn, D_MODEL), row),
        out_shape=SDS((t, D_MODEL), F32),
        scratch_shapes=[pltpu.VMEM((tn, D_MODEL), F32)],
        compiler_params=_cparams("arbitrary"),
        name="combine_ln2",
    )(wts_t, h1, gathered, wsg, wsu, wsd, g2, b2)


def kernel(x, meta_tokens, ln_in_g, ln_in_b, rel_bias, w_in, conv_w, conv_b, conv_ln_g, conv_ln_b, sinks,
           w_out, ln1_g, ln1_b, w_router, router_bias, w_gate, w_up, w_down, ws_gate, ws_up, ws_down,
           ln2_g, ln2_b):
    nbatch, seq, d = x.shape
    t = nbatch * seq
    assert d == D_MODEL and w_in.shape[0] == DEPTH
    assert seq % (ATTN_QBLOCKS * BLOCK) == 0 and seq % T_CONV == 0
    assert all(t % tile == 0 for tile in (TQ_PROJ, TQ_MIX, TN_ROUTE, TN_COMB))
    x2d = x.reshape(t, D_MODEL)
    vec = lambda a: a.reshape(1, -1).astype(F32)
    gin, bin_ = vec(ln_in_g), vec(ln_in_b)
    w_in_b = w_in[0].astype(BF16)

    q, k, v, u = _proj_call(x2d, gin, bin_, w_in_b, TQ_PROJ)
    meta_blk = jnp.concatenate([jnp.zeros((PAD_FRONT, D_MODEL), F32), meta_tokens.astype(F32)], axis=0)
    _, k_meta, v_meta, u_meta = _proj_call(meta_blk, gin, bin_, w_in_b, BLOCK)

    u_halo = jnp.concatenate([jnp.zeros((CONV_HALO - N_META, CONV_CH), F32), u_meta[PAD_FRONT:]], axis=0)
    attn, conv = _attn_conv_call(q, k, v, k_meta, v_meta, _rel_bias_table(rel_bias), sinks[0].astype(F32),
                                 u, u_halo, conv_w[0].astype(F32), vec(conv_b[0]), vec(conv_ln_g[0]),
                                 vec(conv_ln_b[0]), nbatch, seq)

    w_out_b = w_out[0].astype(BF16)
    h1, h1rows, logits = _mix_call(x2d, attn, conv, gin, bin_, w_out_b[:ATTN_W], w_out_b[ATTN_W:],
                                   vec(ln1_g[0]), vec(ln1_b[0]), w_router[0].astype(BF16).T)

    idx, wts_t, rank, cnt = _route_call(logits, router_bias[0].astype(F32).reshape(N_EXPERTS, 1))

    tm = TM_EXP
    n_tiles = (t * TOP_K) // tm + N_EXPERTS
    counts = cnt[:, 0]
    tiles_e = (counts + tm - 1) // tm
    tile_end = jnp.cumsum(tiles_e).astype(I32)
    tile_start = (tile_end - tiles_e).astype(I32)
    offs = tile_start * tm
    tile_id = jnp.arange(n_tiles, dtype=I32)
    lo = jnp.maximum(tile_id[:, None] * tm, offs[None, :])
    hi = jnp.minimum((tile_id[:, None] + 1) * tm, (offs + counts)[None, :])
    tile_rows = jnp.sum(jnp.clip(hi - lo, 0, tm), axis=1).astype(I32)
    n_valid = tile_end[-1:]

    xs, dest = _sc_dispatch_call(idx, rank, offs, h1rows.reshape(t, ROW_CHUNKS, LANES), n_tiles * tm)
    xs = xs.reshape(n_tiles * tm * ROW_CHUNKS, LANES)
    ys = _expert_call(tile_start, tile_end, tile_rows, n_valid, xs, w_gate[0], w_up[0], w_down[0], n_tiles * tm)
    gathered = _sc_gather_call(dest, ys.reshape(n_tiles * tm, ROW_CHUNKS, LANES))
    gathered = gathered.reshape(TOP_K, t * ROW_CHUNKS, LANES)
    out = _combine_call(wts_t, h1, gathered, ws_gate[0].astype(BF16), ws_up[0].astype(BF16),
                        ws_down[0].astype(BF16), vec(ln2_g[0]), vec(ln2_b[0]))
    return out.reshape(nbatch, seq, D_MODEL)
```

```python
import functools
import math

import numpy as np
import jax
import jax.numpy as jnp
from jax import lax
from jax.experimental import pallas as pl
from jax.experimental.pallas import tpu as pltpu
from jax.experimental.pallas import tpu_sc as plsc

F32 = jnp.float32
BF16 = jnp.bfloat16
I32 = jnp.int32
U32 = jnp.uint32
SDS = jax.ShapeDtypeStruct

D_MODEL = 1024
HALF = D_MODEL // 2
LANES = 128
SUBLANES = 8
ROW_CHUNKS = HALF // LANES
N_META = 16
HEAD_DIM = 64
N_Q_HEADS = 8
N_KV_HEADS = 2
GQA_GROUP = N_Q_HEADS // N_KV_HEADS
ATTN_W = N_Q_HEADS * HEAD_DIM
KV_W = N_KV_HEADS * HEAD_DIM
WINDOW = 128
BLOCK = 128
CONV_CH = D_MODEL - ATTN_W
CONV_K = 31
IN_W = ATTN_W + 2 * KV_W + 2 * CONV_CH
NUM_BUCKETS = 32
MAX_EXACT = NUM_BUCKETS // 2
REL_MAX_DIST = 128
N_EXPERTS = 256
TOP_K = 8
N_GROUPS = 8
GROUP_SIZE = N_EXPERTS // N_GROUPS
TOPK_GROUPS = 4
EXPERT_FF = 256
SHARED_FF = 256
ROUTED_SCALE = 2.5
DEPTH = 1
ALPHA = (2.0 * DEPTH) ** 0.25
LN_EPS = 1e-5
NEG = -1e30
PAD_FRONT = (-N_META) % BLOCK

VMEM_LIMIT = 48 * 1024 * 1024

TQ_PROJ = 1024
PROJ_CHAINS = 4
ATTN_QBLOCKS = 2
T_CONV = 256
CONV_HALO = 32
R_CONV = 64
TQ_MIX = 1024
MIX_CHAINS = 4
TN_ROUTE = 512
ROUTE_CHAINS = 2
TM_EXP = 256
TM_PART = 64
X_SLOTS = 8
X_AHEAD = 4
Y_SLOTS = 4
W_SLOTS = 3
W_AHEAD = 2
TN_COMB = 512


def _cparams(*sem):
    return pltpu.CompilerParams(dimension_semantics=sem, vmem_limit_bytes=VMEM_LIMIT)


def _layer_norm(x, g, b):
    mu = jnp.mean(x, axis=-1, keepdims=True)
    xc = x - mu
    var = jnp.mean(xc * xc, axis=-1, keepdims=True)
    return xc * lax.rsqrt(var + LN_EPS) * g + b


def _sigmoid(x):
    return 1.0 / (1.0 + jnp.exp(-x))


def _pack_rows(lo_half, hi_half):
    lo = lax.bitcast_convert_type(lo_half.astype(BF16).astype(F32), U32)
    hi = lax.bitcast_convert_type(hi_half.astype(BF16).astype(F32), U32)
    return lax.shift_right_logical(lo, jnp.uint32(16)) | hi


def _unpack_rows(p):
    lo = lax.bitcast_convert_type(lax.shift_left(p, jnp.uint32(16)), F32)
    hi = lax.bitcast_convert_type(p & jnp.uint32(0xFFFF0000), F32)
    return lo, hi


def _chunk_index(start, j, n, lead):
    rows = pl.ds(start + j, n, stride=ROW_CHUNKS)
    return (rows, slice(None)) if lead is None else (lead, rows, slice(None))


def _store_packed(ref, start, n, packed, lead=None):
    for j in range(ROW_CHUNKS):
        ref[_chunk_index(start, j, n, lead)] = packed[:, j * LANES:(j + 1) * LANES]


def _load_packed_bf16(ref, start, n, lead=None):
    halves = [_unpack_rows(ref[_chunk_index(start, j, n, lead)]) for j in range(ROW_CHUNKS)]
    return jnp.concatenate([h[0] for h in halves] + [h[1] for h in halves], axis=1).astype(BF16)


def _proj_kernel(chains, x_ref, g_ref, b_ref, w_ref, q_ref, k_ref, v_ref, u_ref):
    rows = x_ref.shape[0] // chains
    for c in range(chains):
        r = slice(c * rows, (c + 1) * rows)
        h = _layer_norm(x_ref[r, :], g_ref[...], b_ref[...])
        p = jnp.dot(h.astype(BF16), w_ref[...], preferred_element_type=F32)
        q_ref[r, :] = (p[:, :ATTN_W] * (HEAD_DIM ** -0.5)).astype(BF16)
        k_ref[r, :] = p[:, ATTN_W:ATTN_W + KV_W].astype(BF16)
        v_ref[r, :] = p[:, ATTN_W + KV_W:ATTN_W + 2 * KV_W].astype(BF16)
        a = p[:, ATTN_W + 2 * KV_W:ATTN_W + 2 * KV_W + CONV_CH]
        gate = p[:, ATTN_W + 2 * KV_W + CONV_CH:]
        u_ref[r, :] = a * _sigmoid(gate)


def _proj_call(x2d, gin, bin_, w_in_b, tq):
    t = x2d.shape[0]
    row = lambda i: (i, 0)
    fix = lambda i: (0, 0)
    chains = PROJ_CHAINS if tq % (PROJ_CHAINS * BLOCK) == 0 else 1
    return pl.pallas_call(
        functools.partial(_proj_kernel, chains),
        grid=(t // tq,),
        in_specs=[pl.BlockSpec((tq, D_MODEL), row), pl.BlockSpec((1, D_MODEL), fix),
                  pl.BlockSpec((1, D_MODEL), fix), pl.BlockSpec((D_MODEL, IN_W), fix)],
        out_specs=[pl.BlockSpec((tq, ATTN_W), row), pl.BlockSpec((tq, KV_W), row),
                   pl.BlockSpec((tq, KV_W), row), pl.BlockSpec((tq, CONV_CH), row)],
        out_shape=[SDS((t, ATTN_W), BF16), SDS((t, KV_W), BF16), SDS((t, KV_W), BF16), SDS((t, CONV_CH), F32)],
        compiler_params=_cparams("arbitrary"),
        name="ln_in_proj",
    )(x2d, gin, bin_, w_in_b)


def _attn_stages(sinks_ref, q_ref, kc_ref, kp_ref, vc_ref, vp_ref, km_ref, vm_ref, bias_ref, o_ref):
    first = pl.program_id(1) == 0
    kp = jnp.where(first, km_ref[...], kp_ref[...])
    vp = jnp.where(first, vm_ref[...], vp_ref[...])
    k = jnp.concatenate([kp, kc_ref[...]], axis=0)
    v = jnp.concatenate([vp, vc_ref[...]], axis=0)
    col = lax.broadcasted_iota(I32, (BLOCK, 2 * BLOCK), 1)
    pad_bias = jnp.where(jnp.logical_and(first, col < PAD_FRONT), NEG, 0.0).astype(F32)
    def block(a):
        q = q_ref[a * BLOCK:(a + 1) * BLOCK, :]
        kw = k[a * BLOCK:(a + 2) * BLOCK, :]
        vw = v[a * BLOCK:(a + 2) * BLOCK, :]
        outs = []
        for h in range(N_Q_HEADS):
            g = h // GQA_GROUP
            qh = q[:, h * HEAD_DIM:(h + 1) * HEAD_DIM]
            kg = kw[:, g * HEAD_DIM:(g + 1) * HEAD_DIM]
            vg = vw[:, g * HEAD_DIM:(g + 1) * HEAD_DIM]
            s = lax.dot_general(qh, kg, (((1,), (1,)), ((), ())), preferred_element_type=F32)
            s = s + bias_ref[h]
            if a == 0:
                s = s + pad_bias
            sink = sinks_ref[h]
            m = jnp.maximum(jnp.max(s, axis=-1, keepdims=True), sink)
            p = jnp.exp(s - m)
            den = jnp.sum(p, axis=-1, keepdims=True) + jnp.exp(sink - m)
            o = jnp.dot(p.astype(BF16), vg, preferred_element_type=F32)
            outs.append(o / den)
        o_ref[a * BLOCK:(a + 1) * BLOCK, :] = jnp.concatenate(outs, axis=1).astype(BF16)

    return [functools.partial(block, a) for a in range(ATTN_QBLOCKS)]


def _rel_bias_table(rel_bias):
    qi = np.arange(BLOCK, dtype=np.int32)[:, None]
    kj = np.arange(2 * BLOCK, dtype=np.int32)[None, :]
    dist = BLOCK + qi - kj
    dc = np.clip(dist, 0, WINDOW - 1)
    nf = np.maximum(dc, 1).astype(np.float32)
    large = MAX_EXACT + (np.log(nf / np.float32(MAX_EXACT)) / np.float32(math.log(REL_MAX_DIST / MAX_EXACT))
                         * np.float32(NUM_BUCKETS - MAX_EXACT)).astype(np.int32)
    large = np.minimum(large, NUM_BUCKETS - 1)
    bucket = np.where(dc < MAX_EXACT, dc, large)
    in_window = (dist >= 0) & (dist < WINDOW)
    onehot = (bucket.reshape(-1, 1) == np.arange(NUM_BUCKETS)[None, :]).astype(np.float32)
    bias = jnp.dot(jnp.asarray(onehot), rel_bias.astype(F32), precision=lax.Precision.HIGHEST)
    bias = jnp.transpose(bias.reshape(BLOCK, 2 * BLOCK, N_Q_HEADS), (2, 0, 1))
    return jnp.where(in_window[None], bias, NEG)


def _conv_stages(uc_ref, up_ref, um_ref, w_ref, cb_ref, g_ref, b_ref, o_ref, s_ref, sh_ref):
    first = pl.program_id(1) == 0
    s_ref[0:CONV_HALO, :] = jnp.where(first, um_ref[...], up_ref[...])
    s_ref[CONV_HALO:CONV_HALO + T_CONV, :] = uc_ref[...]
    off = CONV_HALO - (CONV_K - 1)
    span = sh_ref.shape[1]
    for p in range(1, SUBLANES):
        sh_ref[p] = s_ref[p:p + span, :]
    def chunk(c):
        acc = jnp.zeros((R_CONV, CONV_CH), F32) + cb_ref[...]
        for kk in range(CONV_K):
            p, a = (off + kk) % SUBLANES, (off + kk) // SUBLANES * SUBLANES
            if p == 0:
                win = s_ref[c + a:c + a + R_CONV, :]
            else:
                win = sh_ref[p, c + a:c + a + R_CONV, :]
            acc = acc + win * w_ref[kk:kk + 1, :]
        y = _layer_norm(acc, g_ref[...], b_ref[...])
        o_ref[c:c + R_CONV, :] = (y * _sigmoid(y)).astype(BF16)

    return [functools.partial(chunk, c) for c in range(0, T_CONV, R_CONV)]


def _attn_conv_kernel(sinks_ref, q_ref, kc_ref, kp_ref, vc_ref, vp_ref, km_ref, vm_ref, bias_ref,
                      uc_ref, up_ref, um_ref, w_ref, cb_ref, g_ref, b_ref, ao_ref, co_ref, s_ref, sh_ref):
    conv = _conv_stages(uc_ref, up_ref, um_ref, w_ref, cb_ref, g_ref, b_ref, co_ref, s_ref, sh_ref)
    attn = _attn_stages(sinks_ref, q_ref, kc_ref, kp_ref, vc_ref, vp_ref, km_ref, vm_ref, bias_ref, ao_ref)
    per = len(conv) // len(attn)
    for a, attn_block in enumerate(attn):
        for chunk in conv[a * per:(a + 1) * per]:
            chunk()
        attn_block()


def _attn_conv_call(q, k, v, k_meta, v_meta, bias, sinks, u, u_meta_halo, conv_w, conv_b, g, b, nbatch, seq):
    t = q.shape[0]
    rows = ATTN_QBLOCKS * BLOCK
    assert rows == T_CONV
    nstep = seq // rows
    per_blk = rows // BLOCK
    per_halo = rows // CONV_HALO
    cur = lambda bb, j: (bb * nstep + j, 0)
    prev_blk = lambda bb, j: (jnp.maximum((bb * nstep + j) * per_blk - 1, 0), 0)
    prev_halo = lambda bb, j: (jnp.maximum((bb * nstep + j) * per_halo - 1, 0), 0)
    fix = lambda bb, j: (0, 0)
    return pl.pallas_call(
        _attn_conv_kernel,
        grid=(nbatch, nstep),
        in_specs=[pl.BlockSpec(memory_space=pltpu.SMEM),
                  pl.BlockSpec((rows, ATTN_W), cur),
                  pl.BlockSpec((rows, KV_W), cur), pl.BlockSpec((BLOCK, KV_W), prev_blk),
                  pl.BlockSpec((rows, KV_W), cur), pl.BlockSpec((BLOCK, KV_W), prev_blk),
                  pl.BlockSpec((BLOCK, KV_W), fix), pl.BlockSpec((BLOCK, KV_W), fix),
                  pl.BlockSpec((N_Q_HEADS, BLOCK, 2 * BLOCK), lambda bb, j: (0, 0, 0)),
                  pl.BlockSpec((rows, CONV_CH), cur), pl.BlockSpec((CONV_HALO, CONV_CH), prev_halo),
                  pl.BlockSpec((CONV_HALO, CONV_CH), fix), pl.BlockSpec((CONV_K, CONV_CH), fix),
                  pl.BlockSpec((1, CONV_CH), fix), pl.BlockSpec((1, CONV_CH), fix), pl.BlockSpec((1, CONV_CH), fix)],
        out_specs=[pl.BlockSpec((rows, ATTN_W), cur), pl.BlockSpec((rows, CONV_CH), cur)],
        out_shape=[SDS((t, ATTN_W), BF16), SDS((t, CONV_CH), BF16)],
        scratch_shapes=[pltpu.VMEM((CONV_HALO + T_CONV, CONV_CH), F32),
                        pltpu.VMEM((SUBLANES, T_CONV + CONV_HALO - SUBLANES, CONV_CH), F32)],
        compiler_params=_cparams("arbitrary", "arbitrary"),
        name="attn_conv",
    )(sinks, q, k, k, v, v, k_meta, v_meta, bias, u, u, u_meta_halo, conv_w, conv_b, g, b)


def _mix_kernel(x_ref, at_ref, cv_ref, gin_ref, bin_ref, woa_ref, woc_ref, g1_ref, b1_ref,
                wr_ref, h1_ref, h1r_ref, lg_ref):
    rows = x_ref.shape[0] // MIX_CHAINS
    nt = (((1,), (1,)), ((), ()))
    for c in range(MIX_CHAINS):
        r = slice(c * rows, (c + 1) * rows)
        h = _layer_norm(x_ref[r, :], gin_ref[...], bin_ref[...])
        mix = (jnp.dot(at_ref[r, :], woa_ref[...], preferred_element_type=F32)
               + jnp.dot(cv_ref[r, :], woc_ref[...], preferred_element_type=F32))
        h1 = _layer_norm(ALPHA * h + mix, g1_ref[...], b1_ref[...])
        h1_ref[r, :] = h1
        _store_packed(h1r_ref, c * rows * ROW_CHUNKS, rows, _pack_rows(h1[:, :HALF], h1[:, HALF:]))
        lg_ref[:, r] = lax.dot_general(wr_ref[...], h1.astype(BF16), nt, preferred_element_type=F32)


def _mix_call(x2d, attn, conv, gin, bin_, woa, woc, g1, b1, wr):
    t = x2d.shape[0]
    tq = TQ_MIX
    row = lambda i: (i, 0)
    fix = lambda i: (0, 0)
    return pl.pallas_call(
        _mix_kernel,
        grid=(t // tq,),
        in_specs=[pl.BlockSpec((tq, D_MODEL), row), pl.BlockSpec((tq, ATTN_W), row), pl.BlockSpec((tq, CONV_CH), row),
                  pl.BlockSpec((1, D_MODEL), fix), pl.BlockSpec((1, D_MODEL), fix),
                  pl.BlockSpec((ATTN_W, D_MODEL), fix), pl.BlockSpec((CONV_CH, D_MODEL), fix),
                  pl.BlockSpec((1, D_MODEL), fix), pl.BlockSpec((1, D_MODEL), fix),
                  pl.BlockSpec((N_EXPERTS, D_MODEL), fix)],
        out_specs=[pl.BlockSpec((tq, D_MODEL), row), pl.BlockSpec((tq * ROW_CHUNKS, LANES), row),
                   pl.BlockSpec((N_EXPERTS, tq), lambda i: (0, i))],
        out_shape=[SDS((t, D_MODEL), F32), SDS((t * ROW_CHUNKS, LANES), U32), SDS((N_EXPERTS, t), F32)],
        compiler_params=_cparams("arbitrary"),
        name="mix_ln1",
    )(x2d, attn, conv, gin, bin_, woa, woc, g1, b1, wr)


def _row_numbers(nrows, n):
    return lax.broadcasted_iota(I32, (nrows, n), 0).astype(F32)


def _first_argmax(x, rows, nrows):
    m = jnp.max(x, axis=0, keepdims=True)
    idx = jnp.min(jnp.where(x == m, rows, float(nrows)), axis=0, keepdims=True)
    return m, idx


def _route_tile(logits, rbias, carry):
    tn = logits.shape[1]
    scores = _sigmoid(logits)
    choice = scores + rbias
    rows_g = _row_numbers(GROUP_SIZE, tn)
    rows_8 = _row_numbers(N_GROUPS, tn)

    gs = []
    for g in range(N_GROUPS):
        xg = choice[g * GROUP_SIZE:(g + 1) * GROUP_SIZE, :]
        m1, i1 = _first_argmax(xg, rows_g, GROUP_SIZE)
        m2 = jnp.max(jnp.where(rows_g == i1, -jnp.inf, xg), axis=0, keepdims=True)
        gs.append(m1 + m2)
    gsc = jnp.concatenate(gs, axis=0)
    gsel = jnp.zeros((N_GROUPS, tn), F32)
    for _ in range(TOPK_GROUPS):
        _, gi = _first_argmax(gsc, rows_8, N_GROUPS)
        hit = rows_8 == gi
        gsel = jnp.where(hit, 1.0, gsel)
        gsc = jnp.where(hit, -jnp.inf, gsc)

    spread = N_GROUPS - TOPK_GROUPS
    g_on = [gsel[g:g + 1, :] > 0.5 for g in range(N_GROUPS)]
    pos, n_before = [], jnp.zeros((1, tn), F32)
    for g in range(N_GROUPS):
        pos.append(n_before)
        n_before = n_before + gsel[g:g + 1, :]
    in_slot = [{g: jnp.logical_and(g_on[g], pos[g] == float(j)) for g in range(j, j + spread + 1)}
               for j in range(TOPK_GROUPS)]

    def group_rows(x, g):
        return x[g * GROUP_SIZE:(g + 1) * GROUP_SIZE, :]

    def compact(x):
        slots = []
        for j in range(TOPK_GROUPS):
            v = group_rows(x, j + spread)
            for g in range(j + spread - 1, j - 1, -1):
                v = jnp.where(in_slot[j][g], group_rows(x, g), v)
            slots.append(v)
        return jnp.concatenate(slots, axis=0)

    n_c = TOPK_GROUPS * GROUP_SIZE
    rows_c = _row_numbers(n_c, tn)
    scores_c = compact(scores)
    masked = jnp.maximum(compact(choice), float(jnp.finfo(F32).min))
    shift = []
    for j in range(TOPK_GROUPS):
        s = jnp.full((1, tn), float((j + spread) * GROUP_SIZE), F32)
        for g in range(j + spread - 1, j - 1, -1):
            s = jnp.where(in_slot[j][g], float(g * GROUP_SIZE), s)
        shift.append(s - float(j * GROUP_SIZE))

    hits, idxs, ws = [], [], []
    for _ in range(TOP_K):
        _, ii = _first_argmax(masked, rows_c, n_c)
        hit = rows_c == ii
        hits.append(hit)
        e = ii + shift[TOPK_GROUPS - 1]
        for j in range(TOPK_GROUPS - 2, -1, -1):
            e = jnp.where(ii < float((j + 1) * GROUP_SIZE), ii + shift[j], e)
        idxs.append(e)
        ws.append(jnp.sum(jnp.where(hit, scores_c, 0.0), axis=0, keepdims=True))
        masked = jnp.where(hit, -jnp.inf, masked)
    wsum = ws[0]
    for w in ws[1:]:
        wsum = wsum + w
    idx = jnp.concatenate(idxs, axis=0).astype(I32)
    wts = jnp.concatenate([w / wsum * ROUTED_SCALE for w in ws], axis=0)

    picked_c = jnp.where(masked == -jnp.inf, 1.0, 0.0)
    sel = []
    for g in range(N_GROUPS):
        v = jnp.zeros((GROUP_SIZE, tn), F32)
        for j in range(max(0, g - spread), min(TOPK_GROUPS - 1, g) + 1):
            v = jnp.where(in_slot[j][g], group_rows(picked_c, j), v)
        sel.append(v)
    sel_b = jnp.concatenate(sel, axis=0).astype(BF16)

    r_i = lax.broadcasted_iota(I32, (tn, tn), 0)
    c_i = lax.broadcasted_iota(I32, (tn, tn), 1)
    upper = jnp.where(r_i < c_i, 1.0, 0.0).astype(BF16)
    before = jnp.dot(sel_b, upper, preferred_element_type=F32)
    before = compact(before + jnp.concatenate([carry] * (tn // LANES), axis=1))
    rank = jnp.concatenate(
        [jnp.sum(jnp.where(h, before, 0.0), axis=0, keepdims=True) for h in hits], axis=0).astype(I32)
    carry = carry + jnp.dot(sel_b, jnp.ones((tn, LANES), BF16), preferred_element_type=F32)
    return idx, wts, rank, carry


def _route_kernel(lg_ref, rb_ref, idx_ref, wts_ref, rank_ref, cnt_ref, carry_ref):
    @pl.when(pl.program_id(0) == 0)
    def _():
        carry_ref[...] = jnp.zeros_like(carry_ref)

    carry = carry_ref[...]
    tn = lg_ref.shape[1] // ROUTE_CHAINS
    for c in range(ROUTE_CHAINS):
        cols = slice(c * tn, (c + 1) * tn)
        idx, wts, rank, carry = _route_tile(lg_ref[:, cols], rb_ref[...], carry)
        idx_ref[:, cols] = idx
        wts_ref[cols, :] = wts.T
        rank_ref[:, cols] = rank
    carry_ref[...] = carry
    cnt_ref[...] = carry.astype(I32)


def _route_call(lg, rbias):
    t = lg.shape[1]
    tn = TN_ROUTE
    col = lambda i: (0, i)
    return pl.pallas_call(
        _route_kernel,
        grid=(t // tn,),
        in_specs=[pl.BlockSpec((N_EXPERTS, tn), col), pl.BlockSpec((N_EXPERTS, 1), lambda i: (0, 0))],
        out_specs=[pl.BlockSpec((TOP_K, tn), col), pl.BlockSpec((tn, TOP_K), lambda i: (i, 0)),
                   pl.BlockSpec((TOP_K, tn), col), pl.BlockSpec((N_EXPERTS, LANES), lambda i: (0, 0))],
        out_shape=[SDS((TOP_K, t), I32), SDS((t, TOP_K), F32), SDS((TOP_K, t), I32), SDS((N_EXPERTS, LANES), I32)],
        scratch_shapes=[pltpu.VMEM((N_EXPERTS, LANES), F32)],
        compiler_params=_cparams("arbitrary"),
        name="route",
    )(lg, rbias)


SC_CORES = 2
SC_SUBCORES = 16
SC_CHUNK = 128
SC_LANES = 16
SC_BUFS = 2


def _sc_worker_chunks(t):
    per_worker = t // (SC_CORES * SC_SUBCORES)
    assert per_worker % SC_CHUNK == 0
    return per_worker


def _sc_dispatch_call(idx, rank, offs, h1rows3, n_rows):
    t = idx.shape[1]
    per_worker = _sc_worker_chunks(t)
    mesh = plsc.VectorSubcoreMesh(core_axis_name="c", subcore_axis_name="s")

    @functools.partial(
        pl.kernel, mesh=mesh, out_type=[SDS((n_rows, ROW_CHUNKS, LANES), U32), SDS((TOP_K, t), I32)],
        scratch_types=[pltpu.VMEM((TOP_K, SC_CHUNK), I32), pltpu.VMEM((TOP_K, SC_CHUNK), I32),
                       pltpu.VMEM((N_EXPERTS,), I32), pltpu.VMEM((SC_CHUNK, ROW_CHUNKS, LANES), U32),
                       pltpu.SemaphoreType.DMA],
        compiler_params=pltpu.CompilerParams(needs_layout_passes=False),
        name="sc_dispatch")
    def body(h_hbm, idx_hbm, rank_hbm, offs_hbm, xs_hbm, dest_hbm, idx_v, rank_v, offs_v, rows_v, sem):
        wid = lax.axis_index("s") * SC_CORES + lax.axis_index("c")
        pltpu.sync_copy(offs_hbm, offs_v)

        @pl.loop(0, per_worker // SC_CHUNK)
        def _(i):
            t0 = wid * per_worker + i * SC_CHUNK
            pltpu.sync_copy(idx_hbm.at[:, pl.ds(t0, SC_CHUNK)], idx_v)
            pltpu.sync_copy(rank_hbm.at[:, pl.ds(t0, SC_CHUNK)], rank_v)
            pltpu.sync_copy(h_hbm.at[pl.ds(t0, SC_CHUNK)], rows_v)
            for kk in range(TOP_K):
                @pl.loop(0, SC_CHUNK // SC_LANES)
                def _(c):
                    lanes = pl.ds(c * SC_LANES, SC_LANES)
                    idx_v[kk, lanes] = plsc.load_gather(offs_v, [idx_v[kk, lanes]]) + rank_v[kk, lanes]
            pltpu.sync_copy(idx_v, dest_hbm.at[:, pl.ds(t0, SC_CHUNK)])
            copies = [pltpu.async_copy(rows_v, xs_hbm.at[idx_v.at[kk]], sem) for kk in range(TOP_K)]
            for c in copies:
                c.wait()

    return body(h1rows3, idx, rank, offs)


def _sc_gather_call(dest, ys3):
    t = dest.shape[1]
    per_worker = _sc_worker_chunks(t)
    mesh = plsc.VectorSubcoreMesh(core_axis_name="c", subcore_axis_name="s")

    half = SC_CHUNK // 2
    items = [(kk, h) for kk in range(TOP_K) for h in range(2)]

    @functools.partial(
        pl.kernel, mesh=mesh, out_type=SDS((TOP_K * t, ROW_CHUNKS, LANES), U32),
        scratch_types=[pltpu.VMEM((TOP_K, SC_CHUNK), I32), pltpu.VMEM((SC_BUFS, half, ROW_CHUNKS, LANES), U32),
                       pltpu.SemaphoreType.DMA((SC_BUFS,)), pltpu.SemaphoreType.DMA((SC_BUFS,))],
        name="sc_gather")
    def body(ys_hbm, dest_hbm, out_hbm, idx_v, rows_v, gsem, wsem):
        wid = lax.axis_index("s") * SC_CORES + lax.axis_index("c")

        @pl.loop(0, per_worker // SC_CHUNK)
        def _(i):
            t0 = wid * per_worker + i * SC_CHUNK
            pltpu.sync_copy(dest_hbm.at[:, pl.ds(t0, SC_CHUNK)], idx_v)

            def gather(j):
                kk, h = items[j]
                b = j % SC_BUFS
                return pltpu.make_async_copy(ys_hbm.at[idx_v.at[kk, pl.ds(h * half, half)]], rows_v.at[b], gsem.at[b])

            def put(j):
                kk, h = items[j]
                b = j % SC_BUFS
                return pltpu.make_async_copy(rows_v.at[b], out_hbm.at[pl.ds(kk * t + t0 + h * half, half)], wsem.at[b])

            ahead = SC_BUFS - 1
            n = len(items)
            for j in range(ahead):
                gather(j).start()
            for j in range(n):
                if j + ahead < n:
                    if j >= 1:
                        put(j - 1).wait()
                    gather(j + ahead).start()
                gather(j).wait()
                put(j).start()
            for j in range(max(n - ahead - 1, 0), n):
                put(j).wait()

    return body(ys3, dest)


def _expert_kernel(ts_ref, te_ref, tr_ref, nv_ref, wg_hbm, wu_hbm, wd_hbm, xs_hbm, ys_hbm,
                   xbuf, ybuf, wg_f, wu_f, wd_f, wg_b, wu_b, wd_b, xsem, ysem, wsem):
    e = pl.program_id(0)
    rows = xbuf.shape[1]
    tm = rows // ROW_CHUNKS
    g0, g1, nv = ts_ref[e], te_ref[e], nv_ref[0]

    def sized(g, fn):
        parts = (tr_ref[g] + TM_PART - 1) // TM_PART
        for n in range(1, tm // TM_PART + 1):
            @pl.when(parts == n)
            def _():
                fn(n * TM_PART * ROW_CHUNKS)

    def x_copy(g, part):
        s = g % X_SLOTS
        return pltpu.make_async_copy(xs_hbm.at[pl.ds(pl.multiple_of(g * rows, rows), part), :],
                                     xbuf.at[s, pl.ds(0, part), :], xsem.at[s])

    def y_copy(g, part):
        s = g % Y_SLOTS
        return pltpu.make_async_copy(ybuf.at[s, pl.ds(0, part), :],
                                     ys_hbm.at[pl.ds(pl.multiple_of(g * rows, rows), part), :], ysem.at[s])

    def w_copies(ex):
        s = ex % W_SLOTS
        return (pltpu.make_async_copy(wg_hbm.at[ex], wg_f.at[s], wsem.at[s]),
                pltpu.make_async_copy(wu_hbm.at[ex], wu_f.at[s], wsem.at[s]),
                pltpu.make_async_copy(wd_hbm.at[ex], wd_f.at[s], wsem.at[s]))

    n_exp = pl.num_programs(0)

    @pl.when(e == 0)
    def _():
        for ex in range(W_AHEAD):
            for c in w_copies(ex):
                c.start()

    @pl.when(e + W_AHEAD < n_exp)
    def _():
        for c in w_copies(e + W_AHEAD):
            c.start()

    for c in w_copies(e):
        c.wait()

    def compute_tile(g):
        x = _load_packed_bf16(xbuf, 0, tm, lead=g % X_SLOTS)
        gate = jnp.dot(x, wg_b[...], preferred_element_type=F32)
        up = jnp.dot(x, wu_b[...], preferred_element_type=F32)
        live = lax.broadcasted_iota(I32, (tm, EXPERT_FF), 0) < tr_ref[g]
        hid = jnp.where(live, gate * _sigmoid(gate) * up, 0.0).astype(BF16)
        y = jnp.dot(hid, wd_b[...], preferred_element_type=F32)
        return _pack_rows(y[:, :HALF], y[:, HALF:])

    def run_tiles(g, n):
        for r in range(n):
            sized(g + r, lambda part, t=g + r: x_copy(t, part).wait())

            @pl.when(g + r + X_AHEAD < nv)
            def _():
                sized(g + r + X_AHEAD, lambda part, t=g + r + X_AHEAD: x_copy(t, part).start(priority=1))

            @pl.when(g + r >= Y_SLOTS)
            def _():
                sized(g + r - Y_SLOTS, lambda part, t=g + r - Y_SLOTS: y_copy(t, part).wait())

        packed = [compute_tile(g + r) for r in range(n)]
        for r in range(n):
            _store_packed(ybuf, 0, tm, packed[r], lead=(g + r) % Y_SLOTS)
        for r in range(n):
            sized(g + r, lambda part, t=g + r: y_copy(t, part).start(priority=1))

    @pl.when(e == 0)
    def _():
        xbuf[...] = jnp.zeros(xbuf.shape, U32)
        for g in range(X_AHEAD):
            @pl.when(g < nv)
            def _():
                sized(g, lambda part, t=g: x_copy(t, part).start(priority=1))

    @pl.when(g1 > g0)
    def _():
        ws = e % W_SLOTS
        wg_b[...] = wg_f[ws].astype(BF16)
        wu_b[...] = wu_f[ws].astype(BF16)
        wd_b[...] = wd_f[ws].astype(BF16)
        n_tiles = g1 - g0

        def pair(p, c):
            run_tiles(g0 + 2 * p, 2)
            return c

        lax.fori_loop(0, n_tiles // 2, pair, 0)

        @pl.when(n_tiles % 2 == 1)
        def _():
            run_tiles(g1 - 1, 1)

    @pl.when(e == pl.num_programs(0) - 1)
    def _():
        for back in range(1, Y_SLOTS + 1):
            @pl.when(nv >= back)
            def _():
                sized(nv - back, lambda part, t=nv - back: y_copy(t, part).wait())


def _expert_call(tile_start, tile_end, tile_rows, n_valid, xs, w_gate, w_up, w_down, n_rows):
    tm = TM_EXP
    hbm = pl.BlockSpec(memory_space=pl.ANY)
    return pl.pallas_call(
        _expert_kernel,
        grid_spec=pltpu.PrefetchScalarGridSpec(
            num_scalar_prefetch=4,
            grid=(N_EXPERTS,),
            in_specs=[hbm, hbm, hbm, hbm],
            out_specs=hbm,
            scratch_shapes=[pltpu.VMEM((X_SLOTS, tm * ROW_CHUNKS, LANES), U32),
                            pltpu.VMEM((Y_SLOTS, tm * ROW_CHUNKS, LANES), U32),
                            pltpu.VMEM((W_SLOTS, D_MODEL, EXPERT_FF), F32), pltpu.VMEM((W_SLOTS, D_MODEL, EXPERT_FF), F32),
                            pltpu.VMEM((W_SLOTS, EXPERT_FF, D_MODEL), F32),
                            pltpu.VMEM((D_MODEL, EXPERT_FF), BF16), pltpu.VMEM((D_MODEL, EXPERT_FF), BF16),
                            pltpu.VMEM((EXPERT_FF, D_MODEL), BF16),
                            pltpu.SemaphoreType.DMA((X_SLOTS,)), pltpu.SemaphoreType.DMA((Y_SLOTS,)),
                            pltpu.SemaphoreType.DMA((W_SLOTS,))],
        ),
        out_shape=SDS((n_rows * ROW_CHUNKS, LANES), U32),
        compiler_params=_cparams("arbitrary"),
        name="experts",
    )(tile_start, tile_end, tile_rows, n_valid, w_gate, w_up, w_down, xs)


COMB_SUB = 32


def _combine_kernel(wts_ref, h1_ref, g_ref, wsg_ref, wsu_ref, wsd_ref, g2_ref, b2_ref, o_ref, routed_ref):
    tn = h1_ref.shape[0]
    for s0 in range(0, tn, COMB_SUB):
        acc = [jnp.zeros((COMB_SUB, LANES), F32) for _ in range(2 * ROW_CHUNKS)]
        for kk in range(TOP_K):
            wk = jnp.broadcast_to(wts_ref[s0:s0 + COMB_SUB, kk:kk + 1], (COMB_SUB, LANES))
            for cc in range(ROW_CHUNKS):
                lo, hi = _unpack_rows(g_ref[kk, pl.ds(s0 * ROW_CHUNKS + cc, COMB_SUB, stride=ROW_CHUNKS), :])
                acc[cc] = acc[cc] + wk * lo
                acc[ROW_CHUNKS + cc] = acc[ROW_CHUNKS + cc] + wk * hi
        routed_ref[s0:s0 + COMB_SUB, :] = jnp.concatenate(acc, axis=1)

    h1 = h1_ref[...]
    hb = h1.astype(BF16)
    sg = jnp.dot(hb, wsg_ref[...], preferred_element_type=F32)
    su = jnp.dot(hb, wsu_ref[...], preferred_element_type=F32)
    ff = jnp.dot((sg * _sigmoid(sg) * su).astype(BF16), wsd_ref[...], preferred_element_type=F32)
    o_ref[...] = _layer_norm(ALPHA * h1 + ff + routed_ref[...], g2_ref[...], b2_ref[...])


def _combine_call(wts_t, h1, gathered, wsg, wsu, wsd, g2, b2):
    t = h1.shape[0]
    tn = TN_COMB
    row = lambda i: (i, 0)
    fix = lambda i: (0, 0)
    return pl.pallas_call(
        _combine_kernel,
        grid=(t // tn,),
        in_specs=[pl.BlockSpec((tn, TOP_K), row),
                  pl.BlockSpec((tn, D_MODEL), row),
                  pl.BlockSpec((TOP_K, tn * ROW_CHUNKS, LANES), lambda i: (0, i, 0)),
                  pl.BlockSpec((D_MODEL, SHARED_FF), fix), pl.BlockSpec((D_MODEL, SHARED_FF), fix),
                  pl.BlockSpec((SHARED_FF, D_MODEL), fix),
                  pl.BlockSpec((1, D_MODEL), fix), pl.BlockSpec((1, D_MODEL), fix)],
        out_specs=pl.BlockSpec((tn, D_MODEL), row),
        out_shape=SDS((t, D_MODEL), F32),
        scratch_shapes=[pltpu.VMEM((tn, D_MODEL), F32)],
        compiler_params=_cparams("arbitrary"),
        name="combine_ln2",
    )(wts_t, h1, gathered, wsg, wsu, wsd, g2, b2)


def kernel(x, meta_tokens, ln_in_g, ln_in_b, rel_bias, w_in, conv_w, conv_b, conv_ln_g, conv_ln_b, sinks,
           w_out, ln1_g, ln1_b, w_router, router_bias, w_gate, w_up, w_down, ws_gate, ws_up, ws_down,
           ln2_g, ln2_b):
    nbatch, seq, d = x.shape
    t = nbatch * seq
    assert d == D_MODEL and w_in.shape[0] == DEPTH
    assert seq % (ATTN_QBLOCKS * BLOCK) == 0 and seq % T_CONV == 0
    assert all(t % tile == 0 for tile in (TQ_PROJ, TQ_MIX, TN_ROUTE, TN_COMB))
    x2d = x.reshape(t, D_MODEL)
    vec = lambda a: a.reshape(1, -1).astype(F32)
    gin, bin_ = vec(ln_in_g), vec(ln_in_b)
    w_in_b = w_in[0].astype(BF16)

    q, k, v, u = _proj_call(x2d, gin, bin_, w_in_b, TQ_PROJ)
    meta_blk = jnp.concatenate([jnp.zeros((PAD_FRONT, D_MODEL), F32), meta_tokens.astype(F32)], axis=0)
    _, k_meta, v_meta, u_meta = _proj_call(meta_blk, gin, bin_, w_in_b, BLOCK)

    u_halo = jnp.concatenate([jnp.zeros((CONV_HALO - N_META, CONV_CH), F32), u_meta[PAD_FRONT:]], axis=0)
    attn, conv = _attn_conv_call(q, k, v, k_meta, v_meta, _rel_bias_table(rel_bias), sinks[0].astype(F32),
                                 u, u_halo, conv_w[0].astype(F32), vec(conv_b[0]), vec(conv_ln_g[0]),
                                 vec(conv_ln_b[0]), nbatch, seq)

    w_out_b = w_out[0].astype(BF16)
    h1, h1rows, logits = _mix_call(x2d, attn, conv, gin, bin_, w_out_b[:ATTN_W], w_out_b[ATTN_W:],
                                   vec(ln1_g[0]), vec(ln1_b[0]), w_router[0].astype(BF16).T)

    idx, wts_t, rank, cnt = _route_call(logits, router_bias[0].astype(F32).reshape(N_EXPERTS, 1))

    tm = TM_EXP
    n_tiles = (t * TOP_K) // tm + N_EXPERTS
    counts = cnt[:, 0]
    tiles_e = (counts + tm - 1) // tm
    tile_end = jnp.cumsum(tiles_e).astype(I32)
    tile_start = (tile_end - tiles_e).astype(I32)
    offs = tile_start * tm
    tile_id = jnp.arange(n_tiles, dtype=I32)
    lo = jnp.maximum(tile_id[:, None] * tm, offs[None, :])
    hi = jnp.minimum((tile_id[:, None] + 1) * tm, (offs + counts)[None, :])
    tile_rows = jnp.sum(jnp.clip(hi - lo, 0, tm), axis=1).astype(I32)
    n_valid = tile_end[-1:]

    xs, dest = _sc_dispatch_call(idx, rank, offs, h1rows.reshape(t, ROW_CHUNKS, LANES), n_tiles * tm)
    xs = xs.reshape(n_tiles * tm * ROW_CHUNKS, LANES)
    ys = _expert_call(tile_start, tile_end, tile_rows, n_valid, xs, w_gate[0], w_up[0], w_down[0], n_tiles * tm)
    gathered = _sc_gather_call(dest, ys.reshape(n_tiles * tm, ROW_CHUNKS, LANES))
    gathered = gathered.reshape(TOP_K, t * ROW_CHUNKS, LANES)
    out = _combine_call(wts_t, h1, gathered, ws_gate[0].astype(BF16), ws_up[0].astype(BF16),
                        ws_down[0].astype(BF16), vec(ln2_g[0]), vec(ln2_b[0]))
    return out.reshape(nbatch, seq, D_MODEL)
```

```python
import functools
import math

import numpy as np
import jax
import jax.numpy as jnp
from jax import lax
from jax.experimental import pallas as pl
from jax.experimental.pallas import tpu as pltpu
from jax.experimental.pallas import tpu_sc as plsc

F32 = jnp.float32
BF16 = jnp.bfloat16
I32 = jnp.int32
U32 = jnp.uint32
SDS = jax.ShapeDtypeStruct

D_MODEL = 1024
HALF = D_MODEL // 2
LANES = 128
SUBLANES = 8
ROW_CHUNKS = HALF // LANES
N_META = 16
HEAD_DIM = 64
N_Q_HEADS = 8
N_KV_HEADS = 2
GQA_GROUP = N_Q_HEADS // N_KV_HEADS
ATTN_W = N_Q_HEADS * HEAD_DIM
KV_W = N_KV_HEADS * HEAD_DIM
WINDOW = 128
BLOCK = 128
CONV_CH = D_MODEL - ATTN_W
CONV_K = 31
IN_W = ATTN_W + 2 * KV_W + 2 * CONV_CH
NUM_BUCKETS = 32
MAX_EXACT = NUM_BUCKETS // 2
REL_MAX_DIST = 128
N_EXPERTS = 256
TOP_K = 8
N_GROUPS = 8
GROUP_SIZE = N_EXPERTS // N_GROUPS
TOPK_GROUPS = 4
EXPERT_FF = 256
SHARED_FF = 256
ROUTED_SCALE = 2.5
DEPTH = 1
ALPHA = (2.0 * DEPTH) ** 0.25
LN_EPS = 1e-5
NEG = -1e30
PAD_FRONT = (-N_META) % BLOCK

VMEM_LIMIT = 48 * 1024 * 1024

TQ_PROJ = 1024
PROJ_CHAINS = 4
ATTN_QBLOCKS = 2
T_CONV = 256
CONV_HALO = 32
R_CONV = 64
TQ_MIX = 1024
MIX_CHAINS = 4
TN_ROUTE = 1024
ROUTE_CHAINS = 4
TM_EXP = 256
X_SLOTS = 8
X_AHEAD = 4
Y_SLOTS = 4
W_SLOTS = 3
W_AHEAD = 2
TN_COMB = 512


def _cparams(*sem):
    return pltpu.CompilerParams(dimension_semantics=sem, vmem_limit_bytes=VMEM_LIMIT)


def _layer_norm(x, g, b):
    mu = jnp.mean(x, axis=-1, keepdims=True)
    xc = x - mu
    var = jnp.mean(xc * xc, axis=-1, keepdims=True)
    return xc * lax.rsqrt(var + LN_EPS) * g + b


def _sigmoid(x):
    return 1.0 / (1.0 + jnp.exp(-x))


def _pack_rows(lo_half, hi_half):
    lo = lax.bitcast_convert_type(lo_half.astype(BF16).astype(F32), U32)
    hi = lax.bitcast_convert_type(hi_half.astype(BF16).astype(F32), U32)
    return lax.shift_right_logical(lo, jnp.uint32(16)) | hi


def _unpack_rows(p):
    lo = lax.bitcast_convert_type(lax.shift_left(p, jnp.uint32(16)), F32)
    hi = lax.bitcast_convert_type(p & jnp.uint32(0xFFFF0000), F32)
    return lo, hi


def _chunk_index(start, j, n, lead):
    rows = pl.ds(start + j, n, stride=ROW_CHUNKS)
    return (rows, slice(None)) if lead is None else (lead, rows, slice(None))


def _store_packed(ref, start, n, packed, lead=None):
    for j in range(ROW_CHUNKS):
        ref[_chunk_index(start, j, n, lead)] = packed[:, j * LANES:(j + 1) * LANES]


def _load_packed_bf16(ref, start, n, lead=None):
    halves = [_unpack_rows(ref[_chunk_index(start, j, n, lead)]) for j in range(ROW_CHUNKS)]
    return jnp.concatenate([h[0] for h in halves] + [h[1] for h in halves], axis=1).astype(BF16)


def _proj_kernel(chains, x_ref, g_ref, b_ref, w_ref, q_ref, k_ref, v_ref, u_ref):
    rows = x_ref.shape[0] // chains
    for c in range(chains):
        r = slice(c * rows, (c + 1) * rows)
        h = _layer_norm(x_ref[r, :], g_ref[...], b_ref[...])
        p = jnp.dot(h.astype(BF16), w_ref[...], preferred_element_type=F32)
        q_ref[r, :] = (p[:, :ATTN_W] * (HEAD_DIM ** -0.5)).astype(BF16)
        k_ref[r, :] = p[:, ATTN_W:ATTN_W + KV_W].astype(BF16)
        v_ref[r, :] = p[:, ATTN_W + KV_W:ATTN_W + 2 * KV_W].astype(BF16)
        a = p[:, ATTN_W + 2 * KV_W:ATTN_W + 2 * KV_W + CONV_CH]
        gate = p[:, ATTN_W + 2 * KV_W + CONV_CH:]
        u_ref[r, :] = a * _sigmoid(gate)


def _proj_call(x2d, gin, bin_, w_in_b, tq):
    t = x2d.shape[0]
    row = lambda i: (i, 0)
    fix = lambda i: (0, 0)
    chains = PROJ_CHAINS if tq % (PROJ_CHAINS * BLOCK) == 0 else 1
    return pl.pallas_call(
        functools.partial(_proj_kernel, chains),
        grid=(t // tq,),
        in_specs=[pl.BlockSpec((tq, D_MODEL), row), pl.BlockSpec((1, D_MODEL), fix),
                  pl.BlockSpec((1, D_MODEL), fix), pl.BlockSpec((D_MODEL, IN_W), fix)],
        out_specs=[pl.BlockSpec((tq, ATTN_W), row), pl.BlockSpec((tq, KV_W), row),
                   pl.BlockSpec((tq, KV_W), row), pl.BlockSpec((tq, CONV_CH), row)],
        out_shape=[SDS((t, ATTN_W), BF16), SDS((t, KV_W), BF16), SDS((t, KV_W), BF16), SDS((t, CONV_CH), F32)],
        compiler_params=_cparams("arbitrary"),
        name="ln_in_proj",
    )(x2d, gin, bin_, w_in_b)


def _attn_stages(sinks_ref, q_ref, kc_ref, kp_ref, vc_ref, vp_ref, km_ref, vm_ref, bias_ref, o_ref):
    first = pl.program_id(1) == 0
    kp = jnp.where(first, km_ref[...], kp_ref[...])
    vp = jnp.where(first, vm_ref[...], vp_ref[...])
    k = jnp.concatenate([kp, kc_ref[...]], axis=0)
    v = jnp.concatenate([vp, vc_ref[...]], axis=0)
    col = lax.broadcasted_iota(I32, (BLOCK, 2 * BLOCK), 1)
    pad_bias = jnp.where(jnp.logical_and(first, col < PAD_FRONT), NEG, 0.0).astype(F32)
    def block(a):
        q = q_ref[a * BLOCK:(a + 1) * BLOCK, :]
        kw = k[a * BLOCK:(a + 2) * BLOCK, :]
        vw = v[a * BLOCK:(a + 2) * BLOCK, :]
        outs = []
        for h in range(N_Q_HEADS):
            g = h // GQA_GROUP
            qh = q[:, h * HEAD_DIM:(h + 1) * HEAD_DIM]
            kg = kw[:, g * HEAD_DIM:(g + 1) * HEAD_DIM]
            vg = vw[:, g * HEAD_DIM:(g + 1) * HEAD_DIM]
            s = lax.dot_general(qh, kg, (((1,), (1,)), ((), ())), preferred_element_type=F32)
            s = s + bias_ref[h]
            if a == 0:
                s = s + pad_bias
            sink = sinks_ref[h]
            m = jnp.maximum(jnp.max(s, axis=-1, keepdims=True), sink)
            p = jnp.exp(s - m)
            den = jnp.sum(p, axis=-1, keepdims=True) + jnp.exp(sink - m)
            o = jnp.dot(p.astype(BF16), vg, preferred_element_type=F32)
            outs.append(o / den)
        o_ref[a * BLOCK:(a + 1) * BLOCK, :] = jnp.concatenate(outs, axis=1).astype(BF16)

    return [functools.partial(block, a) for a in range(ATTN_QBLOCKS)]


def _rel_bias_table(rel_bias):
    qi = np.arange(BLOCK, dtype=np.int32)[:, None]
    kj = np.arange(2 * BLOCK, dtype=np.int32)[None, :]
    dist = BLOCK + qi - kj
    dc = np.clip(dist, 0, WINDOW - 1)
    nf = np.maximum(dc, 1).astype(np.float32)
    large = MAX_EXACT + (np.log(nf / np.float32(MAX_EXACT)) / np.float32(math.log(REL_MAX_DIST / MAX_EXACT))
                         * np.float32(NUM_BUCKETS - MAX_EXACT)).astype(np.int32)
    large = np.minimum(large, NUM_BUCKETS - 1)
    bucket = np.where(dc < MAX_EXACT, dc, large)
    in_window = (dist >= 0) & (dist < WINDOW)
    onehot = (bucket.reshape(-1, 1) == np.arange(NUM_BUCKETS)[None, :]).astype(np.float32)
    bias = jnp.dot(jnp.asarray(onehot), rel_bias.astype(F32), precision=lax.Precision.HIGHEST)
    bias = jnp.transpose(bias.reshape(BLOCK, 2 * BLOCK, N_Q_HEADS), (2, 0, 1))
    return jnp.where(in_window[None], bias, NEG)


def _conv_stages(uc_ref, up_ref, um_ref, w_ref, cb_ref, g_ref, b_ref, o_ref, s_ref, sh_ref):
    first = pl.program_id(1) == 0
    s_ref[0:CONV_HALO, :] = jnp.where(first, um_ref[...], up_ref[...])
    s_ref[CONV_HALO:CONV_HALO + T_CONV, :] = uc_ref[...]
    off = CONV_HALO - (CONV_K - 1)
    span = sh_ref.shape[1]
    for p in range(1, SUBLANES):
        sh_ref[p] = s_ref[p:p + span, :]
    def chunk(c):
        acc = jnp.zeros((R_CONV, CONV_CH), F32) + cb_ref[...]
        for kk in range(CONV_K):
            p, a = (off + kk) % SUBLANES, (off + kk) // SUBLANES * SUBLANES
            if p == 0:
                win = s_ref[c + a:c + a + R_CONV, :]
            else:
                win = sh_ref[p, c + a:c + a + R_CONV, :]
            acc = acc + win * w_ref[kk:kk + 1, :]
        y = _layer_norm(acc, g_ref[...], b_ref[...])
        o_ref[c:c + R_CONV, :] = (y * _sigmoid(y)).astype(BF16)

    return [functools.partial(chunk, c) for c in range(0, T_CONV, R_CONV)]


def _attn_conv_kernel(sinks_ref, q_ref, kc_ref, kp_ref, vc_ref, vp_ref, km_ref, vm_ref, bias_ref,
                      uc_ref, up_ref, um_ref, w_ref, cb_ref, g_ref, b_ref, ao_ref, co_ref, s_ref, sh_ref):
    conv = _conv_stages(uc_ref, up_ref, um_ref, w_ref, cb_ref, g_ref, b_ref, co_ref, s_ref, sh_ref)
    attn = _attn_stages(sinks_ref, q_ref, kc_ref, kp_ref, vc_ref, vp_ref, km_ref, vm_ref, bias_ref, ao_ref)
    per = len(conv) // len(attn)
    for a, attn_block in enumerate(attn):
        for chunk in conv[a * per:(a + 1) * per]:
            chunk()
        attn_block()


def _attn_conv_call(q, k, v, k_meta, v_meta, bias, sinks, u, u_meta_halo, conv_w, conv_b, g, b, nbatch, seq):
    t = q.shape[0]
    rows = ATTN_QBLOCKS * BLOCK
    assert rows == T_CONV
    nstep = seq // rows
    per_blk = rows // BLOCK
    per_halo = rows // CONV_HALO
    cur = lambda bb, j: (bb * nstep + j, 0)
    prev_blk = lambda bb, j: (jnp.maximum((bb * nstep + j) * per_blk - 1, 0), 0)
    prev_halo = lambda bb, j: (jnp.maximum((bb * nstep + j) * per_halo - 1, 0), 0)
    fix = lambda bb, j: (0, 0)
    return pl.pallas_call(
        _attn_conv_kernel,
        grid=(nbatch, nstep),
        in_specs=[pl.BlockSpec(memory_space=pltpu.SMEM),
                  pl.BlockSpec((rows, ATTN_W), cur),
                  pl.BlockSpec((rows, KV_W), cur), pl.BlockSpec((BLOCK, KV_W), prev_blk),
                  pl.BlockSpec((rows, KV_W), cur), pl.BlockSpec((BLOCK, KV_W), prev_blk),
                  pl.BlockSpec((BLOCK, KV_W), fix), pl.BlockSpec((BLOCK, KV_W), fix),
                  pl.BlockSpec((N_Q_HEADS, BLOCK, 2 * BLOCK), lambda bb, j: (0, 0, 0)),
                  pl.BlockSpec((rows, CONV_CH), cur), pl.BlockSpec((CONV_HALO, CONV_CH), prev_halo),
                  pl.BlockSpec((CONV_HALO, CONV_CH), fix), pl.BlockSpec((CONV_K, CONV_CH), fix),
                  pl.BlockSpec((1, CONV_CH), fix), pl.BlockSpec((1, CONV_CH), fix), pl.BlockSpec((1, CONV_CH), fix)],
        out_specs=[pl.BlockSpec((rows, ATTN_W), cur), pl.BlockSpec((rows, CONV_CH), cur)],
        out_shape=[SDS((t, ATTN_W), BF16), SDS((t, CONV_CH), BF16)],
        scratch_shapes=[pltpu.VMEM((CONV_HALO + T_CONV, CONV_CH), F32),
                        pltpu.VMEM((SUBLANES, T_CONV + CONV_HALO - SUBLANES, CONV_CH), F32)],
        compiler_params=_cparams("arbitrary", "arbitrary"),
        name="attn_conv",
    )(sinks, q, k, k, v, v, k_meta, v_meta, bias, u, u, u_meta_halo, conv_w, conv_b, g, b)


def _mix_kernel(x_ref, at_ref, cv_ref, gin_ref, bin_ref, woa_ref, woc_ref, g1_ref, b1_ref,
                wr_ref, h1_ref, h1r_ref, lg_ref):
    rows = x_ref.shape[0] // MIX_CHAINS
    nt = (((1,), (1,)), ((), ()))
    for c in range(MIX_CHAINS):
        r = slice(c * rows, (c + 1) * rows)
        h = _layer_norm(x_ref[r, :], gin_ref[...], bin_ref[...])
        mix = (jnp.dot(at_ref[r, :], woa_ref[...], preferred_element_type=F32)
               + jnp.dot(cv_ref[r, :], woc_ref[...], preferred_element_type=F32))
        h1 = _layer_norm(ALPHA * h + mix, g1_ref[...], b1_ref[...])
        h1_ref[r, :] = h1
        _store_packed(h1r_ref, c * rows * ROW_CHUNKS, rows, _pack_rows(h1[:, :HALF], h1[:, HALF:]))
        lg_ref[:, r] = lax.dot_general(wr_ref[...], h1.astype(BF16), nt, preferred_element_type=F32)


def _mix_call(x2d, attn, conv, gin, bin_, woa, woc, g1, b1, wr):
    t = x2d.shape[0]
    tq = TQ_MIX
    row = lambda i: (i, 0)
    fix = lambda i: (0, 0)
    return pl.pallas_call(
        _mix_kernel,
        grid=(t // tq,),
        in_specs=[pl.BlockSpec((tq, D_MODEL), row), pl.BlockSpec((tq, ATTN_W), row), pl.BlockSpec((tq, CONV_CH), row),
                  pl.BlockSpec((1, D_MODEL), fix), pl.BlockSpec((1, D_MODEL), fix),
                  pl.BlockSpec((ATTN_W, D_MODEL), fix), pl.BlockSpec((CONV_CH, D_MODEL), fix),
                  pl.BlockSpec((1, D_MODEL), fix), pl.BlockSpec((1, D_MODEL), fix),
                  pl.BlockSpec((N_EXPERTS, D_MODEL), fix)],
        out_specs=[pl.BlockSpec((tq, D_MODEL), row), pl.BlockSpec((tq * ROW_CHUNKS, LANES), row),
                   pl.BlockSpec((N_EXPERTS, tq), lambda i: (0, i))],
        out_shape=[SDS((t, D_MODEL), F32), SDS((t * ROW_CHUNKS, LANES), U32), SDS((N_EXPERTS, t), F32)],
        compiler_params=_cparams("arbitrary"),
        name="mix_ln1",
    )(x2d, attn, conv, gin, bin_, woa, woc, g1, b1, wr)


def _row_numbers(nrows, n):
    return lax.broadcasted_iota(I32, (nrows, n), 0).astype(F32)


def _first_argmax(x, rows, nrows):
    m = jnp.max(x, axis=0, keepdims=True)
    idx = jnp.min(jnp.where(x == m, rows, float(nrows)), axis=0, keepdims=True)
    return m, idx


def _route_tile(logits, rbias, carry):
    tn = logits.shape[1]
    scores = _sigmoid(logits)
    choice = scores + rbias
    rows_g = _row_numbers(GROUP_SIZE, tn)
    rows_8 = _row_numbers(N_GROUPS, tn)

    gs = []
    for g in range(N_GROUPS):
        xg = choice[g * GROUP_SIZE:(g + 1) * GROUP_SIZE, :]
        m1, i1 = _first_argmax(xg, rows_g, GROUP_SIZE)
        m2 = jnp.max(jnp.where(rows_g == i1, -jnp.inf, xg), axis=0, keepdims=True)
        gs.append(m1 + m2)
    gsc = jnp.concatenate(gs, axis=0)
    gsel = jnp.zeros((N_GROUPS, tn), F32)
    for _ in range(TOPK_GROUPS):
        _, gi = _first_argmax(gsc, rows_8, N_GROUPS)
        hit = rows_8 == gi
        gsel = jnp.where(hit, 1.0, gsel)
        gsc = jnp.where(hit, -jnp.inf, gsc)

    spread = N_GROUPS - TOPK_GROUPS
    g_on = [gsel[g:g + 1, :] > 0.5 for g in range(N_GROUPS)]
    pos, n_before = [], jnp.zeros((1, tn), F32)
    for g in range(N_GROUPS):
        pos.append(n_before)
        n_before = n_before + gsel[g:g + 1, :]
    in_slot = [{g: jnp.logical_and(g_on[g], pos[g] == float(j)) for g in range(j, j + spread + 1)}
               for j in range(TOPK_GROUPS)]

    def group_rows(x, g):
        return x[g * GROUP_SIZE:(g + 1) * GROUP_SIZE, :]

    def compact(x):
        slots = []
        for j in range(TOPK_GROUPS):
            v = group_rows(x, j + spread)
            for g in range(j + spread - 1, j - 1, -1):
                v = jnp.where(in_slot[j][g], group_rows(x, g), v)
            slots.append(v)
        return jnp.concatenate(slots, axis=0)

    n_c = TOPK_GROUPS * GROUP_SIZE
    rows_c = _row_numbers(n_c, tn)
    scores_c = compact(scores)
    masked = jnp.maximum(compact(choice), float(jnp.finfo(F32).min))
    shift = []
    for j in range(TOPK_GROUPS):
        s = jnp.full((1, tn), float((j + spread) * GROUP_SIZE), F32)
        for g in range(j + spread - 1, j - 1, -1):
            s = jnp.where(in_slot[j][g], float(g * GROUP_SIZE), s)
        shift.append(s - float(j * GROUP_SIZE))

    hits, idxs, ws = [], [], []
    for _ in range(TOP_K):
        _, ii = _first_argmax(masked, rows_c, n_c)
        hit = rows_c == ii
        hits.append(hit)
        e = ii + shift[TOPK_GROUPS - 1]
        for j in range(TOPK_GROUPS - 2, -1, -1):
            e = jnp.where(ii < float((j + 1) * GROUP_SIZE), ii + shift[j], e)
        idxs.append(e)
        ws.append(jnp.sum(jnp.where(hit, scores_c, 0.0), axis=0, keepdims=True))
        masked = jnp.where(hit, -jnp.inf, masked)
    wsum = ws[0]
    for w in ws[1:]:
        wsum = wsum + w
    idx = jnp.concatenate(idxs, axis=0).astype(I32)
    wts = jnp.concatenate([w / wsum * ROUTED_SCALE for w in ws], axis=0)

    picked_c = jnp.where(masked == -jnp.inf, 1.0, 0.0)
    sel = []
    for g in range(N_GROUPS):
        v = jnp.zeros((GROUP_SIZE, tn), F32)
        for j in range(max(0, g - spread), min(TOPK_GROUPS - 1, g) + 1):
            v = jnp.where(in_slot[j][g], group_rows(picked_c, j), v)
        sel.append(v)
    sel_b = jnp.concatenate(sel, axis=0).astype(BF16)

    r_i = lax.broadcasted_iota(I32, (tn, tn), 0)
    c_i = lax.broadcasted_iota(I32, (tn, tn), 1)
    upper = jnp.where(r_i < c_i, 1.0, 0.0).astype(BF16)
    before = jnp.dot(sel_b, upper, preferred_element_type=F32)
    before = compact(before + jnp.concatenate([carry] * (tn // LANES), axis=1))
    rank = jnp.concatenate(
        [jnp.sum(jnp.where(h, before, 0.0), axis=0, keepdims=True) for h in hits], axis=0).astype(I32)
    carry = carry + jnp.dot(sel_b, jnp.ones((tn, LANES), BF16), preferred_element_type=F32)
    return idx, wts, rank, carry


def _route_kernel(lg_ref, rb_ref, idx_ref, wts_ref, rank_ref, cnt_ref, carry_ref):
    @pl.when(pl.program_id(0) == 0)
    def _():
        carry_ref[...] = jnp.zeros_like(carry_ref)

    carry = carry_ref[...]
    tn = lg_ref.shape[1] // ROUTE_CHAINS
    for c in range(ROUTE_CHAINS):
        cols = slice(c * tn, (c + 1) * tn)
        idx, wts, rank, carry = _route_tile(lg_ref[:, cols], rb_ref[...], carry)
        idx_ref[:, cols] = idx
        wts_ref[cols, :] = wts.T
        rank_ref[:, cols] = rank
    carry_ref[...] = carry
    cnt_ref[...] = carry.astype(I32)


def _route_call(lg, rbias):
    t = lg.shape[1]
    tn = TN_ROUTE
    col = lambda i: (0, i)
    return pl.pallas_call(
        _route_kernel,
        grid=(t // tn,),
        in_specs=[pl.BlockSpec((N_EXPERTS, tn), col), pl.BlockSpec((N_EXPERTS, 1), lambda i: (0, 0))],
        out_specs=[pl.BlockSpec((TOP_K, tn), col), pl.BlockSpec((tn, TOP_K), lambda i: (i, 0)),
                   pl.BlockSpec((TOP_K, tn), col), pl.BlockSpec((N_EXPERTS, LANES), lambda i: (0, 0))],
        out_shape=[SDS((TOP_K, t), I32), SDS((t, TOP_K), F32), SDS((TOP_K, t), I32), SDS((N_EXPERTS, LANES), I32)],
        scratch_shapes=[pltpu.VMEM((N_EXPERTS, LANES), F32)],
        compiler_params=_cparams("arbitrary"),
        name="route",
    )(lg, rbias)


SC_CORES = 2
SC_SUBCORES = 16
SC_CHUNK = 128
SC_LANES = 16
SC_BUFS = 2


def _sc_worker_chunks(t):
    per_worker = t // (SC_CORES * SC_SUBCORES)
    assert per_worker % SC_CHUNK == 0
    return per_worker


def _sc_dispatch_call(idx, rank, offs, h1rows3, n_rows):
    t = idx.shape[1]
    per_worker = _sc_worker_chunks(t)
    mesh = plsc.VectorSubcoreMesh(core_axis_name="c", subcore_axis_name="s")

    @functools.partial(
        pl.kernel, mesh=mesh, out_type=[SDS((n_rows, ROW_CHUNKS, LANES), U32), SDS((TOP_K, t), I32)],
        scratch_types=[pltpu.VMEM((TOP_K, SC_CHUNK), I32), pltpu.VMEM((TOP_K, SC_CHUNK), I32),
                       pltpu.VMEM((N_EXPERTS,), I32), pltpu.VMEM((SC_CHUNK, ROW_CHUNKS, LANES), U32),
                       pltpu.SemaphoreType.DMA],
        compiler_params=pltpu.CompilerParams(needs_layout_passes=False),
        name="sc_dispatch")
    def body(h_hbm, idx_hbm, rank_hbm, offs_hbm, xs_hbm, dest_hbm, idx_v, rank_v, offs_v, rows_v, sem):
        wid = lax.axis_index("s") * SC_CORES + lax.axis_index("c")
        pltpu.sync_copy(offs_hbm, offs_v)

        @pl.loop(0, per_worker // SC_CHUNK)
        def _(i):
            t0 = wid * per_worker + i * SC_CHUNK
            pltpu.sync_copy(idx_hbm.at[:, pl.ds(t0, SC_CHUNK)], idx_v)
            pltpu.sync_copy(rank_hbm.at[:, pl.ds(t0, SC_CHUNK)], rank_v)
            pltpu.sync_copy(h_hbm.at[pl.ds(t0, SC_CHUNK)], rows_v)
            for kk in range(TOP_K):
                @pl.loop(0, SC_CHUNK // SC_LANES)
                def _(c):
                    lanes = pl.ds(c * SC_LANES, SC_LANES)
                    idx_v[kk, lanes] = plsc.load_gather(offs_v, [idx_v[kk, lanes]]) + rank_v[kk, lanes]
            pltpu.sync_copy(idx_v, dest_hbm.at[:, pl.ds(t0, SC_CHUNK)])
            copies = [pltpu.async_copy(rows_v, xs_hbm.at[idx_v.at[kk]], sem) for kk in range(TOP_K)]
            for c in copies:
                c.wait()

    return body(h1rows3, idx, rank, offs)


def _sc_gather_call(dest, ys3):
    t = dest.shape[1]
    per_worker = _sc_worker_chunks(t)
    mesh = plsc.VectorSubcoreMesh(core_axis_name="c", subcore_axis_name="s")

    half = SC_CHUNK // 2
    items = [(kk, h) for kk in range(TOP_K) for h in range(2)]

    @functools.partial(
        pl.kernel, mesh=mesh, out_type=SDS((TOP_K * t, ROW_CHUNKS, LANES), U32),
        scratch_types=[pltpu.VMEM((TOP_K, SC_CHUNK), I32), pltpu.VMEM((SC_BUFS, half, ROW_CHUNKS, LANES), U32),
                       pltpu.SemaphoreType.DMA((SC_BUFS,)), pltpu.SemaphoreType.DMA((SC_BUFS,))],
        name="sc_gather")
    def body(ys_hbm, dest_hbm, out_hbm, idx_v, rows_v, gsem, wsem):
        wid = lax.axis_index("s") * SC_CORES + lax.axis_index("c")

        @pl.loop(0, per_worker // SC_CHUNK)
        def _(i):
            t0 = wid * per_worker + i * SC_CHUNK
            pltpu.sync_copy(dest_hbm.at[:, pl.ds(t0, SC_CHUNK)], idx_v)

            def gather(j):
                kk, h = items[j]
                b = j % SC_BUFS
                return pltpu.make_async_copy(ys_hbm.at[idx_v.at[kk, pl.ds(h * half, half)]], rows_v.at[b], gsem.at[b])

            def put(j):
                kk, h = items[j]
                b = j % SC_BUFS
                return pltpu.make_async_copy(rows_v.at[b], out_hbm.at[pl.ds(kk * t + t0 + h * half, half)], wsem.at[b])

            ahead = SC_BUFS - 1
            n = len(items)
            for j in range(ahead):
                gather(j).start()
            for j in range(n):
                if j + ahead < n:
                    if j >= 1:
                        put(j - 1).wait()
                    gather(j + ahead).start()
                gather(j).wait()
                put(j).start()
            for j in range(max(n - ahead - 1, 0), n):
                put(j).wait()

    return body(ys3, dest)


def _expert_kernel(ts_ref, te_ref, tr_ref, nv_ref, wg_hbm, wu_hbm, wd_hbm, xs_hbm, ys_hbm,
                   xbuf, ybuf, wg_f, wu_f, wd_f, wg_b, wu_b, wd_b, xsem, ysem, wsem):
    e = pl.program_id(0)
    rows = xbuf.shape[1]
    tm = rows // ROW_CHUNKS
    g0, g1, nv = ts_ref[e], te_ref[e], nv_ref[0]

    def x_copy(g):
        s = g % X_SLOTS
        return pltpu.make_async_copy(xs_hbm.at[pl.ds(pl.multiple_of(g * rows, rows), rows), :], xbuf.at[s], xsem.at[s])

    def y_copy(g):
        s = g % Y_SLOTS
        return pltpu.make_async_copy(ybuf.at[s], ys_hbm.at[pl.ds(pl.multiple_of(g * rows, rows), rows), :], ysem.at[s])

    def w_copies(ex):
        s = ex % W_SLOTS
        return (pltpu.make_async_copy(wg_hbm.at[ex], wg_f.at[s], wsem.at[s]),
                pltpu.make_async_copy(wu_hbm.at[ex], wu_f.at[s], wsem.at[s]),
                pltpu.make_async_copy(wd_hbm.at[ex], wd_f.at[s], wsem.at[s]))

    n_exp = pl.num_programs(0)

    @pl.when(e == 0)
    def _():
        for ex in range(W_AHEAD):
            for c in w_copies(ex):
                c.start()

    @pl.when(e + W_AHEAD < n_exp)
    def _():
        for c in w_copies(e + W_AHEAD):
            c.start()

    for c in w_copies(e):
        c.wait()

    def compute_tile(g):
        x = _load_packed_bf16(xbuf, 0, tm, lead=g % X_SLOTS)
        gate = jnp.dot(x, wg_b[...], preferred_element_type=F32)
        up = jnp.dot(x, wu_b[...], preferred_element_type=F32)
        live = lax.broadcasted_iota(I32, (tm, EXPERT_FF), 0) < tr_ref[g]
        hid = jnp.where(live, gate * _sigmoid(gate) * up, 0.0).astype(BF16)
        y = jnp.dot(hid, wd_b[...], preferred_element_type=F32)
        return _pack_rows(y[:, :HALF], y[:, HALF:])

    def run_tiles(g, n):
        for r in range(n):
            x_copy(g + r).wait()

            @pl.when(g + r + X_AHEAD < nv)
            def _():
                x_copy(g + r + X_AHEAD).start(priority=1)

            @pl.when(g + r >= Y_SLOTS)
            def _():
                y_copy(g + r - Y_SLOTS).wait()

        packed = [compute_tile(g + r) for r in range(n)]
        for r in range(n):
            _store_packed(ybuf, 0, tm, packed[r], lead=(g + r) % Y_SLOTS)
        for r in range(n):
            y_copy(g + r).start(priority=1)

    @pl.when(e == 0)
    def _():
        for g in range(X_AHEAD):
            @pl.when(g < nv)
            def _():
                x_copy(g).start(priority=1)

    @pl.when(g1 > g0)
    def _():
        ws = e % W_SLOTS
        wg_b[...] = wg_f[ws].astype(BF16)
        wu_b[...] = wu_f[ws].astype(BF16)
        wd_b[...] = wd_f[ws].astype(BF16)
        n_tiles = g1 - g0

        def pair(p, c):
            run_tiles(g0 + 2 * p, 2)
            return c

        lax.fori_loop(0, n_tiles // 2, pair, 0)

        @pl.when(n_tiles % 2 == 1)
        def _():
            run_tiles(g1 - 1, 1)

    @pl.when(e == pl.num_programs(0) - 1)
    def _():
        for back in range(1, Y_SLOTS + 1):
            @pl.when(nv >= back)
            def _():
                y_copy(nv - back).wait()


def _expert_call(tile_start, tile_end, tile_rows, n_valid, xs, w_gate, w_up, w_down, n_rows):
    tm = TM_EXP
    hbm = pl.BlockSpec(memory_space=pl.ANY)
    return pl.pallas_call(
        _expert_kernel,
        grid_spec=pltpu.PrefetchScalarGridSpec(
            num_scalar_prefetch=4,
            grid=(N_EXPERTS,),
            in_specs=[hbm, hbm, hbm, hbm],
            out_specs=hbm,
            scratch_shapes=[pltpu.VMEM((X_SLOTS, tm * ROW_CHUNKS, LANES), U32),
                            pltpu.VMEM((Y_SLOTS, tm * ROW_CHUNKS, LANES), U32),
                            pltpu.VMEM((W_SLOTS, D_MODEL, EXPERT_FF), F32), pltpu.VMEM((W_SLOTS, D_MODEL, EXPERT_FF), F32),
                            pltpu.VMEM((W_SLOTS, EXPERT_FF, D_MODEL), F32),
                            pltpu.VMEM((D_MODEL, EXPERT_FF), BF16), pltpu.VMEM((D_MODEL, EXPERT_FF), BF16),
                            pltpu.VMEM((EXPERT_FF, D_MODEL), BF16),
                            pltpu.SemaphoreType.DMA((X_SLOTS,)), pltpu.SemaphoreType.DMA((Y_SLOTS,)),
                            pltpu.SemaphoreType.DMA((W_SLOTS,))],
        ),
        out_shape=SDS((n_rows * ROW_CHUNKS, LANES), U32),
        compiler_params=_cparams("arbitrary"),
        name="experts",
    )(tile_start, tile_end, tile_rows, n_valid, w_gate, w_up, w_down, xs)


COMB_SUB = 32


def _combine_kernel(wts_ref, h1_ref, g_ref, wsg_ref, wsu_ref, wsd_ref, g2_ref, b2_ref, o_ref, routed_ref):
    tn = h1_ref.shape[0]
    for s0 in range(0, tn, COMB_SUB):
        acc = [jnp.zeros((COMB_SUB, LANES), F32) for _ in range(2 * ROW_CHUNKS)]
        for kk in range(TOP_K):
            wk = jnp.broadcast_to(wts_ref[s0:s0 + COMB_SUB, kk:kk + 1], (COMB_SUB, LANES))
            for cc in range(ROW_CHUNKS):
                lo, hi = _unpack_rows(g_ref[kk, pl.ds(s0 * ROW_CHUNKS + cc, COMB_SUB, stride=ROW_CHUNKS), :])
                acc[cc] = acc[cc] + wk * lo
                acc[ROW_CHUNKS + cc] = acc[ROW_CHUNKS + cc] + wk * hi
        routed_ref[s0:s0 + COMB_SUB, :] = jnp.concatenate(acc, axis=1)

    h1 = h1_ref[...]
    hb = h1.astype(BF16)
    sg = jnp.dot(hb, wsg_ref[...], preferred_element_type=F32)
    su = jnp.dot(hb, wsu_ref[...], preferred_element_type=F32)
    ff = jnp.dot((sg * _sigmoid(sg) * su).astype(BF16), wsd_ref[...], preferred_element_type=F32)
    o_ref[...] = _layer_norm(ALPHA * h1 + ff + routed_ref[...], g2_ref[...], b2_ref[...])


def _combine_call(wts_t, h1, gathered, wsg, wsu, wsd, g2, b2):
    t = h1.shape[0]
    tn = TN_COMB
    row = lambda i: (i, 0)
    fix = lambda i: (0, 0)
    return pl.pallas_call(
        _combine_kernel,
        grid=(t // tn,),
        in_specs=[pl.BlockSpec((tn, TOP_K), row),
                  pl.BlockSpec((tn, D_MODEL), row),
                  pl.BlockSpec((TOP_K, tn * ROW_CHUNKS, LANES), lambda i: (0, i, 0)),
                  pl.BlockSpec((D_MODEL, SHARED_FF), fix), pl.BlockSpec((D_MODEL, SHARED_FF), fix),
                  pl.BlockSpec((SHARED_FF, D_MODEL), fix),
                  pl.BlockSpec((1, D_MODEL), fix), pl.BlockSpec((1, D_MODEL), fix)],
        out_specs=pl.BlockSpec((tn, D_MODEL), row),
        out_shape=SDS((t, D_MODEL), F32),
        scratch_shapes=[pltpu.VMEM((tn, D_MODEL), F32)],
        compiler_params=_cparams("arbitrary"),
        name="combine_ln2",
    )(wts_t, h1, gathered, wsg, wsu, wsd, g2, b2)


def kernel(x, meta_tokens, ln_in_g, ln_in_b, rel_bias, w_in, conv_w, conv_b, conv_ln_g, conv_ln_b, sinks,
           w_out, ln1_g, ln1_b, w_router, router_bias, w_gate, w_up, w_down, ws_gate, ws_up, ws_down,
           ln2_g, ln2_b):
    nbatch, seq, d = x.shape
    t = nbatch * seq
    assert d == D_MODEL and w_in.shape[0] == DEPTH
    assert seq % (ATTN_QBLOCKS * BLOCK) == 0 and seq % T_CONV == 0
    assert all(t % tile == 0 for tile in (TQ_PROJ, TQ_MIX, TN_ROUTE, TN_COMB))
    x2d = x.reshape(t, D_MODEL)
    vec = lambda a: a.reshape(1, -1).astype(F32)
    gin, bin_ = vec(ln_in_g), vec(ln_in_b)
    w_in_b = w_in[0].astype(BF16)

    q, k, v, u = _proj_call(x2d, gin, bin_, w_in_b, TQ_PROJ)
    meta_blk = jnp.concatenate([jnp.zeros((PAD_FRONT, D_MODEL), F32), meta_tokens.astype(F32)], axis=0)
    _, k_meta, v_meta, u_meta = _proj_call(meta_blk, gin, bin_, w_in_b, BLOCK)

    u_halo = jnp.concatenate([jnp.zeros((CONV_HALO - N_META, CONV_CH), F32), u_meta[PAD_FRONT:]], axis=0)
    attn, conv = _attn_conv_call(q, k, v, k_meta, v_meta, _rel_bias_table(rel_bias), sinks[0].astype(F32),
                                 u, u_halo, conv_w[0].astype(F32), vec(conv_b[0]), vec(conv_ln_g[0]),
                                 vec(conv_ln_b[0]), nbatch, seq)

    w_out_b = w_out[0].astype(BF16)
    h1, h1rows, logits = _mix_call(x2d, attn, conv, gin, bin_, w_out_b[:ATTN_W], w_out_b[ATTN_W:],
                                   vec(ln1_g[0]), vec(ln1_b[0]), w_router[0].astype(BF16).T)

    idx, wts_t, rank, cnt = _route_call(logits, router_bias[0].astype(F32).reshape(N_EXPERTS, 1))

    tm = TM_EXP
    n_tiles = (t * TOP_K) // tm + N_EXPERTS
    counts = cnt[:, 0]
    tiles_e = (counts + tm - 1) // tm
    tile_end = jnp.cumsum(tiles_e).astype(I32)
    tile_start = (tile_end - tiles_e).astype(I32)
    offs = tile_start * tm
    tile_id = jnp.arange(n_tiles, dtype=I32)
    lo = jnp.maximum(tile_id[:, None] * tm, offs[None, :])
    hi = jnp.minimum((tile_id[:, None] + 1) * tm, (offs + counts)[None, :])
    tile_rows = jnp.sum(jnp.clip(hi - lo, 0, tm), axis=1).astype(I32)
    n_valid = tile_end[-1:]

    xs, dest = _sc_dispatch_call(idx, rank, offs, h1rows.reshape(t, ROW_CHUNKS, LANES), n_tiles * tm)
    xs = xs.reshape(n_tiles * tm * ROW_CHUNKS, LANES)
    ys = _expert_call(tile_start, tile_end, tile_rows, n_valid, xs, w_gate[0], w_up[0], w_down[0], n_tiles * tm)
    gathered = _sc_gather_call(dest, ys.reshape(n_tiles * tm, ROW_CHUNKS, LANES))
    gathered = gathered.reshape(TOP_K, t * ROW_CHUNKS, LANES)
    out = _combine_call(wts_t, h1, gathered, ws_gate[0].astype(BF16), ws_up[0].astype(BF16),
                        ws_down[0].astype(BF16), vec(ln2_g[0]), vec(ln2_b[0]))
    return out.reshape(nbatch, seq, D_MODEL)
```

```python
import functools
import math

import numpy as np
import jax
import jax.numpy as jnp
from jax import lax
from jax.experimental import pallas as pl
from jax.experimental.pallas import tpu as pltpu
from jax.experimental.pallas import tpu_sc as plsc

F32 = jnp.float32
BF16 = jnp.bfloat16
I32 = jnp.int32
U32 = jnp.uint32
SDS = jax.ShapeDtypeStruct

D_MODEL = 1024
HALF = D_MODEL // 2
LANES = 128
SUBLANES = 8
ROW_CHUNKS = HALF // LANES
N_META = 16
HEAD_DIM = 64
N_Q_HEADS = 8
N_KV_HEADS = 2
GQA_GROUP = N_Q_HEADS // N_KV_HEADS
ATTN_W = N_Q_HEADS * HEAD_DIM
KV_W = N_KV_HEADS * HEAD_DIM
WINDOW = 128
BLOCK = 128
CONV_CH = D_MODEL - ATTN_W
CONV_K = 31
IN_W = ATTN_W + 2 * KV_W + 2 * CONV_CH
NUM_BUCKETS = 32
MAX_EXACT = NUM_BUCKETS // 2
REL_MAX_DIST = 128
N_EXPERTS = 256
TOP_K = 8
N_GROUPS = 8
GROUP_SIZE = N_EXPERTS // N_GROUPS
TOPK_GROUPS = 4
EXPERT_FF = 256
SHARED_FF = 256
ROUTED_SCALE = 2.5
DEPTH = 1
ALPHA = (2.0 * DEPTH) ** 0.25
LN_EPS = 1e-5
NEG = -1e30
PAD_FRONT = (-N_META) % BLOCK

VMEM_LIMIT = 48 * 1024 * 1024

TQ_PROJ = 1024
PROJ_CHAINS = 4
ATTN_QBLOCKS = 2
T_CONV = 256
CONV_HALO = 32
R_CONV = 64
TQ_MIX = 1024
MIX_CHAINS = 4
TN_ROUTE = 1024
ROUTE_CHAINS = 4
TM_EXP = 256
X_SLOTS = 8
X_AHEAD = 4
Y_SLOTS = 4
W_SLOTS = 3
W_AHEAD = 2
TN_COMB = 512


def _cparams(*sem):
    return pltpu.CompilerParams(dimension_semantics=sem, vmem_limit_bytes=VMEM_LIMIT)


def _layer_norm(x, g, b):
    mu = jnp.mean(x, axis=-1, keepdims=True)
    xc = x - mu
    var = jnp.mean(xc * xc, axis=-1, keepdims=True)
    return xc * lax.rsqrt(var + LN_EPS) * g + b


def _sigmoid(x):
    return 1.0 / (1.0 + jnp.exp(-x))


def _pack_rows(lo_half, hi_half):
    lo = lax.bitcast_convert_type(lo_half.astype(BF16).astype(F32), U32)
    hi = lax.bitcast_convert_type(hi_half.astype(BF16).astype(F32), U32)
    return lax.shift_right_logical(lo, jnp.uint32(16)) | hi


def _unpack_rows(p):
    lo = lax.bitcast_convert_type(lax.shift_left(p, jnp.uint32(16)), F32)
    hi = lax.bitcast_convert_type(p & jnp.uint32(0xFFFF0000), F32)
    return lo, hi


def _chunk_index(start, j, n, lead):
    rows = pl.ds(start + j, n, stride=ROW_CHUNKS)
    return (rows, slice(None)) if lead is None else (lead, rows, slice(None))


def _store_packed(ref, start, n, packed, lead=None):
    for j in range(ROW_CHUNKS):
        ref[_chunk_index(start, j, n, lead)] = packed[:, j * LANES:(j + 1) * LANES]


def _load_packed_bf16(ref, start, n, lead=None):
    halves = [_unpack_rows(ref[_chunk_index(start, j, n, lead)]) for j in range(ROW_CHUNKS)]
    return jnp.concatenate([h[0] for h in halves] + [h[1] for h in halves], axis=1).astype(BF16)


def _proj_kernel(chains, x_ref, g_ref, b_ref, w_ref, q_ref, k_ref, v_ref, u_ref):
    rows = x_ref.shape[0] // chains
    for c in range(chains):
        r = slice(c * rows, (c + 1) * rows)
        h = _layer_norm(x_ref[r, :], g_ref[...], b_ref[...])
        p = jnp.dot(h.astype(BF16), w_ref[...], preferred_element_type=F32)
        q_ref[r, :] = (p[:, :ATTN_W] * (HEAD_DIM ** -0.5)).astype(BF16)
        k_ref[r, :] = p[:, ATTN_W:ATTN_W + KV_W].astype(BF16)
        v_ref[r, :] = p[:, ATTN_W + KV_W:ATTN_W + 2 * KV_W].astype(BF16)
        a = p[:, ATTN_W + 2 * KV_W:ATTN_W + 2 * KV_W + CONV_CH]
        gate = p[:, ATTN_W + 2 * KV_W + CONV_CH:]
        u_ref[r, :] = a * _sigmoid(gate)


def _proj_call(x2d, gin, bin_, w_in_b, tq):
    t = x2d.shape[0]
    row = lambda i: (i, 0)
    fix = lambda i: (0, 0)
    chains = PROJ_CHAINS if tq % (PROJ_CHAINS * BLOCK) == 0 else 1
    return pl.pallas_call(
        functools.partial(_proj_kernel, chains),
        grid=(t // tq,),
        in_specs=[pl.BlockSpec((tq, D_MODEL), row), pl.BlockSpec((1, D_MODEL), fix),
                  pl.BlockSpec((1, D_MODEL), fix), pl.BlockSpec((D_MODEL, IN_W), fix)],
        out_specs=[pl.BlockSpec((tq, ATTN_W), row), pl.BlockSpec((tq, KV_W), row),
                   pl.BlockSpec((tq, KV_W), row), pl.BlockSpec((tq, CONV_CH), row)],
        out_shape=[SDS((t, ATTN_W), BF16), SDS((t, KV_W), BF16), SDS((t, KV_W), BF16), SDS((t, CONV_CH), F32)],
        compiler_params=_cparams("arbitrary"),
        name="ln_in_proj",
    )(x2d, gin, bin_, w_in_b)


def _attn_stages(sinks_ref, q_ref, kc_ref, kp_ref, vc_ref, vp_ref, km_ref, vm_ref, bias_ref, o_ref):
    first = pl.program_id(1) == 0
    kp = jnp.where(first, km_ref[...], kp_ref[...])
    vp = jnp.where(first, vm_ref[...], vp_ref[...])
    k = jnp.concatenate([kp, kc_ref[...]], axis=0)
    v = jnp.concatenate([vp, vc_ref[...]], axis=0)
    col = lax.broadcasted_iota(I32, (BLOCK, 2 * BLOCK), 1)
    pad_bias = jnp.where(jnp.logical_and(first, col < PAD_FRONT), NEG, 0.0).astype(F32)
    def block(a):
        q = q_ref[a * BLOCK:(a + 1) * BLOCK, :]
        kw = k[a * BLOCK:(a + 2) * BLOCK, :]
        vw = v[a * BLOCK:(a + 2) * BLOCK, :]
        outs = []
        for h in range(N_Q_HEADS):
            g = h // GQA_GROUP
            qh = q[:, h * HEAD_DIM:(h + 1) * HEAD_DIM]
            kg = kw[:, g * HEAD_DIM:(g + 1) * HEAD_DIM]
            vg = vw[:, g * HEAD_DIM:(g + 1) * HEAD_DIM]
            s = lax.dot_general(qh, kg, (((1,), (1,)), ((), ())), preferred_element_type=F32)
            s = s + bias_ref[h]
            if a == 0:
                s = s + pad_bias
            sink = sinks_ref[h]
            m = jnp.maximum(jnp.max(s, axis=-1, keepdims=True), sink)
            p = jnp.exp(s - m)
            den = jnp.sum(p, axis=-1, keepdims=True) + jnp.exp(sink - m)
            o = jnp.dot(p.astype(BF16), vg, preferred_element_type=F32)
            outs.append(o / den)
        o_ref[a * BLOCK:(a + 1) * BLOCK, :] = jnp.concatenate(outs, axis=1).astype(BF16)

    return [functools.partial(block, a) for a in range(ATTN_QBLOCKS)]


def _rel_bias_table(rel_bias):
    qi = np.arange(BLOCK, dtype=np.int32)[:, None]
    kj = np.arange(2 * BLOCK, dtype=np.int32)[None, :]
    dist = BLOCK + qi - kj
    dc = np.clip(dist, 0, WINDOW - 1)
    nf = np.maximum(dc, 1).astype(np.float32)
    large = MAX_EXACT + (np.log(nf / np.float32(MAX_EXACT)) / np.float32(math.log(REL_MAX_DIST / MAX_EXACT))
                         * np.float32(NUM_BUCKETS - MAX_EXACT)).astype(np.int32)
    large = np.minimum(large, NUM_BUCKETS - 1)
    bucket = np.where(dc < MAX_EXACT, dc, large)
    in_window = (dist >= 0) & (dist < WINDOW)
    onehot = (bucket.reshape(-1, 1) == np.arange(NUM_BUCKETS)[None, :]).astype(np.float32)
    bias = jnp.dot(jnp.asarray(onehot), rel_bias.astype(F32), precision=lax.Precision.HIGHEST)
    bias = jnp.transpose(bias.reshape(BLOCK, 2 * BLOCK, N_Q_HEADS), (2, 0, 1))
    return jnp.where(in_window[None], bias, NEG)


def _conv_stages(uc_ref, up_ref, um_ref, w_ref, cb_ref, g_ref, b_ref, o_ref, s_ref, sh_ref):
    first = pl.program_id(1) == 0
    s_ref[0:CONV_HALO, :] = jnp.where(first, um_ref[...], up_ref[...])
    s_ref[CONV_HALO:CONV_HALO + T_CONV, :] = uc_ref[...]
    off = CONV_HALO - (CONV_K - 1)
    span = sh_ref.shape[1]
    for p in range(1, SUBLANES):
        sh_ref[p] = s_ref[p:p + span, :]
    def chunk(c):
        acc = jnp.zeros((R_CONV, CONV_CH), F32) + cb_ref[...]
        for kk in range(CONV_K):
            p, a = (off + kk) % SUBLANES, (off + kk) // SUBLANES * SUBLANES
            if p == 0:
                win = s_ref[c + a:c + a + R_CONV, :]
            else:
                win = sh_ref[p, c + a:c + a + R_CONV, :]
            acc = acc + win * w_ref[kk:kk + 1, :]
        y = _layer_norm(acc, g_ref[...], b_ref[...])
        o_ref[c:c + R_CONV, :] = (y * _sigmoid(y)).astype(BF16)

    return [functools.partial(chunk, c) for c in range(0, T_CONV, R_CONV)]


def _attn_conv_kernel(sinks_ref, q_ref, kc_ref, kp_ref, vc_ref, vp_ref, km_ref, vm_ref, bias_ref,
                      uc_ref, up_ref, um_ref, w_ref, cb_ref, g_ref, b_ref, ao_ref, co_ref, s_ref, sh_ref):
    conv = _conv_stages(uc_ref, up_ref, um_ref, w_ref, cb_ref, g_ref, b_ref, co_ref, s_ref, sh_ref)
    attn = _attn_stages(sinks_ref, q_ref, kc_ref, kp_ref, vc_ref, vp_ref, km_ref, vm_ref, bias_ref, ao_ref)
    per = len(conv) // len(attn)
    for a, attn_block in enumerate(attn):
        for chunk in conv[a * per:(a + 1) * per]:
            chunk()
        attn_block()


def _attn_conv_call(q, k, v, k_meta, v_meta, bias, sinks, u, u_meta_halo, conv_w, conv_b, g, b, nbatch, seq):
    t = q.shape[0]
    rows = ATTN_QBLOCKS * BLOCK
    assert rows == T_CONV
    nstep = seq // rows
    per_blk = rows // BLOCK
    per_halo = rows // CONV_HALO
    cur = lambda bb, j: (bb * nstep + j, 0)
    prev_blk = lambda bb, j: (jnp.maximum((bb * nstep + j) * per_blk - 1, 0), 0)
    prev_halo = lambda bb, j: (jnp.maximum((bb * nstep + j) * per_halo - 1, 0), 0)
    fix = lambda bb, j: (0, 0)
    return pl.pallas_call(
        _attn_conv_kernel,
        grid=(nbatch, nstep),
        in_specs=[pl.BlockSpec(memory_space=pltpu.SMEM),
                  pl.BlockSpec((rows, ATTN_W), cur),
                  pl.BlockSpec((rows, KV_W), cur), pl.BlockSpec((BLOCK, KV_W), prev_blk),
                  pl.BlockSpec((rows, KV_W), cur), pl.BlockSpec((BLOCK, KV_W), prev_blk),
                  pl.BlockSpec((BLOCK, KV_W), fix), pl.BlockSpec((BLOCK, KV_W), fix),
                  pl.BlockSpec((N_Q_HEADS, BLOCK, 2 * BLOCK), lambda bb, j: (0, 0, 0)),
                  pl.BlockSpec((rows, CONV_CH), cur), pl.BlockSpec((CONV_HALO, CONV_CH), prev_halo),
                  pl.BlockSpec((CONV_HALO, CONV_CH), fix), pl.BlockSpec((CONV_K, CONV_CH), fix),
                  pl.BlockSpec((1, CONV_CH), fix), pl.BlockSpec((1, CONV_CH), fix), pl.BlockSpec((1, CONV_CH), fix)],
        out_specs=[pl.BlockSpec((rows, ATTN_W), cur), pl.BlockSpec((rows, CONV_CH), cur)],
        out_shape=[SDS((t, ATTN_W), BF16), SDS((t, CONV_CH), BF16)],
        scratch_shapes=[pltpu.VMEM((CONV_HALO + T_CONV, CONV_CH), F32),
                        pltpu.VMEM((SUBLANES, T_CONV + CONV_HALO - SUBLANES, CONV_CH), F32)],
        compiler_params=_cparams("arbitrary", "arbitrary"),
        name="attn_conv",
    )(sinks, q, k, k, v, v, k_meta, v_meta, bias, u, u, u_meta_halo, conv_w, conv_b, g, b)


def _mix_kernel(x_ref, at_ref, cv_ref, gin_ref, bin_ref, woa_ref, woc_ref, g1_ref, b1_ref,
                wr_ref, h1_ref, h1r_ref, lg_ref):
    rows = x_ref.shape[0] // MIX_CHAINS
    nt = (((1,), (1,)), ((), ()))
    for c in range(MIX_CHAINS):
        r = slice(c * rows, (c + 1) * rows)
        h = _layer_norm(x_ref[r, :], gin_ref[...], bin_ref[...])
        mix = (jnp.dot(at_ref[r, :], woa_ref[...], preferred_element_type=F32)
               + jnp.dot(cv_ref[r, :], woc_ref[...], preferred_element_type=F32))
        h1 = _layer_norm(ALPHA * h + mix, g1_ref[...], b1_ref[...])
        h1_ref[r, :] = h1
        _store_packed(h1r_ref, c * rows * ROW_CHUNKS, rows, _pack_rows(h1[:, :HALF], h1[:, HALF:]))
        lg_ref[:, r] = lax.dot_general(wr_ref[...], h1.astype(BF16), nt, preferred_element_type=F32)


def _mix_call(x2d, attn, conv, gin, bin_, woa, woc, g1, b1, wr):
    t = x2d.shape[0]
    tq = TQ_MIX
    row = lambda i: (i, 0)
    fix = lambda i: (0, 0)
    return pl.pallas_call(
        _mix_kernel,
        grid=(t // tq,),
        in_specs=[pl.BlockSpec((tq, D_MODEL), row), pl.BlockSpec((tq, ATTN_W), row), pl.BlockSpec((tq, CONV_CH), row),
                  pl.BlockSpec((1, D_MODEL), fix), pl.BlockSpec((1, D_MODEL), fix),
                  pl.BlockSpec((ATTN_W, D_MODEL), fix), pl.BlockSpec((CONV_CH, D_MODEL), fix),
                  pl.BlockSpec((1, D_MODEL), fix), pl.BlockSpec((1, D_MODEL), fix),
                  pl.BlockSpec((N_EXPERTS, D_MODEL), fix)],
        out_specs=[pl.BlockSpec((tq, D_MODEL), row), pl.BlockSpec((tq * ROW_CHUNKS, LANES), row),
                   pl.BlockSpec((N_EXPERTS, tq), lambda i: (0, i))],
        out_shape=[SDS((t, D_MODEL), F32), SDS((t * ROW_CHUNKS, LANES), U32), SDS((N_EXPERTS, t), F32)],
        compiler_params=_cparams("arbitrary"),
        name="mix_ln1",
    )(x2d, attn, conv, gin, bin_, woa, woc, g1, b1, wr)


def _row_numbers(nrows, n):
    return lax.broadcasted_iota(I32, (nrows, n), 0).astype(F32)


def _first_argmax(x, rows, nrows):
    m = jnp.max(x, axis=0, keepdims=True)
    idx = jnp.min(jnp.where(x == m, rows, float(nrows)), axis=0, keepdims=True)
    return m, idx


def _route_tile(logits, rbias, carry):
    tn = logits.shape[1]
    scores = _sigmoid(logits)
    choice = scores + rbias
    rows_g = _row_numbers(GROUP_SIZE, tn)
    rows_8 = _row_numbers(N_GROUPS, tn)

    gs = []
    for g in range(N_GROUPS):
        xg = choice[g * GROUP_SIZE:(g + 1) * GROUP_SIZE, :]
        m1, i1 = _first_argmax(xg, rows_g, GROUP_SIZE)
        m2 = jnp.max(jnp.where(rows_g == i1, -jnp.inf, xg), axis=0, keepdims=True)
        gs.append(m1 + m2)
    gsc = jnp.concatenate(gs, axis=0)
    gsel = jnp.zeros((N_GROUPS, tn), F32)
    for _ in range(TOPK_GROUPS):
        _, gi = _first_argmax(gsc, rows_8, N_GROUPS)
        hit = rows_8 == gi
        gsel = jnp.where(hit, 1.0, gsel)
        gsc = jnp.where(hit, -jnp.inf, gsc)

    spread = N_GROUPS - TOPK_GROUPS
    g_on = [gsel[g:g + 1, :] > 0.5 for g in range(N_GROUPS)]
    pos, n_before = [], jnp.zeros((1, tn), F32)
    for g in range(N_GROUPS):
        pos.append(n_before)
        n_before = n_before + gsel[g:g + 1, :]
    in_slot = [{g: jnp.logical_and(g_on[g], pos[g] == float(j)) for g in range(j, j + spread + 1)}
               for j in range(TOPK_GROUPS)]

    def group_rows(x, g):
        return x[g * GROUP_SIZE:(g + 1) * GROUP_SIZE, :]

    def compact(x):
        slots = []
        for j in range(TOPK_GROUPS):
            v = group_rows(x, j + spread)
            for g in range(j + spread - 1, j - 1, -1):
                v = jnp.where(in_slot[j][g], group_rows(x, g), v)
            slots.append(v)
        return jnp.concatenate(slots, axis=0)

    n_c = TOPK_GROUPS * GROUP_SIZE
    rows_c = _row_numbers(n_c, tn)
    scores_c = compact(scores)
    masked = jnp.maximum(compact(choice), float(jnp.finfo(F32).min))
    shift = []
    for j in range(TOPK_GROUPS):
        s = jnp.full((1, tn), float((j + spread) * GROUP_SIZE), F32)
        for g in range(j + spread - 1, j - 1, -1):
            s = jnp.where(in_slot[j][g], float(g * GROUP_SIZE), s)
        shift.append(s - float(j * GROUP_SIZE))

    hits, idxs, ws = [], [], []
    for _ in range(TOP_K):
        _, ii = _first_argmax(masked, rows_c, n_c)
        hit = rows_c == ii
        hits.append(hit)
        e = ii + shift[TOPK_GROUPS - 1]
        for j in range(TOPK_GROUPS - 2, -1, -1):
            e = jnp.where(ii < float((j + 1) * GROUP_SIZE), ii + shift[j], e)
        idxs.append(e)
        ws.append(jnp.sum(jnp.where(hit, scores_c, 0.0), axis=0, keepdims=True))
        masked = jnp.where(hit, -jnp.inf, masked)
    wsum = ws[0]
    for w in ws[1:]:
        wsum = wsum + w
    idx = jnp.concatenate(idxs, axis=0).astype(I32)
    wts = jnp.concatenate([w / wsum * ROUTED_SCALE for w in ws], axis=0)

    picked_c = jnp.where(masked == -jnp.inf, 1.0, 0.0)
    sel = []
    for g in range(N_GROUPS):
        v = jnp.zeros((GROUP_SIZE, tn), F32)
        for j in range(max(0, g - spread), min(TOPK_GROUPS - 1, g) + 1):
            v = jnp.where(in_slot[j][g], group_rows(picked_c, j), v)
        sel.append(v)
    sel_b = jnp.concatenate(sel, axis=0).astype(BF16)

    r_i = lax.broadcasted_iota(I32, (tn, tn), 0)
    c_i = lax.broadcasted_iota(I32, (tn, tn), 1)
    upper = jnp.where(r_i < c_i, 1.0, 0.0).astype(BF16)
    before = jnp.dot(sel_b, upper, preferred_element_type=F32)
    before = compact(before + jnp.concatenate([carry] * (tn // LANES), axis=1))
    rank = jnp.concatenate(
        [jnp.sum(jnp.where(h, before, 0.0), axis=0, keepdims=True) for h in hits], axis=0).astype(I32)
    carry = carry + jnp.dot(sel_b, jnp.ones((tn, LANES), BF16), preferred_element_type=F32)
    return idx, wts, rank, carry


def _route_kernel(lg_ref, rb_ref, idx_ref, wts_ref, rank_ref, cnt_ref, carry_ref):
    @pl.when(pl.program_id(0) == 0)
    def _():
        carry_ref[...] = jnp.zeros_like(carry_ref)

    carry = carry_ref[...]
    tn = lg_ref.shape[1] // ROUTE_CHAINS
    for c in range(ROUTE_CHAINS):
        cols = slice(c * tn, (c + 1) * tn)
        idx, wts, rank, carry = _route_tile(lg_ref[:, cols], rb_ref[...], carry)
        idx_ref[:, cols] = idx
        wts_ref[cols, :] = wts.T
        rank_ref[:, cols] = rank
    carry_ref[...] = carry
    cnt_ref[...] = carry.astype(I32)


def _route_call(lg, rbias):
    t = lg.shape[1]
    tn = TN_ROUTE
    col = lambda i: (0, i)
    return pl.pallas_call(
        _route_kernel,
        grid=(t // tn,),
        in_specs=[pl.BlockSpec((N_EXPERTS, tn), col), pl.BlockSpec((N_EXPERTS, 1), lambda i: (0, 0))],
        out_specs=[pl.BlockSpec((TOP_K, tn), col), pl.BlockSpec((tn, TOP_K), lambda i: (i, 0)),
                   pl.BlockSpec((TOP_K, tn), col), pl.BlockSpec((N_EXPERTS, LANES), lambda i: (0, 0))],
        out_shape=[SDS((TOP_K, t), I32), SDS((t, TOP_K), F32), SDS((TOP_K, t), I32), SDS((N_EXPERTS, LANES), I32)],
        scratch_shapes=[pltpu.VMEM((N_EXPERTS, LANES), F32)],
        compiler_params=_cparams("arbitrary"),
        name="route",
    )(lg, rbias)


SC_CORES = 2
SC_SUBCORES = 16
SC_CHUNK = 128
SC_LANES = 16
SC_BUFS = 2


def _sc_worker_chunks(t):
    per_worker = t // (SC_CORES * SC_SUBCORES)
    assert per_worker % SC_CHUNK == 0
    return per_worker


def _sc_dispatch_call(idx, rank, offs, h1rows3, n_rows):
    t = idx.shape[1]
    per_worker = _sc_worker_chunks(t)
    mesh = plsc.VectorSubcoreMesh(core_axis_name="c", subcore_axis_name="s")

    @functools.partial(
        pl.kernel, mesh=mesh, out_type=[SDS((n_rows, ROW_CHUNKS, LANES), U32), SDS((TOP_K, t), I32)],
        scratch_types=[pltpu.VMEM((TOP_K, SC_CHUNK), I32), pltpu.VMEM((TOP_K, SC_CHUNK), I32),
                       pltpu.VMEM((N_EXPERTS,), I32), pltpu.VMEM((SC_CHUNK, ROW_CHUNKS, LANES), U32),
                       pltpu.SemaphoreType.DMA],
        compiler_params=pltpu.CompilerParams(needs_layout_passes=False),
        name="sc_dispatch")
    def body(h_hbm, idx_hbm, rank_hbm, offs_hbm, xs_hbm, dest_hbm, idx_v, rank_v, offs_v, rows_v, sem):
        wid = lax.axis_index("s") * SC_CORES + lax.axis_index("c")
        pltpu.sync_copy(offs_hbm, offs_v)

        @pl.loop(0, per_worker // SC_CHUNK)
        def _(i):
            t0 = wid * per_worker + i * SC_CHUNK
            pltpu.sync_copy(idx_hbm.at[:, pl.ds(t0, SC_CHUNK)], idx_v)
            pltpu.sync_copy(rank_hbm.at[:, pl.ds(t0, SC_CHUNK)], rank_v)
            pltpu.sync_copy(h_hbm.at[pl.ds(t0, SC_CHUNK)], rows_v)
            for kk in range(TOP_K):
                @pl.loop(0, SC_CHUNK // SC_LANES)
                def _(c):
                    lanes = pl.ds(c * SC_LANES, SC_LANES)
                    idx_v[kk, lanes] = plsc.load_gather(offs_v, [idx_v[kk, lanes]]) + rank_v[kk, lanes]
            pltpu.sync_copy(idx_v, dest_hbm.at[:, pl.ds(t0, SC_CHUNK)])
            copies = [pltpu.async_copy(rows_v, xs_hbm.at[idx_v.at[kk]], sem) for kk in range(TOP_K)]
            for c in copies:
                c.wait()

    return body(h1rows3, idx, rank, offs)


def _sc_gather_call(dest, ys3):
    t = dest.shape[1]
    per_worker = _sc_worker_chunks(t)
    mesh = plsc.VectorSubcoreMesh(core_axis_name="c", subcore_axis_name="s")

    half = SC_CHUNK // 2
    items = [(kk, h) for kk in range(TOP_K) for h in range(2)]

    @functools.partial(
        pl.kernel, mesh=mesh, out_type=SDS((TOP_K * t, ROW_CHUNKS, LANES), U32),
        scratch_types=[pltpu.VMEM((TOP_K, SC_CHUNK), I32), pltpu.VMEM((SC_BUFS, half, ROW_CHUNKS, LANES), U32),
                       pltpu.SemaphoreType.DMA((SC_BUFS,)), pltpu.SemaphoreType.DMA((SC_BUFS,))],
        name="sc_gather")
    def body(ys_hbm, dest_hbm, out_hbm, idx_v, rows_v, gsem, wsem):
        wid = lax.axis_index("s") * SC_CORES + lax.axis_index("c")

        @pl.loop(0, per_worker // SC_CHUNK)
        def _(i):
            t0 = wid * per_worker + i * SC_CHUNK
            pltpu.sync_copy(dest_hbm.at[:, pl.ds(t0, SC_CHUNK)], idx_v)

            def gather(j):
                kk, h = items[j]
                b = j % SC_BUFS
                return pltpu.make_async_copy(ys_hbm.at[idx_v.at[kk, pl.ds(h * half, half)]], rows_v.at[b], gsem.at[b])

            def put(j):
                kk, h = items[j]
                b = j % SC_BUFS
                return pltpu.make_async_copy(rows_v.at[b], out_hbm.at[pl.ds(kk * t + t0 + h * half, half)], wsem.at[b])

            ahead = SC_BUFS - 1
            n = len(items)
            for j in range(ahead):
                gather(j).start()
            for j in range(n):
                if j + ahead < n:
                    if j >= 1:
                        put(j - 1).wait()
                    gather(j + ahead).start()
                gather(j).wait()
                put(j).start()
            for j in range(max(n - ahead - 1, 0), n):
                put(j).wait()

    return body(ys3, dest)


def _expert_kernel(ts_ref, te_ref, tr_ref, nv_ref, wg_hbm, wu_hbm, wd_hbm, xs_hbm, ys_hbm,
                   xbuf, ybuf, wg_f, wu_f, wd_f, wg_b, wu_b, wd_b, xsem, ysem, wsem):
    e = pl.program_id(0)
    rows = xbuf.shape[1]
    tm = rows // ROW_CHUNKS
    g0, g1, nv = ts_ref[e], te_ref[e], nv_ref[0]

    def x_copy(g):
        s = g % X_SLOTS
        return pltpu.make_async_copy(xs_hbm.at[pl.ds(pl.multiple_of(g * rows, rows), rows), :], xbuf.at[s], xsem.at[s])

    def y_copy(g):
        s = g % Y_SLOTS
        return pltpu.make_async_copy(ybuf.at[s], ys_hbm.at[pl.ds(pl.multiple_of(g * rows, rows), rows), :], ysem.at[s])

    def w_copies(ex):
        s = ex % W_SLOTS
        return (pltpu.make_async_copy(wg_hbm.at[ex], wg_f.at[s], wsem.at[s]),
                pltpu.make_async_copy(wu_hbm.at[ex], wu_f.at[s], wsem.at[s]),
                pltpu.make_async_copy(wd_hbm.at[ex], wd_f.at[s], wsem.at[s]))

    n_exp = pl.num_programs(0)

    @pl.when(e == 0)
    def _():
        for ex in range(W_AHEAD):
            for c in w_copies(ex):
                c.start()

    @pl.when(e + W_AHEAD < n_exp)
    def _():
        for c in w_copies(e + W_AHEAD):
            c.start()

    for c in w_copies(e):
        c.wait()

    def compute_tile(g):
        x = _load_packed_bf16(xbuf, 0, tm, lead=g % X_SLOTS)
        gate = jnp.dot(x, wg_b[...], preferred_element_type=F32)
        up = jnp.dot(x, wu_b[...], preferred_element_type=F32)
        live = lax.broadcasted_iota(I32, (tm, EXPERT_FF), 0) < tr_ref[g]
        hid = jnp.where(live, gate * _sigmoid(gate) * up, 0.0).astype(BF16)
        y = jnp.dot(hid, wd_b[...], preferred_element_type=F32)
        return _pack_rows(y[:, :HALF], y[:, HALF:])

    def run_tiles(g, n):
        for r in range(n):
            x_copy(g + r).wait()

            @pl.when(g + r + X_AHEAD < nv)
            def _():
                x_copy(g + r + X_AHEAD).start(priority=1)

            @pl.when(g + r >= Y_SLOTS)
            def _():
                y_copy(g + r - Y_SLOTS).wait()

        packed = [compute_tile(g + r) for r in range(n)]
        for r in range(n):
            _store_packed(ybuf, 0, tm, packed[r], lead=(g + r) % Y_SLOTS)
        for r in range(n):
            y_copy(g + r).start(priority=1)

    @pl.when(e == 0)
    def _():
        for g in range(X_AHEAD):
            @pl.when(g < nv)
            def _():
                x_copy(g).start(priority=1)

    @pl.when(g1 > g0)
    def _():
        ws = e % W_SLOTS
        wg_b[...] = wg_f[ws].astype(BF16)
        wu_b[...] = wu_f[ws].astype(BF16)
        wd_b[...] = wd_f[ws].astype(BF16)
        n_tiles = g1 - g0

        def pair(p, c):
            run_tiles(g0 + 2 * p, 2)
            return c

        lax.fori_loop(0, n_tiles // 2, pair, 0)

        @pl.when(n_tiles % 2 == 1)
        def _():
            run_tiles(g1 - 1, 1)

    @pl.when(e == pl.num_programs(0) - 1)
    def _():
        for back in range(1, Y_SLOTS + 1):
            @pl.when(nv >= back)
            def _():
                y_copy(nv - back).wait()


def _expert_call(tile_start, tile_end, tile_rows, n_valid, xs, w_gate, w_up, w_down, n_rows):
    tm = TM_EXP
    hbm = pl.BlockSpec(memory_space=pl.ANY)
    return pl.pallas_call(
        _expert_kernel,
        grid_spec=pltpu.PrefetchScalarGridSpec(
            num_scalar_prefetch=4,
            grid=(N_EXPERTS,),
            in_specs=[hbm, hbm, hbm, hbm],
            out_specs=hbm,
            scratch_shapes=[pltpu.VMEM((X_SLOTS, tm * ROW_CHUNKS, LANES), U32),
                            pltpu.VMEM((Y_SLOTS, tm * ROW_CHUNKS, LANES), U32),
                            pltpu.VMEM((W_SLOTS, D_MODEL, EXPERT_FF), F32), pltpu.VMEM((W_SLOTS, D_MODEL, EXPERT_FF), F32),
                            pltpu.VMEM((W_SLOTS, EXPERT_FF, D_MODEL), F32),
                            pltpu.VMEM((D_MODEL, EXPERT_FF), BF16), pltpu.VMEM((D_MODEL, EXPERT_FF), BF16),
                            pltpu.VMEM((EXPERT_FF, D_MODEL), BF16),
                            pltpu.SemaphoreType.DMA((X_SLOTS,)), pltpu.SemaphoreType.DMA((Y_SLOTS,)),
                            pltpu.SemaphoreType.DMA((W_SLOTS,))],
        ),
        out_shape=SDS((n_rows * ROW_CHUNKS, LANES), U32),
        compiler_params=_cparams("arbitrary"),
        name="experts",
    )(tile_start, tile_end, tile_rows, n_valid, w_gate, w_up, w_down, xs)


COMB_SUB = 32
G_SLOTS = 3
G_AHEAD = 2


def _combine_kernel(wts_ref, h1_ref, g_hbm, wsg_ref, wsu_ref, wsd_ref, g2_ref, b2_ref, o_ref, routed_ref,
                    gbuf, gsem):
    tn = h1_ref.shape[0]
    i = pl.program_id(0)
    n = pl.num_programs(0)

    def g_copy(step):
        s = step % G_SLOTS
        src = g_hbm.at[:, pl.ds(pl.multiple_of(step * tn * ROW_CHUNKS, tn * ROW_CHUNKS), tn * ROW_CHUNKS), :]
        return pltpu.make_async_copy(src, gbuf.at[s], gsem.at[s])

    @pl.when(i == 0)
    def _():
        for step in range(G_AHEAD):
            @pl.when(step < n)
            def _():
                g_copy(step).start()

    @pl.when(i + G_AHEAD < n)
    def _():
        g_copy(i + G_AHEAD).start()

    g_copy(i).wait()
    slot = i % G_SLOTS
    for s0 in range(0, tn, COMB_SUB):
        acc = [jnp.zeros((COMB_SUB, LANES), F32) for _ in range(2 * ROW_CHUNKS)]
        for kk in range(TOP_K):
            wk = jnp.broadcast_to(wts_ref[s0:s0 + COMB_SUB, kk:kk + 1], (COMB_SUB, LANES))
            for cc in range(ROW_CHUNKS):
                lo, hi = _unpack_rows(gbuf[slot, kk, pl.ds(s0 * ROW_CHUNKS + cc, COMB_SUB, stride=ROW_CHUNKS), :])
                acc[cc] = acc[cc] + wk * lo
                acc[ROW_CHUNKS + cc] = acc[ROW_CHUNKS + cc] + wk * hi
        routed_ref[s0:s0 + COMB_SUB, :] = jnp.concatenate(acc, axis=1)

    h1 = h1_ref[...]
    hb = h1.astype(BF16)
    sg = jnp.dot(hb, wsg_ref[...], preferred_element_type=F32)
    su = jnp.dot(hb, wsu_ref[...], preferred_element_type=F32)
    ff = jnp.dot((sg * _sigmoid(sg) * su).astype(BF16), wsd_ref[...], preferred_element_type=F32)
    o_ref[...] = _layer_norm(ALPHA * h1 + ff + routed_ref[...], g2_ref[...], b2_ref[...])


def _combine_call(wts_t, h1, gathered, wsg, wsu, wsd, g2, b2):
    t = h1.shape[0]
    tn = TN_COMB
    row = lambda i: (i, 0)
    fix = lambda i: (0, 0)
    return pl.pallas_call(
        _combine_kernel,
        grid=(t // tn,),
        in_specs=[pl.BlockSpec((tn, TOP_K), row),
                  pl.BlockSpec((tn, D_MODEL), row),
                  pl.BlockSpec(memory_space=pl.ANY),
                  pl.BlockSpec((D_MODEL, SHARED_FF), fix), pl.BlockSpec((D_MODEL, SHARED_FF), fix),
                  pl.BlockSpec((SHARED_FF, D_MODEL), fix),
                  pl.BlockSpec((1, D_MODEL), fix), pl.BlockSpec((1, D_MODEL), fix)],
        out_specs=pl.BlockSpec((tn, D_MODEL), row),
        out_shape=SDS((t, D_MODEL), F32),
        scratch_shapes=[pltpu.VMEM((tn, D_MODEL), F32),
                        pltpu.VMEM((G_SLOTS, TOP_K, tn * ROW_CHUNKS, LANES), U32), pltpu.SemaphoreType.DMA((G_SLOTS,))],
        compiler_params=_cparams("arbitrary"),
        name="combine_ln2",
    )(wts_t, h1, gathered, wsg, wsu, wsd, g2, b2)


def kernel(x, meta_tokens, ln_in_g, ln_in_b, rel_bias, w_in, conv_w, conv_b, conv_ln_g, conv_ln_b, sinks,
           w_out, ln1_g, ln1_b, w_router, router_bias, w_gate, w_up, w_down, ws_gate, ws_up, ws_down,
           ln2_g, ln2_b):
    nbatch, seq, d = x.shape
    t = nbatch * seq
    assert d == D_MODEL and w_in.shape[0] == DEPTH
    assert seq % (ATTN_QBLOCKS * BLOCK) == 0 and seq % T_CONV == 0
    assert all(t % tile == 0 for tile in (TQ_PROJ, TQ_MIX, TN_ROUTE, TN_COMB))
    x2d = x.reshape(t, D_MODEL)
    vec = lambda a: a.reshape(1, -1).astype(F32)
    gin, bin_ = vec(ln_in_g), vec(ln_in_b)
    w_in_b = w_in[0].astype(BF16)

    q, k, v, u = _proj_call(x2d, gin, bin_, w_in_b, TQ_PROJ)
    meta_blk = jnp.concatenate([jnp.zeros((PAD_FRONT, D_MODEL), F32), meta_tokens.astype(F32)], axis=0)
    _, k_meta, v_meta, u_meta = _proj_call(meta_blk, gin, bin_, w_in_b, BLOCK)

    u_halo = jnp.concatenate([jnp.zeros((CONV_HALO - N_META, CONV_CH), F32), u_meta[PAD_FRONT:]], axis=0)
    attn, conv = _attn_conv_call(q, k, v, k_meta, v_meta, _rel_bias_table(rel_bias), sinks[0].astype(F32),
                                 u, u_halo, conv_w[0].astype(F32), vec(conv_b[0]), vec(conv_ln_g[0]),
                                 vec(conv_ln_b[0]), nbatch, seq)

    w_out_b = w_out[0].astype(BF16)
    h1, h1rows, logits = _mix_call(x2d, attn, conv, gin, bin_, w_out_b[:ATTN_W], w_out_b[ATTN_W:],
                                   vec(ln1_g[0]), vec(ln1_b[0]), w_router[0].astype(BF16).T)

    idx, wts_t, rank, cnt = _route_call(logits, router_bias[0].astype(F32).reshape(N_EXPERTS, 1))

    tm = TM_EXP
    n_tiles = (t * TOP_K) // tm + N_EXPERTS
    counts = cnt[:, 0]
    tiles_e = (counts + tm - 1) // tm
    tile_end = jnp.cumsum(tiles_e).astype(I32)
    tile_start = (tile_end - tiles_e).astype(I32)
    offs = tile_start * tm
    tile_id = jnp.arange(n_tiles, dtype=I32)
    lo = jnp.maximum(tile_id[:, None] * tm, offs[None, :])
    hi = jnp.minimum((tile_id[:, None] + 1) * tm, (offs + counts)[None, :])
    tile_rows = jnp.sum(jnp.clip(hi - lo, 0, tm), axis=1).astype(I32)
    n_valid = tile_end[-1:]

    xs, dest = _sc_dispatch_call(idx, rank, offs, h1rows.reshape(t, ROW_CHUNKS, LANES), n_tiles * tm)
    xs = xs.reshape(n_tiles * tm * ROW_CHUNKS, LANES)
    ys = _expert_call(tile_start, tile_end, tile_rows, n_valid, xs, w_gate[0], w_up[0], w_down[0], n_tiles * tm)
    gathered = _sc_gather_call(dest, ys.reshape(n_tiles * tm, ROW_CHUNKS, LANES))
    gathered = gathered.reshape(TOP_K, t * ROW_CHUNKS, LANES)
    out = _combine_call(wts_t, h1, gathered, ws_gate[0].astype(BF16), ws_up[0].astype(BF16),
                        ws_down[0].astype(BF16), vec(ln2_g[0]), vec(ln2_b[0]))
    return out.reshape(nbatch, seq, D_MODEL)
```

```python
import functools
import math

import numpy as np
import jax
import jax.numpy as jnp
from jax import lax
from jax.experimental import pallas as pl
from jax.experimental.pallas import tpu as pltpu
from jax.experimental.pallas import tpu_sc as plsc

F32 = jnp.float32
BF16 = jnp.bfloat16
I32 = jnp.int32
U32 = jnp.uint32
SDS = jax.ShapeDtypeStruct

D_MODEL = 1024
HALF = D_MODEL // 2
LANES = 128
SUBLANES = 8
ROW_CHUNKS = HALF // LANES
N_META = 16
HEAD_DIM = 64
N_Q_HEADS = 8
N_KV_HEADS = 2
GQA_GROUP = N_Q_HEADS // N_KV_HEADS
ATTN_W = N_Q_HEADS * HEAD_DIM
KV_W = N_KV_HEADS * HEAD_DIM
WINDOW = 128
BLOCK = 128
CONV_CH = D_MODEL - ATTN_W
CONV_K = 31
IN_W = ATTN_W + 2 * KV_W + 2 * CONV_CH
NUM_BUCKETS = 32
MAX_EXACT = NUM_BUCKETS // 2
REL_MAX_DIST = 128
N_EXPERTS = 256
TOP_K = 8
N_GROUPS = 8
GROUP_SIZE = N_EXPERTS // N_GROUPS
TOPK_GROUPS = 4
EXPERT_FF = 256
SHARED_FF = 256
ROUTED_SCALE = 2.5
DEPTH = 1
ALPHA = (2.0 * DEPTH) ** 0.25
LN_EPS = 1e-5
NEG = -1e30
PAD_FRONT = (-N_META) % BLOCK

VMEM_LIMIT = 48 * 1024 * 1024

TQ_PROJ = 1024
PROJ_CHAINS = 4
ATTN_QBLOCKS = 2
T_CONV = 256
CONV_HALO = 32
R_CONV = 64
TQ_MIX = 1024
MIX_CHAINS = 4
X_MIX_SLOTS = 3
X_MIX_AHEAD = 2
TN_ROUTE = 1024
ROUTE_CHAINS = 4
TM_EXP = 256
X_SLOTS = 8
X_AHEAD = 4
Y_SLOTS = 4
W_SLOTS = 3
W_AHEAD = 2
TN_COMB = 512


def _cparams(*sem):
    return pltpu.CompilerParams(dimension_semantics=sem, vmem_limit_bytes=VMEM_LIMIT)


def _layer_norm(x, g, b):
    mu = jnp.mean(x, axis=-1, keepdims=True)
    xc = x - mu
    var = jnp.mean(xc * xc, axis=-1, keepdims=True)
    return xc * lax.rsqrt(var + LN_EPS) * g + b


def _sigmoid(x):
    return 1.0 / (1.0 + jnp.exp(-x))


def _pack_rows(lo_half, hi_half):
    lo = lax.bitcast_convert_type(lo_half.astype(BF16).astype(F32), U32)
    hi = lax.bitcast_convert_type(hi_half.astype(BF16).astype(F32), U32)
    return lax.shift_right_logical(lo, jnp.uint32(16)) | hi


def _unpack_rows(p):
    lo = lax.bitcast_convert_type(lax.shift_left(p, jnp.uint32(16)), F32)
    hi = lax.bitcast_convert_type(p & jnp.uint32(0xFFFF0000), F32)
    return lo, hi


def _chunk_index(start, j, n, lead):
    rows = pl.ds(start + j, n, stride=ROW_CHUNKS)
    return (rows, slice(None)) if lead is None else (lead, rows, slice(None))


def _store_packed(ref, start, n, packed, lead=None):
    for j in range(ROW_CHUNKS):
        ref[_chunk_index(start, j, n, lead)] = packed[:, j * LANES:(j + 1) * LANES]


def _load_packed_bf16(ref, start, n, lead=None):
    halves = [_unpack_rows(ref[_chunk_index(start, j, n, lead)]) for j in range(ROW_CHUNKS)]
    return jnp.concatenate([h[0] for h in halves] + [h[1] for h in halves], axis=1).astype(BF16)


def _proj_kernel(chains, x_ref, g_ref, b_ref, w_ref, q_ref, k_ref, v_ref, u_ref):
    rows = x_ref.shape[0] // chains
    for c in range(chains):
        r = slice(c * rows, (c + 1) * rows)
        h = _layer_norm(x_ref[r, :], g_ref[...], b_ref[...])
        p = jnp.dot(h.astype(BF16), w_ref[...], preferred_element_type=F32)
        q_ref[r, :] = (p[:, :ATTN_W] * (HEAD_DIM ** -0.5)).astype(BF16)
        k_ref[r, :] = p[:, ATTN_W:ATTN_W + KV_W].astype(BF16)
        v_ref[r, :] = p[:, ATTN_W + KV_W:ATTN_W + 2 * KV_W].astype(BF16)
        a = p[:, ATTN_W + 2 * KV_W:ATTN_W + 2 * KV_W + CONV_CH]
        gate = p[:, ATTN_W + 2 * KV_W + CONV_CH:]
        u_ref[r, :] = a * _sigmoid(gate)


def _proj_call(x2d, gin, bin_, w_in_b, tq):
    t = x2d.shape[0]
    row = lambda i: (i, 0)
    fix = lambda i: (0, 0)
    chains = PROJ_CHAINS if tq % (PROJ_CHAINS * BLOCK) == 0 else 1
    return pl.pallas_call(
        functools.partial(_proj_kernel, chains),
        grid=(t // tq,),
        in_specs=[pl.BlockSpec((tq, D_MODEL), row), pl.BlockSpec((1, D_MODEL), fix),
                  pl.BlockSpec((1, D_MODEL), fix), pl.BlockSpec((D_MODEL, IN_W), fix)],
        out_specs=[pl.BlockSpec((tq, ATTN_W), row), pl.BlockSpec((tq, KV_W), row),
                   pl.BlockSpec((tq, KV_W), row), pl.BlockSpec((tq, CONV_CH), row)],
        out_shape=[SDS((t, ATTN_W), BF16), SDS((t, KV_W), BF16), SDS((t, KV_W), BF16), SDS((t, CONV_CH), F32)],
        compiler_params=_cparams("arbitrary"),
        name="ln_in_proj",
    )(x2d, gin, bin_, w_in_b)


def _attn_stages(sinks_ref, q_ref, kc_ref, kp_ref, vc_ref, vp_ref, km_ref, vm_ref, bias_ref, o_ref):
    first = pl.program_id(1) == 0
    kp = jnp.where(first, km_ref[...], kp_ref[...])
    vp = jnp.where(first, vm_ref[...], vp_ref[...])
    k = jnp.concatenate([kp, kc_ref[...]], axis=0)
    v = jnp.concatenate([vp, vc_ref[...]], axis=0)
    col = lax.broadcasted_iota(I32, (BLOCK, 2 * BLOCK), 1)
    pad_bias = jnp.where(jnp.logical_and(first, col < PAD_FRONT), NEG, 0.0).astype(F32)
    def block(a):
        q = q_ref[a * BLOCK:(a + 1) * BLOCK, :]
        kw = k[a * BLOCK:(a + 2) * BLOCK, :]
        vw = v[a * BLOCK:(a + 2) * BLOCK, :]
        outs = []
        for h in range(N_Q_HEADS):
            g = h // GQA_GROUP
            qh = q[:, h * HEAD_DIM:(h + 1) * HEAD_DIM]
            kg = kw[:, g * HEAD_DIM:(g + 1) * HEAD_DIM]
            vg = vw[:, g * HEAD_DIM:(g + 1) * HEAD_DIM]
            s = lax.dot_general(qh, kg, (((1,), (1,)), ((), ())), preferred_element_type=F32)
            s = s + bias_ref[h]
            if a == 0:
                s = s + pad_bias
            sink = sinks_ref[h]
            m = jnp.maximum(jnp.max(s, axis=-1, keepdims=True), sink)
            p = jnp.exp(s - m)
            den = jnp.sum(p, axis=-1, keepdims=True) + jnp.exp(sink - m)
            o = jnp.dot(p.astype(BF16), vg, preferred_element_type=F32)
            outs.append(o / den)
        o_ref[a * BLOCK:(a + 1) * BLOCK, :] = jnp.concatenate(outs, axis=1).astype(BF16)

    return [functools.partial(block, a) for a in range(ATTN_QBLOCKS)]


def _rel_bias_table(rel_bias):
    qi = np.arange(BLOCK, dtype=np.int32)[:, None]
    kj = np.arange(2 * BLOCK, dtype=np.int32)[None, :]
    dist = BLOCK + qi - kj
    dc = np.clip(dist, 0, WINDOW - 1)
    nf = np.maximum(dc, 1).astype(np.float32)
    large = MAX_EXACT + (np.log(nf / np.float32(MAX_EXACT)) / np.float32(math.log(REL_MAX_DIST / MAX_EXACT))
                         * np.float32(NUM_BUCKETS - MAX_EXACT)).astype(np.int32)
    large = np.minimum(large, NUM_BUCKETS - 1)
    bucket = np.where(dc < MAX_EXACT, dc, large)
    in_window = (dist >= 0) & (dist < WINDOW)
    onehot = (bucket.reshape(-1, 1) == np.arange(NUM_BUCKETS)[None, :]).astype(np.float32)
    bias = jnp.dot(jnp.asarray(onehot), rel_bias.astype(F32), precision=lax.Precision.HIGHEST)
    bias = jnp.transpose(bias.reshape(BLOCK, 2 * BLOCK, N_Q_HEADS), (2, 0, 1))
    return jnp.where(in_window[None], bias, NEG)


def _conv_stages(uc_ref, up_ref, um_ref, w_ref, cb_ref, g_ref, b_ref, o_ref, s_ref, sh_ref):
    first = pl.program_id(1) == 0
    s_ref[0:CONV_HALO, :] = jnp.where(first, um_ref[...], up_ref[...])
    s_ref[CONV_HALO:CONV_HALO + T_CONV, :] = uc_ref[...]
    off = CONV_HALO - (CONV_K - 1)
    span = sh_ref.shape[1]
    for p in range(1, SUBLANES):
        sh_ref[p] = s_ref[p:p + span, :]
    def chunk(c):
        acc = jnp.zeros((R_CONV, CONV_CH), F32) + cb_ref[...]
        for kk in range(CONV_K):
            p, a = (off + kk) % SUBLANES, (off + kk) // SUBLANES * SUBLANES
            if p == 0:
                win = s_ref[c + a:c + a + R_CONV, :]
            else:
                win = sh_ref[p, c + a:c + a + R_CONV, :]
            acc = acc + win * w_ref[kk:kk + 1, :]
        y = _layer_norm(acc, g_ref[...], b_ref[...])
        o_ref[c:c + R_CONV, :] = (y * _sigmoid(y)).astype(BF16)

    return [functools.partial(chunk, c) for c in range(0, T_CONV, R_CONV)]


def _attn_conv_kernel(sinks_ref, q_ref, kc_ref, kp_ref, vc_ref, vp_ref, km_ref, vm_ref, bias_ref,
                      uc_ref, up_ref, um_ref, w_ref, cb_ref, g_ref, b_ref, ao_ref, co_ref, s_ref, sh_ref):
    conv = _conv_stages(uc_ref, up_ref, um_ref, w_ref, cb_ref, g_ref, b_ref, co_ref, s_ref, sh_ref)
    attn = _attn_stages(sinks_ref, q_ref, kc_ref, kp_ref, vc_ref, vp_ref, km_ref, vm_ref, bias_ref, ao_ref)
    per = len(conv) // len(attn)
    for a, attn_block in enumerate(attn):
        for chunk in conv[a * per:(a + 1) * per]:
            chunk()
        attn_block()


def _attn_conv_call(q, k, v, k_meta, v_meta, bias, sinks, u, u_meta_halo, conv_w, conv_b, g, b, nbatch, seq):
    t = q.shape[0]
    rows = ATTN_QBLOCKS * BLOCK
    assert rows == T_CONV
    nstep = seq // rows
    per_blk = rows // BLOCK
    per_halo = rows // CONV_HALO
    cur = lambda bb, j: (bb * nstep + j, 0)
    prev_blk = lambda bb, j: (jnp.maximum((bb * nstep + j) * per_blk - 1, 0), 0)
    prev_halo = lambda bb, j: (jnp.maximum((bb * nstep + j) * per_halo - 1, 0), 0)
    fix = lambda bb, j: (0, 0)
    return pl.pallas_call(
        _attn_conv_kernel,
        grid=(nbatch, nstep),
        in_specs=[pl.BlockSpec(memory_space=pltpu.SMEM),
                  pl.BlockSpec((rows, ATTN_W), cur),
                  pl.BlockSpec((rows, KV_W), cur), pl.BlockSpec((BLOCK, KV_W), prev_blk),
                  pl.BlockSpec((rows, KV_W), cur), pl.BlockSpec((BLOCK, KV_W), prev_blk),
                  pl.BlockSpec((BLOCK, KV_W), fix), pl.BlockSpec((BLOCK, KV_W), fix),
                  pl.BlockSpec((N_Q_HEADS, BLOCK, 2 * BLOCK), lambda bb, j: (0, 0, 0)),
                  pl.BlockSpec((rows, CONV_CH), cur), pl.BlockSpec((CONV_HALO, CONV_CH), prev_halo),
                  pl.BlockSpec((CONV_HALO, CONV_CH), fix), pl.BlockSpec((CONV_K, CONV_CH), fix),
                  pl.BlockSpec((1, CONV_CH), fix), pl.BlockSpec((1, CONV_CH), fix), pl.BlockSpec((1, CONV_CH), fix)],
        out_specs=[pl.BlockSpec((rows, ATTN_W), cur), pl.BlockSpec((rows, CONV_CH), cur)],
        out_shape=[SDS((t, ATTN_W), BF16), SDS((t, CONV_CH), BF16)],
        scratch_shapes=[pltpu.VMEM((CONV_HALO + T_CONV, CONV_CH), F32),
                        pltpu.VMEM((SUBLANES, T_CONV + CONV_HALO - SUBLANES, CONV_CH), F32)],
        compiler_params=_cparams("arbitrary", "arbitrary"),
        name="attn_conv",
    )(sinks, q, k, k, v, v, k_meta, v_meta, bias, u, u, u_meta_halo, conv_w, conv_b, g, b)


def _mix_kernel(x_hbm, at_ref, cv_ref, gin_ref, bin_ref, woa_ref, woc_ref, g1_ref, b1_ref,
                wr_ref, h1_ref, h1r_ref, lg_ref, xbuf, xsem):
    tq = at_ref.shape[0]
    rows = tq // MIX_CHAINS
    nt = (((1,), (1,)), ((), ()))
    i = pl.program_id(0)
    n = pl.num_programs(0)

    def x_copy(step):
        s = step % X_MIX_SLOTS
        return pltpu.make_async_copy(x_hbm.at[pl.ds(pl.multiple_of(step * tq, tq), tq), :], xbuf.at[s], xsem.at[s])

    @pl.when(i == 0)
    def _():
        for step in range(X_MIX_AHEAD):
            @pl.when(step < n)
            def _():
                x_copy(step).start()

    @pl.when(i + X_MIX_AHEAD < n)
    def _():
        x_copy(i + X_MIX_AHEAD).start()

    x_copy(i).wait()
    slot = i % X_MIX_SLOTS
    for c in range(MIX_CHAINS):
        r = slice(c * rows, (c + 1) * rows)
        h = _layer_norm(xbuf[slot, r, :], gin_ref[...], bin_ref[...])
        mix = (jnp.dot(at_ref[r, :], woa_ref[...], preferred_element_type=F32)
               + jnp.dot(cv_ref[r, :], woc_ref[...], preferred_element_type=F32))
        h1 = _layer_norm(ALPHA * h + mix, g1_ref[...], b1_ref[...])
        h1_ref[r, :] = h1
        _store_packed(h1r_ref, c * rows * ROW_CHUNKS, rows, _pack_rows(h1[:, :HALF], h1[:, HALF:]))
        lg_ref[:, r] = lax.dot_general(wr_ref[...], h1.astype(BF16), nt, preferred_element_type=F32)


def _mix_call(x2d, attn, conv, gin, bin_, woa, woc, g1, b1, wr):
    t = x2d.shape[0]
    tq = TQ_MIX
    row = lambda i: (i, 0)
    fix = lambda i: (0, 0)
    return pl.pallas_call(
        _mix_kernel,
        grid=(t // tq,),
        in_specs=[pl.BlockSpec(memory_space=pl.ANY), pl.BlockSpec((tq, ATTN_W), row), pl.BlockSpec((tq, CONV_CH), row),
                  pl.BlockSpec((1, D_MODEL), fix), pl.BlockSpec((1, D_MODEL), fix),
                  pl.BlockSpec((ATTN_W, D_MODEL), fix), pl.BlockSpec((CONV_CH, D_MODEL), fix),
                  pl.BlockSpec((1, D_MODEL), fix), pl.BlockSpec((1, D_MODEL), fix),
                  pl.BlockSpec((N_EXPERTS, D_MODEL), fix)],
        out_specs=[pl.BlockSpec((tq, D_MODEL), row), pl.BlockSpec((tq * ROW_CHUNKS, LANES), row),
                   pl.BlockSpec((N_EXPERTS, tq), lambda i: (0, i))],
        out_shape=[SDS((t, D_MODEL), F32), SDS((t * ROW_CHUNKS, LANES), U32), SDS((N_EXPERTS, t), F32)],
        scratch_shapes=[pltpu.VMEM((X_MIX_SLOTS, tq, D_MODEL), F32), pltpu.SemaphoreType.DMA((X_MIX_SLOTS,))],
        compiler_params=_cparams("arbitrary"),
        name="mix_ln1",
    )(x2d, attn, conv, gin, bin_, woa, woc, g1, b1, wr)


def _row_numbers(nrows, n):
    return lax.broadcasted_iota(I32, (nrows, n), 0).astype(F32)


def _first_argmax(x, rows, nrows):
    m = jnp.max(x, axis=0, keepdims=True)
    idx = jnp.min(jnp.where(x == m, rows, float(nrows)), axis=0, keepdims=True)
    return m, idx


def _route_tile(logits, rbias, carry):
    tn = logits.shape[1]
    scores = _sigmoid(logits)
    choice = scores + rbias
    rows_g = _row_numbers(GROUP_SIZE, tn)
    rows_8 = _row_numbers(N_GROUPS, tn)

    gs = []
    for g in range(N_GROUPS):
        xg = choice[g * GROUP_SIZE:(g + 1) * GROUP_SIZE, :]
        m1, i1 = _first_argmax(xg, rows_g, GROUP_SIZE)
        m2 = jnp.max(jnp.where(rows_g == i1, -jnp.inf, xg), axis=0, keepdims=True)
        gs.append(m1 + m2)
    gsc = jnp.concatenate(gs, axis=0)
    gsel = jnp.zeros((N_GROUPS, tn), F32)
    for _ in range(TOPK_GROUPS):
        _, gi = _first_argmax(gsc, rows_8, N_GROUPS)
        hit = rows_8 == gi
        gsel = jnp.where(hit, 1.0, gsel)
        gsc = jnp.where(hit, -jnp.inf, gsc)

    spread = N_GROUPS - TOPK_GROUPS
    g_on = [gsel[g:g + 1, :] > 0.5 for g in range(N_GROUPS)]
    pos, n_before = [], jnp.zeros((1, tn), F32)
    for g in range(N_GROUPS):
        pos.append(n_before)
        n_before = n_before + gsel[g:g + 1, :]
    in_slot = [{g: jnp.logical_and(g_on[g], pos[g] == float(j)) for g in range(j, j + spread + 1)}
               for j in range(TOPK_GROUPS)]

    def group_rows(x, g):
        return x[g * GROUP_SIZE:(g + 1) * GROUP_SIZE, :]

    def compact(x):
        slots = []
        for j in range(TOPK_GROUPS):
            v = group_rows(x, j + spread)
            for g in range(j + spread - 1, j - 1, -1):
                v = jnp.where(in_slot[j][g], group_rows(x, g), v)
            slots.append(v)
        return jnp.concatenate(slots, axis=0)

    n_c = TOPK_GROUPS * GROUP_SIZE
    rows_c = _row_numbers(n_c, tn)
    scores_c = compact(scores)
    masked = jnp.maximum(compact(choice), float(jnp.finfo(F32).min))
    shift = []
    for j in range(TOPK_GROUPS):
        s = jnp.full((1, tn), float((j + spread) * GROUP_SIZE), F32)
        for g in range(j + spread - 1, j - 1, -1):
            s = jnp.where(in_slot[j][g], float(g * GROUP_SIZE), s)
        shift.append(s - float(j * GROUP_SIZE))

    hits, idxs, ws = [], [], []
    for _ in range(TOP_K):
        _, ii = _first_argmax(masked, rows_c, n_c)
        hit = rows_c == ii
        hits.append(hit)
        e = ii + shift[TOPK_GROUPS - 1]
        for j in range(TOPK_GROUPS - 2, -1, -1):
            e = jnp.where(ii < float((j + 1) * GROUP_SIZE), ii + shift[j], e)
        idxs.append(e)
        ws.append(jnp.sum(jnp.where(hit, scores_c, 0.0), axis=0, keepdims=True))
        masked = jnp.where(hit, -jnp.inf, masked)
    wsum = ws[0]
    for w in ws[1:]:
        wsum = wsum + w
    idx = jnp.concatenate(idxs, axis=0).astype(I32)
    wts = jnp.concatenate([w / wsum * ROUTED_SCALE for w in ws], axis=0)

    picked_c = jnp.where(masked == -jnp.inf, 1.0, 0.0)
    sel = []
    for g in range(N_GROUPS):
        v = jnp.zeros((GROUP_SIZE, tn), F32)
        for j in range(max(0, g - spread), min(TOPK_GROUPS - 1, g) + 1):
            v = jnp.where(in_slot[j][g], group_rows(picked_c, j), v)
        sel.append(v)
    sel_b = jnp.concatenate(sel, axis=0).astype(BF16)

    r_i = lax.broadcasted_iota(I32, (tn, tn), 0)
    c_i = lax.broadcasted_iota(I32, (tn, tn), 1)
    upper = jnp.where(r_i < c_i, 1.0, 0.0).astype(BF16)
    before = jnp.dot(sel_b, upper, preferred_element_type=F32)
    before = compact(before + jnp.concatenate([carry] * (tn // LANES), axis=1))
    rank = jnp.concatenate(
        [jnp.sum(jnp.where(h, before, 0.0), axis=0, keepdims=True) for h in hits], axis=0).astype(I32)
    carry = carry + jnp.dot(sel_b, jnp.ones((tn, LANES), BF16), preferred_element_type=F32)
    return idx, wts, rank, carry


def _route_kernel(lg_ref, rb_ref, idx_ref, wts_ref, rank_ref, cnt_ref, carry_ref):
    @pl.when(pl.program_id(0) == 0)
    def _():
        carry_ref[...] = jnp.zeros_like(carry_ref)

    carry = carry_ref[...]
    tn = lg_ref.shape[1] // ROUTE_CHAINS
    for c in range(ROUTE_CHAINS):
        cols = slice(c * tn, (c + 1) * tn)
        idx, wts, rank, carry = _route_tile(lg_ref[:, cols], rb_ref[...], carry)
        idx_ref[:, cols] = idx
        wts_ref[cols, :] = wts.T
        rank_ref[:, cols] = rank
    carry_ref[...] = carry
    cnt_ref[...] = carry.astype(I32)


def _route_call(lg, rbias):
    t = lg.shape[1]
    tn = TN_ROUTE
    col = lambda i: (0, i)
    return pl.pallas_call(
        _route_kernel,
        grid=(t // tn,),
        in_specs=[pl.BlockSpec((N_EXPERTS, tn), col), pl.BlockSpec((N_EXPERTS, 1), lambda i: (0, 0))],
        out_specs=[pl.BlockSpec((TOP_K, tn), col), pl.BlockSpec((tn, TOP_K), lambda i: (i, 0)),
                   pl.BlockSpec((TOP_K, tn), col), pl.BlockSpec((N_EXPERTS, LANES), lambda i: (0, 0))],
        out_shape=[SDS((TOP_K, t), I32), SDS((t, TOP_K), F32), SDS((TOP_K, t), I32), SDS((N_EXPERTS, LANES), I32)],
        scratch_shapes=[pltpu.VMEM((N_EXPERTS, LANES), F32)],
        compiler_params=_cparams("arbitrary"),
        name="route",
    )(lg, rbias)


SC_CORES = 2
SC_SUBCORES = 16
SC_CHUNK = 128
SC_LANES = 16
SC_BUFS = 2


def _sc_worker_chunks(t):
    per_worker = t // (SC_CORES * SC_SUBCORES)
    assert per_worker % SC_CHUNK == 0
    return per_worker


def _sc_dispatch_call(idx, rank, offs, h1rows3, n_rows):
    t = idx.shape[1]
    per_worker = _sc_worker_chunks(t)
    mesh = plsc.VectorSubcoreMesh(core_axis_name="c", subcore_axis_name="s")

    @functools.partial(
        pl.kernel, mesh=mesh, out_type=[SDS((n_rows, ROW_CHUNKS, LANES), U32), SDS((TOP_K, t), I32)],
        scratch_types=[pltpu.VMEM((TOP_K, SC_CHUNK), I32), pltpu.VMEM((TOP_K, SC_CHUNK), I32),
                       pltpu.VMEM((N_EXPERTS,), I32), pltpu.VMEM((SC_CHUNK, ROW_CHUNKS, LANES), U32),
                       pltpu.SemaphoreType.DMA],
        compiler_params=pltpu.CompilerParams(needs_layout_passes=False),
        name="sc_dispatch")
    def body(h_hbm, idx_hbm, rank_hbm, offs_hbm, xs_hbm, dest_hbm, idx_v, rank_v, offs_v, rows_v, sem):
        wid = lax.axis_index("s") * SC_CORES + lax.axis_index("c")
        pltpu.sync_copy(offs_hbm, offs_v)

        @pl.loop(0, per_worker // SC_CHUNK)
        def _(i):
            t0 = wid * per_worker + i * SC_CHUNK
            pltpu.sync_copy(idx_hbm.at[:, pl.ds(t0, SC_CHUNK)], idx_v)
            pltpu.sync_copy(rank_hbm.at[:, pl.ds(t0, SC_CHUNK)], rank_v)
            pltpu.sync_copy(h_hbm.at[pl.ds(t0, SC_CHUNK)], rows_v)
            for kk in range(TOP_K):
                @pl.loop(0, SC_CHUNK // SC_LANES)
                def _(c):
                    lanes = pl.ds(c * SC_LANES, SC_LANES)
                    idx_v[kk, lanes] = plsc.load_gather(offs_v, [idx_v[kk, lanes]]) + rank_v[kk, lanes]
            pltpu.sync_copy(idx_v, dest_hbm.at[:, pl.ds(t0, SC_CHUNK)])
            copies = [pltpu.async_copy(rows_v, xs_hbm.at[idx_v.at[kk]], sem) for kk in range(TOP_K)]
            for c in copies:
                c.wait()

    return body(h1rows3, idx, rank, offs)


def _sc_gather_call(dest, ys3):
    t = dest.shape[1]
    per_worker = _sc_worker_chunks(t)
    mesh = plsc.VectorSubcoreMesh(core_axis_name="c", subcore_axis_name="s")

    half = SC_CHUNK // 2
    items = [(kk, h) for kk in range(TOP_K) for h in range(2)]

    @functools.partial(
        pl.kernel, mesh=mesh, out_type=SDS((TOP_K * t, ROW_CHUNKS, LANES), U32),
        scratch_types=[pltpu.VMEM((TOP_K, SC_CHUNK), I32), pltpu.VMEM((SC_BUFS, half, ROW_CHUNKS, LANES), U32),
                       pltpu.SemaphoreType.DMA((SC_BUFS,)), pltpu.SemaphoreType.DMA((SC_BUFS,))],
        name="sc_gather")
    def body(ys_hbm, dest_hbm, out_hbm, idx_v, rows_v, gsem, wsem):
        wid = lax.axis_index("s") * SC_CORES + lax.axis_index("c")

        @pl.loop(0, per_worker // SC_CHUNK)
        def _(i):
            t0 = wid * per_worker + i * SC_CHUNK
            pltpu.sync_copy(dest_hbm.at[:, pl.ds(t0, SC_CHUNK)], idx_v)

            def gather(j):
                kk, h = items[j]
                b = j % SC_BUFS
                return pltpu.make_async_copy(ys_hbm.at[idx_v.at[kk, pl.ds(h * half, half)]], rows_v.at[b], gsem.at[b])

            def put(j):
                kk, h = items[j]
                b = j % SC_BUFS
                return pltpu.make_async_copy(rows_v.at[b], out_hbm.at[pl.ds(kk * t + t0 + h * half, half)], wsem.at[b])

            ahead = SC_BUFS - 1
            n = len(items)
            for j in range(ahead):
                gather(j).start()
            for j in range(n):
                if j + ahead < n:
                    if j >= 1:
                        put(j - 1).wait()
                    gather(j + ahead).start()
                gather(j).wait()
                put(j).start()
            for j in range(max(n - ahead - 1, 0), n):
                put(j).wait()

    return body(ys3, dest)


def _expert_kernel(ts_ref, te_ref, tr_ref, nv_ref, wg_hbm, wu_hbm, wd_hbm, xs_hbm, ys_hbm,
                   xbuf, ybuf, wg_f, wu_f, wd_f, wg_b, wu_b, wd_b, xsem, ysem, wsem):
    e = pl.program_id(0)
    rows = xbuf.shape[1]
    tm = rows // ROW_CHUNKS
    g0, g1, nv = ts_ref[e], te_ref[e], nv_ref[0]

    def x_copy(g):
        s = g % X_SLOTS
        return pltpu.make_async_copy(xs_hbm.at[pl.ds(pl.multiple_of(g * rows, rows), rows), :], xbuf.at[s], xsem.at[s])

    def y_copy(g):
        s = g % Y_SLOTS
        return pltpu.make_async_copy(ybuf.at[s], ys_hbm.at[pl.ds(pl.multiple_of(g * rows, rows), rows), :], ysem.at[s])

    def w_copies(ex):
        s = ex % W_SLOTS
        return (pltpu.make_async_copy(wg_hbm.at[ex], wg_f.at[s], wsem.at[s]),
                pltpu.make_async_copy(wu_hbm.at[ex], wu_f.at[s], wsem.at[s]),
                pltpu.make_async_copy(wd_hbm.at[ex], wd_f.at[s], wsem.at[s]))

    n_exp = pl.num_programs(0)

    @pl.when(e == 0)
    def _():
        for ex in range(W_AHEAD):
            for c in w_copies(ex):
                c.start()

    @pl.when(e + W_AHEAD < n_exp)
    def _():
        for c in w_copies(e + W_AHEAD):
            c.start()

    for c in w_copies(e):
        c.wait()

    def compute_tile(g):
        x = _load_packed_bf16(xbuf, 0, tm, lead=g % X_SLOTS)
        gate = jnp.dot(x, wg_b[...], preferred_element_type=F32)
        up = jnp.dot(x, wu_b[...], preferred_element_type=F32)
        live = lax.broadcasted_iota(I32, (tm, EXPERT_FF), 0) < tr_ref[g]
        hid = jnp.where(live, gate * _sigmoid(gate) * up, 0.0).astype(BF16)
        y = jnp.dot(hid, wd_b[...], preferred_element_type=F32)
        return _pack_rows(y[:, :HALF], y[:, HALF:])

    def run_tiles(g, n):
        for r in range(n):
            x_copy(g + r).wait()

            @pl.when(g + r + X_AHEAD < nv)
            def _():
                x_copy(g + r + X_AHEAD).start(priority=1)

            @pl.when(g + r >= Y_SLOTS)
            def _():
                y_copy(g + r - Y_SLOTS).wait()

        packed = [compute_tile(g + r) for r in range(n)]
        for r in range(n):
            _store_packed(ybuf, 0, tm, packed[r], lead=(g + r) % Y_SLOTS)
        for r in range(n):
            y_copy(g + r).start(priority=1)

    @pl.when(e == 0)
    def _():
        for g in range(X_AHEAD):
            @pl.when(g < nv)
            def _():
                x_copy(g).start(priority=1)

    @pl.when(g1 > g0)
    def _():
        ws = e % W_SLOTS
        wg_b[...] = wg_f[ws].astype(BF16)
        wu_b[...] = wu_f[ws].astype(BF16)
        wd_b[...] = wd_f[ws].astype(BF16)
        n_tiles = g1 - g0

        def pair(p, c):
            run_tiles(g0 + 2 * p, 2)
            return c

        lax.fori_loop(0, n_tiles // 2, pair, 0)

        @pl.when(n_tiles % 2 == 1)
        def _():
            run_tiles(g1 - 1, 1)

    @pl.when(e == pl.num_programs(0) - 1)
    def _():
        for back in range(1, Y_SLOTS + 1):
            @pl.when(nv >= back)
            def _():
                y_copy(nv - back).wait()


def _expert_call(tile_start, tile_end, tile_rows, n_valid, xs, w_gate, w_up, w_down, n_rows):
    tm = TM_EXP
    hbm = pl.BlockSpec(memory_space=pl.ANY)
    return pl.pallas_call(
        _expert_kernel,
        grid_spec=pltpu.PrefetchScalarGridSpec(
            num_scalar_prefetch=4,
            grid=(N_EXPERTS,),
            in_specs=[hbm, hbm, hbm, hbm],
            out_specs=hbm,
            scratch_shapes=[pltpu.VMEM((X_SLOTS, tm * ROW_CHUNKS, LANES), U32),
                            pltpu.VMEM((Y_SLOTS, tm * ROW_CHUNKS, LANES), U32),
                            pltpu.VMEM((W_SLOTS, D_MODEL, EXPERT_FF), F32), pltpu.VMEM((W_SLOTS, D_MODEL, EXPERT_FF), F32),
                            pltpu.VMEM((W_SLOTS, EXPERT_FF, D_MODEL), F32),
                            pltpu.VMEM((D_MODEL, EXPERT_FF), BF16), pltpu.VMEM((D_MODEL, EXPERT_FF), BF16),
                            pltpu.VMEM((EXPERT_FF, D_MODEL), BF16),
                            pltpu.SemaphoreType.DMA((X_SLOTS,)), pltpu.SemaphoreType.DMA((Y_SLOTS,)),
                            pltpu.SemaphoreType.DMA((W_SLOTS,))],
        ),
        out_shape=SDS((n_rows * ROW_CHUNKS, LANES), U32),
        compiler_params=_cparams("arbitrary"),
        name="experts",
    )(tile_start, tile_end, tile_rows, n_valid, w_gate, w_up, w_down, xs)


COMB_SUB = 32
G_SLOTS = 3
G_AHEAD = 2


def _combine_kernel(wts_ref, h1_ref, g_hbm, wsg_ref, wsu_ref, wsd_ref, g2_ref, b2_ref, o_ref, routed_ref,
                    gbuf, gsem):
    tn = h1_ref.shape[0]
    i = pl.program_id(0)
    n = pl.num_programs(0)

    def g_copy(step):
        s = step % G_SLOTS
        src = g_hbm.at[:, pl.ds(pl.multiple_of(step * tn * ROW_CHUNKS, tn * ROW_CHUNKS), tn * ROW_CHUNKS), :]
        return pltpu.make_async_copy(src, gbuf.at[s], gsem.at[s])

    @pl.when(i == 0)
    def _():
        for step in range(G_AHEAD):
            @pl.when(step < n)
            def _():
                g_copy(step).start()

    @pl.when(i + G_AHEAD < n)
    def _():
        g_copy(i + G_AHEAD).start()

    g_copy(i).wait()
    slot = i % G_SLOTS
    for s0 in range(0, tn, COMB_SUB):
        acc = [jnp.zeros((COMB_SUB, LANES), F32) for _ in range(2 * ROW_CHUNKS)]
        for kk in range(TOP_K):
            wk = jnp.broadcast_to(wts_ref[s0:s0 + COMB_SUB, kk:kk + 1], (COMB_SUB, LANES))
            for cc in range(ROW_CHUNKS):
                lo, hi = _unpack_rows(gbuf[slot, kk, pl.ds(s0 * ROW_CHUNKS + cc, COMB_SUB, stride=ROW_CHUNKS), :])
                acc[cc] = acc[cc] + wk * lo
                acc[ROW_CHUNKS + cc] = acc[ROW_CHUNKS + cc] + wk * hi
        routed_ref[s0:s0 + COMB_SUB, :] = jnp.concatenate(acc, axis=1)

    h1 = h1_ref[...]
    hb = h1.astype(BF16)
    sg = jnp.dot(hb, wsg_ref[...], preferred_element_type=F32)
    su = jnp.dot(hb, wsu_ref[...], preferred_element_type=F32)
    ff = jnp.dot((sg * _sigmoid(sg) * su).astype(BF16), wsd_ref[...], preferred_element_type=F32)
    o_ref[...] = _layer_norm(ALPHA * h1 + ff + routed_ref[...], g2_ref[...], b2_ref[...])


def _combine_call(wts_t, h1, gathered, wsg, wsu, wsd, g2, b2):
    t = h1.shape[0]
    tn = TN_COMB
    row = lambda i: (i, 0)
    fix = lambda i: (0, 0)
    return pl.pallas_call(
        _combine_kernel,
        grid=(t // tn,),
        in_specs=[pl.BlockSpec((tn, TOP_K), row),
                  pl.BlockSpec((tn, D_MODEL), row),
                  pl.BlockSpec(memory_space=pl.ANY),
                  pl.BlockSpec((D_MODEL, SHARED_FF), fix), pl.BlockSpec((D_MODEL, SHARED_FF), fix),
                  pl.BlockSpec((SHARED_FF, D_MODEL), fix),
                  pl.BlockSpec((1, D_MODEL), fix), pl.BlockSpec((1, D_MODEL), fix)],
        out_specs=pl.BlockSpec((tn, D_MODEL), row),
        out_shape=SDS((t, D_MODEL), F32),
        scratch_shapes=[pltpu.VMEM((tn, D_MODEL), F32),
                        pltpu.VMEM((G_SLOTS, TOP_K, tn * ROW_CHUNKS, LANES), U32), pltpu.SemaphoreType.DMA((G_SLOTS,))],
        compiler_params=_cparams("arbitrary"),
        name="combine_ln2",
    )(wts_t, h1, gathered, wsg, wsu, wsd, g2, b2)


def kernel(x, meta_tokens, ln_in_g, ln_in_b, rel_bias, w_in, conv_w, conv_b, conv_ln_g, conv_ln_b, sinks,
           w_out, ln1_g, ln1_b, w_router, router_bias, w_gate, w_up, w_down, ws_gate, ws_up, ws_down,
           ln2_g, ln2_b):
    nbatch, seq, d = x.shape
    t = nbatch * seq
    assert d == D_MODEL and w_in.shape[0] == DEPTH
    assert seq % (ATTN_QBLOCKS * BLOCK) == 0 and seq % T_CONV == 0
    assert all(t % tile == 0 for tile in (TQ_PROJ, TQ_MIX, TN_ROUTE, TN_COMB))
    x2d = x.reshape(t, D_MODEL)
    vec = lambda a: a.reshape(1, -1).astype(F32)
    gin, bin_ = vec(ln_in_g), vec(ln_in_b)
    w_in_b = w_in[0].astype(BF16)

    q, k, v, u = _proj_call(x2d, gin, bin_, w_in_b, TQ_PROJ)
    meta_blk = jnp.concatenate([jnp.zeros((PAD_FRONT, D_MODEL), F32), meta_tokens.astype(F32)], axis=0)
    _, k_meta, v_meta, u_meta = _proj_call(meta_blk, gin, bin_, w_in_b, BLOCK)

    u_halo = jnp.concatenate([jnp.zeros((CONV_HALO - N_META, CONV_CH), F32), u_meta[PAD_FRONT:]], axis=0)
    attn, conv = _attn_conv_call(q, k, v, k_meta, v_meta, _rel_bias_table(rel_bias), sinks[0].astype(F32),
                                 u, u_halo, conv_w[0].astype(F32), vec(conv_b[0]), vec(conv_ln_g[0]),
                                 vec(conv_ln_b[0]), nbatch, seq)

    w_out_b = w_out[0].astype(BF16)
    h1, h1rows, logits = _mix_call(x2d, attn, conv, gin, bin_, w_out_b[:ATTN_W], w_out_b[ATTN_W:],
                                   vec(ln1_g[0]), vec(ln1_b[0]), w_router[0].astype(BF16).T)

    idx, wts_t, rank, cnt = _route_call(logits, router_bias[0].astype(F32).reshape(N_EXPERTS, 1))

    tm = TM_EXP
    n_tiles = (t * TOP_K) // tm + N_EXPERTS
    counts = cnt[:, 0]
    tiles_e = (counts + tm - 1) // tm
    tile_end = jnp.cumsum(tiles_e).astype(I32)
    tile_start = (tile_end - tiles_e).astype(I32)
    offs = tile_start * tm
    tile_id = jnp.arange(n_tiles, dtype=I32)
    lo = jnp.maximum(tile_id[:, None] * tm, offs[None, :])
    hi = jnp.minimum((tile_id[:, None] + 1) * tm, (offs + counts)[None, :])
    tile_rows = jnp.sum(jnp.clip(hi - lo, 0, tm), axis=1).astype(I32)
    n_valid = tile_end[-1:]

    xs, dest = _sc_dispatch_call(idx, rank, offs, h1rows.reshape(t, ROW_CHUNKS, LANES), n_tiles * tm)
    xs = xs.reshape(n_tiles * tm * ROW_CHUNKS, LANES)
    ys = _expert_call(tile_start, tile_end, tile_rows, n_valid, xs, w_gate[0], w_up[0], w_down[0], n_tiles * tm)
    gathered = _sc_gather_call(dest, ys.reshape(n_tiles * tm, ROW_CHUNKS, LANES))
    gathered = gathered.reshape(TOP_K, t * ROW_CHUNKS, LANES)
    out = _combine_call(wts_t, h1, gathered, ws_gate[0].astype(BF16), ws_up[0].astype(BF16),
                        ws_down[0].astype(BF16), vec(ln2_g[0]), vec(ln2_b[0]))
    return out.reshape(nbatch, seq, D_MODEL)
```
